```python
import math
import jax, jax.numpy as jnp
from jax import lax
import numpy as np

D_MODEL = 1024
BATCH = 16
SEQ = 2048
DEPTH = 1

HEAD_DIM = 64
GROUPS = ((128, 1), (512, 4), (2048, 16))
N_GROUPS = len(GROUPS)
HEADS_PER_GROUP = 8
N_HEADS = N_GROUPS * HEADS_PER_GROUP
ATTN_WIDTH = N_HEADS * HEAD_DIM
ATTN_OUT_WIDTH = HEADS_PER_GROUP * HEAD_DIM
Q_BLOCK = 128
CONV_WIDTH = D_MODEL
CONV_KERNEL = 31
N_BRANCHES = 2
D_FF = -(-8 * D_MODEL // (3 * 256)) * 256
IN_WIDTH = 3 * ATTN_WIDTH + 2 * CONV_WIDTH + N_BRANCHES * D_MODEL
RMS_EPS = 1e-6
LN_EPS = 1e-5

kernel_name = "hybrid_dilated_attn_conformer_conv_gated"


def _alibi_slope_list(n):
    def pow2(m):
        start = 2.0 ** (-8.0 / m)
        return [start ** (i + 1) for i in range(m)]
    if math.log2(n).is_integer():
        return pow2(n)
    c = 2 ** math.floor(math.log2(n))
    return pow2(c) + _alibi_slope_list(2 * c)[0::2][: n - c]


def _alibi_slopes():
    s = sorted(_alibi_slope_list(N_HEADS), reverse=True)
    return np.asarray(s, dtype=np.float32).reshape(N_GROUPS, HEADS_PER_GROUP)


def _rmsnorm(x, g):
    x32 = x.astype(jnp.float32)
    y = x32 * lax.rsqrt(jnp.mean(x32 * x32, axis=-1, keepdims=True) + RMS_EPS)
    return (y * g.astype(jnp.float32)).astype(x.dtype)


def _layernorm(x, g, b):
    x32 = x.astype(jnp.float32)
    mu = jnp.mean(x32, axis=-1, keepdims=True)
    var = jnp.mean(jnp.square(x32 - mu), axis=-1, keepdims=True)
    y = (x32 - mu) * lax.rsqrt(var + LN_EPS)
    return (y * g.astype(jnp.float32) + b.astype(jnp.float32)).astype(x.dtype)


def _dilated_group(q, k, v, slopes, window, dilation):
    B, S, Hg, hd = q.shape
    r = dilation
    L = S // r
    n_back = window // r
    assert n_back <= Q_BLOCK
    nb = -(-L // Q_BLOCK)
    Lp = nb * Q_BLOCK

    def to_sub(t):
        return t.reshape(B, L, r, Hg, hd).transpose(0, 2, 3, 1, 4)

    qb = jnp.pad(to_sub(q), ((0, 0), (0, 0), (0, 0), (0, Lp - L), (0, 0)))
    qb = qb.reshape(B, r, Hg, nb, Q_BLOCK, hd)

    def band(t):
        t = jnp.pad(to_sub(t), ((0, 0), (0, 0), (0, 0), (Q_BLOCK, Lp - L), (0, 0)))
        t = t.reshape(B, r, Hg, nb + 1, Q_BLOCK, hd)
        return jnp.concatenate([t[:, :, :, :-1], t[:, :, :, 1:]], axis=4)

    kb, vb = band(k), band(v)
    qi = jnp.arange(Q_BLOCK)[:, None]
    kj = jnp.arange(2 * Q_BLOCK)[None, :]
    rel = Q_BLOCK + qi - kj
    kpos = (jnp.arange(nb)[:, None, None] - 1) * Q_BLOCK + kj[None]
    valid = (rel >= 0) & (rel <= n_back) & (kpos >= 0)
    dist = (rel * r).astype(jnp.float32)

    s = jnp.einsum('brhnqd,brhnkd->brhnqk', qb, kb).astype(jnp.float32) * (hd ** -0.5)
    s = s - slopes.astype(jnp.float32)[:, None, None, None] * dist
    s = jnp.where(valid, s, -jnp.inf)
    m = jnp.max(s, axis=-1, keepdims=True)
    p = jnp.exp(s - m)
    denom = jnp.sum(p, axis=-1)
    o = jnp.einsum('brhnqk,brhnkd->brhnqd', p, vb.astype(jnp.float32)) / denom[..., None]
    lse = m[..., 0] + jnp.log(denom)

    o = o.reshape(B, r, Hg, Lp, hd)[:, :, :, :L].transpose(0, 3, 1, 2, 4).reshape(B, S, Hg, hd)
    lse = lse.reshape(B, r, Hg, Lp)[:, :, :, :L].transpose(0, 3, 1, 2).reshape(B, S, Hg)
    return o, lse


def _causal_depthwise_conv(u, w, b):
    C = u.shape[-1]
    y = lax.conv_general_dilated(
        u, w[:, None, :], window_strides=(1,), padding=[(CONV_KERNEL - 1, 0)],
        dimension_numbers=('NWC', 'WIO', 'NWC'), feature_group_count=C)
    return y + b


def _fwd_setup_inputs(seed: int = 0) -> dict:
    key = jax.random.key(seed)
    ks = jax.random.split(key, 17)
    f32 = jnp.float32

    def w(k, shape, fan_in):
        return jax.random.normal(k, shape, f32) * (fan_in ** -0.5)

    def gain(k, shape):
        return 1.0 + 0.05 * jax.random.normal(k, shape, f32)

    D = DEPTH
    return {
        "x": jax.random.normal(ks[0], (BATCH, SEQ, D_MODEL), f32),
        "norm1_g": gain(ks[1], (D, D_MODEL)),
        "w_in": w(ks[2], (D, D_MODEL, IN_WIDTH), D_MODEL),
        "gate_b": 0.1 * jax.random.normal(ks[3], (D, N_BRANCHES * D_MODEL), f32),
        "conv_w": w(ks[4], (D, CONV_KERNEL, CONV_WIDTH), CONV_KERNEL),
        "conv_b": 0.02 * jax.random.normal(ks[5], (D, CONV_WIDTH), f32),
        "conv_ln_g": gain(ks[6], (D, CONV_WIDTH)),
        "conv_ln_b": 0.02 * jax.random.normal(ks[7], (D, CONV_WIDTH), f32),
        "w_conv_out": w(ks[8], (D, CONV_WIDTH, D_MODEL), CONV_WIDTH),
        "w_attn_out": w(ks[9], (D, ATTN_OUT_WIDTH, D_MODEL), ATTN_OUT_WIDTH),
        "w_o": w(ks[10], (D, D_MODEL, D_MODEL), D_MODEL),
        "norm2_g": gain(ks[11], (D, D_MODEL)),
        "w_ffn_gate": w(ks[12], (D, D_MODEL, D_FF), D_MODEL),
        "w_ffn_up": w(ks[13], (D, D_MODEL, D_FF), D_MODEL),
        "w_ffn_down": w(ks[14], (D, D_FF, D_MODEL), D_FF),
        "norm_f_g": gain(ks[15], (D_MODEL,)),
    }


def _fwd_reference(x, norm1_g, w_in, gate_b, conv_w, conv_b, conv_ln_g, conv_ln_b,
              w_conv_out, w_attn_out, w_o, norm2_g, w_ffn_gate, w_ffn_up,
              w_ffn_down, norm_f_g):
    B, S, _ = x.shape
    slopes = jnp.asarray(_alibi_slopes())
    splits = [ATTN_WIDTH, 2 * ATTN_WIDTH, 3 * ATTN_WIDTH, 3 * ATTN_WIDTH + 2 * CONV_WIDTH]
    for l in range(DEPTH):
        h = _rmsnorm(x, norm1_g[l])
        proj = h @ w_in[l]
        q, k, v, u, g_logits = jnp.split(proj, splits, axis=-1)
        q = q.reshape(B, S, N_GROUPS, HEADS_PER_GROUP, HEAD_DIM)
        k = k.reshape(B, S, N_GROUPS, HEADS_PER_GROUP, HEAD_DIM)
        v = v.reshape(B, S, N_GROUPS, HEADS_PER_GROUP, HEAD_DIM)

        outs, lses = [], []
        for g, (window, dilation) in enumerate(GROUPS):
            o, lse = _dilated_group(q[:, :, g], k[:, :, g], v[:, :, g], slopes[g], window, dilation)
            outs.append(o)
            lses.append(lse)
        alpha = jax.nn.softmax(jnp.stack(lses, axis=0), axis=0)
        y_attn = jnp.sum(alpha[..., None] * jnp.stack(outs, axis=0), axis=0)
        y_attn = y_attn.reshape(B, S, ATTN_OUT_WIDTH).astype(x.dtype) @ w_attn_out[l]

        ua, ub = jnp.split(u, 2, axis=-1)
        c = ua * jax.nn.sigmoid(ub)
        c = _causal_depthwise_conv(c, conv_w[l], conv_b[l])
        c = jax.nn.silu(_layernorm(c, conv_ln_g[l], conv_ln_b[l]))
        y_conv = c @ w_conv_out[l]

        gates = jax.nn.sigmoid(g_logits + gate_b[l])
        g_attn, g_conv = jnp.split(gates, 2, axis=-1)
        x = x + (g_attn * y_attn + g_conv * y_conv) @ w_o[l]

        h2 = _rmsnorm(x, norm2_g[l])
        x = x + (jax.nn.silu(h2 @ w_ffn_gate[l]) * (h2 @ w_ffn_up[l])) @ w_ffn_down[l]
    return _rmsnorm(x, norm_f_g)


import jax as _jax
import jax.numpy as _jnp

TWIN_FORMAT = 'train_step'
FWD_PARAMS = ['x', 'norm1_g', 'w_in', 'gate_b', 'conv_w', 'conv_b', 'conv_ln_g', 'conv_ln_b', 'w_conv_out', 'w_attn_out', 'w_o', 'norm2_g', 'w_ffn_gate', 'w_ffn_up', 'w_ffn_down', 'norm_f_g']
TWIN_WEIGHTS = ['norm1_g', 'w_in', 'gate_b', 'conv_w', 'conv_b', 'conv_ln_g', 'conv_ln_b', 'w_conv_out', 'w_attn_out', 'w_o', 'norm2_g', 'w_ffn_gate', 'w_ffn_up', 'w_ffn_down', 'norm_f_g']
TWIN_DIFF_INPUT = 'x'
TWIN_INPUTS = ['x', 'norm1_g', 'w_in', 'gate_b', 'conv_w', 'conv_b', 'conv_ln_g', 'conv_ln_b', 'w_conv_out', 'w_attn_out', 'w_o', 'norm2_g', 'w_ffn_gate', 'w_ffn_up', 'w_ffn_down', 'norm_f_g', 'loss_target', 'm_norm1_g', 'm_w_in', 'm_gate_b', 'm_conv_w', 'm_conv_b', 'm_conv_ln_g', 'm_conv_ln_b', 'm_w_conv_out', 'm_w_attn_out', 'm_w_o', 'm_norm2_g', 'm_w_ffn_gate', 'm_w_ffn_up', 'm_w_ffn_down', 'm_norm_f_g', 'v_norm1_g', 'v_w_in', 'v_gate_b', 'v_conv_w', 'v_conv_b', 'v_conv_ln_g', 'v_conv_ln_b', 'v_w_conv_out', 'v_w_attn_out', 'v_w_o', 'v_norm2_g', 'v_w_ffn_gate', 'v_w_ffn_up', 'v_w_ffn_down', 'v_norm_f_g']
TWIN_OUTPUTS = ['loss', 'grad_x', 'grad_norm1_g', 'grad_w_in', 'grad_gate_b', 'grad_conv_w', 'grad_conv_b', 'grad_conv_ln_g', 'grad_conv_ln_b', 'grad_w_conv_out', 'grad_w_attn_out', 'grad_w_o', 'grad_norm2_g', 'grad_w_ffn_gate', 'grad_w_ffn_up', 'grad_w_ffn_down', 'grad_norm_f_g', 'delta_norm1_g', 'delta_w_in', 'delta_gate_b', 'delta_conv_w', 'delta_conv_b', 'delta_conv_ln_g', 'delta_conv_ln_b', 'delta_w_conv_out', 'delta_w_attn_out', 'delta_w_o', 'delta_norm2_g', 'delta_w_ffn_gate', 'delta_w_ffn_up', 'delta_w_ffn_down', 'delta_norm_f_g', 'new_m_norm1_g', 'new_m_w_in', 'new_m_gate_b', 'new_m_conv_w', 'new_m_conv_b', 'new_m_conv_ln_g', 'new_m_conv_ln_b', 'new_m_w_conv_out', 'new_m_w_attn_out', 'new_m_w_o', 'new_m_norm2_g', 'new_m_w_ffn_gate', 'new_m_w_ffn_up', 'new_m_w_ffn_down', 'new_m_norm_f_g', 'new_v_norm1_g', 'new_v_w_in', 'new_v_gate_b', 'new_v_conv_w', 'new_v_conv_b', 'new_v_conv_ln_g', 'new_v_conv_ln_b', 'new_v_w_conv_out', 'new_v_w_attn_out', 'new_v_w_o', 'new_v_norm2_g', 'new_v_w_ffn_gate', 'new_v_w_ffn_up', 'new_v_w_ffn_down', 'new_v_norm_f_g']
TWIN_LEAF_KINDS = {'loss': 'loss', 'grad_x': 'grad_x', 'grad_norm1_g': 'grad_w', 'grad_w_in': 'grad_w', 'grad_gate_b': 'grad_w', 'grad_conv_w': 'grad_w', 'grad_conv_b': 'grad_w', 'grad_conv_ln_g': 'grad_w', 'grad_conv_ln_b': 'grad_w', 'grad_w_conv_out': 'grad_w', 'grad_w_attn_out': 'grad_w', 'grad_w_o': 'grad_w', 'grad_norm2_g': 'grad_w', 'grad_w_ffn_gate': 'grad_w', 'grad_w_ffn_up': 'grad_w', 'grad_w_ffn_down': 'grad_w', 'grad_norm_f_g': 'grad_w', 'delta_norm1_g': 'delta_w', 'delta_w_in': 'delta_w', 'delta_gate_b': 'delta_w', 'delta_conv_w': 'delta_w', 'delta_conv_b': 'delta_w', 'delta_conv_ln_g': 'delta_w', 'delta_conv_ln_b': 'delta_w', 'delta_w_conv_out': 'delta_w', 'delta_w_attn_out': 'delta_w', 'delta_w_o': 'delta_w', 'delta_norm2_g': 'delta_w', 'delta_w_ffn_gate': 'delta_w', 'delta_w_ffn_up': 'delta_w', 'delta_w_ffn_down': 'delta_w', 'delta_norm_f_g': 'delta_w', 'new_m_norm1_g': 'new_m', 'new_m_w_in': 'new_m', 'new_m_gate_b': 'new_m', 'new_m_conv_w': 'new_m', 'new_m_conv_b': 'new_m', 'new_m_conv_ln_g': 'new_m', 'new_m_conv_ln_b': 'new_m', 'new_m_w_conv_out': 'new_m', 'new_m_w_attn_out': 'new_m', 'new_m_w_o': 'new_m', 'new_m_norm2_g': 'new_m', 'new_m_w_ffn_gate': 'new_m', 'new_m_w_ffn_up': 'new_m', 'new_m_w_ffn_down': 'new_m', 'new_m_norm_f_g': 'new_m', 'new_v_norm1_g': 'new_v', 'new_v_w_in': 'new_v', 'new_v_gate_b': 'new_v', 'new_v_conv_w': 'new_v', 'new_v_conv_b': 'new_v', 'new_v_conv_ln_g': 'new_v', 'new_v_conv_ln_b': 'new_v', 'new_v_w_conv_out': 'new_v', 'new_v_w_attn_out': 'new_v', 'new_v_w_o': 'new_v', 'new_v_norm2_g': 'new_v', 'new_v_w_ffn_gate': 'new_v', 'new_v_w_ffn_up': 'new_v', 'new_v_w_ffn_down': 'new_v', 'new_v_norm_f_g': 'new_v'}


def _forward(args):
    return _fwd_reference(*[args[k] for k in FWD_PARAMS])


def _output_shape():
    out = _jax.eval_shape(lambda: _forward(_fwd_setup_inputs(0)))
    return out.shape, out.dtype

N_MICROBATCH = 1
ADAM_LR = 0.001
ADAM_B1 = 0.9
ADAM_B2 = 0.999
ADAM_EPS = 1e-08
ADAM_WD = 0.01
ADAM_STEP = 10
PER_EXAMPLE_BATCH_AXIS = {'x': 0, 'loss_target': 0}
SHARED_INPUTS = []
_WEIGHT_DTYPES = {'norm1_g': _jnp.float32, 'w_in': _jnp.float32, 'gate_b': _jnp.float32, 'conv_w': _jnp.float32, 'conv_b': _jnp.float32, 'conv_ln_g': _jnp.float32, 'conv_ln_b': _jnp.float32, 'w_conv_out': _jnp.float32, 'w_attn_out': _jnp.float32, 'w_o': _jnp.float32, 'norm2_g': _jnp.float32, 'w_ffn_gate': _jnp.float32, 'w_ffn_up': _jnp.float32, 'w_ffn_down': _jnp.float32, 'norm_f_g': _jnp.float32}
MOMENT_SCALE = {'norm1_g': 9.812451e-02, 'w_in': 3.204795e-02, 'gate_b': 1.971308e-02, 'conv_w': 6.283108e-02, 'conv_b': 1.329806e-01, 'conv_ln_g': 8.139494e-02, 'conv_ln_b': 7.534295e-02, 'w_conv_out': 6.200308e-02, 'w_attn_out': 3.872811e-02, 'w_o': 7.218959e-02, 'norm2_g': 1.300206e-01, 'w_ffn_gate': 5.439753e-02, 'w_ffn_up': 5.279643e-02, 'w_ffn_down': 8.772363e-02, 'norm_f_g': 3.200943e+01}


def _to_microbatches(a, axis):
    t = _jnp.moveaxis(a, axis, 0)
    t = t.reshape((N_MICROBATCH, t.shape[0] // N_MICROBATCH) + t.shape[1:])
    return _jnp.moveaxis(t, 1, axis + 1)


def setup_inputs(seed: int = 0) -> dict:
    inp = _fwd_setup_inputs(seed)
    key = _jax.random.fold_in(_jax.random.key(seed), 7919)
    shape, _ = _output_shape()
    out = dict(inp)
    out["loss_target"] = _jax.random.normal(_jax.random.fold_in(key, 0), shape, _jnp.float32)
    for i, name in enumerate(TWIN_WEIGHTS):
        w = inp[name].astype(_jnp.float32)
        if MOMENT_SCALE is None:
            s = _jnp.sqrt(_jnp.mean(_jnp.square(w)) + 1e-30)
        else:
            s = MOMENT_SCALE[name]
        km, kv = _jax.random.split(_jax.random.fold_in(key, i + 1))
        out[name] = w
        out["m_" + name] = s * _jax.random.normal(km, w.shape, _jnp.float32)
        out["v_" + name] = (s * s) * _jax.random.uniform(kv, w.shape, _jnp.float32, 0.5, 1.5)
    if N_MICROBATCH > 1:
        for name, axis in PER_EXAMPLE_BATCH_AXIS.items():
            out[name] = _to_microbatches(out[name], axis)
    return {'x': out['x'], 'norm1_g': out['norm1_g'], 'w_in': out['w_in'], 'gate_b': out['gate_b'], 'conv_w': out['conv_w'], 'conv_b': out['conv_b'], 'conv_ln_g': out['conv_ln_g'], 'conv_ln_b': out['conv_ln_b'], 'w_conv_out': out['w_conv_out'], 'w_attn_out': out['w_attn_out'], 'w_o': out['w_o'], 'norm2_g': out['norm2_g'], 'w_ffn_gate': out['w_ffn_gate'], 'w_ffn_up': out['w_ffn_up'], 'w_ffn_down': out['w_ffn_down'], 'norm_f_g': out['norm_f_g'], 'loss_target': out['loss_target'], 'm_norm1_g': out['m_norm1_g'], 'm_w_in': out['m_w_in'], 'm_gate_b': out['m_gate_b'], 'm_conv_w': out['m_conv_w'], 'm_conv_b': out['m_conv_b'], 'm_conv_ln_g': out['m_conv_ln_g'], 'm_conv_ln_b': out['m_conv_ln_b'], 'm_w_conv_out': out['m_w_conv_out'], 'm_w_attn_out': out['m_w_attn_out'], 'm_w_o': out['m_w_o'], 'm_norm2_g': out['m_norm2_g'], 'm_w_ffn_gate': out['m_w_ffn_gate'], 'm_w_ffn_up': out['m_w_ffn_up'], 'm_w_ffn_down': out['m_w_ffn_down'], 'm_norm_f_g': out['m_norm_f_g'], 'v_norm1_g': out['v_norm1_g'], 'v_w_in': out['v_w_in'], 'v_gate_b': out['v_gate_b'], 'v_conv_w': out['v_conv_w'], 'v_conv_b': out['v_conv_b'], 'v_conv_ln_g': out['v_conv_ln_g'], 'v_conv_ln_b': out['v_conv_ln_b'], 'v_w_conv_out': out['v_w_conv_out'], 'v_w_attn_out': out['v_w_attn_out'], 'v_w_o': out['v_w_o'], 'v_norm2_g': out['v_norm2_g'], 'v_w_ffn_gate': out['v_w_ffn_gate'], 'v_w_ffn_up': out['v_w_ffn_up'], 'v_w_ffn_down': out['v_w_ffn_down'], 'v_norm_f_g': out['v_norm_f_g']}


def _loss(weights, diff, rest, loss_target):
    with _jax.named_scope("forward"):
        args = {**rest, TWIN_DIFF_INPUT: diff, **{k: w.astype(_WEIGHT_DTYPES[k]) for k, w in weights.items()}}
        y = _forward(args)
    with _jax.named_scope("loss_head"):
        err = _jnp.square(y.astype(_jnp.float32) - loss_target)
        return 0.5 * _jnp.sum(_jnp.mean(err, axis=-1)) if err.ndim else 0.5 * err


def _adamw(w, g, m, v):
    m = ADAM_B1 * m + (1.0 - ADAM_B1) * g
    v = ADAM_B2 * v + (1.0 - ADAM_B2) * _jnp.square(g)
    m_hat = m / (1.0 - ADAM_B1 ** ADAM_STEP)
    v_hat = v / (1.0 - ADAM_B2 ** ADAM_STEP)
    delta = -ADAM_LR * (m_hat / (_jnp.sqrt(v_hat) + ADAM_EPS) + ADAM_WD * w)
    return delta, m, v


def reference(x, norm1_g, w_in, gate_b, conv_w, conv_b, conv_ln_g, conv_ln_b, w_conv_out, w_attn_out, w_o, norm2_g, w_ffn_gate, w_ffn_up, w_ffn_down, norm_f_g, loss_target, m_norm1_g, m_w_in, m_gate_b, m_conv_w, m_conv_b, m_conv_ln_g, m_conv_ln_b, m_w_conv_out, m_w_attn_out, m_w_o, m_norm2_g, m_w_ffn_gate, m_w_ffn_up, m_w_ffn_down, m_norm_f_g, v_norm1_g, v_w_in, v_gate_b, v_conv_w, v_conv_b, v_conv_ln_g, v_conv_ln_b, v_w_conv_out, v_w_attn_out, v_w_o, v_norm2_g, v_w_ffn_gate, v_w_ffn_up, v_w_ffn_down, v_norm_f_g):
    given = dict(x=x, norm1_g=norm1_g, w_in=w_in, gate_b=gate_b, conv_w=conv_w, conv_b=conv_b, conv_ln_g=conv_ln_g, conv_ln_b=conv_ln_b, w_conv_out=w_conv_out, w_attn_out=w_attn_out, w_o=w_o, norm2_g=norm2_g, w_ffn_gate=w_ffn_gate, w_ffn_up=w_ffn_up, w_ffn_down=w_ffn_down, norm_f_g=norm_f_g, loss_target=loss_target, m_norm1_g=m_norm1_g, m_w_in=m_w_in, m_gate_b=m_gate_b, m_conv_w=m_conv_w, m_conv_b=m_conv_b, m_conv_ln_g=m_conv_ln_g, m_conv_ln_b=m_conv_ln_b, m_w_conv_out=m_w_conv_out, m_w_attn_out=m_w_attn_out, m_w_o=m_w_o, m_norm2_g=m_norm2_g, m_w_ffn_gate=m_w_ffn_gate, m_w_ffn_up=m_w_ffn_up, m_w_ffn_down=m_w_ffn_down, m_norm_f_g=m_norm_f_g, v_norm1_g=v_norm1_g, v_w_in=v_w_in, v_gate_b=v_gate_b, v_conv_w=v_conv_w, v_conv_b=v_conv_b, v_conv_ln_g=v_conv_ln_g, v_conv_ln_b=v_conv_ln_b, v_w_conv_out=v_w_conv_out, v_w_attn_out=v_w_attn_out, v_w_o=v_w_o, v_norm2_g=v_norm2_g, v_w_ffn_gate=v_w_ffn_gate, v_w_ffn_up=v_w_ffn_up, v_w_ffn_down=v_w_ffn_down, v_norm_f_g=v_norm_f_g)
    weights = {n: given[n] for n in TWIN_WEIGHTS}
    shared = {n: given[n] for n in SHARED_INPUTS}
    per_example = {n: given[n] for n in ['x']}
    grad_fn = _jax.value_and_grad(_loss, argnums=(0, 1))

    def one_microbatch(ex, loss_target):
        ex = dict(ex)
        diff = ex.pop(TWIN_DIFF_INPUT)
        return grad_fn(weights, diff, {**shared, **ex}, loss_target)

    if N_MICROBATCH == 1:
        loss, (grad_w, grad_x) = one_microbatch(per_example, given["loss_target"])
    else:
        def body(carry, xs):
            loss_sum, grad_sum = carry
            l_k, (gw_k, gx_k) = one_microbatch(xs[0], xs[1])
            with _jax.named_scope("update"):
                return (loss_sum + l_k, _jax.tree.map(_jnp.add, grad_sum, gw_k)), gx_k

        init = (_jnp.zeros((), _jnp.float32), _jax.tree.map(_jnp.zeros_like, weights))
        (loss, grad_w), grad_x = _jax.lax.scan(body, init, (per_example, given["loss_target"]))
    with _jax.named_scope("update"):
        delta_w, new_m, new_v = {}, {}, {}
        for n in TWIN_WEIGHTS:
            delta_w[n], new_m[n], new_v[n] = _adamw(weights[n], grad_w[n], given["m_" + n], given["v_" + n])
    return (loss, grad_x, *[grad_w[n] for n in TWIN_WEIGHTS], *[delta_w[n] for n in TWIN_WEIGHTS],
            *[new_m[n] for n in TWIN_WEIGHTS], *[new_v[n] for n in TWIN_WEIGHTS])
```

```python
import functools
import math

import numpy as np
import jax
import jax.numpy as jnp
from jax import lax
from jax.experimental import pallas as pl
from jax.experimental.pallas import tpu as pltpu

F32 = jnp.float32
BF16 = jnp.bfloat16
SDS = jax.ShapeDtypeStruct
MESH = pl.DeviceIdType.MESH

D_MODEL = 1024
SEQ = 2048
HEAD_DIM = 64
GROUPS = ((128, 1), (512, 4), (2048, 16))
HEADS_PER_GROUP = 8
N_HEADS = 24
ATTN_WIDTH = N_HEADS * HEAD_DIM
ATTN_OUT = HEADS_PER_GROUP * HEAD_DIM
CONV_K = 31
CONV_PAD = 32
D_FF = 2816
IN_WIDTH = 3 * ATTN_WIDTH + 2 * D_MODEL + 2 * D_MODEL
RMS_EPS = 1e-6
LN_EPS = 1e-5
Q_BLOCK = 128
NEG = -1e30
N_DEV = 8

ADAM_LR = 0.001
ADAM_B1 = 0.9
ADAM_B2 = 0.999
ADAM_EPS = 1e-08
ADAM_WD = 0.01
ADAM_STEP = 10


def _alibi_slope_list(n):
    def pow2(m):
        start = 2.0 ** (-8.0 / m)
        return [start ** (i + 1) for i in range(m)]
    if math.log2(n).is_integer():
        return pow2(n)
    c = 2 ** math.floor(math.log2(n))
    return pow2(c) + _alibi_slope_list(2 * c)[0::2][: n - c]


def _slopes_times_dilation():
    s = np.asarray(sorted(_alibi_slope_list(N_HEADS), reverse=True), dtype=np.float32).reshape(3, HEADS_PER_GROUP)
    r = np.asarray([g[1] for g in GROUPS], dtype=np.float32)[:, None]
    return (s * r).reshape(N_HEADS)


def _sigmoid(x):
    return 1.0 / (1.0 + jnp.exp(-x))


def _dot(a, b):
    return jnp.dot(a, b, preferred_element_type=F32)


def _dot_nt(a, b):
    return lax.dot_general(a, b, (((1,), (1,)), ((), ())), preferred_element_type=F32)


def _dot_tn(a, b):
    return lax.dot_general(a, b, (((0,), (0,)), ((), ())), preferred_element_type=F32)


def _rowsum(x):
    return jnp.sum(x, axis=0, keepdims=True)


def _params(*sem):
    return pltpu.CompilerParams(dimension_semantics=sem)


def _norm_qkv(x, g1, w_in):
    T = x.shape[0]
    tm, tn = 1024, 512
    nj = 3 * ATTN_WIDTH // tn

    def body(x_ref, g_ref, w_ref, h_ref, o_ref, h_scr):
        @pl.when(pl.program_id(1) == 0)
        def _():
            xv = x_ref[...]
            r = lax.rsqrt(jnp.mean(xv * xv, axis=-1, keepdims=True) + RMS_EPS)
            h = (xv * r * g_ref[...]).astype(BF16)
            h_scr[...] = h
            h_ref[...] = h
        o_ref[...] = _dot(h_scr[...], w_ref[...]).astype(o_ref.dtype)

    return pl.pallas_call(
        body, grid=(T // tm, nj),
        in_specs=[pl.BlockSpec((tm, D_MODEL), lambda i, j: (i, 0)),
                  pl.BlockSpec((1, D_MODEL), lambda i, j: (0, 0)),
                  pl.BlockSpec((D_MODEL, tn), lambda i, j: (0, j))],
        out_specs=[pl.BlockSpec((tm, D_MODEL), lambda i, j: (i, 0)),
                   pl.BlockSpec((tm, tn), lambda i, j: (i, j))],
        out_shape=[SDS((T, D_MODEL), BF16), SDS((T, 3 * ATTN_WIDTH), BF16)],
        scratch_shapes=[pltpu.VMEM((tm, D_MODEL), BF16)],
        compiler_params=_params("parallel", "arbitrary"), name="norm_qkv")(x, g1, w_in)


def _mm_nn(a, w, col_block0, n_out, out_dtype, name, tm=1024, tn=512):
    M, K = a.shape

    def body(a_ref, w_ref, o_ref):
        o_ref[...] = _dot(a_ref[...], w_ref[...]).astype(o_ref.dtype)

    return pl.pallas_call(
        body, grid=(M // tm, n_out // tn),
        in_specs=[pl.BlockSpec((tm, K), lambda i, j: (i, 0)),
                  pl.BlockSpec((K, tn), lambda i, j: (0, j + col_block0))],
        out_specs=pl.BlockSpec((tm, tn), lambda i, j: (i, j)),
        out_shape=SDS((M, n_out), out_dtype),
        compiler_params=_params("parallel", "parallel"), name=name)(a, w)


def _mm_tn(a, b, out_dtype, name, tn, tt=512):
    T, K = a.shape
    N = b.shape[1]
    nt = T // tt

    def body(a_ref, b_ref, o_ref, acc):
        t = pl.program_id(1)
        p = _dot_tn(a_ref[...], b_ref[...])

        @pl.when(t == 0)
        def _():
            acc[...] = p

        @pl.when(t > 0)
        def _():
            acc[...] += p

        @pl.when(t == nt - 1)
        def _():
            o_ref[...] = acc[...].astype(o_ref.dtype)

    return pl.pallas_call(
        body, grid=(N // tn, nt),
        in_specs=[pl.BlockSpec((tt, K), lambda j, t: (t, 0)),
                  pl.BlockSpec((tt, tn), lambda j, t: (t, j))],
        out_specs=pl.BlockSpec((K, tn), lambda j, t: (0, j)),
        out_shape=SDS((K, N), out_dtype),
        scratch_shapes=[pltpu.VMEM((K, tn), F32)],
        compiler_params=_params("parallel", "arbitrary"), name=name)(a, b)


def _to_dilated(t, batch):
    t = t.reshape(batch, SEQ, 3, HEADS_PER_GROUP, HEAD_DIM)
    parts = []
    for g, (_, r) in enumerate(GROUPS):
        p = t[:, :, g].reshape(batch, SEQ // r, r, HEADS_PER_GROUP, HEAD_DIM)
        parts.append(p.transpose(0, 3, 2, 1, 4).reshape(batch, HEADS_PER_GROUP, SEQ, HEAD_DIM))
    return jnp.concatenate(parts, axis=1).reshape(batch * N_HEADS, SEQ, HEAD_DIM)


def _from_dilated(t, batch):
    t = t.reshape(batch, 3, HEADS_PER_GROUP, SEQ, HEAD_DIM)
    parts = []
    for g, (_, r) in enumerate(GROUPS):
        p = t[:, g].reshape(batch, HEADS_PER_GROUP, r, SEQ // r, HEAD_DIM)
        parts.append(p.transpose(0, 3, 2, 1, 4).reshape(batch * SEQ, ATTN_OUT))
    return parts


def _attn_masks(slope_r):
    qi = lax.broadcasted_iota(jnp.int32, (Q_BLOCK, Q_BLOCK), 0)
    kj = lax.broadcasted_iota(jnp.int32, (Q_BLOCK, Q_BLOCK), 1)
    rel = (qi - kj).astype(F32)
    bias_cur = jnp.where(qi >= kj, -slope_r * rel, NEG)
    bias_prev = jnp.where(qi <= kj, -slope_r * (rel + float(Q_BLOCK)), NEG)
    return bias_cur, bias_prev


def _blocks_per_class(head):
    g = head // HEADS_PER_GROUP
    return jnp.where(g == 0, SEQ // Q_BLOCK, jnp.where(g == 1, SEQ // 4 // Q_BLOCK, SEQ // 16 // Q_BLOCK))


def _attn_fwd(q, k, v, slopes_r):
    BH = q.shape[0]
    nblk = SEQ // Q_BLOCK

    def body(sl_ref, q_ref, k_ref, v_ref, o_ref, l_ref):
        head = pl.program_id(0) % N_HEADS
        nbc = _blocks_per_class(head)
        bias_cur, bias_prev = _attn_masks(sl_ref[head])

        def blk(n, carry):
            off = pl.multiple_of(n * Q_BLOCK, Q_BLOCK)
            offp = pl.multiple_of(jnp.maximum(n - 1, 0) * Q_BLOCK, Q_BLOCK)
            qb = q_ref[0, pl.ds(off, Q_BLOCK), :]
            kc = k_ref[0, pl.ds(off, Q_BLOCK), :]
            kp = k_ref[0, pl.ds(offp, Q_BLOCK), :]
            vc = v_ref[0, pl.ds(off, Q_BLOCK), :]
            vp = v_ref[0, pl.ds(offp, Q_BLOCK), :]
            first = (n % nbc) == 0
            sc = _dot_nt(qb, kc) * 0.125 + bias_cur
            sp = jnp.where(first, NEG, _dot_nt(qb, kp) * 0.125 + bias_prev)
            m = jnp.maximum(jnp.max(sc, axis=-1, keepdims=True), jnp.max(sp, axis=-1, keepdims=True))
            pc = jnp.exp(sc - m)
            pp = jnp.exp(sp - m)
            l = jnp.sum(pc, axis=-1, keepdims=True) + jnp.sum(pp, axis=-1, keepdims=True)
            o = (_dot(pc.astype(BF16), vc) + _dot(pp.astype(BF16), vp)) / l
            o_ref[0, pl.ds(off, Q_BLOCK), :] = o
            l_ref[0, pl.ds(off, Q_BLOCK), :] = jnp.broadcast_to(m + jnp.log(l), (Q_BLOCK, HEAD_DIM))
            return carry

        lax.fori_loop(0, nblk, blk, 0)

    spec = pl.BlockSpec((1, SEQ, HEAD_DIM), lambda i: (i, 0, 0))
    return pl.pallas_call(
        body, grid=(BH,),
        in_specs=[pl.BlockSpec(memory_space=pltpu.SMEM), spec, spec, spec],
        out_specs=[spec, spec],
        out_shape=[SDS((BH, SEQ, HEAD_DIM), F32), SDS((BH, SEQ, HEAD_DIM), F32)],
        compiler_params=_params("parallel"), name="attn_fwd")(slopes_r, q, k, v)


def _attn_bwd(q, k, v, do, lse, dsum, slopes_r):
    BH = q.shape[0]
    nblk = SEQ // Q_BLOCK

    def body(sl_ref, q_ref, k_ref, v_ref, do_ref, l_ref, d_ref, dq_ref, dk_ref, dv_ref, dk_acc, dv_acc):
        head = pl.program_id(0) % N_HEADS
        nbc = _blocks_per_class(head)
        bias_cur, bias_prev = _attn_masks(sl_ref[head])
        dk_acc[...] = jnp.zeros_like(dk_acc)
        dv_acc[...] = jnp.zeros_like(dv_acc)

        def blk(n, carry):
            off = pl.multiple_of(n * Q_BLOCK, Q_BLOCK)
            offp = pl.multiple_of(jnp.maximum(n - 1, 0) * Q_BLOCK, Q_BLOCK)
            qb = q_ref[0, pl.ds(off, Q_BLOCK), :]
            kc = k_ref[0, pl.ds(off, Q_BLOCK), :]
            kp = k_ref[0, pl.ds(offp, Q_BLOCK), :]
            vc = v_ref[0, pl.ds(off, Q_BLOCK), :]
            vp = v_ref[0, pl.ds(offp, Q_BLOCK), :]
            dob = do_ref[0, pl.ds(off, Q_BLOCK), :]
            lse_b = l_ref[0, pl.ds(off, Q_BLOCK), 0:1]
            dsum_b = d_ref[0, pl.ds(off, Q_BLOCK), 0:1]
            first = (n % nbc) == 0
            sc = _dot_nt(qb, kc) * 0.125 + bias_cur
            sp = jnp.where(first, NEG, _dot_nt(qb, kp) * 0.125 + bias_prev)
            pc = jnp.exp(sc - lse_b)
            pp = jnp.exp(sp - lse_b)
            dsc = (pc * (_dot_nt(dob, vc) - dsum_b)).astype(BF16)
            dsp = (pp * (_dot_nt(dob, vp) - dsum_b)).astype(BF16)
            dq_ref[0, pl.ds(off, Q_BLOCK), :] = ((_dot(dsc, kc) + _dot(dsp, kp)) * 0.125).astype(dq_ref.dtype)
            dk_acc[pl.ds(off, Q_BLOCK), :] += _dot_tn(dsc, qb) * 0.125
            dk_acc[pl.ds(offp, Q_BLOCK), :] += _dot_tn(dsp, qb) * 0.125
            dv_acc[pl.ds(off, Q_BLOCK), :] += _dot_tn(pc.astype(BF16), dob)
            dv_acc[pl.ds(offp, Q_BLOCK), :] += _dot_tn(pp.astype(BF16), dob)
            return carry

        lax.fori_loop(0, nblk, blk, 0)
        dk_ref[0] = dk_acc[...].astype(dk_ref.dtype)
        dv_ref[0] = dv_acc[...].astype(dv_ref.dtype)

    spec = pl.BlockSpec((1, SEQ, HEAD_DIM), lambda i: (i, 0, 0))
    out = SDS((BH, SEQ, HEAD_DIM), BF16)
    return pl.pallas_call(
        body, grid=(BH,),
        in_specs=[pl.BlockSpec(memory_space=pltpu.SMEM)] + [spec] * 6,
        out_specs=[spec, spec, spec],
        out_shape=[out, out, out],
        scratch_shapes=[pltpu.VMEM((SEQ, HEAD_DIM), F32), pltpu.VMEM((SEQ, HEAD_DIM), F32)],
        compiler_params=_params("parallel"), name="attn_bwd")(slopes_r, q, k, v, do, lse, dsum)


CONV_TC = 256
CONV_ROWS = 64


def _conv_fwd(u, conv_w, conv_b, batch):
    nct = D_MODEL // CONV_TC

    def body(ua_ref, ub_ref, w_ref, b_ref, o_ref, pad):
        pad[0:CONV_PAD, :] = jnp.zeros((CONV_PAD, CONV_TC), F32)
        pad[CONV_PAD:, :] = ua_ref[0] * _sigmoid(ub_ref[0])

        def chunk(c, carry):
            base = pl.multiple_of(c * CONV_ROWS, CONV_ROWS)
            win = pad[pl.ds(base, CONV_ROWS + CONV_PAD), :]
            acc = jnp.broadcast_to(b_ref[...], (CONV_ROWS, CONV_TC))
            for t in range(CONV_K):
                s = t + CONV_PAD - (CONV_K - 1)
                acc = acc + win[s:s + CONV_ROWS, :] * w_ref[t:t + 1, :]
            o_ref[0, pl.ds(base, CONV_ROWS), :] = acc
            return carry

        lax.fori_loop(0, SEQ // CONV_ROWS, chunk, 0)

    return pl.pallas_call(
        body, grid=(nct, batch),
        in_specs=[pl.BlockSpec((1, SEQ, CONV_TC), lambda j, b: (b, 0, j)),
                  pl.BlockSpec((1, SEQ, CONV_TC), lambda j, b: (b, 0, j + nct)),
                  pl.BlockSpec((CONV_PAD, CONV_TC), lambda j, b: (0, j)),
                  pl.BlockSpec((1, CONV_TC), lambda j, b: (0, j))],
        out_specs=pl.BlockSpec((1, SEQ, CONV_TC), lambda j, b: (b, 0, j)),
        out_shape=SDS((batch, SEQ, D_MODEL), F32),
        scratch_shapes=[pltpu.VMEM((SEQ + CONV_PAD, CONV_TC), F32)],
        compiler_params=_params("parallel", "parallel"), name="conv_fwd")(u, u, conv_w, conv_b)


def _conv_bwd(u, dc1, conv_w, batch):
    nct = D_MODEL // CONV_TC
    nchunk = SEQ // CONV_ROWS

    def body(ua_ref, ub_ref, d_ref, w_ref, dua_ref, dub_ref, gw_ref, gb_ref, padc, padd, gacc):
        b = pl.program_id(1)
        sig = _sigmoid(ub_ref[0])
        padc[0:CONV_PAD, :] = jnp.zeros((CONV_PAD, CONV_TC), F32)
        padc[CONV_PAD:, :] = ua_ref[0] * sig
        padd[0:SEQ, :] = d_ref[0]
        padd[SEQ:, :] = jnp.zeros((CONV_PAD, CONV_TC), F32)

        @pl.when(b == 0)
        def _():
            gacc[...] = jnp.zeros_like(gacc)
            gb_ref[...] = jnp.zeros_like(gb_ref)

        gb_ref[...] += _rowsum(d_ref[0])

        def chunk(c, carry):
            base = pl.multiple_of(c * CONV_ROWS, CONV_ROWS)
            wind = padd[pl.ds(base, CONV_ROWS + CONV_PAD), :]
            winc = padc[pl.ds(base, CONV_ROWS + CONV_PAD), :]
            dcur = wind[0:CONV_ROWS, :]
            acc = jnp.zeros((CONV_ROWS, CONV_TC), F32)
            for t in range(CONV_K):
                s = CONV_K - 1 - t
                acc = acc + wind[s:s + CONV_ROWS, :] * w_ref[t:t + 1, :]
                sc = t + CONV_PAD - (CONV_K - 1)
                prod = winc[sc:sc + CONV_ROWS, :] * dcur
                gacc[t] += jnp.sum(prod.reshape(CONV_ROWS // 8, 8, CONV_TC), axis=0)
            ua = ua_ref[0, pl.ds(base, CONV_ROWS), :]
            sg = _sigmoid(ub_ref[0, pl.ds(base, CONV_ROWS), :])
            dua_ref[0, pl.ds(base, CONV_ROWS), :] = (acc * sg).astype(dua_ref.dtype)
            dub_ref[0, pl.ds(base, CONV_ROWS), :] = (acc * ua * sg * (1.0 - sg)).astype(dub_ref.dtype)
            return carry

        lax.fori_loop(0, nchunk, chunk, 0)

        @pl.when(b == batch - 1)
        def _():
            for t in range(CONV_K):
                gw_ref[t:t + 1, :] = jnp.sum(gacc[t], axis=0, keepdims=True)
            gw_ref[CONV_K:CONV_PAD, :] = jnp.zeros((CONV_PAD - CONV_K, CONV_TC), F32)

    du = SDS((batch, SEQ, D_MODEL), BF16)
    return pl.pallas_call(
        body, grid=(nct, batch),
        in_specs=[pl.BlockSpec((1, SEQ, CONV_TC), lambda j, b: (b, 0, j)),
                  pl.BlockSpec((1, SEQ, CONV_TC), lambda j, b: (b, 0, j + nct)),
                  pl.BlockSpec((1, SEQ, CONV_TC), lambda j, b: (b, 0, j)),
                  pl.BlockSpec((CONV_PAD, CONV_TC), lambda j, b: (0, j))],
        out_specs=[pl.BlockSpec((1, SEQ, CONV_TC), lambda j, b: (b, 0, j)),
                   pl.BlockSpec((1, SEQ, CONV_TC), lambda j, b: (b, 0, j)),
                   pl.BlockSpec((CONV_PAD, CONV_TC), lambda j, b: (0, j)),
                   pl.BlockSpec((1, CONV_TC), lambda j, b: (0, j))],
        out_shape=[du, du, SDS((CONV_PAD, D_MODEL), F32), SDS((1, D_MODEL), F32)],
        scratch_shapes=[pltpu.VMEM((SEQ + CONV_PAD, CONV_TC), F32), pltpu.VMEM((SEQ + CONV_PAD, CONV_TC), F32),
                        pltpu.VMEM((CONV_K, 8, CONV_TC), F32)],
        compiler_params=_params("parallel", "arbitrary"), name="conv_bwd")(u, u, dc1, conv_w)


MID_TM = 256


def _layernorm_stats(c1):
    mu = jnp.mean(c1, axis=-1, keepdims=True)
    cen = c1 - mu
    rs = lax.rsqrt(jnp.mean(cen * cen, axis=-1, keepdims=True) + LN_EPS)
    return cen * rs, rs


def _mid_fwd(o3, l3, c1, logits, x, w_a, w_c, w_o, gate_b, ln_g, ln_b, g2):
    T = x.shape[0]
    tm = MID_TM

    def body(o_ref, l_ref, c1_ref, lg_ref, x_ref, wa_ref, wc_ref, wo_ref, gb_ref, lng_ref, lnb_ref, g2_ref,
             att_ref, lse_ref, c3_ref, ya_ref, yc_ref, mix_ref, x1_ref, h2_ref):
        l0, l1, l2 = l_ref[0], l_ref[1], l_ref[2]
        m = jnp.maximum(jnp.maximum(l0, l1), l2)
        e0, e1, e2 = jnp.exp(l0 - m), jnp.exp(l1 - m), jnp.exp(l2 - m)
        den = e0 + e1 + e2
        att = ((e0 * o_ref[0] + e1 * o_ref[1] + e2 * o_ref[2]) / den).astype(BF16)
        att_ref[...] = att
        lse_ref[...] = m + jnp.log(den)
        ya = _dot(att, wa_ref[...])
        xh, _ = _layernorm_stats(c1_ref[...])
        c2 = xh * lng_ref[...] + lnb_ref[...]
        c3 = (c2 * _sigmoid(c2)).astype(BF16)
        c3_ref[...] = c3
        yc = _dot(c3, wc_ref[...])
        gates = _sigmoid(lg_ref[...] + gb_ref[...])
        mix = (gates[:, :D_MODEL] * ya + gates[:, D_MODEL:] * yc).astype(BF16)
        ya_ref[...] = ya.astype(BF16)
        yc_ref[...] = yc.astype(BF16)
        mix_ref[...] = mix
        x1 = x_ref[...] + _dot(mix, wo_ref[...])
        x1_ref[...] = x1
        r = lax.rsqrt(jnp.mean(x1 * x1, axis=-1, keepdims=True) + RMS_EPS)
        h2_ref[...] = (x1 * r * g2_ref[...]).astype(BF16)

    row = lambda n: pl.BlockSpec((tm, n), lambda i: (i, 0))
    full = lambda a, b: pl.BlockSpec((a, b), lambda i: (0, 0))
    grp = pl.BlockSpec((3, tm, ATTN_OUT), lambda i: (0, i, 0))
    return pl.pallas_call(
        body, grid=(T // tm,),
        in_specs=[grp, grp, row(D_MODEL), row(2 * D_MODEL), row(D_MODEL),
                  full(ATTN_OUT, D_MODEL), full(D_MODEL, D_MODEL), full(D_MODEL, D_MODEL),
                  full(1, 2 * D_MODEL), full(1, D_MODEL), full(1, D_MODEL), full(1, D_MODEL)],
        out_specs=[row(ATTN_OUT), row(ATTN_OUT), row(D_MODEL), row(D_MODEL), row(D_MODEL), row(D_MODEL),
                   row(D_MODEL), row(D_MODEL)],
        out_shape=[SDS((T, ATTN_OUT), BF16), SDS((T, ATTN_OUT), F32), SDS((T, D_MODEL), BF16), SDS((T, D_MODEL), BF16),
                   SDS((T, D_MODEL), BF16), SDS((T, D_MODEL), BF16), SDS((T, D_MODEL), F32), SDS((T, D_MODEL), BF16)],
        compiler_params=_params("parallel"), name="mid_fwd")(o3, l3, c1, logits, x, w_a, w_c, w_o, gate_b, ln_g, ln_b, g2)


def _mid_bwd(dx1b, ya, yc, logits, att, c1, w_a, w_c, w_o, gate_b, ln_g, ln_b, head_ones):
    T = dx1b.shape[0]
    tm = MID_TM

    def body(dx_ref, ya_ref, yc_ref, lg_ref, att_ref, c1_ref, wa_ref, wc_ref, wo_ref, gb_ref, lng_ref, lnb_ref, e_ref,
             dlg_ref, dya_ref, dyc_ref, datt_ref, dsum_ref, dc1_ref, ggb_ref, glg_ref, glb_ref):
        @pl.when(pl.program_id(0) == 0)
        def _():
            ggb_ref[...] = jnp.zeros_like(ggb_ref)
            glg_ref[...] = jnp.zeros_like(glg_ref)
            glb_ref[...] = jnp.zeros_like(glb_ref)

        dmix = _dot_nt(dx_ref[...], wo_ref[...])
        gates = _sigmoid(lg_ref[...] + gb_ref[...])
        ga, gc = gates[:, :D_MODEL], gates[:, D_MODEL:]
        dla = dmix * ya_ref[...].astype(F32) * ga * (1.0 - ga)
        dlc = dmix * yc_ref[...].astype(F32) * gc * (1.0 - gc)
        dlg_ref[:, :D_MODEL] = dla.astype(BF16)
        dlg_ref[:, D_MODEL:] = dlc.astype(BF16)
        ggb_ref[:, :D_MODEL] += _rowsum(dla)
        ggb_ref[:, D_MODEL:] += _rowsum(dlc)
        dya = (dmix * ga).astype(BF16)
        dyc = (dmix * gc).astype(BF16)
        dya_ref[...] = dya
        dyc_ref[...] = dyc
        datt = _dot_nt(dya, wa_ref[...])
        datt_ref[...] = datt.astype(BF16)
        dsum_ref[...] = jnp.dot(datt * att_ref[...].astype(F32), e_ref[...], preferred_element_type=F32,
                                precision=lax.Precision.HIGHEST)
        dc3 = _dot_nt(dyc, wc_ref[...])
        xh, rs = _layernorm_stats(c1_ref[...])
        c2 = xh * lng_ref[...] + lnb_ref[...]
        sg = _sigmoid(c2)
        dc2 = dc3 * (sg * (1.0 + c2 * (1.0 - sg)))
        glg_ref[...] += _rowsum(dc2 * xh)
        glb_ref[...] += _rowsum(dc2)
        dxh = dc2 * lng_ref[...]
        dc1_ref[...] = rs * (dxh - jnp.mean(dxh, axis=-1, keepdims=True) - xh * jnp.mean(dxh * xh, axis=-1, keepdims=True))

    row = lambda n: pl.BlockSpec((tm, n), lambda i: (i, 0))
    full = lambda a, b: pl.BlockSpec((a, b), lambda i: (0, 0))
    return pl.pallas_call(
        body, grid=(T // tm,),
        in_specs=[row(D_MODEL), row(D_MODEL), row(D_MODEL), row(2 * D_MODEL), row(ATTN_OUT), row(D_MODEL),
                  full(ATTN_OUT, D_MODEL), full(D_MODEL, D_MODEL), full(D_MODEL, D_MODEL),
                  full(1, 2 * D_MODEL), full(1, D_MODEL), full(1, D_MODEL), full(ATTN_OUT, ATTN_OUT)],
        out_specs=[row(2 * D_MODEL), row(D_MODEL), row(D_MODEL), row(ATTN_OUT), row(ATTN_OUT), row(D_MODEL),
                   full(1, 2 * D_MODEL), full(1, D_MODEL), full(1, D_MODEL)],
        out_shape=[SDS((T, 2 * D_MODEL), BF16), SDS((T, D_MODEL), BF16), SDS((T, D_MODEL), BF16), SDS((T, ATTN_OUT), BF16),
                   SDS((T, ATTN_OUT), F32), SDS((T, D_MODEL), F32),
                   SDS((1, 2 * D_MODEL), F32), SDS((1, D_MODEL), F32), SDS((1, D_MODEL), F32)],
        compiler_params=_params("arbitrary"), name="mid_bwd")(dx1b, ya, yc, logits, att, c1, w_a, w_c, w_o, gate_b, ln_g, ln_b,
                                                               head_ones)


FFN_TM = 512
FFN_TF = D_FF // 2


def _rms_bwd(dy_times_g, xh, r):
    return r * (dy_times_g - xh * jnp.mean(dy_times_g * xh, axis=-1, keepdims=True))


def _ffn_fwd(h2, x1, target, gf, w_g, w_u, w_d):
    T = h2.shape[0]
    tm, tf = FFN_TM, FFN_TF
    nf = D_FF // tf

    def body(h_ref, x1_ref, t_ref, gf_ref, wg_ref, wu_ref, wd_ref,
             a_ref, b_ref, f_ref, dx2_ref, dx2b_ref, loss_ref, gnf_ref, acc):
        i, j = pl.program_id(0), pl.program_id(1)
        h = h_ref[...]
        a = _dot(h, wg_ref[...])
        b = _dot(h, wu_ref[...])
        f = (a * _sigmoid(a) * b).astype(BF16)
        a_ref[...] = a.astype(BF16)
        b_ref[...] = b.astype(BF16)
        f_ref[...] = f
        p = _dot(f, wd_ref[...])

        @pl.when(j == 0)
        def _():
            acc[...] = x1_ref[...] + p

        @pl.when(j > 0)
        def _():
            acc[...] += p

        @pl.when((i == 0) & (j == nf - 1))
        def _():
            loss_ref[...] = jnp.zeros_like(loss_ref)
            gnf_ref[...] = jnp.zeros_like(gnf_ref)

        @pl.when(j == nf - 1)
        def _():
            x2 = acc[...]
            r = lax.rsqrt(jnp.mean(x2 * x2, axis=-1, keepdims=True) + RMS_EPS)
            xh = x2 * r
            err = xh * gf_ref[...] - t_ref[...]
            loss_ref[...] += (0.5 / D_MODEL) * jnp.sum(err * err)
            dy = err * (1.0 / D_MODEL)
            gnf_ref[...] += _rowsum(dy * xh)
            dx2 = _rms_bwd(dy * gf_ref[...], xh, r)
            dx2_ref[...] = dx2
            dx2b_ref[...] = dx2.astype(BF16)

    row = lambda n: pl.BlockSpec((tm, n), lambda i, j: (i, 0))
    ffb = pl.BlockSpec((tm, tf), lambda i, j: (i, j))
    return pl.pallas_call(
        body, grid=(T // tm, nf),
        in_specs=[row(D_MODEL), row(D_MODEL), row(D_MODEL), pl.BlockSpec((1, D_MODEL), lambda i, j: (0, 0)),
                  pl.BlockSpec((D_MODEL, tf), lambda i, j: (0, j)), pl.BlockSpec((D_MODEL, tf), lambda i, j: (0, j)),
                  pl.BlockSpec((tf, D_MODEL), lambda i, j: (j, 0))],
        out_specs=[ffb, ffb, ffb, row(D_MODEL), row(D_MODEL),
                   pl.BlockSpec((1, 128), lambda i, j: (0, 0)), pl.BlockSpec((1, D_MODEL), lambda i, j: (0, 0))],
        out_shape=[SDS((T, D_FF), BF16), SDS((T, D_FF), BF16), SDS((T, D_FF), BF16), SDS((T, D_MODEL), F32),
                   SDS((T, D_MODEL), BF16), SDS((1, 128), F32), SDS((1, D_MODEL), F32)],
        scratch_shapes=[pltpu.VMEM((tm, D_MODEL), F32)],
        compiler_params=_params("arbitrary", "arbitrary"), name="ffn_fwd")(h2, x1, target, gf, w_g, w_u, w_d)


def _ffn_bwd(dx2b, dx2, a, b, x1, g2, w_g, w_u, w_d):
    T = dx2.shape[0]
    tm, tf = FFN_TM, FFN_TF
    nf = D_FF // tf

    def body(dxb_ref, dx2_ref, a_ref, b_ref, x1_ref, g2_ref, wg_ref, wu_ref, wd_ref,
             da_ref, db_ref, dx1_ref, dx1b_ref, gn2_ref, acc):
        i, j = pl.program_id(0), pl.program_id(1)
        df = _dot_nt(dxb_ref[...], wd_ref[...])
        av = a_ref[...].astype(F32)
        bv = b_ref[...].astype(F32)
        sg = _sigmoid(av)
        db = (df * av * sg).astype(BF16)
        da = (df * bv * (sg * (1.0 + av * (1.0 - sg)))).astype(BF16)
        da_ref[...] = da
        db_ref[...] = db
        p = _dot_nt(da, wg_ref[...]) + _dot_nt(db, wu_ref[...])

        @pl.when(j == 0)
        def _():
            acc[...] = p

        @pl.when(j > 0)
        def _():
            acc[...] += p

        @pl.when((i == 0) & (j == nf - 1))
        def _():
            gn2_ref[...] = jnp.zeros_like(gn2_ref)

        @pl.when(j == nf - 1)
        def _():
            dh2 = acc[...]
            x1 = x1_ref[...]
            r = lax.rsqrt(jnp.mean(x1 * x1, axis=-1, keepdims=True) + RMS_EPS)
            xh = x1 * r
            gn2_ref[...] += _rowsum(dh2 * xh)
            dx1 = dx2_ref[...] + _rms_bwd(dh2 * g2_ref[...], xh, r)
            dx1_ref[...] = dx1
            dx1b_ref[...] = dx1.astype(BF16)

    row = lambda n: pl.BlockSpec((tm, n), lambda i, j: (i, 0))
    ffb = pl.BlockSpec((tm, tf), lambda i, j: (i, j))
    return pl.pallas_call(
        body, grid=(T // tm, nf),
        in_specs=[row(D_MODEL), row(D_MODEL), ffb, ffb, row(D_MODEL), pl.BlockSpec((1, D_MODEL), lambda i, j: (0, 0)),
                  pl.BlockSpec((D_MODEL, tf), lambda i, j: (0, j)), pl.BlockSpec((D_MODEL, tf), lambda i, j: (0, j)),
                  pl.BlockSpec((tf, D_MODEL), lambda i, j: (j, 0))],
        out_specs=[ffb, ffb, row(D_MODEL), row(D_MODEL), pl.BlockSpec((1, D_MODEL), lambda i, j: (0, 0))],
        out_shape=[SDS((T, D_FF), BF16), SDS((T, D_FF), BF16), SDS((T, D_MODEL), F32), SDS((T, D_MODEL), BF16),
                   SDS((1, D_MODEL), F32)],
        scratch_shapes=[pltpu.VMEM((tm, D_MODEL), F32)],
        compiler_params=_params("arbitrary", "arbitrary"), name="ffn_bwd")(dx2b, dx2, a, b, x1, g2, w_g, w_u, w_d)


def _in_bwd(dproj, w_in, x, dx1, g1):
    T = x.shape[0]
    tm, tk = 512, 512
    nk = IN_WIDTH // tk

    def body(dp_ref, w_ref, x_ref, dx1_ref, g_ref, dx_ref, gn1_ref, acc):
        i, j = pl.program_id(0), pl.program_id(1)
        p = _dot_nt(dp_ref[...], w_ref[...])

        @pl.when(j == 0)
        def _():
            acc[...] = p

        @pl.when(j > 0)
        def _():
            acc[...] += p

        @pl.when((i == 0) & (j == nk - 1))
        def _():
            gn1_ref[...] = jnp.zeros_like(gn1_ref)

        @pl.when(j == nk - 1)
        def _():
            dh = acc[...]
            xv = x_ref[...]
            r = lax.rsqrt(jnp.mean(xv * xv, axis=-1, keepdims=True) + RMS_EPS)
            xh = xv * r
            gn1_ref[...] += _rowsum(dh * xh)
            dx_ref[...] = dx1_ref[...] + _rms_bwd(dh * g_ref[...], xh, r)

    row = lambda n: pl.BlockSpec((tm, n), lambda i, j: (i, 0))
    return pl.pallas_call(
        body, grid=(T // tm, nk),
        in_specs=[pl.BlockSpec((tm, tk), lambda i, j: (i, j)), pl.BlockSpec((D_MODEL, tk), lambda i, j: (0, j)),
                  row(D_MODEL), row(D_MODEL), pl.BlockSpec((1, D_MODEL), lambda i, j: (0, 0))],
        out_specs=[row(D_MODEL), pl.BlockSpec((1, D_MODEL), lambda i, j: (0, 0))],
        out_shape=[SDS((T, D_MODEL), F32), SDS((1, D_MODEL), F32)],
        scratch_shapes=[pltpu.VMEM((tm, D_MODEL), F32)],
        compiler_params=_params("arbitrary", "arbitrary"), name="in_bwd")(dproj, w_in, x, dx1, g1)


def _local_step(x, target, w, small):
    T = x.shape[0]
    batch = T // SEQ
    slopes_r = jnp.asarray(_slopes_times_dilation())

    h, qkv = _norm_qkv(x, small["norm1_g"], w["w_in"])
    u = _mm_nn(h, w["w_in"], 3 * ATTN_WIDTH // 512, 2 * D_MODEL, F32, "proj_u")
    logits = _mm_nn(h, w["w_in"], (3 * ATTN_WIDTH + 2 * D_MODEL) // 512, 2 * D_MODEL, F32, "proj_gate")

    qd = _to_dilated(qkv[:, :ATTN_WIDTH], batch)
    kd = _to_dilated(qkv[:, ATTN_WIDTH:2 * ATTN_WIDTH], batch)
    vd = _to_dilated(qkv[:, 2 * ATTN_WIDTH:], batch)
    od, ld = _attn_fwd(qd, kd, vd, slopes_r)
    o3 = jnp.stack(_from_dilated(od, batch))
    l3 = jnp.stack(_from_dilated(ld, batch))

    u3 = u.reshape(batch, SEQ, 2 * D_MODEL)
    c1 = _conv_fwd(u3, w["conv_w"], small["conv_b"], batch).reshape(T, D_MODEL)

    att, lse, c3, ya, yc, mix, x1, h2 = _mid_fwd(
        o3, l3, c1, logits, x, w["w_attn_out"], w["w_conv_out"], w["w_o"],
        small["gate_b"], small["conv_ln_g"], small["conv_ln_b"], small["norm2_g"])

    a, b, f, dx2, dx2b, loss, g_normf = _ffn_fwd(h2, x1, target, small["norm_f_g"],
                                                   w["w_ffn_gate"], w["w_ffn_up"], w["w_ffn_down"])

    da, db, dx1, dx1b, g_norm2 = _ffn_bwd(dx2b, dx2, a, b, x1, small["norm2_g"],
                                           w["w_ffn_gate"], w["w_ffn_up"], w["w_ffn_down"])
    gw = {}
    gw["w_ffn_down"] = _mm_tn(f, dx2b, BF16, "gw_ffn_down", tn=512)
    gw["w_ffn_gate"] = _mm_tn(h2, da, BF16, "gw_ffn_gate", tn=1408)
    gw["w_ffn_up"] = _mm_tn(h2, db, BF16, "gw_ffn_up", tn=1408)

    head_ones = jnp.asarray(np.kron(np.eye(HEADS_PER_GROUP, dtype=np.float32), np.ones((HEAD_DIM, HEAD_DIM), np.float32)))
    dlogits, dya, dyc, datt, dsum, dc1, g_gate_b, g_ln_g, g_ln_b = _mid_bwd(
        dx1b, ya, yc, logits, att, c1, w["w_attn_out"], w["w_conv_out"], w["w_o"],
        small["gate_b"], small["conv_ln_g"], small["conv_ln_b"], head_ones)
    gw["w_o"] = _mm_tn(mix, dx1b, BF16, "gw_o", tn=512)
    gw["w_attn_out"] = _mm_tn(att, dya, BF16, "gw_attn_out", tn=512)
    gw["w_conv_out"] = _mm_tn(c3, dyc, BF16, "gw_conv_out", tn=512)

    dua, dub, g_conv_w, g_conv_b = _conv_bwd(u3, dc1.reshape(batch, SEQ, D_MODEL), w["conv_w"], batch)

    datt3 = jnp.concatenate([datt, datt, datt], axis=1)
    lse3 = jnp.concatenate([lse, lse, lse], axis=1)
    dsum3 = jnp.concatenate([dsum, dsum, dsum], axis=1)
    dqd, dkd, dvd = _attn_bwd(qd, kd, vd, _to_dilated(datt3, batch), _to_dilated(lse3, batch), _to_dilated(dsum3, batch),
                              slopes_r)
    dproj = jnp.concatenate(
        _from_dilated(dqd, batch) + _from_dilated(dkd, batch) + _from_dilated(dvd, batch)
        + [dua.reshape(T, D_MODEL), dub.reshape(T, D_MODEL), dlogits], axis=1)

    grad_x, g_norm1 = _in_bwd(dproj, w["w_in"], x, dx1, small["norm1_g"])
    gw["w_in"] = _mm_tn(h, dproj, BF16, "gw_in", tn=2176)
    gw["conv_w"] = g_conv_w

    gsmall = {"norm1_g": g_norm1, "gate_b": g_gate_b, "conv_b": g_conv_b, "conv_ln_g": g_ln_g, "conv_ln_b": g_ln_b,
              "norm2_g": g_norm2, "norm_f_g": g_normf}
    return loss, grad_x, gw, gsmall


ANY = pl.BlockSpec(memory_space=pl.ANY)


def _all_gather(arrs):
    n = len(arrs)

    def body(*refs):
        ins, outs = refs[:n], refs[n:2 * n]
        send_sems, recv_sems, local_sems = refs[2 * n:]
        x, y, c = lax.axis_index("x"), lax.axis_index("y"), lax.axis_index("c")
        me, sibling = (x, y, c), (x, y, 1 - c)
        chips = [(1 - x, y), (x, 1 - y), (1 - x, 1 - y)]

        def copy(a, k, block, to, src=None):
            px, py, pc = block
            dst = outs[a].at[4 * px + 2 * py + pc]
            return pltpu.make_async_remote_copy(
                src_ref=dst if src is None else src, dst_ref=dst,
                send_sem=send_sems.at[a, k], recv_sem=recv_sems.at[a, k], device_id=to, device_id_type=MESH)

        mine = [pltpu.make_async_copy(ins[a], outs[a].at[4 * x + 2 * y + c], local_sems.at[a]) for a in range(n)]
        for cp in mine:
            cp.start()
        first = []
        for j, chip in enumerate(chips):
            first += [copy(a, 1 + j, me, (*chip, c), src=ins[a]) for a in range(n)]
        first += [copy(a, 0, me, sibling, src=ins[a]) for a in range(n)]
        for cp in first:
            cp.start()
        passed = []
        for j, chip in enumerate(chips):
            for a in range(n):
                copy(a, 1 + j, (*chip, c), me).wait_recv()
                cp = copy(a, 4 + j, (*chip, c), sibling)
                cp.start()
                passed.append(cp)
        for a in range(n):
            copy(a, 0, sibling, me).wait_recv()
        for j, chip in enumerate(chips):
            for a in range(n):
                copy(a, 4 + j, (*chip, 1 - c), me).wait_recv()
        for cp in first + passed:
            cp.wait_send()
        for cp in mine:
            cp.wait()

    return pl.pallas_call(
        body, in_specs=[ANY] * n, out_specs=[ANY] * n,
        out_shape=[SDS((N_DEV,) + a.shape, a.dtype) for a in arrs],
        scratch_shapes=[pltpu.SemaphoreType.DMA((n, 7)), pltpu.SemaphoreType.DMA((n, 7)), pltpu.SemaphoreType.DMA((n,))],
        name="all_gather_weights")(*arrs)


def _exchange_sibling(gs):
    n = len(gs)

    def body(*refs):
        ins, outs = refs[:n], refs[n:2 * n]
        send_sems, recv_sems = refs[2 * n:]
        x, y, c = lax.axis_index("x"), lax.axis_index("y"), lax.axis_index("c")
        copies = []
        for a in range(n):
            for j in range(4):
                copies.append(pltpu.make_async_remote_copy(
                    src_ref=ins[a].at[2 * j + (1 - c)], dst_ref=outs[a].at[j],
                    send_sem=send_sems.at[a, j], recv_sem=recv_sems.at[a, j],
                    device_id=(x, y, 1 - c), device_id_type=MESH))
        for cp in copies:
            cp.start()
        for cp in copies:
            cp.wait_recv()
        for cp in copies:
            cp.wait_send()

    return pl.pallas_call(
        body, in_specs=[ANY] * n, out_specs=[ANY] * n,
        out_shape=[SDS((4,) + g.shape[1:], g.dtype) for g in gs],
        scratch_shapes=[pltpu.SemaphoreType.DMA((n, 4)), pltpu.SemaphoreType.DMA((n, 4))],
        name="reduce_scatter_sibling")(*gs)


def _row_tile(rows, cols, itemsize_total):
    budget = (4 << 20) // max(1, cols * itemsize_total)
    if rows <= budget:
        return rows
    t = rows
    while t > budget and t % 2 == 0 and (t // 2) % 16 == 0:
        t //= 2
    return t


def _add_pair(g, r1, core, name):
    _, rows, cols = g.shape
    tr = _row_tile(rows, cols, 3 * g.dtype.itemsize)

    def body(c_ref, g_ref, r_ref, o_ref):
        o_ref[...] = (g_ref[...].astype(F32) + r_ref[...].astype(F32)).astype(o_ref.dtype)

    return pl.pallas_call(
        body,
        grid_spec=pltpu.PrefetchScalarGridSpec(
            num_scalar_prefetch=1, grid=(4, rows // tr),
            in_specs=[pl.BlockSpec((1, tr, cols), lambda j, i, c_ref: (2 * j + c_ref[0], i, 0)),
                      pl.BlockSpec((1, tr, cols), lambda j, i, c_ref: (j, i, 0))],
            out_specs=pl.BlockSpec((1, tr, cols), lambda j, i, c_ref: (j, i, 0))),
        out_shape=SDS((4, rows, cols), g.dtype),
        compiler_params=_params("parallel", "parallel"), name=name)(core, g, r1)


def _exchange_chips(ps):
    n = len(ps)

    def body(*refs):
        ins, outs = refs[:n], refs[n:2 * n]
        send_sems, recv_sems, local_sems = refs[2 * n:]
        x, y, c = lax.axis_index("x"), lax.axis_index("y"), lax.axis_index("c")
        my_chip = 2 * x + y
        mine = [pltpu.make_async_copy(ins[a].at[my_chip], outs[a].at[my_chip], local_sems.at[a]) for a in range(n)]
        for cp in mine:
            cp.start()
        copies = []
        for k, (px, py) in enumerate([(1 - x, y), (x, 1 - y), (1 - x, 1 - y)]):
            for a in range(n):
                copies.append(pltpu.make_async_remote_copy(
                    src_ref=ins[a].at[2 * px + py], dst_ref=outs[a].at[my_chip],
                    send_sem=send_sems.at[a, k], recv_sem=recv_sems.at[a, k],
                    device_id=(px, py, c), device_id_type=MESH))
        for cp in copies:
            cp.start()
        for k, (px, py) in enumerate([(1 - x, y), (x, 1 - y), (1 - x, 1 - y)]):
            for a in range(n):
                pltpu.make_async_remote_copy(
                    src_ref=ins[a].at[my_chip], dst_ref=outs[a].at[2 * px + py],
                    send_sem=send_sems.at[a, k], recv_sem=recv_sems.at[a, k],
                    device_id=(px, py, c), device_id_type=MESH).wait_recv()
        for cp in copies:
            cp.wait_send()
        for cp in mine:
            cp.wait()

    return pl.pallas_call(
        body, in_specs=[ANY] * n, out_specs=[ANY] * n,
        out_shape=[SDS(p.shape, p.dtype) for p in ps],
        scratch_shapes=[pltpu.SemaphoreType.DMA((n, 3)), pltpu.SemaphoreType.DMA((n, 3)), pltpu.SemaphoreType.DMA((n,))],
        name="reduce_scatter_chips")(*ps)


def _adam_math(g, w, m, v):
    m_new = ADAM_B1 * m + (1.0 - ADAM_B1) * g
    v_new = ADAM_B2 * v + (1.0 - ADAM_B2) * (g * g)
    m_hat = m_new / (1.0 - ADAM_B1 ** ADAM_STEP)
    v_hat = v_new / (1.0 - ADAM_B2 ** ADAM_STEP)
    delta = -ADAM_LR * (m_hat / (jnp.sqrt(v_hat) + ADAM_EPS) + ADAM_WD * w)
    return delta, m_new, v_new


def _sum_adam(parts, w, m, v, name):
    rows, cols = w.shape
    tr = _row_tile(rows, cols, 4 * parts.dtype.itemsize + 7 * 4)

    def body(p_ref, w_ref, m_ref, v_ref, g_ref, d_ref, mo_ref, vo_ref):
        g = p_ref[0].astype(F32)
        for s in range(1, 4):
            g = g + p_ref[s].astype(F32)
        delta, m_new, v_new = _adam_math(g, w_ref[...], m_ref[...], v_ref[...])
        g_ref[...] = g
        d_ref[...] = delta
        mo_ref[...] = m_new
        vo_ref[...] = v_new

    blk = pl.BlockSpec((tr, cols), lambda i: (i, 0))
    out = SDS((rows, cols), F32)
    return pl.pallas_call(
        body, grid=(rows // tr,),
        in_specs=[pl.BlockSpec((4, tr, cols), lambda i: (0, i, 0)), blk, blk, blk],
        out_specs=[blk, blk, blk, blk], out_shape=[out, out, out, out],
        compiler_params=_params("parallel"), name=name)(parts, w, m, v)


SMALL_ROWS = 64


def _small_allreduce_adam(gpart, w, m, v):
    def body(g_ref, w_ref, m_ref, v_ref, go_ref, d_ref, mo_ref, vo_ref, gath, send_sems, recv_sems):
        x, y, c = lax.axis_index("x"), lax.axis_index("y"), lax.axis_index("c")
        me = 4 * x + 2 * y + c
        gath[me] = g_ref[...]
        copies = []
        for k in range(1, N_DEV):
            fx, fy, fc = (k >> 2) & 1, (k >> 1) & 1, k & 1
            peer = (x ^ fx, y ^ fy, c ^ fc)
            copies.append(pltpu.make_async_remote_copy(
                src_ref=gath.at[me], dst_ref=gath.at[me], send_sem=send_sems.at[k - 1], recv_sem=recv_sems.at[k - 1],
                device_id=peer, device_id_type=MESH))
        for cp in copies:
            cp.start()
        for cp in copies:
            cp.wait_recv()
        for cp in copies:
            cp.wait_send()
        g = gath[0]
        for d in range(1, N_DEV):
            g = g + gath[d]
        delta, m_new, v_new = _adam_math(g, w_ref[...], m_ref[...], v_ref[...])
        go_ref[...] = g
        d_ref[...] = delta
        mo_ref[...] = m_new
        vo_ref[...] = v_new

    vm = pl.BlockSpec(memory_space=pltpu.VMEM)
    out = SDS((SMALL_ROWS, 128), F32)
    return pl.pallas_call(
        body, in_specs=[vm] * 4, out_specs=[vm] * 4, out_shape=[out] * 4,
        scratch_shapes=[pltpu.VMEM((N_DEV, SMALL_ROWS, 128), F32), pltpu.SemaphoreType.DMA((N_DEV - 1,)),
                        pltpu.SemaphoreType.DMA((N_DEV - 1,))],
        name="small_allreduce_adam")(gpart, w, m, v)


BIG = ("w_in", "conv_w", "w_conv_out", "w_attn_out", "w_o", "w_ffn_gate", "w_ffn_up", "w_ffn_down")
COL_SHARDED = ("w_in", "conv_w", "w_attn_out", "w_ffn_gate", "w_ffn_up")
SMALL = ("norm1_g", "gate_b", "conv_b", "conv_ln_g", "conv_ln_b", "norm2_g", "norm_f_g")
WEIGHTS = ("norm1_g", "w_in", "gate_b", "conv_w", "conv_b", "conv_ln_g", "conv_ln_b", "w_conv_out", "w_attn_out", "w_o",
           "norm2_g", "w_ffn_gate", "w_ffn_up", "w_ffn_down", "norm_f_g")


def _shard2d(name, a):
    a = a.reshape(a.shape[-2], a.shape[-1])
    if name == "conv_w":
        a = jnp.pad(a, ((0, CONV_PAD - CONV_K), (0, 0)))
    return a


def _gathered_to_full(name, g):
    if name in COL_SHARDED:
        return g.transpose(1, 0, 2).reshape(g.shape[1], N_DEV * g.shape[2])
    return g.reshape(N_DEV * g.shape[1], g.shape[2])


def _full_to_blocks(name, g):
    if name in COL_SHARDED:
        return g.reshape(g.shape[0], N_DEV, g.shape[1] // N_DEV).transpose(1, 0, 2)
    return g.reshape(N_DEV, g.shape[0] // N_DEV, g.shape[1])


def _pack_small(d):
    return jnp.concatenate([d[n].reshape(-1) for n in SMALL]).reshape(SMALL_ROWS, 128)


def _unpack_small(p, like):
    flat = p.reshape(-1)
    out, off = {}, 0
    for n in SMALL:
        size = like[n].size
        out[n] = flat[off:off + size].reshape(like[n].shape)
        off += size
    return out


def kernel(x, norm1_g, w_in, gate_b, conv_w, conv_b, conv_ln_g, conv_ln_b, w_conv_out, w_attn_out, w_o, norm2_g, w_ffn_gate, w_ffn_up, w_ffn_down, norm_f_g, loss_target, m_norm1_g, m_w_in, m_gate_b, m_conv_w, m_conv_b, m_conv_ln_g, m_conv_ln_b, m_w_conv_out, m_w_attn_out, m_w_o, m_norm2_g, m_w_ffn_gate, m_w_ffn_up, m_w_ffn_down, m_norm_f_g, v_norm1_g, v_w_in, v_gate_b, v_conv_w, v_conv_b, v_conv_ln_g, v_conv_ln_b, v_w_conv_out, v_w_attn_out, v_w_o, v_norm2_g, v_w_ffn_gate, v_w_ffn_up, v_w_ffn_down, v_norm_f_g):
    wts = dict(norm1_g=norm1_g, w_in=w_in, gate_b=gate_b, conv_w=conv_w, conv_b=conv_b, conv_ln_g=conv_ln_g,
               conv_ln_b=conv_ln_b, w_conv_out=w_conv_out, w_attn_out=w_attn_out, w_o=w_o, norm2_g=norm2_g,
               w_ffn_gate=w_ffn_gate, w_ffn_up=w_ffn_up, w_ffn_down=w_ffn_down, norm_f_g=norm_f_g)
    mom1 = dict(norm1_g=m_norm1_g, w_in=m_w_in, gate_b=m_gate_b, conv_w=m_conv_w, conv_b=m_conv_b, conv_ln_g=m_conv_ln_g,
                conv_ln_b=m_conv_ln_b, w_conv_out=m_w_conv_out, w_attn_out=m_w_attn_out, w_o=m_w_o, norm2_g=m_norm2_g,
                w_ffn_gate=m_w_ffn_gate, w_ffn_up=m_w_ffn_up, w_ffn_down=m_w_ffn_down, norm_f_g=m_norm_f_g)
    mom2 = dict(norm1_g=v_norm1_g, w_in=v_w_in, gate_b=v_gate_b, conv_w=v_conv_w, conv_b=v_conv_b, conv_ln_g=v_conv_ln_g,
                conv_ln_b=v_conv_ln_b, w_conv_out=v_w_conv_out, w_attn_out=v_w_attn_out, w_o=v_w_o, norm2_g=v_norm2_g,
                w_ffn_gate=v_w_ffn_gate, w_ffn_up=v_w_ffn_up, w_ffn_down=v_w_ffn_down, norm_f_g=v_norm_f_g)

    T = x.shape[0] * x.shape[1]
    x2 = x.reshape(T, D_MODEL)
    t2 = loss_target.reshape(T, D_MODEL)

    shards = {n: _shard2d(n, wts[n]) for n in BIG}
    send = [shards[n] if n == "conv_w" else shards[n].astype(BF16) for n in BIG]
    gathered = _all_gather(send)
    full = {n: _gathered_to_full(n, g) for n, g in zip(BIG, gathered)}
    small = {n: wts[n].reshape(1, -1) for n in SMALL}

    loss_part, grad_x, gw, gsmall = _local_step(x2, t2, full, small)

    blocks = [_full_to_blocks(n, gw[n]) for n in BIG]
    core = lax.axis_index("c").astype(jnp.int32).reshape(1)
    from_sibling = _exchange_sibling(blocks)
    chip_sums = [_add_pair(g, r, core, "chip_sum_" + n) for n, g, r in zip(BIG, blocks, from_sibling)]
    by_chip = _exchange_chips(chip_sums)

    grads, deltas, new_m, new_v = {}, {}, {}, {}
    for n, parts in zip(BIG, by_chip):
        g, d, mo, vo = _sum_adam(parts, shards[n], _shard2d(n, mom1[n]), _shard2d(n, mom2[n]), "adam_" + n)
        for dst, val in ((grads, g), (deltas, d), (new_m, mo), (new_v, vo)):
            if n == "conv_w":
                val = val[:CONV_K]
            dst[n] = val.reshape(wts[n].shape)

    sg, sd, sm, sv = _small_allreduce_adam(_pack_small(gsmall), _pack_small(wts), _pack_small(mom1), _pack_small(mom2))
    for dst, val in ((grads, sg), (deltas, sd), (new_m, sm), (new_v, sv)):
        dst.update(_unpack_small(val, wts))

    loss = lax.psum(loss_part[0, 0], ("x", "y", "c"))
    return (loss, grad_x.reshape(x.shape), *[grads[n] for n in WEIGHTS], *[deltas[n] for n in WEIGHTS],
            *[new_m[n] for n in WEIGHTS], *[new_v[n] for n in WEIGHTS])
```

```python
import math

import numpy as np
import jax
import jax.numpy as jnp
from jax import lax
from jax.experimental import pallas as pl
from jax.experimental.pallas import tpu as pltpu

F32 = jnp.float32
BF16 = jnp.bfloat16
SDS = jax.ShapeDtypeStruct
MESH = pl.DeviceIdType.MESH

D_MODEL = 1024
SEQ = 2048
HEAD_DIM = 64
GROUPS = ((128, 1), (512, 4), (2048, 16))
HEADS_PER_GROUP = 8
N_HEADS = 24
ATTN_WIDTH = N_HEADS * HEAD_DIM
ATTN_OUT = HEADS_PER_GROUP * HEAD_DIM
CONV_K = 31
CONV_PAD = 32
D_FF = 2816
IN_WIDTH = 3 * ATTN_WIDTH + 2 * D_MODEL + 2 * D_MODEL
RMS_EPS = 1e-6
LN_EPS = 1e-5
Q_BLOCK = 128
LANES = 128
NEG = -1e30
N_DEV = 8

ADAM_LR = 0.001
ADAM_B1 = 0.9
ADAM_B2 = 0.999
ADAM_EPS = 1e-08
ADAM_WD = 0.01
ADAM_STEP = 10


def _alibi_slope_list(n):
    def pow2(m):
        start = 2.0 ** (-8.0 / m)
        return [start ** (i + 1) for i in range(m)]
    if math.log2(n).is_integer():
        return pow2(n)
    c = 2 ** math.floor(math.log2(n))
    return pow2(c) + _alibi_slope_list(2 * c)[0::2][: n - c]


def _slopes_times_dilation():
    s = np.asarray(sorted(_alibi_slope_list(N_HEADS), reverse=True), dtype=np.float32).reshape(3, HEADS_PER_GROUP)
    r = np.asarray([g[1] for g in GROUPS], dtype=np.float32)[:, None]
    return (s * r).reshape(N_HEADS)


def _sigmoid(x):
    return 1.0 / (1.0 + jnp.exp(-x))


def _dot(a, b):
    return jnp.dot(a, b, preferred_element_type=F32)


def _dot_nt(a, b):
    return lax.dot_general(a, b, (((1,), (1,)), ((), ())), preferred_element_type=F32)


def _dot_tn(a, b):
    return lax.dot_general(a, b, (((0,), (0,)), ((), ())), preferred_element_type=F32)


def _rowsum(x):
    return jnp.sum(x, axis=0, keepdims=True)


def _params(*sem):
    return pltpu.CompilerParams(dimension_semantics=sem)


def _norm_qkv(x, g1, w_in):
    T = x.shape[0]
    tm, tn = 1024, 512
    nj = 3 * ATTN_WIDTH // tn

    def body(x_ref, g_ref, w_ref, h_ref, o_ref, h_scr):
        @pl.when(pl.program_id(1) == 0)
        def _():
            xv = x_ref[...]
            r = lax.rsqrt(jnp.mean(xv * xv, axis=-1, keepdims=True) + RMS_EPS)
            h = (xv * r * g_ref[...]).astype(BF16)
            h_scr[...] = h
            h_ref[...] = h
        o_ref[...] = _dot(h_scr[...], w_ref[...]).astype(o_ref.dtype)

    return pl.pallas_call(
        body, grid=(T // tm, nj),
        in_specs=[pl.BlockSpec((tm, D_MODEL), lambda i, j: (i, 0)),
                  pl.BlockSpec((1, D_MODEL), lambda i, j: (0, 0)),
                  pl.BlockSpec((D_MODEL, tn), lambda i, j: (0, j))],
        out_specs=[pl.BlockSpec((tm, D_MODEL), lambda i, j: (i, 0)),
                   pl.BlockSpec((tm, tn), lambda i, j: (i, j))],
        out_shape=[SDS((T, D_MODEL), BF16), SDS((T, 3 * ATTN_WIDTH), F32)],
        scratch_shapes=[pltpu.VMEM((tm, D_MODEL), BF16)],
        compiler_params=_params("parallel", "arbitrary"), name="norm_qkv")(x, g1, w_in)


def _mm_nn(a, w, col_block0, n_out, out_dtype, name, tm=1024, tn=512):
    M, K = a.shape

    def body(a_ref, w_ref, o_ref):
        o_ref[...] = _dot(a_ref[...], w_ref[...]).astype(o_ref.dtype)

    return pl.pallas_call(
        body, grid=(M // tm, n_out // tn),
        in_specs=[pl.BlockSpec((tm, K), lambda i, j: (i, 0)),
                  pl.BlockSpec((K, tn), lambda i, j: (0, j + col_block0))],
        out_specs=pl.BlockSpec((tm, tn), lambda i, j: (i, j)),
        out_shape=SDS((M, n_out), out_dtype),
        compiler_params=_params("parallel", "parallel"), name=name)(a, w)


def _mm_tn(a, b, out_dtype, name, tn, tt=512):
    T, K = a.shape
    N = b.shape[1]
    nt = T // tt

    def body(a_ref, b_ref, o_ref, acc):
        t = pl.program_id(1)
        p = _dot_tn(a_ref[...], b_ref[...])

        @pl.when(t == 0)
        def _():
            acc[...] = p

        @pl.when(t > 0)
        def _():
            acc[...] += p

        @pl.when(t == nt - 1)
        def _():
            o_ref[...] = acc[...].astype(o_ref.dtype)

    return pl.pallas_call(
        body, grid=(N // tn, nt),
        in_specs=[pl.BlockSpec((tt, K), lambda j, t: (t, 0)),
                  pl.BlockSpec((tt, tn), lambda j, t: (t, j))],
        out_specs=pl.BlockSpec((K, tn), lambda j, t: (0, j)),
        out_shape=SDS((K, N), out_dtype),
        scratch_shapes=[pltpu.VMEM((K, tn), F32)],
        compiler_params=_params("parallel", "arbitrary"), name=name)(a, b)


def _gather_classes(src_ref, dst, r):
    L = SEQ // r
    for c in range(r):
        dst[c * L:(c + 1) * L, :] = src_ref[0, pl.ds(c, L, stride=r), :].astype(dst.dtype)


def _scatter_classes(src, dst, r):
    L = SEQ // r
    for c in range(r):
        dst[pl.ds(c, L, stride=r), :] = src[c * L:(c + 1) * L, :].astype(dst.dtype)


def _attn_masks(slope_r):
    qi = lax.broadcasted_iota(jnp.int32, (Q_BLOCK, Q_BLOCK), 0)
    kj = lax.broadcasted_iota(jnp.int32, (Q_BLOCK, Q_BLOCK), 1)
    rel = (qi - kj).astype(F32)
    bias_cur = jnp.where(qi >= kj, -slope_r * rel, NEG)
    bias_prev = jnp.where(qi <= kj, -slope_r * (rel + float(Q_BLOCK)), NEG)
    return bias_cur, bias_prev


def _store_biases(bias, sl_ref, g, hp):
    for hh in range(2):
        cur, prev = _attn_masks(sl_ref[g * HEADS_PER_GROUP + 2 * hp + hh])
        bias[2 * hh] = cur
        bias[2 * hh + 1] = prev


def _unit_offsets(u, nb):
    off = pl.multiple_of(u * Q_BLOCK, Q_BLOCK)
    offp = pl.multiple_of(jnp.maximum(u - 1, 0) * Q_BLOCK, Q_BLOCK)
    n = u & (nb - 1)
    c = u >> int(math.log2(nb))
    return off, offp, n == 0, c, n


def _attn_fwd(qkv, slopes_r, batch):
    nblk = SEQ // Q_BLOCK

    def body(sl_ref, *refs):
        qkv_refs = refs[:9]
        att_ref, lse_ref = refs[9:11]
        qd, kd, vd, opos, lpos, bias = refs[11:]
        hp = pl.program_id(1)
        low = lax.broadcasted_iota(jnp.int32, (Q_BLOCK, LANES), 1) < HEAD_DIM

        for g in range(3):
            r = GROUPS[g][1]
            nb = SEQ // r // Q_BLOCK
            _gather_classes(qkv_refs[3 * g], qd, r)
            _gather_classes(qkv_refs[3 * g + 1], kd, r)
            _gather_classes(qkv_refs[3 * g + 2], vd, r)
            _store_biases(bias, sl_ref, g, hp)

            def unit(u, carry, g=g, r=r, nb=nb):
                off, offp, first, c, n = _unit_offsets(u, nb)
                qb = qd[pl.ds(off, Q_BLOCK), :]
                kc = kd[pl.ds(off, Q_BLOCK), :]
                kp = kd[pl.ds(offp, Q_BLOCK), :]
                vc = vd[pl.ds(off, Q_BLOCK), :]
                vp = vd[pl.ds(offp, Q_BLOCK), :]
                res = []
                for hh in range(2):
                    qm = jnp.where(low if hh == 0 else jnp.logical_not(low), qb, jnp.zeros_like(qb))
                    sc = _dot_nt(qm, kc) * 0.125 + bias[2 * hh]
                    sp = jnp.where(first, NEG, _dot_nt(qm, kp) * 0.125 + bias[2 * hh + 1])
                    m = jnp.maximum(jnp.max(sc, axis=-1, keepdims=True), jnp.max(sp, axis=-1, keepdims=True))
                    pc = jnp.exp(sc - m)
                    pp = jnp.exp(sp - m)
                    l = jnp.sum(pc, axis=-1, keepdims=True) + jnp.sum(pp, axis=-1, keepdims=True)
                    o = (_dot(pc.astype(BF16), vc) + _dot(pp.astype(BF16), vp)) / l
                    res.append((o, m + jnp.log(l)))
                rows = pl.ds(c + n * (Q_BLOCK * r), Q_BLOCK, stride=r)
                opos[g, rows, :] = jnp.where(low, res[0][0], res[1][0])
                lpos[g, rows, :] = jnp.where(low, res[0][1], res[1][1])
                return carry

            lax.fori_loop(0, nblk, unit, 0)

        def merge(i, carry):
            rows = pl.ds(pl.multiple_of(i * 256, 256), 256)
            l0, l1, l2 = lpos[0, rows, :], lpos[1, rows, :], lpos[2, rows, :]
            m = jnp.maximum(jnp.maximum(l0, l1), l2)
            e0, e1, e2 = jnp.exp(l0 - m), jnp.exp(l1 - m), jnp.exp(l2 - m)
            den = e0 + e1 + e2
            att = (e0 * opos[0, rows, :] + e1 * opos[1, rows, :] + e2 * opos[2, rows, :]) / den
            att_ref[0, rows, :] = att.astype(att_ref.dtype)
            lse_ref[0, rows, :] = m + jnp.log(den)
            return carry

        lax.fori_loop(0, SEQ // 256, merge, 0)

    def col(sec, g):
        return pl.BlockSpec((1, SEQ, LANES), lambda b, hp: (b, 0, sec * 12 + g * 4 + hp))

    out = pl.BlockSpec((1, SEQ, LANES), lambda b, hp: (b, 0, hp))
    return pl.pallas_call(
        body, grid=(batch, 4),
        in_specs=[pl.BlockSpec(memory_space=pltpu.SMEM)] + [col(sec, g) for g in range(3) for sec in range(3)],
        out_specs=[out, out],
        out_shape=[SDS((batch, SEQ, ATTN_OUT), BF16), SDS((batch, SEQ, ATTN_OUT), F32)],
        scratch_shapes=[pltpu.VMEM((SEQ, LANES), BF16), pltpu.VMEM((SEQ, LANES), BF16), pltpu.VMEM((SEQ, LANES), BF16),
                        pltpu.VMEM((3, SEQ, LANES), F32), pltpu.VMEM((3, SEQ, LANES), F32),
                        pltpu.VMEM((4, Q_BLOCK, Q_BLOCK), F32)],
        compiler_params=_params("parallel", "parallel"), name="attn_fwd")(slopes_r, *([qkv] * 9))


def _attn_bwd(qkv, datt, lse, dsum, slopes_r, batch):
    nblk = SEQ // Q_BLOCK

    def body(sl_ref, q_ref, k_ref, v_ref, do_ref, l_ref, d_ref, dq_ref, dk_ref, dv_ref,
             qd, kd, vd, dod, ld, dd, dq_acc, dk_acc, dv_acc, stage, bias):
        gid, hp = pl.program_id(1), pl.program_id(2)
        low = lax.broadcasted_iota(jnp.int32, (Q_BLOCK, LANES), 1) < HEAD_DIM

        def section(g):
            r = GROUPS[g][1]
            nb = SEQ // r // Q_BLOCK
            _gather_classes(q_ref, qd, r)
            _gather_classes(k_ref, kd, r)
            _gather_classes(v_ref, vd, r)
            _gather_classes(do_ref, dod, r)
            _gather_classes(l_ref, ld, r)
            _gather_classes(d_ref, dd, r)
            _store_biases(bias, sl_ref, g, hp)
            dk_acc[...] = jnp.zeros_like(dk_acc)
            dv_acc[...] = jnp.zeros_like(dv_acc)

            def unit(u, carry):
                off, offp, first, _, _ = _unit_offsets(u, nb)
                qb = qd[pl.ds(off, Q_BLOCK), :]
                kc = kd[pl.ds(off, Q_BLOCK), :]
                kp = kd[pl.ds(offp, Q_BLOCK), :]
                vc = vd[pl.ds(off, Q_BLOCK), :]
                vp = vd[pl.ds(offp, Q_BLOCK), :]
                dob = dod[pl.ds(off, Q_BLOCK), :]
                lse2 = ld[pl.ds(off, Q_BLOCK), :]
                dsum2 = dd[pl.ds(off, Q_BLOCK), :]
                res = []
                for hh in range(2):
                    sel = low if hh == 0 else jnp.logical_not(low)
                    qm = jnp.where(sel, qb, jnp.zeros_like(qb))
                    dom = jnp.where(sel, dob, jnp.zeros_like(dob))
                    lse_h = lse2[:, hh * HEAD_DIM:hh * HEAD_DIM + 1]
                    dsum_h = dsum2[:, hh * HEAD_DIM:hh * HEAD_DIM + 1]
                    sc = _dot_nt(qm, kc) * 0.125 + bias[2 * hh]
                    sp = jnp.where(first, NEG, _dot_nt(qm, kp) * 0.125 + bias[2 * hh + 1])
                    pc = jnp.exp(sc - lse_h)
                    pp = jnp.exp(sp - lse_h)
                    dsc = (pc * (_dot_nt(dom, vc) - dsum_h)).astype(BF16)
                    dsp = (pp * (_dot_nt(dom, vp) - dsum_h)).astype(BF16)
                    res.append((_dot(dsc, kc) + _dot(dsp, kp), _dot_tn(dsc, qb), _dot_tn(dsp, qb),
                                _dot_tn(pc.astype(BF16), dob), _dot_tn(pp.astype(BF16), dob)))
                pick = lambda i: jnp.where(low, res[0][i], res[1][i])
                dq_acc[pl.ds(off, Q_BLOCK), :] = pick(0) * 0.125
                dk_acc[pl.ds(off, Q_BLOCK), :] += pick(1) * 0.125
                dk_acc[pl.ds(offp, Q_BLOCK), :] += pick(2) * 0.125
                dv_acc[pl.ds(off, Q_BLOCK), :] += pick(3)
                dv_acc[pl.ds(offp, Q_BLOCK), :] += pick(4)
                return carry

            lax.fori_loop(0, nblk, unit, 0)
            for acc, out_ref in ((dq_acc, dq_ref), (dk_acc, dk_ref), (dv_acc, dv_ref)):
                _scatter_classes(acc, stage, r)
                out_ref[0] = stage[...].astype(out_ref.dtype)

        for g in range(3):
            pl.when(gid == g)(lambda g=g: section(g))

    def col(sec):
        return pl.BlockSpec((1, SEQ, LANES), lambda b, g, hp: (b, 0, sec * 12 + g * 4 + hp))

    pos = pl.BlockSpec((1, SEQ, LANES), lambda b, g, hp: (b, 0, hp))
    dout = pl.BlockSpec((1, SEQ, LANES), lambda b, g, hp: (b, 0, g * 4 + hp))
    out = SDS((batch, SEQ, ATTN_WIDTH), BF16)
    seq_bf = pltpu.VMEM((SEQ, LANES), BF16)
    seq_f = pltpu.VMEM((SEQ, LANES), F32)
    return pl.pallas_call(
        body, grid=(batch, 3, 4),
        in_specs=[pl.BlockSpec(memory_space=pltpu.SMEM), col(0), col(1), col(2), pos, pos, pos],
        out_specs=[dout, dout, dout],
        out_shape=[out, out, out],
        scratch_shapes=[seq_bf, seq_bf, seq_bf, seq_bf, seq_f, seq_f, seq_f, seq_f, seq_f, seq_f,
                        pltpu.VMEM((4, Q_BLOCK, Q_BLOCK), F32)],
        compiler_params=_params("parallel", "parallel", "parallel"), name="attn_bwd")(
            slopes_r, qkv, qkv, qkv, datt, lse, dsum)


CONV_TC = 256
CONV_ROWS = 64


def _conv_fwd(u, conv_w, conv_b, batch):
    nct = D_MODEL // CONV_TC

    def body(ua_ref, ub_ref, w_ref, b_ref, o_ref, pad):
        pad[0:CONV_PAD, :] = jnp.zeros((CONV_PAD, CONV_TC), F32)
        pad[CONV_PAD:, :] = ua_ref[0] * _sigmoid(ub_ref[0])

        def chunk(c, carry):
            base = pl.multiple_of(c * CONV_ROWS, CONV_ROWS)
            win = pad[pl.ds(base, CONV_ROWS + CONV_PAD), :]
            acc = jnp.broadcast_to(b_ref[...], (CONV_ROWS, CONV_TC))
            for t in range(CONV_K):
                s = t + CONV_PAD - (CONV_K - 1)
                acc = acc + win[s:s + CONV_ROWS, :] * w_ref[t:t + 1, :]
            o_ref[0, pl.ds(base, CONV_ROWS), :] = acc
            return carry

        lax.fori_loop(0, SEQ // CONV_ROWS, chunk, 0)

    return pl.pallas_call(
        body, grid=(nct, batch),
        in_specs=[pl.BlockSpec((1, SEQ, CONV_TC), lambda j, b: (b, 0, j)),
                  pl.BlockSpec((1, SEQ, CONV_TC), lambda j, b: (b, 0, j + nct)),
                  pl.BlockSpec((CONV_PAD, CONV_TC), lambda j, b: (0, j)),
                  pl.BlockSpec((1, CONV_TC), lambda j, b: (0, j))],
        out_specs=pl.BlockSpec((1, SEQ, CONV_TC), lambda j, b: (b, 0, j)),
        out_shape=SDS((batch, SEQ, D_MODEL), F32),
        scratch_shapes=[pltpu.VMEM((SEQ + CONV_PAD, CONV_TC), F32)],
        compiler_params=_params("parallel", "parallel"), name="conv_fwd")(u, u, conv_w, conv_b)


def _conv_bwd(u, dc1, conv_w, batch):
    nct = D_MODEL // CONV_TC
    nchunk = SEQ // CONV_ROWS

    def body(ua_ref, ub_ref, d_ref, w_ref, dua_ref, dub_ref, gw_ref, gb_ref, padc, padd, gacc):
        b = pl.program_id(1)
        sig = _sigmoid(ub_ref[0])
        padc[0:CONV_PAD, :] = jnp.zeros((CONV_PAD, CONV_TC), F32)
        padc[CONV_PAD:, :] = ua_ref[0] * sig
        padd[0:SEQ, :] = d_ref[0]
        padd[SEQ:, :] = jnp.zeros((CONV_PAD, CONV_TC), F32)

        @pl.when(b == 0)
        def _():
            gacc[...] = jnp.zeros_like(gacc)
            gb_ref[...] = jnp.zeros_like(gb_ref)

        gb_ref[...] += _rowsum(d_ref[0])

        def chunk(c, carry):
            base = pl.multiple_of(c * CONV_ROWS, CONV_ROWS)
            wind = padd[pl.ds(base, CONV_ROWS + CONV_PAD), :]
            winc = padc[pl.ds(base, CONV_ROWS + CONV_PAD), :]
            dcur = wind[0:CONV_ROWS, :]
            acc = jnp.zeros((CONV_ROWS, CONV_TC), F32)
            for t in range(CONV_K):
                s = CONV_K - 1 - t
                acc = acc + wind[s:s + CONV_ROWS, :] * w_ref[t:t + 1, :]
                sc = t + CONV_PAD - (CONV_K - 1)
                prod = winc[sc:sc + CONV_ROWS, :] * dcur
                gacc[t] += jnp.sum(prod.reshape(CONV_ROWS // 8, 8, CONV_TC), axis=0)
            ua = ua_ref[0, pl.ds(base, CONV_ROWS), :]
            sg = _sigmoid(ub_ref[0, pl.ds(base, CONV_ROWS), :])
            dua_ref[0, pl.ds(base, CONV_ROWS), :] = (acc * sg).astype(dua_ref.dtype)
            dub_ref[0, pl.ds(base, CONV_ROWS), :] = (acc * ua * sg * (1.0 - sg)).astype(dub_ref.dtype)
            return carry

        lax.fori_loop(0, nchunk, chunk, 0)

        @pl.when(b == batch - 1)
        def _():
            for t in range(CONV_K):
                gw_ref[t:t + 1, :] = jnp.sum(gacc[t], axis=0, keepdims=True)
            gw_ref[CONV_K:CONV_PAD, :] = jnp.zeros((CONV_PAD - CONV_K, CONV_TC), F32)

    du = SDS((batch, SEQ, D_MODEL), BF16)
    return pl.pallas_call(
        body, grid=(nct, batch),
        in_specs=[pl.BlockSpec((1, SEQ, CONV_TC), lambda j, b: (b, 0, j)),
                  pl.BlockSpec((1, SEQ, CONV_TC), lambda j, b: (b, 0, j + nct)),
                  pl.BlockSpec((1, SEQ, CONV_TC), lambda j, b: (b, 0, j)),
                  pl.BlockSpec((CONV_PAD, CONV_TC), lambda j, b: (0, j))],
        out_specs=[pl.BlockSpec((1, SEQ, CONV_TC), lambda j, b: (b, 0, j)),
                   pl.BlockSpec((1, SEQ, CONV_TC), lambda j, b: (b, 0, j)),
                   pl.BlockSpec((CONV_PAD, CONV_TC), lambda j, b: (0, j)),
                   pl.BlockSpec((1, CONV_TC), lambda j, b: (0, j))],
        out_shape=[du, du, SDS((CONV_PAD, D_MODEL), F32), SDS((1, D_MODEL), F32)],
        scratch_shapes=[pltpu.VMEM((SEQ + CONV_PAD, CONV_TC), F32), pltpu.VMEM((SEQ + CONV_PAD, CONV_TC), F32),
                        pltpu.VMEM((CONV_K, 8, CONV_TC), F32)],
        compiler_params=_params("parallel", "arbitrary"), name="conv_bwd")(u, u, dc1, conv_w)


MID_TM = 256


def _layernorm_stats(c1):
    mu = jnp.mean(c1, axis=-1, keepdims=True)
    cen = c1 - mu
    rs = lax.rsqrt(jnp.mean(cen * cen, axis=-1, keepdims=True) + LN_EPS)
    return cen * rs, rs


def _mid_fwd(att, c1, logits, x, w_a, w_c, w_o, gate_b, ln_g, ln_b, g2):
    T = x.shape[0]
    tm = MID_TM

    def body(att_ref, c1_ref, lg_ref, x_ref, wa_ref, wc_ref, wo_ref, gb_ref, lng_ref, lnb_ref, g2_ref,
             c3_ref, ya_ref, yc_ref, mix_ref, x1_ref, h2_ref):
        ya = _dot(att_ref[...], wa_ref[...])
        xh, _ = _layernorm_stats(c1_ref[...])
        c2 = xh * lng_ref[...] + lnb_ref[...]
        c3 = (c2 * _sigmoid(c2)).astype(BF16)
        c3_ref[...] = c3
        yc = _dot(c3, wc_ref[...])
        gates = _sigmoid(lg_ref[...] + gb_ref[...])
        mix = (gates[:, :D_MODEL] * ya + gates[:, D_MODEL:] * yc).astype(BF16)
        ya_ref[...] = ya.astype(BF16)
        yc_ref[...] = yc.astype(BF16)
        mix_ref[...] = mix
        x1 = x_ref[...] + _dot(mix, wo_ref[...])
        x1_ref[...] = x1
        r = lax.rsqrt(jnp.mean(x1 * x1, axis=-1, keepdims=True) + RMS_EPS)
        h2_ref[...] = (x1 * r * g2_ref[...]).astype(BF16)

    row = lambda n: pl.BlockSpec((tm, n), lambda i: (i, 0))
    full = lambda a, b: pl.BlockSpec((a, b), lambda i: (0, 0))
    return pl.pallas_call(
        body, grid=(T // tm,),
        in_specs=[row(ATTN_OUT), row(D_MODEL), row(2 * D_MODEL), row(D_MODEL),
                  full(ATTN_OUT, D_MODEL), full(D_MODEL, D_MODEL), full(D_MODEL, D_MODEL),
                  full(1, 2 * D_MODEL), full(1, D_MODEL), full(1, D_MODEL), full(1, D_MODEL)],
        out_specs=[row(D_MODEL), row(D_MODEL), row(D_MODEL), row(D_MODEL), row(D_MODEL), row(D_MODEL)],
        out_shape=[SDS((T, D_MODEL), BF16), SDS((T, D_MODEL), BF16), SDS((T, D_MODEL), BF16), SDS((T, D_MODEL), BF16),
                   SDS((T, D_MODEL), F32), SDS((T, D_MODEL), BF16)],
        compiler_params=_params("parallel"), name="mid_fwd")(att, c1, logits, x, w_a, w_c, w_o, gate_b, ln_g, ln_b, g2)


def _mid_bwd(dx1b, ya, yc, logits, att, c1, w_a, w_c, w_o, gate_b, ln_g, ln_b, head_ones):
    T = dx1b.shape[0]
    tm = MID_TM

    def body(dx_ref, ya_ref, yc_ref, lg_ref, att_ref, c1_ref, wa_ref, wc_ref, wo_ref, gb_ref, lng_ref, lnb_ref, e_ref,
             dlg_ref, dya_ref, dyc_ref, datt_ref, dsum_ref, dc1_ref, ggb_ref, glg_ref, glb_ref):
        @pl.when(pl.program_id(0) == 0)
        def _():
            ggb_ref[...] = jnp.zeros_like(ggb_ref)
            glg_ref[...] = jnp.zeros_like(glg_ref)
            glb_ref[...] = jnp.zeros_like(glb_ref)

        dmix = _dot_nt(dx_ref[...], wo_ref[...])
        gates = _sigmoid(lg_ref[...] + gb_ref[...])
        ga, gc = gates[:, :D_MODEL], gates[:, D_MODEL:]
        dla = dmix * ya_ref[...].astype(F32) * ga * (1.0 - ga)
        dlc = dmix * yc_ref[...].astype(F32) * gc * (1.0 - gc)
        dlg_ref[:, :D_MODEL] = dla.astype(BF16)
        dlg_ref[:, D_MODEL:] = dlc.astype(BF16)
        ggb_ref[:, :D_MODEL] += _rowsum(dla)
        ggb_ref[:, D_MODEL:] += _rowsum(dlc)
        dya = (dmix * ga).astype(BF16)
        dyc = (dmix * gc).astype(BF16)
        dya_ref[...] = dya
        dyc_ref[...] = dyc
        datt = _dot_nt(dya, wa_ref[...])
        datt_ref[...] = datt
        dsum_ref[...] = jnp.dot(datt * att_ref[...].astype(F32), e_ref[...], preferred_element_type=F32,
                                precision=lax.Precision.HIGHEST)
        dc3 = _dot_nt(dyc, wc_ref[...])
        xh, rs = _layernorm_stats(c1_ref[...])
        c2 = xh * lng_ref[...] + lnb_ref[...]
        sg = _sigmoid(c2)
        dc2 = dc3 * (sg * (1.0 + c2 * (1.0 - sg)))
        glg_ref[...] += _rowsum(dc2 * xh)
        glb_ref[...] += _rowsum(dc2)
        dxh = dc2 * lng_ref[...]
        dc1_ref[...] = rs * (dxh - jnp.mean(dxh, axis=-1, keepdims=True) - xh * jnp.mean(dxh * xh, axis=-1, keepdims=True))

    row = lambda n: pl.BlockSpec((tm, n), lambda i: (i, 0))
    full = lambda a, b: pl.BlockSpec((a, b), lambda i: (0, 0))
    return pl.pallas_call(
        body, grid=(T // tm,),
        in_specs=[row(D_MODEL), row(D_MODEL), row(D_MODEL), row(2 * D_MODEL), row(ATTN_OUT), row(D_MODEL),
                  full(ATTN_OUT, D_MODEL), full(D_MODEL, D_MODEL), full(D_MODEL, D_MODEL),
                  full(1, 2 * D_MODEL), full(1, D_MODEL), full(1, D_MODEL), full(ATTN_OUT, ATTN_OUT)],
        out_specs=[row(2 * D_MODEL), row(D_MODEL), row(D_MODEL), row(ATTN_OUT), row(ATTN_OUT), row(D_MODEL),
                   full(1, 2 * D_MODEL), full(1, D_MODEL), full(1, D_MODEL)],
        out_shape=[SDS((T, 2 * D_MODEL), BF16), SDS((T, D_MODEL), BF16), SDS((T, D_MODEL), BF16), SDS((T, ATTN_OUT), F32),
                   SDS((T, ATTN_OUT), F32), SDS((T, D_MODEL), F32),
                   SDS((1, 2 * D_MODEL), F32), SDS((1, D_MODEL), F32), SDS((1, D_MODEL), F32)],
        compiler_params=_params("arbitrary"), name="mid_bwd")(dx1b, ya, yc, logits, att, c1, w_a, w_c, w_o, gate_b, ln_g, ln_b,
                                                               head_ones)


FFN_TM = 512
FFN_TF = D_FF // 2


def _rms_bwd(dy_times_g, xh, r):
    return r * (dy_times_g - xh * jnp.mean(dy_times_g * xh, axis=-1, keepdims=True))


def _ffn_fwd(h2, x1, target, gf, w_g, w_u, w_d):
    T = h2.shape[0]
    tm, tf = FFN_TM, FFN_TF
    nf = D_FF // tf

    def body(h_ref, x1_ref, t_ref, gf_ref, wg_ref, wu_ref, wd_ref,
             a_ref, b_ref, f_ref, dx2_ref, dx2b_ref, loss_ref, gnf_ref, acc):
        i, j = pl.program_id(0), pl.program_id(1)
        h = h_ref[...]
        a = _dot(h, wg_ref[...])
        b = _dot(h, wu_ref[...])
        f = (a * _sigmoid(a) * b).astype(BF16)
        a_ref[...] = a.astype(BF16)
        b_ref[...] = b.astype(BF16)
        f_ref[...] = f
        p = _dot(f, wd_ref[...])

        @pl.when(j == 0)
        def _():
            acc[...] = x1_ref[...] + p

        @pl.when(j > 0)
        def _():
            acc[...] += p

        @pl.when((i == 0) & (j == nf - 1))
        def _():
            loss_ref[...] = jnp.zeros_like(loss_ref)
            gnf_ref[...] = jnp.zeros_like(gnf_ref)

        @pl.when(j == nf - 1)
        def _():
            x2 = acc[...]
            r = lax.rsqrt(jnp.mean(x2 * x2, axis=-1, keepdims=True) + RMS_EPS)
            xh = x2 * r
            err = xh * gf_ref[...] - t_ref[...]
            loss_ref[...] += (0.5 / D_MODEL) * jnp.sum(err * err)
            dy = err * (1.0 / D_MODEL)
            gnf_ref[...] += _rowsum(dy * xh)
            dx2 = _rms_bwd(dy * gf_ref[...], xh, r)
            dx2_ref[...] = dx2
            dx2b_ref[...] = dx2.astype(BF16)

    row = lambda n: pl.BlockSpec((tm, n), lambda i, j: (i, 0))
    ffb = pl.BlockSpec((tm, tf), lambda i, j: (i, j))
    return pl.pallas_call(
        body, grid=(T // tm, nf),
        in_specs=[row(D_MODEL), row(D_MODEL), row(D_MODEL), pl.BlockSpec((1, D_MODEL), lambda i, j: (0, 0)),
                  pl.BlockSpec((D_MODEL, tf), lambda i, j: (0, j)), pl.BlockSpec((D_MODEL, tf), lambda i, j: (0, j)),
                  pl.BlockSpec((tf, D_MODEL), lambda i, j: (j, 0))],
        out_specs=[ffb, ffb, ffb, row(D_MODEL), row(D_MODEL),
                   pl.BlockSpec((1, 128), lambda i, j: (0, 0)), pl.BlockSpec((1, D_MODEL), lambda i, j: (0, 0))],
        out_shape=[SDS((T, D_FF), BF16), SDS((T, D_FF), BF16), SDS((T, D_FF), BF16), SDS((T, D_MODEL), F32),
                   SDS((T, D_MODEL), BF16), SDS((1, 128), F32), SDS((1, D_MODEL), F32)],
        scratch_shapes=[pltpu.VMEM((tm, D_MODEL), F32)],
        compiler_params=_params("arbitrary", "arbitrary"), name="ffn_fwd")(h2, x1, target, gf, w_g, w_u, w_d)


def _ffn_bwd(dx2b, dx2, a, b, x1, g2, w_g, w_u, w_d):
    T = dx2.shape[0]
    tm, tf = FFN_TM, FFN_TF
    nf = D_FF // tf

    def body(dxb_ref, dx2_ref, a_ref, b_ref, x1_ref, g2_ref, wg_ref, wu_ref, wd_ref,
             da_ref, db_ref, dx1_ref, dx1b_ref, gn2_ref, acc):
        i, j = pl.program_id(0), pl.program_id(1)
        df = _dot_nt(dxb_ref[...], wd_ref[...])
        av = a_ref[...].astype(F32)
        bv = b_ref[...].astype(F32)
        sg = _sigmoid(av)
        db = (df * av * sg).astype(BF16)
        da = (df * bv * (sg * (1.0 + av * (1.0 - sg)))).astype(BF16)
        da_ref[...] = da
        db_ref[...] = db
        p = _dot_nt(da, wg_ref[...]) + _dot_nt(db, wu_ref[...])

        @pl.when(j == 0)
        def _():
            acc[...] = p

        @pl.when(j > 0)
        def _():
            acc[...] += p

        @pl.when((i == 0) & (j == nf - 1))
        def _():
            gn2_ref[...] = jnp.zeros_like(gn2_ref)

        @pl.when(j == nf - 1)
        def _():
            dh2 = acc[...]
            x1 = x1_ref[...]
            r = lax.rsqrt(jnp.mean(x1 * x1, axis=-1, keepdims=True) + RMS_EPS)
            xh = x1 * r
            gn2_ref[...] += _rowsum(dh2 * xh)
            dx1 = dx2_ref[...] + _rms_bwd(dh2 * g2_ref[...], xh, r)
            dx1_ref[...] = dx1
            dx1b_ref[...] = dx1.astype(BF16)

    row = lambda n: pl.BlockSpec((tm, n), lambda i, j: (i, 0))
    ffb = pl.BlockSpec((tm, tf), lambda i, j: (i, j))
    return pl.pallas_call(
        body, grid=(T // tm, nf),
        in_specs=[row(D_MODEL), row(D_MODEL), ffb, ffb, row(D_MODEL), pl.BlockSpec((1, D_MODEL), lambda i, j: (0, 0)),
                  pl.BlockSpec((D_MODEL, tf), lambda i, j: (0, j)), pl.BlockSpec((D_MODEL, tf), lambda i, j: (0, j)),
                  pl.BlockSpec((tf, D_MODEL), lambda i, j: (j, 0))],
        out_specs=[ffb, ffb, row(D_MODEL), row(D_MODEL), pl.BlockSpec((1, D_MODEL), lambda i, j: (0, 0))],
        out_shape=[SDS((T, D_FF), BF16), SDS((T, D_FF), BF16), SDS((T, D_MODEL), F32), SDS((T, D_MODEL), BF16),
                   SDS((1, D_MODEL), F32)],
        scratch_shapes=[pltpu.VMEM((tm, D_MODEL), F32)],
        compiler_params=_params("arbitrary", "arbitrary"), name="ffn_bwd")(dx2b, dx2, a, b, x1, g2, w_g, w_u, w_d)


IN_TK = 512


def _in_bwd(pieces, w_in, x, dx1, g1):
    T = x.shape[0]
    tm, tk = 512, IN_TK
    nk = IN_WIDTH // tk
    bounds = []
    lo = 0
    for p in pieces:
        bounds.append((lo, lo + p.shape[1] // tk))
        lo = bounds[-1][1]
    assert lo == nk
    npc = len(pieces)

    def body(*refs):
        p_refs = refs[:npc]
        w_ref, x_ref, dx1_ref, g_ref, dx_ref, gn1_ref, acc = refs[npc:]
        i, j = pl.program_id(0), pl.program_id(1)

        @pl.when(j == 0)
        def _():
            acc[...] = jnp.zeros_like(acc)

        for (lo, hi), p_ref in zip(bounds, p_refs):
            @pl.when((j >= lo) & (j < hi))
            def _(p_ref=p_ref):
                acc[...] += _dot_nt(p_ref[...], w_ref[...])

        @pl.when((i == 0) & (j == nk - 1))
        def _():
            gn1_ref[...] = jnp.zeros_like(gn1_ref)

        @pl.when(j == nk - 1)
        def _():
            dh = acc[...]
            xv = x_ref[...]
            r = lax.rsqrt(jnp.mean(xv * xv, axis=-1, keepdims=True) + RMS_EPS)
            xh = xv * r
            gn1_ref[...] += _rowsum(dh * xh)
            dx_ref[...] = dx1_ref[...] + _rms_bwd(dh * g_ref[...], xh, r)

    def piece_spec(lo, hi):
        return pl.BlockSpec((tm, tk), lambda i, j: (i, jnp.clip(j - lo, 0, hi - lo - 1)))

    row = lambda n: pl.BlockSpec((tm, n), lambda i, j: (i, 0))
    return pl.pallas_call(
        body, grid=(T // tm, nk),
        in_specs=[piece_spec(lo, hi) for lo, hi in bounds]
        + [pl.BlockSpec((D_MODEL, tk), lambda i, j: (0, j)), row(D_MODEL), row(D_MODEL),
           pl.BlockSpec((1, D_MODEL), lambda i, j: (0, 0))],
        out_specs=[row(D_MODEL), pl.BlockSpec((1, D_MODEL), lambda i, j: (0, 0))],
        out_shape=[SDS((T, D_MODEL), F32), SDS((1, D_MODEL), F32)],
        scratch_shapes=[pltpu.VMEM((tm, D_MODEL), F32)],
        compiler_params=_params("arbitrary", "arbitrary"), name="in_bwd")(*pieces, w_in, x, dx1, g1)


def _local_step(x, target, w, small):
    T = x.shape[0]
    batch = T // SEQ
    slopes_r = jnp.asarray(_slopes_times_dilation())

    h, qkv = _norm_qkv(x, small["norm1_g"], w["w_in"])
    u = _mm_nn(h, w["w_in"], 3 * ATTN_WIDTH // 512, 2 * D_MODEL, F32, "proj_u")
    logits = _mm_nn(h, w["w_in"], (3 * ATTN_WIDTH + 2 * D_MODEL) // 512, 2 * D_MODEL, F32, "proj_gate")

    qkv3 = qkv.reshape(batch, SEQ, 3 * ATTN_WIDTH)
    att, lse = _attn_fwd(qkv3, slopes_r, batch)
    att = att.reshape(T, ATTN_OUT)

    u3 = u.reshape(batch, SEQ, 2 * D_MODEL)
    c1 = _conv_fwd(u3, w["conv_w"], small["conv_b"], batch).reshape(T, D_MODEL)

    c3, ya, yc, mix, x1, h2 = _mid_fwd(
        att, c1, logits, x, w["w_attn_out"], w["w_conv_out"], w["w_o"],
        small["gate_b"], small["conv_ln_g"], small["conv_ln_b"], small["norm2_g"])

    a, b, f, dx2, dx2b, loss, g_normf = _ffn_fwd(h2, x1, target, small["norm_f_g"],
                                                   w["w_ffn_gate"], w["w_ffn_up"], w["w_ffn_down"])

    da, db, dx1, dx1b, g_norm2 = _ffn_bwd(dx2b, dx2, a, b, x1, small["norm2_g"],
                                           w["w_ffn_gate"], w["w_ffn_up"], w["w_ffn_down"])
    gw = {}
    gw["w_ffn_down"] = _mm_tn(f, dx2b, BF16, "gw_ffn_down", tn=512)
    gw["w_ffn_gate"] = _mm_tn(h2, da, BF16, "gw_ffn_gate", tn=1408)
    gw["w_ffn_up"] = _mm_tn(h2, db, BF16, "gw_ffn_up", tn=1408)

    head_ones = jnp.asarray(np.kron(np.eye(HEADS_PER_GROUP, dtype=np.float32), np.ones((HEAD_DIM, HEAD_DIM), np.float32)))
    dlogits, dya, dyc, datt, dsum, dc1, g_gate_b, g_ln_g, g_ln_b = _mid_bwd(
        dx1b, ya, yc, logits, att, c1, w["w_attn_out"], w["w_conv_out"], w["w_o"],
        small["gate_b"], small["conv_ln_g"], small["conv_ln_b"], head_ones)
    gw["w_o"] = _mm_tn(mix, dx1b, BF16, "gw_o", tn=512)
    gw["w_attn_out"] = _mm_tn(att, dya, BF16, "gw_attn_out", tn=512)
    gw["w_conv_out"] = _mm_tn(c3, dyc, BF16, "gw_conv_out", tn=512)

    dua, dub, g_conv_w, g_conv_b = _conv_bwd(u3, dc1.reshape(batch, SEQ, D_MODEL), w["conv_w"], batch)

    dq, dk, dv = _attn_bwd(qkv3, datt.reshape(batch, SEQ, ATTN_OUT), lse, dsum.reshape(batch, SEQ, ATTN_OUT),
                           slopes_r, batch)
    pieces = [dq.reshape(T, ATTN_WIDTH), dk.reshape(T, ATTN_WIDTH), dv.reshape(T, ATTN_WIDTH),
              dua.reshape(T, D_MODEL), dub.reshape(T, D_MODEL), dlogits]

    grad_x, g_norm1 = _in_bwd(pieces, w["w_in"], x, dx1, small["norm1_g"])
    names = ("q", "k", "v", "ua", "ub", "gate")
    gw["w_in"] = jnp.concatenate(
        [_mm_tn(h, p, BF16, "gw_in_" + nm, tn=min(p.shape[1], 1024) if p.shape[1] != ATTN_WIDTH else 768)
         for nm, p in zip(names, pieces)], axis=1)
    gw["conv_w"] = g_conv_w

    gsmall = {"norm1_g": g_norm1, "gate_b": g_gate_b, "conv_b": g_conv_b, "conv_ln_g": g_ln_g, "conv_ln_b": g_ln_b,
              "norm2_g": g_norm2, "norm_f_g": g_normf}
    return loss, grad_x, gw, gsmall


ANY = pl.BlockSpec(memory_space=pl.ANY)


def _all_gather(arrs):
    n = len(arrs)

    def body(*refs):
        ins, outs = refs[:n], refs[n:2 * n]
        send_sems, recv_sems, local_sems = refs[2 * n:]
        x, y, c = lax.axis_index("x"), lax.axis_index("y"), lax.axis_index("c")
        me, sibling = (x, y, c), (x, y, 1 - c)
        chips = [(1 - x, y), (x, 1 - y), (1 - x, 1 - y)]

        def copy(a, k, block, to, src=None):
            px, py, pc = block
            dst = outs[a].at[4 * px + 2 * py + pc]
            return pltpu.make_async_remote_copy(
                src_ref=dst if src is None else src, dst_ref=dst,
                send_sem=send_sems.at[a, k], recv_sem=recv_sems.at[a, k], device_id=to, device_id_type=MESH)

        mine = [pltpu.make_async_copy(ins[a], outs[a].at[4 * x + 2 * y + c], local_sems.at[a]) for a in range(n)]
        for cp in mine:
            cp.start()
        first = []
        for j, chip in enumerate(chips):
            first += [copy(a, 1 + j, me, (*chip, c), src=ins[a]) for a in range(n)]
        first += [copy(a, 0, me, sibling, src=ins[a]) for a in range(n)]
        for cp in first:
            cp.start()
        passed = []
        for j, chip in enumerate(chips):
            for a in range(n):
                copy(a, 1 + j, (*chip, c), me).wait_recv()
                cp = copy(a, 4 + j, (*chip, c), sibling)
                cp.start()
                passed.append(cp)
        for a in range(n):
            copy(a, 0, sibling, me).wait_recv()
        for j, chip in enumerate(chips):
            for a in range(n):
                copy(a, 4 + j, (*chip, 1 - c), me).wait_recv()
        for cp in first + passed:
            cp.wait_send()
        for cp in mine:
            cp.wait()

    return pl.pallas_call(
        body, in_specs=[ANY] * n, out_specs=[ANY] * n,
        out_shape=[SDS((N_DEV,) + a.shape, a.dtype) for a in arrs],
        scratch_shapes=[pltpu.SemaphoreType.DMA((n, 7)), pltpu.SemaphoreType.DMA((n, 7)), pltpu.SemaphoreType.DMA((n,))],
        name="all_gather_weights")(*arrs)


def _exchange_sibling(gs):
    n = len(gs)

    def body(*refs):
        ins, outs = refs[:n], refs[n:2 * n]
        send_sems, recv_sems = refs[2 * n:]
        x, y, c = lax.axis_index("x"), lax.axis_index("y"), lax.axis_index("c")
        copies = []
        for a in range(n):
            for j in range(4):
                copies.append(pltpu.make_async_remote_copy(
                    src_ref=ins[a].at[2 * j + (1 - c)], dst_ref=outs[a].at[j],
                    send_sem=send_sems.at[a, j], recv_sem=recv_sems.at[a, j],
                    device_id=(x, y, 1 - c), device_id_type=MESH))
        for cp in copies:
            cp.start()
        for cp in copies:
            cp.wait_recv()
        for cp in copies:
            cp.wait_send()

    return pl.pallas_call(
        body, in_specs=[ANY] * n, out_specs=[ANY] * n,
        out_shape=[SDS((4,) + g.shape[1:], g.dtype) for g in gs],
        scratch_shapes=[pltpu.SemaphoreType.DMA((n, 4)), pltpu.SemaphoreType.DMA((n, 4))],
        name="reduce_scatter_sibling")(*gs)


def _row_tile(rows, cols, itemsize_total):
    budget = (4 << 20) // max(1, cols * itemsize_total)
    if rows <= budget:
        return rows
    t = rows
    while t > budget and t % 2 == 0 and (t // 2) % 16 == 0:
        t //= 2
    return t


def _add_pair(g, r1, core, name):
    _, rows, cols = g.shape
    tr = _row_tile(rows, cols, 3 * g.dtype.itemsize)

    def body(c_ref, g_ref, r_ref, o_ref):
        o_ref[...] = (g_ref[...].astype(F32) + r_ref[...].astype(F32)).astype(o_ref.dtype)

    return pl.pallas_call(
        body,
        grid_spec=pltpu.PrefetchScalarGridSpec(
            num_scalar_prefetch=1, grid=(4, rows // tr),
            in_specs=[pl.BlockSpec((1, tr, cols), lambda j, i, c_ref: (2 * j + c_ref[0], i, 0)),
                      pl.BlockSpec((1, tr, cols), lambda j, i, c_ref: (j, i, 0))],
            out_specs=pl.BlockSpec((1, tr, cols), lambda j, i, c_ref: (j, i, 0))),
        out_shape=SDS((4, rows, cols), g.dtype),
        compiler_params=_params("parallel", "parallel"), name=name)(core, g, r1)


def _exchange_chips(ps):
    n = len(ps)

    def body(*refs):
        ins, outs = refs[:n], refs[n:2 * n]
        send_sems, recv_sems, local_sems = refs[2 * n:]
        x, y, c = lax.axis_index("x"), lax.axis_index("y"), lax.axis_index("c")
        my_chip = 2 * x + y
        mine = [pltpu.make_async_copy(ins[a].at[my_chip], outs[a].at[my_chip], local_sems.at[a]) for a in range(n)]
        for cp in mine:
            cp.start()
        copies = []
        for k, (px, py) in enumerate([(1 - x, y), (x, 1 - y), (1 - x, 1 - y)]):
            for a in range(n):
                copies.append(pltpu.make_async_remote_copy(
                    src_ref=ins[a].at[2 * px + py], dst_ref=outs[a].at[my_chip],
                    send_sem=send_sems.at[a, k], recv_sem=recv_sems.at[a, k],
                    device_id=(px, py, c), device_id_type=MESH))
        for cp in copies:
            cp.start()
        for k, (px, py) in enumerate([(1 - x, y), (x, 1 - y), (1 - x, 1 - y)]):
            for a in range(n):
                pltpu.make_async_remote_copy(
                    src_ref=ins[a].at[my_chip], dst_ref=outs[a].at[2 * px + py],
                    send_sem=send_sems.at[a, k], recv_sem=recv_sems.at[a, k],
                    device_id=(px, py, c), device_id_type=MESH).wait_recv()
        for cp in copies:
            cp.wait_send()
        for cp in mine:
            cp.wait()

    return pl.pallas_call(
        body, in_specs=[ANY] * n, out_specs=[ANY] * n,
        out_shape=[SDS(p.shape, p.dtype) for p in ps],
        scratch_shapes=[pltpu.SemaphoreType.DMA((n, 3)), pltpu.SemaphoreType.DMA((n, 3)), pltpu.SemaphoreType.DMA((n,))],
        name="reduce_scatter_chips")(*ps)


def _adam_math(g, w, m, v):
    m_new = ADAM_B1 * m + (1.0 - ADAM_B1) * g
    v_new = ADAM_B2 * v + (1.0 - ADAM_B2) * (g * g)
    m_hat = m_new / (1.0 - ADAM_B1 ** ADAM_STEP)
    v_hat = v_new / (1.0 - ADAM_B2 ** ADAM_STEP)
    delta = -ADAM_LR * (m_hat / (jnp.sqrt(v_hat) + ADAM_EPS) + ADAM_WD * w)
    return delta, m_new, v_new


def _sum_adam(parts, w, m, v, name):
    rows, cols = w.shape
    tr = _row_tile(rows, cols, 4 * parts.dtype.itemsize + 7 * 4)

    def body(p_ref, w_ref, m_ref, v_ref, g_ref, d_ref, mo_ref, vo_ref):
        g = p_ref[0].astype(F32)
        for s in range(1, 4):
            g = g + p_ref[s].astype(F32)
        delta, m_new, v_new = _adam_math(g, w_ref[...], m_ref[...], v_ref[...])
        g_ref[...] = g
        d_ref[...] = delta
        mo_ref[...] = m_new
        vo_ref[...] = v_new

    blk = pl.BlockSpec((tr, cols), lambda i: (i, 0))
    out = SDS((rows, cols), F32)
    return pl.pallas_call(
        body, grid=(rows // tr,),
        in_specs=[pl.BlockSpec((4, tr, cols), lambda i: (0, i, 0)), blk, blk, blk],
        out_specs=[blk, blk, blk, blk], out_shape=[out, out, out, out],
        compiler_params=_params("parallel"), name=name)(parts, w, m, v)


SMALL_ROWS = 64


def _small_allreduce_adam(gpart, w, m, v):
    def body(g_ref, w_ref, m_ref, v_ref, go_ref, d_ref, mo_ref, vo_ref, gath, send_sems, recv_sems):
        x, y, c = lax.axis_index("x"), lax.axis_index("y"), lax.axis_index("c")
        me = 4 * x + 2 * y + c
        gath[me] = g_ref[...]
        copies = []
        for k in range(1, N_DEV):
            fx, fy, fc = (k >> 2) & 1, (k >> 1) & 1, k & 1
            peer = (x ^ fx, y ^ fy, c ^ fc)
            copies.append(pltpu.make_async_remote_copy(
                src_ref=gath.at[me], dst_ref=gath.at[me], send_sem=send_sems.at[k - 1], recv_sem=recv_sems.at[k - 1],
                device_id=peer, device_id_type=MESH))
        for cp in copies:
            cp.start()
        for cp in copies:
            cp.wait_recv()
        for cp in copies:
            cp.wait_send()
        g = gath[0]
        for d in range(1, N_DEV):
            g = g + gath[d]
        delta, m_new, v_new = _adam_math(g, w_ref[...], m_ref[...], v_ref[...])
        go_ref[...] = g
        d_ref[...] = delta
        mo_ref[...] = m_new
        vo_ref[...] = v_new

    vm = pl.BlockSpec(memory_space=pltpu.VMEM)
    out = SDS((SMALL_ROWS, 128), F32)
    return pl.pallas_call(
        body, in_specs=[vm] * 4, out_specs=[vm] * 4, out_shape=[out] * 4,
        scratch_shapes=[pltpu.VMEM((N_DEV, SMALL_ROWS, 128), F32), pltpu.SemaphoreType.DMA((N_DEV - 1,)),
                        pltpu.SemaphoreType.DMA((N_DEV - 1,))],
        name="small_allreduce_adam")(gpart, w, m, v)


BIG = ("w_in", "conv_w", "w_conv_out", "w_attn_out", "w_o", "w_ffn_gate", "w_ffn_up", "w_ffn_down")
COL_SHARDED = ("w_in", "conv_w", "w_attn_out", "w_ffn_gate", "w_ffn_up")
SMALL = ("norm1_g", "gate_b", "conv_b", "conv_ln_g", "conv_ln_b", "norm2_g", "norm_f_g")
WEIGHTS = ("norm1_g", "w_in", "gate_b", "conv_w", "conv_b", "conv_ln_g", "conv_ln_b", "w_conv_out", "w_attn_out", "w_o",
           "norm2_g", "w_ffn_gate", "w_ffn_up", "w_ffn_down", "norm_f_g")


def _shard2d(name, a):
    a = a.reshape(a.shape[-2], a.shape[-1])
    if name == "conv_w":
        a = jnp.pad(a, ((0, CONV_PAD - CONV_K), (0, 0)))
    return a


def _gathered_to_full(name, g):
    if name in COL_SHARDED:
        return g.transpose(1, 0, 2).reshape(g.shape[1], N_DEV * g.shape[2])
    return g.reshape(N_DEV * g.shape[1], g.shape[2])


def _full_to_blocks(name, g):
    if name in COL_SHARDED:
        return g.reshape(g.shape[0], N_DEV, g.shape[1] // N_DEV).transpose(1, 0, 2)
    return g.reshape(N_DEV, g.shape[0] // N_DEV, g.shape[1])


def _pack_small(d):
    return jnp.concatenate([d[n].reshape(-1) for n in SMALL]).reshape(SMALL_ROWS, 128)


def _unpack_small(p, like):
    flat = p.reshape(-1)
    out, off = {}, 0
    for n in SMALL:
        size = like[n].size
        out[n] = flat[off:off + size].reshape(like[n].shape)
        off += size
    return out


def kernel(x, norm1_g, w_in, gate_b, conv_w, conv_b, conv_ln_g, conv_ln_b, w_conv_out, w_attn_out, w_o, norm2_g, w_ffn_gate, w_ffn_up, w_ffn_down, norm_f_g, loss_target, m_norm1_g, m_w_in, m_gate_b, m_conv_w, m_conv_b, m_conv_ln_g, m_conv_ln_b, m_w_conv_out, m_w_attn_out, m_w_o, m_norm2_g, m_w_ffn_gate, m_w_ffn_up, m_w_ffn_down, m_norm_f_g, v_norm1_g, v_w_in, v_gate_b, v_conv_w, v_conv_b, v_conv_ln_g, v_conv_ln_b, v_w_conv_out, v_w_attn_out, v_w_o, v_norm2_g, v_w_ffn_gate, v_w_ffn_up, v_w_ffn_down, v_norm_f_g):
    wts = dict(norm1_g=norm1_g, w_in=w_in, gate_b=gate_b, conv_w=conv_w, conv_b=conv_b, conv_ln_g=conv_ln_g,
               conv_ln_b=conv_ln_b, w_conv_out=w_conv_out, w_attn_out=w_attn_out, w_o=w_o, norm2_g=norm2_g,
               w_ffn_gate=w_ffn_gate, w_ffn_up=w_ffn_up, w_ffn_down=w_ffn_down, norm_f_g=norm_f_g)
    mom1 = dict(norm1_g=m_norm1_g, w_in=m_w_in, gate_b=m_gate_b, conv_w=m_conv_w, conv_b=m_conv_b, conv_ln_g=m_conv_ln_g,
                conv_ln_b=m_conv_ln_b, w_conv_out=m_w_conv_out, w_attn_out=m_w_attn_out, w_o=m_w_o, norm2_g=m_norm2_g,
                w_ffn_gate=m_w_ffn_gate, w_ffn_up=m_w_ffn_up, w_ffn_down=m_w_ffn_down, norm_f_g=m_norm_f_g)
    mom2 = dict(norm1_g=v_norm1_g, w_in=v_w_in, gate_b=v_gate_b, conv_w=v_conv_w, conv_b=v_conv_b, conv_ln_g=v_conv_ln_g,
                conv_ln_b=v_conv_ln_b, w_conv_out=v_w_conv_out, w_attn_out=v_w_attn_out, w_o=v_w_o, norm2_g=v_norm2_g,
                w_ffn_gate=v_w_ffn_gate, w_ffn_up=v_w_ffn_up, w_ffn_down=v_w_ffn_down, norm_f_g=v_norm_f_g)

    T = x.shape[0] * x.shape[1]
    x2 = x.reshape(T, D_MODEL)
    t2 = loss_target.reshape(T, D_MODEL)

    shards = {n: _shard2d(n, wts[n]) for n in BIG}
    send = [shards[n] if n == "conv_w" else shards[n].astype(BF16) for n in BIG]
    gathered = _all_gather(send)
    full = {n: _gathered_to_full(n, g) for n, g in zip(BIG, gathered)}
    small = {n: wts[n].reshape(1, -1) for n in SMALL}

    loss_part, grad_x, gw, gsmall = _local_step(x2, t2, full, small)

    blocks = [_full_to_blocks(n, gw[n]) for n in BIG]
    core = lax.axis_index("c").astype(jnp.int32).reshape(1)
    from_sibling = _exchange_sibling(blocks)
    chip_sums = [_add_pair(g, r, core, "chip_sum_" + n) for n, g, r in zip(BIG, blocks, from_sibling)]
    by_chip = _exchange_chips(chip_sums)

    grads, deltas, new_m, new_v = {}, {}, {}, {}
    for n, parts in zip(BIG, by_chip):
        g, d, mo, vo = _sum_adam(parts, shards[n], _shard2d(n, mom1[n]), _shard2d(n, mom2[n]), "adam_" + n)
        for dst, val in ((grads, g), (deltas, d), (new_m, mo), (new_v, vo)):
            if n == "conv_w":
                val = val[:CONV_K]
            dst[n] = val.reshape(wts[n].shape)

    sg, sd, sm, sv = _small_allreduce_adam(_pack_small(gsmall), _pack_small(wts), _pack_small(mom1), _pack_small(mom2))
    for dst, val in ((grads, sg), (deltas, sd), (new_m, sm), (new_v, sv)):
        dst.update(_unpack_small(val, wts))

    loss = lax.psum(loss_part[0, 0], ("x", "y", "c"))
    return (loss, grad_x.reshape(x.shape), *[grads[n] for n in WEIGHTS], *[deltas[n] for n in WEIGHTS],
            *[new_m[n] for n in WEIGHTS], *[new_v[n] for n in WEIGHTS])
```

```python
import math

import numpy as np
import jax
import jax.numpy as jnp
from jax import lax
from jax.experimental import pallas as pl
from jax.experimental.pallas import tpu as pltpu

F32 = jnp.float32
BF16 = jnp.bfloat16
SDS = jax.ShapeDtypeStruct
MESH = pl.DeviceIdType.MESH

D_MODEL = 1024
SEQ = 2048
HEAD_DIM = 64
GROUPS = ((128, 1), (512, 4), (2048, 16))
HEADS_PER_GROUP = 8
N_HEADS = 24
ATTN_WIDTH = N_HEADS * HEAD_DIM
ATTN_OUT = HEADS_PER_GROUP * HEAD_DIM
CONV_K = 31
CONV_PAD = 32
D_FF = 2816
IN_WIDTH = 3 * ATTN_WIDTH + 2 * D_MODEL + 2 * D_MODEL
RMS_EPS = 1e-6
LN_EPS = 1e-5
Q_BLOCK = 128
LANES = 128
NEG = -1e30
N_DEV = 8

ADAM_LR = 0.001
ADAM_B1 = 0.9
ADAM_B2 = 0.999
ADAM_EPS = 1e-08
ADAM_WD = 0.01
ADAM_STEP = 10


def _alibi_slope_list(n):
    def pow2(m):
        start = 2.0 ** (-8.0 / m)
        return [start ** (i + 1) for i in range(m)]
    if math.log2(n).is_integer():
        return pow2(n)
    c = 2 ** math.floor(math.log2(n))
    return pow2(c) + _alibi_slope_list(2 * c)[0::2][: n - c]


def _slopes_times_dilation():
    s = np.asarray(sorted(_alibi_slope_list(N_HEADS), reverse=True), dtype=np.float32).reshape(3, HEADS_PER_GROUP)
    r = np.asarray([g[1] for g in GROUPS], dtype=np.float32)[:, None]
    return (s * r).reshape(N_HEADS)


def _sigmoid(x):
    return 1.0 / (1.0 + jnp.exp(-x))


def _dot(a, b):
    return jnp.dot(a, b, preferred_element_type=F32)


def _dot_nt(a, b):
    return lax.dot_general(a, b, (((1,), (1,)), ((), ())), preferred_element_type=F32)


def _dot_tn(a, b):
    return lax.dot_general(a, b, (((0,), (0,)), ((), ())), preferred_element_type=F32)


def _rowsum(x):
    return jnp.sum(x, axis=0, keepdims=True)


def _params(*sem):
    return pltpu.CompilerParams(dimension_semantics=sem)


IN_TM = 256
IN_CHUNK = 512


def _in_proj(x, g1, w_in):
    T = x.shape[0]
    tm = IN_TM
    widths = (3 * ATTN_WIDTH, 2 * D_MODEL, 2 * D_MODEL)

    def body(x_ref, g_ref, w_hbm, h_ref, qkv_ref, u_ref, lg_ref, w_vmem, sem):
        @pl.when(pl.program_id(0) == 0)
        def _():
            cp = pltpu.make_async_copy(w_hbm, w_vmem, sem)
            cp.start()
            cp.wait()

        xv = x_ref[...]
        r = lax.rsqrt(jnp.mean(xv * xv, axis=-1, keepdims=True) + RMS_EPS)
        h = (xv * r * g_ref[...]).astype(BF16)
        h_ref[...] = h
        col = 0
        for o_ref, width in zip((qkv_ref, u_ref, lg_ref), widths):
            for j in range(width // IN_CHUNK):
                o_ref[:, j * IN_CHUNK:(j + 1) * IN_CHUNK] = _dot(h, w_vmem[:, col:col + IN_CHUNK])
                col += IN_CHUNK

    row = lambda n: pl.BlockSpec((tm, n), lambda i: (i, 0))
    return pl.pallas_call(
        body, grid=(T // tm,),
        in_specs=[row(D_MODEL), pl.BlockSpec((1, D_MODEL), lambda i: (0, 0)), pl.BlockSpec(memory_space=pl.ANY)],
        out_specs=[row(D_MODEL)] + [row(n) for n in widths],
        out_shape=[SDS((T, D_MODEL), BF16)] + [SDS((T, n), F32) for n in widths],
        scratch_shapes=[pltpu.VMEM((D_MODEL, IN_WIDTH), BF16), pltpu.SemaphoreType.DMA],
        compiler_params=_params("arbitrary"), name="in_proj")(x, g1, w_in)


def _mm_tn(a, b, out_dtype, name, tn, tt=512):
    T, K = a.shape
    N = b.shape[1]
    nt = T // tt

    def body(a_ref, b_ref, o_ref, acc):
        t = pl.program_id(1)
        p = _dot_tn(a_ref[...], b_ref[...])

        @pl.when(t == 0)
        def _():
            acc[...] = p

        @pl.when(t > 0)
        def _():
            acc[...] += p

        @pl.when(t == nt - 1)
        def _():
            o_ref[...] = acc[...].astype(o_ref.dtype)

    return pl.pallas_call(
        body, grid=(N // tn, nt),
        in_specs=[pl.BlockSpec((tt, K), lambda j, t: (t, 0)),
                  pl.BlockSpec((tt, tn), lambda j, t: (t, j))],
        out_specs=pl.BlockSpec((K, tn), lambda j, t: (0, j)),
        out_shape=SDS((K, N), out_dtype),
        scratch_shapes=[pltpu.VMEM((K, tn), F32)],
        compiler_params=_params("parallel", "arbitrary"), name=name)(a, b)


def _gather_classes(src_ref, dst, r):
    L = SEQ // r
    for c in range(r):
        dst[c * L:(c + 1) * L, :] = src_ref[0, pl.ds(c, L, stride=r), :].astype(dst.dtype)


def _scatter_classes(src, dst, r):
    L = SEQ // r
    for c in range(r):
        dst[pl.ds(c, L, stride=r), :] = src[c * L:(c + 1) * L, :].astype(dst.dtype)


def _attn_masks(slope_r):
    qi = lax.broadcasted_iota(jnp.int32, (Q_BLOCK, Q_BLOCK), 0)
    kj = lax.broadcasted_iota(jnp.int32, (Q_BLOCK, Q_BLOCK), 1)
    rel = (qi - kj).astype(F32)
    bias_cur = jnp.where(qi >= kj, -slope_r * rel, NEG)
    bias_prev = jnp.where(qi <= kj, -slope_r * (rel + float(Q_BLOCK)), NEG)
    return bias_cur, bias_prev


def _store_biases(bias, sl_ref, g, hp):
    for hh in range(2):
        cur, prev = _attn_masks(sl_ref[g * HEADS_PER_GROUP + 2 * hp + hh])
        bias[0, hh * Q_BLOCK:(hh + 1) * Q_BLOCK, :] = cur
        bias[1, hh * Q_BLOCK:(hh + 1) * Q_BLOCK, :] = prev


def _transpose_blocks(src, dst):
    for b in range(SEQ // Q_BLOCK):
        dst[b] = src[b * Q_BLOCK:(b + 1) * Q_BLOCK, :].T


def _stack_heads(t, low):
    z = jnp.zeros_like(t)
    return jnp.concatenate([jnp.where(low, t, z), jnp.where(low, z, t)], axis=0)


def _unstack_heads(t2, low):
    return jnp.where(low, t2[:Q_BLOCK], t2[Q_BLOCK:])


def _unit_offsets(u, nb):
    off = pl.multiple_of(u * Q_BLOCK, Q_BLOCK)
    offp = pl.multiple_of(jnp.maximum(u - 1, 0) * Q_BLOCK, Q_BLOCK)
    n = u & (nb - 1)
    c = u >> int(math.log2(nb))
    return off, offp, n == 0, c, n


def _attn_fwd(qkv, slopes_r, batch):
    nblk = SEQ // Q_BLOCK

    def body(sl_ref, *refs):
        qkv_refs = refs[:9]
        att_ref, lse_ref = refs[9:11]
        qd, kd, vd, kt, opos, lpos, bias = refs[11:]
        hp = pl.program_id(1)
        low = lax.broadcasted_iota(jnp.int32, (Q_BLOCK, LANES), 1) < HEAD_DIM

        for g in range(3):
            r = GROUPS[g][1]
            nb = SEQ // r // Q_BLOCK
            _gather_classes(qkv_refs[3 * g], qd, r)
            _gather_classes(qkv_refs[3 * g + 1], kd, r)
            _gather_classes(qkv_refs[3 * g + 2], vd, r)
            _transpose_blocks(kd, kt)
            _store_biases(bias, sl_ref, g, hp)

            def unit(u, carry, g=g, r=r, nb=nb):
                off, offp, first, c, n = _unit_offsets(u, nb)
                q2 = _stack_heads(qd[pl.ds(off, Q_BLOCK), :], low)
                vc = vd[pl.ds(off, Q_BLOCK), :]
                vp = vd[pl.ds(offp, Q_BLOCK), :]
                sc = _dot(q2, kt[u]) * 0.125 + bias[0]
                sp = jnp.where(first, NEG, _dot(q2, kt[jnp.maximum(u - 1, 0)]) * 0.125 + bias[1])
                m = jnp.max(jnp.maximum(sc, sp), axis=-1, keepdims=True)
                pc = jnp.exp(sc - m)
                pp = jnp.exp(sp - m)
                l = jnp.sum(pc + pp, axis=-1, keepdims=True)
                o2 = (_dot(pc.astype(BF16), vc) + _dot(pp.astype(BF16), vp)) * (1.0 / l)
                lse2 = m + jnp.log(l)
                rows = pl.ds(c + n * (Q_BLOCK * r), Q_BLOCK, stride=r)
                opos[g, rows, :] = _unstack_heads(o2, low)
                lpos[g, rows, :] = jnp.where(low, lse2[:Q_BLOCK], lse2[Q_BLOCK:])
                return carry

            lax.fori_loop(0, nblk, unit, 0, unroll=2)

        def merge(i, carry):
            rows = pl.ds(pl.multiple_of(i * 256, 256), 256)
            l0, l1, l2 = lpos[0, rows, :], lpos[1, rows, :], lpos[2, rows, :]
            m = jnp.maximum(jnp.maximum(l0, l1), l2)
            e0, e1, e2 = jnp.exp(l0 - m), jnp.exp(l1 - m), jnp.exp(l2 - m)
            den = e0 + e1 + e2
            att = (e0 * opos[0, rows, :] + e1 * opos[1, rows, :] + e2 * opos[2, rows, :]) / den
            att_ref[0, rows, :] = att.astype(att_ref.dtype)
            lse_ref[0, rows, :] = m + jnp.log(den)
            return carry

        lax.fori_loop(0, SEQ // 256, merge, 0)

    def col(sec, g):
        return pl.BlockSpec((1, SEQ, LANES), lambda b, hp: (b, 0, sec * 12 + g * 4 + hp))

    out = pl.BlockSpec((1, SEQ, LANES), lambda b, hp: (b, 0, hp))
    return pl.pallas_call(
        body, grid=(batch, 4),
        in_specs=[pl.BlockSpec(memory_space=pltpu.SMEM)] + [col(sec, g) for g in range(3) for sec in range(3)],
        out_specs=[out, out],
        out_shape=[SDS((batch, SEQ, ATTN_OUT), BF16), SDS((batch, SEQ, ATTN_OUT), F32)],
        scratch_shapes=[pltpu.VMEM((SEQ, LANES), BF16), pltpu.VMEM((SEQ, LANES), BF16), pltpu.VMEM((SEQ, LANES), BF16),
                        pltpu.VMEM((nblk, LANES, Q_BLOCK), BF16),
                        pltpu.VMEM((3, SEQ, LANES), F32), pltpu.VMEM((3, SEQ, LANES), F32),
                        pltpu.VMEM((2, 2 * Q_BLOCK, Q_BLOCK), F32)],
        compiler_params=_params("parallel", "parallel"), name="attn_fwd")(slopes_r, *([qkv] * 9))


def _attn_bwd(qkv, datt, lse, dsum, slopes_r, batch):
    nblk = SEQ // Q_BLOCK

    def body(sl_ref, q_ref, k_ref, v_ref, do_ref, l_ref, d_ref, dq_ref, dk_ref, dv_ref,
             qd, kd, vd, dod, kt, vt, ld, dd, dq_acc, dk_acc, dv_acc, stage, bias):
        gid, hp = pl.program_id(1), pl.program_id(2)
        low = lax.broadcasted_iota(jnp.int32, (Q_BLOCK, LANES), 1) < HEAD_DIM

        def section(g):
            r = GROUPS[g][1]
            nb = SEQ // r // Q_BLOCK
            _gather_classes(q_ref, qd, r)
            _gather_classes(k_ref, kd, r)
            _gather_classes(v_ref, vd, r)
            _gather_classes(do_ref, dod, r)
            _gather_classes(l_ref, ld, r)
            _gather_classes(d_ref, dd, r)
            _transpose_blocks(kd, kt)
            _transpose_blocks(vd, vt)
            _store_biases(bias, sl_ref, g, hp)
            dk_acc[...] = jnp.zeros_like(dk_acc)
            dv_acc[...] = jnp.zeros_like(dv_acc)

            def unit(u, carry):
                off, offp, first, _, _ = _unit_offsets(u, nb)
                up = jnp.maximum(u - 1, 0)
                q2 = _stack_heads(qd[pl.ds(off, Q_BLOCK), :], low)
                do2 = _stack_heads(dod[pl.ds(off, Q_BLOCK), :], low)
                kc = kd[pl.ds(off, Q_BLOCK), :]
                kp = kd[pl.ds(offp, Q_BLOCK), :]
                lse_t = ld[pl.ds(off, Q_BLOCK), :]
                dsum_t = dd[pl.ds(off, Q_BLOCK), :]
                lse2 = jnp.concatenate([lse_t[:, 0:1], lse_t[:, HEAD_DIM:HEAD_DIM + 1]], axis=0)
                dsum2 = jnp.concatenate([dsum_t[:, 0:1], dsum_t[:, HEAD_DIM:HEAD_DIM + 1]], axis=0)
                sc = _dot(q2, kt[u]) * 0.125 + bias[0]
                sp = jnp.where(first, NEG, _dot(q2, kt[up]) * 0.125 + bias[1])
                pc = jnp.exp(sc - lse2)
                pp = jnp.exp(sp - lse2)
                dsc = (pc * (_dot(do2, vt[u]) - dsum2)).astype(BF16)
                dsp = (pp * (_dot(do2, vt[up]) - dsum2)).astype(BF16)
                dq2 = _dot(dsc, kc) + _dot(dsp, kp)
                dq_acc[pl.ds(off, Q_BLOCK), :] = _unstack_heads(dq2, low) * 0.125
                dk_acc[pl.ds(off, Q_BLOCK), :] += _dot_tn(dsc, q2) * 0.125
                dk_acc[pl.ds(offp, Q_BLOCK), :] += _dot_tn(dsp, q2) * 0.125
                dv_acc[pl.ds(off, Q_BLOCK), :] += _dot_tn(pc.astype(BF16), do2)
                dv_acc[pl.ds(offp, Q_BLOCK), :] += _dot_tn(pp.astype(BF16), do2)
                return carry

            lax.fori_loop(0, nblk, unit, 0, unroll=2)
            for acc, out_ref in ((dq_acc, dq_ref), (dk_acc, dk_ref), (dv_acc, dv_ref)):
                _scatter_classes(acc, stage, r)
                out_ref[0] = stage[...].astype(out_ref.dtype)

        for g in range(3):
            pl.when(gid == g)(lambda g=g: section(g))

    def col(sec):
        return pl.BlockSpec((1, SEQ, LANES), lambda b, g, hp: (b, 0, sec * 12 + g * 4 + hp))

    pos = pl.BlockSpec((1, SEQ, LANES), lambda b, g, hp: (b, 0, hp))
    dout = pl.BlockSpec((1, SEQ, LANES), lambda b, g, hp: (b, 0, g * 4 + hp))
    out = SDS((batch, SEQ, ATTN_WIDTH), BF16)
    seq_bf = pltpu.VMEM((SEQ, LANES), BF16)
    seq_f = pltpu.VMEM((SEQ, LANES), F32)
    blk_t = pltpu.VMEM((nblk, LANES, Q_BLOCK), BF16)
    return pl.pallas_call(
        body, grid=(batch, 3, 4),
        in_specs=[pl.BlockSpec(memory_space=pltpu.SMEM), col(0), col(1), col(2), pos, pos, pos],
        out_specs=[dout, dout, dout],
        out_shape=[out, out, out],
        scratch_shapes=[seq_bf, seq_bf, seq_bf, seq_bf, blk_t, blk_t, seq_f, seq_f, seq_f, seq_f, seq_f, seq_f,
                        pltpu.VMEM((2, 2 * Q_BLOCK, Q_BLOCK), F32)],
        compiler_params=_params("parallel", "parallel", "parallel"), name="attn_bwd")(
            slopes_r, qkv, qkv, qkv, datt, lse, dsum)


CONV_TC = 256
CONV_ROWS = 64


def _conv_fwd(u, conv_w, conv_b, batch):
    nct = D_MODEL // CONV_TC

    def body(ua_ref, ub_ref, w_ref, b_ref, o_ref, pad):
        pad[0:CONV_PAD, :] = jnp.zeros((CONV_PAD, CONV_TC), F32)
        pad[CONV_PAD:, :] = ua_ref[0] * _sigmoid(ub_ref[0])

        def chunk(c, carry):
            base = pl.multiple_of(c * CONV_ROWS, CONV_ROWS)
            win = pad[pl.ds(base, CONV_ROWS + CONV_PAD), :]
            acc = jnp.broadcast_to(b_ref[...], (CONV_ROWS, CONV_TC))
            for t in range(CONV_K):
                s = t + CONV_PAD - (CONV_K - 1)
                acc = acc + win[s:s + CONV_ROWS, :] * w_ref[t:t + 1, :]
            o_ref[0, pl.ds(base, CONV_ROWS), :] = acc
            return carry

        lax.fori_loop(0, SEQ // CONV_ROWS, chunk, 0)

    return pl.pallas_call(
        body, grid=(nct, batch),
        in_specs=[pl.BlockSpec((1, SEQ, CONV_TC), lambda j, b: (b, 0, j)),
                  pl.BlockSpec((1, SEQ, CONV_TC), lambda j, b: (b, 0, j + nct)),
                  pl.BlockSpec((CONV_PAD, CONV_TC), lambda j, b: (0, j)),
                  pl.BlockSpec((1, CONV_TC), lambda j, b: (0, j))],
        out_specs=pl.BlockSpec((1, SEQ, CONV_TC), lambda j, b: (b, 0, j)),
        out_shape=SDS((batch, SEQ, D_MODEL), F32),
        scratch_shapes=[pltpu.VMEM((SEQ + CONV_PAD, CONV_TC), F32)],
        compiler_params=_params("parallel", "parallel"), name="conv_fwd")(u, u, conv_w, conv_b)


def _conv_bwd(u, dc1, conv_w, batch):
    nct = D_MODEL // CONV_TC
    nchunk = SEQ // CONV_ROWS

    def body(ua_ref, ub_ref, d_ref, w_ref, dua_ref, dub_ref, gw_ref, gb_ref, padc, padd, gacc):
        b = pl.program_id(1)
        sig = _sigmoid(ub_ref[0])
        padc[0:CONV_PAD, :] = jnp.zeros((CONV_PAD, CONV_TC), F32)
        padc[CONV_PAD:, :] = ua_ref[0] * sig
        padd[0:SEQ, :] = d_ref[0]
        padd[SEQ:, :] = jnp.zeros((CONV_PAD, CONV_TC), F32)

        @pl.when(b == 0)
        def _():
            gacc[...] = jnp.zeros_like(gacc)
            gb_ref[...] = jnp.zeros_like(gb_ref)

        gb_ref[...] += _rowsum(d_ref[0])

        def chunk(c, carry):
            base = pl.multiple_of(c * CONV_ROWS, CONV_ROWS)
            wind = padd[pl.ds(base, CONV_ROWS + CONV_PAD), :]
            winc = padc[pl.ds(base, CONV_ROWS + CONV_PAD), :]
            dcur = wind[0:CONV_ROWS, :]
            acc = jnp.zeros((CONV_ROWS, CONV_TC), F32)
            for t in range(CONV_K):
                s = CONV_K - 1 - t
                acc = acc + wind[s:s + CONV_ROWS, :] * w_ref[t:t + 1, :]
                sc = t + CONV_PAD - (CONV_K - 1)
                prod = winc[sc:sc + CONV_ROWS, :] * dcur
                gacc[t] += jnp.sum(prod.reshape(CONV_ROWS // 8, 8, CONV_TC), axis=0)
            ua = ua_ref[0, pl.ds(base, CONV_ROWS), :]
            sg = _sigmoid(ub_ref[0, pl.ds(base, CONV_ROWS), :])
            dua_ref[0, pl.ds(base, CONV_ROWS), :] = (acc * sg).astype(dua_ref.dtype)
            dub_ref[0, pl.ds(base, CONV_ROWS), :] = (acc * ua * sg * (1.0 - sg)).astype(dub_ref.dtype)
            return carry

        lax.fori_loop(0, nchunk, chunk, 0)

        @pl.when(b == batch - 1)
        def _():
            for t in range(CONV_K):
                gw_ref[t:t + 1, :] = jnp.sum(gacc[t], axis=0, keepdims=True)
            gw_ref[CONV_K:CONV_PAD, :] = jnp.zeros((CONV_PAD - CONV_K, CONV_TC), F32)

    du = SDS((batch, SEQ, D_MODEL), BF16)
    return pl.pallas_call(
        body, grid=(nct, batch),
        in_specs=[pl.BlockSpec((1, SEQ, CONV_TC), lambda j, b: (b, 0, j)),
                  pl.BlockSpec((1, SEQ, CONV_TC), lambda j, b: (b, 0, j + nct)),
                  pl.BlockSpec((1, SEQ, CONV_TC), lambda j, b: (b, 0, j)),
                  pl.BlockSpec((CONV_PAD, CONV_TC), lambda j, b: (0, j))],
        out_specs=[pl.BlockSpec((1, SEQ, CONV_TC), lambda j, b: (b, 0, j)),
                   pl.BlockSpec((1, SEQ, CONV_TC), lambda j, b: (b, 0, j)),
                   pl.BlockSpec((CONV_PAD, CONV_TC), lambda j, b: (0, j)),
                   pl.BlockSpec((1, CONV_TC), lambda j, b: (0, j))],
        out_shape=[du, du, SDS((CONV_PAD, D_MODEL), F32), SDS((1, D_MODEL), F32)],
        scratch_shapes=[pltpu.VMEM((SEQ + CONV_PAD, CONV_TC), F32), pltpu.VMEM((SEQ + CONV_PAD, CONV_TC), F32),
                        pltpu.VMEM((CONV_K, 8, CONV_TC), F32)],
        compiler_params=_params("parallel", "arbitrary"), name="conv_bwd")(u, u, dc1, conv_w)


MID_TM = 256


def _layernorm_stats(c1):
    mu = jnp.mean(c1, axis=-1, keepdims=True)
    cen = c1 - mu
    rs = lax.rsqrt(jnp.mean(cen * cen, axis=-1, keepdims=True) + LN_EPS)
    return cen * rs, rs


def _mid_fwd(att, c1, logits, x, w_a, w_c, w_o, gate_b, ln_g, ln_b, g2):
    T = x.shape[0]
    tm = MID_TM

    def body(att_ref, c1_ref, lg_ref, x_ref, wa_ref, wc_ref, wo_ref, gb_ref, lng_ref, lnb_ref, g2_ref,
             c3_ref, ya_ref, yc_ref, mix_ref, x1_ref, h2_ref):
        ya = _dot(att_ref[...], wa_ref[...])
        xh, _ = _layernorm_stats(c1_ref[...])
        c2 = xh * lng_ref[...] + lnb_ref[...]
        c3 = (c2 * _sigmoid(c2)).astype(BF16)
        c3_ref[...] = c3
        yc = _dot(c3, wc_ref[...])
        gates = _sigmoid(lg_ref[...] + gb_ref[...])
        mix = (gates[:, :D_MODEL] * ya + gates[:, D_MODEL:] * yc).astype(BF16)
        ya_ref[...] = ya.astype(BF16)
        yc_ref[...] = yc.astype(BF16)
        mix_ref[...] = mix
        x1 = x_ref[...] + _dot(mix, wo_ref[...])
        x1_ref[...] = x1
        r = lax.rsqrt(jnp.mean(x1 * x1, axis=-1, keepdims=True) + RMS_EPS)
        h2_ref[...] = (x1 * r * g2_ref[...]).astype(BF16)

    row = lambda n: pl.BlockSpec((tm, n), lambda i: (i, 0))
    full = lambda a, b: pl.BlockSpec((a, b), lambda i: (0, 0))
    return pl.pallas_call(
        body, grid=(T // tm,),
        in_specs=[row(ATTN_OUT), row(D_MODEL), row(2 * D_MODEL), row(D_MODEL),
                  full(ATTN_OUT, D_MODEL), full(D_MODEL, D_MODEL), full(D_MODEL, D_MODEL),
                  full(1, 2 * D_MODEL), full(1, D_MODEL), full(1, D_MODEL), full(1, D_MODEL)],
        out_specs=[row(D_MODEL), row(D_MODEL), row(D_MODEL), row(D_MODEL), row(D_MODEL), row(D_MODEL)],
        out_shape=[SDS((T, D_MODEL), BF16), SDS((T, D_MODEL), BF16), SDS((T, D_MODEL), BF16), SDS((T, D_MODEL), BF16),
                   SDS((T, D_MODEL), F32), SDS((T, D_MODEL), BF16)],
        compiler_params=_params("parallel"), name="mid_fwd")(att, c1, logits, x, w_a, w_c, w_o, gate_b, ln_g, ln_b, g2)


def _mid_bwd(dx1b, ya, yc, logits, att, c1, w_a, w_c, w_o, gate_b, ln_g, ln_b, head_ones):
    T = dx1b.shape[0]
    tm = MID_TM

    def body(dx_ref, ya_ref, yc_ref, lg_ref, att_ref, c1_ref, wa_ref, wc_ref, wo_ref, gb_ref, lng_ref, lnb_ref, e_ref,
             dlg_ref, dya_ref, dyc_ref, datt_ref, dsum_ref, dc1_ref, ggb_ref, glg_ref, glb_ref):
        @pl.when(pl.program_id(0) == 0)
        def _():
            ggb_ref[...] = jnp.zeros_like(ggb_ref)
            glg_ref[...] = jnp.zeros_like(glg_ref)
            glb_ref[...] = jnp.zeros_like(glb_ref)

        dmix = _dot_nt(dx_ref[...], wo_ref[...])
        gates = _sigmoid(lg_ref[...] + gb_ref[...])
        ga, gc = gates[:, :D_MODEL], gates[:, D_MODEL:]
        dla = dmix * ya_ref[...].astype(F32) * ga * (1.0 - ga)
        dlc = dmix * yc_ref[...].astype(F32) * gc * (1.0 - gc)
        dlg_ref[:, :D_MODEL] = dla.astype(BF16)
        dlg_ref[:, D_MODEL:] = dlc.astype(BF16)
        ggb_ref[:, :D_MODEL] += _rowsum(dla)
        ggb_ref[:, D_MODEL:] += _rowsum(dlc)
        dya = (dmix * ga).astype(BF16)
        dyc = (dmix * gc).astype(BF16)
        dya_ref[...] = dya
        dyc_ref[...] = dyc
        datt = _dot_nt(dya, wa_ref[...])
        datt_ref[...] = datt
        dsum_ref[...] = jnp.dot(datt * att_ref[...].astype(F32), e_ref[...], preferred_element_type=F32,
                                precision=lax.Precision.HIGHEST)
        dc3 = _dot_nt(dyc, wc_ref[...])
        xh, rs = _layernorm_stats(c1_ref[...])
        c2 = xh * lng_ref[...] + lnb_ref[...]
        sg = _sigmoid(c2)
        dc2 = dc3 * (sg * (1.0 + c2 * (1.0 - sg)))
        glg_ref[...] += _rowsum(dc2 * xh)
        glb_ref[...] += _rowsum(dc2)
        dxh = dc2 * lng_ref[...]
        dc1_ref[...] = rs * (dxh - jnp.mean(dxh, axis=-1, keepdims=True) - xh * jnp.mean(dxh * xh, axis=-1, keepdims=True))

    row = lambda n: pl.BlockSpec((tm, n), lambda i: (i, 0))
    full = lambda a, b: pl.BlockSpec((a, b), lambda i: (0, 0))
    return pl.pallas_call(
        body, grid=(T // tm,),
        in_specs=[row(D_MODEL), row(D_MODEL), row(D_MODEL), row(2 * D_MODEL), row(ATTN_OUT), row(D_MODEL),
                  full(ATTN_OUT, D_MODEL), full(D_MODEL, D_MODEL), full(D_MODEL, D_MODEL),
                  full(1, 2 * D_MODEL), full(1, D_MODEL), full(1, D_MODEL), full(ATTN_OUT, ATTN_OUT)],
        out_specs=[row(2 * D_MODEL), row(D_MODEL), row(D_MODEL), row(ATTN_OUT), row(ATTN_OUT), row(D_MODEL),
                   full(1, 2 * D_MODEL), full(1, D_MODEL), full(1, D_MODEL)],
        out_shape=[SDS((T, 2 * D_MODEL), BF16), SDS((T, D_MODEL), BF16), SDS((T, D_MODEL), BF16), SDS((T, ATTN_OUT), F32),
                   SDS((T, ATTN_OUT), F32), SDS((T, D_MODEL), F32),
                   SDS((1, 2 * D_MODEL), F32), SDS((1, D_MODEL), F32), SDS((1, D_MODEL), F32)],
        compiler_params=_params("arbitrary"), name="mid_bwd")(dx1b, ya, yc, logits, att, c1, w_a, w_c, w_o, gate_b, ln_g, ln_b,
                                                               head_ones)


FFN_TM = 512
FFN_TF = D_FF // 2


def _rms_bwd(dy_times_g, xh, r):
    return r * (dy_times_g - xh * jnp.mean(dy_times_g * xh, axis=-1, keepdims=True))


def _ffn_fwd(h2, x1, target, gf, w_g, w_u, w_d):
    T = h2.shape[0]
    tm, tf = FFN_TM, FFN_TF
    nf = D_FF // tf

    def body(h_ref, x1_ref, t_ref, gf_ref, wg_ref, wu_ref, wd_ref,
             a_ref, b_ref, f_ref, dx2_ref, dx2b_ref, loss_ref, gnf_ref, acc):
        i, j = pl.program_id(0), pl.program_id(1)
        h = h_ref[...]
        a = _dot(h, wg_ref[...])
        b = _dot(h, wu_ref[...])
        f = (a * _sigmoid(a) * b).astype(BF16)
        a_ref[...] = a.astype(BF16)
        b_ref[...] = b.astype(BF16)
        f_ref[...] = f
        p = _dot(f, wd_ref[...])

        @pl.when(j == 0)
        def _():
            acc[...] = x1_ref[...] + p

        @pl.when(j > 0)
        def _():
            acc[...] += p

        @pl.when((i == 0) & (j == nf - 1))
        def _():
            loss_ref[...] = jnp.zeros_like(loss_ref)
            gnf_ref[...] = jnp.zeros_like(gnf_ref)

        @pl.when(j == nf - 1)
        def _():
            x2 = acc[...]
            r = lax.rsqrt(jnp.mean(x2 * x2, axis=-1, keepdims=True) + RMS_EPS)
            xh = x2 * r
            err = xh * gf_ref[...] - t_ref[...]
            loss_ref[...] += (0.5 / D_MODEL) * jnp.sum(err * err)
            dy = err * (1.0 / D_MODEL)
            gnf_ref[...] += _rowsum(dy * xh)
            dx2 = _rms_bwd(dy * gf_ref[...], xh, r)
            dx2_ref[...] = dx2
            dx2b_ref[...] = dx2.astype(BF16)

    row = lambda n: pl.BlockSpec((tm, n), lambda i, j: (i, 0))
    ffb = pl.BlockSpec((tm, tf), lambda i, j: (i, j))
    return pl.pallas_call(
        body, grid=(T // tm, nf),
        in_specs=[row(D_MODEL), row(D_MODEL), row(D_MODEL), pl.BlockSpec((1, D_MODEL), lambda i, j: (0, 0)),
                  pl.BlockSpec((D_MODEL, tf), lambda i, j: (0, j)), pl.BlockSpec((D_MODEL, tf), lambda i, j: (0, j)),
                  pl.BlockSpec((tf, D_MODEL), lambda i, j: (j, 0))],
        out_specs=[ffb, ffb, ffb, row(D_MODEL), row(D_MODEL),
                   pl.BlockSpec((1, 128), lambda i, j: (0, 0)), pl.BlockSpec((1, D_MODEL), lambda i, j: (0, 0))],
        out_shape=[SDS((T, D_FF), BF16), SDS((T, D_FF), BF16), SDS((T, D_FF), BF16), SDS((T, D_MODEL), F32),
                   SDS((T, D_MODEL), BF16), SDS((1, 128), F32), SDS((1, D_MODEL), F32)],
        scratch_shapes=[pltpu.VMEM((tm, D_MODEL), F32)],
        compiler_params=_params("arbitrary", "arbitrary"), name="ffn_fwd")(h2, x1, target, gf, w_g, w_u, w_d)


def _ffn_bwd(dx2b, dx2, a, b, x1, g2, w_g, w_u, w_d):
    T = dx2.shape[0]
    tm, tf = FFN_TM, FFN_TF
    nf = D_FF // tf

    def body(dxb_ref, dx2_ref, a_ref, b_ref, x1_ref, g2_ref, wg_ref, wu_ref, wd_ref,
             da_ref, db_ref, dx1_ref, dx1b_ref, gn2_ref, acc):
        i, j = pl.program_id(0), pl.program_id(1)
        df = _dot_nt(dxb_ref[...], wd_ref[...])
        av = a_ref[...].astype(F32)
        bv = b_ref[...].astype(F32)
        sg = _sigmoid(av)
        db = (df * av * sg).astype(BF16)
        da = (df * bv * (sg * (1.0 + av * (1.0 - sg)))).astype(BF16)
        da_ref[...] = da
        db_ref[...] = db
        p = _dot_nt(da, wg_ref[...]) + _dot_nt(db, wu_ref[...])

        @pl.when(j == 0)
        def _():
            acc[...] = p

        @pl.when(j > 0)
        def _():
            acc[...] += p

        @pl.when((i == 0) & (j == nf - 1))
        def _():
            gn2_ref[...] = jnp.zeros_like(gn2_ref)

        @pl.when(j == nf - 1)
        def _():
            dh2 = acc[...]
            x1 = x1_ref[...]
            r = lax.rsqrt(jnp.mean(x1 * x1, axis=-1, keepdims=True) + RMS_EPS)
            xh = x1 * r
            gn2_ref[...] += _rowsum(dh2 * xh)
            dx1 = dx2_ref[...] + _rms_bwd(dh2 * g2_ref[...], xh, r)
            dx1_ref[...] = dx1
            dx1b_ref[...] = dx1.astype(BF16)

    row = lambda n: pl.BlockSpec((tm, n), lambda i, j: (i, 0))
    ffb = pl.BlockSpec((tm, tf), lambda i, j: (i, j))
    return pl.pallas_call(
        body, grid=(T // tm, nf),
        in_specs=[row(D_MODEL), row(D_MODEL), ffb, ffb, row(D_MODEL), pl.BlockSpec((1, D_MODEL), lambda i, j: (0, 0)),
                  pl.BlockSpec((D_MODEL, tf), lambda i, j: (0, j)), pl.BlockSpec((D_MODEL, tf), lambda i, j: (0, j)),
                  pl.BlockSpec((tf, D_MODEL), lambda i, j: (j, 0))],
        out_specs=[ffb, ffb, row(D_MODEL), row(D_MODEL), pl.BlockSpec((1, D_MODEL), lambda i, j: (0, 0))],
        out_shape=[SDS((T, D_FF), BF16), SDS((T, D_FF), BF16), SDS((T, D_MODEL), F32), SDS((T, D_MODEL), BF16),
                   SDS((1, D_MODEL), F32)],
        scratch_shapes=[pltpu.VMEM((tm, D_MODEL), F32)],
        compiler_params=_params("arbitrary", "arbitrary"), name="ffn_bwd")(dx2b, dx2, a, b, x1, g2, w_g, w_u, w_d)


def _in_bwd(pieces, w_in, x, dx1, g1):
    T = x.shape[0]
    tm = IN_TM
    npc = len(pieces)
    assert sum(p.shape[1] for p in pieces) == IN_WIDTH

    def body(*refs):
        p_refs = refs[:npc]
        w_hbm, x_ref, dx1_ref, g_ref, dx_ref, gn1_ref, w_vmem, sem = refs[npc:]

        @pl.when(pl.program_id(0) == 0)
        def _():
            cp = pltpu.make_async_copy(w_hbm, w_vmem, sem)
            cp.start()
            cp.wait()
            gn1_ref[...] = jnp.zeros_like(gn1_ref)

        dh = jnp.zeros((tm, D_MODEL), F32)
        col = 0
        for p_ref in p_refs:
            for j in range(p_ref.shape[1] // IN_CHUNK):
                dh = dh + _dot_nt(p_ref[:, j * IN_CHUNK:(j + 1) * IN_CHUNK], w_vmem[:, col:col + IN_CHUNK])
                col += IN_CHUNK
        xv = x_ref[...]
        r = lax.rsqrt(jnp.mean(xv * xv, axis=-1, keepdims=True) + RMS_EPS)
        xh = xv * r
        gn1_ref[...] += _rowsum(dh * xh)
        dx_ref[...] = dx1_ref[...] + _rms_bwd(dh * g_ref[...], xh, r)

    row = lambda n: pl.BlockSpec((tm, n), lambda i: (i, 0))
    return pl.pallas_call(
        body, grid=(T // tm,),
        in_specs=[row(p.shape[1]) for p in pieces]
        + [pl.BlockSpec(memory_space=pl.ANY), row(D_MODEL), row(D_MODEL), pl.BlockSpec((1, D_MODEL), lambda i: (0, 0))],
        out_specs=[row(D_MODEL), pl.BlockSpec((1, D_MODEL), lambda i: (0, 0))],
        out_shape=[SDS((T, D_MODEL), F32), SDS((1, D_MODEL), F32)],
        scratch_shapes=[pltpu.VMEM((D_MODEL, IN_WIDTH), BF16), pltpu.SemaphoreType.DMA],
        compiler_params=_params("arbitrary"), name="in_bwd")(*pieces, w_in, x, dx1, g1)


def _local_step(x, target, w, small):
    T = x.shape[0]
    batch = T // SEQ
    slopes_r = jnp.asarray(_slopes_times_dilation())

    h, qkv, u, logits = _in_proj(x, small["norm1_g"], w["w_in"])

    qkv3 = qkv.reshape(batch, SEQ, 3 * ATTN_WIDTH)
    att, lse = _attn_fwd(qkv3, slopes_r, batch)
    att = att.reshape(T, ATTN_OUT)

    u3 = u.reshape(batch, SEQ, 2 * D_MODEL)
    c1 = _conv_fwd(u3, w["conv_w"], small["conv_b"], batch).reshape(T, D_MODEL)

    c3, ya, yc, mix, x1, h2 = _mid_fwd(
        att, c1, logits, x, w["w_attn_out"], w["w_conv_out"], w["w_o"],
        small["gate_b"], small["conv_ln_g"], small["conv_ln_b"], small["norm2_g"])

    a, b, f, dx2, dx2b, loss, g_normf = _ffn_fwd(h2, x1, target, small["norm_f_g"],
                                                   w["w_ffn_gate"], w["w_ffn_up"], w["w_ffn_down"])

    da, db, dx1, dx1b, g_norm2 = _ffn_bwd(dx2b, dx2, a, b, x1, small["norm2_g"],
                                           w["w_ffn_gate"], w["w_ffn_up"], w["w_ffn_down"])
    gw = {}
    gw["w_ffn_down"] = _mm_tn(f, dx2b, BF16, "gw_ffn_down", tn=512)
    gw["w_ffn_gate"] = _mm_tn(h2, da, BF16, "gw_ffn_gate", tn=1408)
    gw["w_ffn_up"] = _mm_tn(h2, db, BF16, "gw_ffn_up", tn=1408)

    head_ones = jnp.asarray(np.kron(np.eye(HEADS_PER_GROUP, dtype=np.float32), np.ones((HEAD_DIM, HEAD_DIM), np.float32)))
    dlogits, dya, dyc, datt, dsum, dc1, g_gate_b, g_ln_g, g_ln_b = _mid_bwd(
        dx1b, ya, yc, logits, att, c1, w["w_attn_out"], w["w_conv_out"], w["w_o"],
        small["gate_b"], small["conv_ln_g"], small["conv_ln_b"], head_ones)
    gw["w_o"] = _mm_tn(mix, dx1b, BF16, "gw_o", tn=512)
    gw["w_attn_out"] = _mm_tn(att, dya, BF16, "gw_attn_out", tn=512)
    gw["w_conv_out"] = _mm_tn(c3, dyc, BF16, "gw_conv_out", tn=512)

    dua, dub, g_conv_w, g_conv_b = _conv_bwd(u3, dc1.reshape(batch, SEQ, D_MODEL), w["conv_w"], batch)

    dq, dk, dv = _attn_bwd(qkv3, datt.reshape(batch, SEQ, ATTN_OUT), lse, dsum.reshape(batch, SEQ, ATTN_OUT),
                           slopes_r, batch)
    pieces = [dq.reshape(T, ATTN_WIDTH), dk.reshape(T, ATTN_WIDTH), dv.reshape(T, ATTN_WIDTH),
              dua.reshape(T, D_MODEL), dub.reshape(T, D_MODEL), dlogits]

    grad_x, g_norm1 = _in_bwd(pieces, w["w_in"], x, dx1, small["norm1_g"])
    names = ("q", "k", "v", "ua", "ub", "gate")
    gw["w_in"] = jnp.concatenate(
        [_mm_tn(h, p, BF16, "gw_in_" + nm, tn=min(p.shape[1], 1024) if p.shape[1] != ATTN_WIDTH else 768)
         for nm, p in zip(names, pieces)], axis=1)
    gw["conv_w"] = g_conv_w

    gsmall = {"norm1_g": g_norm1, "gate_b": g_gate_b, "conv_b": g_conv_b, "conv_ln_g": g_ln_g, "conv_ln_b": g_ln_b,
              "norm2_g": g_norm2, "norm_f_g": g_normf}
    return loss, grad_x, gw, gsmall


ANY = pl.BlockSpec(memory_space=pl.ANY)


def _all_gather(arrs):
    n = len(arrs)

    def body(*refs):
        ins, outs = refs[:n], refs[n:2 * n]
        send_sems, recv_sems, local_sems = refs[2 * n:]
        x, y, c = lax.axis_index("x"), lax.axis_index("y"), lax.axis_index("c")
        me, sibling = (x, y, c), (x, y, 1 - c)
        chips = [(1 - x, y), (x, 1 - y), (1 - x, 1 - y)]

        def copy(a, k, block, to, src=None):
            px, py, pc = block
            dst = outs[a].at[4 * px + 2 * py + pc]
            return pltpu.make_async_remote_copy(
                src_ref=dst if src is None else src, dst_ref=dst,
                send_sem=send_sems.at[a, k], recv_sem=recv_sems.at[a, k], device_id=to, device_id_type=MESH)

        mine = [pltpu.make_async_copy(ins[a], outs[a].at[4 * x + 2 * y + c], local_sems.at[a]) for a in range(n)]
        for cp in mine:
            cp.start()
        first = []
        for j, chip in enumerate(chips):
            first += [copy(a, 1 + j, me, (*chip, c), src=ins[a]) for a in range(n)]
        first += [copy(a, 0, me, sibling, src=ins[a]) for a in range(n)]
        for cp in first:
            cp.start()
        passed = []
        for j, chip in enumerate(chips):
            for a in range(n):
                copy(a, 1 + j, (*chip, c), me).wait_recv()
                cp = copy(a, 4 + j, (*chip, c), sibling)
                cp.start()
                passed.append(cp)
        for a in range(n):
            copy(a, 0, sibling, me).wait_recv()
        for j, chip in enumerate(chips):
            for a in range(n):
                copy(a, 4 + j, (*chip, 1 - c), me).wait_recv()
        for cp in first + passed:
            cp.wait_send()
        for cp in mine:
            cp.wait()

    return pl.pallas_call(
        body, in_specs=[ANY] * n, out_specs=[ANY] * n,
        out_shape=[SDS((N_DEV,) + a.shape, a.dtype) for a in arrs],
        scratch_shapes=[pltpu.SemaphoreType.DMA((n, 7)), pltpu.SemaphoreType.DMA((n, 7)), pltpu.SemaphoreType.DMA((n,))],
        name="all_gather_weights")(*arrs)


def _exchange_sibling(gs):
    n = len(gs)

    def body(*refs):
        ins, outs = refs[:n], refs[n:2 * n]
        send_sems, recv_sems = refs[2 * n:]
        x, y, c = lax.axis_index("x"), lax.axis_index("y"), lax.axis_index("c")
        copies = []
        for a in range(n):
            for j in range(4):
                copies.append(pltpu.make_async_remote_copy(
                    src_ref=ins[a].at[2 * j + (1 - c)], dst_ref=outs[a].at[j],
                    send_sem=send_sems.at[a, j], recv_sem=recv_sems.at[a, j],
                    device_id=(x, y, 1 - c), device_id_type=MESH))
        for cp in copies:
            cp.start()
        for cp in copies:
            cp.wait_recv()
        for cp in copies:
            cp.wait_send()

    return pl.pallas_call(
        body, in_specs=[ANY] * n, out_specs=[ANY] * n,
        out_shape=[SDS((4,) + g.shape[1:], g.dtype) for g in gs],
        scratch_shapes=[pltpu.SemaphoreType.DMA((n, 4)), pltpu.SemaphoreType.DMA((n, 4))],
        name="reduce_scatter_sibling")(*gs)


def _row_tile(rows, cols, itemsize_total):
    budget = (4 << 20) // max(1, cols * itemsize_total)
    if rows <= budget:
        return rows
    t = rows
    while t > budget and t % 2 == 0 and (t // 2) % 16 == 0:
        t //= 2
    return t


def _add_pair(g, r1, core, name):
    _, rows, cols = g.shape
    tr = _row_tile(rows, cols, 3 * g.dtype.itemsize)

    def body(c_ref, g_ref, r_ref, o_ref):
        o_ref[...] = (g_ref[...].astype(F32) + r_ref[...].astype(F32)).astype(o_ref.dtype)

    return pl.pallas_call(
        body,
        grid_spec=pltpu.PrefetchScalarGridSpec(
            num_scalar_prefetch=1, grid=(4, rows // tr),
            in_specs=[pl.BlockSpec((1, tr, cols), lambda j, i, c_ref: (2 * j + c_ref[0], i, 0)),
                      pl.BlockSpec((1, tr, cols), lambda j, i, c_ref: (j, i, 0))],
            out_specs=pl.BlockSpec((1, tr, cols), lambda j, i, c_ref: (j, i, 0))),
        out_shape=SDS((4, rows, cols), g.dtype),
        compiler_params=_params("parallel", "parallel"), name=name)(core, g, r1)


def _exchange_chips(ps):
    n = len(ps)

    def body(*refs):
        ins, outs = refs[:n], refs[n:2 * n]
        send_sems, recv_sems, local_sems = refs[2 * n:]
        x, y, c = lax.axis_index("x"), lax.axis_index("y"), lax.axis_index("c")
        my_chip = 2 * x + y
        mine = [pltpu.make_async_copy(ins[a].at[my_chip], outs[a].at[my_chip], local_sems.at[a]) for a in range(n)]
        for cp in mine:
            cp.start()
        copies = []
        for k, (px, py) in enumerate([(1 - x, y), (x, 1 - y), (1 - x, 1 - y)]):
            for a in range(n):
                copies.append(pltpu.make_async_remote_copy(
                    src_ref=ins[a].at[2 * px + py], dst_ref=outs[a].at[my_chip],
                    send_sem=send_sems.at[a, k], recv_sem=recv_sems.at[a, k],
                    device_id=(px, py, c), device_id_type=MESH))
        for cp in copies:
            cp.start()
        for k, (px, py) in enumerate([(1 - x, y), (x, 1 - y), (1 - x, 1 - y)]):
            for a in range(n):
                pltpu.make_async_remote_copy(
                    src_ref=ins[a].at[my_chip], dst_ref=outs[a].at[2 * px + py],
                    send_sem=send_sems.at[a, k], recv_sem=recv_sems.at[a, k],
                    device_id=(px, py, c), device_id_type=MESH).wait_recv()
        for cp in copies:
            cp.wait_send()
        for cp in mine:
            cp.wait()

    return pl.pallas_call(
        body, in_specs=[ANY] * n, out_specs=[ANY] * n,
        out_shape=[SDS(p.shape, p.dtype) for p in ps],
        scratch_shapes=[pltpu.SemaphoreType.DMA((n, 3)), pltpu.SemaphoreType.DMA((n, 3)), pltpu.SemaphoreType.DMA((n,))],
        name="reduce_scatter_chips")(*ps)


def _adam_math(g, w, m, v):
    m_new = ADAM_B1 * m + (1.0 - ADAM_B1) * g
    v_new = ADAM_B2 * v + (1.0 - ADAM_B2) * (g * g)
    m_hat = m_new / (1.0 - ADAM_B1 ** ADAM_STEP)
    v_hat = v_new / (1.0 - ADAM_B2 ** ADAM_STEP)
    delta = -ADAM_LR * (m_hat / (jnp.sqrt(v_hat) + ADAM_EPS) + ADAM_WD * w)
    return delta, m_new, v_new


def _sum_adam(parts, w, m, v, name):
    rows, cols = w.shape
    tr = _row_tile(rows, cols, 4 * parts.dtype.itemsize + 7 * 4)

    def body(p_ref, w_ref, m_ref, v_ref, g_ref, d_ref, mo_ref, vo_ref):
        g = p_ref[0].astype(F32)
        for s in range(1, 4):
            g = g + p_ref[s].astype(F32)
        delta, m_new, v_new = _adam_math(g, w_ref[...], m_ref[...], v_ref[...])
        g_ref[...] = g
        d_ref[...] = delta
        mo_ref[...] = m_new
        vo_ref[...] = v_new

    blk = pl.BlockSpec((tr, cols), lambda i: (i, 0))
    out = SDS((rows, cols), F32)
    return pl.pallas_call(
        body, grid=(rows // tr,),
        in_specs=[pl.BlockSpec((4, tr, cols), lambda i: (0, i, 0)), blk, blk, blk],
        out_specs=[blk, blk, blk, blk], out_shape=[out, out, out, out],
        compiler_params=_params("parallel"), name=name)(parts, w, m, v)


SMALL_ROWS = 64


def _small_allreduce_adam(gpart, w, m, v):
    def body(g_ref, w_ref, m_ref, v_ref, go_ref, d_ref, mo_ref, vo_ref, gath, send_sems, recv_sems):
        x, y, c = lax.axis_index("x"), lax.axis_index("y"), lax.axis_index("c")
        me = 4 * x + 2 * y + c
        gath[me] = g_ref[...]
        copies = []
        for k in range(1, N_DEV):
            fx, fy, fc = (k >> 2) & 1, (k >> 1) & 1, k & 1
            peer = (x ^ fx, y ^ fy, c ^ fc)
            copies.append(pltpu.make_async_remote_copy(
                src_ref=gath.at[me], dst_ref=gath.at[me], send_sem=send_sems.at[k - 1], recv_sem=recv_sems.at[k - 1],
                device_id=peer, device_id_type=MESH))
        for cp in copies:
            cp.start()
        for cp in copies:
            cp.wait_recv()
        for cp in copies:
            cp.wait_send()
        g = gath[0]
        for d in range(1, N_DEV):
            g = g + gath[d]
        delta, m_new, v_new = _adam_math(g, w_ref[...], m_ref[...], v_ref[...])
        go_ref[...] = g
        d_ref[...] = delta
        mo_ref[...] = m_new
        vo_ref[...] = v_new

    vm = pl.BlockSpec(memory_space=pltpu.VMEM)
    out = SDS((SMALL_ROWS, 128), F32)
    return pl.pallas_call(
        body, in_specs=[vm] * 4, out_specs=[vm] * 4, out_shape=[out] * 4,
        scratch_shapes=[pltpu.VMEM((N_DEV, SMALL_ROWS, 128), F32), pltpu.SemaphoreType.DMA((N_DEV - 1,)),
                        pltpu.SemaphoreType.DMA((N_DEV - 1,))],
        name="small_allreduce_adam")(gpart, w, m, v)


BIG = ("w_in", "conv_w", "w_conv_out", "w_attn_out", "w_o", "w_ffn_gate", "w_ffn_up", "w_ffn_down")
COL_SHARDED = ("w_in", "conv_w", "w_attn_out", "w_ffn_gate", "w_ffn_up")
SMALL = ("norm1_g", "gate_b", "conv_b", "conv_ln_g", "conv_ln_b", "norm2_g", "norm_f_g")
WEIGHTS = ("norm1_g", "w_in", "gate_b", "conv_w", "conv_b", "conv_ln_g", "conv_ln_b", "w_conv_out", "w_attn_out", "w_o",
           "norm2_g", "w_ffn_gate", "w_ffn_up", "w_ffn_down", "norm_f_g")


def _shard2d(name, a):
    a = a.reshape(a.shape[-2], a.shape[-1])
    if name == "conv_w":
        a = jnp.pad(a, ((0, CONV_PAD - CONV_K), (0, 0)))
    return a


def _gathered_to_full(name, g):
    if name in COL_SHARDED:
        return g.transpose(1, 0, 2).reshape(g.shape[1], N_DEV * g.shape[2])
    return g.reshape(N_DEV * g.shape[1], g.shape[2])


def _full_to_blocks(name, g):
    if name in COL_SHARDED:
        return g.reshape(g.shape[0], N_DEV, g.shape[1] // N_DEV).transpose(1, 0, 2)
    return g.reshape(N_DEV, g.shape[0] // N_DEV, g.shape[1])


def _pack_small(d):
    return jnp.concatenate([d[n].reshape(-1) for n in SMALL]).reshape(SMALL_ROWS, 128)


def _unpack_small(p, like):
    flat = p.reshape(-1)
    out, off = {}, 0
    for n in SMALL:
        size = like[n].size
        out[n] = flat[off:off + size].reshape(like[n].shape)
        off += size
    return out


def kernel(x, norm1_g, w_in, gate_b, conv_w, conv_b, conv_ln_g, conv_ln_b, w_conv_out, w_attn_out, w_o, norm2_g, w_ffn_gate, w_ffn_up, w_ffn_down, norm_f_g, loss_target, m_norm1_g, m_w_in, m_gate_b, m_conv_w, m_conv_b, m_conv_ln_g, m_conv_ln_b, m_w_conv_out, m_w_attn_out, m_w_o, m_norm2_g, m_w_ffn_gate, m_w_ffn_up, m_w_ffn_down, m_norm_f_g, v_norm1_g, v_w_in, v_gate_b, v_conv_w, v_conv_b, v_conv_ln_g, v_conv_ln_b, v_w_conv_out, v_w_attn_out, v_w_o, v_norm2_g, v_w_ffn_gate, v_w_ffn_up, v_w_ffn_down, v_norm_f_g):
    wts = dict(norm1_g=norm1_g, w_in=w_in, gate_b=gate_b, conv_w=conv_w, conv_b=conv_b, conv_ln_g=conv_ln_g,
               conv_ln_b=conv_ln_b, w_conv_out=w_conv_out, w_attn_out=w_attn_out, w_o=w_o, norm2_g=norm2_g,
               w_ffn_gate=w_ffn_gate, w_ffn_up=w_ffn_up, w_ffn_down=w_ffn_down, norm_f_g=norm_f_g)
    mom1 = dict(norm1_g=m_norm1_g, w_in=m_w_in, gate_b=m_gate_b, conv_w=m_conv_w, conv_b=m_conv_b, conv_ln_g=m_conv_ln_g,
                conv_ln_b=m_conv_ln_b, w_conv_out=m_w_conv_out, w_attn_out=m_w_attn_out, w_o=m_w_o, norm2_g=m_norm2_g,
                w_ffn_gate=m_w_ffn_gate, w_ffn_up=m_w_ffn_up, w_ffn_down=m_w_ffn_down, norm_f_g=m_norm_f_g)
    mom2 = dict(norm1_g=v_norm1_g, w_in=v_w_in, gate_b=v_gate_b, conv_w=v_conv_w, conv_b=v_conv_b, conv_ln_g=v_conv_ln_g,
                conv_ln_b=v_conv_ln_b, w_conv_out=v_w_conv_out, w_attn_out=v_w_attn_out, w_o=v_w_o, norm2_g=v_norm2_g,
                w_ffn_gate=v_w_ffn_gate, w_ffn_up=v_w_ffn_up, w_ffn_down=v_w_ffn_down, norm_f_g=v_norm_f_g)

    T = x.shape[0] * x.shape[1]
    x2 = x.reshape(T, D_MODEL)
    t2 = loss_target.reshape(T, D_MODEL)

    shards = {n: _shard2d(n, wts[n]) for n in BIG}
    send = [shards[n] if n == "conv_w" else shards[n].astype(BF16) for n in BIG]
    gathered = _all_gather(send)
    full = {n: _gathered_to_full(n, g) for n, g in zip(BIG, gathered)}
    small = {n: wts[n].reshape(1, -1) for n in SMALL}

    loss_part, grad_x, gw, gsmall = _local_step(x2, t2, full, small)

    blocks = [_full_to_blocks(n, gw[n]) for n in BIG]
    core = lax.axis_index("c").astype(jnp.int32).reshape(1)
    from_sibling = _exchange_sibling(blocks)
    chip_sums = [_add_pair(g, r, core, "chip_sum_" + n) for n, g, r in zip(BIG, blocks, from_sibling)]
    by_chip = _exchange_chips(chip_sums)

    grads, deltas, new_m, new_v = {}, {}, {}, {}
    for n, parts in zip(BIG, by_chip):
        g, d, mo, vo = _sum_adam(parts, shards[n], _shard2d(n, mom1[n]), _shard2d(n, mom2[n]), "adam_" + n)
        for dst, val in ((grads, g), (deltas, d), (new_m, mo), (new_v, vo)):
            if n == "conv_w":
                val = val[:CONV_K]
            dst[n] = val.reshape(wts[n].shape)

    sg, sd, sm, sv = _small_allreduce_adam(_pack_small(gsmall), _pack_small(wts), _pack_small(mom1), _pack_small(mom2))
    for dst, val in ((grads, sg), (deltas, sd), (new_m, sm), (new_v, sv)):
        dst.update(_unpack_small(val, wts))

    loss = lax.psum(loss_part[0, 0], ("x", "y", "c"))
    return (loss, grad_x.reshape(x.shape), *[grads[n] for n in WEIGHTS], *[deltas[n] for n in WEIGHTS],
            *[new_m[n] for n in WEIGHTS], *[new_v[n] for n in WEIGHTS])
```

```python
import math

import numpy as np
import jax
import jax.numpy as jnp
from jax import lax
from jax.experimental import pallas as pl
from jax.experimental.pallas import tpu as pltpu

F32 = jnp.float32
BF16 = jnp.bfloat16
SDS = jax.ShapeDtypeStruct
MESH = pl.DeviceIdType.MESH

D_MODEL = 1024
SEQ = 2048
HEAD_DIM = 64
GROUPS = ((128, 1), (512, 4), (2048, 16))
HEADS_PER_GROUP = 8
N_HEADS = 24
ATTN_WIDTH = N_HEADS * HEAD_DIM
ATTN_OUT = HEADS_PER_GROUP * HEAD_DIM
CONV_K = 31
CONV_PAD = 32
D_FF = 2816
IN_WIDTH = 3 * ATTN_WIDTH + 2 * D_MODEL + 2 * D_MODEL
RMS_EPS = 1e-6
LN_EPS = 1e-5
Q_BLOCK = 128
LANES = 128
NEG = -1e30
N_DEV = 8

ADAM_LR = 0.001
ADAM_B1 = 0.9
ADAM_B2 = 0.999
ADAM_EPS = 1e-08
ADAM_WD = 0.01
ADAM_STEP = 10


def _alibi_slope_list(n):
    def pow2(m):
        start = 2.0 ** (-8.0 / m)
        return [start ** (i + 1) for i in range(m)]
    if math.log2(n).is_integer():
        return pow2(n)
    c = 2 ** math.floor(math.log2(n))
    return pow2(c) + _alibi_slope_list(2 * c)[0::2][: n - c]


def _slopes_times_dilation():
    s = np.asarray(sorted(_alibi_slope_list(N_HEADS), reverse=True), dtype=np.float32).reshape(3, HEADS_PER_GROUP)
    r = np.asarray([g[1] for g in GROUPS], dtype=np.float32)[:, None]
    return (s * r).reshape(N_HEADS)


def _sigmoid(x):
    return 1.0 / (1.0 + jnp.exp(-x))


def _dot(a, b):
    return jnp.dot(a, b, preferred_element_type=F32)


def _dot_nt(a, b):
    return lax.dot_general(a, b, (((1,), (1,)), ((), ())), preferred_element_type=F32)


def _dot_tn(a, b):
    return lax.dot_general(a, b, (((0,), (0,)), ((), ())), preferred_element_type=F32)


def _rowsum(x):
    return jnp.sum(x, axis=0, keepdims=True)


def _params(*sem):
    return pltpu.CompilerParams(dimension_semantics=sem)


def _anchored(body, n_in, dep):
    if dep is None:
        return body, [], []

    def wrapped(*refs):
        return body(*refs[:n_in], *refs[n_in + 1:])

    return wrapped, [pl.BlockSpec(memory_space=pl.ANY)], [dep]


IN_TM = 256
IN_CHUNK = 512


def _in_proj(x, g1, w_in, dep=None):
    T = x.shape[0]
    tm = IN_TM
    widths = (3 * ATTN_WIDTH, 2 * D_MODEL, 2 * D_MODEL)

    def body(x_ref, g_ref, w_hbm, h_ref, qkv_ref, u_ref, lg_ref, w_vmem, sem):
        @pl.when(pl.program_id(0) == 0)
        def _():
            cp = pltpu.make_async_copy(w_hbm, w_vmem, sem)
            cp.start()
            cp.wait()

        xv = x_ref[...]
        r = lax.rsqrt(jnp.mean(xv * xv, axis=-1, keepdims=True) + RMS_EPS)
        h = (xv * r * g_ref[...]).astype(BF16)
        h_ref[...] = h
        col = 0
        for o_ref, width in zip((qkv_ref, u_ref, lg_ref), widths):
            for j in range(width // IN_CHUNK):
                o_ref[:, j * IN_CHUNK:(j + 1) * IN_CHUNK] = _dot(h, w_vmem[:, col:col + IN_CHUNK])
                col += IN_CHUNK

    row = lambda n: pl.BlockSpec((tm, n), lambda i: (i, 0))
    body, dep_spec, dep_arg = _anchored(body, 3, dep)
    return pl.pallas_call(
        body, grid=(T // tm,),
        in_specs=[row(D_MODEL), pl.BlockSpec((1, D_MODEL), lambda i: (0, 0)), pl.BlockSpec(memory_space=pl.ANY)] + dep_spec,
        out_specs=[row(D_MODEL)] + [row(n) for n in widths],
        out_shape=[SDS((T, D_MODEL), BF16)] + [SDS((T, n), F32) for n in widths],
        scratch_shapes=[pltpu.VMEM((D_MODEL, IN_WIDTH), BF16), pltpu.SemaphoreType.DMA],
        compiler_params=_params("arbitrary"), name="in_proj")(x, g1, w_in, *dep_arg)


def _mm_tn(a, b, out_dtype, name, tn, tt=512):
    T, K = a.shape
    N = b.shape[1]
    nt = T // tt

    def body(a_ref, b_ref, o_ref, acc):
        t = pl.program_id(1)
        p = _dot_tn(a_ref[...], b_ref[...])

        @pl.when(t == 0)
        def _():
            acc[...] = p

        @pl.when(t > 0)
        def _():
            acc[...] += p

        @pl.when(t == nt - 1)
        def _():
            o_ref[...] = acc[...].astype(o_ref.dtype)

    return pl.pallas_call(
        body, grid=(N // tn, nt),
        in_specs=[pl.BlockSpec((tt, K), lambda j, t: (t, 0)),
                  pl.BlockSpec((tt, tn), lambda j, t: (t, j))],
        out_specs=pl.BlockSpec((K, tn), lambda j, t: (0, j)),
        out_shape=SDS((K, N), out_dtype),
        scratch_shapes=[pltpu.VMEM((K, tn), F32)],
        compiler_params=_params("parallel", "arbitrary"), name=name)(a, b)


def _gather_classes(src_ref, dst, r):
    L = SEQ // r
    for c in range(r):
        dst[c * L:(c + 1) * L, :] = src_ref[0, pl.ds(c, L, stride=r), :].astype(dst.dtype)


def _scatter_classes(src, dst, r):
    L = SEQ // r
    for c in range(r):
        dst[pl.ds(c, L, stride=r), :] = src[c * L:(c + 1) * L, :].astype(dst.dtype)


def _attn_masks(slope_r):
    qi = lax.broadcasted_iota(jnp.int32, (Q_BLOCK, Q_BLOCK), 0)
    kj = lax.broadcasted_iota(jnp.int32, (Q_BLOCK, Q_BLOCK), 1)
    rel = (qi - kj).astype(F32)
    bias_cur = jnp.where(qi >= kj, -slope_r * rel, NEG)
    bias_prev = jnp.where(qi <= kj, -slope_r * (rel + float(Q_BLOCK)), NEG)
    return bias_cur, bias_prev


def _store_biases(bias, sl_ref, g, hp):
    for hh in range(2):
        cur, prev = _attn_masks(sl_ref[g * HEADS_PER_GROUP + 2 * hp + hh])
        bias[0, hh * Q_BLOCK:(hh + 1) * Q_BLOCK, :] = cur
        bias[1, hh * Q_BLOCK:(hh + 1) * Q_BLOCK, :] = prev


def _transpose_blocks(src, dst):
    for b in range(SEQ // Q_BLOCK):
        dst[b] = src[b * Q_BLOCK:(b + 1) * Q_BLOCK, :].T


def _stack_heads(t, low):
    z = jnp.zeros_like(t)
    return jnp.concatenate([jnp.where(low, t, z), jnp.where(low, z, t)], axis=0)


def _unstack_heads(t2, low):
    return jnp.where(low, t2[:Q_BLOCK], t2[Q_BLOCK:])


def _unit_offsets(u, nb):
    off = pl.multiple_of(u * Q_BLOCK, Q_BLOCK)
    offp = pl.multiple_of(jnp.maximum(u - 1, 0) * Q_BLOCK, Q_BLOCK)
    n = u & (nb - 1)
    c = u >> int(math.log2(nb))
    return off, offp, n == 0, c, n


def _attn_fwd(qkv, slopes_r, batch):
    nblk = SEQ // Q_BLOCK

    def body(sl_ref, *refs):
        qkv_refs = refs[:9]
        att_ref, lse_ref = refs[9:11]
        qd, kd, vd, kt, opos, lpos, bias = refs[11:]
        hp = pl.program_id(1)
        low = lax.broadcasted_iota(jnp.int32, (Q_BLOCK, LANES), 1) < HEAD_DIM

        for g in range(3):
            r = GROUPS[g][1]
            nb = SEQ // r // Q_BLOCK
            _gather_classes(qkv_refs[3 * g], qd, r)
            _gather_classes(qkv_refs[3 * g + 1], kd, r)
            _gather_classes(qkv_refs[3 * g + 2], vd, r)
            _transpose_blocks(kd, kt)
            _store_biases(bias, sl_ref, g, hp)

            def unit(u, carry, g=g, r=r, nb=nb):
                off, offp, first, c, n = _unit_offsets(u, nb)
                q2 = _stack_heads(qd[pl.ds(off, Q_BLOCK), :], low)
                vc = vd[pl.ds(off, Q_BLOCK), :]
                vp = vd[pl.ds(offp, Q_BLOCK), :]
                sc = _dot(q2, kt[u]) * 0.125 + bias[0]
                sp = jnp.where(first, NEG, _dot(q2, kt[jnp.maximum(u - 1, 0)]) * 0.125 + bias[1])
                m = jnp.max(jnp.maximum(sc, sp), axis=-1, keepdims=True)
                pc = jnp.exp(sc - m)
                pp = jnp.exp(sp - m)
                l = jnp.sum(pc + pp, axis=-1, keepdims=True)
                o2 = (_dot(pc.astype(BF16), vc) + _dot(pp.astype(BF16), vp)) * (1.0 / l)
                lse2 = m + jnp.log(l)
                rows = pl.ds(c + n * (Q_BLOCK * r), Q_BLOCK, stride=r)
                opos[g, rows, :] = _unstack_heads(o2, low)
                lpos[g, rows, :] = jnp.where(low, lse2[:Q_BLOCK], lse2[Q_BLOCK:])
                return carry

            lax.fori_loop(0, nblk, unit, 0, unroll=2)

        def merge(i, carry):
            rows = pl.ds(pl.multiple_of(i * 256, 256), 256)
            l0, l1, l2 = lpos[0, rows, :], lpos[1, rows, :], lpos[2, rows, :]
            m = jnp.maximum(jnp.maximum(l0, l1), l2)
            e0, e1, e2 = jnp.exp(l0 - m), jnp.exp(l1 - m), jnp.exp(l2 - m)
            den = e0 + e1 + e2
            att = (e0 * opos[0, rows, :] + e1 * opos[1, rows, :] + e2 * opos[2, rows, :]) / den
            att_ref[0, rows, :] = att.astype(att_ref.dtype)
            lse_ref[0, rows, :] = m + jnp.log(den)
            return carry

        lax.fori_loop(0, SEQ // 256, merge, 0)

    def col(sec, g):
        return pl.BlockSpec((1, SEQ, LANES), lambda b, hp: (b, 0, sec * 12 + g * 4 + hp))

    out = pl.BlockSpec((1, SEQ, LANES), lambda b, hp: (b, 0, hp))
    return pl.pallas_call(
        body, grid=(batch, 4),
        in_specs=[pl.BlockSpec(memory_space=pltpu.SMEM)] + [col(sec, g) for g in range(3) for sec in range(3)],
        out_specs=[out, out],
        out_shape=[SDS((batch, SEQ, ATTN_OUT), BF16), SDS((batch, SEQ, ATTN_OUT), F32)],
        scratch_shapes=[pltpu.VMEM((SEQ, LANES), BF16), pltpu.VMEM((SEQ, LANES), BF16), pltpu.VMEM((SEQ, LANES), BF16),
                        pltpu.VMEM((nblk, LANES, Q_BLOCK), BF16),
                        pltpu.VMEM((3, SEQ, LANES), F32), pltpu.VMEM((3, SEQ, LANES), F32),
                        pltpu.VMEM((2, 2 * Q_BLOCK, Q_BLOCK), F32)],
        compiler_params=_params("parallel", "parallel"), name="attn_fwd")(slopes_r, *([qkv] * 9))


def _attn_bwd(qkv, datt, lse, dsum, slopes_r, batch):
    nblk = SEQ // Q_BLOCK

    def body(sl_ref, q_ref, k_ref, v_ref, do_ref, l_ref, d_ref, dq_ref, dk_ref, dv_ref,
             qd, kd, vd, dod, kt, vt, ld, dd, dq_acc, dk_acc, dv_acc, stage, bias):
        gid, hp = pl.program_id(1), pl.program_id(2)
        low = lax.broadcasted_iota(jnp.int32, (Q_BLOCK, LANES), 1) < HEAD_DIM

        def section(g):
            r = GROUPS[g][1]
            nb = SEQ // r // Q_BLOCK
            _gather_classes(q_ref, qd, r)
            _gather_classes(k_ref, kd, r)
            _gather_classes(v_ref, vd, r)
            _gather_classes(do_ref, dod, r)
            _gather_classes(l_ref, ld, r)
            _gather_classes(d_ref, dd, r)
            _transpose_blocks(kd, kt)
            _transpose_blocks(vd, vt)
            _store_biases(bias, sl_ref, g, hp)
            dk_acc[...] = jnp.zeros_like(dk_acc)
            dv_acc[...] = jnp.zeros_like(dv_acc)

            def unit(u, carry):
                off, offp, first, _, _ = _unit_offsets(u, nb)
                up = jnp.maximum(u - 1, 0)
                q2 = _stack_heads(qd[pl.ds(off, Q_BLOCK), :], low)
                do2 = _stack_heads(dod[pl.ds(off, Q_BLOCK), :], low)
                kc = kd[pl.ds(off, Q_BLOCK), :]
                kp = kd[pl.ds(offp, Q_BLOCK), :]
                lse_t = ld[pl.ds(off, Q_BLOCK), :]
                dsum_t = dd[pl.ds(off, Q_BLOCK), :]
                lse2 = jnp.concatenate([lse_t[:, 0:1], lse_t[:, HEAD_DIM:HEAD_DIM + 1]], axis=0)
                dsum2 = jnp.concatenate([dsum_t[:, 0:1], dsum_t[:, HEAD_DIM:HEAD_DIM + 1]], axis=0)
                sc = _dot(q2, kt[u]) * 0.125 + bias[0]
                sp = jnp.where(first, NEG, _dot(q2, kt[up]) * 0.125 + bias[1])
                pc = jnp.exp(sc - lse2)
                pp = jnp.exp(sp - lse2)
                dsc = (pc * (_dot(do2, vt[u]) - dsum2)).astype(BF16)
                dsp = (pp * (_dot(do2, vt[up]) - dsum2)).astype(BF16)
                dq2 = _dot(dsc, kc) + _dot(dsp, kp)
                dq_acc[pl.ds(off, Q_BLOCK), :] = _unstack_heads(dq2, low) * 0.125
                dk_acc[pl.ds(off, Q_BLOCK), :] += _dot_tn(dsc, q2) * 0.125
                dk_acc[pl.ds(offp, Q_BLOCK), :] += _dot_tn(dsp, q2) * 0.125
                dv_acc[pl.ds(off, Q_BLOCK), :] += _dot_tn(pc.astype(BF16), do2)
                dv_acc[pl.ds(offp, Q_BLOCK), :] += _dot_tn(pp.astype(BF16), do2)
                return carry

            lax.fori_loop(0, nblk, unit, 0, unroll=2)
            for acc, out_ref in ((dq_acc, dq_ref), (dk_acc, dk_ref), (dv_acc, dv_ref)):
                _scatter_classes(acc, stage, r)
                out_ref[0] = stage[...].astype(out_ref.dtype)

        for g in range(3):
            pl.when(gid == g)(lambda g=g: section(g))

    def col(sec):
        return pl.BlockSpec((1, SEQ, LANES), lambda b, g, hp: (b, 0, sec * 12 + g * 4 + hp))

    pos = pl.BlockSpec((1, SEQ, LANES), lambda b, g, hp: (b, 0, hp))
    dout = pl.BlockSpec((1, SEQ, LANES), lambda b, g, hp: (b, 0, g * 4 + hp))
    out = SDS((batch, SEQ, ATTN_WIDTH), BF16)
    seq_bf = pltpu.VMEM((SEQ, LANES), BF16)
    seq_f = pltpu.VMEM((SEQ, LANES), F32)
    blk_t = pltpu.VMEM((nblk, LANES, Q_BLOCK), BF16)
    return pl.pallas_call(
        body, grid=(batch, 3, 4),
        in_specs=[pl.BlockSpec(memory_space=pltpu.SMEM), col(0), col(1), col(2), pos, pos, pos],
        out_specs=[dout, dout, dout],
        out_shape=[out, out, out],
        scratch_shapes=[seq_bf, seq_bf, seq_bf, seq_bf, blk_t, blk_t, seq_f, seq_f, seq_f, seq_f, seq_f, seq_f,
                        pltpu.VMEM((2, 2 * Q_BLOCK, Q_BLOCK), F32)],
        compiler_params=_params("parallel", "parallel", "parallel"), name="attn_bwd")(
            slopes_r, qkv, qkv, qkv, datt, lse, dsum)


CONV_TC = 256
CONV_ROWS = 64


def _conv_fwd(u, conv_w, conv_b, batch):
    nct = D_MODEL // CONV_TC

    def body(ua_ref, ub_ref, w_ref, b_ref, o_ref, pad):
        pad[0:CONV_PAD, :] = jnp.zeros((CONV_PAD, CONV_TC), F32)
        pad[CONV_PAD:, :] = ua_ref[0] * _sigmoid(ub_ref[0])

        def chunk(c, carry):
            base = pl.multiple_of(c * CONV_ROWS, CONV_ROWS)
            win = pad[pl.ds(base, CONV_ROWS + CONV_PAD), :]
            acc = jnp.broadcast_to(b_ref[...], (CONV_ROWS, CONV_TC))
            for t in range(CONV_K):
                s = t + CONV_PAD - (CONV_K - 1)
                acc = acc + win[s:s + CONV_ROWS, :] * w_ref[t:t + 1, :]
            o_ref[0, pl.ds(base, CONV_ROWS), :] = acc
            return carry

        lax.fori_loop(0, SEQ // CONV_ROWS, chunk, 0)

    return pl.pallas_call(
        body, grid=(nct, batch),
        in_specs=[pl.BlockSpec((1, SEQ, CONV_TC), lambda j, b: (b, 0, j)),
                  pl.BlockSpec((1, SEQ, CONV_TC), lambda j, b: (b, 0, j + nct)),
                  pl.BlockSpec((CONV_PAD, CONV_TC), lambda j, b: (0, j)),
                  pl.BlockSpec((1, CONV_TC), lambda j, b: (0, j))],
        out_specs=pl.BlockSpec((1, SEQ, CONV_TC), lambda j, b: (b, 0, j)),
        out_shape=SDS((batch, SEQ, D_MODEL), F32),
        scratch_shapes=[pltpu.VMEM((SEQ + CONV_PAD, CONV_TC), F32)],
        compiler_params=_params("parallel", "parallel"), name="conv_fwd")(u, u, conv_w, conv_b)


def _conv_bwd(u, dc1, conv_w, batch, dep=None):
    nct = D_MODEL // CONV_TC
    nchunk = SEQ // CONV_ROWS

    def body(ua_ref, ub_ref, d_ref, w_ref, dua_ref, dub_ref, gw_ref, gb_ref, padc, padd, gacc):
        b = pl.program_id(1)
        sig = _sigmoid(ub_ref[0])
        padc[0:CONV_PAD, :] = jnp.zeros((CONV_PAD, CONV_TC), F32)
        padc[CONV_PAD:, :] = ua_ref[0] * sig
        padd[0:SEQ, :] = d_ref[0]
        padd[SEQ:, :] = jnp.zeros((CONV_PAD, CONV_TC), F32)

        @pl.when(b == 0)
        def _():
            gacc[...] = jnp.zeros_like(gacc)
            gb_ref[...] = jnp.zeros_like(gb_ref)

        gb_ref[...] += _rowsum(d_ref[0])

        def chunk(c, carry):
            base = pl.multiple_of(c * CONV_ROWS, CONV_ROWS)
            wind = padd[pl.ds(base, CONV_ROWS + CONV_PAD), :]
            winc = padc[pl.ds(base, CONV_ROWS + CONV_PAD), :]
            dcur = wind[0:CONV_ROWS, :]
            acc = jnp.zeros((CONV_ROWS, CONV_TC), F32)
            for t in range(CONV_K):
                s = CONV_K - 1 - t
                acc = acc + wind[s:s + CONV_ROWS, :] * w_ref[t:t + 1, :]
                sc = t + CONV_PAD - (CONV_K - 1)
                prod = winc[sc:sc + CONV_ROWS, :] * dcur
                gacc[t] += jnp.sum(prod.reshape(CONV_ROWS // 8, 8, CONV_TC), axis=0)
            ua = ua_ref[0, pl.ds(base, CONV_ROWS), :]
            sg = _sigmoid(ub_ref[0, pl.ds(base, CONV_ROWS), :])
            dua_ref[0, pl.ds(base, CONV_ROWS), :] = (acc * sg).astype(dua_ref.dtype)
            dub_ref[0, pl.ds(base, CONV_ROWS), :] = (acc * ua * sg * (1.0 - sg)).astype(dub_ref.dtype)
            return carry

        lax.fori_loop(0, nchunk, chunk, 0)

        @pl.when(b == batch - 1)
        def _():
            for t in range(CONV_K):
                gw_ref[t:t + 1, :] = jnp.sum(gacc[t], axis=0, keepdims=True)
            gw_ref[CONV_K:CONV_PAD, :] = jnp.zeros((CONV_PAD - CONV_K, CONV_TC), F32)

    du = SDS((batch, SEQ, D_MODEL), BF16)
    body, dep_spec, dep_arg = _anchored(body, 4, dep)
    return pl.pallas_call(
        body, grid=(nct, batch),
        in_specs=[pl.BlockSpec((1, SEQ, CONV_TC), lambda j, b: (b, 0, j)),
                  pl.BlockSpec((1, SEQ, CONV_TC), lambda j, b: (b, 0, j + nct)),
                  pl.BlockSpec((1, SEQ, CONV_TC), lambda j, b: (b, 0, j)),
                  pl.BlockSpec((CONV_PAD, CONV_TC), lambda j, b: (0, j))] + dep_spec,
        out_specs=[pl.BlockSpec((1, SEQ, CONV_TC), lambda j, b: (b, 0, j)),
                   pl.BlockSpec((1, SEQ, CONV_TC), lambda j, b: (b, 0, j)),
                   pl.BlockSpec((CONV_PAD, CONV_TC), lambda j, b: (0, j)),
                   pl.BlockSpec((1, CONV_TC), lambda j, b: (0, j))],
        out_shape=[du, du, SDS((CONV_PAD, D_MODEL), F32), SDS((1, D_MODEL), F32)],
        scratch_shapes=[pltpu.VMEM((SEQ + CONV_PAD, CONV_TC), F32), pltpu.VMEM((SEQ + CONV_PAD, CONV_TC), F32),
                        pltpu.VMEM((CONV_K, 8, CONV_TC), F32)],
        compiler_params=_params("parallel", "arbitrary"), name="conv_bwd")(u, u, dc1, conv_w, *dep_arg)


MID_TM = 256


def _layernorm_stats(c1):
    mu = jnp.mean(c1, axis=-1, keepdims=True)
    cen = c1 - mu
    rs = lax.rsqrt(jnp.mean(cen * cen, axis=-1, keepdims=True) + LN_EPS)
    return cen * rs, rs


def _mid_fwd(att, c1, logits, x, w_a, w_c, w_o, gate_b, ln_g, ln_b, g2):
    T = x.shape[0]
    tm = MID_TM

    def body(att_ref, c1_ref, lg_ref, x_ref, wa_ref, wc_ref, wo_ref, gb_ref, lng_ref, lnb_ref, g2_ref,
             c3_ref, ya_ref, yc_ref, mix_ref, x1_ref, h2_ref):
        ya = _dot(att_ref[...], wa_ref[...])
        xh, _ = _layernorm_stats(c1_ref[...])
        c2 = xh * lng_ref[...] + lnb_ref[...]
        c3 = (c2 * _sigmoid(c2)).astype(BF16)
        c3_ref[...] = c3
        yc = _dot(c3, wc_ref[...])
        gates = _sigmoid(lg_ref[...] + gb_ref[...])
        mix = (gates[:, :D_MODEL] * ya + gates[:, D_MODEL:] * yc).astype(BF16)
        ya_ref[...] = ya.astype(BF16)
        yc_ref[...] = yc.astype(BF16)
        mix_ref[...] = mix
        x1 = x_ref[...] + _dot(mix, wo_ref[...])
        x1_ref[...] = x1
        r = lax.rsqrt(jnp.mean(x1 * x1, axis=-1, keepdims=True) + RMS_EPS)
        h2_ref[...] = (x1 * r * g2_ref[...]).astype(BF16)

    row = lambda n: pl.BlockSpec((tm, n), lambda i: (i, 0))
    full = lambda a, b: pl.BlockSpec((a, b), lambda i: (0, 0))
    return pl.pallas_call(
        body, grid=(T // tm,),
        in_specs=[row(ATTN_OUT), row(D_MODEL), row(2 * D_MODEL), row(D_MODEL),
                  full(ATTN_OUT, D_MODEL), full(D_MODEL, D_MODEL), full(D_MODEL, D_MODEL),
                  full(1, 2 * D_MODEL), full(1, D_MODEL), full(1, D_MODEL), full(1, D_MODEL)],
        out_specs=[row(D_MODEL), row(D_MODEL), row(D_MODEL), row(D_MODEL), row(D_MODEL), row(D_MODEL)],
        out_shape=[SDS((T, D_MODEL), BF16), SDS((T, D_MODEL), BF16), SDS((T, D_MODEL), BF16), SDS((T, D_MODEL), BF16),
                   SDS((T, D_MODEL), F32), SDS((T, D_MODEL), BF16)],
        compiler_params=_params("parallel"), name="mid_fwd")(att, c1, logits, x, w_a, w_c, w_o, gate_b, ln_g, ln_b, g2)


def _mid_bwd(dx1b, ya, yc, logits, att, c1, w_a, w_c, w_o, gate_b, ln_g, ln_b, head_ones, dep=None):
    T = dx1b.shape[0]
    tm = MID_TM

    def body(dx_ref, ya_ref, yc_ref, lg_ref, att_ref, c1_ref, wa_ref, wc_ref, wo_ref, gb_ref, lng_ref, lnb_ref, e_ref,
             dlg_ref, dya_ref, dyc_ref, datt_ref, dsum_ref, dc1_ref, ggb_ref, glg_ref, glb_ref):
        @pl.when(pl.program_id(0) == 0)
        def _():
            ggb_ref[...] = jnp.zeros_like(ggb_ref)
            glg_ref[...] = jnp.zeros_like(glg_ref)
            glb_ref[...] = jnp.zeros_like(glb_ref)

        dmix = _dot_nt(dx_ref[...], wo_ref[...])
        gates = _sigmoid(lg_ref[...] + gb_ref[...])
        ga, gc = gates[:, :D_MODEL], gates[:, D_MODEL:]
        dla = dmix * ya_ref[...].astype(F32) * ga * (1.0 - ga)
        dlc = dmix * yc_ref[...].astype(F32) * gc * (1.0 - gc)
        dlg_ref[:, :D_MODEL] = dla.astype(BF16)
        dlg_ref[:, D_MODEL:] = dlc.astype(BF16)
        ggb_ref[:, :D_MODEL] += _rowsum(dla)
        ggb_ref[:, D_MODEL:] += _rowsum(dlc)
        dya = (dmix * ga).astype(BF16)
        dyc = (dmix * gc).astype(BF16)
        dya_ref[...] = dya
        dyc_ref[...] = dyc
        datt = _dot_nt(dya, wa_ref[...])
        datt_ref[...] = datt
        dsum_ref[...] = jnp.dot(datt * att_ref[...].astype(F32), e_ref[...], preferred_element_type=F32,
                                precision=lax.Precision.HIGHEST)
        dc3 = _dot_nt(dyc, wc_ref[...])
        xh, rs = _layernorm_stats(c1_ref[...])
        c2 = xh * lng_ref[...] + lnb_ref[...]
        sg = _sigmoid(c2)
        dc2 = dc3 * (sg * (1.0 + c2 * (1.0 - sg)))
        glg_ref[...] += _rowsum(dc2 * xh)
        glb_ref[...] += _rowsum(dc2)
        dxh = dc2 * lng_ref[...]
        dc1_ref[...] = rs * (dxh - jnp.mean(dxh, axis=-1, keepdims=True) - xh * jnp.mean(dxh * xh, axis=-1, keepdims=True))

    row = lambda n: pl.BlockSpec((tm, n), lambda i: (i, 0))
    full = lambda a, b: pl.BlockSpec((a, b), lambda i: (0, 0))
    body, dep_spec, dep_arg = _anchored(body, 13, dep)
    return pl.pallas_call(
        body, grid=(T // tm,),
        in_specs=[row(D_MODEL), row(D_MODEL), row(D_MODEL), row(2 * D_MODEL), row(ATTN_OUT), row(D_MODEL),
                  full(ATTN_OUT, D_MODEL), full(D_MODEL, D_MODEL), full(D_MODEL, D_MODEL),
                  full(1, 2 * D_MODEL), full(1, D_MODEL), full(1, D_MODEL), full(ATTN_OUT, ATTN_OUT)] + dep_spec,
        out_specs=[row(2 * D_MODEL), row(D_MODEL), row(D_MODEL), row(ATTN_OUT), row(ATTN_OUT), row(D_MODEL),
                   full(1, 2 * D_MODEL), full(1, D_MODEL), full(1, D_MODEL)],
        out_shape=[SDS((T, 2 * D_MODEL), BF16), SDS((T, D_MODEL), BF16), SDS((T, D_MODEL), BF16), SDS((T, ATTN_OUT), F32),
                   SDS((T, ATTN_OUT), F32), SDS((T, D_MODEL), F32),
                   SDS((1, 2 * D_MODEL), F32), SDS((1, D_MODEL), F32), SDS((1, D_MODEL), F32)],
        compiler_params=_params("arbitrary"), name="mid_bwd")(dx1b, ya, yc, logits, att, c1, w_a, w_c, w_o, gate_b, ln_g, ln_b,
                                                               head_ones, *dep_arg)


FFN_TM = 512
FFN_TF = D_FF // 2


def _rms_bwd(dy_times_g, xh, r):
    return r * (dy_times_g - xh * jnp.mean(dy_times_g * xh, axis=-1, keepdims=True))


def _ffn_fwd(h2, x1, target, gf, w_g, w_u, w_d):
    T = h2.shape[0]
    tm, tf = FFN_TM, FFN_TF
    nf = D_FF // tf

    def body(h_ref, x1_ref, t_ref, gf_ref, wg_ref, wu_ref, wd_ref,
             a_ref, b_ref, f_ref, dx2_ref, dx2b_ref, loss_ref, gnf_ref, acc):
        i, j = pl.program_id(0), pl.program_id(1)
        h = h_ref[...]
        a = _dot(h, wg_ref[...])
        b = _dot(h, wu_ref[...])
        f = (a * _sigmoid(a) * b).astype(BF16)
        a_ref[...] = a.astype(BF16)
        b_ref[...] = b.astype(BF16)
        f_ref[...] = f
        p = _dot(f, wd_ref[...])

        @pl.when(j == 0)
        def _():
            acc[...] = x1_ref[...] + p

        @pl.when(j > 0)
        def _():
            acc[...] += p

        @pl.when((i == 0) & (j == nf - 1))
        def _():
            loss_ref[...] = jnp.zeros_like(loss_ref)
            gnf_ref[...] = jnp.zeros_like(gnf_ref)

        @pl.when(j == nf - 1)
        def _():
            x2 = acc[...]
            r = lax.rsqrt(jnp.mean(x2 * x2, axis=-1, keepdims=True) + RMS_EPS)
            xh = x2 * r
            err = xh * gf_ref[...] - t_ref[...]
            loss_ref[...] += (0.5 / D_MODEL) * jnp.sum(err * err)
            dy = err * (1.0 / D_MODEL)
            gnf_ref[...] += _rowsum(dy * xh)
            dx2 = _rms_bwd(dy * gf_ref[...], xh, r)
            dx2_ref[...] = dx2
            dx2b_ref[...] = dx2.astype(BF16)

    row = lambda n: pl.BlockSpec((tm, n), lambda i, j: (i, 0))
    ffb = pl.BlockSpec((tm, tf), lambda i, j: (i, j))
    return pl.pallas_call(
        body, grid=(T // tm, nf),
        in_specs=[row(D_MODEL), row(D_MODEL), row(D_MODEL), pl.BlockSpec((1, D_MODEL), lambda i, j: (0, 0)),
                  pl.BlockSpec((D_MODEL, tf), lambda i, j: (0, j)), pl.BlockSpec((D_MODEL, tf), lambda i, j: (0, j)),
                  pl.BlockSpec((tf, D_MODEL), lambda i, j: (j, 0))],
        out_specs=[ffb, ffb, ffb, row(D_MODEL), row(D_MODEL),
                   pl.BlockSpec((1, 128), lambda i, j: (0, 0)), pl.BlockSpec((1, D_MODEL), lambda i, j: (0, 0))],
        out_shape=[SDS((T, D_FF), BF16), SDS((T, D_FF), BF16), SDS((T, D_FF), BF16), SDS((T, D_MODEL), F32),
                   SDS((T, D_MODEL), BF16), SDS((1, 128), F32), SDS((1, D_MODEL), F32)],
        scratch_shapes=[pltpu.VMEM((tm, D_MODEL), F32)],
        compiler_params=_params("arbitrary", "arbitrary"), name="ffn_fwd")(h2, x1, target, gf, w_g, w_u, w_d)


def _ffn_bwd(dx2b, dx2, a, b, x1, g2, w_g, w_u, w_d):
    T = dx2.shape[0]
    tm, tf = FFN_TM, FFN_TF
    nf = D_FF // tf

    def body(dxb_ref, dx2_ref, a_ref, b_ref, x1_ref, g2_ref, wg_ref, wu_ref, wd_ref,
             da_ref, db_ref, dx1_ref, dx1b_ref, gn2_ref, acc):
        i, j = pl.program_id(0), pl.program_id(1)
        df = _dot_nt(dxb_ref[...], wd_ref[...])
        av = a_ref[...].astype(F32)
        bv = b_ref[...].astype(F32)
        sg = _sigmoid(av)
        db = (df * av * sg).astype(BF16)
        da = (df * bv * (sg * (1.0 + av * (1.0 - sg)))).astype(BF16)
        da_ref[...] = da
        db_ref[...] = db
        p = _dot_nt(da, wg_ref[...]) + _dot_nt(db, wu_ref[...])

        @pl.when(j == 0)
        def _():
            acc[...] = p

        @pl.when(j > 0)
        def _():
            acc[...] += p

        @pl.when((i == 0) & (j == nf - 1))
        def _():
            gn2_ref[...] = jnp.zeros_like(gn2_ref)

        @pl.when(j == nf - 1)
        def _():
            dh2 = acc[...]
            x1 = x1_ref[...]
            r = lax.rsqrt(jnp.mean(x1 * x1, axis=-1, keepdims=True) + RMS_EPS)
            xh = x1 * r
            gn2_ref[...] += _rowsum(dh2 * xh)
            dx1 = dx2_ref[...] + _rms_bwd(dh2 * g2_ref[...], xh, r)
            dx1_ref[...] = dx1
            dx1b_ref[...] = dx1.astype(BF16)

    row = lambda n: pl.BlockSpec((tm, n), lambda i, j: (i, 0))
    ffb = pl.BlockSpec((tm, tf), lambda i, j: (i, j))
    return pl.pallas_call(
        body, grid=(T // tm, nf),
        in_specs=[row(D_MODEL), row(D_MODEL), ffb, ffb, row(D_MODEL), pl.BlockSpec((1, D_MODEL), lambda i, j: (0, 0)),
                  pl.BlockSpec((D_MODEL, tf), lambda i, j: (0, j)), pl.BlockSpec((D_MODEL, tf), lambda i, j: (0, j)),
                  pl.BlockSpec((tf, D_MODEL), lambda i, j: (j, 0))],
        out_specs=[ffb, ffb, row(D_MODEL), row(D_MODEL), pl.BlockSpec((1, D_MODEL), lambda i, j: (0, 0))],
        out_shape=[SDS((T, D_FF), BF16), SDS((T, D_FF), BF16), SDS((T, D_MODEL), F32), SDS((T, D_MODEL), BF16),
                   SDS((1, D_MODEL), F32)],
        scratch_shapes=[pltpu.VMEM((tm, D_MODEL), F32)],
        compiler_params=_params("arbitrary", "arbitrary"), name="ffn_bwd")(dx2b, dx2, a, b, x1, g2, w_g, w_u, w_d)


def _in_bwd(pieces, w_in, x, dx1, g1, dep=None):
    T = x.shape[0]
    tm = IN_TM
    npc = len(pieces)
    assert sum(p.shape[1] for p in pieces) == IN_WIDTH

    def body(*refs):
        p_refs = refs[:npc]
        w_hbm, x_ref, dx1_ref, g_ref, dx_ref, gn1_ref, w_vmem, sem = refs[npc:]

        @pl.when(pl.program_id(0) == 0)
        def _():
            cp = pltpu.make_async_copy(w_hbm, w_vmem, sem)
            cp.start()
            cp.wait()
            gn1_ref[...] = jnp.zeros_like(gn1_ref)

        dh = jnp.zeros((tm, D_MODEL), F32)
        col = 0
        for p_ref in p_refs:
            for j in range(p_ref.shape[1] // IN_CHUNK):
                dh = dh + _dot_nt(p_ref[:, j * IN_CHUNK:(j + 1) * IN_CHUNK], w_vmem[:, col:col + IN_CHUNK])
                col += IN_CHUNK
        xv = x_ref[...]
        r = lax.rsqrt(jnp.mean(xv * xv, axis=-1, keepdims=True) + RMS_EPS)
        xh = xv * r
        gn1_ref[...] += _rowsum(dh * xh)
        dx_ref[...] = dx1_ref[...] + _rms_bwd(dh * g_ref[...], xh, r)

    row = lambda n: pl.BlockSpec((tm, n), lambda i: (i, 0))
    body, dep_spec, dep_arg = _anchored(body, npc + 4, dep)
    return pl.pallas_call(
        body, grid=(T // tm,),
        in_specs=[row(p.shape[1]) for p in pieces]
        + [pl.BlockSpec(memory_space=pl.ANY), row(D_MODEL), row(D_MODEL), pl.BlockSpec((1, D_MODEL), lambda i: (0, 0))]
        + dep_spec,
        out_specs=[row(D_MODEL), pl.BlockSpec((1, D_MODEL), lambda i: (0, 0))],
        out_shape=[SDS((T, D_MODEL), F32), SDS((1, D_MODEL), F32)],
        scratch_shapes=[pltpu.VMEM((D_MODEL, IN_WIDTH), BF16), pltpu.SemaphoreType.DMA],
        compiler_params=_params("arbitrary"), name="in_bwd")(*pieces, w_in, x, dx1, g1, *dep_arg)


def _local_step(x, target, w, small, dep=None, late_weights=None, emit=None):
    T = x.shape[0]
    batch = T // SEQ
    slopes_r = jnp.asarray(_slopes_times_dilation())
    emit = emit or (lambda names, grads: None)

    h, qkv, u, logits = _in_proj(x, small["norm1_g"], w["w_in"], dep)

    qkv3 = qkv.reshape(batch, SEQ, 3 * ATTN_WIDTH)
    att, lse = _attn_fwd(qkv3, slopes_r, batch)
    att = att.reshape(T, ATTN_OUT)

    u3 = u.reshape(batch, SEQ, 2 * D_MODEL)
    c1 = _conv_fwd(u3, w["conv_w"], small["conv_b"], batch).reshape(T, D_MODEL)
    if late_weights is not None:
        w = {**w, **late_weights(c1)}

    c3, ya, yc, mix, x1, h2 = _mid_fwd(
        att, c1, logits, x, w["w_attn_out"], w["w_conv_out"], w["w_o"],
        small["gate_b"], small["conv_ln_g"], small["conv_ln_b"], small["norm2_g"])

    a, b, f, dx2, dx2b, loss, g_normf = _ffn_fwd(h2, x1, target, small["norm_f_g"],
                                                   w["w_ffn_gate"], w["w_ffn_up"], w["w_ffn_down"])

    da, db, dx1, dx1b, g_norm2 = _ffn_bwd(dx2b, dx2, a, b, x1, small["norm2_g"],
                                           w["w_ffn_gate"], w["w_ffn_up"], w["w_ffn_down"])
    gw = {}
    gw["w_ffn_down"] = _mm_tn(f, dx2b, BF16, "gw_ffn_down", tn=512)
    gw["w_ffn_gate"] = _mm_tn(h2, da, BF16, "gw_ffn_gate", tn=1408)
    gw["w_ffn_up"] = _mm_tn(h2, db, BF16, "gw_ffn_up", tn=1408)
    token = emit(("w_ffn_gate", "w_ffn_up", "w_ffn_down"), gw)

    head_ones = jnp.asarray(np.kron(np.eye(HEADS_PER_GROUP, dtype=np.float32), np.ones((HEAD_DIM, HEAD_DIM), np.float32)))
    dlogits, dya, dyc, datt, dsum, dc1, g_gate_b, g_ln_g, g_ln_b = _mid_bwd(
        dx1b, ya, yc, logits, att, c1, w["w_attn_out"], w["w_conv_out"], w["w_o"],
        small["gate_b"], small["conv_ln_g"], small["conv_ln_b"], head_ones, token)
    gw["w_o"] = _mm_tn(mix, dx1b, BF16, "gw_o", tn=512)
    gw["w_attn_out"] = _mm_tn(att, dya, BF16, "gw_attn_out", tn=512)
    gw["w_conv_out"] = _mm_tn(c3, dyc, BF16, "gw_conv_out", tn=512)
    token = emit(("w_conv_out", "w_attn_out", "w_o"), gw)

    dua, dub, g_conv_w, g_conv_b = _conv_bwd(u3, dc1.reshape(batch, SEQ, D_MODEL), w["conv_w"], batch, token)

    dq, dk, dv = _attn_bwd(qkv3, datt.reshape(batch, SEQ, ATTN_OUT), lse, dsum.reshape(batch, SEQ, ATTN_OUT),
                           slopes_r, batch)
    pieces = [dq.reshape(T, ATTN_WIDTH), dk.reshape(T, ATTN_WIDTH), dv.reshape(T, ATTN_WIDTH),
              dua.reshape(T, D_MODEL), dub.reshape(T, D_MODEL), dlogits]

    names = ("q", "k", "v", "ua", "ub", "gate")
    gw["w_in"] = jnp.concatenate(
        [_mm_tn(h, p, BF16, "gw_in_" + nm, tn=min(p.shape[1], 1024) if p.shape[1] != ATTN_WIDTH else 768)
         for nm, p in zip(names, pieces)], axis=1)
    gw["conv_w"] = g_conv_w
    token = emit(("w_in", "conv_w"), gw)
    grad_x, g_norm1 = _in_bwd(pieces, w["w_in"], x, dx1, small["norm1_g"], token)

    gsmall = {"norm1_g": g_norm1, "gate_b": g_gate_b, "conv_b": g_conv_b, "conv_ln_g": g_ln_g, "conv_ln_b": g_ln_b,
              "norm2_g": g_norm2, "norm_f_g": g_normf}
    return loss, grad_x, gw, gsmall


ANY = pl.BlockSpec(memory_space=pl.ANY)


def _all_gather(arrs):
    n = len(arrs)

    def body(*refs):
        ins, outs = refs[:n], refs[n:2 * n]
        send_sems, recv_sems, local_sems = refs[2 * n:]
        x, y, c = lax.axis_index("x"), lax.axis_index("y"), lax.axis_index("c")
        me, sibling = (x, y, c), (x, y, 1 - c)
        chips = [(1 - x, y), (x, 1 - y), (1 - x, 1 - y)]

        def copy(a, k, block, to, src=None):
            px, py, pc = block
            dst = outs[a].at[4 * px + 2 * py + pc]
            return pltpu.make_async_remote_copy(
                src_ref=dst if src is None else src, dst_ref=dst,
                send_sem=send_sems.at[a, k], recv_sem=recv_sems.at[a, k], device_id=to, device_id_type=MESH)

        mine = [pltpu.make_async_copy(ins[a], outs[a].at[4 * x + 2 * y + c], local_sems.at[a]) for a in range(n)]
        for cp in mine:
            cp.start()
        first = []
        for j, chip in enumerate(chips):
            first += [copy(a, 1 + j, me, (*chip, c), src=ins[a]) for a in range(n)]
        first += [copy(a, 0, me, sibling, src=ins[a]) for a in range(n)]
        for cp in first:
            cp.start()
        passed = []
        for j, chip in enumerate(chips):
            for a in range(n):
                copy(a, 1 + j, (*chip, c), me).wait_recv()
                cp = copy(a, 4 + j, (*chip, c), sibling)
                cp.start()
                passed.append(cp)
        for a in range(n):
            copy(a, 0, sibling, me).wait_recv()
        for j, chip in enumerate(chips):
            for a in range(n):
                copy(a, 4 + j, (*chip, 1 - c), me).wait_recv()
        for cp in first + passed:
            cp.wait_send()
        for cp in mine:
            cp.wait()

    return pl.pallas_call(
        body, in_specs=[ANY] * n, out_specs=[ANY] * n,
        out_shape=[SDS((N_DEV,) + a.shape, a.dtype) for a in arrs],
        scratch_shapes=[pltpu.SemaphoreType.DMA((n, 7)), pltpu.SemaphoreType.DMA((n, 7)), pltpu.SemaphoreType.DMA((n,))],
        name="all_gather_weights")(*arrs)


HBM = pl.BlockSpec(memory_space=pltpu.HBM)
SEM = pl.BlockSpec(memory_space=pltpu.SEMAPHORE)
N_PEERS = N_DEV - 1


def _peer_copies(srcs, lands, send_sems, recv_sems, scatter):
    x, y, c = lax.axis_index("x"), lax.axis_index("y"), lax.axis_index("c")
    me = 4 * x + 2 * y + c
    send, recv = [], []
    for a in range(len(srcs)):
        for k in range(1, N_DEV):
            peer = (x ^ ((k >> 2) & 1), y ^ ((k >> 1) & 1), c ^ (k & 1))
            pidx = 4 * peer[0] + 2 * peer[1] + peer[2]
            src = srcs[a].at[pidx] if scatter else srcs[a]
            s = a * N_PEERS + k - 1
            send.append(pltpu.make_async_remote_copy(
                src_ref=src, dst_ref=lands[a].at[me], send_sem=send_sems.at[s], recv_sem=recv_sems.at[s],
                device_id=peer, device_id_type=MESH))
            recv.append(pltpu.make_async_remote_copy(
                src_ref=src, dst_ref=lands[a].at[pidx], send_sem=send_sems.at[s], recv_sem=recv_sems.at[s],
                device_id=peer, device_id_type=MESH))
    return send, recv


def _send_start(srcs, scatter, name):
    n = len(srcs)
    lands = [lax.empty((N_DEV,) + (s.shape[1:] if scatter else s.shape), s.dtype) for s in srcs]

    def body(*refs):
        send, _ = _peer_copies(refs[:n], refs[n:2 * n], refs[2 * n], refs[2 * n + 1], scatter)
        for cp in send:
            cp.start()
        token = refs[-1]
        token[...] = jnp.zeros_like(token)

    res = pl.pallas_call(
        body, name=name,
        out_shape=(pltpu.SemaphoreType.DMA((n * N_PEERS,)), pltpu.SemaphoreType.DMA((n * N_PEERS,)),
                   *[pltpu.HBM(s.shape, s.dtype) for s in srcs], *[pltpu.HBM(l.shape, l.dtype) for l in lands],
                   SDS((8, 128), F32)),
        in_specs=[HBM] * (2 * n), out_specs=(SEM, SEM, *([HBM] * (2 * n)), pl.BlockSpec(memory_space=pltpu.VMEM)),
        input_output_aliases={i: 2 + i for i in range(2 * n)},
        compiler_params=pltpu.CompilerParams(has_side_effects=pltpu.SideEffectType.DATAFLOW_SIDE_EFFECTING),
    )(*[pltpu.with_memory_space_constraint(s, pltpu.HBM) for s in srcs],
      *[pltpu.with_memory_space_constraint(l, pltpu.HBM) for l in lands])
    return dict(send_sems=res[0], recv_sems=res[1], srcs=res[2:2 + n], lands=res[2 + n:2 + 2 * n], token=res[-1])


def _send_wait(started, scatter, after, name):
    n = len(started["srcs"])

    def body(*refs):
        send, recv = _peer_copies(refs[:n], refs[n:2 * n], refs[2 * n], refs[2 * n + 1], scatter)
        for cp in send:
            cp.wait_send()
        for cp in recv:
            cp.wait_recv()

    both = list(started["srcs"]) + list(started["lands"])
    res = pl.pallas_call(
        body, name=name,
        out_shape=tuple(pltpu.HBM(a.shape, a.dtype) for a in both),
        in_specs=[HBM] * (2 * n) + [SEM, SEM, ANY], out_specs=tuple([HBM] * (2 * n)),
        input_output_aliases={i: i for i in range(2 * n)},
        compiler_params=pltpu.CompilerParams(has_side_effects=pltpu.SideEffectType.DATAFLOW_SIDE_EFFECTING),
    )(*both, started["send_sems"], started["recv_sems"], after)
    return res[:n], res[n:]


def _row_tile(rows, cols, itemsize_total):
    budget = (4 << 20) // max(1, cols * itemsize_total)
    if rows <= budget:
        return rows
    t = rows
    while t > budget and t % 2 == 0 and (t // 2) % 16 == 0:
        t //= 2
    return t


def _adam_math(g, w, m, v):
    m_new = ADAM_B1 * m + (1.0 - ADAM_B1) * g
    v_new = ADAM_B2 * v + (1.0 - ADAM_B2) * (g * g)
    m_hat = m_new / (1.0 - ADAM_B1 ** ADAM_STEP)
    v_hat = v_new / (1.0 - ADAM_B2 ** ADAM_STEP)
    delta = -ADAM_LR * (m_hat / (jnp.sqrt(v_hat) + ADAM_EPS) + ADAM_WD * w)
    return delta, m_new, v_new


def _sum_adam(parts, w, m, v, name):
    rows, cols = w.shape
    nparts = parts.shape[0]
    tr = _row_tile(rows, cols, nparts * parts.dtype.itemsize + 7 * 4)

    def body(p_ref, w_ref, m_ref, v_ref, g_ref, d_ref, mo_ref, vo_ref):
        g = p_ref[0].astype(F32)
        for s in range(1, nparts):
            g = g + p_ref[s].astype(F32)
        delta, m_new, v_new = _adam_math(g, w_ref[...], m_ref[...], v_ref[...])
        g_ref[...] = g
        d_ref[...] = delta
        mo_ref[...] = m_new
        vo_ref[...] = v_new

    blk = pl.BlockSpec((tr, cols), lambda i: (i, 0))
    out = SDS((rows, cols), F32)
    return pl.pallas_call(
        body, grid=(rows // tr,),
        in_specs=[pl.BlockSpec((nparts, tr, cols), lambda i: (0, i, 0)), blk, blk, blk],
        out_specs=[blk, blk, blk, blk], out_shape=[out, out, out, out],
        compiler_params=_params("parallel"), name=name)(parts, w, m, v)


SMALL_ROWS = 64


def _small_allreduce_adam(gpart, w, m, v):
    def body(g_ref, w_ref, m_ref, v_ref, go_ref, d_ref, mo_ref, vo_ref, gath, send_sems, recv_sems):
        x, y, c = lax.axis_index("x"), lax.axis_index("y"), lax.axis_index("c")
        me = 4 * x + 2 * y + c
        gath[me] = g_ref[...]
        copies = []
        for k in range(1, N_DEV):
            fx, fy, fc = (k >> 2) & 1, (k >> 1) & 1, k & 1
            peer = (x ^ fx, y ^ fy, c ^ fc)
            copies.append(pltpu.make_async_remote_copy(
                src_ref=gath.at[me], dst_ref=gath.at[me], send_sem=send_sems.at[k - 1], recv_sem=recv_sems.at[k - 1],
                device_id=peer, device_id_type=MESH))
        for cp in copies:
            cp.start()
        for cp in copies:
            cp.wait_recv()
        for cp in copies:
            cp.wait_send()
        g = gath[0]
        for d in range(1, N_DEV):
            g = g + gath[d]
        delta, m_new, v_new = _adam_math(g, w_ref[...], m_ref[...], v_ref[...])
        go_ref[...] = g
        d_ref[...] = delta
        mo_ref[...] = m_new
        vo_ref[...] = v_new

    vm = pl.BlockSpec(memory_space=pltpu.VMEM)
    out = SDS((SMALL_ROWS, 128), F32)
    return pl.pallas_call(
        body, in_specs=[vm] * 4, out_specs=[vm] * 4, out_shape=[out] * 4,
        scratch_shapes=[pltpu.VMEM((N_DEV, SMALL_ROWS, 128), F32), pltpu.SemaphoreType.DMA((N_DEV - 1,)),
                        pltpu.SemaphoreType.DMA((N_DEV - 1,))],
        name="small_allreduce_adam")(gpart, w, m, v)


BIG = ("w_in", "conv_w", "w_conv_out", "w_attn_out", "w_o", "w_ffn_gate", "w_ffn_up", "w_ffn_down")
EARLY = ("w_in", "conv_w")
LATE = ("w_conv_out", "w_attn_out", "w_o", "w_ffn_gate", "w_ffn_up", "w_ffn_down")
COL_SHARDED = ("w_in", "conv_w", "w_attn_out", "w_ffn_gate", "w_ffn_up")
SMALL = ("norm1_g", "gate_b", "conv_b", "conv_ln_g", "conv_ln_b", "norm2_g", "norm_f_g")
WEIGHTS = ("norm1_g", "w_in", "gate_b", "conv_w", "conv_b", "conv_ln_g", "conv_ln_b", "w_conv_out", "w_attn_out", "w_o",
           "norm2_g", "w_ffn_gate", "w_ffn_up", "w_ffn_down", "norm_f_g")


def _shard2d(name, a):
    a = a.reshape(a.shape[-2], a.shape[-1])
    if name == "conv_w":
        a = jnp.pad(a, ((0, CONV_PAD - CONV_K), (0, 0)))
    return a


def _gathered_to_full(name, g):
    if name in COL_SHARDED:
        return g.transpose(1, 0, 2).reshape(g.shape[1], N_DEV * g.shape[2])
    return g.reshape(N_DEV * g.shape[1], g.shape[2])


def _full_to_blocks(name, g):
    if name in COL_SHARDED:
        return g.reshape(g.shape[0], N_DEV, g.shape[1] // N_DEV).transpose(1, 0, 2)
    return g.reshape(N_DEV, g.shape[0] // N_DEV, g.shape[1])


def _pack_small(d):
    return jnp.concatenate([d[n].reshape(-1) for n in SMALL]).reshape(SMALL_ROWS, 128)


def _unpack_small(p, like):
    flat = p.reshape(-1)
    out, off = {}, 0
    for n in SMALL:
        size = like[n].size
        out[n] = flat[off:off + size].reshape(like[n].shape)
        off += size
    return out


def kernel(x, norm1_g, w_in, gate_b, conv_w, conv_b, conv_ln_g, conv_ln_b, w_conv_out, w_attn_out, w_o, norm2_g, w_ffn_gate, w_ffn_up, w_ffn_down, norm_f_g, loss_target, m_norm1_g, m_w_in, m_gate_b, m_conv_w, m_conv_b, m_conv_ln_g, m_conv_ln_b, m_w_conv_out, m_w_attn_out, m_w_o, m_norm2_g, m_w_ffn_gate, m_w_ffn_up, m_w_ffn_down, m_norm_f_g, v_norm1_g, v_w_in, v_gate_b, v_conv_w, v_conv_b, v_conv_ln_g, v_conv_ln_b, v_w_conv_out, v_w_attn_out, v_w_o, v_norm2_g, v_w_ffn_gate, v_w_ffn_up, v_w_ffn_down, v_norm_f_g):
    wts = dict(norm1_g=norm1_g, w_in=w_in, gate_b=gate_b, conv_w=conv_w, conv_b=conv_b, conv_ln_g=conv_ln_g,
               conv_ln_b=conv_ln_b, w_conv_out=w_conv_out, w_attn_out=w_attn_out, w_o=w_o, norm2_g=norm2_g,
               w_ffn_gate=w_ffn_gate, w_ffn_up=w_ffn_up, w_ffn_down=w_ffn_down, norm_f_g=norm_f_g)
    mom1 = dict(norm1_g=m_norm1_g, w_in=m_w_in, gate_b=m_gate_b, conv_w=m_conv_w, conv_b=m_conv_b, conv_ln_g=m_conv_ln_g,
                conv_ln_b=m_conv_ln_b, w_conv_out=m_w_conv_out, w_attn_out=m_w_attn_out, w_o=m_w_o, norm2_g=m_norm2_g,
                w_ffn_gate=m_w_ffn_gate, w_ffn_up=m_w_ffn_up, w_ffn_down=m_w_ffn_down, norm_f_g=m_norm_f_g)
    mom2 = dict(norm1_g=v_norm1_g, w_in=v_w_in, gate_b=v_gate_b, conv_w=v_conv_w, conv_b=v_conv_b, conv_ln_g=v_conv_ln_g,
                conv_ln_b=v_conv_ln_b, w_conv_out=v_w_conv_out, w_attn_out=v_w_attn_out, w_o=v_w_o, norm2_g=v_norm2_g,
                w_ffn_gate=v_w_ffn_gate, w_ffn_up=v_w_ffn_up, w_ffn_down=v_w_ffn_down, norm_f_g=v_norm_f_g)

    T = x.shape[0] * x.shape[1]
    x2 = x.reshape(T, D_MODEL)
    t2 = loss_target.reshape(T, D_MODEL)

    me = 4 * lax.axis_index("x") + 2 * lax.axis_index("y") + lax.axis_index("c")
    shards = {n: _shard2d(n, wts[n]) for n in BIG}
    sent = {n: shards[n] if n == "conv_w" else shards[n].astype(BF16) for n in BIG}
    small = {n: wts[n].reshape(1, -1) for n in SMALL}

    gathered = _all_gather([sent[n] for n in EARLY])
    full = {n: _gathered_to_full(n, g) for n, g in zip(EARLY, gathered)}
    late_gather = _send_start([sent[n] for n in LATE], False, "gather_late_start")

    def late_weights(after):
        srcs, lands = _send_wait(late_gather, False, after, "gather_late_wait")
        return {n: _gathered_to_full(n, lax.dynamic_update_slice(land, src[None], (me, 0, 0)))
                for n, src, land in zip(LATE, srcs, lands)}

    scatters = []

    def emit(names, gw):
        started = _send_start([_full_to_blocks(n, gw[n]) for n in names], True, "scatter_start_" + names[0])
        scatters.append((names, started))
        return started["token"]

    loss_part, grad_x, gw, gsmall = _local_step(x2, t2, full, small, late_gather["token"], late_weights, emit)

    grads, deltas, new_m, new_v = {}, {}, {}, {}
    for names, started in scatters:
        srcs, lands = _send_wait(started, True, grad_x, "scatter_wait_" + names[0])
        for n, src, land in zip(names, srcs, lands):
            parts = lax.dynamic_update_slice(land, lax.dynamic_slice_in_dim(src, me, 1, axis=0), (me, 0, 0))
            g, d, mo, vo = _sum_adam(parts, shards[n], _shard2d(n, mom1[n]), _shard2d(n, mom2[n]), "adam_" + n)
            for dst, val in ((grads, g), (deltas, d), (new_m, mo), (new_v, vo)):
                if n == "conv_w":
                    val = val[:CONV_K]
                dst[n] = val.reshape(wts[n].shape)

    sg, sd, sm, sv = _small_allreduce_adam(_pack_small(gsmall), _pack_small(wts), _pack_small(mom1), _pack_small(mom2))
    for dst, val in ((grads, sg), (deltas, sd), (new_m, sm), (new_v, sv)):
        dst.update(_unpack_small(val, wts))

    loss = lax.psum(loss_part[0, 0], ("x", "y", "c"))
    return (loss, grad_x.reshape(x.shape), *[grads[n] for n in WEIGHTS], *[deltas[n] for n in WEIGHTS],
            *[new_m[n] for n in WEIGHTS], *[new_v[n] for n in WEIGHTS])
```

```python
import math

import numpy as np
import jax
import jax.numpy as jnp
from jax import lax
from jax.experimental import pallas as pl
from jax.experimental.pallas import tpu as pltpu

F32 = jnp.float32
BF16 = jnp.bfloat16
SDS = jax.ShapeDtypeStruct
MESH = pl.DeviceIdType.MESH

D_MODEL = 1024
SEQ = 2048
HEAD_DIM = 64
GROUPS = ((128, 1), (512, 4), (2048, 16))
HEADS_PER_GROUP = 8
N_HEADS = 24
ATTN_WIDTH = N_HEADS * HEAD_DIM
ATTN_OUT = HEADS_PER_GROUP * HEAD_DIM
CONV_K = 31
CONV_PAD = 32
D_FF = 2816
IN_WIDTH = 3 * ATTN_WIDTH + 2 * D_MODEL + 2 * D_MODEL
RMS_EPS = 1e-6
LN_EPS = 1e-5
Q_BLOCK = 128
LANES = 128
NEG = -1e30
N_DEV = 8

ADAM_LR = 0.001
ADAM_B1 = 0.9
ADAM_B2 = 0.999
ADAM_EPS = 1e-08
ADAM_WD = 0.01
ADAM_STEP = 10


def _alibi_slope_list(n):
    def pow2(m):
        start = 2.0 ** (-8.0 / m)
        return [start ** (i + 1) for i in range(m)]
    if math.log2(n).is_integer():
        return pow2(n)
    c = 2 ** math.floor(math.log2(n))
    return pow2(c) + _alibi_slope_list(2 * c)[0::2][: n - c]


def _slopes_times_dilation():
    s = np.asarray(sorted(_alibi_slope_list(N_HEADS), reverse=True), dtype=np.float32).reshape(3, HEADS_PER_GROUP)
    r = np.asarray([g[1] for g in GROUPS], dtype=np.float32)[:, None]
    return (s * r).reshape(N_HEADS)


def _sigmoid(x):
    return 1.0 / (1.0 + jnp.exp(-x))


def _dot(a, b):
    return jnp.dot(a, b, preferred_element_type=F32)


def _dot_nt(a, b):
    return lax.dot_general(a, b, (((1,), (1,)), ((), ())), preferred_element_type=F32)


def _dot_tn(a, b):
    return lax.dot_general(a, b, (((0,), (0,)), ((), ())), preferred_element_type=F32)


def _rowsum(x):
    return jnp.sum(x, axis=0, keepdims=True)


def _params(*sem):
    return pltpu.CompilerParams(dimension_semantics=sem)


def _anchored(body, n_in, dep):
    if dep is None:
        return body, [], []

    def wrapped(*refs):
        return body(*refs[:n_in], *refs[n_in + 1:])

    return wrapped, [pl.BlockSpec(memory_space=pl.ANY)], [dep]


IN_TM = 256
IN_CHUNK = 512


def _in_proj(x, g1, w_in, dep=None):
    T = x.shape[0]
    tm = IN_TM
    widths = (3 * ATTN_WIDTH, 2 * D_MODEL, 2 * D_MODEL)

    def body(x_ref, g_ref, w_hbm, h_ref, qkv_ref, u_ref, lg_ref, w_vmem, sem):
        @pl.when(pl.program_id(0) == 0)
        def _():
            cp = pltpu.make_async_copy(w_hbm, w_vmem, sem)
            cp.start()
            cp.wait()

        xv = x_ref[...]
        r = lax.rsqrt(jnp.mean(xv * xv, axis=-1, keepdims=True) + RMS_EPS)
        h = (xv * r * g_ref[...]).astype(BF16)
        h_ref[...] = h
        col = 0
        for o_ref, width in zip((qkv_ref, u_ref, lg_ref), widths):
            for j in range(width // IN_CHUNK):
                o_ref[:, j * IN_CHUNK:(j + 1) * IN_CHUNK] = _dot(h, w_vmem[:, col:col + IN_CHUNK])
                col += IN_CHUNK

    row = lambda n: pl.BlockSpec((tm, n), lambda i: (i, 0))
    body, dep_spec, dep_arg = _anchored(body, 3, dep)
    return pl.pallas_call(
        body, grid=(T // tm,),
        in_specs=[row(D_MODEL), pl.BlockSpec((1, D_MODEL), lambda i: (0, 0)), pl.BlockSpec(memory_space=pl.ANY)] + dep_spec,
        out_specs=[row(D_MODEL)] + [row(n) for n in widths],
        out_shape=[SDS((T, D_MODEL), BF16)] + [SDS((T, n), F32) for n in widths],
        scratch_shapes=[pltpu.VMEM((D_MODEL, IN_WIDTH), BF16), pltpu.SemaphoreType.DMA],
        compiler_params=_params("arbitrary"), name="in_proj")(x, g1, w_in, *dep_arg)


def _mm_tn(a, b, out_dtype, name, tn, tt=512):
    T, K = a.shape
    N = b.shape[1]
    nt = T // tt

    def body(a_ref, b_ref, o_ref, acc):
        t = pl.program_id(1)
        p = _dot_tn(a_ref[...], b_ref[...])

        @pl.when(t == 0)
        def _():
            acc[...] = p

        @pl.when(t > 0)
        def _():
            acc[...] += p

        @pl.when(t == nt - 1)
        def _():
            o_ref[...] = acc[...].astype(o_ref.dtype)

    return pl.pallas_call(
        body, grid=(N // tn, nt),
        in_specs=[pl.BlockSpec((tt, K), lambda j, t: (t, 0)),
                  pl.BlockSpec((tt, tn), lambda j, t: (t, j))],
        out_specs=pl.BlockSpec((K, tn), lambda j, t: (0, j)),
        out_shape=SDS((K, N), out_dtype),
        scratch_shapes=[pltpu.VMEM((K, tn), F32)],
        compiler_params=_params("parallel", "arbitrary"), name=name)(a, b)


def _gather_classes(src_ref, dst, r):
    L = SEQ // r
    for c in range(r):
        dst[c * L:(c + 1) * L, :] = src_ref[0, pl.ds(c, L, stride=r), :].astype(dst.dtype)


def _scatter_classes(src, dst, r):
    L = SEQ // r
    for c in range(r):
        dst[pl.ds(c, L, stride=r), :] = src[c * L:(c + 1) * L, :].astype(dst.dtype)


def _attn_masks(slope_r):
    qi = lax.broadcasted_iota(jnp.int32, (Q_BLOCK, Q_BLOCK), 0)
    kj = lax.broadcasted_iota(jnp.int32, (Q_BLOCK, Q_BLOCK), 1)
    rel = (qi - kj).astype(F32)
    bias_cur = jnp.where(qi >= kj, -slope_r * rel, NEG)
    bias_prev = jnp.where(qi <= kj, -slope_r * (rel + float(Q_BLOCK)), NEG)
    return bias_cur, bias_prev


def _store_biases(bias, sl_ref, g, hp):
    for hh in range(2):
        cur, prev = _attn_masks(sl_ref[g * HEADS_PER_GROUP + 2 * hp + hh])
        bias[0, hh * Q_BLOCK:(hh + 1) * Q_BLOCK, :] = cur
        bias[1, hh * Q_BLOCK:(hh + 1) * Q_BLOCK, :] = prev


def _transpose_blocks(src, dst):
    for b in range(SEQ // Q_BLOCK):
        dst[b] = src[b * Q_BLOCK:(b + 1) * Q_BLOCK, :].T


def _stack_heads(t, low):
    z = jnp.zeros_like(t)
    return jnp.concatenate([jnp.where(low, t, z), jnp.where(low, z, t)], axis=0)


def _unstack_heads(t2, low):
    return jnp.where(low, t2[:Q_BLOCK], t2[Q_BLOCK:])


def _unit_offsets(u, nb):
    off = pl.multiple_of(u * Q_BLOCK, Q_BLOCK)
    offp = pl.multiple_of(jnp.maximum(u - 1, 0) * Q_BLOCK, Q_BLOCK)
    n = u & (nb - 1)
    c = u >> int(math.log2(nb))
    return off, offp, n == 0, c, n


def _attn_fwd(qkv, slopes_r, batch):
    nblk = SEQ // Q_BLOCK

    def body(sl_ref, *refs):
        qkv_refs = refs[:9]
        att_ref, lse_ref = refs[9:11]
        qd, kd, vd, kt, opos, lpos, bias = refs[11:]
        hp = pl.program_id(1)
        low = lax.broadcasted_iota(jnp.int32, (Q_BLOCK, LANES), 1) < HEAD_DIM

        for g in range(3):
            r = GROUPS[g][1]
            nb = SEQ // r // Q_BLOCK
            _gather_classes(qkv_refs[3 * g], qd, r)
            _gather_classes(qkv_refs[3 * g + 1], kd, r)
            _gather_classes(qkv_refs[3 * g + 2], vd, r)
            _transpose_blocks(kd, kt)
            _store_biases(bias, sl_ref, g, hp)

            def unit(u, carry, g=g, r=r, nb=nb):
                off, offp, first, c, n = _unit_offsets(u, nb)
                q2 = _stack_heads(qd[pl.ds(off, Q_BLOCK), :], low)
                vc = vd[pl.ds(off, Q_BLOCK), :]
                vp = vd[pl.ds(offp, Q_BLOCK), :]
                sc = _dot(q2, kt[u]) * 0.125 + bias[0]
                sp = jnp.where(first, NEG, _dot(q2, kt[jnp.maximum(u - 1, 0)]) * 0.125 + bias[1])
                m = jnp.max(jnp.maximum(sc, sp), axis=-1, keepdims=True)
                pc = jnp.exp(sc - m)
                pp = jnp.exp(sp - m)
                l = jnp.sum(pc + pp, axis=-1, keepdims=True)
                o2 = (_dot(pc.astype(BF16), vc) + _dot(pp.astype(BF16), vp)) * (1.0 / l)
                lse2 = m + jnp.log(l)
                rows = pl.ds(c + n * (Q_BLOCK * r), Q_BLOCK, stride=r)
                opos[g, rows, :] = _unstack_heads(o2, low)
                lpos[g, rows, :] = jnp.where(low, lse2[:Q_BLOCK], lse2[Q_BLOCK:])
                return carry

            lax.fori_loop(0, nblk, unit, 0, unroll=2)

        def merge(i, carry):
            rows = pl.ds(pl.multiple_of(i * 256, 256), 256)
            l0, l1, l2 = lpos[0, rows, :], lpos[1, rows, :], lpos[2, rows, :]
            m = jnp.maximum(jnp.maximum(l0, l1), l2)
            e0, e1, e2 = jnp.exp(l0 - m), jnp.exp(l1 - m), jnp.exp(l2 - m)
            den = e0 + e1 + e2
            att = (e0 * opos[0, rows, :] + e1 * opos[1, rows, :] + e2 * opos[2, rows, :]) / den
            att_ref[0, rows, :] = att.astype(att_ref.dtype)
            lse_ref[0, rows, :] = m + jnp.log(den)
            return carry

        lax.fori_loop(0, SEQ // 256, merge, 0)

    def col(sec, g):
        return pl.BlockSpec((1, SEQ, LANES), lambda b, hp: (b, 0, sec * 12 + g * 4 + hp))

    out = pl.BlockSpec((1, SEQ, LANES), lambda b, hp: (b, 0, hp))
    return pl.pallas_call(
        body, grid=(batch, 4),
        in_specs=[pl.BlockSpec(memory_space=pltpu.SMEM)] + [col(sec, g) for g in range(3) for sec in range(3)],
        out_specs=[out, out],
        out_shape=[SDS((batch, SEQ, ATTN_OUT), BF16), SDS((batch, SEQ, ATTN_OUT), F32)],
        scratch_shapes=[pltpu.VMEM((SEQ, LANES), BF16), pltpu.VMEM((SEQ, LANES), BF16), pltpu.VMEM((SEQ, LANES), BF16),
                        pltpu.VMEM((nblk, LANES, Q_BLOCK), BF16),
                        pltpu.VMEM((3, SEQ, LANES), F32), pltpu.VMEM((3, SEQ, LANES), F32),
                        pltpu.VMEM((2, 2 * Q_BLOCK, Q_BLOCK), F32)],
        compiler_params=_params("parallel", "parallel"), name="attn_fwd")(slopes_r, *([qkv] * 9))


def _attn_bwd(qkv, datt, lse, dsum, slopes_r, batch):
    nblk = SEQ // Q_BLOCK

    def body(sl_ref, q_ref, k_ref, v_ref, do_ref, l_ref, d_ref, dq_ref, dk_ref, dv_ref,
             qd, kd, vd, dod, kt, vt, ld, dd, dq_acc, dk_acc, dv_acc, stage, bias):
        gid, hp = pl.program_id(1), pl.program_id(2)
        low = lax.broadcasted_iota(jnp.int32, (Q_BLOCK, LANES), 1) < HEAD_DIM

        def section(g):
            r = GROUPS[g][1]
            nb = SEQ // r // Q_BLOCK
            _gather_classes(q_ref, qd, r)
            _gather_classes(k_ref, kd, r)
            _gather_classes(v_ref, vd, r)
            _gather_classes(do_ref, dod, r)
            _gather_classes(l_ref, ld, r)
            _gather_classes(d_ref, dd, r)
            _transpose_blocks(kd, kt)
            _transpose_blocks(vd, vt)
            _store_biases(bias, sl_ref, g, hp)
            dk_acc[...] = jnp.zeros_like(dk_acc)
            dv_acc[...] = jnp.zeros_like(dv_acc)

            def unit(u, carry):
                off, offp, first, _, _ = _unit_offsets(u, nb)
                up = jnp.maximum(u - 1, 0)
                q2 = _stack_heads(qd[pl.ds(off, Q_BLOCK), :], low)
                do2 = _stack_heads(dod[pl.ds(off, Q_BLOCK), :], low)
                kc = kd[pl.ds(off, Q_BLOCK), :]
                kp = kd[pl.ds(offp, Q_BLOCK), :]
                lse_t = ld[pl.ds(off, Q_BLOCK), :]
                dsum_t = dd[pl.ds(off, Q_BLOCK), :]
                lse2 = jnp.concatenate([lse_t[:, 0:1], lse_t[:, HEAD_DIM:HEAD_DIM + 1]], axis=0)
                dsum2 = jnp.concatenate([dsum_t[:, 0:1], dsum_t[:, HEAD_DIM:HEAD_DIM + 1]], axis=0)
                sc = _dot(q2, kt[u]) * 0.125 + bias[0]
                sp = jnp.where(first, NEG, _dot(q2, kt[up]) * 0.125 + bias[1])
                pc = jnp.exp(sc - lse2)
                pp = jnp.exp(sp - lse2)
                dsc = (pc * (_dot(do2, vt[u]) - dsum2)).astype(BF16)
                dsp = (pp * (_dot(do2, vt[up]) - dsum2)).astype(BF16)
                dq2 = _dot(dsc, kc) + _dot(dsp, kp)
                dq_acc[pl.ds(off, Q_BLOCK), :] = _unstack_heads(dq2, low) * 0.125
                dk_acc[pl.ds(off, Q_BLOCK), :] += _dot_tn(dsc, q2) * 0.125
                dk_acc[pl.ds(offp, Q_BLOCK), :] += _dot_tn(dsp, q2) * 0.125
                dv_acc[pl.ds(off, Q_BLOCK), :] += _dot_tn(pc.astype(BF16), do2)
                dv_acc[pl.ds(offp, Q_BLOCK), :] += _dot_tn(pp.astype(BF16), do2)
                return carry

            lax.fori_loop(0, nblk, unit, 0, unroll=2)
            for acc, out_ref in ((dq_acc, dq_ref), (dk_acc, dk_ref), (dv_acc, dv_ref)):
                _scatter_classes(acc, stage, r)
                out_ref[0] = stage[...].astype(out_ref.dtype)

        for g in range(3):
            pl.when(gid == g)(lambda g=g: section(g))

    def col(sec):
        return pl.BlockSpec((1, SEQ, LANES), lambda b, g, hp: (b, 0, sec * 12 + g * 4 + hp))

    pos = pl.BlockSpec((1, SEQ, LANES), lambda b, g, hp: (b, 0, hp))
    dout = pl.BlockSpec((1, SEQ, LANES), lambda b, g, hp: (b, 0, g * 4 + hp))
    out = SDS((batch, SEQ, ATTN_WIDTH), BF16)
    seq_bf = pltpu.VMEM((SEQ, LANES), BF16)
    seq_f = pltpu.VMEM((SEQ, LANES), F32)
    blk_t = pltpu.VMEM((nblk, LANES, Q_BLOCK), BF16)
    return pl.pallas_call(
        body, grid=(batch, 3, 4),
        in_specs=[pl.BlockSpec(memory_space=pltpu.SMEM), col(0), col(1), col(2), pos, pos, pos],
        out_specs=[dout, dout, dout],
        out_shape=[out, out, out],
        scratch_shapes=[seq_bf, seq_bf, seq_bf, seq_bf, blk_t, blk_t, seq_f, seq_f, seq_f, seq_f, seq_f, seq_f,
                        pltpu.VMEM((2, 2 * Q_BLOCK, Q_BLOCK), F32)],
        compiler_params=_params("parallel", "parallel", "parallel"), name="attn_bwd")(
            slopes_r, qkv, qkv, qkv, datt, lse, dsum)


CONV_TC = 128
CONV_ROWS = 128
SUBLANES = 8


def _fill_shifted(sh):
    n = SEQ + CONV_PAD - SUBLANES
    for s in range(1, SUBLANES):
        sh[s, 0:n, :] = sh[0, s:s + n, :]


def _tap(sh, base, offset):
    s = offset % SUBLANES
    return sh[s, pl.ds(pl.multiple_of(base + (offset - s), SUBLANES), CONV_ROWS), :]


def _conv_fwd(u, conv_w, conv_b, batch):
    nct = D_MODEL // CONV_TC

    def body(ua_ref, ub_ref, w_ref, b_ref, o_ref, sh):
        sh[0, 0:CONV_PAD, :] = jnp.zeros((CONV_PAD, CONV_TC), F32)
        sh[0, CONV_PAD:, :] = ua_ref[0] * _sigmoid(ub_ref[0])
        _fill_shifted(sh)

        def chunk(c, carry):
            base = pl.multiple_of(c * CONV_ROWS, CONV_ROWS)
            acc = jnp.broadcast_to(b_ref[...], (CONV_ROWS, CONV_TC))
            for t in range(CONV_K):
                acc = acc + _tap(sh, base, t + CONV_PAD - (CONV_K - 1)) * w_ref[t:t + 1, :]
            o_ref[0, pl.ds(base, CONV_ROWS), :] = acc
            return carry

        lax.fori_loop(0, SEQ // CONV_ROWS, chunk, 0)

    return pl.pallas_call(
        body, grid=(nct, batch),
        in_specs=[pl.BlockSpec((1, SEQ, CONV_TC), lambda j, b: (b, 0, j)),
                  pl.BlockSpec((1, SEQ, CONV_TC), lambda j, b: (b, 0, j + nct)),
                  pl.BlockSpec((CONV_PAD, CONV_TC), lambda j, b: (0, j)),
                  pl.BlockSpec((1, CONV_TC), lambda j, b: (0, j))],
        out_specs=pl.BlockSpec((1, SEQ, CONV_TC), lambda j, b: (b, 0, j)),
        out_shape=SDS((batch, SEQ, D_MODEL), F32),
        scratch_shapes=[pltpu.VMEM((SUBLANES, SEQ + CONV_PAD, CONV_TC), F32)],
        compiler_params=_params("parallel", "parallel"), name="conv_fwd")(u, u, conv_w, conv_b)


def _conv_bwd(u, dc1, conv_w, batch, dep=None):
    nct = D_MODEL // CONV_TC
    nchunk = SEQ // CONV_ROWS

    def body(ua_ref, ub_ref, d_ref, w_ref, dua_ref, dub_ref, gw_ref, gb_ref, shc, shd, gacc):
        b = pl.program_id(1)
        shc[0, 0:CONV_PAD, :] = jnp.zeros((CONV_PAD, CONV_TC), F32)
        shc[0, CONV_PAD:, :] = ua_ref[0] * _sigmoid(ub_ref[0])
        _fill_shifted(shc)
        shd[0, 0:SEQ, :] = d_ref[0]
        shd[0, SEQ:, :] = jnp.zeros((CONV_PAD, CONV_TC), F32)
        _fill_shifted(shd)

        @pl.when(b == 0)
        def _():
            gacc[...] = jnp.zeros_like(gacc)
            gb_ref[...] = jnp.zeros_like(gb_ref)

        gb_ref[...] += _rowsum(d_ref[0])

        def chunk(c, carry):
            base = pl.multiple_of(c * CONV_ROWS, CONV_ROWS)
            dcur = shd[0, pl.ds(base, CONV_ROWS), :]
            acc = jnp.zeros((CONV_ROWS, CONV_TC), F32)
            for t in range(CONV_K):
                acc = acc + _tap(shd, base, CONV_K - 1 - t) * w_ref[t:t + 1, :]
                prod = _tap(shc, base, t + CONV_PAD - (CONV_K - 1)) * dcur
                gacc[t] += jnp.sum(prod.reshape(CONV_ROWS // 8, 8, CONV_TC), axis=0)
            ua = ua_ref[0, pl.ds(base, CONV_ROWS), :]
            sg = _sigmoid(ub_ref[0, pl.ds(base, CONV_ROWS), :])
            dua_ref[0, pl.ds(base, CONV_ROWS), :] = (acc * sg).astype(dua_ref.dtype)
            dub_ref[0, pl.ds(base, CONV_ROWS), :] = (acc * ua * sg * (1.0 - sg)).astype(dub_ref.dtype)
            return carry

        lax.fori_loop(0, nchunk, chunk, 0)

        @pl.when(b == batch - 1)
        def _():
            for t in range(CONV_K):
                gw_ref[t:t + 1, :] = jnp.sum(gacc[t], axis=0, keepdims=True)
            gw_ref[CONV_K:CONV_PAD, :] = jnp.zeros((CONV_PAD - CONV_K, CONV_TC), F32)

    du = SDS((batch, SEQ, D_MODEL), BF16)
    body, dep_spec, dep_arg = _anchored(body, 4, dep)
    return pl.pallas_call(
        body, grid=(nct, batch),
        in_specs=[pl.BlockSpec((1, SEQ, CONV_TC), lambda j, b: (b, 0, j)),
                  pl.BlockSpec((1, SEQ, CONV_TC), lambda j, b: (b, 0, j + nct)),
                  pl.BlockSpec((1, SEQ, CONV_TC), lambda j, b: (b, 0, j)),
                  pl.BlockSpec((CONV_PAD, CONV_TC), lambda j, b: (0, j))] + dep_spec,
        out_specs=[pl.BlockSpec((1, SEQ, CONV_TC), lambda j, b: (b, 0, j)),
                   pl.BlockSpec((1, SEQ, CONV_TC), lambda j, b: (b, 0, j)),
                   pl.BlockSpec((CONV_PAD, CONV_TC), lambda j, b: (0, j)),
                   pl.BlockSpec((1, CONV_TC), lambda j, b: (0, j))],
        out_shape=[du, du, SDS((CONV_PAD, D_MODEL), F32), SDS((1, D_MODEL), F32)],
        scratch_shapes=[pltpu.VMEM((SUBLANES, SEQ + CONV_PAD, CONV_TC), F32),
                        pltpu.VMEM((SUBLANES, SEQ + CONV_PAD, CONV_TC), F32),
                        pltpu.VMEM((CONV_K, 8, CONV_TC), F32)],
        compiler_params=_params("parallel", "arbitrary"), name="conv_bwd")(u, u, dc1, conv_w, *dep_arg)


MID_TM = 256


def _layernorm_stats(c1):
    mu = jnp.mean(c1, axis=-1, keepdims=True)
    cen = c1 - mu
    rs = lax.rsqrt(jnp.mean(cen * cen, axis=-1, keepdims=True) + LN_EPS)
    return cen * rs, rs


def _mid_fwd(att, c1, logits, x, w_a, w_c, w_o, gate_b, ln_g, ln_b, g2):
    T = x.shape[0]
    tm = MID_TM

    def body(att_ref, c1_ref, lg_ref, x_ref, wa_ref, wc_ref, wo_ref, gb_ref, lng_ref, lnb_ref, g2_ref,
             c3_ref, ya_ref, yc_ref, mix_ref, x1_ref, h2_ref):
        ya = _dot(att_ref[...], wa_ref[...])
        xh, _ = _layernorm_stats(c1_ref[...])
        c2 = xh * lng_ref[...] + lnb_ref[...]
        c3 = (c2 * _sigmoid(c2)).astype(BF16)
        c3_ref[...] = c3
        yc = _dot(c3, wc_ref[...])
        gates = _sigmoid(lg_ref[...] + gb_ref[...])
        mix = (gates[:, :D_MODEL] * ya + gates[:, D_MODEL:] * yc).astype(BF16)
        ya_ref[...] = ya.astype(BF16)
        yc_ref[...] = yc.astype(BF16)
        mix_ref[...] = mix
        x1 = x_ref[...] + _dot(mix, wo_ref[...])
        x1_ref[...] = x1
        r = lax.rsqrt(jnp.mean(x1 * x1, axis=-1, keepdims=True) + RMS_EPS)
        h2_ref[...] = (x1 * r * g2_ref[...]).astype(BF16)

    row = lambda n: pl.BlockSpec((tm, n), lambda i: (i, 0))
    full = lambda a, b: pl.BlockSpec((a, b), lambda i: (0, 0))
    return pl.pallas_call(
        body, grid=(T // tm,),
        in_specs=[row(ATTN_OUT), row(D_MODEL), row(2 * D_MODEL), row(D_MODEL),
                  full(ATTN_OUT, D_MODEL), full(D_MODEL, D_MODEL), full(D_MODEL, D_MODEL),
                  full(1, 2 * D_MODEL), full(1, D_MODEL), full(1, D_MODEL), full(1, D_MODEL)],
        out_specs=[row(D_MODEL), row(D_MODEL), row(D_MODEL), row(D_MODEL), row(D_MODEL), row(D_MODEL)],
        out_shape=[SDS((T, D_MODEL), BF16), SDS((T, D_MODEL), BF16), SDS((T, D_MODEL), BF16), SDS((T, D_MODEL), BF16),
                   SDS((T, D_MODEL), F32), SDS((T, D_MODEL), BF16)],
        compiler_params=_params("parallel"), name="mid_fwd")(att, c1, logits, x, w_a, w_c, w_o, gate_b, ln_g, ln_b, g2)


def _mid_bwd(dx1b, ya, yc, logits, att, c1, w_a, w_c, w_o, gate_b, ln_g, ln_b, head_ones, dep=None):
    T = dx1b.shape[0]
    tm = MID_TM

    def body(dx_ref, ya_ref, yc_ref, lg_ref, att_ref, c1_ref, wa_ref, wc_ref, wo_ref, gb_ref, lng_ref, lnb_ref, e_ref,
             dlg_ref, dya_ref, dyc_ref, datt_ref, dsum_ref, dc1_ref, ggb_ref, glg_ref, glb_ref):
        @pl.when(pl.program_id(0) == 0)
        def _():
            ggb_ref[...] = jnp.zeros_like(ggb_ref)
            glg_ref[...] = jnp.zeros_like(glg_ref)
            glb_ref[...] = jnp.zeros_like(glb_ref)

        dmix = _dot_nt(dx_ref[...], wo_ref[...])
        gates = _sigmoid(lg_ref[...] + gb_ref[...])
        ga, gc = gates[:, :D_MODEL], gates[:, D_MODEL:]
        dla = dmix * ya_ref[...].astype(F32) * ga * (1.0 - ga)
        dlc = dmix * yc_ref[...].astype(F32) * gc * (1.0 - gc)
        dlg_ref[:, :D_MODEL] = dla.astype(BF16)
        dlg_ref[:, D_MODEL:] = dlc.astype(BF16)
        ggb_ref[:, :D_MODEL] += _rowsum(dla)
        ggb_ref[:, D_MODEL:] += _rowsum(dlc)
        dya = (dmix * ga).astype(BF16)
        dyc = (dmix * gc).astype(BF16)
        dya_ref[...] = dya
        dyc_ref[...] = dyc
        datt = _dot_nt(dya, wa_ref[...])
        datt_ref[...] = datt
        dsum_ref[...] = jnp.dot(datt * att_ref[...].astype(F32), e_ref[...], preferred_element_type=F32,
                                precision=lax.Precision.HIGHEST)
        dc3 = _dot_nt(dyc, wc_ref[...])
        xh, rs = _layernorm_stats(c1_ref[...])
        c2 = xh * lng_ref[...] + lnb_ref[...]
        sg = _sigmoid(c2)
        dc2 = dc3 * (sg * (1.0 + c2 * (1.0 - sg)))
        glg_ref[...] += _rowsum(dc2 * xh)
        glb_ref[...] += _rowsum(dc2)
        dxh = dc2 * lng_ref[...]
        dc1_ref[...] = rs * (dxh - jnp.mean(dxh, axis=-1, keepdims=True) - xh * jnp.mean(dxh * xh, axis=-1, keepdims=True))

    row = lambda n: pl.BlockSpec((tm, n), lambda i: (i, 0))
    full = lambda a, b: pl.BlockSpec((a, b), lambda i: (0, 0))
    body, dep_spec, dep_arg = _anchored(body, 13, dep)
    return pl.pallas_call(
        body, grid=(T // tm,),
        in_specs=[row(D_MODEL), row(D_MODEL), row(D_MODEL), row(2 * D_MODEL), row(ATTN_OUT), row(D_MODEL),
                  full(ATTN_OUT, D_MODEL), full(D_MODEL, D_MODEL), full(D_MODEL, D_MODEL),
                  full(1, 2 * D_MODEL), full(1, D_MODEL), full(1, D_MODEL), full(ATTN_OUT, ATTN_OUT)] + dep_spec,
        out_specs=[row(2 * D_MODEL), row(D_MODEL), row(D_MODEL), row(ATTN_OUT), row(ATTN_OUT), row(D_MODEL),
                   full(1, 2 * D_MODEL), full(1, D_MODEL), full(1, D_MODEL)],
        out_shape=[SDS((T, 2 * D_MODEL), BF16), SDS((T, D_MODEL), BF16), SDS((T, D_MODEL), BF16), SDS((T, ATTN_OUT), F32),
                   SDS((T, ATTN_OUT), F32), SDS((T, D_MODEL), F32),
                   SDS((1, 2 * D_MODEL), F32), SDS((1, D_MODEL), F32), SDS((1, D_MODEL), F32)],
        compiler_params=_params("arbitrary"), name="mid_bwd")(dx1b, ya, yc, logits, att, c1, w_a, w_c, w_o, gate_b, ln_g, ln_b,
                                                               head_ones, *dep_arg)


FFN_TM = 512
FFN_TF = D_FF // 2


def _rms_bwd(dy_times_g, xh, r):
    return r * (dy_times_g - xh * jnp.mean(dy_times_g * xh, axis=-1, keepdims=True))


def _ffn_fwd(h2, x1, target, gf, w_g, w_u, w_d):
    T = h2.shape[0]
    tm, tf = FFN_TM, FFN_TF
    nf = D_FF // tf

    def body(h_ref, x1_ref, t_ref, gf_ref, wg_ref, wu_ref, wd_ref,
             a_ref, b_ref, f_ref, dx2_ref, dx2b_ref, loss_ref, gnf_ref, acc):
        i, j = pl.program_id(0), pl.program_id(1)
        h = h_ref[...]
        a = _dot(h, wg_ref[...])
        b = _dot(h, wu_ref[...])
        f = (a * _sigmoid(a) * b).astype(BF16)
        a_ref[...] = a.astype(BF16)
        b_ref[...] = b.astype(BF16)
        f_ref[...] = f
        p = _dot(f, wd_ref[...])

        @pl.when(j == 0)
        def _():
            acc[...] = x1_ref[...] + p

        @pl.when(j > 0)
        def _():
            acc[...] += p

        @pl.when((i == 0) & (j == nf - 1))
        def _():
            loss_ref[...] = jnp.zeros_like(loss_ref)
            gnf_ref[...] = jnp.zeros_like(gnf_ref)

        @pl.when(j == nf - 1)
        def _():
            x2 = acc[...]
            r = lax.rsqrt(jnp.mean(x2 * x2, axis=-1, keepdims=True) + RMS_EPS)
            xh = x2 * r
            err = xh * gf_ref[...] - t_ref[...]
            loss_ref[...] += (0.5 / D_MODEL) * jnp.sum(err * err)
            dy = err * (1.0 / D_MODEL)
            gnf_ref[...] += _rowsum(dy * xh)
            dx2 = _rms_bwd(dy * gf_ref[...], xh, r)
            dx2_ref[...] = dx2
            dx2b_ref[...] = dx2.astype(BF16)

    row = lambda n: pl.BlockSpec((tm, n), lambda i, j: (i, 0))
    ffb = pl.BlockSpec((tm, tf), lambda i, j: (i, j))
    return pl.pallas_call(
        body, grid=(T // tm, nf),
        in_specs=[row(D_MODEL), row(D_MODEL), row(D_MODEL), pl.BlockSpec((1, D_MODEL), lambda i, j: (0, 0)),
                  pl.BlockSpec((D_MODEL, tf), lambda i, j: (0, j)), pl.BlockSpec((D_MODEL, tf), lambda i, j: (0, j)),
                  pl.BlockSpec((tf, D_MODEL), lambda i, j: (j, 0))],
        out_specs=[ffb, ffb, ffb, row(D_MODEL), row(D_MODEL),
                   pl.BlockSpec((1, 128), lambda i, j: (0, 0)), pl.BlockSpec((1, D_MODEL), lambda i, j: (0, 0))],
        out_shape=[SDS((T, D_FF), BF16), SDS((T, D_FF), BF16), SDS((T, D_FF), BF16), SDS((T, D_MODEL), F32),
                   SDS((T, D_MODEL), BF16), SDS((1, 128), F32), SDS((1, D_MODEL), F32)],
        scratch_shapes=[pltpu.VMEM((tm, D_MODEL), F32)],
        compiler_params=_params("arbitrary", "arbitrary"), name="ffn_fwd")(h2, x1, target, gf, w_g, w_u, w_d)


def _ffn_bwd(dx2b, dx2, a, b, x1, g2, w_g, w_u, w_d):
    T = dx2.shape[0]
    tm, tf = FFN_TM, FFN_TF
    nf = D_FF // tf

    def body(dxb_ref, dx2_ref, a_ref, b_ref, x1_ref, g2_ref, wg_ref, wu_ref, wd_ref,
             da_ref, db_ref, dx1_ref, dx1b_ref, gn2_ref, acc):
        i, j = pl.program_id(0), pl.program_id(1)
        df = _dot_nt(dxb_ref[...], wd_ref[...])
        av = a_ref[...].astype(F32)
        bv = b_ref[...].astype(F32)
        sg = _sigmoid(av)
        db = (df * av * sg).astype(BF16)
        da = (df * bv * (sg * (1.0 + av * (1.0 - sg)))).astype(BF16)
        da_ref[...] = da
        db_ref[...] = db
        p = _dot_nt(da, wg_ref[...]) + _dot_nt(db, wu_ref[...])

        @pl.when(j == 0)
        def _():
            acc[...] = p

        @pl.when(j > 0)
        def _():
            acc[...] += p

        @pl.when((i == 0) & (j == nf - 1))
        def _():
            gn2_ref[...] = jnp.zeros_like(gn2_ref)

        @pl.when(j == nf - 1)
        def _():
            dh2 = acc[...]
            x1 = x1_ref[...]
            r = lax.rsqrt(jnp.mean(x1 * x1, axis=-1, keepdims=True) + RMS_EPS)
            xh = x1 * r
            gn2_ref[...] += _rowsum(dh2 * xh)
            dx1 = dx2_ref[...] + _rms_bwd(dh2 * g2_ref[...], xh, r)
            dx1_ref[...] = dx1
            dx1b_ref[...] = dx1.astype(BF16)

    row = lambda n: pl.BlockSpec((tm, n), lambda i, j: (i, 0))
    ffb = pl.BlockSpec((tm, tf), lambda i, j: (i, j))
    return pl.pallas_call(
        body, grid=(T // tm, nf),
        in_specs=[row(D_MODEL), row(D_MODEL), ffb, ffb, row(D_MODEL), pl.BlockSpec((1, D_MODEL), lambda i, j: (0, 0)),
                  pl.BlockSpec((D_MODEL, tf), lambda i, j: (0, j)), pl.BlockSpec((D_MODEL, tf), lambda i, j: (0, j)),
                  pl.BlockSpec((tf, D_MODEL), lambda i, j: (j, 0))],
        out_specs=[ffb, ffb, row(D_MODEL), row(D_MODEL), pl.BlockSpec((1, D_MODEL), lambda i, j: (0, 0))],
        out_shape=[SDS((T, D_FF), BF16), SDS((T, D_FF), BF16), SDS((T, D_MODEL), F32), SDS((T, D_MODEL), BF16),
                   SDS((1, D_MODEL), F32)],
        scratch_shapes=[pltpu.VMEM((tm, D_MODEL), F32)],
        compiler_params=_params("arbitrary", "arbitrary"), name="ffn_bwd")(dx2b, dx2, a, b, x1, g2, w_g, w_u, w_d)


def _in_bwd(pieces, w_in, x, dx1, g1, dep=None):
    T = x.shape[0]
    tm = IN_TM
    npc = len(pieces)
    assert sum(p.shape[1] for p in pieces) == IN_WIDTH

    def body(*refs):
        p_refs = refs[:npc]
        w_hbm, x_ref, dx1_ref, g_ref, dx_ref, gn1_ref, w_vmem, sem = refs[npc:]

        @pl.when(pl.program_id(0) == 0)
        def _():
            cp = pltpu.make_async_copy(w_hbm, w_vmem, sem)
            cp.start()
            cp.wait()
            gn1_ref[...] = jnp.zeros_like(gn1_ref)

        dh = jnp.zeros((tm, D_MODEL), F32)
        col = 0
        for p_ref in p_refs:
            for j in range(p_ref.shape[1] // IN_CHUNK):
                dh = dh + _dot_nt(p_ref[:, j * IN_CHUNK:(j + 1) * IN_CHUNK], w_vmem[:, col:col + IN_CHUNK])
                col += IN_CHUNK
        xv = x_ref[...]
        r = lax.rsqrt(jnp.mean(xv * xv, axis=-1, keepdims=True) + RMS_EPS)
        xh = xv * r
        gn1_ref[...] += _rowsum(dh * xh)
        dx_ref[...] = dx1_ref[...] + _rms_bwd(dh * g_ref[...], xh, r)

    row = lambda n: pl.BlockSpec((tm, n), lambda i: (i, 0))
    body, dep_spec, dep_arg = _anchored(body, npc + 4, dep)
    return pl.pallas_call(
        body, grid=(T // tm,),
        in_specs=[row(p.shape[1]) for p in pieces]
        + [pl.BlockSpec(memory_space=pl.ANY), row(D_MODEL), row(D_MODEL), pl.BlockSpec((1, D_MODEL), lambda i: (0, 0))]
        + dep_spec,
        out_specs=[row(D_MODEL), pl.BlockSpec((1, D_MODEL), lambda i: (0, 0))],
        out_shape=[SDS((T, D_MODEL), F32), SDS((1, D_MODEL), F32)],
        scratch_shapes=[pltpu.VMEM((D_MODEL, IN_WIDTH), BF16), pltpu.SemaphoreType.DMA],
        compiler_params=_params("arbitrary"), name="in_bwd")(*pieces, w_in, x, dx1, g1, *dep_arg)


def _local_step(x, target, w, small, dep=None, late_weights=None, emit=None):
    T = x.shape[0]
    batch = T // SEQ
    slopes_r = jnp.asarray(_slopes_times_dilation())
    emit = emit or (lambda names, grads: None)

    h, qkv, u, logits = _in_proj(x, small["norm1_g"], w["w_in"], dep)

    qkv3 = qkv.reshape(batch, SEQ, 3 * ATTN_WIDTH)
    att, lse = _attn_fwd(qkv3, slopes_r, batch)
    att = att.reshape(T, ATTN_OUT)

    u3 = u.reshape(batch, SEQ, 2 * D_MODEL)
    c1 = _conv_fwd(u3, w["conv_w"], small["conv_b"], batch).reshape(T, D_MODEL)
    if late_weights is not None:
        w = {**w, **late_weights(LATE_MERGE, c1)}

    c3, ya, yc, mix, x1, h2 = _mid_fwd(
        att, c1, logits, x, w["w_attn_out"], w["w_conv_out"], w["w_o"],
        small["gate_b"], small["conv_ln_g"], small["conv_ln_b"], small["norm2_g"])
    if late_weights is not None:
        w = {**w, **late_weights(LATE_FFN, h2)}

    a, b, f, dx2, dx2b, loss, g_normf = _ffn_fwd(h2, x1, target, small["norm_f_g"],
                                                   w["w_ffn_gate"], w["w_ffn_up"], w["w_ffn_down"])

    da, db, dx1, dx1b, g_norm2 = _ffn_bwd(dx2b, dx2, a, b, x1, small["norm2_g"],
                                           w["w_ffn_gate"], w["w_ffn_up"], w["w_ffn_down"])
    gw = {}
    gw["w_ffn_down"] = _mm_tn(f, dx2b, BF16, "gw_ffn_down", tn=512)
    gw["w_ffn_gate"] = _mm_tn(h2, da, BF16, "gw_ffn_gate", tn=1408)
    gw["w_ffn_up"] = _mm_tn(h2, db, BF16, "gw_ffn_up", tn=1408)
    token = emit(("w_ffn_gate", "w_ffn_up", "w_ffn_down"), gw)

    head_ones = jnp.asarray(np.kron(np.eye(HEADS_PER_GROUP, dtype=np.float32), np.ones((HEAD_DIM, HEAD_DIM), np.float32)))
    dlogits, dya, dyc, datt, dsum, dc1, g_gate_b, g_ln_g, g_ln_b = _mid_bwd(
        dx1b, ya, yc, logits, att, c1, w["w_attn_out"], w["w_conv_out"], w["w_o"],
        small["gate_b"], small["conv_ln_g"], small["conv_ln_b"], head_ones, token)
    gw["w_o"] = _mm_tn(mix, dx1b, BF16, "gw_o", tn=512)
    gw["w_attn_out"] = _mm_tn(att, dya, BF16, "gw_attn_out", tn=512)
    gw["w_conv_out"] = _mm_tn(c3, dyc, BF16, "gw_conv_out", tn=512)
    token = emit(("w_conv_out", "w_attn_out", "w_o"), gw)

    dua, dub, g_conv_w, g_conv_b = _conv_bwd(u3, dc1.reshape(batch, SEQ, D_MODEL), w["conv_w"], batch, token)

    dq, dk, dv = _attn_bwd(qkv3, datt.reshape(batch, SEQ, ATTN_OUT), lse, dsum.reshape(batch, SEQ, ATTN_OUT),
                           slopes_r, batch)
    pieces = [dq.reshape(T, ATTN_WIDTH), dk.reshape(T, ATTN_WIDTH), dv.reshape(T, ATTN_WIDTH),
              dua.reshape(T, D_MODEL), dub.reshape(T, D_MODEL), dlogits]

    names = ("q", "k", "v", "ua", "ub", "gate")
    gw["w_in"] = jnp.concatenate(
        [_mm_tn(h, p, BF16, "gw_in_" + nm, tn=min(p.shape[1], 1024) if p.shape[1] != ATTN_WIDTH else 768)
         for nm, p in zip(names, pieces)], axis=1)
    gw["conv_w"] = g_conv_w
    token = emit(("w_in", "conv_w"), gw)
    grad_x, g_norm1 = _in_bwd(pieces, w["w_in"], x, dx1, small["norm1_g"], token)

    gsmall = {"norm1_g": g_norm1, "gate_b": g_gate_b, "conv_b": g_conv_b, "conv_ln_g": g_ln_g, "conv_ln_b": g_ln_b,
              "norm2_g": g_norm2, "norm_f_g": g_normf}
    return loss, grad_x, gw, gsmall


ANY = pl.BlockSpec(memory_space=pl.ANY)


def _all_gather(arrs):
    n = len(arrs)

    def body(*refs):
        ins, outs = refs[:n], refs[n:2 * n]
        send_sems, recv_sems, local_sems = refs[2 * n:]
        x, y, c = lax.axis_index("x"), lax.axis_index("y"), lax.axis_index("c")
        me, sibling = (x, y, c), (x, y, 1 - c)
        chips = [(1 - x, y), (x, 1 - y), (1 - x, 1 - y)]

        def copy(a, k, block, to, src=None):
            px, py, pc = block
            dst = outs[a].at[4 * px + 2 * py + pc]
            return pltpu.make_async_remote_copy(
                src_ref=dst if src is None else src, dst_ref=dst,
                send_sem=send_sems.at[a, k], recv_sem=recv_sems.at[a, k], device_id=to, device_id_type=MESH)

        mine = [pltpu.make_async_copy(ins[a], outs[a].at[4 * x + 2 * y + c], local_sems.at[a]) for a in range(n)]
        for cp in mine:
            cp.start()
        first = []
        for j, chip in enumerate(chips):
            first += [copy(a, 1 + j, me, (*chip, c), src=ins[a]) for a in range(n)]
        first += [copy(a, 0, me, sibling, src=ins[a]) for a in range(n)]
        for cp in first:
            cp.start()
        passed = []
        for j, chip in enumerate(chips):
            for a in range(n):
                copy(a, 1 + j, (*chip, c), me).wait_recv()
                cp = copy(a, 4 + j, (*chip, c), sibling)
                cp.start()
                passed.append(cp)
        for a in range(n):
            copy(a, 0, sibling, me).wait_recv()
        for j, chip in enumerate(chips):
            for a in range(n):
                copy(a, 4 + j, (*chip, 1 - c), me).wait_recv()
        for cp in first + passed:
            cp.wait_send()
        for cp in mine:
            cp.wait()

    return pl.pallas_call(
        body, in_specs=[ANY] * n, out_specs=[ANY] * n,
        out_shape=[SDS((N_DEV,) + a.shape, a.dtype) for a in arrs],
        scratch_shapes=[pltpu.SemaphoreType.DMA((n, 7)), pltpu.SemaphoreType.DMA((n, 7)), pltpu.SemaphoreType.DMA((n,))],
        name="all_gather_weights")(*arrs)


HBM = pl.BlockSpec(memory_space=pltpu.HBM)
SEM = pl.BlockSpec(memory_space=pltpu.SEMAPHORE)
N_PEERS = N_DEV - 1


def _peer_copies(srcs, lands, send_sems, recv_sems, scatter):
    x, y, c = lax.axis_index("x"), lax.axis_index("y"), lax.axis_index("c")
    me = 4 * x + 2 * y + c
    send, recv = [], []
    for a in range(len(srcs)):
        for k in range(1, N_DEV):
            peer = (x ^ ((k >> 2) & 1), y ^ ((k >> 1) & 1), c ^ (k & 1))
            pidx = 4 * peer[0] + 2 * peer[1] + peer[2]
            src = srcs[a].at[pidx] if scatter else srcs[a]
            s = a * N_PEERS + k - 1
            send.append(pltpu.make_async_remote_copy(
                src_ref=src, dst_ref=lands[a].at[me], send_sem=send_sems.at[s], recv_sem=recv_sems.at[s],
                device_id=peer, device_id_type=MESH))
            recv.append(pltpu.make_async_remote_copy(
                src_ref=src, dst_ref=lands[a].at[pidx], send_sem=send_sems.at[s], recv_sem=recv_sems.at[s],
                device_id=peer, device_id_type=MESH))
    return send, recv


def _send_start(srcs, scatter, name):
    n = len(srcs)
    lands = [lax.empty((N_DEV,) + (s.shape[1:] if scatter else s.shape), s.dtype) for s in srcs]

    def body(*refs):
        send, _ = _peer_copies(refs[:n], refs[n:2 * n], refs[2 * n], refs[2 * n + 1], scatter)
        for cp in send:
            cp.start()
        token = refs[-1]
        token[...] = jnp.zeros_like(token)

    res = pl.pallas_call(
        body, name=name,
        out_shape=(pltpu.SemaphoreType.DMA((n * N_PEERS,)), pltpu.SemaphoreType.DMA((n * N_PEERS,)),
                   *[pltpu.HBM(s.shape, s.dtype) for s in srcs], *[pltpu.HBM(l.shape, l.dtype) for l in lands],
                   SDS((8, 128), F32)),
        in_specs=[HBM] * (2 * n), out_specs=(SEM, SEM, *([HBM] * (2 * n)), pl.BlockSpec(memory_space=pltpu.VMEM)),
        input_output_aliases={i: 2 + i for i in range(2 * n)},
        compiler_params=pltpu.CompilerParams(has_side_effects=pltpu.SideEffectType.DATAFLOW_SIDE_EFFECTING),
    )(*[pltpu.with_memory_space_constraint(s, pltpu.HBM) for s in srcs],
      *[pltpu.with_memory_space_constraint(l, pltpu.HBM) for l in lands])
    return dict(send_sems=res[0], recv_sems=res[1], srcs=res[2:2 + n], lands=res[2 + n:2 + 2 * n], token=res[-1])


def _send_wait(started, scatter, after, name):
    n = len(started["srcs"])

    def body(*refs):
        send, recv = _peer_copies(refs[:n], refs[n:2 * n], refs[2 * n], refs[2 * n + 1], scatter)
        for cp in send:
            cp.wait_send()
        for cp in recv:
            cp.wait_recv()

    both = list(started["srcs"]) + list(started["lands"])
    res = pl.pallas_call(
        body, name=name,
        out_shape=tuple(pltpu.HBM(a.shape, a.dtype) for a in both),
        in_specs=[HBM] * (2 * n) + [SEM, SEM, ANY], out_specs=tuple([HBM] * (2 * n)),
        input_output_aliases={i: i for i in range(2 * n)},
        compiler_params=pltpu.CompilerParams(has_side_effects=pltpu.SideEffectType.DATAFLOW_SIDE_EFFECTING),
    )(*both, started["send_sems"], started["recv_sems"], after)
    return res[:n], res[n:]


def _row_tile(rows, cols, itemsize_total):
    budget = (4 << 20) // max(1, cols * itemsize_total)
    if rows <= budget:
        return rows
    t = rows
    while t > budget and t % 2 == 0 and (t // 2) % 16 == 0:
        t //= 2
    return t


def _adam_math(g, w, m, v):
    m_new = ADAM_B1 * m + (1.0 - ADAM_B1) * g
    v_new = ADAM_B2 * v + (1.0 - ADAM_B2) * (g * g)
    m_hat = m_new / (1.0 - ADAM_B1 ** ADAM_STEP)
    v_hat = v_new / (1.0 - ADAM_B2 ** ADAM_STEP)
    delta = -ADAM_LR * (m_hat / (jnp.sqrt(v_hat) + ADAM_EPS) + ADAM_WD * w)
    return delta, m_new, v_new


def _sum_adam(parts, w, m, v, name):
    rows, cols = w.shape
    nparts = parts.shape[0]
    tr = _row_tile(rows, cols, nparts * parts.dtype.itemsize + 7 * 4)

    def body(p_ref, w_ref, m_ref, v_ref, g_ref, d_ref, mo_ref, vo_ref):
        g = p_ref[0].astype(F32)
        for s in range(1, nparts):
            g = g + p_ref[s].astype(F32)
        delta, m_new, v_new = _adam_math(g, w_ref[...], m_ref[...], v_ref[...])
        g_ref[...] = g
        d_ref[...] = delta
        mo_ref[...] = m_new
        vo_ref[...] = v_new

    blk = pl.BlockSpec((tr, cols), lambda i: (i, 0))
    out = SDS((rows, cols), F32)
    return pl.pallas_call(
        body, grid=(rows // tr,),
        in_specs=[pl.BlockSpec((nparts, tr, cols), lambda i: (0, i, 0)), blk, blk, blk],
        out_specs=[blk, blk, blk, blk], out_shape=[out, out, out, out],
        compiler_params=_params("parallel"), name=name)(parts, w, m, v)


SMALL_ROWS = 64


def _small_allreduce_adam(gpart, w, m, v, dep=None):
    def body(g_ref, w_ref, m_ref, v_ref, go_ref, d_ref, mo_ref, vo_ref, gath, send_sems, recv_sems):
        x, y, c = lax.axis_index("x"), lax.axis_index("y"), lax.axis_index("c")
        me = 4 * x + 2 * y + c
        gath[me] = g_ref[...]
        copies = []
        for k in range(1, N_DEV):
            fx, fy, fc = (k >> 2) & 1, (k >> 1) & 1, k & 1
            peer = (x ^ fx, y ^ fy, c ^ fc)
            copies.append(pltpu.make_async_remote_copy(
                src_ref=gath.at[me], dst_ref=gath.at[me], send_sem=send_sems.at[k - 1], recv_sem=recv_sems.at[k - 1],
                device_id=peer, device_id_type=MESH))
        for cp in copies:
            cp.start()
        for cp in copies:
            cp.wait_recv()
        for cp in copies:
            cp.wait_send()
        g = gath[0]
        for d in range(1, N_DEV):
            g = g + gath[d]
        delta, m_new, v_new = _adam_math(g, w_ref[...], m_ref[...], v_ref[...])
        go_ref[...] = g
        d_ref[...] = delta
        mo_ref[...] = m_new
        vo_ref[...] = v_new

    vm = pl.BlockSpec(memory_space=pltpu.VMEM)
    out = SDS((SMALL_ROWS, 128), F32)
    body, dep_spec, dep_arg = _anchored(body, 4, dep)
    return pl.pallas_call(
        body, in_specs=[vm] * 4 + dep_spec, out_specs=[vm] * 4, out_shape=[out] * 4,
        scratch_shapes=[pltpu.VMEM((N_DEV, SMALL_ROWS, 128), F32), pltpu.SemaphoreType.DMA((N_DEV - 1,)),
                        pltpu.SemaphoreType.DMA((N_DEV - 1,))],
        name="small_allreduce_adam")(gpart, w, m, v, *dep_arg)


BIG = ("w_in", "conv_w", "w_conv_out", "w_attn_out", "w_o", "w_ffn_gate", "w_ffn_up", "w_ffn_down")
EARLY = ("w_in", "conv_w")
LATE_MERGE = ("w_conv_out", "w_attn_out", "w_o")
LATE_FFN = ("w_ffn_gate", "w_ffn_up", "w_ffn_down")
COL_SHARDED = ("w_in", "conv_w", "w_attn_out", "w_ffn_gate", "w_ffn_up")
SMALL = ("norm1_g", "gate_b", "conv_b", "conv_ln_g", "conv_ln_b", "norm2_g", "norm_f_g")
WEIGHTS = ("norm1_g", "w_in", "gate_b", "conv_w", "conv_b", "conv_ln_g", "conv_ln_b", "w_conv_out", "w_attn_out", "w_o",
           "norm2_g", "w_ffn_gate", "w_ffn_up", "w_ffn_down", "norm_f_g")


def _shard2d(name, a):
    a = a.reshape(a.shape[-2], a.shape[-1])
    if name == "conv_w":
        a = jnp.pad(a, ((0, CONV_PAD - CONV_K), (0, 0)))
    return a


def _gathered_to_full(name, g):
    if name in COL_SHARDED:
        return g.transpose(1, 0, 2).reshape(g.shape[1], N_DEV * g.shape[2])
    return g.reshape(N_DEV * g.shape[1], g.shape[2])


def _full_to_blocks(name, g):
    if name in COL_SHARDED:
        return g.reshape(g.shape[0], N_DEV, g.shape[1] // N_DEV).transpose(1, 0, 2)
    return g.reshape(N_DEV, g.shape[0] // N_DEV, g.shape[1])


def _pack_small(d):
    return jnp.concatenate([d[n].reshape(-1) for n in SMALL]).reshape(SMALL_ROWS, 128)


def _unpack_small(p, like):
    flat = p.reshape(-1)
    out, off = {}, 0
    for n in SMALL:
        size = like[n].size
        out[n] = flat[off:off + size].reshape(like[n].shape)
        off += size
    return out


def kernel(x, norm1_g, w_in, gate_b, conv_w, conv_b, conv_ln_g, conv_ln_b, w_conv_out, w_attn_out, w_o, norm2_g, w_ffn_gate, w_ffn_up, w_ffn_down, norm_f_g, loss_target, m_norm1_g, m_w_in, m_gate_b, m_conv_w, m_conv_b, m_conv_ln_g, m_conv_ln_b, m_w_conv_out, m_w_attn_out, m_w_o, m_norm2_g, m_w_ffn_gate, m_w_ffn_up, m_w_ffn_down, m_norm_f_g, v_norm1_g, v_w_in, v_gate_b, v_conv_w, v_conv_b, v_conv_ln_g, v_conv_ln_b, v_w_conv_out, v_w_attn_out, v_w_o, v_norm2_g, v_w_ffn_gate, v_w_ffn_up, v_w_ffn_down, v_norm_f_g):
    wts = dict(norm1_g=norm1_g, w_in=w_in, gate_b=gate_b, conv_w=conv_w, conv_b=conv_b, conv_ln_g=conv_ln_g,
               conv_ln_b=conv_ln_b, w_conv_out=w_conv_out, w_attn_out=w_attn_out, w_o=w_o, norm2_g=norm2_g,
               w_ffn_gate=w_ffn_gate, w_ffn_up=w_ffn_up, w_ffn_down=w_ffn_down, norm_f_g=norm_f_g)
    mom1 = dict(norm1_g=m_norm1_g, w_in=m_w_in, gate_b=m_gate_b, conv_w=m_conv_w, conv_b=m_conv_b, conv_ln_g=m_conv_ln_g,
                conv_ln_b=m_conv_ln_b, w_conv_out=m_w_conv_out, w_attn_out=m_w_attn_out, w_o=m_w_o, norm2_g=m_norm2_g,
                w_ffn_gate=m_w_ffn_gate, w_ffn_up=m_w_ffn_up, w_ffn_down=m_w_ffn_down, norm_f_g=m_norm_f_g)
    mom2 = dict(norm1_g=v_norm1_g, w_in=v_w_in, gate_b=v_gate_b, conv_w=v_conv_w, conv_b=v_conv_b, conv_ln_g=v_conv_ln_g,
                conv_ln_b=v_conv_ln_b, w_conv_out=v_w_conv_out, w_attn_out=v_w_attn_out, w_o=v_w_o, norm2_g=v_norm2_g,
                w_ffn_gate=v_w_ffn_gate, w_ffn_up=v_w_ffn_up, w_ffn_down=v_w_ffn_down, norm_f_g=v_norm_f_g)

    T = x.shape[0] * x.shape[1]
    x2 = x.reshape(T, D_MODEL)
    t2 = loss_target.reshape(T, D_MODEL)

    me = 4 * lax.axis_index("x") + 2 * lax.axis_index("y") + lax.axis_index("c")
    shards = {n: _shard2d(n, wts[n]) for n in BIG}
    sent = {n: shards[n] if n == "conv_w" else shards[n].astype(BF16) for n in BIG}
    small = {n: wts[n].reshape(1, -1) for n in SMALL}

    gathered = _all_gather([sent[n] for n in EARLY])
    full = {n: _gathered_to_full(n, g) for n, g in zip(EARLY, gathered)}
    late = {names: _send_start([sent[n] for n in names], False, "gather_start_" + names[0])
            for names in (LATE_MERGE, LATE_FFN)}

    def late_weights(names, after):
        srcs, lands = _send_wait(late[names], False, after, "gather_wait_" + names[0])
        return {n: _gathered_to_full(n, lax.dynamic_update_slice(land, src[None], (me, 0, 0)))
                for n, src, land in zip(names, srcs, lands)}

    scatters = []

    def emit(names, gw):
        started = _send_start([_full_to_blocks(n, gw[n]) for n in names], True, "scatter_start_" + names[0])
        scatters.append((names, started))
        return started["token"]

    loss_part, grad_x, gw, gsmall = _local_step(x2, t2, full, small, late[LATE_FFN]["token"], late_weights, emit)

    grads, deltas, new_m, new_v = {}, {}, {}, {}
    after = grad_x
    for names, started in scatters:
        srcs, lands = _send_wait(started, True, after, "scatter_wait_" + names[0])
        for n, src, land in zip(names, srcs, lands):
            parts = lax.dynamic_update_slice(land, lax.dynamic_slice_in_dim(src, me, 1, axis=0), (me, 0, 0))
            g, d, mo, vo = _sum_adam(parts, shards[n], _shard2d(n, mom1[n]), _shard2d(n, mom2[n]), "adam_" + n)
            for dst, val in ((grads, g), (deltas, d), (new_m, mo), (new_v, vo)):
                if n == "conv_w":
                    val = val[:CONV_K]
                dst[n] = val.reshape(wts[n].shape)
            after = g

    sg, sd, sm, sv = _small_allreduce_adam(_pack_small(gsmall), _pack_small(wts), _pack_small(mom1), _pack_small(mom2),
                                           after)
    for dst, val in ((grads, sg), (deltas, sd), (new_m, sm), (new_v, sv)):
        dst.update(_unpack_small(val, wts))

    loss = lax.psum(loss_part[0, 0], ("x", "y", "c"))
    return (loss, grad_x.reshape(x.shape), *[grads[n] for n in WEIGHTS], *[deltas[n] for n in WEIGHTS],
            *[new_m[n] for n in WEIGHTS], *[new_v[n] for n in WEIGHTS])
```

```python
import math

import numpy as np
import jax
import jax.numpy as jnp
from jax import lax
from jax.experimental import pallas as pl
from jax.experimental.pallas import tpu as pltpu

F32 = jnp.float32
BF16 = jnp.bfloat16
SDS = jax.ShapeDtypeStruct
MESH = pl.DeviceIdType.MESH

D_MODEL = 1024
SEQ = 2048
HEAD_DIM = 64
GROUPS = ((128, 1), (512, 4), (2048, 16))
HEADS_PER_GROUP = 8
N_HEADS = 24
ATTN_WIDTH = N_HEADS * HEAD_DIM
ATTN_OUT = HEADS_PER_GROUP * HEAD_DIM
CONV_K = 31
CONV_PAD = 32
D_FF = 2816
IN_WIDTH = 3 * ATTN_WIDTH + 2 * D_MODEL + 2 * D_MODEL
RMS_EPS = 1e-6
LN_EPS = 1e-5
Q_BLOCK = 128
LANES = 128
NEG = -1e30
N_DEV = 8

ADAM_LR = 0.001
ADAM_B1 = 0.9
ADAM_B2 = 0.999
ADAM_EPS = 1e-08
ADAM_WD = 0.01
ADAM_STEP = 10


def _alibi_slope_list(n):
    def pow2(m):
        start = 2.0 ** (-8.0 / m)
        return [start ** (i + 1) for i in range(m)]
    if math.log2(n).is_integer():
        return pow2(n)
    c = 2 ** math.floor(math.log2(n))
    return pow2(c) + _alibi_slope_list(2 * c)[0::2][: n - c]


def _slopes_times_dilation():
    s = np.asarray(sorted(_alibi_slope_list(N_HEADS), reverse=True), dtype=np.float32).reshape(3, HEADS_PER_GROUP)
    r = np.asarray([g[1] for g in GROUPS], dtype=np.float32)[:, None]
    return (s * r).reshape(N_HEADS)


def _sigmoid(x):
    return 1.0 / (1.0 + jnp.exp(-x))


def _dot(a, b):
    return jnp.dot(a, b, preferred_element_type=F32)


def _dot_nt(a, b):
    return lax.dot_general(a, b, (((1,), (1,)), ((), ())), preferred_element_type=F32)


def _dot_tn(a, b):
    return lax.dot_general(a, b, (((0,), (0,)), ((), ())), preferred_element_type=F32)


def _rowsum(x):
    return jnp.sum(x, axis=0, keepdims=True)


def _params(*sem):
    return pltpu.CompilerParams(dimension_semantics=sem)


def _anchored(body, n_in, dep):
    if dep is None:
        return body, [], []

    def wrapped(*refs):
        return body(*refs[:n_in], *refs[n_in + 1:])

    return wrapped, [pl.BlockSpec(memory_space=pl.ANY)], [dep]


IN_TM = 256
IN_CHUNK = 512


def _in_proj(x, g1, w_in, dep=None):
    T = x.shape[0]
    tm = IN_TM
    widths = (3 * ATTN_WIDTH, 2 * D_MODEL, 2 * D_MODEL)

    def body(x_ref, g_ref, w_hbm, h_ref, qkv_ref, u_ref, lg_ref, w_vmem, sem):
        @pl.when(pl.program_id(0) == 0)
        def _():
            cp = pltpu.make_async_copy(w_hbm, w_vmem, sem)
            cp.start()
            cp.wait()

        xv = x_ref[...]
        r = lax.rsqrt(jnp.mean(xv * xv, axis=-1, keepdims=True) + RMS_EPS)
        h = (xv * r * g_ref[...]).astype(BF16)
        h_ref[...] = h
        col = 0
        for o_ref, width in zip((qkv_ref, u_ref, lg_ref), widths):
            for j in range(width // IN_CHUNK):
                o_ref[:, j * IN_CHUNK:(j + 1) * IN_CHUNK] = _dot(h, w_vmem[:, col:col + IN_CHUNK])
                col += IN_CHUNK

    row = lambda n: pl.BlockSpec((tm, n), lambda i: (i, 0))
    body, dep_spec, dep_arg = _anchored(body, 3, dep)
    return pl.pallas_call(
        body, grid=(T // tm,),
        in_specs=[row(D_MODEL), pl.BlockSpec((1, D_MODEL), lambda i: (0, 0)), pl.BlockSpec(memory_space=pl.ANY)] + dep_spec,
        out_specs=[row(D_MODEL)] + [row(n) for n in widths],
        out_shape=[SDS((T, D_MODEL), BF16)] + [SDS((T, n), F32) for n in widths],
        scratch_shapes=[pltpu.VMEM((D_MODEL, IN_WIDTH), BF16), pltpu.SemaphoreType.DMA],
        compiler_params=_params("arbitrary"), name="in_proj")(x, g1, w_in, *dep_arg)


def _mm_tn(a, b, out_dtype, name, tn, tt=512):
    T, K = a.shape
    N = b.shape[1]
    nt = T // tt

    def body(a_ref, b_ref, o_ref, acc):
        t = pl.program_id(1)
        p = _dot_tn(a_ref[...], b_ref[...])

        @pl.when(t == 0)
        def _():
            acc[...] = p

        @pl.when(t > 0)
        def _():
            acc[...] += p

        @pl.when(t == nt - 1)
        def _():
            o_ref[...] = acc[...].astype(o_ref.dtype)

    return pl.pallas_call(
        body, grid=(N // tn, nt),
        in_specs=[pl.BlockSpec((tt, K), lambda j, t: (t, 0)),
                  pl.BlockSpec((tt, tn), lambda j, t: (t, j))],
        out_specs=pl.BlockSpec((K, tn), lambda j, t: (0, j)),
        out_shape=SDS((K, N), out_dtype),
        scratch_shapes=[pltpu.VMEM((K, tn), F32)],
        compiler_params=_params("parallel", "arbitrary"), name=name)(a, b)


def _gather_classes(src_ref, dst, r):
    L = SEQ // r
    for c in range(r):
        dst[c * L:(c + 1) * L, :] = src_ref[0, pl.ds(c, L, stride=r), :].astype(dst.dtype)


def _scatter_classes(src, dst, r):
    L = SEQ // r
    for c in range(r):
        dst[pl.ds(c, L, stride=r), :] = src[c * L:(c + 1) * L, :].astype(dst.dtype)


def _attn_masks(slope_r):
    qi = lax.broadcasted_iota(jnp.int32, (Q_BLOCK, Q_BLOCK), 0)
    kj = lax.broadcasted_iota(jnp.int32, (Q_BLOCK, Q_BLOCK), 1)
    rel = (qi - kj).astype(F32)
    bias_cur = jnp.where(qi >= kj, -slope_r * rel, NEG)
    bias_prev = jnp.where(qi <= kj, -slope_r * (rel + float(Q_BLOCK)), NEG)
    return bias_cur, bias_prev


def _store_biases(bias, sl_ref, g, hp):
    for hh in range(2):
        cur, prev = _attn_masks(sl_ref[g * HEADS_PER_GROUP + 2 * hp + hh])
        bias[0, hh * Q_BLOCK:(hh + 1) * Q_BLOCK, :] = cur
        bias[1, hh * Q_BLOCK:(hh + 1) * Q_BLOCK, :] = prev


def _transpose_blocks(src, dst):
    for b in range(SEQ // Q_BLOCK):
        dst[b] = src[b * Q_BLOCK:(b + 1) * Q_BLOCK, :].T


def _stack_heads(t, low):
    z = jnp.zeros_like(t)
    return jnp.concatenate([jnp.where(low, t, z), jnp.where(low, z, t)], axis=0)


def _unstack_heads(t2, low):
    return jnp.where(low, t2[:Q_BLOCK], t2[Q_BLOCK:])


def _unit_offsets(u, nb):
    off = pl.multiple_of(u * Q_BLOCK, Q_BLOCK)
    offp = pl.multiple_of(jnp.maximum(u - 1, 0) * Q_BLOCK, Q_BLOCK)
    n = u & (nb - 1)
    c = u >> int(math.log2(nb))
    return off, offp, n == 0, c, n


def _attn_fwd(qkv, slopes_r, batch):
    nblk = SEQ // Q_BLOCK

    def body(sl_ref, *refs):
        qkv_refs = refs[:9]
        att_ref, lse_ref = refs[9:11]
        qd, kd, vd, kt, opos, lpos, bias = refs[11:]
        hp = pl.program_id(1)
        low = lax.broadcasted_iota(jnp.int32, (Q_BLOCK, LANES), 1) < HEAD_DIM

        for g in range(3):
            r = GROUPS[g][1]
            nb = SEQ // r // Q_BLOCK
            _gather_classes(qkv_refs[3 * g], qd, r)
            _gather_classes(qkv_refs[3 * g + 1], kd, r)
            _gather_classes(qkv_refs[3 * g + 2], vd, r)
            _transpose_blocks(kd, kt)
            _store_biases(bias, sl_ref, g, hp)

            def unit(u, carry, g=g, r=r, nb=nb):
                off, offp, first, c, n = _unit_offsets(u, nb)
                q2 = _stack_heads(qd[pl.ds(off, Q_BLOCK), :], low)
                vc = vd[pl.ds(off, Q_BLOCK), :]
                vp = vd[pl.ds(offp, Q_BLOCK), :]
                sc = _dot(q2, kt[u]) * 0.125 + bias[0]
                sp = jnp.where(first, NEG, _dot(q2, kt[jnp.maximum(u - 1, 0)]) * 0.125 + bias[1])
                m = jnp.max(jnp.maximum(sc, sp), axis=-1, keepdims=True)
                pc = jnp.exp(sc - m)
                pp = jnp.exp(sp - m)
                l = jnp.sum(pc + pp, axis=-1, keepdims=True)
                o2 = (_dot(pc.astype(BF16), vc) + _dot(pp.astype(BF16), vp)) * (1.0 / l)
                lse2 = m + jnp.log(l)
                rows = pl.ds(c + n * (Q_BLOCK * r), Q_BLOCK, stride=r)
                opos[g, rows, :] = _unstack_heads(o2, low)
                lpos[g, rows, :] = jnp.where(low, lse2[:Q_BLOCK], lse2[Q_BLOCK:])
                return carry

            lax.fori_loop(0, nblk, unit, 0, unroll=2)

        def merge(i, carry):
            rows = pl.ds(pl.multiple_of(i * 256, 256), 256)
            l0, l1, l2 = lpos[0, rows, :], lpos[1, rows, :], lpos[2, rows, :]
            m = jnp.maximum(jnp.maximum(l0, l1), l2)
            e0, e1, e2 = jnp.exp(l0 - m), jnp.exp(l1 - m), jnp.exp(l2 - m)
            den = e0 + e1 + e2
            att = (e0 * opos[0, rows, :] + e1 * opos[1, rows, :] + e2 * opos[2, rows, :]) / den
            att_ref[0, rows, :] = att.astype(att_ref.dtype)
            lse_ref[0, rows, :] = m + jnp.log(den)
            return carry

        lax.fori_loop(0, SEQ // 256, merge, 0)

    def col(sec, g):
        return pl.BlockSpec((1, SEQ, LANES), lambda b, hp: (b, 0, sec * 12 + g * 4 + hp))

    out = pl.BlockSpec((1, SEQ, LANES), lambda b, hp: (b, 0, hp))
    return pl.pallas_call(
        body, grid=(batch, 4),
        in_specs=[pl.BlockSpec(memory_space=pltpu.SMEM)] + [col(sec, g) for g in range(3) for sec in range(3)],
        out_specs=[out, out],
        out_shape=[SDS((batch, SEQ, ATTN_OUT), BF16), SDS((batch, SEQ, ATTN_OUT), F32)],
        scratch_shapes=[pltpu.VMEM((SEQ, LANES), BF16), pltpu.VMEM((SEQ, LANES), BF16), pltpu.VMEM((SEQ, LANES), BF16),
                        pltpu.VMEM((nblk, LANES, Q_BLOCK), BF16),
                        pltpu.VMEM((3, SEQ, LANES), F32), pltpu.VMEM((3, SEQ, LANES), F32),
                        pltpu.VMEM((2, 2 * Q_BLOCK, Q_BLOCK), F32)],
        compiler_params=_params("parallel", "parallel"), name="attn_fwd")(slopes_r, *([qkv] * 9))


def _attn_bwd(qkv, datt, lse, dsum, slopes_r, batch):
    nblk = SEQ // Q_BLOCK

    def body(sl_ref, q_ref, k_ref, v_ref, do_ref, l_ref, d_ref, dq_ref, dk_ref, dv_ref,
             qd, kd, vd, dod, kt, vt, ld, dd, dq_acc, dk_acc, dv_acc, stage, bias):
        gid, hp = pl.program_id(1), pl.program_id(2)
        low = lax.broadcasted_iota(jnp.int32, (Q_BLOCK, LANES), 1) < HEAD_DIM

        def section(g):
            r = GROUPS[g][1]
            nb = SEQ // r // Q_BLOCK
            _gather_classes(q_ref, qd, r)
            _gather_classes(k_ref, kd, r)
            _gather_classes(v_ref, vd, r)
            _gather_classes(do_ref, dod, r)
            _gather_classes(l_ref, ld, r)
            _gather_classes(d_ref, dd, r)
            _transpose_blocks(kd, kt)
            _transpose_blocks(vd, vt)
            _store_biases(bias, sl_ref, g, hp)
            dk_acc[...] = jnp.zeros_like(dk_acc)
            dv_acc[...] = jnp.zeros_like(dv_acc)

            def unit(u, carry):
                off, offp, first, _, _ = _unit_offsets(u, nb)
                up = jnp.maximum(u - 1, 0)
                q2 = _stack_heads(qd[pl.ds(off, Q_BLOCK), :], low)
                do2 = _stack_heads(dod[pl.ds(off, Q_BLOCK), :], low)
                kc = kd[pl.ds(off, Q_BLOCK), :]
                kp = kd[pl.ds(offp, Q_BLOCK), :]
                lse_t = ld[pl.ds(off, Q_BLOCK), :]
                dsum_t = dd[pl.ds(off, Q_BLOCK), :]
                lse2 = jnp.concatenate([lse_t[:, 0:1], lse_t[:, HEAD_DIM:HEAD_DIM + 1]], axis=0)
                dsum2 = jnp.concatenate([dsum_t[:, 0:1], dsum_t[:, HEAD_DIM:HEAD_DIM + 1]], axis=0)
                sc = _dot(q2, kt[u]) * 0.125 + bias[0]
                sp = jnp.where(first, NEG, _dot(q2, kt[up]) * 0.125 + bias[1])
                pc = jnp.exp(sc - lse2)
                pp = jnp.exp(sp - lse2)
                dsc = (pc * (_dot(do2, vt[u]) - dsum2)).astype(BF16)
                dsp = (pp * (_dot(do2, vt[up]) - dsum2)).astype(BF16)
                dq2 = _dot(dsc, kc) + _dot(dsp, kp)
                dq_acc[pl.ds(off, Q_BLOCK), :] = _unstack_heads(dq2, low) * 0.125
                dk_acc[pl.ds(off, Q_BLOCK), :] += _dot_tn(dsc, q2) * 0.125
                dk_acc[pl.ds(offp, Q_BLOCK), :] += _dot_tn(dsp, q2) * 0.125
                dv_acc[pl.ds(off, Q_BLOCK), :] += _dot_tn(pc.astype(BF16), do2)
                dv_acc[pl.ds(offp, Q_BLOCK), :] += _dot_tn(pp.astype(BF16), do2)
                return carry

            lax.fori_loop(0, nblk, unit, 0, unroll=2)
            for acc, out_ref in ((dq_acc, dq_ref), (dk_acc, dk_ref), (dv_acc, dv_ref)):
                _scatter_classes(acc, stage, r)
                out_ref[0] = stage[...].astype(out_ref.dtype)

        for g in range(3):
            pl.when(gid == g)(lambda g=g: section(g))

    def col(sec):
        return pl.BlockSpec((1, SEQ, LANES), lambda b, g, hp: (b, 0, sec * 12 + g * 4 + hp))

    pos = pl.BlockSpec((1, SEQ, LANES), lambda b, g, hp: (b, 0, hp))
    dout = pl.BlockSpec((1, SEQ, LANES), lambda b, g, hp: (b, 0, g * 4 + hp))
    out = SDS((batch, SEQ, ATTN_WIDTH), BF16)
    seq_bf = pltpu.VMEM((SEQ, LANES), BF16)
    seq_f = pltpu.VMEM((SEQ, LANES), F32)
    blk_t = pltpu.VMEM((nblk, LANES, Q_BLOCK), BF16)
    return pl.pallas_call(
        body, grid=(batch, 3, 4),
        in_specs=[pl.BlockSpec(memory_space=pltpu.SMEM), col(0), col(1), col(2), pos, pos, pos],
        out_specs=[dout, dout, dout],
        out_shape=[out, out, out],
        scratch_shapes=[seq_bf, seq_bf, seq_bf, seq_bf, blk_t, blk_t, seq_f, seq_f, seq_f, seq_f, seq_f, seq_f,
                        pltpu.VMEM((2, 2 * Q_BLOCK, Q_BLOCK), F32)],
        compiler_params=_params("parallel", "parallel", "parallel"), name="attn_bwd")(
            slopes_r, qkv, qkv, qkv, datt, lse, dsum)


CONV_TC = 128
CONV_ROWS = 128
SUBLANES = 8


def _fill_shifted(sh):
    n = SEQ + CONV_PAD - SUBLANES
    for s in range(1, SUBLANES):
        sh[s, 0:n, :] = sh[0, s:s + n, :]


def _tap(sh, base, offset):
    s = offset % SUBLANES
    return sh[s, pl.ds(pl.multiple_of(base + (offset - s), SUBLANES), CONV_ROWS), :]


def _conv_fwd(u, conv_w, conv_b, batch):
    nct = D_MODEL // CONV_TC

    def body(ua_ref, ub_ref, w_ref, b_ref, o_ref, sh):
        sh[0, 0:CONV_PAD, :] = jnp.zeros((CONV_PAD, CONV_TC), F32)
        sh[0, CONV_PAD:, :] = ua_ref[0] * _sigmoid(ub_ref[0])
        _fill_shifted(sh)

        def chunk(c, carry):
            base = pl.multiple_of(c * CONV_ROWS, CONV_ROWS)
            acc = jnp.broadcast_to(b_ref[...], (CONV_ROWS, CONV_TC))
            for t in range(CONV_K):
                acc = acc + _tap(sh, base, t + CONV_PAD - (CONV_K - 1)) * w_ref[t:t + 1, :]
            o_ref[0, pl.ds(base, CONV_ROWS), :] = acc
            return carry

        lax.fori_loop(0, SEQ // CONV_ROWS, chunk, 0)

    return pl.pallas_call(
        body, grid=(nct, batch),
        in_specs=[pl.BlockSpec((1, SEQ, CONV_TC), lambda j, b: (b, 0, j)),
                  pl.BlockSpec((1, SEQ, CONV_TC), lambda j, b: (b, 0, j + nct)),
                  pl.BlockSpec((CONV_PAD, CONV_TC), lambda j, b: (0, j)),
                  pl.BlockSpec((1, CONV_TC), lambda j, b: (0, j))],
        out_specs=pl.BlockSpec((1, SEQ, CONV_TC), lambda j, b: (b, 0, j)),
        out_shape=SDS((batch, SEQ, D_MODEL), F32),
        scratch_shapes=[pltpu.VMEM((SUBLANES, SEQ + CONV_PAD, CONV_TC), F32)],
        compiler_params=_params("parallel", "parallel"), name="conv_fwd")(u, u, conv_w, conv_b)


def _conv_bwd(u, dc1, conv_w, batch, dep=None):
    nct = D_MODEL // CONV_TC
    nchunk = SEQ // CONV_ROWS

    def body(ua_ref, ub_ref, d_ref, w_ref, dua_ref, dub_ref, gw_ref, gb_ref, shc, shd, gacc):
        b = pl.program_id(1)
        shc[0, 0:CONV_PAD, :] = jnp.zeros((CONV_PAD, CONV_TC), F32)
        shc[0, CONV_PAD:, :] = ua_ref[0] * _sigmoid(ub_ref[0])
        _fill_shifted(shc)
        shd[0, 0:SEQ, :] = d_ref[0]
        shd[0, SEQ:, :] = jnp.zeros((CONV_PAD, CONV_TC), F32)
        _fill_shifted(shd)

        @pl.when(b == 0)
        def _():
            gacc[...] = jnp.zeros_like(gacc)
            gb_ref[...] = jnp.zeros_like(gb_ref)

        gb_ref[...] += _rowsum(d_ref[0])

        def chunk(c, carry):
            base = pl.multiple_of(c * CONV_ROWS, CONV_ROWS)
            dcur = shd[0, pl.ds(base, CONV_ROWS), :]
            acc = jnp.zeros((CONV_ROWS, CONV_TC), F32)
            for t in range(CONV_K):
                acc = acc + _tap(shd, base, CONV_K - 1 - t) * w_ref[t:t + 1, :]
                prod = _tap(shc, base, t + CONV_PAD - (CONV_K - 1)) * dcur
                gacc[t] += jnp.sum(prod.reshape(CONV_ROWS // 8, 8, CONV_TC), axis=0)
            ua = ua_ref[0, pl.ds(base, CONV_ROWS), :]
            sg = _sigmoid(ub_ref[0, pl.ds(base, CONV_ROWS), :])
            dua_ref[0, pl.ds(base, CONV_ROWS), :] = (acc * sg).astype(dua_ref.dtype)
            dub_ref[0, pl.ds(base, CONV_ROWS), :] = (acc * ua * sg * (1.0 - sg)).astype(dub_ref.dtype)
            return carry

        lax.fori_loop(0, nchunk, chunk, 0)

        @pl.when(b == batch - 1)
        def _():
            for t in range(CONV_K):
                gw_ref[t:t + 1, :] = jnp.sum(gacc[t], axis=0, keepdims=True)
            gw_ref[CONV_K:CONV_PAD, :] = jnp.zeros((CONV_PAD - CONV_K, CONV_TC), F32)

    du = SDS((batch, SEQ, D_MODEL), BF16)
    body, dep_spec, dep_arg = _anchored(body, 4, dep)
    return pl.pallas_call(
        body, grid=(nct, batch),
        in_specs=[pl.BlockSpec((1, SEQ, CONV_TC), lambda j, b: (b, 0, j)),
                  pl.BlockSpec((1, SEQ, CONV_TC), lambda j, b: (b, 0, j + nct)),
                  pl.BlockSpec((1, SEQ, CONV_TC), lambda j, b: (b, 0, j)),
                  pl.BlockSpec((CONV_PAD, CONV_TC), lambda j, b: (0, j))] + dep_spec,
        out_specs=[pl.BlockSpec((1, SEQ, CONV_TC), lambda j, b: (b, 0, j)),
                   pl.BlockSpec((1, SEQ, CONV_TC), lambda j, b: (b, 0, j)),
                   pl.BlockSpec((CONV_PAD, CONV_TC), lambda j, b: (0, j)),
                   pl.BlockSpec((1, CONV_TC), lambda j, b: (0, j))],
        out_shape=[du, du, SDS((CONV_PAD, D_MODEL), F32), SDS((1, D_MODEL), F32)],
        scratch_shapes=[pltpu.VMEM((SUBLANES, SEQ + CONV_PAD, CONV_TC), F32),
                        pltpu.VMEM((SUBLANES, SEQ + CONV_PAD, CONV_TC), F32),
                        pltpu.VMEM((CONV_K, 8, CONV_TC), F32)],
        compiler_params=_params("parallel", "arbitrary"), name="conv_bwd")(u, u, dc1, conv_w, *dep_arg)


MID_TM = 256


def _layernorm_stats(c1):
    mu = jnp.mean(c1, axis=-1, keepdims=True)
    cen = c1 - mu
    rs = lax.rsqrt(jnp.mean(cen * cen, axis=-1, keepdims=True) + LN_EPS)
    return cen * rs, rs


def _mid_fwd(att, c1, logits, x, w_a, w_c, w_o, gate_b, ln_g, ln_b, g2, dep=None):
    T = x.shape[0]
    tm = MID_TM

    def body(att_ref, c1_ref, lg_ref, x_ref, wa_ref, wc_ref, wo_ref, gb_ref, lng_ref, lnb_ref, g2_ref,
             c3_ref, ya_ref, yc_ref, mix_ref, x1_ref, h2_ref):
        ya = _dot(att_ref[...], wa_ref[...])
        xh, _ = _layernorm_stats(c1_ref[...])
        c2 = xh * lng_ref[...] + lnb_ref[...]
        c3 = (c2 * _sigmoid(c2)).astype(BF16)
        c3_ref[...] = c3
        yc = _dot(c3, wc_ref[...])
        gates = _sigmoid(lg_ref[...] + gb_ref[...])
        mix = (gates[:, :D_MODEL] * ya + gates[:, D_MODEL:] * yc).astype(BF16)
        ya_ref[...] = ya.astype(BF16)
        yc_ref[...] = yc.astype(BF16)
        mix_ref[...] = mix
        x1 = x_ref[...] + _dot(mix, wo_ref[...])
        x1_ref[...] = x1
        r = lax.rsqrt(jnp.mean(x1 * x1, axis=-1, keepdims=True) + RMS_EPS)
        h2_ref[...] = (x1 * r * g2_ref[...]).astype(BF16)

    row = lambda n: pl.BlockSpec((tm, n), lambda i: (i, 0))
    full = lambda a, b: pl.BlockSpec((a, b), lambda i: (0, 0))
    body, dep_spec, dep_arg = _anchored(body, 11, dep)
    return pl.pallas_call(
        body, grid=(T // tm,),
        in_specs=[row(ATTN_OUT), row(D_MODEL), row(2 * D_MODEL), row(D_MODEL),
                  full(ATTN_OUT, D_MODEL), full(D_MODEL, D_MODEL), full(D_MODEL, D_MODEL),
                  full(1, 2 * D_MODEL), full(1, D_MODEL), full(1, D_MODEL), full(1, D_MODEL)] + dep_spec,
        out_specs=[row(D_MODEL), row(D_MODEL), row(D_MODEL), row(D_MODEL), row(D_MODEL), row(D_MODEL)],
        out_shape=[SDS((T, D_MODEL), BF16), SDS((T, D_MODEL), BF16), SDS((T, D_MODEL), BF16), SDS((T, D_MODEL), BF16),
                   SDS((T, D_MODEL), F32), SDS((T, D_MODEL), BF16)],
        compiler_params=_params("parallel"), name="mid_fwd")(att, c1, logits, x, w_a, w_c, w_o, gate_b, ln_g, ln_b, g2,
                                                             *dep_arg)


def _mid_bwd(dx1b, ya, yc, logits, att, c1, w_a, w_c, w_o, gate_b, ln_g, ln_b, head_ones, dep=None):
    T = dx1b.shape[0]
    tm = MID_TM

    def body(dx_ref, ya_ref, yc_ref, lg_ref, att_ref, c1_ref, wa_ref, wc_ref, wo_ref, gb_ref, lng_ref, lnb_ref, e_ref,
             dlg_ref, dya_ref, dyc_ref, datt_ref, dsum_ref, dc1_ref, ggb_ref, glg_ref, glb_ref):
        @pl.when(pl.program_id(0) == 0)
        def _():
            ggb_ref[...] = jnp.zeros_like(ggb_ref)
            glg_ref[...] = jnp.zeros_like(glg_ref)
            glb_ref[...] = jnp.zeros_like(glb_ref)

        dmix = _dot_nt(dx_ref[...], wo_ref[...])
        gates = _sigmoid(lg_ref[...] + gb_ref[...])
        ga, gc = gates[:, :D_MODEL], gates[:, D_MODEL:]
        dla = dmix * ya_ref[...].astype(F32) * ga * (1.0 - ga)
        dlc = dmix * yc_ref[...].astype(F32) * gc * (1.0 - gc)
        dlg_ref[:, :D_MODEL] = dla.astype(BF16)
        dlg_ref[:, D_MODEL:] = dlc.astype(BF16)
        ggb_ref[:, :D_MODEL] += _rowsum(dla)
        ggb_ref[:, D_MODEL:] += _rowsum(dlc)
        dya = (dmix * ga).astype(BF16)
        dyc = (dmix * gc).astype(BF16)
        dya_ref[...] = dya
        dyc_ref[...] = dyc
        datt = _dot_nt(dya, wa_ref[...])
        datt_ref[...] = datt
        dsum_ref[...] = jnp.dot(datt * att_ref[...].astype(F32), e_ref[...], preferred_element_type=F32,
                                precision=lax.Precision.HIGHEST)
        dc3 = _dot_nt(dyc, wc_ref[...])
        xh, rs = _layernorm_stats(c1_ref[...])
        c2 = xh * lng_ref[...] + lnb_ref[...]
        sg = _sigmoid(c2)
        dc2 = dc3 * (sg * (1.0 + c2 * (1.0 - sg)))
        glg_ref[...] += _rowsum(dc2 * xh)
        glb_ref[...] += _rowsum(dc2)
        dxh = dc2 * lng_ref[...]
        dc1_ref[...] = rs * (dxh - jnp.mean(dxh, axis=-1, keepdims=True) - xh * jnp.mean(dxh * xh, axis=-1, keepdims=True))

    row = lambda n: pl.BlockSpec((tm, n), lambda i: (i, 0))
    full = lambda a, b: pl.BlockSpec((a, b), lambda i: (0, 0))
    body, dep_spec, dep_arg = _anchored(body, 13, dep)
    return pl.pallas_call(
        body, grid=(T // tm,),
        in_specs=[row(D_MODEL), row(D_MODEL), row(D_MODEL), row(2 * D_MODEL), row(ATTN_OUT), row(D_MODEL),
                  full(ATTN_OUT, D_MODEL), full(D_MODEL, D_MODEL), full(D_MODEL, D_MODEL),
                  full(1, 2 * D_MODEL), full(1, D_MODEL), full(1, D_MODEL), full(ATTN_OUT, ATTN_OUT)] + dep_spec,
        out_specs=[row(2 * D_MODEL), row(D_MODEL), row(D_MODEL), row(ATTN_OUT), row(ATTN_OUT), row(D_MODEL),
                   full(1, 2 * D_MODEL), full(1, D_MODEL), full(1, D_MODEL)],
        out_shape=[SDS((T, 2 * D_MODEL), BF16), SDS((T, D_MODEL), BF16), SDS((T, D_MODEL), BF16), SDS((T, ATTN_OUT), F32),
                   SDS((T, ATTN_OUT), F32), SDS((T, D_MODEL), F32),
                   SDS((1, 2 * D_MODEL), F32), SDS((1, D_MODEL), F32), SDS((1, D_MODEL), F32)],
        compiler_params=_params("arbitrary"), name="mid_bwd")(dx1b, ya, yc, logits, att, c1, w_a, w_c, w_o, gate_b, ln_g, ln_b,
                                                               head_ones, *dep_arg)


FFN_TM = 512
FFN_TF = D_FF // 2


def _rms_bwd(dy_times_g, xh, r):
    return r * (dy_times_g - xh * jnp.mean(dy_times_g * xh, axis=-1, keepdims=True))


def _ffn_fwd(h2, x1, target, gf, w_g, w_u, w_d):
    T = h2.shape[0]
    tm, tf = FFN_TM, FFN_TF
    nf = D_FF // tf

    def body(h_ref, x1_ref, t_ref, gf_ref, wg_ref, wu_ref, wd_ref,
             a_ref, b_ref, f_ref, dx2_ref, dx2b_ref, loss_ref, gnf_ref, acc):
        i, j = pl.program_id(0), pl.program_id(1)
        h = h_ref[...]
        a = _dot(h, wg_ref[...])
        b = _dot(h, wu_ref[...])
        f = (a * _sigmoid(a) * b).astype(BF16)
        a_ref[...] = a.astype(BF16)
        b_ref[...] = b.astype(BF16)
        f_ref[...] = f
        p = _dot(f, wd_ref[...])

        @pl.when(j == 0)
        def _():
            acc[...] = x1_ref[...] + p

        @pl.when(j > 0)
        def _():
            acc[...] += p

        @pl.when((i == 0) & (j == nf - 1))
        def _():
            loss_ref[...] = jnp.zeros_like(loss_ref)
            gnf_ref[...] = jnp.zeros_like(gnf_ref)

        @pl.when(j == nf - 1)
        def _():
            x2 = acc[...]
            r = lax.rsqrt(jnp.mean(x2 * x2, axis=-1, keepdims=True) + RMS_EPS)
            xh = x2 * r
            err = xh * gf_ref[...] - t_ref[...]
            loss_ref[...] += (0.5 / D_MODEL) * jnp.sum(err * err)
            dy = err * (1.0 / D_MODEL)
            gnf_ref[...] += _rowsum(dy * xh)
            dx2 = _rms_bwd(dy * gf_ref[...], xh, r)
            dx2_ref[...] = dx2
            dx2b_ref[...] = dx2.astype(BF16)

    row = lambda n: pl.BlockSpec((tm, n), lambda i, j: (i, 0))
    ffb = pl.BlockSpec((tm, tf), lambda i, j: (i, j))
    return pl.pallas_call(
        body, grid=(T // tm, nf),
        in_specs=[row(D_MODEL), row(D_MODEL), row(D_MODEL), pl.BlockSpec((1, D_MODEL), lambda i, j: (0, 0)),
                  pl.BlockSpec((D_MODEL, tf), lambda i, j: (0, j)), pl.BlockSpec((D_MODEL, tf), lambda i, j: (0, j)),
                  pl.BlockSpec((tf, D_MODEL), lambda i, j: (j, 0))],
        out_specs=[ffb, ffb, ffb, row(D_MODEL), row(D_MODEL),
                   pl.BlockSpec((1, 128), lambda i, j: (0, 0)), pl.BlockSpec((1, D_MODEL), lambda i, j: (0, 0))],
        out_shape=[SDS((T, D_FF), BF16), SDS((T, D_FF), BF16), SDS((T, D_FF), BF16), SDS((T, D_MODEL), F32),
                   SDS((T, D_MODEL), BF16), SDS((1, 128), F32), SDS((1, D_MODEL), F32)],
        scratch_shapes=[pltpu.VMEM((tm, D_MODEL), F32)],
        compiler_params=_params("arbitrary", "arbitrary"), name="ffn_fwd")(h2, x1, target, gf, w_g, w_u, w_d)


def _ffn_bwd(dx2b, dx2, a, b, x1, g2, w_g, w_u, w_d):
    T = dx2.shape[0]
    tm, tf = FFN_TM, FFN_TF
    nf = D_FF // tf

    def body(dxb_ref, dx2_ref, a_ref, b_ref, x1_ref, g2_ref, wg_ref, wu_ref, wd_ref,
             da_ref, db_ref, dx1_ref, dx1b_ref, gn2_ref, acc):
        i, j = pl.program_id(0), pl.program_id(1)
        df = _dot_nt(dxb_ref[...], wd_ref[...])
        av = a_ref[...].astype(F32)
        bv = b_ref[...].astype(F32)
        sg = _sigmoid(av)
        db = (df * av * sg).astype(BF16)
        da = (df * bv * (sg * (1.0 + av * (1.0 - sg)))).astype(BF16)
        da_ref[...] = da
        db_ref[...] = db
        p = _dot_nt(da, wg_ref[...]) + _dot_nt(db, wu_ref[...])

        @pl.when(j == 0)
        def _():
            acc[...] = p

        @pl.when(j > 0)
        def _():
            acc[...] += p

        @pl.when((i == 0) & (j == nf - 1))
        def _():
            gn2_ref[...] = jnp.zeros_like(gn2_ref)

        @pl.when(j == nf - 1)
        def _():
            dh2 = acc[...]
            x1 = x1_ref[...]
            r = lax.rsqrt(jnp.mean(x1 * x1, axis=-1, keepdims=True) + RMS_EPS)
            xh = x1 * r
            gn2_ref[...] += _rowsum(dh2 * xh)
            dx1 = dx2_ref[...] + _rms_bwd(dh2 * g2_ref[...], xh, r)
            dx1_ref[...] = dx1
            dx1b_ref[...] = dx1.astype(BF16)

    row = lambda n: pl.BlockSpec((tm, n), lambda i, j: (i, 0))
    ffb = pl.BlockSpec((tm, tf), lambda i, j: (i, j))
    return pl.pallas_call(
        body, grid=(T // tm, nf),
        in_specs=[row(D_MODEL), row(D_MODEL), ffb, ffb, row(D_MODEL), pl.BlockSpec((1, D_MODEL), lambda i, j: (0, 0)),
                  pl.BlockSpec((D_MODEL, tf), lambda i, j: (0, j)), pl.BlockSpec((D_MODEL, tf), lambda i, j: (0, j)),
                  pl.BlockSpec((tf, D_MODEL), lambda i, j: (j, 0))],
        out_specs=[ffb, ffb, row(D_MODEL), row(D_MODEL), pl.BlockSpec((1, D_MODEL), lambda i, j: (0, 0))],
        out_shape=[SDS((T, D_FF), BF16), SDS((T, D_FF), BF16), SDS((T, D_MODEL), F32), SDS((T, D_MODEL), BF16),
                   SDS((1, D_MODEL), F32)],
        scratch_shapes=[pltpu.VMEM((tm, D_MODEL), F32)],
        compiler_params=_params("arbitrary", "arbitrary"), name="ffn_bwd")(dx2b, dx2, a, b, x1, g2, w_g, w_u, w_d)


def _in_bwd(pieces, w_in, x, dx1, g1, dep=None):
    T = x.shape[0]
    tm = IN_TM
    npc = len(pieces)
    assert sum(p.shape[1] for p in pieces) == IN_WIDTH

    def body(*refs):
        p_refs = refs[:npc]
        w_hbm, x_ref, dx1_ref, g_ref, dx_ref, gn1_ref, w_vmem, sem = refs[npc:]

        @pl.when(pl.program_id(0) == 0)
        def _():
            cp = pltpu.make_async_copy(w_hbm, w_vmem, sem)
            cp.start()
            cp.wait()
            gn1_ref[...] = jnp.zeros_like(gn1_ref)

        dh = jnp.zeros((tm, D_MODEL), F32)
        col = 0
        for p_ref in p_refs:
            for j in range(p_ref.shape[1] // IN_CHUNK):
                dh = dh + _dot_nt(p_ref[:, j * IN_CHUNK:(j + 1) * IN_CHUNK], w_vmem[:, col:col + IN_CHUNK])
                col += IN_CHUNK
        xv = x_ref[...]
        r = lax.rsqrt(jnp.mean(xv * xv, axis=-1, keepdims=True) + RMS_EPS)
        xh = xv * r
        gn1_ref[...] += _rowsum(dh * xh)
        dx_ref[...] = dx1_ref[...] + _rms_bwd(dh * g_ref[...], xh, r)

    row = lambda n: pl.BlockSpec((tm, n), lambda i: (i, 0))
    body, dep_spec, dep_arg = _anchored(body, npc + 4, dep)
    return pl.pallas_call(
        body, grid=(T // tm,),
        in_specs=[row(p.shape[1]) for p in pieces]
        + [pl.BlockSpec(memory_space=pl.ANY), row(D_MODEL), row(D_MODEL), pl.BlockSpec((1, D_MODEL), lambda i: (0, 0))]
        + dep_spec,
        out_specs=[row(D_MODEL), pl.BlockSpec((1, D_MODEL), lambda i: (0, 0))],
        out_shape=[SDS((T, D_MODEL), F32), SDS((1, D_MODEL), F32)],
        scratch_shapes=[pltpu.VMEM((D_MODEL, IN_WIDTH), BF16), pltpu.SemaphoreType.DMA],
        compiler_params=_params("arbitrary"), name="in_bwd")(*pieces, w_in, x, dx1, g1, *dep_arg)


def _local_step(x, target, w, small, dep=None, late_weights=None, emit=None):
    T = x.shape[0]
    batch = T // SEQ
    slopes_r = jnp.asarray(_slopes_times_dilation())
    emit = emit or (lambda names, grads: None)

    h, qkv, u, logits = _in_proj(x, small["norm1_g"], w["w_in"], dep)

    qkv3 = qkv.reshape(batch, SEQ, 3 * ATTN_WIDTH)
    att, lse = _attn_fwd(qkv3, slopes_r, batch)
    att = att.reshape(T, ATTN_OUT)

    u3 = u.reshape(batch, SEQ, 2 * D_MODEL)
    c1 = _conv_fwd(u3, w["conv_w"], small["conv_b"], batch).reshape(T, D_MODEL)
    if late_weights is not None:
        w = {**w, **late_weights(LATE_MERGE, c1)}

    c3, ya, yc, mix, x1, h2 = _mid_fwd(
        att, c1, logits, x, w["w_attn_out"], w["w_conv_out"], w["w_o"],
        small["gate_b"], small["conv_ln_g"], small["conv_ln_b"], small["norm2_g"], w.get("token"))
    if late_weights is not None:
        w = {**w, **late_weights(LATE_FFN, h2)}

    a, b, f, dx2, dx2b, loss, g_normf = _ffn_fwd(h2, x1, target, small["norm_f_g"],
                                                   w["w_ffn_gate"], w["w_ffn_up"], w["w_ffn_down"])

    da, db, dx1, dx1b, g_norm2 = _ffn_bwd(dx2b, dx2, a, b, x1, small["norm2_g"],
                                           w["w_ffn_gate"], w["w_ffn_up"], w["w_ffn_down"])
    gw = {}
    gw["w_ffn_down"] = _mm_tn(f, dx2b, BF16, "gw_ffn_down", tn=512)
    gw["w_ffn_gate"] = _mm_tn(h2, da, BF16, "gw_ffn_gate", tn=1408)
    gw["w_ffn_up"] = _mm_tn(h2, db, BF16, "gw_ffn_up", tn=1408)
    token = emit(("w_ffn_gate", "w_ffn_up", "w_ffn_down"), gw)

    head_ones = jnp.asarray(np.kron(np.eye(HEADS_PER_GROUP, dtype=np.float32), np.ones((HEAD_DIM, HEAD_DIM), np.float32)))
    dlogits, dya, dyc, datt, dsum, dc1, g_gate_b, g_ln_g, g_ln_b = _mid_bwd(
        dx1b, ya, yc, logits, att, c1, w["w_attn_out"], w["w_conv_out"], w["w_o"],
        small["gate_b"], small["conv_ln_g"], small["conv_ln_b"], head_ones, token)
    gw["w_o"] = _mm_tn(mix, dx1b, BF16, "gw_o", tn=512)
    gw["w_attn_out"] = _mm_tn(att, dya, BF16, "gw_attn_out", tn=512)
    gw["w_conv_out"] = _mm_tn(c3, dyc, BF16, "gw_conv_out", tn=512)
    token = emit(("w_conv_out", "w_attn_out", "w_o"), gw)

    dua, dub, g_conv_w, g_conv_b = _conv_bwd(u3, dc1.reshape(batch, SEQ, D_MODEL), w["conv_w"], batch, token)

    dq, dk, dv = _attn_bwd(qkv3, datt.reshape(batch, SEQ, ATTN_OUT), lse, dsum.reshape(batch, SEQ, ATTN_OUT),
                           slopes_r, batch)
    pieces = [dq.reshape(T, ATTN_WIDTH), dk.reshape(T, ATTN_WIDTH), dv.reshape(T, ATTN_WIDTH),
              dua.reshape(T, D_MODEL), dub.reshape(T, D_MODEL), dlogits]

    names = ("q", "k", "v", "ua", "ub", "gate")
    gw["w_in"] = jnp.concatenate(
        [_mm_tn(h, p, BF16, "gw_in_" + nm, tn=min(p.shape[1], 1024) if p.shape[1] != ATTN_WIDTH else 768)
         for nm, p in zip(names, pieces)], axis=1)
    gw["conv_w"] = g_conv_w
    token = emit(("w_in", "conv_w"), gw)
    grad_x, g_norm1 = _in_bwd(pieces, w["w_in"], x, dx1, small["norm1_g"], token)

    gsmall = {"norm1_g": g_norm1, "gate_b": g_gate_b, "conv_b": g_conv_b, "conv_ln_g": g_ln_g, "conv_ln_b": g_ln_b,
              "norm2_g": g_norm2, "norm_f_g": g_normf}
    return loss, grad_x, gw, gsmall


ANY = pl.BlockSpec(memory_space=pl.ANY)


def _all_gather(arrs):
    n = len(arrs)

    def body(*refs):
        ins, outs = refs[:n], refs[n:2 * n]
        send_sems, recv_sems, local_sems = refs[2 * n:]
        x, y, c = lax.axis_index("x"), lax.axis_index("y"), lax.axis_index("c")
        me, sibling = (x, y, c), (x, y, 1 - c)
        chips = [(1 - x, y), (x, 1 - y), (1 - x, 1 - y)]

        def copy(a, k, block, to, src=None):
            px, py, pc = block
            dst = outs[a].at[4 * px + 2 * py + pc]
            return pltpu.make_async_remote_copy(
                src_ref=dst if src is None else src, dst_ref=dst,
                send_sem=send_sems.at[a, k], recv_sem=recv_sems.at[a, k], device_id=to, device_id_type=MESH)

        mine = [pltpu.make_async_copy(ins[a], outs[a].at[4 * x + 2 * y + c], local_sems.at[a]) for a in range(n)]
        for cp in mine:
            cp.start()
        first = []
        for j, chip in enumerate(chips):
            first += [copy(a, 1 + j, me, (*chip, c), src=ins[a]) for a in range(n)]
        first += [copy(a, 0, me, sibling, src=ins[a]) for a in range(n)]
        for cp in first:
            cp.start()
        passed = []
        for j, chip in enumerate(chips):
            for a in range(n):
                copy(a, 1 + j, (*chip, c), me).wait_recv()
                cp = copy(a, 4 + j, (*chip, c), sibling)
                cp.start()
                passed.append(cp)
        for a in range(n):
            copy(a, 0, sibling, me).wait_recv()
        for j, chip in enumerate(chips):
            for a in range(n):
                copy(a, 4 + j, (*chip, 1 - c), me).wait_recv()
        for cp in first + passed:
            cp.wait_send()
        for cp in mine:
            cp.wait()

    return pl.pallas_call(
        body, in_specs=[ANY] * n, out_specs=[ANY] * n,
        out_shape=[SDS((N_DEV,) + a.shape, a.dtype) for a in arrs],
        scratch_shapes=[pltpu.SemaphoreType.DMA((n, 7)), pltpu.SemaphoreType.DMA((n, 7)), pltpu.SemaphoreType.DMA((n,))],
        name="all_gather_weights")(*arrs)


HBM = pl.BlockSpec(memory_space=pltpu.HBM)
SEM = pl.BlockSpec(memory_space=pltpu.SEMAPHORE)
ALL_PEERS = tuple(range(1, N_DEV))
OTHER_CHIPS = (2, 4, 6)
SPLIT_EFFECT = pltpu.CompilerParams(has_side_effects=pltpu.SideEffectType.DATAFLOW_SIDE_EFFECTING)


def _exchange_copies(mode, ks, srcs, lands, send_sems, recv_sems):
    x, y, c = lax.axis_index("x"), lax.axis_index("y"), lax.axis_index("c")
    me = 4 * x + 2 * y + c
    send, recv = [], []
    for a in range(len(lands)):
        for i, k in enumerate(ks):
            peer = (x ^ ((k >> 2) & 1), y ^ ((k >> 1) & 1), c ^ (k & 1))
            pidx = 4 * peer[0] + 2 * peer[1] + peer[2]
            if mode == "gather":
                src, to, out_slot, in_slot = srcs[a], peer, me, pidx
            elif mode == "scatter":
                src, to, out_slot, in_slot = srcs[a].at[pidx], peer, me, pidx
            elif mode == "chip_scatter":
                src, to, out_slot, in_slot = srcs[a].at[pidx >> 1], peer, me >> 1, pidx >> 1
            else:
                src, to, out_slot, in_slot = lands[a].at[pidx], (x, y, 1 - c), pidx, pidx ^ 1
            s = a * len(ks) + i
            send.append(pltpu.make_async_remote_copy(
                src_ref=src, dst_ref=lands[a].at[out_slot], send_sem=send_sems.at[s], recv_sem=recv_sems.at[s],
                device_id=to, device_id_type=MESH))
            recv.append(pltpu.make_async_remote_copy(
                src_ref=src, dst_ref=lands[a].at[in_slot], send_sem=send_sems.at[s], recv_sem=recv_sems.at[s],
                device_id=to, device_id_type=MESH))
    return send, recv


def _send_start(mode, ks, name, srcs=(), lands=None):
    srcs = list(srcs)
    if lands is None:
        slots = 4 if mode == "chip_scatter" else N_DEV
        lands = [lax.empty((slots,) + (s.shape if mode == "gather" else s.shape[1:]), s.dtype) for s in srcs]
    ns, nl = len(srcs), len(lands)
    nsem = nl * len(ks)

    def body(*refs):
        send, _ = _exchange_copies(mode, ks, refs[:ns], refs[ns:ns + nl], refs[ns + nl], refs[ns + nl + 1])
        for cp in send:
            cp.start()
        token = refs[-1]
        token[...] = jnp.zeros_like(token)

    both = srcs + list(lands)
    res = pl.pallas_call(
        body, name=name,
        out_shape=(pltpu.SemaphoreType.DMA((nsem,)), pltpu.SemaphoreType.DMA((nsem,)),
                   *[pltpu.HBM(a.shape, a.dtype) for a in both], SDS((8, 128), F32)),
        in_specs=[HBM] * (ns + nl), out_specs=(SEM, SEM, *([HBM] * (ns + nl)), pl.BlockSpec(memory_space=pltpu.VMEM)),
        input_output_aliases={i: 2 + i for i in range(ns + nl)}, compiler_params=SPLIT_EFFECT,
    )(*[pltpu.with_memory_space_constraint(a, pltpu.HBM) for a in both])
    return dict(mode=mode, ks=ks, send_sems=res[0], recv_sems=res[1], srcs=res[2:2 + ns], lands=res[2 + ns:2 + ns + nl],
                token=res[-1])


def _send_wait(started, after, name):
    ns, nl = len(started["srcs"]), len(started["lands"])

    def body(*refs):
        send, recv = _exchange_copies(started["mode"], started["ks"], refs[:ns], refs[ns:ns + nl],
                                      refs[ns + nl], refs[ns + nl + 1])
        for cp in send:
            cp.wait_send()
        for cp in recv:
            cp.wait_recv()

    both = list(started["srcs"]) + list(started["lands"])
    res = pl.pallas_call(
        body, name=name,
        out_shape=tuple(pltpu.HBM(a.shape, a.dtype) for a in both),
        in_specs=[HBM] * (ns + nl) + [SEM, SEM, ANY], out_specs=tuple([HBM] * (ns + nl)),
        input_output_aliases={i: i for i in range(ns + nl)}, compiler_params=SPLIT_EFFECT,
    )(*both, started["send_sems"], started["recv_sems"], after)
    return res[:ns], res[ns:]


def _exchange_sibling(gs):
    n = len(gs)

    def body(*refs):
        ins, outs = refs[:n], refs[n:2 * n]
        send_sems, recv_sems = refs[2 * n:]
        x, y, c = lax.axis_index("x"), lax.axis_index("y"), lax.axis_index("c")
        copies = []
        for a in range(n):
            for j in range(4):
                copies.append(pltpu.make_async_remote_copy(
                    src_ref=ins[a].at[2 * j + (1 - c)], dst_ref=outs[a].at[j],
                    send_sem=send_sems.at[a, j], recv_sem=recv_sems.at[a, j],
                    device_id=(x, y, 1 - c), device_id_type=MESH))
        for cp in copies:
            cp.start()
        for cp in copies:
            cp.wait_recv()
        for cp in copies:
            cp.wait_send()

    return pl.pallas_call(
        body, in_specs=[ANY] * n, out_specs=[ANY] * n,
        out_shape=[SDS((4,) + g.shape[1:], g.dtype) for g in gs],
        scratch_shapes=[pltpu.SemaphoreType.DMA((n, 4)), pltpu.SemaphoreType.DMA((n, 4))],
        name="reduce_scatter_sibling")(*gs)


def _add_pair(g, r1, core, name):
    _, rows, cols = g.shape
    tr = _row_tile(rows, cols, 3 * g.dtype.itemsize)

    def body(c_ref, g_ref, r_ref, o_ref):
        o_ref[...] = (g_ref[...].astype(F32) + r_ref[...].astype(F32)).astype(o_ref.dtype)

    return pl.pallas_call(
        body,
        grid_spec=pltpu.PrefetchScalarGridSpec(
            num_scalar_prefetch=1, grid=(4, rows // tr),
            in_specs=[pl.BlockSpec((1, tr, cols), lambda j, i, c_ref: (2 * j + c_ref[0], i, 0)),
                      pl.BlockSpec((1, tr, cols), lambda j, i, c_ref: (j, i, 0))],
            out_specs=pl.BlockSpec((1, tr, cols), lambda j, i, c_ref: (j, i, 0))),
        out_shape=SDS((4, rows, cols), g.dtype),
        compiler_params=_params("parallel", "parallel"), name=name)(core, g, r1)


def _row_tile(rows, cols, itemsize_total):
    budget = (4 << 20) // max(1, cols * itemsize_total)
    if rows <= budget:
        return rows
    t = rows
    while t > budget and t % 2 == 0 and (t // 2) % 16 == 0:
        t //= 2
    return t


def _adam_math(g, w, m, v):
    m_new = ADAM_B1 * m + (1.0 - ADAM_B1) * g
    v_new = ADAM_B2 * v + (1.0 - ADAM_B2) * (g * g)
    m_hat = m_new / (1.0 - ADAM_B1 ** ADAM_STEP)
    v_hat = v_new / (1.0 - ADAM_B2 ** ADAM_STEP)
    delta = -ADAM_LR * (m_hat / (jnp.sqrt(v_hat) + ADAM_EPS) + ADAM_WD * w)
    return delta, m_new, v_new


def _sum_adam(parts, w, m, v, name):
    rows, cols = w.shape
    nparts = parts.shape[0]
    tr = _row_tile(rows, cols, nparts * parts.dtype.itemsize + 7 * 4)

    def body(p_ref, w_ref, m_ref, v_ref, g_ref, d_ref, mo_ref, vo_ref):
        g = p_ref[0].astype(F32)
        for s in range(1, nparts):
            g = g + p_ref[s].astype(F32)
        delta, m_new, v_new = _adam_math(g, w_ref[...], m_ref[...], v_ref[...])
        g_ref[...] = g
        d_ref[...] = delta
        mo_ref[...] = m_new
        vo_ref[...] = v_new

    blk = pl.BlockSpec((tr, cols), lambda i: (i, 0))
    out = SDS((rows, cols), F32)
    return pl.pallas_call(
        body, grid=(rows // tr,),
        in_specs=[pl.BlockSpec((nparts, tr, cols), lambda i: (0, i, 0)), blk, blk, blk],
        out_specs=[blk, blk, blk, blk], out_shape=[out, out, out, out],
        compiler_params=_params("parallel"), name=name)(parts, w, m, v)


SMALL_ROWS = 64


def _small_allreduce_adam(gpart, w, m, v, dep=None):
    def body(g_ref, w_ref, m_ref, v_ref, go_ref, d_ref, mo_ref, vo_ref, gath, send_sems, recv_sems):
        x, y, c = lax.axis_index("x"), lax.axis_index("y"), lax.axis_index("c")
        me = 4 * x + 2 * y + c
        gath[me] = g_ref[...]
        copies = []
        for k in range(1, N_DEV):
            fx, fy, fc = (k >> 2) & 1, (k >> 1) & 1, k & 1
            peer = (x ^ fx, y ^ fy, c ^ fc)
            copies.append(pltpu.make_async_remote_copy(
                src_ref=gath.at[me], dst_ref=gath.at[me], send_sem=send_sems.at[k - 1], recv_sem=recv_sems.at[k - 1],
                device_id=peer, device_id_type=MESH))
        for cp in copies:
            cp.start()
        for cp in copies:
            cp.wait_recv()
        for cp in copies:
            cp.wait_send()
        g = gath[0]
        for d in range(1, N_DEV):
            g = g + gath[d]
        delta, m_new, v_new = _adam_math(g, w_ref[...], m_ref[...], v_ref[...])
        go_ref[...] = g
        d_ref[...] = delta
        mo_ref[...] = m_new
        vo_ref[...] = v_new

    vm = pl.BlockSpec(memory_space=pltpu.VMEM)
    out = SDS((SMALL_ROWS, 128), F32)
    body, dep_spec, dep_arg = _anchored(body, 4, dep)
    return pl.pallas_call(
        body, in_specs=[vm] * 4 + dep_spec, out_specs=[vm] * 4, out_shape=[out] * 4,
        scratch_shapes=[pltpu.VMEM((N_DEV, SMALL_ROWS, 128), F32), pltpu.SemaphoreType.DMA((N_DEV - 1,)),
                        pltpu.SemaphoreType.DMA((N_DEV - 1,))],
        name="small_allreduce_adam")(gpart, w, m, v, *dep_arg)


BIG = ("w_in", "conv_w", "w_conv_out", "w_attn_out", "w_o", "w_ffn_gate", "w_ffn_up", "w_ffn_down")
EARLY = ("w_in", "conv_w")
LATE_MERGE = ("w_conv_out", "w_attn_out", "w_o")
LATE_FFN = ("w_ffn_gate", "w_ffn_up", "w_ffn_down")
COL_SHARDED = ("w_in", "conv_w", "w_attn_out", "w_ffn_gate", "w_ffn_up")
SMALL = ("norm1_g", "gate_b", "conv_b", "conv_ln_g", "conv_ln_b", "norm2_g", "norm_f_g")
WEIGHTS = ("norm1_g", "w_in", "gate_b", "conv_w", "conv_b", "conv_ln_g", "conv_ln_b", "w_conv_out", "w_attn_out", "w_o",
           "norm2_g", "w_ffn_gate", "w_ffn_up", "w_ffn_down", "norm_f_g")


def _shard2d(name, a):
    a = a.reshape(a.shape[-2], a.shape[-1])
    if name == "conv_w":
        a = jnp.pad(a, ((0, CONV_PAD - CONV_K), (0, 0)))
    return a


def _gathered_to_full(name, g):
    if name in COL_SHARDED:
        return g.transpose(1, 0, 2).reshape(g.shape[1], N_DEV * g.shape[2])
    return g.reshape(N_DEV * g.shape[1], g.shape[2])


def _full_to_blocks(name, g):
    if name in COL_SHARDED:
        return g.reshape(g.shape[0], N_DEV, g.shape[1] // N_DEV).transpose(1, 0, 2)
    return g.reshape(N_DEV, g.shape[0] // N_DEV, g.shape[1])


def _pack_small(d):
    return jnp.concatenate([d[n].reshape(-1) for n in SMALL]).reshape(SMALL_ROWS, 128)


def _unpack_small(p, like):
    flat = p.reshape(-1)
    out, off = {}, 0
    for n in SMALL:
        size = like[n].size
        out[n] = flat[off:off + size].reshape(like[n].shape)
        off += size
    return out


def kernel(x, norm1_g, w_in, gate_b, conv_w, conv_b, conv_ln_g, conv_ln_b, w_conv_out, w_attn_out, w_o, norm2_g, w_ffn_gate, w_ffn_up, w_ffn_down, norm_f_g, loss_target, m_norm1_g, m_w_in, m_gate_b, m_conv_w, m_conv_b, m_conv_ln_g, m_conv_ln_b, m_w_conv_out, m_w_attn_out, m_w_o, m_norm2_g, m_w_ffn_gate, m_w_ffn_up, m_w_ffn_down, m_norm_f_g, v_norm1_g, v_w_in, v_gate_b, v_conv_w, v_conv_b, v_conv_ln_g, v_conv_ln_b, v_w_conv_out, v_w_attn_out, v_w_o, v_norm2_g, v_w_ffn_gate, v_w_ffn_up, v_w_ffn_down, v_norm_f_g):
    wts = dict(norm1_g=norm1_g, w_in=w_in, gate_b=gate_b, conv_w=conv_w, conv_b=conv_b, conv_ln_g=conv_ln_g,
               conv_ln_b=conv_ln_b, w_conv_out=w_conv_out, w_attn_out=w_attn_out, w_o=w_o, norm2_g=norm2_g,
               w_ffn_gate=w_ffn_gate, w_ffn_up=w_ffn_up, w_ffn_down=w_ffn_down, norm_f_g=norm_f_g)
    mom1 = dict(norm1_g=m_norm1_g, w_in=m_w_in, gate_b=m_gate_b, conv_w=m_conv_w, conv_b=m_conv_b, conv_ln_g=m_conv_ln_g,
                conv_ln_b=m_conv_ln_b, w_conv_out=m_w_conv_out, w_attn_out=m_w_attn_out, w_o=m_w_o, norm2_g=m_norm2_g,
                w_ffn_gate=m_w_ffn_gate, w_ffn_up=m_w_ffn_up, w_ffn_down=m_w_ffn_down, norm_f_g=m_norm_f_g)
    mom2 = dict(norm1_g=v_norm1_g, w_in=v_w_in, gate_b=v_gate_b, conv_w=v_conv_w, conv_b=v_conv_b, conv_ln_g=v_conv_ln_g,
                conv_ln_b=v_conv_ln_b, w_conv_out=v_w_conv_out, w_attn_out=v_w_attn_out, w_o=v_w_o, norm2_g=v_norm2_g,
                w_ffn_gate=v_w_ffn_gate, w_ffn_up=v_w_ffn_up, w_ffn_down=v_w_ffn_down, norm_f_g=v_norm_f_g)

    T = x.shape[0] * x.shape[1]
    x2 = x.reshape(T, D_MODEL)
    t2 = loss_target.reshape(T, D_MODEL)

    me = 4 * lax.axis_index("x") + 2 * lax.axis_index("y") + lax.axis_index("c")
    shards = {n: _shard2d(n, wts[n]) for n in BIG}
    sent = {n: shards[n] if n == "conv_w" else shards[n].astype(BF16) for n in BIG}
    small = {n: wts[n].reshape(1, -1) for n in SMALL}

    gathered = _all_gather([sent[n] for n in EARLY])
    full = {n: _gathered_to_full(n, g) for n, g in zip(EARLY, gathered)}
    merge_gather = _send_start("gather", ALL_PEERS, "gather_start_merge", [sent[n] for n in LATE_MERGE])
    ffn_gather = _send_start("gather", (1,) + OTHER_CHIPS, "gather_start_ffn", [sent[n] for n in LATE_FFN])
    ffn_state = {}

    def filled(names, srcs, lands):
        return {n: _gathered_to_full(n, lax.dynamic_update_slice(land, src[None], (me, 0, 0)))
                for n, src, land in zip(names, srcs, lands)}

    def late_weights(names, after):
        if names is LATE_MERGE:
            srcs, lands = _send_wait(merge_gather, after, "gather_wait_merge")
            ffn_state["srcs"], ffn_lands = _send_wait(ffn_gather, after, "gather_wait_ffn")
            ffn_state["forward"] = _send_start("forward", OTHER_CHIPS, "forward_start_ffn", lands=ffn_lands)
            return {**filled(names, srcs, lands), "token": ffn_state["forward"]["token"]}
        _, lands = _send_wait(ffn_state["forward"], after, "forward_wait_ffn")
        return filled(names, ffn_state["srcs"], lands)

    scatters = []
    core = lax.axis_index("c").astype(jnp.int32).reshape(1)

    def emit(names, gw):
        blocks = [_full_to_blocks(n, gw[n]) for n in names]
        if "w_in" in names:
            sums = [_add_pair(g, r, core, "chip_sum_" + n) for n, g, r in zip(names, blocks, _exchange_sibling(blocks))]
            started = _send_start("chip_scatter", OTHER_CHIPS, "scatter_start_" + names[0], sums)
        else:
            started = _send_start("scatter", ALL_PEERS, "scatter_start_" + names[0], blocks)
        scatters.append((names, started))
        return started["token"]

    loss_part, grad_x, gw, gsmall = _local_step(x2, t2, full, small, ffn_gather["token"], late_weights, emit)

    grads, deltas, new_m, new_v = {}, {}, {}, {}
    after = grad_x
    for names, started in scatters:
        srcs, lands = _send_wait(started, after, "scatter_wait_" + names[0])
        mine = me >> 1 if started["mode"] == "chip_scatter" else me
        for n, src, land in zip(names, srcs, lands):
            parts = lax.dynamic_update_slice(land, lax.dynamic_slice_in_dim(src, mine, 1, axis=0), (mine, 0, 0))
            g, d, mo, vo = _sum_adam(parts, shards[n], _shard2d(n, mom1[n]), _shard2d(n, mom2[n]), "adam_" + n)
            for dst, val in ((grads, g), (deltas, d), (new_m, mo), (new_v, vo)):
                if n == "conv_w":
                    val = val[:CONV_K]
                dst[n] = val.reshape(wts[n].shape)
            after = g

    sg, sd, sm, sv = _small_allreduce_adam(_pack_small(gsmall), _pack_small(wts), _pack_small(mom1), _pack_small(mom2),
                                           after)
    for dst, val in ((grads, sg), (deltas, sd), (new_m, sm), (new_v, sv)):
        dst.update(_unpack_small(val, wts))

    loss = lax.psum(loss_part[0, 0], ("x", "y", "c"))
    return (loss, grad_x.reshape(x.shape), *[grads[n] for n in WEIGHTS], *[deltas[n] for n in WEIGHTS],
            *[new_m[n] for n in WEIGHTS], *[new_v[n] for n in WEIGHTS])
```

```python
import math

import numpy as np
import jax
import jax.numpy as jnp
from jax import lax
from jax.experimental import pallas as pl
from jax.experimental.pallas import tpu as pltpu

F32 = jnp.float32
BF16 = jnp.bfloat16
SDS = jax.ShapeDtypeStruct
MESH = pl.DeviceIdType.MESH

D_MODEL = 1024
SEQ = 2048
HEAD_DIM = 64
GROUPS = ((128, 1), (512, 4), (2048, 16))
HEADS_PER_GROUP = 8
N_HEADS = 24
ATTN_WIDTH = N_HEADS * HEAD_DIM
ATTN_OUT = HEADS_PER_GROUP * HEAD_DIM
CONV_K = 31
CONV_PAD = 32
D_FF = 2816
IN_WIDTH = 3 * ATTN_WIDTH + 2 * D_MODEL + 2 * D_MODEL
RMS_EPS = 1e-6
LN_EPS = 1e-5
Q_BLOCK = 128
LANES = 128
NEG = -1e30
N_DEV = 8

ADAM_LR = 0.001
ADAM_B1 = 0.9
ADAM_B2 = 0.999
ADAM_EPS = 1e-08
ADAM_WD = 0.01
ADAM_STEP = 10


def _alibi_slope_list(n):
    def pow2(m):
        start = 2.0 ** (-8.0 / m)
        return [start ** (i + 1) for i in range(m)]
    if math.log2(n).is_integer():
        return pow2(n)
    c = 2 ** math.floor(math.log2(n))
    return pow2(c) + _alibi_slope_list(2 * c)[0::2][: n - c]


def _slopes_times_dilation():
    s = np.asarray(sorted(_alibi_slope_list(N_HEADS), reverse=True), dtype=np.float32).reshape(3, HEADS_PER_GROUP)
    r = np.asarray([g[1] for g in GROUPS], dtype=np.float32)[:, None]
    return (s * r).reshape(N_HEADS)


def _sigmoid(x):
    return 1.0 / (1.0 + jnp.exp(-x))


def _dot(a, b):
    return jnp.dot(a, b, preferred_element_type=F32)


def _dot_nt(a, b):
    return lax.dot_general(a, b, (((1,), (1,)), ((), ())), preferred_element_type=F32)


def _dot_tn(a, b):
    return lax.dot_general(a, b, (((0,), (0,)), ((), ())), preferred_element_type=F32)


def _rowsum(x):
    return jnp.sum(x, axis=0, keepdims=True)


def _params(*sem):
    return pltpu.CompilerParams(dimension_semantics=sem)


def _anchored(body, n_in, dep):
    if dep is None:
        return body, [], []

    def wrapped(*refs):
        return body(*refs[:n_in], *refs[n_in + 1:])

    return wrapped, [pl.BlockSpec(memory_space=pl.ANY)], [dep]


IN_TM = 256
IN_CHUNK = 512


def _in_proj(x, g1, w_in, dep=None):
    T = x.shape[0]
    tm = IN_TM
    widths = (3 * ATTN_WIDTH, 2 * D_MODEL, 2 * D_MODEL)

    def body(x_ref, g_ref, w_hbm, h_ref, qkv_ref, u_ref, lg_ref, w_vmem, sem):
        @pl.when(pl.program_id(0) == 0)
        def _():
            cp = pltpu.make_async_copy(w_hbm, w_vmem, sem)
            cp.start()
            cp.wait()

        xv = x_ref[...]
        r = lax.rsqrt(jnp.mean(xv * xv, axis=-1, keepdims=True) + RMS_EPS)
        h = (xv * r * g_ref[...]).astype(BF16)
        h_ref[...] = h
        col = 0
        for o_ref, width in zip((qkv_ref, u_ref, lg_ref), widths):
            for j in range(width // IN_CHUNK):
                o_ref[:, j * IN_CHUNK:(j + 1) * IN_CHUNK] = _dot(h, w_vmem[:, col:col + IN_CHUNK])
                col += IN_CHUNK

    row = lambda n: pl.BlockSpec((tm, n), lambda i: (i, 0))
    body, dep_spec, dep_arg = _anchored(body, 3, dep)
    return pl.pallas_call(
        body, grid=(T // tm,),
        in_specs=[row(D_MODEL), pl.BlockSpec((1, D_MODEL), lambda i: (0, 0)), pl.BlockSpec(memory_space=pl.ANY)] + dep_spec,
        out_specs=[row(D_MODEL)] + [row(n) for n in widths],
        out_shape=[SDS((T, D_MODEL), BF16)] + [SDS((T, n), F32) for n in widths],
        scratch_shapes=[pltpu.VMEM((D_MODEL, IN_WIDTH), BF16), pltpu.SemaphoreType.DMA],
        compiler_params=_params("arbitrary"), name="in_proj")(x, g1, w_in, *dep_arg)


def _mm_tn(a, b, out_dtype, name, tn, tt=1024):
    T, K = a.shape
    N = b.shape[1]
    nt = T // tt

    def body(a_ref, b_ref, o_ref, acc):
        t = pl.program_id(1)

        @pl.when(t == 0)
        def _():
            acc[...] = jnp.zeros_like(acc)

        acc[...] += _dot_tn(a_ref[...], b_ref[...])

        @pl.when(t == nt - 1)
        def _():
            o_ref[...] = acc[...].astype(o_ref.dtype)

    return pl.pallas_call(
        body, grid=(N // tn, nt),
        in_specs=[pl.BlockSpec((tt, K), lambda j, t: (t, 0)),
                  pl.BlockSpec((tt, tn), lambda j, t: (t, j))],
        out_specs=pl.BlockSpec((K, tn), lambda j, t: (0, j)),
        out_shape=SDS((K, N), out_dtype),
        scratch_shapes=[pltpu.VMEM((K, tn), F32)],
        compiler_params=_params("parallel", "arbitrary"), name=name)(a, b)


def _gather_classes(src_ref, dst, r):
    L = SEQ // r
    for c in range(r):
        dst[c * L:(c + 1) * L, :] = src_ref[0, pl.ds(c, L, stride=r), :].astype(dst.dtype)


def _scatter_classes(src, dst, r):
    L = SEQ // r
    for c in range(r):
        dst[pl.ds(c, L, stride=r), :] = src[c * L:(c + 1) * L, :].astype(dst.dtype)


def _attn_masks(slope_r):
    qi = lax.broadcasted_iota(jnp.int32, (Q_BLOCK, Q_BLOCK), 0)
    kj = lax.broadcasted_iota(jnp.int32, (Q_BLOCK, Q_BLOCK), 1)
    rel = (qi - kj).astype(F32)
    bias_cur = jnp.where(qi >= kj, -slope_r * rel, NEG)
    bias_prev = jnp.where(qi <= kj, -slope_r * (rel + float(Q_BLOCK)), NEG)
    return bias_cur, bias_prev


def _store_biases(bias, sl_ref, g, hp):
    for hh in range(2):
        cur, prev = _attn_masks(sl_ref[g * HEADS_PER_GROUP + 2 * hp + hh])
        bias[0, hh * Q_BLOCK:(hh + 1) * Q_BLOCK, :] = cur
        bias[1, hh * Q_BLOCK:(hh + 1) * Q_BLOCK, :] = prev


def _transpose_blocks(src, dst):
    for b in range(SEQ // Q_BLOCK):
        dst[b] = src[b * Q_BLOCK:(b + 1) * Q_BLOCK, :].T


def _stack_heads(t, low):
    z = jnp.zeros_like(t)
    return jnp.concatenate([jnp.where(low, t, z), jnp.where(low, z, t)], axis=0)


def _unstack_heads(t2, low):
    return jnp.where(low, t2[:Q_BLOCK], t2[Q_BLOCK:])


def _unit_offsets(u, nb):
    off = pl.multiple_of(u * Q_BLOCK, Q_BLOCK)
    offp = pl.multiple_of(jnp.maximum(u - 1, 0) * Q_BLOCK, Q_BLOCK)
    n = u & (nb - 1)
    c = u >> int(math.log2(nb))
    return off, offp, n == 0, c, n


def _attn_fwd(qkv, slopes_r, batch):
    nblk = SEQ // Q_BLOCK

    def body(sl_ref, *refs):
        qkv_refs = refs[:9]
        att_ref, lse_ref = refs[9:11]
        qd, kd, vd, kt, opos, lpos, bias = refs[11:]
        hp = pl.program_id(1)
        low = lax.broadcasted_iota(jnp.int32, (Q_BLOCK, LANES), 1) < HEAD_DIM

        for g in range(3):
            r = GROUPS[g][1]
            nb = SEQ // r // Q_BLOCK
            _gather_classes(qkv_refs[3 * g], qd, r)
            _gather_classes(qkv_refs[3 * g + 1], kd, r)
            _gather_classes(qkv_refs[3 * g + 2], vd, r)
            _transpose_blocks(kd, kt)
            _store_biases(bias, sl_ref, g, hp)

            def unit(u, carry, g=g, r=r, nb=nb):
                off, offp, first, c, n = _unit_offsets(u, nb)
                q2 = _stack_heads(qd[pl.ds(off, Q_BLOCK), :], low)
                vc = vd[pl.ds(off, Q_BLOCK), :]
                vp = vd[pl.ds(offp, Q_BLOCK), :]
                sc = _dot(q2, kt[u]) * 0.125 + bias[0]
                sp = jnp.where(first, NEG, _dot(q2, kt[jnp.maximum(u - 1, 0)]) * 0.125 + bias[1])
                m = jnp.max(jnp.maximum(sc, sp), axis=-1, keepdims=True)
                pc = jnp.exp(sc - m)
                pp = jnp.exp(sp - m)
                l = jnp.sum(pc + pp, axis=-1, keepdims=True)
                o2 = (_dot(pc.astype(BF16), vc) + _dot(pp.astype(BF16), vp)) * (1.0 / l)
                lse2 = m + jnp.log(l)
                rows = pl.ds(c + n * (Q_BLOCK * r), Q_BLOCK, stride=r)
                opos[g, rows, :] = _unstack_heads(o2, low)
                lpos[g, rows, :] = jnp.where(low, lse2[:Q_BLOCK], lse2[Q_BLOCK:])
                return carry

            lax.fori_loop(0, nblk, unit, 0, unroll=2)

        def merge(i, carry):
            rows = pl.ds(pl.multiple_of(i * 256, 256), 256)
            l0, l1, l2 = lpos[0, rows, :], lpos[1, rows, :], lpos[2, rows, :]
            m = jnp.maximum(jnp.maximum(l0, l1), l2)
            e0, e1, e2 = jnp.exp(l0 - m), jnp.exp(l1 - m), jnp.exp(l2 - m)
            den = e0 + e1 + e2
            att = (e0 * opos[0, rows, :] + e1 * opos[1, rows, :] + e2 * opos[2, rows, :]) / den
            att_ref[0, rows, :] = att.astype(att_ref.dtype)
            lse_ref[0, rows, :] = m + jnp.log(den)
            return carry

        lax.fori_loop(0, SEQ // 256, merge, 0)

    def col(sec, g):
        return pl.BlockSpec((1, SEQ, LANES), lambda b, hp: (b, 0, sec * 12 + g * 4 + hp))

    out = pl.BlockSpec((1, SEQ, LANES), lambda b, hp: (b, 0, hp))
    return pl.pallas_call(
        body, grid=(batch, 4),
        in_specs=[pl.BlockSpec(memory_space=pltpu.SMEM)] + [col(sec, g) for g in range(3) for sec in range(3)],
        out_specs=[out, out],
        out_shape=[SDS((batch, SEQ, ATTN_OUT), BF16), SDS((batch, SEQ, ATTN_OUT), F32)],
        scratch_shapes=[pltpu.VMEM((SEQ, LANES), BF16), pltpu.VMEM((SEQ, LANES), BF16), pltpu.VMEM((SEQ, LANES), BF16),
                        pltpu.VMEM((nblk, LANES, Q_BLOCK), BF16),
                        pltpu.VMEM((3, SEQ, LANES), F32), pltpu.VMEM((3, SEQ, LANES), F32),
                        pltpu.VMEM((2, 2 * Q_BLOCK, Q_BLOCK), F32)],
        compiler_params=_params("parallel", "parallel"), name="attn_fwd")(slopes_r, *([qkv] * 9))


def _attn_bwd(qkv, datt, lse, dsum, slopes_r, batch):
    nblk = SEQ // Q_BLOCK

    def body(sl_ref, q_ref, k_ref, v_ref, do_ref, l_ref, d_ref, dq_ref, dk_ref, dv_ref,
             qd, kd, vd, dod, kt, vt, ld, dd, dq_acc, dk_acc, dv_acc, stage, bias):
        gid, hp = pl.program_id(1), pl.program_id(2)
        low = lax.broadcasted_iota(jnp.int32, (Q_BLOCK, LANES), 1) < HEAD_DIM

        def section(g):
            r = GROUPS[g][1]
            nb = SEQ // r // Q_BLOCK
            _gather_classes(q_ref, qd, r)
            _gather_classes(k_ref, kd, r)
            _gather_classes(v_ref, vd, r)
            _gather_classes(do_ref, dod, r)
            _gather_classes(l_ref, ld, r)
            _gather_classes(d_ref, dd, r)
            _transpose_blocks(kd, kt)
            _transpose_blocks(vd, vt)
            _store_biases(bias, sl_ref, g, hp)
            dk_acc[...] = jnp.zeros_like(dk_acc)
            dv_acc[...] = jnp.zeros_like(dv_acc)

            def unit(u, carry):
                off, offp, first, _, _ = _unit_offsets(u, nb)
                up = jnp.maximum(u - 1, 0)
                q2 = _stack_heads(qd[pl.ds(off, Q_BLOCK), :], low)
                do2 = _stack_heads(dod[pl.ds(off, Q_BLOCK), :], low)
                kc = kd[pl.ds(off, Q_BLOCK), :]
                kp = kd[pl.ds(offp, Q_BLOCK), :]
                lse_t = ld[pl.ds(off, Q_BLOCK), :]
                dsum_t = dd[pl.ds(off, Q_BLOCK), :]
                lse2 = jnp.concatenate([lse_t[:, 0:1], lse_t[:, HEAD_DIM:HEAD_DIM + 1]], axis=0)
                dsum2 = jnp.concatenate([dsum_t[:, 0:1], dsum_t[:, HEAD_DIM:HEAD_DIM + 1]], axis=0)
                sc = _dot(q2, kt[u]) * 0.125 + bias[0]
                sp = jnp.where(first, NEG, _dot(q2, kt[up]) * 0.125 + bias[1])
                pc = jnp.exp(sc - lse2)
                pp = jnp.exp(sp - lse2)
                dsc = (pc * (_dot(do2, vt[u]) - dsum2)).astype(BF16)
                dsp = (pp * (_dot(do2, vt[up]) - dsum2)).astype(BF16)
                dq2 = _dot(dsc, kc) + _dot(dsp, kp)
                dq_acc[pl.ds(off, Q_BLOCK), :] = _unstack_heads(dq2, low) * 0.125
                dk_acc[pl.ds(off, Q_BLOCK), :] += _dot_tn(dsc, q2) * 0.125
                dk_acc[pl.ds(offp, Q_BLOCK), :] += _dot_tn(dsp, q2) * 0.125
                dv_acc[pl.ds(off, Q_BLOCK), :] += _dot_tn(pc.astype(BF16), do2)
                dv_acc[pl.ds(offp, Q_BLOCK), :] += _dot_tn(pp.astype(BF16), do2)
                return carry

            lax.fori_loop(0, nblk, unit, 0, unroll=2)
            for acc, out_ref in ((dq_acc, dq_ref), (dk_acc, dk_ref), (dv_acc, dv_ref)):
                _scatter_classes(acc, stage, r)
                out_ref[0] = stage[...].astype(out_ref.dtype)

        for g in range(3):
            pl.when(gid == g)(lambda g=g: section(g))

    def col(sec):
        return pl.BlockSpec((1, SEQ, LANES), lambda b, g, hp: (b, 0, sec * 12 + g * 4 + hp))

    pos = pl.BlockSpec((1, SEQ, LANES), lambda b, g, hp: (b, 0, hp))
    dout = pl.BlockSpec((1, SEQ, LANES), lambda b, g, hp: (b, 0, g * 4 + hp))
    out = SDS((batch, SEQ, ATTN_WIDTH), BF16)
    seq_bf = pltpu.VMEM((SEQ, LANES), BF16)
    seq_f = pltpu.VMEM((SEQ, LANES), F32)
    blk_t = pltpu.VMEM((nblk, LANES, Q_BLOCK), BF16)
    return pl.pallas_call(
        body, grid=(batch, 3, 4),
        in_specs=[pl.BlockSpec(memory_space=pltpu.SMEM), col(0), col(1), col(2), pos, pos, pos],
        out_specs=[dout, dout, dout],
        out_shape=[out, out, out],
        scratch_shapes=[seq_bf, seq_bf, seq_bf, seq_bf, blk_t, blk_t, seq_f, seq_f, seq_f, seq_f, seq_f, seq_f,
                        pltpu.VMEM((2, 2 * Q_BLOCK, Q_BLOCK), F32)],
        compiler_params=_params("parallel", "parallel", "parallel"), name="attn_bwd")(
            slopes_r, qkv, qkv, qkv, datt, lse, dsum)


CONV_TC = 128
CONV_ROWS = 128
SUBLANES = 8


def _fill_shifted(sh):
    n = SEQ + CONV_PAD - SUBLANES
    for s in range(1, SUBLANES):
        sh[s, 0:n, :] = sh[0, s:s + n, :]


def _tap(sh, base, offset):
    s = offset % SUBLANES
    return sh[s, pl.ds(pl.multiple_of(base + (offset - s), SUBLANES), CONV_ROWS), :]


def _conv_fwd(u, conv_w, conv_b, batch):
    nct = D_MODEL // CONV_TC

    def body(ua_ref, ub_ref, w_ref, b_ref, o_ref, sh):
        sh[0, 0:CONV_PAD, :] = jnp.zeros((CONV_PAD, CONV_TC), F32)
        sh[0, CONV_PAD:, :] = ua_ref[0] * _sigmoid(ub_ref[0])
        _fill_shifted(sh)

        def chunk(c, carry):
            base = pl.multiple_of(c * CONV_ROWS, CONV_ROWS)
            acc = jnp.broadcast_to(b_ref[...], (CONV_ROWS, CONV_TC))
            for t in range(CONV_K):
                acc = acc + _tap(sh, base, t + CONV_PAD - (CONV_K - 1)) * w_ref[t:t + 1, :]
            o_ref[0, pl.ds(base, CONV_ROWS), :] = acc
            return carry

        lax.fori_loop(0, SEQ // CONV_ROWS, chunk, 0)

    return pl.pallas_call(
        body, grid=(nct, batch),
        in_specs=[pl.BlockSpec((1, SEQ, CONV_TC), lambda j, b: (b, 0, j)),
                  pl.BlockSpec((1, SEQ, CONV_TC), lambda j, b: (b, 0, j + nct)),
                  pl.BlockSpec((CONV_PAD, CONV_TC), lambda j, b: (0, j)),
                  pl.BlockSpec((1, CONV_TC), lambda j, b: (0, j))],
        out_specs=pl.BlockSpec((1, SEQ, CONV_TC), lambda j, b: (b, 0, j)),
        out_shape=SDS((batch, SEQ, D_MODEL), F32),
        scratch_shapes=[pltpu.VMEM((SUBLANES, SEQ + CONV_PAD, CONV_TC), F32)],
        compiler_params=_params("parallel", "parallel"), name="conv_fwd")(u, u, conv_w, conv_b)


def _conv_bwd(u, dc1, conv_w, batch, dep=None):
    nct = D_MODEL // CONV_TC
    nchunk = SEQ // CONV_ROWS

    def body(ua_ref, ub_ref, d_ref, w_ref, dua_ref, dub_ref, gw_ref, gb_ref, shc, shd, gacc):
        b = pl.program_id(1)
        shc[0, 0:CONV_PAD, :] = jnp.zeros((CONV_PAD, CONV_TC), F32)
        shc[0, CONV_PAD:, :] = ua_ref[0] * _sigmoid(ub_ref[0])
        _fill_shifted(shc)
        shd[0, 0:SEQ, :] = d_ref[0]
        shd[0, SEQ:, :] = jnp.zeros((CONV_PAD, CONV_TC), F32)
        _fill_shifted(shd)

        @pl.when(b == 0)
        def _():
            gacc[...] = jnp.zeros_like(gacc)
            gb_ref[...] = jnp.zeros_like(gb_ref)

        gb_ref[...] += _rowsum(d_ref[0])

        def chunk(c, carry):
            base = pl.multiple_of(c * CONV_ROWS, CONV_ROWS)
            dcur = shd[0, pl.ds(base, CONV_ROWS), :]
            acc = jnp.zeros((CONV_ROWS, CONV_TC), F32)
            for t in range(CONV_K):
                acc = acc + _tap(shd, base, CONV_K - 1 - t) * w_ref[t:t + 1, :]
                prod = _tap(shc, base, t + CONV_PAD - (CONV_K - 1)) * dcur
                gacc[t] += jnp.sum(prod.reshape(CONV_ROWS // 8, 8, CONV_TC), axis=0)
            ua = ua_ref[0, pl.ds(base, CONV_ROWS), :]
            sg = _sigmoid(ub_ref[0, pl.ds(base, CONV_ROWS), :])
            dua_ref[0, pl.ds(base, CONV_ROWS), :] = (acc * sg).astype(dua_ref.dtype)
            dub_ref[0, pl.ds(base, CONV_ROWS), :] = (acc * ua * sg * (1.0 - sg)).astype(dub_ref.dtype)
            return carry

        lax.fori_loop(0, nchunk, chunk, 0)

        @pl.when(b == batch - 1)
        def _():
            for t in range(CONV_K):
                gw_ref[t:t + 1, :] = jnp.sum(gacc[t], axis=0, keepdims=True)
            gw_ref[CONV_K:CONV_PAD, :] = jnp.zeros((CONV_PAD - CONV_K, CONV_TC), F32)

    du = SDS((batch, SEQ, D_MODEL), BF16)
    body, dep_spec, dep_arg = _anchored(body, 4, dep)
    return pl.pallas_call(
        body, grid=(nct, batch),
        in_specs=[pl.BlockSpec((1, SEQ, CONV_TC), lambda j, b: (b, 0, j)),
                  pl.BlockSpec((1, SEQ, CONV_TC), lambda j, b: (b, 0, j + nct)),
                  pl.BlockSpec((1, SEQ, CONV_TC), lambda j, b: (b, 0, j)),
                  pl.BlockSpec((CONV_PAD, CONV_TC), lambda j, b: (0, j))] + dep_spec,
        out_specs=[pl.BlockSpec((1, SEQ, CONV_TC), lambda j, b: (b, 0, j)),
                   pl.BlockSpec((1, SEQ, CONV_TC), lambda j, b: (b, 0, j)),
                   pl.BlockSpec((CONV_PAD, CONV_TC), lambda j, b: (0, j)),
                   pl.BlockSpec((1, CONV_TC), lambda j, b: (0, j))],
        out_shape=[du, du, SDS((CONV_PAD, D_MODEL), F32), SDS((1, D_MODEL), F32)],
        scratch_shapes=[pltpu.VMEM((SUBLANES, SEQ + CONV_PAD, CONV_TC), F32),
                        pltpu.VMEM((SUBLANES, SEQ + CONV_PAD, CONV_TC), F32),
                        pltpu.VMEM((CONV_K, 8, CONV_TC), F32)],
        compiler_params=_params("parallel", "arbitrary"), name="conv_bwd")(u, u, dc1, conv_w, *dep_arg)


MID_TM = 256


def _layernorm_stats(c1):
    mu = jnp.mean(c1, axis=-1, keepdims=True)
    cen = c1 - mu
    rs = lax.rsqrt(jnp.mean(cen * cen, axis=-1, keepdims=True) + LN_EPS)
    return cen * rs, rs


def _mid_fwd(att, c1, logits, x, w_a, w_c, w_o, gate_b, ln_g, ln_b, g2, dep=None):
    T = x.shape[0]
    tm = MID_TM

    def body(att_ref, c1_ref, lg_ref, x_ref, wa_ref, wc_ref, wo_ref, gb_ref, lng_ref, lnb_ref, g2_ref,
             c3_ref, ya_ref, yc_ref, mix_ref, x1_ref, h2_ref):
        ya = _dot(att_ref[...], wa_ref[...])
        xh, _ = _layernorm_stats(c1_ref[...])
        c2 = xh * lng_ref[...] + lnb_ref[...]
        c3 = (c2 * _sigmoid(c2)).astype(BF16)
        c3_ref[...] = c3
        yc = _dot(c3, wc_ref[...])
        gates = _sigmoid(lg_ref[...] + gb_ref[...])
        mix = (gates[:, :D_MODEL] * ya + gates[:, D_MODEL:] * yc).astype(BF16)
        ya_ref[...] = ya.astype(BF16)
        yc_ref[...] = yc.astype(BF16)
        mix_ref[...] = mix
        x1 = x_ref[...] + _dot(mix, wo_ref[...])
        x1_ref[...] = x1
        r = lax.rsqrt(jnp.mean(x1 * x1, axis=-1, keepdims=True) + RMS_EPS)
        h2_ref[...] = (x1 * r * g2_ref[...]).astype(BF16)

    row = lambda n: pl.BlockSpec((tm, n), lambda i: (i, 0))
    full = lambda a, b: pl.BlockSpec((a, b), lambda i: (0, 0))
    body, dep_spec, dep_arg = _anchored(body, 11, dep)
    return pl.pallas_call(
        body, grid=(T // tm,),
        in_specs=[row(ATTN_OUT), row(D_MODEL), row(2 * D_MODEL), row(D_MODEL),
                  full(ATTN_OUT, D_MODEL), full(D_MODEL, D_MODEL), full(D_MODEL, D_MODEL),
                  full(1, 2 * D_MODEL), full(1, D_MODEL), full(1, D_MODEL), full(1, D_MODEL)] + dep_spec,
        out_specs=[row(D_MODEL), row(D_MODEL), row(D_MODEL), row(D_MODEL), row(D_MODEL), row(D_MODEL)],
        out_shape=[SDS((T, D_MODEL), BF16), SDS((T, D_MODEL), BF16), SDS((T, D_MODEL), BF16), SDS((T, D_MODEL), BF16),
                   SDS((T, D_MODEL), F32), SDS((T, D_MODEL), BF16)],
        compiler_params=_params("parallel"), name="mid_fwd")(att, c1, logits, x, w_a, w_c, w_o, gate_b, ln_g, ln_b, g2,
                                                             *dep_arg)


def _mid_bwd(dx1b, ya, yc, logits, att, c1, w_a, w_c, w_o, gate_b, ln_g, ln_b, head_ones, dep=None):
    T = dx1b.shape[0]
    tm = MID_TM

    def body(dx_ref, ya_ref, yc_ref, lg_ref, att_ref, c1_ref, wa_ref, wc_ref, wo_ref, gb_ref, lng_ref, lnb_ref, e_ref,
             dlg_ref, dya_ref, dyc_ref, datt_ref, dsum_ref, dc1_ref, ggb_ref, glg_ref, glb_ref):
        @pl.when(pl.program_id(0) == 0)
        def _():
            ggb_ref[...] = jnp.zeros_like(ggb_ref)
            glg_ref[...] = jnp.zeros_like(glg_ref)
            glb_ref[...] = jnp.zeros_like(glb_ref)

        dmix = _dot_nt(dx_ref[...], wo_ref[...])
        gates = _sigmoid(lg_ref[...] + gb_ref[...])
        ga, gc = gates[:, :D_MODEL], gates[:, D_MODEL:]
        dla = dmix * ya_ref[...].astype(F32) * ga * (1.0 - ga)
        dlc = dmix * yc_ref[...].astype(F32) * gc * (1.0 - gc)
        dlg_ref[:, :D_MODEL] = dla.astype(BF16)
        dlg_ref[:, D_MODEL:] = dlc.astype(BF16)
        ggb_ref[:, :D_MODEL] += _rowsum(dla)
        ggb_ref[:, D_MODEL:] += _rowsum(dlc)
        dya = (dmix * ga).astype(BF16)
        dyc = (dmix * gc).astype(BF16)
        dya_ref[...] = dya
        dyc_ref[...] = dyc
        datt = _dot_nt(dya, wa_ref[...])
        datt_ref[...] = datt
        dsum_ref[...] = jnp.dot(datt * att_ref[...].astype(F32), e_ref[...], preferred_element_type=F32,
                                precision=lax.Precision.HIGHEST)
        dc3 = _dot_nt(dyc, wc_ref[...])
        xh, rs = _layernorm_stats(c1_ref[...])
        c2 = xh * lng_ref[...] + lnb_ref[...]
        sg = _sigmoid(c2)
        dc2 = dc3 * (sg * (1.0 + c2 * (1.0 - sg)))
        glg_ref[...] += _rowsum(dc2 * xh)
        glb_ref[...] += _rowsum(dc2)
        dxh = dc2 * lng_ref[...]
        dc1_ref[...] = rs * (dxh - jnp.mean(dxh, axis=-1, keepdims=True) - xh * jnp.mean(dxh * xh, axis=-1, keepdims=True))

    row = lambda n: pl.BlockSpec((tm, n), lambda i: (i, 0))
    full = lambda a, b: pl.BlockSpec((a, b), lambda i: (0, 0))
    body, dep_spec, dep_arg = _anchored(body, 13, dep)
    return pl.pallas_call(
        body, grid=(T // tm,),
        in_specs=[row(D_MODEL), row(D_MODEL), row(D_MODEL), row(2 * D_MODEL), row(ATTN_OUT), row(D_MODEL),
                  full(ATTN_OUT, D_MODEL), full(D_MODEL, D_MODEL), full(D_MODEL, D_MODEL),
                  full(1, 2 * D_MODEL), full(1, D_MODEL), full(1, D_MODEL), full(ATTN_OUT, ATTN_OUT)] + dep_spec,
        out_specs=[row(2 * D_MODEL), row(D_MODEL), row(D_MODEL), row(ATTN_OUT), row(ATTN_OUT), row(D_MODEL),
                   full(1, 2 * D_MODEL), full(1, D_MODEL), full(1, D_MODEL)],
        out_shape=[SDS((T, 2 * D_MODEL), BF16), SDS((T, D_MODEL), BF16), SDS((T, D_MODEL), BF16), SDS((T, ATTN_OUT), F32),
                   SDS((T, ATTN_OUT), F32), SDS((T, D_MODEL), F32),
                   SDS((1, 2 * D_MODEL), F32), SDS((1, D_MODEL), F32), SDS((1, D_MODEL), F32)],
        compiler_params=_params("arbitrary"), name="mid_bwd")(dx1b, ya, yc, logits, att, c1, w_a, w_c, w_o, gate_b, ln_g, ln_b,
                                                               head_ones, *dep_arg)


FFN_TM = 512
FFN_TF = D_FF // 2


def _rms_bwd(dy_times_g, xh, r):
    return r * (dy_times_g - xh * jnp.mean(dy_times_g * xh, axis=-1, keepdims=True))


def _ffn_fwd(h2, x1, target, gf, w_g, w_u, w_d):
    T = h2.shape[0]
    tm, tf = FFN_TM, FFN_TF
    nf = D_FF // tf

    def body(h_ref, x1_ref, t_ref, gf_ref, wg_ref, wu_ref, wd_ref,
             a_ref, b_ref, f_ref, dx2_ref, dx2b_ref, loss_ref, gnf_ref, acc):
        i, j = pl.program_id(0), pl.program_id(1)
        h = h_ref[...]
        a = _dot(h, wg_ref[...])
        b = _dot(h, wu_ref[...])
        f = (a * _sigmoid(a) * b).astype(BF16)
        a_ref[...] = a.astype(BF16)
        b_ref[...] = b.astype(BF16)
        f_ref[...] = f
        p = _dot(f, wd_ref[...])

        @pl.when(j == 0)
        def _():
            acc[...] = x1_ref[...] + p

        @pl.when(j > 0)
        def _():
            acc[...] += p

        @pl.when((i == 0) & (j == nf - 1))
        def _():
            loss_ref[...] = jnp.zeros_like(loss_ref)
            gnf_ref[...] = jnp.zeros_like(gnf_ref)

        @pl.when(j == nf - 1)
        def _():
            x2 = acc[...]
            r = lax.rsqrt(jnp.mean(x2 * x2, axis=-1, keepdims=True) + RMS_EPS)
            xh = x2 * r
            err = xh * gf_ref[...] - t_ref[...]
            loss_ref[...] += (0.5 / D_MODEL) * jnp.sum(err * err)
            dy = err * (1.0 / D_MODEL)
            gnf_ref[...] += _rowsum(dy * xh)
            dx2 = _rms_bwd(dy * gf_ref[...], xh, r)
            dx2_ref[...] = dx2
            dx2b_ref[...] = dx2.astype(BF16)

    row = lambda n: pl.BlockSpec((tm, n), lambda i, j: (i, 0))
    ffb = pl.BlockSpec((tm, tf), lambda i, j: (i, j))
    return pl.pallas_call(
        body, grid=(T // tm, nf),
        in_specs=[row(D_MODEL), row(D_MODEL), row(D_MODEL), pl.BlockSpec((1, D_MODEL), lambda i, j: (0, 0)),
                  pl.BlockSpec((D_MODEL, tf), lambda i, j: (0, j)), pl.BlockSpec((D_MODEL, tf), lambda i, j: (0, j)),
                  pl.BlockSpec((tf, D_MODEL), lambda i, j: (j, 0))],
        out_specs=[ffb, ffb, ffb, row(D_MODEL), row(D_MODEL),
                   pl.BlockSpec((1, 128), lambda i, j: (0, 0)), pl.BlockSpec((1, D_MODEL), lambda i, j: (0, 0))],
        out_shape=[SDS((T, D_FF), BF16), SDS((T, D_FF), BF16), SDS((T, D_FF), BF16), SDS((T, D_MODEL), F32),
                   SDS((T, D_MODEL), BF16), SDS((1, 128), F32), SDS((1, D_MODEL), F32)],
        scratch_shapes=[pltpu.VMEM((tm, D_MODEL), F32)],
        compiler_params=_params("arbitrary", "arbitrary"), name="ffn_fwd")(h2, x1, target, gf, w_g, w_u, w_d)


def _ffn_bwd(dx2b, dx2, a, b, x1, g2, w_g, w_u, w_d):
    T = dx2.shape[0]
    tm, tf = FFN_TM, FFN_TF
    nf = D_FF // tf

    def body(dxb_ref, dx2_ref, a_ref, b_ref, x1_ref, g2_ref, wg_ref, wu_ref, wd_ref,
             da_ref, db_ref, dx1_ref, dx1b_ref, gn2_ref, acc):
        i, j = pl.program_id(0), pl.program_id(1)
        df = _dot_nt(dxb_ref[...], wd_ref[...])
        av = a_ref[...].astype(F32)
        bv = b_ref[...].astype(F32)
        sg = _sigmoid(av)
        db = (df * av * sg).astype(BF16)
        da = (df * bv * (sg * (1.0 + av * (1.0 - sg)))).astype(BF16)
        da_ref[...] = da
        db_ref[...] = db
        p = _dot_nt(da, wg_ref[...]) + _dot_nt(db, wu_ref[...])

        @pl.when(j == 0)
        def _():
            acc[...] = p

        @pl.when(j > 0)
        def _():
            acc[...] += p

        @pl.when((i == 0) & (j == nf - 1))
        def _():
            gn2_ref[...] = jnp.zeros_like(gn2_ref)

        @pl.when(j == nf - 1)
        def _():
            dh2 = acc[...]
            x1 = x1_ref[...]
            r = lax.rsqrt(jnp.mean(x1 * x1, axis=-1, keepdims=True) + RMS_EPS)
            xh = x1 * r
            gn2_ref[...] += _rowsum(dh2 * xh)
            dx1 = dx2_ref[...] + _rms_bwd(dh2 * g2_ref[...], xh, r)
            dx1_ref[...] = dx1
            dx1b_ref[...] = dx1.astype(BF16)

    row = lambda n: pl.BlockSpec((tm, n), lambda i, j: (i, 0))
    ffb = pl.BlockSpec((tm, tf), lambda i, j: (i, j))
    return pl.pallas_call(
        body, grid=(T // tm, nf),
        in_specs=[row(D_MODEL), row(D_MODEL), ffb, ffb, row(D_MODEL), pl.BlockSpec((1, D_MODEL), lambda i, j: (0, 0)),
                  pl.BlockSpec((D_MODEL, tf), lambda i, j: (0, j)), pl.BlockSpec((D_MODEL, tf), lambda i, j: (0, j)),
                  pl.BlockSpec((tf, D_MODEL), lambda i, j: (j, 0))],
        out_specs=[ffb, ffb, row(D_MODEL), row(D_MODEL), pl.BlockSpec((1, D_MODEL), lambda i, j: (0, 0))],
        out_shape=[SDS((T, D_FF), BF16), SDS((T, D_FF), BF16), SDS((T, D_MODEL), F32), SDS((T, D_MODEL), BF16),
                   SDS((1, D_MODEL), F32)],
        scratch_shapes=[pltpu.VMEM((tm, D_MODEL), F32)],
        compiler_params=_params("arbitrary", "arbitrary"), name="ffn_bwd")(dx2b, dx2, a, b, x1, g2, w_g, w_u, w_d)


def _in_bwd(pieces, w_in, x, dx1, g1, dep=None):
    T = x.shape[0]
    tm = IN_TM
    npc = len(pieces)
    assert sum(p.shape[1] for p in pieces) == IN_WIDTH

    def body(*refs):
        p_refs = refs[:npc]
        w_hbm, x_ref, dx1_ref, g_ref, dx_ref, gn1_ref, w_vmem, sem = refs[npc:]

        @pl.when(pl.program_id(0) == 0)
        def _():
            cp = pltpu.make_async_copy(w_hbm, w_vmem, sem)
            cp.start()
            cp.wait()
            gn1_ref[...] = jnp.zeros_like(gn1_ref)

        dh = jnp.zeros((tm, D_MODEL), F32)
        col = 0
        for p_ref in p_refs:
            for j in range(p_ref.shape[1] // IN_CHUNK):
                dh = dh + _dot_nt(p_ref[:, j * IN_CHUNK:(j + 1) * IN_CHUNK], w_vmem[:, col:col + IN_CHUNK])
                col += IN_CHUNK
        xv = x_ref[...]
        r = lax.rsqrt(jnp.mean(xv * xv, axis=-1, keepdims=True) + RMS_EPS)
        xh = xv * r
        gn1_ref[...] += _rowsum(dh * xh)
        dx_ref[...] = dx1_ref[...] + _rms_bwd(dh * g_ref[...], xh, r)

    row = lambda n: pl.BlockSpec((tm, n), lambda i: (i, 0))
    body, dep_spec, dep_arg = _anchored(body, npc + 4, dep)
    return pl.pallas_call(
        body, grid=(T // tm,),
        in_specs=[row(p.shape[1]) for p in pieces]
        + [pl.BlockSpec(memory_space=pl.ANY), row(D_MODEL), row(D_MODEL), pl.BlockSpec((1, D_MODEL), lambda i: (0, 0))]
        + dep_spec,
        out_specs=[row(D_MODEL), pl.BlockSpec((1, D_MODEL), lambda i: (0, 0))],
        out_shape=[SDS((T, D_MODEL), F32), SDS((1, D_MODEL), F32)],
        scratch_shapes=[pltpu.VMEM((D_MODEL, IN_WIDTH), BF16), pltpu.SemaphoreType.DMA],
        compiler_params=_params("arbitrary"), name="in_bwd")(*pieces, w_in, x, dx1, g1, *dep_arg)


def _local_step(x, target, w, small, dep=None, late_weights=None, emit=None):
    T = x.shape[0]
    batch = T // SEQ
    slopes_r = jnp.asarray(_slopes_times_dilation())
    emit = emit or (lambda names, grads: None)

    h, qkv, u, logits = _in_proj(x, small["norm1_g"], w["w_in"], dep)

    qkv3 = qkv.reshape(batch, SEQ, 3 * ATTN_WIDTH)
    att, lse = _attn_fwd(qkv3, slopes_r, batch)
    att = att.reshape(T, ATTN_OUT)

    u3 = u.reshape(batch, SEQ, 2 * D_MODEL)
    c1 = _conv_fwd(u3, w["conv_w"], small["conv_b"], batch).reshape(T, D_MODEL)
    if late_weights is not None:
        w = {**w, **late_weights(LATE_MERGE, c1)}

    c3, ya, yc, mix, x1, h2 = _mid_fwd(
        att, c1, logits, x, w["w_attn_out"], w["w_conv_out"], w["w_o"],
        small["gate_b"], small["conv_ln_g"], small["conv_ln_b"], small["norm2_g"], w.get("token"))
    if late_weights is not None:
        w = {**w, **late_weights(LATE_FFN, h2)}

    a, b, f, dx2, dx2b, loss, g_normf = _ffn_fwd(h2, x1, target, small["norm_f_g"],
                                                   w["w_ffn_gate"], w["w_ffn_up"], w["w_ffn_down"])

    da, db, dx1, dx1b, g_norm2 = _ffn_bwd(dx2b, dx2, a, b, x1, small["norm2_g"],
                                           w["w_ffn_gate"], w["w_ffn_up"], w["w_ffn_down"])
    gw = {}
    gw["w_ffn_down"] = _mm_tn(f, dx2b, BF16, "gw_ffn_down", tn=512)
    gw["w_ffn_gate"] = _mm_tn(h2, da, BF16, "gw_ffn_gate", tn=1408)
    gw["w_ffn_up"] = _mm_tn(h2, db, BF16, "gw_ffn_up", tn=1408)
    token = emit(("w_ffn_gate", "w_ffn_up", "w_ffn_down"), gw)

    head_ones = jnp.asarray(np.kron(np.eye(HEADS_PER_GROUP, dtype=np.float32), np.ones((HEAD_DIM, HEAD_DIM), np.float32)))
    dlogits, dya, dyc, datt, dsum, dc1, g_gate_b, g_ln_g, g_ln_b = _mid_bwd(
        dx1b, ya, yc, logits, att, c1, w["w_attn_out"], w["w_conv_out"], w["w_o"],
        small["gate_b"], small["conv_ln_g"], small["conv_ln_b"], head_ones, token)
    gw["w_o"] = _mm_tn(mix, dx1b, BF16, "gw_o", tn=512)
    gw["w_attn_out"] = _mm_tn(att, dya, BF16, "gw_attn_out", tn=512)
    gw["w_conv_out"] = _mm_tn(c3, dyc, BF16, "gw_conv_out", tn=512)
    token = emit(("w_conv_out", "w_attn_out", "w_o"), gw)

    dua, dub, g_conv_w, g_conv_b = _conv_bwd(u3, dc1.reshape(batch, SEQ, D_MODEL), w["conv_w"], batch, token)

    dq, dk, dv = _attn_bwd(qkv3, datt.reshape(batch, SEQ, ATTN_OUT), lse, dsum.reshape(batch, SEQ, ATTN_OUT),
                           slopes_r, batch)
    pieces = [dq.reshape(T, ATTN_WIDTH), dk.reshape(T, ATTN_WIDTH), dv.reshape(T, ATTN_WIDTH),
              dua.reshape(T, D_MODEL), dub.reshape(T, D_MODEL), dlogits]

    names = ("q", "k", "v", "ua", "ub", "gate")
    gw["w_in"] = jnp.concatenate(
        [_mm_tn(h, p, BF16, "gw_in_" + nm, tn=min(p.shape[1], 1024) if p.shape[1] != ATTN_WIDTH else 768)
         for nm, p in zip(names, pieces)], axis=1)
    gw["conv_w"] = g_conv_w
    token = emit(("w_in", "conv_w"), gw)
    grad_x, g_norm1 = _in_bwd(pieces, w["w_in"], x, dx1, small["norm1_g"], token)

    gsmall = {"norm1_g": g_norm1, "gate_b": g_gate_b, "conv_b": g_conv_b, "conv_ln_g": g_ln_g, "conv_ln_b": g_ln_b,
              "norm2_g": g_norm2, "norm_f_g": g_normf}
    return loss, grad_x, gw, gsmall


ANY = pl.BlockSpec(memory_space=pl.ANY)


def _all_gather(arrs):
    n = len(arrs)

    def body(*refs):
        ins, outs = refs[:n], refs[n:2 * n]
        send_sems, recv_sems, local_sems = refs[2 * n:]
        x, y, c = lax.axis_index("x"), lax.axis_index("y"), lax.axis_index("c")
        me, sibling = (x, y, c), (x, y, 1 - c)
        chips = [(1 - x, y), (x, 1 - y), (1 - x, 1 - y)]

        def copy(a, k, block, to, src=None):
            px, py, pc = block
            dst = outs[a].at[4 * px + 2 * py + pc]
            return pltpu.make_async_remote_copy(
                src_ref=dst if src is None else src, dst_ref=dst,
                send_sem=send_sems.at[a, k], recv_sem=recv_sems.at[a, k], device_id=to, device_id_type=MESH)

        mine = [pltpu.make_async_copy(ins[a], outs[a].at[4 * x + 2 * y + c], local_sems.at[a]) for a in range(n)]
        for cp in mine:
            cp.start()
        first = []
        for j, chip in enumerate(chips):
            first += [copy(a, 1 + j, me, (*chip, c), src=ins[a]) for a in range(n)]
        first += [copy(a, 0, me, sibling, src=ins[a]) for a in range(n)]
        for cp in first:
            cp.start()
        passed = []
        for j, chip in enumerate(chips):
            for a in range(n):
                copy(a, 1 + j, (*chip, c), me).wait_recv()
                cp = copy(a, 4 + j, (*chip, c), sibling)
                cp.start()
                passed.append(cp)
        for a in range(n):
            copy(a, 0, sibling, me).wait_recv()
        for j, chip in enumerate(chips):
            for a in range(n):
                copy(a, 4 + j, (*chip, 1 - c), me).wait_recv()
        for cp in first + passed:
            cp.wait_send()
        for cp in mine:
            cp.wait()

    return pl.pallas_call(
        body, in_specs=[ANY] * n, out_specs=[ANY] * n,
        out_shape=[SDS((N_DEV,) + a.shape, a.dtype) for a in arrs],
        scratch_shapes=[pltpu.SemaphoreType.DMA((n, 7)), pltpu.SemaphoreType.DMA((n, 7)), pltpu.SemaphoreType.DMA((n,))],
        name="all_gather_weights")(*arrs)


HBM = pl.BlockSpec(memory_space=pltpu.HBM)
SEM = pl.BlockSpec(memory_space=pltpu.SEMAPHORE)
ALL_PEERS = tuple(range(1, N_DEV))
OTHER_CHIPS = (2, 4, 6)
SPLIT_EFFECT = pltpu.CompilerParams(has_side_effects=pltpu.SideEffectType.DATAFLOW_SIDE_EFFECTING)


def _exchange_copies(mode, ks, srcs, lands, send_sems, recv_sems):
    x, y, c = lax.axis_index("x"), lax.axis_index("y"), lax.axis_index("c")
    me = 4 * x + 2 * y + c
    send, recv = [], []
    for a in range(len(lands)):
        for i, k in enumerate(ks):
            peer = (x ^ ((k >> 2) & 1), y ^ ((k >> 1) & 1), c ^ (k & 1))
            pidx = 4 * peer[0] + 2 * peer[1] + peer[2]
            if mode == "gather":
                src, to, out_slot, in_slot = srcs[a], peer, me, pidx
            elif mode == "scatter":
                src, to, out_slot, in_slot = srcs[a].at[pidx], peer, me, pidx
            elif mode == "chip_scatter":
                src, to, out_slot, in_slot = srcs[a].at[pidx >> 1], peer, me >> 1, pidx >> 1
            else:
                src, to, out_slot, in_slot = lands[a].at[pidx], (x, y, 1 - c), pidx, pidx ^ 1
            s = a * len(ks) + i
            send.append(pltpu.make_async_remote_copy(
                src_ref=src, dst_ref=lands[a].at[out_slot], send_sem=send_sems.at[s], recv_sem=recv_sems.at[s],
                device_id=to, device_id_type=MESH))
            recv.append(pltpu.make_async_remote_copy(
                src_ref=src, dst_ref=lands[a].at[in_slot], send_sem=send_sems.at[s], recv_sem=recv_sems.at[s],
                device_id=to, device_id_type=MESH))
    return send, recv


def _send_start(mode, ks, name, srcs=(), lands=None):
    srcs = list(srcs)
    if lands is None:
        slots = 4 if mode == "chip_scatter" else N_DEV
        lands = [lax.empty((slots,) + (s.shape if mode == "gather" else s.shape[1:]), s.dtype) for s in srcs]
    ns, nl = len(srcs), len(lands)
    nsem = nl * len(ks)

    def body(*refs):
        send, _ = _exchange_copies(mode, ks, refs[:ns], refs[ns:ns + nl], refs[ns + nl], refs[ns + nl + 1])
        for cp in send:
            cp.start()
        token = refs[-1]
        token[...] = jnp.zeros_like(token)

    both = srcs + list(lands)
    res = pl.pallas_call(
        body, name=name,
        out_shape=(pltpu.SemaphoreType.DMA((nsem,)), pltpu.SemaphoreType.DMA((nsem,)),
                   *[pltpu.HBM(a.shape, a.dtype) for a in both], SDS((8, 128), F32)),
        in_specs=[HBM] * (ns + nl), out_specs=(SEM, SEM, *([HBM] * (ns + nl)), pl.BlockSpec(memory_space=pltpu.VMEM)),
        input_output_aliases={i: 2 + i for i in range(ns + nl)}, compiler_params=SPLIT_EFFECT,
    )(*[pltpu.with_memory_space_constraint(a, pltpu.HBM) for a in both])
    return dict(mode=mode, ks=ks, send_sems=res[0], recv_sems=res[1], srcs=res[2:2 + ns], lands=res[2 + ns:2 + ns + nl],
                token=res[-1])


def _send_wait(started, after, name):
    ns, nl = len(started["srcs"]), len(started["lands"])

    def body(*refs):
        send, recv = _exchange_copies(started["mode"], started["ks"], refs[:ns], refs[ns:ns + nl],
                                      refs[ns + nl], refs[ns + nl + 1])
        for cp in send:
            cp.wait_send()
        for cp in recv:
            cp.wait_recv()

    both = list(started["srcs"]) + list(started["lands"])
    res = pl.pallas_call(
        body, name=name,
        out_shape=tuple(pltpu.HBM(a.shape, a.dtype) for a in both),
        in_specs=[HBM] * (ns + nl) + [SEM, SEM, ANY], out_specs=tuple([HBM] * (ns + nl)),
        input_output_aliases={i: i for i in range(ns + nl)}, compiler_params=SPLIT_EFFECT,
    )(*both, started["send_sems"], started["recv_sems"], after)
    return res[:ns], res[ns:]


def _exchange_sibling(gs):
    n = len(gs)

    def body(*refs):
        ins, outs = refs[:n], refs[n:2 * n]
        send_sems, recv_sems = refs[2 * n:]
        x, y, c = lax.axis_index("x"), lax.axis_index("y"), lax.axis_index("c")
        copies = []
        for a in range(n):
            for j in range(4):
                copies.append(pltpu.make_async_remote_copy(
                    src_ref=ins[a].at[2 * j + (1 - c)], dst_ref=outs[a].at[j],
                    send_sem=send_sems.at[a, j], recv_sem=recv_sems.at[a, j],
                    device_id=(x, y, 1 - c), device_id_type=MESH))
        for cp in copies:
            cp.start()
        for cp in copies:
            cp.wait_recv()
        for cp in copies:
            cp.wait_send()

    return pl.pallas_call(
        body, in_specs=[ANY] * n, out_specs=[ANY] * n,
        out_shape=[SDS((4,) + g.shape[1:], g.dtype) for g in gs],
        scratch_shapes=[pltpu.SemaphoreType.DMA((n, 4)), pltpu.SemaphoreType.DMA((n, 4))],
        name="reduce_scatter_sibling")(*gs)


def _add_pair(g, r1, core, name):
    _, rows, cols = g.shape
    tr = _row_tile(rows, cols, 3 * g.dtype.itemsize)

    def body(c_ref, g_ref, r_ref, o_ref):
        o_ref[...] = (g_ref[...].astype(F32) + r_ref[...].astype(F32)).astype(o_ref.dtype)

    return pl.pallas_call(
        body,
        grid_spec=pltpu.PrefetchScalarGridSpec(
            num_scalar_prefetch=1, grid=(4, rows // tr),
            in_specs=[pl.BlockSpec((1, tr, cols), lambda j, i, c_ref: (2 * j + c_ref[0], i, 0)),
                      pl.BlockSpec((1, tr, cols), lambda j, i, c_ref: (j, i, 0))],
            out_specs=pl.BlockSpec((1, tr, cols), lambda j, i, c_ref: (j, i, 0))),
        out_shape=SDS((4, rows, cols), g.dtype),
        compiler_params=_params("parallel", "parallel"), name=name)(core, g, r1)


def _row_tile(rows, cols, itemsize_total):
    budget = (4 << 20) // max(1, cols * itemsize_total)
    if rows <= budget:
        return rows
    t = rows
    while t > budget and t % 2 == 0 and (t // 2) % 16 == 0:
        t //= 2
    return t


def _adam_math(g, w, m, v):
    m_new = ADAM_B1 * m + (1.0 - ADAM_B1) * g
    v_new = ADAM_B2 * v + (1.0 - ADAM_B2) * (g * g)
    m_hat = m_new / (1.0 - ADAM_B1 ** ADAM_STEP)
    v_hat = v_new / (1.0 - ADAM_B2 ** ADAM_STEP)
    delta = -ADAM_LR * (m_hat / (jnp.sqrt(v_hat) + ADAM_EPS) + ADAM_WD * w)
    return delta, m_new, v_new


def _sum_adam(parts, w, m, v, name):
    rows, cols = w.shape
    nparts = parts.shape[0]
    tr = _row_tile(rows, cols, nparts * parts.dtype.itemsize + 7 * 4)

    def body(p_ref, w_ref, m_ref, v_ref, g_ref, d_ref, mo_ref, vo_ref):
        g = p_ref[0].astype(F32)
        for s in range(1, nparts):
            g = g + p_ref[s].astype(F32)
        delta, m_new, v_new = _adam_math(g, w_ref[...], m_ref[...], v_ref[...])
        g_ref[...] = g
        d_ref[...] = delta
        mo_ref[...] = m_new
        vo_ref[...] = v_new

    blk = pl.BlockSpec((tr, cols), lambda i: (i, 0))
    out = SDS((rows, cols), F32)
    return pl.pallas_call(
        body, grid=(rows // tr,),
        in_specs=[pl.BlockSpec((nparts, tr, cols), lambda i: (0, i, 0)), blk, blk, blk],
        out_specs=[blk, blk, blk, blk], out_shape=[out, out, out, out],
        compiler_params=_params("parallel"), name=name)(parts, w, m, v)


SMALL_ROWS = 72


def _small_allreduce_adam(gpart, w, m, v, dep=None):
    def body(g_ref, w_ref, m_ref, v_ref, go_ref, d_ref, mo_ref, vo_ref, gath, send_sems, recv_sems):
        x, y, c = lax.axis_index("x"), lax.axis_index("y"), lax.axis_index("c")
        me = 4 * x + 2 * y + c
        gath[me] = g_ref[...]
        copies = []
        for k in range(1, N_DEV):
            fx, fy, fc = (k >> 2) & 1, (k >> 1) & 1, k & 1
            peer = (x ^ fx, y ^ fy, c ^ fc)
            copies.append(pltpu.make_async_remote_copy(
                src_ref=gath.at[me], dst_ref=gath.at[me], send_sem=send_sems.at[k - 1], recv_sem=recv_sems.at[k - 1],
                device_id=peer, device_id_type=MESH))
        for cp in copies:
            cp.start()
        for cp in copies:
            cp.wait_recv()
        for cp in copies:
            cp.wait_send()
        g = gath[0]
        for d in range(1, N_DEV):
            g = g + gath[d]
        delta, m_new, v_new = _adam_math(g, w_ref[...], m_ref[...], v_ref[...])
        go_ref[...] = g
        d_ref[...] = delta
        mo_ref[...] = m_new
        vo_ref[...] = v_new

    vm = pl.BlockSpec(memory_space=pltpu.VMEM)
    out = SDS((SMALL_ROWS, 128), F32)
    body, dep_spec, dep_arg = _anchored(body, 4, dep)
    return pl.pallas_call(
        body, in_specs=[vm] * 4 + dep_spec, out_specs=[vm] * 4, out_shape=[out] * 4,
        scratch_shapes=[pltpu.VMEM((N_DEV, SMALL_ROWS, 128), F32), pltpu.SemaphoreType.DMA((N_DEV - 1,)),
                        pltpu.SemaphoreType.DMA((N_DEV - 1,))],
        name="small_allreduce_adam")(gpart, w, m, v, *dep_arg)


BIG = ("w_in", "conv_w", "w_conv_out", "w_attn_out", "w_o", "w_ffn_gate", "w_ffn_up", "w_ffn_down")
EARLY = ("w_in", "conv_w")
LATE_MERGE = ("w_conv_out", "w_attn_out", "w_o")
LATE_FFN = ("w_ffn_gate", "w_ffn_up", "w_ffn_down")
COL_SHARDED = ("w_in", "conv_w", "w_attn_out", "w_ffn_gate", "w_ffn_up")
SMALL = ("norm1_g", "gate_b", "conv_b", "conv_ln_g", "conv_ln_b", "norm2_g", "norm_f_g")
WEIGHTS = ("norm1_g", "w_in", "gate_b", "conv_w", "conv_b", "conv_ln_g", "conv_ln_b", "w_conv_out", "w_attn_out", "w_o",
           "norm2_g", "w_ffn_gate", "w_ffn_up", "w_ffn_down", "norm_f_g")


def _shard2d(name, a):
    a = a.reshape(a.shape[-2], a.shape[-1])
    if name == "conv_w":
        a = jnp.pad(a, ((0, CONV_PAD - CONV_K), (0, 0)))
    return a


def _gathered_to_full(name, g):
    if name in COL_SHARDED:
        return g.transpose(1, 0, 2).reshape(g.shape[1], N_DEV * g.shape[2])
    return g.reshape(N_DEV * g.shape[1], g.shape[2])


def _full_to_blocks(name, g):
    if name in COL_SHARDED:
        return g.reshape(g.shape[0], N_DEV, g.shape[1] // N_DEV).transpose(1, 0, 2)
    return g.reshape(N_DEV, g.shape[0] // N_DEV, g.shape[1])


def _pack_small(d, last_rows):
    vec = jnp.concatenate([d[n].reshape(-1) for n in SMALL]).reshape(SMALL_ROWS - SUBLANES, 128)
    return jnp.concatenate([vec, last_rows], axis=0)


def _unpack_small(p, like):
    flat = p.reshape(-1)
    out, off = {}, 0
    for n in SMALL:
        size = like[n].size
        out[n] = flat[off:off + size].reshape(like[n].shape)
        off += size
    return out


def kernel(x, norm1_g, w_in, gate_b, conv_w, conv_b, conv_ln_g, conv_ln_b, w_conv_out, w_attn_out, w_o, norm2_g, w_ffn_gate, w_ffn_up, w_ffn_down, norm_f_g, loss_target, m_norm1_g, m_w_in, m_gate_b, m_conv_w, m_conv_b, m_conv_ln_g, m_conv_ln_b, m_w_conv_out, m_w_attn_out, m_w_o, m_norm2_g, m_w_ffn_gate, m_w_ffn_up, m_w_ffn_down, m_norm_f_g, v_norm1_g, v_w_in, v_gate_b, v_conv_w, v_conv_b, v_conv_ln_g, v_conv_ln_b, v_w_conv_out, v_w_attn_out, v_w_o, v_norm2_g, v_w_ffn_gate, v_w_ffn_up, v_w_ffn_down, v_norm_f_g):
    wts = dict(norm1_g=norm1_g, w_in=w_in, gate_b=gate_b, conv_w=conv_w, conv_b=conv_b, conv_ln_g=conv_ln_g,
               conv_ln_b=conv_ln_b, w_conv_out=w_conv_out, w_attn_out=w_attn_out, w_o=w_o, norm2_g=norm2_g,
               w_ffn_gate=w_ffn_gate, w_ffn_up=w_ffn_up, w_ffn_down=w_ffn_down, norm_f_g=norm_f_g)
    mom1 = dict(norm1_g=m_norm1_g, w_in=m_w_in, gate_b=m_gate_b, conv_w=m_conv_w, conv_b=m_conv_b, conv_ln_g=m_conv_ln_g,
                conv_ln_b=m_conv_ln_b, w_conv_out=m_w_conv_out, w_attn_out=m_w_attn_out, w_o=m_w_o, norm2_g=m_norm2_g,
                w_ffn_gate=m_w_ffn_gate, w_ffn_up=m_w_ffn_up, w_ffn_down=m_w_ffn_down, norm_f_g=m_norm_f_g)
    mom2 = dict(norm1_g=v_norm1_g, w_in=v_w_in, gate_b=v_gate_b, conv_w=v_conv_w, conv_b=v_conv_b, conv_ln_g=v_conv_ln_g,
                conv_ln_b=v_conv_ln_b, w_conv_out=v_w_conv_out, w_attn_out=v_w_attn_out, w_o=v_w_o, norm2_g=v_norm2_g,
                w_ffn_gate=v_w_ffn_gate, w_ffn_up=v_w_ffn_up, w_ffn_down=v_w_ffn_down, norm_f_g=v_norm_f_g)

    T = x.shape[0] * x.shape[1]
    x2 = x.reshape(T, D_MODEL)
    t2 = loss_target.reshape(T, D_MODEL)

    me = 4 * lax.axis_index("x") + 2 * lax.axis_index("y") + lax.axis_index("c")
    shards = {n: _shard2d(n, wts[n]) for n in BIG}
    sent = {n: shards[n] if n == "conv_w" else shards[n].astype(BF16) for n in BIG}
    small = {n: wts[n].reshape(1, -1) for n in SMALL}

    gathered = _all_gather([sent[n] for n in EARLY])
    full = {n: _gathered_to_full(n, g) for n, g in zip(EARLY, gathered)}
    merge_gather = _send_start("gather", ALL_PEERS, "gather_start_merge", [sent[n] for n in LATE_MERGE])
    ffn_gather = _send_start("gather", (1,) + OTHER_CHIPS, "gather_start_ffn", [sent[n] for n in LATE_FFN])
    ffn_state = {}

    def filled(names, srcs, lands):
        return {n: _gathered_to_full(n, lax.dynamic_update_slice(land, src[None], (me, 0, 0)))
                for n, src, land in zip(names, srcs, lands)}

    def late_weights(names, after):
        if names is LATE_MERGE:
            srcs, lands = _send_wait(merge_gather, after, "gather_wait_merge")
            ffn_state["srcs"], ffn_lands = _send_wait(ffn_gather, after, "gather_wait_ffn")
            ffn_state["forward"] = _send_start("forward", OTHER_CHIPS, "forward_start_ffn", lands=ffn_lands)
            return {**filled(names, srcs, lands), "token": ffn_state["forward"]["token"]}
        _, lands = _send_wait(ffn_state["forward"], after, "forward_wait_ffn")
        return filled(names, ffn_state["srcs"], lands)

    scatters = []
    core = lax.axis_index("c").astype(jnp.int32).reshape(1)

    def emit(names, gw):
        blocks = [_full_to_blocks(n, gw[n]) for n in names]
        if "w_in" in names:
            sums = [_add_pair(g, r, core, "chip_sum_" + n) for n, g, r in zip(names, blocks, _exchange_sibling(blocks))]
            started = _send_start("chip_scatter", OTHER_CHIPS, "scatter_start_" + names[0], sums)
        else:
            started = _send_start("scatter", ALL_PEERS, "scatter_start_" + names[0], blocks)
        scatters.append((names, started))
        return started["token"]

    loss_part, grad_x, gw, gsmall = _local_step(x2, t2, full, small, ffn_gather["token"], late_weights, emit)

    grads, deltas, new_m, new_v = {}, {}, {}, {}
    after = grad_x
    for names, started in scatters:
        srcs, lands = _send_wait(started, after, "scatter_wait_" + names[0])
        mine = me >> 1 if started["mode"] == "chip_scatter" else me
        for n, src, land in zip(names, srcs, lands):
            parts = lax.dynamic_update_slice(land, lax.dynamic_slice_in_dim(src, mine, 1, axis=0), (mine, 0, 0))
            g, d, mo, vo = _sum_adam(parts, shards[n], _shard2d(n, mom1[n]), _shard2d(n, mom2[n]), "adam_" + n)
            for dst, val in ((grads, g), (deltas, d), (new_m, mo), (new_v, vo)):
                if n == "conv_w":
                    val = val[:CONV_K]
                dst[n] = val.reshape(wts[n].shape)
            after = g

    zeros, ones = jnp.zeros((SUBLANES, 128), F32), jnp.ones((SUBLANES, 128), F32)
    sg, sd, sm, sv = _small_allreduce_adam(
        _pack_small(gsmall, jnp.broadcast_to(loss_part, (SUBLANES, 128))), _pack_small(wts, zeros),
        _pack_small(mom1, zeros), _pack_small(mom2, ones), after)
    for dst, val in ((grads, sg), (deltas, sd), (new_m, sm), (new_v, sv)):
        dst.update(_unpack_small(val, wts))
    loss = sg[SMALL_ROWS - SUBLANES, 0]
    return (loss, grad_x.reshape(x.shape), *[grads[n] for n in WEIGHTS], *[deltas[n] for n in WEIGHTS],
            *[new_m[n] for n in WEIGHTS], *[new_v[n] for n in WEIGHTS])
```

```python
import math

import numpy as np
import jax
import jax.numpy as jnp
from jax import lax
from jax.experimental import pallas as pl
from jax.experimental.pallas import tpu as pltpu

F32 = jnp.float32
BF16 = jnp.bfloat16
SDS = jax.ShapeDtypeStruct
MESH = pl.DeviceIdType.MESH

D_MODEL = 1024
SEQ = 2048
HEAD_DIM = 64
GROUPS = ((128, 1), (512, 4), (2048, 16))
HEADS_PER_GROUP = 8
N_HEADS = 24
ATTN_WIDTH = N_HEADS * HEAD_DIM
ATTN_OUT = HEADS_PER_GROUP * HEAD_DIM
CONV_K = 31
CONV_PAD = 32
D_FF = 2816
IN_WIDTH = 3 * ATTN_WIDTH + 2 * D_MODEL + 2 * D_MODEL
RMS_EPS = 1e-6
LN_EPS = 1e-5
Q_BLOCK = 128
LANES = 128
NEG = -1e30
N_DEV = 8

ADAM_LR = 0.001
ADAM_B1 = 0.9
ADAM_B2 = 0.999
ADAM_EPS = 1e-08
ADAM_WD = 0.01
ADAM_STEP = 10


def _alibi_slope_list(n):
    def pow2(m):
        start = 2.0 ** (-8.0 / m)
        return [start ** (i + 1) for i in range(m)]
    if math.log2(n).is_integer():
        return pow2(n)
    c = 2 ** math.floor(math.log2(n))
    return pow2(c) + _alibi_slope_list(2 * c)[0::2][: n - c]


def _slopes_times_dilation():
    s = np.asarray(sorted(_alibi_slope_list(N_HEADS), reverse=True), dtype=np.float32).reshape(3, HEADS_PER_GROUP)
    r = np.asarray([g[1] for g in GROUPS], dtype=np.float32)[:, None]
    return (s * r).reshape(N_HEADS)


def _sigmoid(x):
    return 1.0 / (1.0 + jnp.exp(-x))


def _dot(a, b):
    return jnp.dot(a, b, preferred_element_type=F32)


def _dot_nt(a, b):
    return lax.dot_general(a, b, (((1,), (1,)), ((), ())), preferred_element_type=F32)


def _dot_tn(a, b):
    return lax.dot_general(a, b, (((0,), (0,)), ((), ())), preferred_element_type=F32)


def _rowsum(x):
    return jnp.sum(x, axis=0, keepdims=True)


def _params(*sem):
    return pltpu.CompilerParams(dimension_semantics=sem)


def _anchored(body, n_in, dep):
    if dep is None:
        return body, [], []

    def wrapped(*refs):
        return body(*refs[:n_in], *refs[n_in + 1:])

    return wrapped, [pl.BlockSpec(memory_space=pl.ANY)], [dep]


IN_TM = 256
IN_CHUNK = 512


def _in_proj(x, g1, w_in, dep=None):
    T = x.shape[0]
    tm = IN_TM
    widths = (3 * ATTN_WIDTH, 2 * D_MODEL, 2 * D_MODEL)

    def body(x_ref, g_ref, w_hbm, h_ref, qkv_ref, u_ref, lg_ref, w_vmem, sem):
        @pl.when(pl.program_id(0) == 0)
        def _():
            cp = pltpu.make_async_copy(w_hbm, w_vmem, sem)
            cp.start()
            cp.wait()

        xv = x_ref[...]
        r = lax.rsqrt(jnp.mean(xv * xv, axis=-1, keepdims=True) + RMS_EPS)
        h = (xv * r * g_ref[...]).astype(BF16)
        h_ref[...] = h
        col = 0
        for o_ref, width in zip((qkv_ref, u_ref, lg_ref), widths):
            for j in range(width // IN_CHUNK):
                o_ref[:, j * IN_CHUNK:(j + 1) * IN_CHUNK] = _dot(h, w_vmem[:, col:col + IN_CHUNK])
                col += IN_CHUNK

    row = lambda n: pl.BlockSpec((tm, n), lambda i: (i, 0))
    body, dep_spec, dep_arg = _anchored(body, 3, dep)
    return pl.pallas_call(
        body, grid=(T // tm,),
        in_specs=[row(D_MODEL), pl.BlockSpec((1, D_MODEL), lambda i: (0, 0)), pl.BlockSpec(memory_space=pl.ANY)] + dep_spec,
        out_specs=[row(D_MODEL)] + [row(n) for n in widths],
        out_shape=[SDS((T, D_MODEL), BF16)] + [SDS((T, n), F32) for n in widths],
        scratch_shapes=[pltpu.VMEM((D_MODEL, IN_WIDTH), BF16), pltpu.SemaphoreType.DMA],
        compiler_params=_params("arbitrary"), name="in_proj")(x, g1, w_in, *dep_arg)


def _mm_tn(a, b, out_dtype, name, tn, tt=1024):
    T, K = a.shape
    N = b.shape[1]
    nt = T // tt

    def body(a_ref, b_ref, o_ref, acc):
        t = pl.program_id(1)

        @pl.when(t == 0)
        def _():
            acc[...] = jnp.zeros_like(acc)

        acc[...] += _dot_tn(a_ref[...], b_ref[...])

        @pl.when(t == nt - 1)
        def _():
            o_ref[...] = acc[...].astype(o_ref.dtype)

    return pl.pallas_call(
        body, grid=(N // tn, nt),
        in_specs=[pl.BlockSpec((tt, K), lambda j, t: (t, 0)),
                  pl.BlockSpec((tt, tn), lambda j, t: (t, j))],
        out_specs=pl.BlockSpec((K, tn), lambda j, t: (0, j)),
        out_shape=SDS((K, N), out_dtype),
        scratch_shapes=[pltpu.VMEM((K, tn), F32)],
        compiler_params=_params("parallel", "arbitrary"), name=name)(a, b)


def _gather_classes(src_ref, dst, r, row0=0):
    L = SEQ // r
    for c in range(r):
        dst[row0 + c * L:row0 + (c + 1) * L, :] = src_ref[0, pl.ds(c, L, stride=r), :].astype(dst.dtype)


def _scatter_classes(src, dst, r, row0=0):
    L = SEQ // r
    for c in range(r):
        dst[pl.ds(c, L, stride=r), :] = src[row0 + c * L:row0 + (c + 1) * L, :].astype(dst.dtype)


def _attn_masks(slope_r):
    qi = lax.broadcasted_iota(jnp.int32, (Q_BLOCK, Q_BLOCK), 0)
    kj = lax.broadcasted_iota(jnp.int32, (Q_BLOCK, Q_BLOCK), 1)
    rel = (qi - kj).astype(F32)
    bias_cur = jnp.where(qi >= kj, -slope_r * rel, NEG)
    bias_prev = jnp.where(qi <= kj, -slope_r * (rel + float(Q_BLOCK)), NEG)
    return bias_cur, bias_prev


def _store_biases(bias, sl_ref, g, hp):
    for hh in range(2):
        cur, prev = _attn_masks(sl_ref[g * HEADS_PER_GROUP + 2 * hp + hh])
        rows = slice(hh * Q_BLOCK, (hh + 1) * Q_BLOCK)
        bias[0, rows, 0:Q_BLOCK] = prev
        bias[1, rows, 0:Q_BLOCK] = jnp.full((Q_BLOCK, Q_BLOCK), NEG, F32)
        bias[0, rows, Q_BLOCK:] = cur
        bias[1, rows, Q_BLOCK:] = cur


def _transpose_pairs(src, dst):
    dst[0, :, 0:Q_BLOCK] = jnp.zeros((LANES, Q_BLOCK), dst.dtype)
    nblk = SEQ // Q_BLOCK
    for b in range(nblk):
        t = src[(b + 1) * Q_BLOCK:(b + 2) * Q_BLOCK, :].T
        dst[b, :, Q_BLOCK:] = t
        if b + 1 < nblk:
            dst[b + 1, :, 0:Q_BLOCK] = t


def _stack_heads(t, low):
    z = jnp.zeros_like(t)
    return jnp.concatenate([jnp.where(low, t, z), jnp.where(low, z, t)], axis=0)


def _unstack_heads(t2, low):
    return jnp.where(low, t2[:Q_BLOCK], t2[Q_BLOCK:])


def _unit_offsets(u, nb):
    off = pl.multiple_of(u * Q_BLOCK, Q_BLOCK)
    n = u & (nb - 1)
    c = u >> int(math.log2(nb))
    return off, n == 0, c, n


ATTN_UNROLL = 4


def _attn_fwd(qkv, slopes_r, batch):
    nblk = SEQ // Q_BLOCK

    def body(sl_ref, *refs):
        qkv_refs = refs[:9]
        att_ref, lse_ref = refs[9:11]
        qd, kd, vd, kt, opos, lpos, bias = refs[11:]
        hp = pl.program_id(1)
        low = lax.broadcasted_iota(jnp.int32, (Q_BLOCK, LANES), 1) < HEAD_DIM

        for g in range(3):
            r = GROUPS[g][1]
            nb = SEQ // r // Q_BLOCK
            _gather_classes(qkv_refs[3 * g], qd, r)
            kd[0:Q_BLOCK, :] = jnp.zeros((Q_BLOCK, LANES), BF16)
            vd[0:Q_BLOCK, :] = jnp.zeros((Q_BLOCK, LANES), BF16)
            _gather_classes(qkv_refs[3 * g + 1], kd, r, Q_BLOCK)
            _gather_classes(qkv_refs[3 * g + 2], vd, r, Q_BLOCK)
            _transpose_pairs(kd, kt)
            _store_biases(bias, sl_ref, g, hp)

            def unit(u, carry, g=g, r=r, nb=nb):
                off, first, c, n = _unit_offsets(u, nb)
                q2 = _stack_heads(qd[pl.ds(off, Q_BLOCK), :], low)
                s = _dot(q2, kt[u]) * 0.125 + bias[first.astype(jnp.int32)]
                m = jnp.max(s, axis=-1, keepdims=True)
                p = jnp.exp(s - m)
                l = jnp.sum(p, axis=-1, keepdims=True)
                o2 = _dot(p.astype(BF16), vd[pl.ds(off, 2 * Q_BLOCK), :]) * (1.0 / l)
                lse2 = m + jnp.log(l)
                rows = pl.ds(c + n * (Q_BLOCK * r), Q_BLOCK, stride=r)
                opos[g, rows, :] = _unstack_heads(o2, low)
                lpos[g, rows, :] = jnp.where(low, lse2[:Q_BLOCK], lse2[Q_BLOCK:])
                return carry

            lax.fori_loop(0, nblk, unit, 0, unroll=ATTN_UNROLL)

        def merge(i, carry):
            rows = pl.ds(pl.multiple_of(i * 256, 256), 256)
            l0, l1, l2 = lpos[0, rows, :], lpos[1, rows, :], lpos[2, rows, :]
            m = jnp.maximum(jnp.maximum(l0, l1), l2)
            e0, e1, e2 = jnp.exp(l0 - m), jnp.exp(l1 - m), jnp.exp(l2 - m)
            den = e0 + e1 + e2
            att = (e0 * opos[0, rows, :] + e1 * opos[1, rows, :] + e2 * opos[2, rows, :]) / den
            att_ref[0, rows, :] = att.astype(att_ref.dtype)
            lse_ref[0, rows, :] = m + jnp.log(den)
            return carry

        lax.fori_loop(0, SEQ // 256, merge, 0)

    def col(sec, g):
        return pl.BlockSpec((1, SEQ, LANES), lambda b, hp: (b, 0, sec * 12 + g * 4 + hp))

    out = pl.BlockSpec((1, SEQ, LANES), lambda b, hp: (b, 0, hp))
    return pl.pallas_call(
        body, grid=(batch, 4),
        in_specs=[pl.BlockSpec(memory_space=pltpu.SMEM)] + [col(sec, g) for g in range(3) for sec in range(3)],
        out_specs=[out, out],
        out_shape=[SDS((batch, SEQ, ATTN_OUT), BF16), SDS((batch, SEQ, ATTN_OUT), F32)],
        scratch_shapes=[pltpu.VMEM((SEQ, LANES), BF16), pltpu.VMEM((Q_BLOCK + SEQ, LANES), BF16),
                        pltpu.VMEM((Q_BLOCK + SEQ, LANES), BF16), pltpu.VMEM((nblk, LANES, 2 * Q_BLOCK), BF16),
                        pltpu.VMEM((3, SEQ, LANES), F32), pltpu.VMEM((3, SEQ, LANES), F32),
                        pltpu.VMEM((2, 2 * Q_BLOCK, 2 * Q_BLOCK), F32)],
        compiler_params=_params("parallel", "parallel"), name="attn_fwd")(slopes_r, *([qkv] * 9))


def _attn_bwd(qkv, datt, lse, dsum, slopes_r, batch):
    nblk = SEQ // Q_BLOCK

    def body(sl_ref, q_ref, k_ref, v_ref, do_ref, l_ref, d_ref, dq_ref, dk_ref, dv_ref,
             qd, kd, vd, dod, kt, vt, ld, dd, dq_acc, dk_acc, dv_acc, dk_part, dv_part, stage, bias):
        gid, hp = pl.program_id(1), pl.program_id(2)
        low = lax.broadcasted_iota(jnp.int32, (Q_BLOCK, LANES), 1) < HEAD_DIM

        def section(g):
            r = GROUPS[g][1]
            nb = SEQ // r // Q_BLOCK
            _gather_classes(q_ref, qd, r)
            kd[0:Q_BLOCK, :] = jnp.zeros((Q_BLOCK, LANES), BF16)
            vd[0:Q_BLOCK, :] = jnp.zeros((Q_BLOCK, LANES), BF16)
            _gather_classes(k_ref, kd, r, Q_BLOCK)
            _gather_classes(v_ref, vd, r, Q_BLOCK)
            _gather_classes(do_ref, dod, r)
            _gather_classes(l_ref, ld, r)
            _gather_classes(d_ref, dd, r)
            _transpose_pairs(kd, kt)
            _transpose_pairs(vd, vt)
            _store_biases(bias, sl_ref, g, hp)

            def unit(u, carry):
                off, first, _, _ = _unit_offsets(u, nb)
                pair = pl.ds(off, 2 * Q_BLOCK)
                q2 = _stack_heads(qd[pl.ds(off, Q_BLOCK), :], low)
                do2 = _stack_heads(dod[pl.ds(off, Q_BLOCK), :], low)
                lse_t = ld[pl.ds(off, Q_BLOCK), :]
                dsum_t = dd[pl.ds(off, Q_BLOCK), :]
                lse2 = jnp.concatenate([lse_t[:, 0:1], lse_t[:, HEAD_DIM:HEAD_DIM + 1]], axis=0)
                dsum2 = jnp.concatenate([dsum_t[:, 0:1], dsum_t[:, HEAD_DIM:HEAD_DIM + 1]], axis=0)
                s = _dot(q2, kt[u]) * 0.125 + bias[first.astype(jnp.int32)]
                p = jnp.exp(s - lse2)
                ds = (p * (_dot(do2, vt[u]) - dsum2)).astype(BF16)
                dq_acc[pl.ds(off, Q_BLOCK), :] = _unstack_heads(_dot(ds, kd[pair, :]), low) * 0.125
                dk_part[u] = _dot_tn(ds, q2) * 0.125
                dv_part[u] = _dot_tn(p.astype(BF16), do2)
                return carry

            lax.fori_loop(0, nblk, unit, 0, unroll=ATTN_UNROLL)
            for part, acc in ((dk_part, dk_acc), (dv_part, dv_acc)):
                for b in range(nblk):
                    t = part[b, Q_BLOCK:, :]
                    if b + 1 < nblk:
                        t = t + part[b + 1, 0:Q_BLOCK, :]
                    acc[b * Q_BLOCK:(b + 1) * Q_BLOCK, :] = t
            for acc, out_ref in ((dq_acc, dq_ref), (dk_acc, dk_ref), (dv_acc, dv_ref)):
                _scatter_classes(acc, stage, r)
                out_ref[0] = stage[...].astype(out_ref.dtype)

        for g in range(3):
            pl.when(gid == g)(lambda g=g: section(g))

    def col(sec):
        return pl.BlockSpec((1, SEQ, LANES), lambda b, g, hp: (b, 0, sec * 12 + g * 4 + hp))

    pos = pl.BlockSpec((1, SEQ, LANES), lambda b, g, hp: (b, 0, hp))
    dout = pl.BlockSpec((1, SEQ, LANES), lambda b, g, hp: (b, 0, g * 4 + hp))
    out = SDS((batch, SEQ, ATTN_WIDTH), BF16)
    seq_bf = pltpu.VMEM((SEQ, LANES), BF16)
    seq_f = pltpu.VMEM((SEQ, LANES), F32)
    pad_bf = pltpu.VMEM((Q_BLOCK + SEQ, LANES), BF16)
    part = pltpu.VMEM((nblk, 2 * Q_BLOCK, LANES), F32)
    blk_t = pltpu.VMEM((nblk, LANES, 2 * Q_BLOCK), BF16)
    return pl.pallas_call(
        body, grid=(batch, 3, 4),
        in_specs=[pl.BlockSpec(memory_space=pltpu.SMEM), col(0), col(1), col(2), pos, pos, pos],
        out_specs=[dout, dout, dout],
        out_shape=[out, out, out],
        scratch_shapes=[seq_bf, pad_bf, pad_bf, seq_bf, blk_t, blk_t, seq_f, seq_f, seq_f, seq_f, seq_f, part, part, seq_f,
                        pltpu.VMEM((2, 2 * Q_BLOCK, 2 * Q_BLOCK), F32)],
        compiler_params=_params("parallel", "parallel", "parallel"), name="attn_bwd")(
            slopes_r, qkv, qkv, qkv, datt, lse, dsum)


CONV_TC = 128
CONV_ROWS = 128
SUBLANES = 8


def _fill_shifted(sh):
    n = SEQ + CONV_PAD - SUBLANES
    for s in range(1, SUBLANES):
        sh[s, 0:n, :] = sh[0, s:s + n, :]


def _tap(sh, base, offset):
    s = offset % SUBLANES
    return sh[s, pl.ds(pl.multiple_of(base + (offset - s), SUBLANES), CONV_ROWS), :]


def _conv_fwd(u, conv_w, conv_b, batch):
    nct = D_MODEL // CONV_TC

    def body(ua_ref, ub_ref, w_ref, b_ref, o_ref, sh):
        sh[0, 0:CONV_PAD, :] = jnp.zeros((CONV_PAD, CONV_TC), F32)
        sh[0, CONV_PAD:, :] = ua_ref[0] * _sigmoid(ub_ref[0])
        _fill_shifted(sh)

        def chunk(c, carry):
            base = pl.multiple_of(c * CONV_ROWS, CONV_ROWS)
            acc = jnp.broadcast_to(b_ref[...], (CONV_ROWS, CONV_TC))
            for t in range(CONV_K):
                acc = acc + _tap(sh, base, t + CONV_PAD - (CONV_K - 1)) * w_ref[t:t + 1, :]
            o_ref[0, pl.ds(base, CONV_ROWS), :] = acc
            return carry

        lax.fori_loop(0, SEQ // CONV_ROWS, chunk, 0)

    return pl.pallas_call(
        body, grid=(nct, batch),
        in_specs=[pl.BlockSpec((1, SEQ, CONV_TC), lambda j, b: (b, 0, j)),
                  pl.BlockSpec((1, SEQ, CONV_TC), lambda j, b: (b, 0, j + nct)),
                  pl.BlockSpec((CONV_PAD, CONV_TC), lambda j, b: (0, j)),
                  pl.BlockSpec((1, CONV_TC), lambda j, b: (0, j))],
        out_specs=pl.BlockSpec((1, SEQ, CONV_TC), lambda j, b: (b, 0, j)),
        out_shape=SDS((batch, SEQ, D_MODEL), F32),
        scratch_shapes=[pltpu.VMEM((SUBLANES, SEQ + CONV_PAD, CONV_TC), F32)],
        compiler_params=_params("parallel", "parallel"), name="conv_fwd")(u, u, conv_w, conv_b)


def _conv_bwd(u, dc1, conv_w, batch, dep=None):
    nct = D_MODEL // CONV_TC
    nchunk = SEQ // CONV_ROWS

    def body(ua_ref, ub_ref, d_ref, w_ref, dua_ref, dub_ref, gw_ref, gb_ref, shc, shd, gacc):
        b = pl.program_id(1)
        shc[0, 0:CONV_PAD, :] = jnp.zeros((CONV_PAD, CONV_TC), F32)
        shc[0, CONV_PAD:, :] = ua_ref[0] * _sigmoid(ub_ref[0])
        _fill_shifted(shc)
        shd[0, 0:SEQ, :] = d_ref[0]
        shd[0, SEQ:, :] = jnp.zeros((CONV_PAD, CONV_TC), F32)
        _fill_shifted(shd)

        @pl.when(b == 0)
        def _():
            gacc[...] = jnp.zeros_like(gacc)
            gb_ref[...] = jnp.zeros_like(gb_ref)

        gb_ref[...] += _rowsum(d_ref[0])

        def chunk(c, carry):
            base = pl.multiple_of(c * CONV_ROWS, CONV_ROWS)
            dcur = shd[0, pl.ds(base, CONV_ROWS), :]
            acc = jnp.zeros((CONV_ROWS, CONV_TC), F32)
            for t in range(CONV_K):
                acc = acc + _tap(shd, base, CONV_K - 1 - t) * w_ref[t:t + 1, :]
                prod = _tap(shc, base, t + CONV_PAD - (CONV_K - 1)) * dcur
                gacc[t] += jnp.sum(prod.reshape(CONV_ROWS // 8, 8, CONV_TC), axis=0)
            ua = ua_ref[0, pl.ds(base, CONV_ROWS), :]
            sg = _sigmoid(ub_ref[0, pl.ds(base, CONV_ROWS), :])
            dua_ref[0, pl.ds(base, CONV_ROWS), :] = (acc * sg).astype(dua_ref.dtype)
            dub_ref[0, pl.ds(base, CONV_ROWS), :] = (acc * ua * sg * (1.0 - sg)).astype(dub_ref.dtype)
            return carry

        lax.fori_loop(0, nchunk, chunk, 0)

        @pl.when(b == batch - 1)
        def _():
            for t in range(CONV_K):
                gw_ref[t:t + 1, :] = jnp.sum(gacc[t], axis=0, keepdims=True)
            gw_ref[CONV_K:CONV_PAD, :] = jnp.zeros((CONV_PAD - CONV_K, CONV_TC), F32)

    du = SDS((batch, SEQ, D_MODEL), BF16)
    body, dep_spec, dep_arg = _anchored(body, 4, dep)
    return pl.pallas_call(
        body, grid=(nct, batch),
        in_specs=[pl.BlockSpec((1, SEQ, CONV_TC), lambda j, b: (b, 0, j)),
                  pl.BlockSpec((1, SEQ, CONV_TC), lambda j, b: (b, 0, j + nct)),
                  pl.BlockSpec((1, SEQ, CONV_TC), lambda j, b: (b, 0, j)),
                  pl.BlockSpec((CONV_PAD, CONV_TC), lambda j, b: (0, j))] + dep_spec,
        out_specs=[pl.BlockSpec((1, SEQ, CONV_TC), lambda j, b: (b, 0, j)),
                   pl.BlockSpec((1, SEQ, CONV_TC), lambda j, b: (b, 0, j)),
                   pl.BlockSpec((CONV_PAD, CONV_TC), lambda j, b: (0, j)),
                   pl.BlockSpec((1, CONV_TC), lambda j, b: (0, j))],
        out_shape=[du, du, SDS((CONV_PAD, D_MODEL), F32), SDS((1, D_MODEL), F32)],
        scratch_shapes=[pltpu.VMEM((SUBLANES, SEQ + CONV_PAD, CONV_TC), F32),
                        pltpu.VMEM((SUBLANES, SEQ + CONV_PAD, CONV_TC), F32),
                        pltpu.VMEM((CONV_K, 8, CONV_TC), F32)],
        compiler_params=_params("parallel", "arbitrary"), name="conv_bwd")(u, u, dc1, conv_w, *dep_arg)


MID_TM = 256


def _layernorm_stats(c1):
    mu = jnp.mean(c1, axis=-1, keepdims=True)
    cen = c1 - mu
    rs = lax.rsqrt(jnp.mean(cen * cen, axis=-1, keepdims=True) + LN_EPS)
    return cen * rs, rs


def _mid_fwd(att, c1, logits, x, w_a, w_c, w_o, gate_b, ln_g, ln_b, g2, dep=None):
    T = x.shape[0]
    tm = MID_TM

    def body(att_ref, c1_ref, lg_ref, x_ref, wa_ref, wc_ref, wo_ref, gb_ref, lng_ref, lnb_ref, g2_ref,
             c3_ref, ya_ref, yc_ref, mix_ref, x1_ref, h2_ref):
        ya = _dot(att_ref[...], wa_ref[...])
        xh, _ = _layernorm_stats(c1_ref[...])
        c2 = xh * lng_ref[...] + lnb_ref[...]
        c3 = (c2 * _sigmoid(c2)).astype(BF16)
        c3_ref[...] = c3
        yc = _dot(c3, wc_ref[...])
        gates = _sigmoid(lg_ref[...] + gb_ref[...])
        mix = (gates[:, :D_MODEL] * ya + gates[:, D_MODEL:] * yc).astype(BF16)
        ya_ref[...] = ya.astype(BF16)
        yc_ref[...] = yc.astype(BF16)
        mix_ref[...] = mix
        x1 = x_ref[...] + _dot(mix, wo_ref[...])
        x1_ref[...] = x1
        r = lax.rsqrt(jnp.mean(x1 * x1, axis=-1, keepdims=True) + RMS_EPS)
        h2_ref[...] = (x1 * r * g2_ref[...]).astype(BF16)

    row = lambda n: pl.BlockSpec((tm, n), lambda i: (i, 0))
    full = lambda a, b: pl.BlockSpec((a, b), lambda i: (0, 0))
    body, dep_spec, dep_arg = _anchored(body, 11, dep)
    return pl.pallas_call(
        body, grid=(T // tm,),
        in_specs=[row(ATTN_OUT), row(D_MODEL), row(2 * D_MODEL), row(D_MODEL),
                  full(ATTN_OUT, D_MODEL), full(D_MODEL, D_MODEL), full(D_MODEL, D_MODEL),
                  full(1, 2 * D_MODEL), full(1, D_MODEL), full(1, D_MODEL), full(1, D_MODEL)] + dep_spec,
        out_specs=[row(D_MODEL), row(D_MODEL), row(D_MODEL), row(D_MODEL), row(D_MODEL), row(D_MODEL)],
        out_shape=[SDS((T, D_MODEL), BF16), SDS((T, D_MODEL), BF16), SDS((T, D_MODEL), BF16), SDS((T, D_MODEL), BF16),
                   SDS((T, D_MODEL), F32), SDS((T, D_MODEL), BF16)],
        compiler_params=_params("parallel"), name="mid_fwd")(att, c1, logits, x, w_a, w_c, w_o, gate_b, ln_g, ln_b, g2,
                                                             *dep_arg)


def _mid_bwd(dx1b, ya, yc, logits, att, c1, w_a, w_c, w_o, gate_b, ln_g, ln_b, head_ones, dep=None):
    T = dx1b.shape[0]
    tm = MID_TM

    def body(dx_ref, ya_ref, yc_ref, lg_ref, att_ref, c1_ref, wa_ref, wc_ref, wo_ref, gb_ref, lng_ref, lnb_ref, e_ref,
             dlg_ref, dya_ref, dyc_ref, datt_ref, dsum_ref, dc1_ref, ggb_ref, glg_ref, glb_ref):
        @pl.when(pl.program_id(0) == 0)
        def _():
            ggb_ref[...] = jnp.zeros_like(ggb_ref)
            glg_ref[...] = jnp.zeros_like(glg_ref)
            glb_ref[...] = jnp.zeros_like(glb_ref)

        dmix = _dot_nt(dx_ref[...], wo_ref[...])
        gates = _sigmoid(lg_ref[...] + gb_ref[...])
        ga, gc = gates[:, :D_MODEL], gates[:, D_MODEL:]
        dla = dmix * ya_ref[...].astype(F32) * ga * (1.0 - ga)
        dlc = dmix * yc_ref[...].astype(F32) * gc * (1.0 - gc)
        dlg_ref[:, :D_MODEL] = dla.astype(BF16)
        dlg_ref[:, D_MODEL:] = dlc.astype(BF16)
        ggb_ref[:, :D_MODEL] += _rowsum(dla)
        ggb_ref[:, D_MODEL:] += _rowsum(dlc)
        dya = (dmix * ga).astype(BF16)
        dyc = (dmix * gc).astype(BF16)
        dya_ref[...] = dya
        dyc_ref[...] = dyc
        datt = _dot_nt(dya, wa_ref[...])
        datt_ref[...] = datt
        dsum_ref[...] = jnp.dot(datt * att_ref[...].astype(F32), e_ref[...], preferred_element_type=F32,
                                precision=lax.Precision.HIGHEST)
        dc3 = _dot_nt(dyc, wc_ref[...])
        xh, rs = _layernorm_stats(c1_ref[...])
        c2 = xh * lng_ref[...] + lnb_ref[...]
        sg = _sigmoid(c2)
        dc2 = dc3 * (sg * (1.0 + c2 * (1.0 - sg)))
        glg_ref[...] += _rowsum(dc2 * xh)
        glb_ref[...] += _rowsum(dc2)
        dxh = dc2 * lng_ref[...]
        dc1_ref[...] = rs * (dxh - jnp.mean(dxh, axis=-1, keepdims=True) - xh * jnp.mean(dxh * xh, axis=-1, keepdims=True))

    row = lambda n: pl.BlockSpec((tm, n), lambda i: (i, 0))
    full = lambda a, b: pl.BlockSpec((a, b), lambda i: (0, 0))
    body, dep_spec, dep_arg = _anchored(body, 13, dep)
    return pl.pallas_call(
        body, grid=(T // tm,),
        in_specs=[row(D_MODEL), row(D_MODEL), row(D_MODEL), row(2 * D_MODEL), row(ATTN_OUT), row(D_MODEL),
                  full(ATTN_OUT, D_MODEL), full(D_MODEL, D_MODEL), full(D_MODEL, D_MODEL),
                  full(1, 2 * D_MODEL), full(1, D_MODEL), full(1, D_MODEL), full(ATTN_OUT, ATTN_OUT)] + dep_spec,
        out_specs=[row(2 * D_MODEL), row(D_MODEL), row(D_MODEL), row(ATTN_OUT), row(ATTN_OUT), row(D_MODEL),
                   full(1, 2 * D_MODEL), full(1, D_MODEL), full(1, D_MODEL)],
        out_shape=[SDS((T, 2 * D_MODEL), BF16), SDS((T, D_MODEL), BF16), SDS((T, D_MODEL), BF16), SDS((T, ATTN_OUT), F32),
                   SDS((T, ATTN_OUT), F32), SDS((T, D_MODEL), F32),
                   SDS((1, 2 * D_MODEL), F32), SDS((1, D_MODEL), F32), SDS((1, D_MODEL), F32)],
        compiler_params=_params("arbitrary"), name="mid_bwd")(dx1b, ya, yc, logits, att, c1, w_a, w_c, w_o, gate_b, ln_g, ln_b,
                                                               head_ones, *dep_arg)


FFN_TM = 512
FFN_TF = D_FF // 2


def _rms_bwd(dy_times_g, xh, r):
    return r * (dy_times_g - xh * jnp.mean(dy_times_g * xh, axis=-1, keepdims=True))


def _ffn_fwd(h2, x1, target, gf, w_g, w_u, w_d):
    T = h2.shape[0]
    tm, tf = FFN_TM, FFN_TF
    nf = D_FF // tf

    def body(h_ref, x1_ref, t_ref, gf_ref, wg_ref, wu_ref, wd_ref,
             a_ref, b_ref, f_ref, dx2_ref, dx2b_ref, loss_ref, gnf_ref, acc):
        i, j = pl.program_id(0), pl.program_id(1)
        h = h_ref[...]
        a = _dot(h, wg_ref[...])
        b = _dot(h, wu_ref[...])
        f = (a * _sigmoid(a) * b).astype(BF16)
        a_ref[...] = a.astype(BF16)
        b_ref[...] = b.astype(BF16)
        f_ref[...] = f
        p = _dot(f, wd_ref[...])

        @pl.when(j == 0)
        def _():
            acc[...] = x1_ref[...] + p

        @pl.when(j > 0)
        def _():
            acc[...] += p

        @pl.when((i == 0) & (j == nf - 1))
        def _():
            loss_ref[...] = jnp.zeros_like(loss_ref)
            gnf_ref[...] = jnp.zeros_like(gnf_ref)

        @pl.when(j == nf - 1)
        def _():
            x2 = acc[...]
            r = lax.rsqrt(jnp.mean(x2 * x2, axis=-1, keepdims=True) + RMS_EPS)
            xh = x2 * r
            err = xh * gf_ref[...] - t_ref[...]
            loss_ref[...] += (0.5 / D_MODEL) * jnp.sum(err * err)
            dy = err * (1.0 / D_MODEL)
            gnf_ref[...] += _rowsum(dy * xh)
            dx2 = _rms_bwd(dy * gf_ref[...], xh, r)
            dx2_ref[...] = dx2
            dx2b_ref[...] = dx2.astype(BF16)

    row = lambda n: pl.BlockSpec((tm, n), lambda i, j: (i, 0))
    ffb = pl.BlockSpec((tm, tf), lambda i, j: (i, j))
    return pl.pallas_call(
        body, grid=(T // tm, nf),
        in_specs=[row(D_MODEL), row(D_MODEL), row(D_MODEL), pl.BlockSpec((1, D_MODEL), lambda i, j: (0, 0)),
                  pl.BlockSpec((D_MODEL, tf), lambda i, j: (0, j)), pl.BlockSpec((D_MODEL, tf), lambda i, j: (0, j)),
                  pl.BlockSpec((tf, D_MODEL), lambda i, j: (j, 0))],
        out_specs=[ffb, ffb, ffb, row(D_MODEL), row(D_MODEL),
                   pl.BlockSpec((1, 128), lambda i, j: (0, 0)), pl.BlockSpec((1, D_MODEL), lambda i, j: (0, 0))],
        out_shape=[SDS((T, D_FF), BF16), SDS((T, D_FF), BF16), SDS((T, D_FF), BF16), SDS((T, D_MODEL), F32),
                   SDS((T, D_MODEL), BF16), SDS((1, 128), F32), SDS((1, D_MODEL), F32)],
        scratch_shapes=[pltpu.VMEM((tm, D_MODEL), F32)],
        compiler_params=_params("arbitrary", "arbitrary"), name="ffn_fwd")(h2, x1, target, gf, w_g, w_u, w_d)


def _ffn_bwd(dx2b, dx2, a, b, x1, g2, w_g, w_u, w_d):
    T = dx2.shape[0]
    tm, tf = FFN_TM, FFN_TF
    nf = D_FF // tf

    def body(dxb_ref, dx2_ref, a_ref, b_ref, x1_ref, g2_ref, wg_ref, wu_ref, wd_ref,
             da_ref, db_ref, dx1_ref, dx1b_ref, gn2_ref, acc):
        i, j = pl.program_id(0), pl.program_id(1)
        df = _dot_nt(dxb_ref[...], wd_ref[...])
        av = a_ref[...].astype(F32)
        bv = b_ref[...].astype(F32)
        sg = _sigmoid(av)
        db = (df * av * sg).astype(BF16)
        da = (df * bv * (sg * (1.0 + av * (1.0 - sg)))).astype(BF16)
        da_ref[...] = da
        db_ref[...] = db
        p = _dot_nt(da, wg_ref[...]) + _dot_nt(db, wu_ref[...])

        @pl.when(j == 0)
        def _():
            acc[...] = p

        @pl.when(j > 0)
        def _():
            acc[...] += p

        @pl.when((i == 0) & (j == nf - 1))
        def _():
            gn2_ref[...] = jnp.zeros_like(gn2_ref)

        @pl.when(j == nf - 1)
        def _():
            dh2 = acc[...]
            x1 = x1_ref[...]
            r = lax.rsqrt(jnp.mean(x1 * x1, axis=-1, keepdims=True) + RMS_EPS)
            xh = x1 * r
            gn2_ref[...] += _rowsum(dh2 * xh)
            dx1 = dx2_ref[...] + _rms_bwd(dh2 * g2_ref[...], xh, r)
            dx1_ref[...] = dx1
            dx1b_ref[...] = dx1.astype(BF16)

    row = lambda n: pl.BlockSpec((tm, n), lambda i, j: (i, 0))
    ffb = pl.BlockSpec((tm, tf), lambda i, j: (i, j))
    return pl.pallas_call(
        body, grid=(T // tm, nf),
        in_specs=[row(D_MODEL), row(D_MODEL), ffb, ffb, row(D_MODEL), pl.BlockSpec((1, D_MODEL), lambda i, j: (0, 0)),
                  pl.BlockSpec((D_MODEL, tf), lambda i, j: (0, j)), pl.BlockSpec((D_MODEL, tf), lambda i, j: (0, j)),
                  pl.BlockSpec((tf, D_MODEL), lambda i, j: (j, 0))],
        out_specs=[ffb, ffb, row(D_MODEL), row(D_MODEL), pl.BlockSpec((1, D_MODEL), lambda i, j: (0, 0))],
        out_shape=[SDS((T, D_FF), BF16), SDS((T, D_FF), BF16), SDS((T, D_MODEL), F32), SDS((T, D_MODEL), BF16),
                   SDS((1, D_MODEL), F32)],
        scratch_shapes=[pltpu.VMEM((tm, D_MODEL), F32)],
        compiler_params=_params("arbitrary", "arbitrary"), name="ffn_bwd")(dx2b, dx2, a, b, x1, g2, w_g, w_u, w_d)


def _in_bwd(pieces, w_in, x, dx1, g1, dep=None):
    T = x.shape[0]
    tm = IN_TM
    npc = len(pieces)
    assert sum(p.shape[1] for p in pieces) == IN_WIDTH

    def body(*refs):
        p_refs = refs[:npc]
        w_hbm, x_ref, dx1_ref, g_ref, dx_ref, gn1_ref, w_vmem, sem = refs[npc:]

        @pl.when(pl.program_id(0) == 0)
        def _():
            cp = pltpu.make_async_copy(w_hbm, w_vmem, sem)
            cp.start()
            cp.wait()
            gn1_ref[...] = jnp.zeros_like(gn1_ref)

        dh = jnp.zeros((tm, D_MODEL), F32)
        col = 0
        for p_ref in p_refs:
            for j in range(p_ref.shape[1] // IN_CHUNK):
                dh = dh + _dot_nt(p_ref[:, j * IN_CHUNK:(j + 1) * IN_CHUNK], w_vmem[:, col:col + IN_CHUNK])
                col += IN_CHUNK
        xv = x_ref[...]
        r = lax.rsqrt(jnp.mean(xv * xv, axis=-1, keepdims=True) + RMS_EPS)
        xh = xv * r
        gn1_ref[...] += _rowsum(dh * xh)
        dx_ref[...] = dx1_ref[...] + _rms_bwd(dh * g_ref[...], xh, r)

    row = lambda n: pl.BlockSpec((tm, n), lambda i: (i, 0))
    body, dep_spec, dep_arg = _anchored(body, npc + 4, dep)
    return pl.pallas_call(
        body, grid=(T // tm,),
        in_specs=[row(p.shape[1]) for p in pieces]
        + [pl.BlockSpec(memory_space=pl.ANY), row(D_MODEL), row(D_MODEL), pl.BlockSpec((1, D_MODEL), lambda i: (0, 0))]
        + dep_spec,
        out_specs=[row(D_MODEL), pl.BlockSpec((1, D_MODEL), lambda i: (0, 0))],
        out_shape=[SDS((T, D_MODEL), F32), SDS((1, D_MODEL), F32)],
        scratch_shapes=[pltpu.VMEM((D_MODEL, IN_WIDTH), BF16), pltpu.SemaphoreType.DMA],
        compiler_params=_params("arbitrary"), name="in_bwd")(*pieces, w_in, x, dx1, g1, *dep_arg)


def _local_step(x, target, w, small, dep=None, late_weights=None, emit=None):
    T = x.shape[0]
    batch = T // SEQ
    slopes_r = jnp.asarray(_slopes_times_dilation())
    emit = emit or (lambda names, grads: None)

    h, qkv, u, logits = _in_proj(x, small["norm1_g"], w["w_in"], dep)

    qkv3 = qkv.reshape(batch, SEQ, 3 * ATTN_WIDTH)
    att, lse = _attn_fwd(qkv3, slopes_r, batch)
    att = att.reshape(T, ATTN_OUT)

    u3 = u.reshape(batch, SEQ, 2 * D_MODEL)
    c1 = _conv_fwd(u3, w["conv_w"], small["conv_b"], batch).reshape(T, D_MODEL)
    if late_weights is not None:
        w = {**w, **late_weights(LATE_MERGE, c1)}

    c3, ya, yc, mix, x1, h2 = _mid_fwd(
        att, c1, logits, x, w["w_attn_out"], w["w_conv_out"], w["w_o"],
        small["gate_b"], small["conv_ln_g"], small["conv_ln_b"], small["norm2_g"], w.get("token"))
    if late_weights is not None:
        w = {**w, **late_weights(LATE_FFN, h2)}

    a, b, f, dx2, dx2b, loss, g_normf = _ffn_fwd(h2, x1, target, small["norm_f_g"],
                                                   w["w_ffn_gate"], w["w_ffn_up"], w["w_ffn_down"])

    da, db, dx1, dx1b, g_norm2 = _ffn_bwd(dx2b, dx2, a, b, x1, small["norm2_g"],
                                           w["w_ffn_gate"], w["w_ffn_up"], w["w_ffn_down"])
    gw = {}
    gw["w_ffn_down"] = _mm_tn(f, dx2b, BF16, "gw_ffn_down", tn=512)
    gw["w_ffn_gate"] = _mm_tn(h2, da, BF16, "gw_ffn_gate", tn=1408)
    gw["w_ffn_up"] = _mm_tn(h2, db, BF16, "gw_ffn_up", tn=1408)
    token = emit(("w_ffn_gate", "w_ffn_up", "w_ffn_down"), gw)

    head_ones = jnp.asarray(np.kron(np.eye(HEADS_PER_GROUP, dtype=np.float32), np.ones((HEAD_DIM, HEAD_DIM), np.float32)))
    dlogits, dya, dyc, datt, dsum, dc1, g_gate_b, g_ln_g, g_ln_b = _mid_bwd(
        dx1b, ya, yc, logits, att, c1, w["w_attn_out"], w["w_conv_out"], w["w_o"],
        small["gate_b"], small["conv_ln_g"], small["conv_ln_b"], head_ones, token)
    gw["w_o"] = _mm_tn(mix, dx1b, BF16, "gw_o", tn=512)
    gw["w_attn_out"] = _mm_tn(att, dya, BF16, "gw_attn_out", tn=512)
    gw["w_conv_out"] = _mm_tn(c3, dyc, BF16, "gw_conv_out", tn=512)
    token = emit(("w_conv_out", "w_attn_out", "w_o"), gw)

    dua, dub, g_conv_w, g_conv_b = _conv_bwd(u3, dc1.reshape(batch, SEQ, D_MODEL), w["conv_w"], batch, token)

    dq, dk, dv = _attn_bwd(qkv3, datt.reshape(batch, SEQ, ATTN_OUT), lse, dsum.reshape(batch, SEQ, ATTN_OUT),
                           slopes_r, batch)
    pieces = [dq.reshape(T, ATTN_WIDTH), dk.reshape(T, ATTN_WIDTH), dv.reshape(T, ATTN_WIDTH),
              dua.reshape(T, D_MODEL), dub.reshape(T, D_MODEL), dlogits]

    names = ("q", "k", "v", "ua", "ub", "gate")
    gw["w_in"] = jnp.concatenate(
        [_mm_tn(h, p, BF16, "gw_in_" + nm, tn=min(p.shape[1], 1024) if p.shape[1] != ATTN_WIDTH else 768)
         for nm, p in zip(names, pieces)], axis=1)
    gw["conv_w"] = g_conv_w
    token = emit(("w_in", "conv_w"), gw)
    grad_x, g_norm1 = _in_bwd(pieces, w["w_in"], x, dx1, small["norm1_g"], token)

    gsmall = {"norm1_g": g_norm1, "gate_b": g_gate_b, "conv_b": g_conv_b, "conv_ln_g": g_ln_g, "conv_ln_b": g_ln_b,
              "norm2_g": g_norm2, "norm_f_g": g_normf}
    return loss, grad_x, gw, gsmall


ANY = pl.BlockSpec(memory_space=pl.ANY)


def _all_gather(arrs):
    n = len(arrs)

    def body(*refs):
        ins, outs = refs[:n], refs[n:2 * n]
        send_sems, recv_sems, local_sems = refs[2 * n:]
        x, y, c = lax.axis_index("x"), lax.axis_index("y"), lax.axis_index("c")
        me, sibling = (x, y, c), (x, y, 1 - c)
        chips = [(1 - x, y), (x, 1 - y), (1 - x, 1 - y)]

        def copy(a, k, block, to, src=None):
            px, py, pc = block
            dst = outs[a].at[4 * px + 2 * py + pc]
            return pltpu.make_async_remote_copy(
                src_ref=dst if src is None else src, dst_ref=dst,
                send_sem=send_sems.at[a, k], recv_sem=recv_sems.at[a, k], device_id=to, device_id_type=MESH)

        mine = [pltpu.make_async_copy(ins[a], outs[a].at[4 * x + 2 * y + c], local_sems.at[a]) for a in range(n)]
        for cp in mine:
            cp.start()
        first = []
        for j, chip in enumerate(chips):
            first += [copy(a, 1 + j, me, (*chip, c), src=ins[a]) for a in range(n)]
        first += [copy(a, 0, me, sibling, src=ins[a]) for a in range(n)]
        for cp in first:
            cp.start()
        passed = []
        for j, chip in enumerate(chips):
            for a in range(n):
                copy(a, 1 + j, (*chip, c), me).wait_recv()
                cp = copy(a, 4 + j, (*chip, c), sibling)
                cp.start()
                passed.append(cp)
        for a in range(n):
            copy(a, 0, sibling, me).wait_recv()
        for j, chip in enumerate(chips):
            for a in range(n):
                copy(a, 4 + j, (*chip, 1 - c), me).wait_recv()
        for cp in first + passed:
            cp.wait_send()
        for cp in mine:
            cp.wait()

    return pl.pallas_call(
        body, in_specs=[ANY] * n, out_specs=[ANY] * n,
        out_shape=[SDS((N_DEV,) + a.shape, a.dtype) for a in arrs],
        scratch_shapes=[pltpu.SemaphoreType.DMA((n, 7)), pltpu.SemaphoreType.DMA((n, 7)), pltpu.SemaphoreType.DMA((n,))],
        name="all_gather_weights")(*arrs)


HBM = pl.BlockSpec(memory_space=pltpu.HBM)
SEM = pl.BlockSpec(memory_space=pltpu.SEMAPHORE)
ALL_PEERS = tuple(range(1, N_DEV))
OTHER_CHIPS = (2, 4, 6)
SPLIT_EFFECT = pltpu.CompilerParams(has_side_effects=pltpu.SideEffectType.DATAFLOW_SIDE_EFFECTING)


def _exchange_copies(mode, ks, srcs, lands, send_sems, recv_sems):
    x, y, c = lax.axis_index("x"), lax.axis_index("y"), lax.axis_index("c")
    me = 4 * x + 2 * y + c
    send, recv = [], []
    for a in range(len(lands)):
        for i, k in enumerate(ks):
            peer = (x ^ ((k >> 2) & 1), y ^ ((k >> 1) & 1), c ^ (k & 1))
            pidx = 4 * peer[0] + 2 * peer[1] + peer[2]
            if mode == "gather":
                src, to, out_slot, in_slot = srcs[a], peer, me, pidx
            elif mode == "scatter":
                src, to, out_slot, in_slot = srcs[a].at[pidx], peer, me, pidx
            elif mode == "chip_scatter":
                src, to, out_slot, in_slot = srcs[a].at[pidx >> 1], peer, me >> 1, pidx >> 1
            else:
                src, to, out_slot, in_slot = lands[a].at[pidx], (x, y, 1 - c), pidx, pidx ^ 1
            s = a * len(ks) + i
            send.append(pltpu.make_async_remote_copy(
                src_ref=src, dst_ref=lands[a].at[out_slot], send_sem=send_sems.at[s], recv_sem=recv_sems.at[s],
                device_id=to, device_id_type=MESH))
            recv.append(pltpu.make_async_remote_copy(
                src_ref=src, dst_ref=lands[a].at[in_slot], send_sem=send_sems.at[s], recv_sem=recv_sems.at[s],
                device_id=to, device_id_type=MESH))
    return send, recv


def _send_start(mode, ks, name, srcs=(), lands=None):
    srcs = list(srcs)
    if lands is None:
        slots = 4 if mode == "chip_scatter" else N_DEV
        lands = [lax.empty((slots,) + (s.shape if mode == "gather" else s.shape[1:]), s.dtype) for s in srcs]
    ns, nl = len(srcs), len(lands)
    nsem = nl * len(ks)

    def body(*refs):
        send, _ = _exchange_copies(mode, ks, refs[:ns], refs[ns:ns + nl], refs[ns + nl], refs[ns + nl + 1])
        for cp in send:
            cp.start()
        token = refs[-1]
        token[...] = jnp.zeros_like(token)

    both = srcs + list(lands)
    res = pl.pallas_call(
        body, name=name,
        out_shape=(pltpu.SemaphoreType.DMA((nsem,)), pltpu.SemaphoreType.DMA((nsem,)),
                   *[pltpu.HBM(a.shape, a.dtype) for a in both], SDS((8, 128), F32)),
        in_specs=[HBM] * (ns + nl), out_specs=(SEM, SEM, *([HBM] * (ns + nl)), pl.BlockSpec(memory_space=pltpu.VMEM)),
        input_output_aliases={i: 2 + i for i in range(ns + nl)}, compiler_params=SPLIT_EFFECT,
    )(*[pltpu.with_memory_space_constraint(a, pltpu.HBM) for a in both])
    return dict(mode=mode, ks=ks, send_sems=res[0], recv_sems=res[1], srcs=res[2:2 + ns], lands=res[2 + ns:2 + ns + nl],
                token=res[-1])


def _send_wait(started, after, name):
    ns, nl = len(started["srcs"]), len(started["lands"])

    def body(*refs):
        send, recv = _exchange_copies(started["mode"], started["ks"], refs[:ns], refs[ns:ns + nl],
                                      refs[ns + nl], refs[ns + nl + 1])
        for cp in send:
            cp.wait_send()
        for cp in recv:
            cp.wait_recv()

    both = list(started["srcs"]) + list(started["lands"])
    res = pl.pallas_call(
        body, name=name,
        out_shape=tuple(pltpu.HBM(a.shape, a.dtype) for a in both),
        in_specs=[HBM] * (ns + nl) + [SEM, SEM, ANY], out_specs=tuple([HBM] * (ns + nl)),
        input_output_aliases={i: i for i in range(ns + nl)}, compiler_params=SPLIT_EFFECT,
    )(*both, started["send_sems"], started["recv_sems"], after)
    return res[:ns], res[ns:]


def _exchange_sibling(gs):
    n = len(gs)

    def body(*refs):
        ins, outs = refs[:n], refs[n:2 * n]
        send_sems, recv_sems = refs[2 * n:]
        x, y, c = lax.axis_index("x"), lax.axis_index("y"), lax.axis_index("c")
        copies = []
        for a in range(n):
            for j in range(4):
                copies.append(pltpu.make_async_remote_copy(
                    src_ref=ins[a].at[2 * j + (1 - c)], dst_ref=outs[a].at[j],
                    send_sem=send_sems.at[a, j], recv_sem=recv_sems.at[a, j],
                    device_id=(x, y, 1 - c), device_id_type=MESH))
        for cp in copies:
            cp.start()
        for cp in copies:
            cp.wait_recv()
        for cp in copies:
            cp.wait_send()

    return pl.pallas_call(
        body, in_specs=[ANY] * n, out_specs=[ANY] * n,
        out_shape=[SDS((4,) + g.shape[1:], g.dtype) for g in gs],
        scratch_shapes=[pltpu.SemaphoreType.DMA((n, 4)), pltpu.SemaphoreType.DMA((n, 4))],
        name="reduce_scatter_sibling")(*gs)


def _add_pair(g, r1, core, name):
    _, rows, cols = g.shape
    tr = _row_tile(rows, cols, 3 * g.dtype.itemsize)

    def body(c_ref, g_ref, r_ref, o_ref):
        o_ref[...] = (g_ref[...].astype(F32) + r_ref[...].astype(F32)).astype(o_ref.dtype)

    return pl.pallas_call(
        body,
        grid_spec=pltpu.PrefetchScalarGridSpec(
            num_scalar_prefetch=1, grid=(4, rows // tr),
            in_specs=[pl.BlockSpec((1, tr, cols), lambda j, i, c_ref: (2 * j + c_ref[0], i, 0)),
                      pl.BlockSpec((1, tr, cols), lambda j, i, c_ref: (j, i, 0))],
            out_specs=pl.BlockSpec((1, tr, cols), lambda j, i, c_ref: (j, i, 0))),
        out_shape=SDS((4, rows, cols), g.dtype),
        compiler_params=_params("parallel", "parallel"), name=name)(core, g, r1)


def _row_tile(rows, cols, itemsize_total):
    budget = (4 << 20) // max(1, cols * itemsize_total)
    if rows <= budget:
        return rows
    t = rows
    while t > budget and t % 2 == 0 and (t // 2) % 16 == 0:
        t //= 2
    return t


def _adam_math(g, w, m, v):
    m_new = ADAM_B1 * m + (1.0 - ADAM_B1) * g
    v_new = ADAM_B2 * v + (1.0 - ADAM_B2) * (g * g)
    m_hat = m_new / (1.0 - ADAM_B1 ** ADAM_STEP)
    v_hat = v_new / (1.0 - ADAM_B2 ** ADAM_STEP)
    delta = -ADAM_LR * (m_hat / (jnp.sqrt(v_hat) + ADAM_EPS) + ADAM_WD * w)
    return delta, m_new, v_new


def _sum_adam(parts, w, m, v, name):
    rows, cols = w.shape
    nparts = parts.shape[0]
    tr = _row_tile(rows, cols, nparts * parts.dtype.itemsize + 7 * 4)

    def body(p_ref, w_ref, m_ref, v_ref, g_ref, d_ref, mo_ref, vo_ref):
        g = p_ref[0].astype(F32)
        for s in range(1, nparts):
            g = g + p_ref[s].astype(F32)
        delta, m_new, v_new = _adam_math(g, w_ref[...], m_ref[...], v_ref[...])
        g_ref[...] = g
        d_ref[...] = delta
        mo_ref[...] = m_new
        vo_ref[...] = v_new

    blk = pl.BlockSpec((tr, cols), lambda i: (i, 0))
    out = SDS((rows, cols), F32)
    return pl.pallas_call(
        body, grid=(rows // tr,),
        in_specs=[pl.BlockSpec((nparts, tr, cols), lambda i: (0, i, 0)), blk, blk, blk],
        out_specs=[blk, blk, blk, blk], out_shape=[out, out, out, out],
        compiler_params=_params("parallel"), name=name)(parts, w, m, v)


SMALL_ROWS = 72


def _small_allreduce_adam(gpart, w, m, v, dep=None):
    def body(g_ref, w_ref, m_ref, v_ref, go_ref, d_ref, mo_ref, vo_ref, gath, send_sems, recv_sems):
        x, y, c = lax.axis_index("x"), lax.axis_index("y"), lax.axis_index("c")
        me = 4 * x + 2 * y + c
        gath[me] = g_ref[...]
        copies = []
        for k in range(1, N_DEV):
            fx, fy, fc = (k >> 2) & 1, (k >> 1) & 1, k & 1
            peer = (x ^ fx, y ^ fy, c ^ fc)
            copies.append(pltpu.make_async_remote_copy(
                src_ref=gath.at[me], dst_ref=gath.at[me], send_sem=send_sems.at[k - 1], recv_sem=recv_sems.at[k - 1],
                device_id=peer, device_id_type=MESH))
        for cp in copies:
            cp.start()
        for cp in copies:
            cp.wait_recv()
        for cp in copies:
            cp.wait_send()
        g = gath[0]
        for d in range(1, N_DEV):
            g = g + gath[d]
        delta, m_new, v_new = _adam_math(g, w_ref[...], m_ref[...], v_ref[...])
        go_ref[...] = g
        d_ref[...] = delta
        mo_ref[...] = m_new
        vo_ref[...] = v_new

    vm = pl.BlockSpec(memory_space=pltpu.VMEM)
    out = SDS((SMALL_ROWS, 128), F32)
    body, dep_spec, dep_arg = _anchored(body, 4, dep)
    return pl.pallas_call(
        body, in_specs=[vm] * 4 + dep_spec, out_specs=[vm] * 4, out_shape=[out] * 4,
        scratch_shapes=[pltpu.VMEM((N_DEV, SMALL_ROWS, 128), F32), pltpu.SemaphoreType.DMA((N_DEV - 1,)),
                        pltpu.SemaphoreType.DMA((N_DEV - 1,))],
        name="small_allreduce_adam")(gpart, w, m, v, *dep_arg)


BIG = ("w_in", "conv_w", "w_conv_out", "w_attn_out", "w_o", "w_ffn_gate", "w_ffn_up", "w_ffn_down")
EARLY = ("w_in", "conv_w")
LATE_MERGE = ("w_conv_out", "w_attn_out", "w_o")
LATE_FFN = ("w_ffn_gate", "w_ffn_up", "w_ffn_down")
COL_SHARDED = ("w_in", "conv_w", "w_attn_out", "w_ffn_gate", "w_ffn_up")
SMALL = ("norm1_g", "gate_b", "conv_b", "conv_ln_g", "conv_ln_b", "norm2_g", "norm_f_g")
WEIGHTS = ("norm1_g", "w_in", "gate_b", "conv_w", "conv_b", "conv_ln_g", "conv_ln_b", "w_conv_out", "w_attn_out", "w_o",
           "norm2_g", "w_ffn_gate", "w_ffn_up", "w_ffn_down", "norm_f_g")


def _shard2d(name, a):
    a = a.reshape(a.shape[-2], a.shape[-1])
    if name == "conv_w":
        a = jnp.pad(a, ((0, CONV_PAD - CONV_K), (0, 0)))
    return a


def _gathered_to_full(name, g):
    if name in COL_SHARDED:
        return g.transpose(1, 0, 2).reshape(g.shape[1], N_DEV * g.shape[2])
    return g.reshape(N_DEV * g.shape[1], g.shape[2])


def _full_to_blocks(name, g):
    if name in COL_SHARDED:
        return g.reshape(g.shape[0], N_DEV, g.shape[1] // N_DEV).transpose(1, 0, 2)
    return g.reshape(N_DEV, g.shape[0] // N_DEV, g.shape[1])


def _pack_small(d, last_rows):
    vec = jnp.concatenate([d[n].reshape(-1) for n in SMALL]).reshape(SMALL_ROWS - SUBLANES, 128)
    return jnp.concatenate([vec, last_rows], axis=0)


def _unpack_small(p, like):
    flat = p.reshape(-1)
    out, off = {}, 0
    for n in SMALL:
        size = like[n].size
        out[n] = flat[off:off + size].reshape(like[n].shape)
        off += size
    return out


def kernel(x, norm1_g, w_in, gate_b, conv_w, conv_b, conv_ln_g, conv_ln_b, w_conv_out, w_attn_out, w_o, norm2_g, w_ffn_gate, w_ffn_up, w_ffn_down, norm_f_g, loss_target, m_norm1_g, m_w_in, m_gate_b, m_conv_w, m_conv_b, m_conv_ln_g, m_conv_ln_b, m_w_conv_out, m_w_attn_out, m_w_o, m_norm2_g, m_w_ffn_gate, m_w_ffn_up, m_w_ffn_down, m_norm_f_g, v_norm1_g, v_w_in, v_gate_b, v_conv_w, v_conv_b, v_conv_ln_g, v_conv_ln_b, v_w_conv_out, v_w_attn_out, v_w_o, v_norm2_g, v_w_ffn_gate, v_w_ffn_up, v_w_ffn_down, v_norm_f_g):
    wts = dict(norm1_g=norm1_g, w_in=w_in, gate_b=gate_b, conv_w=conv_w, conv_b=conv_b, conv_ln_g=conv_ln_g,
               conv_ln_b=conv_ln_b, w_conv_out=w_conv_out, w_attn_out=w_attn_out, w_o=w_o, norm2_g=norm2_g,
               w_ffn_gate=w_ffn_gate, w_ffn_up=w_ffn_up, w_ffn_down=w_ffn_down, norm_f_g=norm_f_g)
    mom1 = dict(norm1_g=m_norm1_g, w_in=m_w_in, gate_b=m_gate_b, conv_w=m_conv_w, conv_b=m_conv_b, conv_ln_g=m_conv_ln_g,
                conv_ln_b=m_conv_ln_b, w_conv_out=m_w_conv_out, w_attn_out=m_w_attn_out, w_o=m_w_o, norm2_g=m_norm2_g,
                w_ffn_gate=m_w_ffn_gate, w_ffn_up=m_w_ffn_up, w_ffn_down=m_w_ffn_down, norm_f_g=m_norm_f_g)
    mom2 = dict(norm1_g=v_norm1_g, w_in=v_w_in, gate_b=v_gate_b, conv_w=v_conv_w, conv_b=v_conv_b, conv_ln_g=v_conv_ln_g,
                conv_ln_b=v_conv_ln_b, w_conv_out=v_w_conv_out, w_attn_out=v_w_attn_out, w_o=v_w_o, norm2_g=v_norm2_g,
                w_ffn_gate=v_w_ffn_gate, w_ffn_up=v_w_ffn_up, w_ffn_down=v_w_ffn_down, norm_f_g=v_norm_f_g)

    T = x.shape[0] * x.shape[1]
    x2 = x.reshape(T, D_MODEL)
    t2 = loss_target.reshape(T, D_MODEL)

    me = 4 * lax.axis_index("x") + 2 * lax.axis_index("y") + lax.axis_index("c")
    shards = {n: _shard2d(n, wts[n]) for n in BIG}
    sent = {n: shards[n] if n == "conv_w" else shards[n].astype(BF16) for n in BIG}
    small = {n: wts[n].reshape(1, -1) for n in SMALL}

    gathered = _all_gather([sent[n] for n in EARLY])
    full = {n: _gathered_to_full(n, g) for n, g in zip(EARLY, gathered)}
    merge_gather = _send_start("gather", ALL_PEERS, "gather_start_merge", [sent[n] for n in LATE_MERGE])
    ffn_gather = _send_start("gather", (1,) + OTHER_CHIPS, "gather_start_ffn", [sent[n] for n in LATE_FFN])
    ffn_state = {}

    def filled(names, srcs, lands):
        return {n: _gathered_to_full(n, lax.dynamic_update_slice(land, src[None], (me, 0, 0)))
                for n, src, land in zip(names, srcs, lands)}

    def late_weights(names, after):
        if names is LATE_MERGE:
            srcs, lands = _send_wait(merge_gather, after, "gather_wait_merge")
            ffn_state["srcs"], ffn_lands = _send_wait(ffn_gather, after, "gather_wait_ffn")
            ffn_state["forward"] = _send_start("forward", OTHER_CHIPS, "forward_start_ffn", lands=ffn_lands)
            return {**filled(names, srcs, lands), "token": ffn_state["forward"]["token"]}
        _, lands = _send_wait(ffn_state["forward"], after, "forward_wait_ffn")
        return filled(names, ffn_state["srcs"], lands)

    scatters = []
    core = lax.axis_index("c").astype(jnp.int32).reshape(1)

    def emit(names, gw):
        blocks = [_full_to_blocks(n, gw[n]) for n in names]
        if "w_in" in names:
            sums = [_add_pair(g, r, core, "chip_sum_" + n) for n, g, r in zip(names, blocks, _exchange_sibling(blocks))]
            started = _send_start("chip_scatter", OTHER_CHIPS, "scatter_start_" + names[0], sums)
        else:
            started = _send_start("scatter", ALL_PEERS, "scatter_start_" + names[0], blocks)
        scatters.append((names, started))
        return started["token"]

    loss_part, grad_x, gw, gsmall = _local_step(x2, t2, full, small, ffn_gather["token"], late_weights, emit)

    grads, deltas, new_m, new_v = {}, {}, {}, {}
    after = grad_x
    for names, started in scatters:
        srcs, lands = _send_wait(started, after, "scatter_wait_" + names[0])
        mine = me >> 1 if started["mode"] == "chip_scatter" else me
        for n, src, land in zip(names, srcs, lands):
            parts = lax.dynamic_update_slice(land, lax.dynamic_slice_in_dim(src, mine, 1, axis=0), (mine, 0, 0))
            g, d, mo, vo = _sum_adam(parts, shards[n], _shard2d(n, mom1[n]), _shard2d(n, mom2[n]), "adam_" + n)
            for dst, val in ((grads, g), (deltas, d), (new_m, mo), (new_v, vo)):
                if n == "conv_w":
                    val = val[:CONV_K]
                dst[n] = val.reshape(wts[n].shape)
            after = g

    zeros, ones = jnp.zeros((SUBLANES, 128), F32), jnp.ones((SUBLANES, 128), F32)
    sg, sd, sm, sv = _small_allreduce_adam(
        _pack_small(gsmall, jnp.broadcast_to(loss_part, (SUBLANES, 128))), _pack_small(wts, zeros),
        _pack_small(mom1, zeros), _pack_small(mom2, ones), after)
    for dst, val in ((grads, sg), (deltas, sd), (new_m, sm), (new_v, sv)):
        dst.update(_unpack_small(val, wts))
    loss = sg[SMALL_ROWS - SUBLANES, 0]
    return (loss, grad_x.reshape(x.shape), *[grads[n] for n in WEIGHTS], *[deltas[n] for n in WEIGHTS],
            *[new_m[n] for n in WEIGHTS], *[new_v[n] for n in WEIGHTS])
```

```python
import math

import numpy as np
import jax
import jax.numpy as jnp
from jax import lax
from jax.experimental import pallas as pl
from jax.experimental.pallas import tpu as pltpu

F32 = jnp.float32
BF16 = jnp.bfloat16
SDS = jax.ShapeDtypeStruct
MESH = pl.DeviceIdType.MESH

D_MODEL = 1024
SEQ = 2048
HEAD_DIM = 64
GROUPS = ((128, 1), (512, 4), (2048, 16))
HEADS_PER_GROUP = 8
N_HEADS = 24
ATTN_WIDTH = N_HEADS * HEAD_DIM
ATTN_OUT = HEADS_PER_GROUP * HEAD_DIM
CONV_K = 31
CONV_PAD = 32
D_FF = 2816
IN_WIDTH = 3 * ATTN_WIDTH + 2 * D_MODEL + 2 * D_MODEL
RMS_EPS = 1e-6
LN_EPS = 1e-5
Q_BLOCK = 128
LANES = 128
NEG = -1e30
N_DEV = 8

ADAM_LR = 0.001
ADAM_B1 = 0.9
ADAM_B2 = 0.999
ADAM_EPS = 1e-08
ADAM_WD = 0.01
ADAM_STEP = 10


def _alibi_slope_list(n):
    def pow2(m):
        start = 2.0 ** (-8.0 / m)
        return [start ** (i + 1) for i in range(m)]
    if math.log2(n).is_integer():
        return pow2(n)
    c = 2 ** math.floor(math.log2(n))
    return pow2(c) + _alibi_slope_list(2 * c)[0::2][: n - c]


def _slopes_times_dilation():
    s = np.asarray(sorted(_alibi_slope_list(N_HEADS), reverse=True), dtype=np.float32).reshape(3, HEADS_PER_GROUP)
    r = np.asarray([g[1] for g in GROUPS], dtype=np.float32)[:, None]
    return (s * r).reshape(N_HEADS)


def _sigmoid(x):
    return 1.0 / (1.0 + jnp.exp(-x))


def _dot(a, b):
    return jnp.dot(a, b, preferred_element_type=F32)


def _dot_nt(a, b):
    return lax.dot_general(a, b, (((1,), (1,)), ((), ())), preferred_element_type=F32)


def _dot_tn(a, b):
    return lax.dot_general(a, b, (((0,), (0,)), ((), ())), preferred_element_type=F32)


def _rowsum(x):
    return jnp.sum(x, axis=0, keepdims=True)


def _params(*sem):
    return pltpu.CompilerParams(dimension_semantics=sem)


def _anchored(body, n_in, dep):
    if dep is None:
        return body, [], []

    def wrapped(*refs):
        return body(*refs[:n_in], *refs[n_in + 1:])

    return wrapped, [pl.BlockSpec(memory_space=pl.ANY)], [dep]


IN_TM = 256
IN_CHUNK = 512


def _in_proj(x, g1, w_in_t, dep=None):
    T = x.shape[0]
    tm = IN_TM
    widths = (3 * ATTN_WIDTH, 2 * D_MODEL, 2 * D_MODEL)

    def body(x_ref, g_ref, w_hbm, h_ref, qkv_ref, u_ref, lg_ref, w_vmem, sem):
        @pl.when(pl.program_id(0) == 0)
        def _():
            cp = pltpu.make_async_copy(w_hbm, w_vmem, sem)
            cp.start()
            cp.wait()

        xv = x_ref[...]
        r = lax.rsqrt(jnp.mean(xv * xv, axis=-1, keepdims=True) + RMS_EPS)
        h = (xv * r * g_ref[...]).astype(BF16)
        h_ref[...] = h
        col = 0
        for o_ref, width in zip((qkv_ref, u_ref, lg_ref), widths):
            for j in range(width // IN_CHUNK):
                o_ref[:, j * IN_CHUNK:(j + 1) * IN_CHUNK] = _dot_nt(h, w_vmem[col:col + IN_CHUNK, :])
                col += IN_CHUNK

    row = lambda n: pl.BlockSpec((tm, n), lambda i: (i, 0))
    body, dep_spec, dep_arg = _anchored(body, 3, dep)
    return pl.pallas_call(
        body, grid=(T // tm,),
        in_specs=[row(D_MODEL), pl.BlockSpec((1, D_MODEL), lambda i: (0, 0)), pl.BlockSpec(memory_space=pl.ANY)] + dep_spec,
        out_specs=[row(D_MODEL)] + [row(n) for n in widths],
        out_shape=[SDS((T, D_MODEL), BF16)] + [SDS((T, n), F32) for n in widths],
        scratch_shapes=[pltpu.VMEM((IN_WIDTH, D_MODEL), BF16), pltpu.SemaphoreType.DMA],
        compiler_params=_params("arbitrary"), name="in_proj")(x, g1, w_in_t, *dep_arg)


def _mm_tn(a, b, out_dtype, name, tn, tt=1024):
    T, K = a.shape
    N = b.shape[1]
    nt = T // tt

    def body(a_ref, b_ref, o_ref, acc):
        t = pl.program_id(1)

        @pl.when(t == 0)
        def _():
            acc[...] = jnp.zeros_like(acc)

        acc[...] += _dot_tn(a_ref[...], b_ref[...])

        @pl.when(t == nt - 1)
        def _():
            o_ref[...] = acc[...].astype(o_ref.dtype)

    return pl.pallas_call(
        body, grid=(N // tn, nt),
        in_specs=[pl.BlockSpec((tt, K), lambda j, t: (t, 0)),
                  pl.BlockSpec((tt, tn), lambda j, t: (t, j))],
        out_specs=pl.BlockSpec((K, tn), lambda j, t: (0, j)),
        out_shape=SDS((K, N), out_dtype),
        scratch_shapes=[pltpu.VMEM((K, tn), F32)],
        compiler_params=_params("parallel", "arbitrary"), name=name)(a, b)


def _gather_classes(src_ref, dst, r, row0=0):
    L = SEQ // r
    for c in range(r):
        dst[row0 + c * L:row0 + (c + 1) * L, :] = src_ref[0, pl.ds(c, L, stride=r), :].astype(dst.dtype)


def _scatter_classes(src, dst, r, row0=0):
    L = SEQ // r
    for c in range(r):
        dst[pl.ds(c, L, stride=r), :] = src[row0 + c * L:row0 + (c + 1) * L, :].astype(dst.dtype)


def _attn_masks(slope_r):
    qi = lax.broadcasted_iota(jnp.int32, (Q_BLOCK, Q_BLOCK), 0)
    kj = lax.broadcasted_iota(jnp.int32, (Q_BLOCK, Q_BLOCK), 1)
    rel = (qi - kj).astype(F32)
    bias_cur = jnp.where(qi >= kj, -slope_r * rel, NEG)
    bias_prev = jnp.where(qi <= kj, -slope_r * (rel + float(Q_BLOCK)), NEG)
    return bias_cur, bias_prev


def _store_biases(bias, sl_ref, g, hp):
    for hh in range(2):
        cur, prev = _attn_masks(sl_ref[g * HEADS_PER_GROUP + 2 * hp + hh])
        rows = slice(hh * Q_BLOCK, (hh + 1) * Q_BLOCK)
        bias[0, rows, 0:Q_BLOCK] = prev
        bias[1, rows, 0:Q_BLOCK] = jnp.full((Q_BLOCK, Q_BLOCK), NEG, F32)
        bias[0, rows, Q_BLOCK:] = cur
        bias[1, rows, Q_BLOCK:] = cur


def _transpose_pairs(src, dst):
    dst[0, :, 0:Q_BLOCK] = jnp.zeros((LANES, Q_BLOCK), dst.dtype)
    nblk = SEQ // Q_BLOCK
    for b in range(nblk):
        t = src[(b + 1) * Q_BLOCK:(b + 2) * Q_BLOCK, :].T
        dst[b, :, Q_BLOCK:] = t
        if b + 1 < nblk:
            dst[b + 1, :, 0:Q_BLOCK] = t


def _stack_heads(t, low):
    z = jnp.zeros_like(t)
    return jnp.concatenate([jnp.where(low, t, z), jnp.where(low, z, t)], axis=0)


def _unstack_heads(t2, low):
    return jnp.where(low, t2[:Q_BLOCK], t2[Q_BLOCK:])


def _unit_offsets(u, nb):
    off = pl.multiple_of(u * Q_BLOCK, Q_BLOCK)
    n = u & (nb - 1)
    c = u >> int(math.log2(nb))
    return off, n == 0, c, n


ATTN_UNROLL = 4


def _attn_fwd(qkv, slopes_r, batch):
    nblk = SEQ // Q_BLOCK

    def body(sl_ref, *refs):
        qkv_refs = refs[:9]
        att_ref, lse_ref = refs[9:11]
        qd, kd, vd, kt, opos, lpos, bias = refs[11:]
        hp = pl.program_id(1)
        low = lax.broadcasted_iota(jnp.int32, (Q_BLOCK, LANES), 1) < HEAD_DIM

        for g in range(3):
            r = GROUPS[g][1]
            nb = SEQ // r // Q_BLOCK
            _gather_classes(qkv_refs[3 * g], qd, r)
            kd[0:Q_BLOCK, :] = jnp.zeros((Q_BLOCK, LANES), BF16)
            vd[0:Q_BLOCK, :] = jnp.zeros((Q_BLOCK, LANES), BF16)
            _gather_classes(qkv_refs[3 * g + 1], kd, r, Q_BLOCK)
            _gather_classes(qkv_refs[3 * g + 2], vd, r, Q_BLOCK)
            _transpose_pairs(kd, kt)
            _store_biases(bias, sl_ref, g, hp)

            def unit(u, carry, g=g, r=r, nb=nb):
                off, first, c, n = _unit_offsets(u, nb)
                q2 = _stack_heads(qd[pl.ds(off, Q_BLOCK), :], low)
                s = _dot(q2, kt[u]) * 0.125 + bias[first.astype(jnp.int32)]
                m = jnp.max(s, axis=-1, keepdims=True)
                p = jnp.exp(s - m)
                l = jnp.sum(p, axis=-1, keepdims=True)
                o2 = _dot(p.astype(BF16), vd[pl.ds(off, 2 * Q_BLOCK), :]) * (1.0 / l)
                lse2 = m + jnp.log(l)
                rows = pl.ds(c + n * (Q_BLOCK * r), Q_BLOCK, stride=r)
                opos[g, rows, :] = _unstack_heads(o2, low)
                lpos[g, rows, :] = jnp.where(low, lse2[:Q_BLOCK], lse2[Q_BLOCK:])
                return carry

            lax.fori_loop(0, nblk, unit, 0, unroll=ATTN_UNROLL)

        def merge(i, carry):
            rows = pl.ds(pl.multiple_of(i * 256, 256), 256)
            l0, l1, l2 = lpos[0, rows, :], lpos[1, rows, :], lpos[2, rows, :]
            m = jnp.maximum(jnp.maximum(l0, l1), l2)
            e0, e1, e2 = jnp.exp(l0 - m), jnp.exp(l1 - m), jnp.exp(l2 - m)
            den = e0 + e1 + e2
            att = (e0 * opos[0, rows, :] + e1 * opos[1, rows, :] + e2 * opos[2, rows, :]) / den
            att_ref[0, rows, :] = att.astype(att_ref.dtype)
            lse_ref[0, rows, :] = m + jnp.log(den)
            return carry

        lax.fori_loop(0, SEQ // 256, merge, 0)

    def col(sec, g):
        return pl.BlockSpec((1, SEQ, LANES), lambda b, hp: (b, 0, sec * 12 + g * 4 + hp))

    out = pl.BlockSpec((1, SEQ, LANES), lambda b, hp: (b, 0, hp))
    return pl.pallas_call(
        body, grid=(batch, 4),
        in_specs=[pl.BlockSpec(memory_space=pltpu.SMEM)] + [col(sec, g) for g in range(3) for sec in range(3)],
        out_specs=[out, out],
        out_shape=[SDS((batch, SEQ, ATTN_OUT), BF16), SDS((batch, SEQ, ATTN_OUT), F32)],
        scratch_shapes=[pltpu.VMEM((SEQ, LANES), BF16), pltpu.VMEM((Q_BLOCK + SEQ, LANES), BF16),
                        pltpu.VMEM((Q_BLOCK + SEQ, LANES), BF16), pltpu.VMEM((nblk, LANES, 2 * Q_BLOCK), BF16),
                        pltpu.VMEM((3, SEQ, LANES), F32), pltpu.VMEM((3, SEQ, LANES), F32),
                        pltpu.VMEM((2, 2 * Q_BLOCK, 2 * Q_BLOCK), F32)],
        compiler_params=_params("parallel", "parallel"), name="attn_fwd")(slopes_r, *([qkv] * 9))


def _attn_bwd(qkv, datt, lse, dsum, slopes_r, batch):
    nblk = SEQ // Q_BLOCK

    def body(sl_ref, q_ref, k_ref, v_ref, do_ref, l_ref, d_ref, dq_ref, dk_ref, dv_ref,
             qd, kd, vd, dod, kt, vt, ld, dd, dq_acc, dk_acc, dv_acc, dk_part, dv_part, stage, bias):
        gid, hp = pl.program_id(1), pl.program_id(2)
        low = lax.broadcasted_iota(jnp.int32, (Q_BLOCK, LANES), 1) < HEAD_DIM

        def section(g):
            r = GROUPS[g][1]
            nb = SEQ // r // Q_BLOCK
            _gather_classes(q_ref, qd, r)
            kd[0:Q_BLOCK, :] = jnp.zeros((Q_BLOCK, LANES), BF16)
            vd[0:Q_BLOCK, :] = jnp.zeros((Q_BLOCK, LANES), BF16)
            _gather_classes(k_ref, kd, r, Q_BLOCK)
            _gather_classes(v_ref, vd, r, Q_BLOCK)
            _gather_classes(do_ref, dod, r)
            _gather_classes(l_ref, ld, r)
            _gather_classes(d_ref, dd, r)
            _transpose_pairs(kd, kt)
            _transpose_pairs(vd, vt)
            _store_biases(bias, sl_ref, g, hp)

            def unit(u, carry):
                off, first, _, _ = _unit_offsets(u, nb)
                pair = pl.ds(off, 2 * Q_BLOCK)
                q2 = _stack_heads(qd[pl.ds(off, Q_BLOCK), :], low)
                do2 = _stack_heads(dod[pl.ds(off, Q_BLOCK), :], low)
                lse_t = ld[pl.ds(off, Q_BLOCK), :]
                dsum_t = dd[pl.ds(off, Q_BLOCK), :]
                lse2 = jnp.concatenate([lse_t[:, 0:1], lse_t[:, HEAD_DIM:HEAD_DIM + 1]], axis=0)
                dsum2 = jnp.concatenate([dsum_t[:, 0:1], dsum_t[:, HEAD_DIM:HEAD_DIM + 1]], axis=0)
                s = _dot(q2, kt[u]) * 0.125 + bias[first.astype(jnp.int32)]
                p = jnp.exp(s - lse2)
                ds = (p * (_dot(do2, vt[u]) - dsum2)).astype(BF16)
                dq_acc[pl.ds(off, Q_BLOCK), :] = _unstack_heads(_dot(ds, kd[pair, :]), low) * 0.125
                dk_part[u] = _dot_tn(ds, q2) * 0.125
                dv_part[u] = _dot_tn(p.astype(BF16), do2)
                return carry

            lax.fori_loop(0, nblk, unit, 0, unroll=ATTN_UNROLL)
            for part, acc in ((dk_part, dk_acc), (dv_part, dv_acc)):
                for b in range(nblk):
                    t = part[b, Q_BLOCK:, :]
                    if b + 1 < nblk:
                        t = t + part[b + 1, 0:Q_BLOCK, :]
                    acc[b * Q_BLOCK:(b + 1) * Q_BLOCK, :] = t
            for acc, out_ref in ((dq_acc, dq_ref), (dk_acc, dk_ref), (dv_acc, dv_ref)):
                _scatter_classes(acc, stage, r)
                out_ref[0] = stage[...].astype(out_ref.dtype)

        for g in range(3):
            pl.when(gid == g)(lambda g=g: section(g))

    def col(sec):
        return pl.BlockSpec((1, SEQ, LANES), lambda b, g, hp: (b, 0, sec * 12 + g * 4 + hp))

    pos = pl.BlockSpec((1, SEQ, LANES), lambda b, g, hp: (b, 0, hp))
    dout = pl.BlockSpec((1, SEQ, LANES), lambda b, g, hp: (b, 0, g * 4 + hp))
    out = SDS((batch, SEQ, ATTN_WIDTH), BF16)
    seq_bf = pltpu.VMEM((SEQ, LANES), BF16)
    seq_f = pltpu.VMEM((SEQ, LANES), F32)
    pad_bf = pltpu.VMEM((Q_BLOCK + SEQ, LANES), BF16)
    part = pltpu.VMEM((nblk, 2 * Q_BLOCK, LANES), F32)
    blk_t = pltpu.VMEM((nblk, LANES, 2 * Q_BLOCK), BF16)
    return pl.pallas_call(
        body, grid=(batch, 3, 4),
        in_specs=[pl.BlockSpec(memory_space=pltpu.SMEM), col(0), col(1), col(2), pos, pos, pos],
        out_specs=[dout, dout, dout],
        out_shape=[out, out, out],
        scratch_shapes=[seq_bf, pad_bf, pad_bf, seq_bf, blk_t, blk_t, seq_f, seq_f, seq_f, seq_f, seq_f, part, part, seq_f,
                        pltpu.VMEM((2, 2 * Q_BLOCK, 2 * Q_BLOCK), F32)],
        compiler_params=_params("parallel", "parallel", "parallel"), name="attn_bwd")(
            slopes_r, qkv, qkv, qkv, datt, lse, dsum)


CONV_TC = 128
CONV_ROWS = 128
SUBLANES = 8


def _fill_shifted(sh):
    n = SEQ + CONV_PAD - SUBLANES
    for s in range(1, SUBLANES):
        sh[s, 0:n, :] = sh[0, s:s + n, :]


def _tap(sh, base, offset):
    s = offset % SUBLANES
    return sh[s, pl.ds(pl.multiple_of(base + (offset - s), SUBLANES), CONV_ROWS), :]


def _conv_fwd(u, conv_w, conv_b, batch):
    nct = D_MODEL // CONV_TC

    def body(ua_ref, ub_ref, w_ref, b_ref, o_ref, sh):
        sh[0, 0:CONV_PAD, :] = jnp.zeros((CONV_PAD, CONV_TC), F32)
        sh[0, CONV_PAD:, :] = ua_ref[0] * _sigmoid(ub_ref[0])
        _fill_shifted(sh)

        def chunk(c, carry):
            base = pl.multiple_of(c * CONV_ROWS, CONV_ROWS)
            acc = jnp.broadcast_to(b_ref[...], (CONV_ROWS, CONV_TC))
            for t in range(CONV_K):
                acc = acc + _tap(sh, base, t + CONV_PAD - (CONV_K - 1)) * w_ref[t:t + 1, :]
            o_ref[0, pl.ds(base, CONV_ROWS), :] = acc
            return carry

        lax.fori_loop(0, SEQ // CONV_ROWS, chunk, 0)

    return pl.pallas_call(
        body, grid=(nct, batch),
        in_specs=[pl.BlockSpec((1, SEQ, CONV_TC), lambda j, b: (b, 0, j)),
                  pl.BlockSpec((1, SEQ, CONV_TC), lambda j, b: (b, 0, j + nct)),
                  pl.BlockSpec((CONV_PAD, CONV_TC), lambda j, b: (0, j)),
                  pl.BlockSpec((1, CONV_TC), lambda j, b: (0, j))],
        out_specs=pl.BlockSpec((1, SEQ, CONV_TC), lambda j, b: (b, 0, j)),
        out_shape=SDS((batch, SEQ, D_MODEL), F32),
        scratch_shapes=[pltpu.VMEM((SUBLANES, SEQ + CONV_PAD, CONV_TC), F32)],
        compiler_params=_params("parallel", "parallel"), name="conv_fwd")(u, u, conv_w, conv_b)


def _conv_bwd(u, dc1, conv_w, batch, dep=None):
    nct = D_MODEL // CONV_TC
    nchunk = SEQ // CONV_ROWS

    def body(ua_ref, ub_ref, d_ref, w_ref, dua_ref, dub_ref, gw_ref, gb_ref, shc, shd, gacc):
        b = pl.program_id(1)
        shc[0, 0:CONV_PAD, :] = jnp.zeros((CONV_PAD, CONV_TC), F32)
        shc[0, CONV_PAD:, :] = ua_ref[0] * _sigmoid(ub_ref[0])
        _fill_shifted(shc)
        shd[0, 0:SEQ, :] = d_ref[0]
        shd[0, SEQ:, :] = jnp.zeros((CONV_PAD, CONV_TC), F32)
        _fill_shifted(shd)

        @pl.when(b == 0)
        def _():
            gacc[...] = jnp.zeros_like(gacc)
            gb_ref[...] = jnp.zeros_like(gb_ref)

        gb_ref[...] += _rowsum(d_ref[0])

        def chunk(c, carry):
            base = pl.multiple_of(c * CONV_ROWS, CONV_ROWS)
            dcur = shd[0, pl.ds(base, CONV_ROWS), :]
            acc = jnp.zeros((CONV_ROWS, CONV_TC), F32)
            for t in range(CONV_K):
                acc = acc + _tap(shd, base, CONV_K - 1 - t) * w_ref[t:t + 1, :]
                prod = _tap(shc, base, t + CONV_PAD - (CONV_K - 1)) * dcur
                gacc[t] += jnp.sum(prod.reshape(CONV_ROWS // 8, 8, CONV_TC), axis=0)
            ua = ua_ref[0, pl.ds(base, CONV_ROWS), :]
            sg = _sigmoid(ub_ref[0, pl.ds(base, CONV_ROWS), :])
            dua_ref[0, pl.ds(base, CONV_ROWS), :] = (acc * sg).astype(dua_ref.dtype)
            dub_ref[0, pl.ds(base, CONV_ROWS), :] = (acc * ua * sg * (1.0 - sg)).astype(dub_ref.dtype)
            return carry

        lax.fori_loop(0, nchunk, chunk, 0)

        @pl.when(b == batch - 1)
        def _():
            for t in range(CONV_K):
                gw_ref[t:t + 1, :] = jnp.sum(gacc[t], axis=0, keepdims=True)
            gw_ref[CONV_K:CONV_PAD, :] = jnp.zeros((CONV_PAD - CONV_K, CONV_TC), F32)

    du = SDS((batch, SEQ, D_MODEL), BF16)
    body, dep_spec, dep_arg = _anchored(body, 4, dep)
    return pl.pallas_call(
        body, grid=(nct, batch),
        in_specs=[pl.BlockSpec((1, SEQ, CONV_TC), lambda j, b: (b, 0, j)),
                  pl.BlockSpec((1, SEQ, CONV_TC), lambda j, b: (b, 0, j + nct)),
                  pl.BlockSpec((1, SEQ, CONV_TC), lambda j, b: (b, 0, j)),
                  pl.BlockSpec((CONV_PAD, CONV_TC), lambda j, b: (0, j))] + dep_spec,
        out_specs=[pl.BlockSpec((1, SEQ, CONV_TC), lambda j, b: (b, 0, j)),
                   pl.BlockSpec((1, SEQ, CONV_TC), lambda j, b: (b, 0, j)),
                   pl.BlockSpec((CONV_PAD, CONV_TC), lambda j, b: (0, j)),
                   pl.BlockSpec((1, CONV_TC), lambda j, b: (0, j))],
        out_shape=[du, du, SDS((CONV_PAD, D_MODEL), F32), SDS((1, D_MODEL), F32)],
        scratch_shapes=[pltpu.VMEM((SUBLANES, SEQ + CONV_PAD, CONV_TC), F32),
                        pltpu.VMEM((SUBLANES, SEQ + CONV_PAD, CONV_TC), F32),
                        pltpu.VMEM((CONV_K, 8, CONV_TC), F32)],
        compiler_params=_params("parallel", "arbitrary"), name="conv_bwd")(u, u, dc1, conv_w, *dep_arg)


MID_TM = 256


def _layernorm_stats(c1):
    mu = jnp.mean(c1, axis=-1, keepdims=True)
    cen = c1 - mu
    rs = lax.rsqrt(jnp.mean(cen * cen, axis=-1, keepdims=True) + LN_EPS)
    return cen * rs, rs


def _mid_fwd(att, c1, logits, x, w_a, w_c, w_o, gate_b, ln_g, ln_b, g2, dep=None):
    T = x.shape[0]
    tm = MID_TM

    def body(att_ref, c1_ref, lg_ref, x_ref, wa_ref, wc_ref, wo_ref, gb_ref, lng_ref, lnb_ref, g2_ref,
             c3_ref, ya_ref, yc_ref, mix_ref, x1_ref, h2_ref):
        ya = _dot(att_ref[...], wa_ref[...])
        xh, _ = _layernorm_stats(c1_ref[...])
        c2 = xh * lng_ref[...] + lnb_ref[...]
        c3 = (c2 * _sigmoid(c2)).astype(BF16)
        c3_ref[...] = c3
        yc = _dot(c3, wc_ref[...])
        gates = _sigmoid(lg_ref[...] + gb_ref[...])
        mix = (gates[:, :D_MODEL] * ya + gates[:, D_MODEL:] * yc).astype(BF16)
        ya_ref[...] = ya.astype(BF16)
        yc_ref[...] = yc.astype(BF16)
        mix_ref[...] = mix
        x1 = x_ref[...] + _dot(mix, wo_ref[...])
        x1_ref[...] = x1
        r = lax.rsqrt(jnp.mean(x1 * x1, axis=-1, keepdims=True) + RMS_EPS)
        h2_ref[...] = (x1 * r * g2_ref[...]).astype(BF16)

    row = lambda n: pl.BlockSpec((tm, n), lambda i: (i, 0))
    full = lambda a, b: pl.BlockSpec((a, b), lambda i: (0, 0))
    body, dep_spec, dep_arg = _anchored(body, 11, dep)
    return pl.pallas_call(
        body, grid=(T // tm,),
        in_specs=[row(ATTN_OUT), row(D_MODEL), row(2 * D_MODEL), row(D_MODEL),
                  full(ATTN_OUT, D_MODEL), full(D_MODEL, D_MODEL), full(D_MODEL, D_MODEL),
                  full(1, 2 * D_MODEL), full(1, D_MODEL), full(1, D_MODEL), full(1, D_MODEL)] + dep_spec,
        out_specs=[row(D_MODEL), row(D_MODEL), row(D_MODEL), row(D_MODEL), row(D_MODEL), row(D_MODEL)],
        out_shape=[SDS((T, D_MODEL), BF16), SDS((T, D_MODEL), BF16), SDS((T, D_MODEL), BF16), SDS((T, D_MODEL), BF16),
                   SDS((T, D_MODEL), F32), SDS((T, D_MODEL), BF16)],
        compiler_params=_params("parallel"), name="mid_fwd")(att, c1, logits, x, w_a, w_c, w_o, gate_b, ln_g, ln_b, g2,
                                                             *dep_arg)


def _mid_bwd(dx1b, ya, yc, logits, att, c1, w_a, w_c, w_o, gate_b, ln_g, ln_b, head_ones, dep=None):
    T = dx1b.shape[0]
    tm = MID_TM

    def body(dx_ref, ya_ref, yc_ref, lg_ref, att_ref, c1_ref, wa_ref, wc_ref, wo_ref, gb_ref, lng_ref, lnb_ref, e_ref,
             dlg_ref, dya_ref, dyc_ref, datt_ref, dsum_ref, dc1_ref, ggb_ref, glg_ref, glb_ref):
        @pl.when(pl.program_id(0) == 0)
        def _():
            ggb_ref[...] = jnp.zeros_like(ggb_ref)
            glg_ref[...] = jnp.zeros_like(glg_ref)
            glb_ref[...] = jnp.zeros_like(glb_ref)

        dmix = _dot_nt(dx_ref[...], wo_ref[...])
        gates = _sigmoid(lg_ref[...] + gb_ref[...])
        ga, gc = gates[:, :D_MODEL], gates[:, D_MODEL:]
        dla = dmix * ya_ref[...].astype(F32) * ga * (1.0 - ga)
        dlc = dmix * yc_ref[...].astype(F32) * gc * (1.0 - gc)
        dlg_ref[:, :D_MODEL] = dla.astype(BF16)
        dlg_ref[:, D_MODEL:] = dlc.astype(BF16)
        ggb_ref[:, :D_MODEL] += _rowsum(dla)
        ggb_ref[:, D_MODEL:] += _rowsum(dlc)
        dya = (dmix * ga).astype(BF16)
        dyc = (dmix * gc).astype(BF16)
        dya_ref[...] = dya
        dyc_ref[...] = dyc
        datt = _dot_nt(dya, wa_ref[...])
        datt_ref[...] = datt
        dsum_ref[...] = jnp.dot(datt * att_ref[...].astype(F32), e_ref[...], preferred_element_type=F32,
                                precision=lax.Precision.HIGHEST)
        dc3 = _dot_nt(dyc, wc_ref[...])
        xh, rs = _layernorm_stats(c1_ref[...])
        c2 = xh * lng_ref[...] + lnb_ref[...]
        sg = _sigmoid(c2)
        dc2 = dc3 * (sg * (1.0 + c2 * (1.0 - sg)))
        glg_ref[...] += _rowsum(dc2 * xh)
        glb_ref[...] += _rowsum(dc2)
        dxh = dc2 * lng_ref[...]
        dc1_ref[...] = rs * (dxh - jnp.mean(dxh, axis=-1, keepdims=True) - xh * jnp.mean(dxh * xh, axis=-1, keepdims=True))

    row = lambda n: pl.BlockSpec((tm, n), lambda i: (i, 0))
    full = lambda a, b: pl.BlockSpec((a, b), lambda i: (0, 0))
    body, dep_spec, dep_arg = _anchored(body, 13, dep)
    return pl.pallas_call(
        body, grid=(T // tm,),
        in_specs=[row(D_MODEL), row(D_MODEL), row(D_MODEL), row(2 * D_MODEL), row(ATTN_OUT), row(D_MODEL),
                  full(ATTN_OUT, D_MODEL), full(D_MODEL, D_MODEL), full(D_MODEL, D_MODEL),
                  full(1, 2 * D_MODEL), full(1, D_MODEL), full(1, D_MODEL), full(ATTN_OUT, ATTN_OUT)] + dep_spec,
        out_specs=[row(2 * D_MODEL), row(D_MODEL), row(D_MODEL), row(ATTN_OUT), row(ATTN_OUT), row(D_MODEL),
                   full(1, 2 * D_MODEL), full(1, D_MODEL), full(1, D_MODEL)],
        out_shape=[SDS((T, 2 * D_MODEL), BF16), SDS((T, D_MODEL), BF16), SDS((T, D_MODEL), BF16), SDS((T, ATTN_OUT), F32),
                   SDS((T, ATTN_OUT), F32), SDS((T, D_MODEL), F32),
                   SDS((1, 2 * D_MODEL), F32), SDS((1, D_MODEL), F32), SDS((1, D_MODEL), F32)],
        compiler_params=_params("arbitrary"), name="mid_bwd")(dx1b, ya, yc, logits, att, c1, w_a, w_c, w_o, gate_b, ln_g, ln_b,
                                                               head_ones, *dep_arg)


FFN_TM = 512
FFN_TF = D_FF // 2


def _rms_bwd(dy_times_g, xh, r):
    return r * (dy_times_g - xh * jnp.mean(dy_times_g * xh, axis=-1, keepdims=True))


def _ffn_fwd(h2, x1, target, gf, w_g_t, w_u_t, w_d):
    T = h2.shape[0]
    tm, tf = FFN_TM, FFN_TF
    nf = D_FF // tf

    def body(h_ref, x1_ref, t_ref, gf_ref, wg_ref, wu_ref, wd_ref,
             a_ref, b_ref, f_ref, dx2_ref, dx2b_ref, loss_ref, gnf_ref, acc):
        i, j = pl.program_id(0), pl.program_id(1)
        h = h_ref[...]
        a = _dot_nt(h, wg_ref[...])
        b = _dot_nt(h, wu_ref[...])
        f = (a * _sigmoid(a) * b).astype(BF16)
        a_ref[...] = a.astype(BF16)
        b_ref[...] = b.astype(BF16)
        f_ref[...] = f
        p = _dot(f, wd_ref[...])

        @pl.when(j == 0)
        def _():
            acc[...] = x1_ref[...] + p

        @pl.when(j > 0)
        def _():
            acc[...] += p

        @pl.when((i == 0) & (j == nf - 1))
        def _():
            loss_ref[...] = jnp.zeros_like(loss_ref)
            gnf_ref[...] = jnp.zeros_like(gnf_ref)

        @pl.when(j == nf - 1)
        def _():
            x2 = acc[...]
            r = lax.rsqrt(jnp.mean(x2 * x2, axis=-1, keepdims=True) + RMS_EPS)
            xh = x2 * r
            err = xh * gf_ref[...] - t_ref[...]
            loss_ref[...] += (0.5 / D_MODEL) * jnp.sum(err * err)
            dy = err * (1.0 / D_MODEL)
            gnf_ref[...] += _rowsum(dy * xh)
            dx2 = _rms_bwd(dy * gf_ref[...], xh, r)
            dx2_ref[...] = dx2
            dx2b_ref[...] = dx2.astype(BF16)

    row = lambda n: pl.BlockSpec((tm, n), lambda i, j: (i, 0))
    ffb = pl.BlockSpec((tm, tf), lambda i, j: (i, j))
    wblk = pl.BlockSpec((tf, D_MODEL), lambda i, j: (j, 0))
    return pl.pallas_call(
        body, grid=(T // tm, nf),
        in_specs=[row(D_MODEL), row(D_MODEL), row(D_MODEL), pl.BlockSpec((1, D_MODEL), lambda i, j: (0, 0)),
                  wblk, wblk, wblk],
        out_specs=[ffb, ffb, ffb, row(D_MODEL), row(D_MODEL),
                   pl.BlockSpec((1, 128), lambda i, j: (0, 0)), pl.BlockSpec((1, D_MODEL), lambda i, j: (0, 0))],
        out_shape=[SDS((T, D_FF), BF16), SDS((T, D_FF), BF16), SDS((T, D_FF), BF16), SDS((T, D_MODEL), F32),
                   SDS((T, D_MODEL), BF16), SDS((1, 128), F32), SDS((1, D_MODEL), F32)],
        scratch_shapes=[pltpu.VMEM((tm, D_MODEL), F32)],
        compiler_params=_params("arbitrary", "arbitrary"), name="ffn_fwd")(h2, x1, target, gf, w_g_t, w_u_t, w_d)


def _ffn_bwd(dx2b, dx2, a, b, x1, g2, w_g_t, w_u_t, w_d):
    T = dx2.shape[0]
    tm, tf = FFN_TM, FFN_TF
    nf = D_FF // tf

    def body(dxb_ref, dx2_ref, a_ref, b_ref, x1_ref, g2_ref, wg_ref, wu_ref, wd_ref,
             da_ref, db_ref, dx1_ref, dx1b_ref, gn2_ref, acc):
        i, j = pl.program_id(0), pl.program_id(1)
        df = _dot_nt(dxb_ref[...], wd_ref[...])
        av = a_ref[...].astype(F32)
        bv = b_ref[...].astype(F32)
        sg = _sigmoid(av)
        db = (df * av * sg).astype(BF16)
        da = (df * bv * (sg * (1.0 + av * (1.0 - sg)))).astype(BF16)
        da_ref[...] = da
        db_ref[...] = db
        p = _dot(da, wg_ref[...]) + _dot(db, wu_ref[...])

        @pl.when(j == 0)
        def _():
            acc[...] = p

        @pl.when(j > 0)
        def _():
            acc[...] += p

        @pl.when((i == 0) & (j == nf - 1))
        def _():
            gn2_ref[...] = jnp.zeros_like(gn2_ref)

        @pl.when(j == nf - 1)
        def _():
            dh2 = acc[...]
            x1 = x1_ref[...]
            r = lax.rsqrt(jnp.mean(x1 * x1, axis=-1, keepdims=True) + RMS_EPS)
            xh = x1 * r
            gn2_ref[...] += _rowsum(dh2 * xh)
            dx1 = dx2_ref[...] + _rms_bwd(dh2 * g2_ref[...], xh, r)
            dx1_ref[...] = dx1
            dx1b_ref[...] = dx1.astype(BF16)

    row = lambda n: pl.BlockSpec((tm, n), lambda i, j: (i, 0))
    ffb = pl.BlockSpec((tm, tf), lambda i, j: (i, j))
    wblk = pl.BlockSpec((tf, D_MODEL), lambda i, j: (j, 0))
    return pl.pallas_call(
        body, grid=(T // tm, nf),
        in_specs=[row(D_MODEL), row(D_MODEL), ffb, ffb, row(D_MODEL), pl.BlockSpec((1, D_MODEL), lambda i, j: (0, 0)),
                  wblk, wblk, wblk],
        out_specs=[ffb, ffb, row(D_MODEL), row(D_MODEL), pl.BlockSpec((1, D_MODEL), lambda i, j: (0, 0))],
        out_shape=[SDS((T, D_FF), BF16), SDS((T, D_FF), BF16), SDS((T, D_MODEL), F32), SDS((T, D_MODEL), BF16),
                   SDS((1, D_MODEL), F32)],
        scratch_shapes=[pltpu.VMEM((tm, D_MODEL), F32)],
        compiler_params=_params("arbitrary", "arbitrary"), name="ffn_bwd")(dx2b, dx2, a, b, x1, g2, w_g_t, w_u_t, w_d)


def _in_bwd(pieces, w_in_t, x, dx1, g1, dep=None):
    T = x.shape[0]
    tm = IN_TM
    npc = len(pieces)
    assert sum(p.shape[1] for p in pieces) == IN_WIDTH

    def body(*refs):
        p_refs = refs[:npc]
        w_hbm, x_ref, dx1_ref, g_ref, dx_ref, gn1_ref, w_vmem, sem = refs[npc:]

        @pl.when(pl.program_id(0) == 0)
        def _():
            cp = pltpu.make_async_copy(w_hbm, w_vmem, sem)
            cp.start()
            cp.wait()
            gn1_ref[...] = jnp.zeros_like(gn1_ref)

        dh = jnp.zeros((tm, D_MODEL), F32)
        col = 0
        for p_ref in p_refs:
            for j in range(p_ref.shape[1] // IN_CHUNK):
                dh = dh + _dot(p_ref[:, j * IN_CHUNK:(j + 1) * IN_CHUNK], w_vmem[col:col + IN_CHUNK, :])
                col += IN_CHUNK
        xv = x_ref[...]
        r = lax.rsqrt(jnp.mean(xv * xv, axis=-1, keepdims=True) + RMS_EPS)
        xh = xv * r
        gn1_ref[...] += _rowsum(dh * xh)
        dx_ref[...] = dx1_ref[...] + _rms_bwd(dh * g_ref[...], xh, r)

    row = lambda n: pl.BlockSpec((tm, n), lambda i: (i, 0))
    body, dep_spec, dep_arg = _anchored(body, npc + 4, dep)
    return pl.pallas_call(
        body, grid=(T // tm,),
        in_specs=[row(p.shape[1]) for p in pieces]
        + [pl.BlockSpec(memory_space=pl.ANY), row(D_MODEL), row(D_MODEL), pl.BlockSpec((1, D_MODEL), lambda i: (0, 0))]
        + dep_spec,
        out_specs=[row(D_MODEL), pl.BlockSpec((1, D_MODEL), lambda i: (0, 0))],
        out_shape=[SDS((T, D_MODEL), F32), SDS((1, D_MODEL), F32)],
        scratch_shapes=[pltpu.VMEM((IN_WIDTH, D_MODEL), BF16), pltpu.SemaphoreType.DMA],
        compiler_params=_params("arbitrary"), name="in_bwd")(*pieces, w_in_t, x, dx1, g1, *dep_arg)


def _local_step(x, target, w, small, dep=None, late_weights=None, emit=None):
    T = x.shape[0]
    batch = T // SEQ
    slopes_r = jnp.asarray(_slopes_times_dilation())
    emit = emit or (lambda names, grads: None)

    h, qkv, u, logits = _in_proj(x, small["norm1_g"], w["w_in"], dep)

    qkv3 = qkv.reshape(batch, SEQ, 3 * ATTN_WIDTH)
    att, lse = _attn_fwd(qkv3, slopes_r, batch)
    att = att.reshape(T, ATTN_OUT)

    u3 = u.reshape(batch, SEQ, 2 * D_MODEL)
    c1 = _conv_fwd(u3, w["conv_w"], small["conv_b"], batch).reshape(T, D_MODEL)
    if late_weights is not None:
        w = {**w, **late_weights(LATE_MERGE, c1)}

    c3, ya, yc, mix, x1, h2 = _mid_fwd(
        att, c1, logits, x, w["w_attn_out"], w["w_conv_out"], w["w_o"],
        small["gate_b"], small["conv_ln_g"], small["conv_ln_b"], small["norm2_g"], w.get("token"))
    if late_weights is not None:
        w = {**w, **late_weights(LATE_FFN, h2)}

    a, b, f, dx2, dx2b, loss, g_normf = _ffn_fwd(h2, x1, target, small["norm_f_g"],
                                                   w["w_ffn_gate"], w["w_ffn_up"], w["w_ffn_down"])

    da, db, dx1, dx1b, g_norm2 = _ffn_bwd(dx2b, dx2, a, b, x1, small["norm2_g"],
                                           w["w_ffn_gate"], w["w_ffn_up"], w["w_ffn_down"])
    gw = {}
    gw["w_ffn_down"] = _mm_tn(f, dx2b, BF16, "gw_ffn_down", tn=512)
    gw["w_ffn_gate"] = _mm_tn(da, h2, BF16, "gw_ffn_gate", tn=512)
    gw["w_ffn_up"] = _mm_tn(db, h2, BF16, "gw_ffn_up", tn=512)
    token = emit(("w_ffn_gate", "w_ffn_up", "w_ffn_down"), gw)

    head_ones = jnp.asarray(np.kron(np.eye(HEADS_PER_GROUP, dtype=np.float32), np.ones((HEAD_DIM, HEAD_DIM), np.float32)))
    dlogits, dya, dyc, datt, dsum, dc1, g_gate_b, g_ln_g, g_ln_b = _mid_bwd(
        dx1b, ya, yc, logits, att, c1, w["w_attn_out"], w["w_conv_out"], w["w_o"],
        small["gate_b"], small["conv_ln_g"], small["conv_ln_b"], head_ones, token)
    gw["w_o"] = _mm_tn(mix, dx1b, BF16, "gw_o", tn=512)
    gw["w_attn_out"] = _mm_tn(att, dya, BF16, "gw_attn_out", tn=512)
    gw["w_conv_out"] = _mm_tn(c3, dyc, BF16, "gw_conv_out", tn=512)
    token = emit(("w_conv_out", "w_attn_out", "w_o"), gw)

    dua, dub, g_conv_w, g_conv_b = _conv_bwd(u3, dc1.reshape(batch, SEQ, D_MODEL), w["conv_w"], batch, token)

    dq, dk, dv = _attn_bwd(qkv3, datt.reshape(batch, SEQ, ATTN_OUT), lse, dsum.reshape(batch, SEQ, ATTN_OUT),
                           slopes_r, batch)
    pieces = [dq.reshape(T, ATTN_WIDTH), dk.reshape(T, ATTN_WIDTH), dv.reshape(T, ATTN_WIDTH),
              dua.reshape(T, D_MODEL), dub.reshape(T, D_MODEL), dlogits]

    names = ("q", "k", "v", "ua", "ub", "gate")
    gw["w_in"] = jnp.concatenate([_mm_tn(p, h, BF16, "gw_in_" + nm, tn=512) for nm, p in zip(names, pieces)], axis=0)
    gw["conv_w"] = g_conv_w
    token = emit(("w_in", "conv_w"), gw)
    grad_x, g_norm1 = _in_bwd(pieces, w["w_in"], x, dx1, small["norm1_g"], token)

    gsmall = {"norm1_g": g_norm1, "gate_b": g_gate_b, "conv_b": g_conv_b, "conv_ln_g": g_ln_g, "conv_ln_b": g_ln_b,
              "norm2_g": g_norm2, "norm_f_g": g_normf}
    return loss, grad_x, gw, gsmall


ANY = pl.BlockSpec(memory_space=pl.ANY)


def _all_gather(arrs):
    n = len(arrs)

    def body(*refs):
        ins, outs = refs[:n], refs[n:2 * n]
        send_sems, recv_sems, local_sems = refs[2 * n:]
        x, y, c = lax.axis_index("x"), lax.axis_index("y"), lax.axis_index("c")
        me, sibling = (x, y, c), (x, y, 1 - c)
        chips = [(1 - x, y), (x, 1 - y), (1 - x, 1 - y)]

        def copy(a, k, block, to, src=None):
            px, py, pc = block
            dst = outs[a].at[4 * px + 2 * py + pc]
            return pltpu.make_async_remote_copy(
                src_ref=dst if src is None else src, dst_ref=dst,
                send_sem=send_sems.at[a, k], recv_sem=recv_sems.at[a, k], device_id=to, device_id_type=MESH)

        mine = [pltpu.make_async_copy(ins[a], outs[a].at[4 * x + 2 * y + c], local_sems.at[a]) for a in range(n)]
        for cp in mine:
            cp.start()
        first = []
        for j, chip in enumerate(chips):
            first += [copy(a, 1 + j, me, (*chip, c), src=ins[a]) for a in range(n)]
        first += [copy(a, 0, me, sibling, src=ins[a]) for a in range(n)]
        for cp in first:
            cp.start()
        passed = []
        for j, chip in enumerate(chips):
            for a in range(n):
                copy(a, 1 + j, (*chip, c), me).wait_recv()
                cp = copy(a, 4 + j, (*chip, c), sibling)
                cp.start()
                passed.append(cp)
        for a in range(n):
            copy(a, 0, sibling, me).wait_recv()
        for j, chip in enumerate(chips):
            for a in range(n):
                copy(a, 4 + j, (*chip, 1 - c), me).wait_recv()
        for cp in first + passed:
            cp.wait_send()
        for cp in mine:
            cp.wait()

    return pl.pallas_call(
        body, in_specs=[ANY] * n, out_specs=[ANY] * n,
        out_shape=[SDS((N_DEV,) + a.shape, a.dtype) for a in arrs],
        scratch_shapes=[pltpu.SemaphoreType.DMA((n, 7)), pltpu.SemaphoreType.DMA((n, 7)), pltpu.SemaphoreType.DMA((n,))],
        name="all_gather_weights")(*arrs)


HBM = pl.BlockSpec(memory_space=pltpu.HBM)
SEM = pl.BlockSpec(memory_space=pltpu.SEMAPHORE)
ALL_PEERS = tuple(range(1, N_DEV))
OTHER_CHIPS = (2, 4, 6)
SPLIT_EFFECT = pltpu.CompilerParams(has_side_effects=pltpu.SideEffectType.DATAFLOW_SIDE_EFFECTING)


def _exchange_copies(mode, ks, srcs, lands, send_sems, recv_sems):
    x, y, c = lax.axis_index("x"), lax.axis_index("y"), lax.axis_index("c")
    me = 4 * x + 2 * y + c
    send, recv = [], []
    for a in range(len(lands)):
        for i, k in enumerate(ks):
            peer = (x ^ ((k >> 2) & 1), y ^ ((k >> 1) & 1), c ^ (k & 1))
            pidx = 4 * peer[0] + 2 * peer[1] + peer[2]
            if mode == "gather":
                src, to, out_slot, in_slot = srcs[a], peer, me, pidx
            elif mode == "scatter":
                src, to, out_slot, in_slot = srcs[a].at[pidx], peer, me, pidx
            elif mode == "chip_scatter":
                src, to, out_slot, in_slot = srcs[a].at[pidx >> 1], peer, me >> 1, pidx >> 1
            else:
                src, to, out_slot, in_slot = lands[a].at[pidx], (x, y, 1 - c), pidx, pidx ^ 1
            s = a * len(ks) + i
            send.append(pltpu.make_async_remote_copy(
                src_ref=src, dst_ref=lands[a].at[out_slot], send_sem=send_sems.at[s], recv_sem=recv_sems.at[s],
                device_id=to, device_id_type=MESH))
            recv.append(pltpu.make_async_remote_copy(
                src_ref=src, dst_ref=lands[a].at[in_slot], send_sem=send_sems.at[s], recv_sem=recv_sems.at[s],
                device_id=to, device_id_type=MESH))
    return send, recv


def _send_start(mode, ks, name, srcs=(), lands=None):
    srcs = list(srcs)
    if lands is None:
        slots = 4 if mode == "chip_scatter" else N_DEV
        lands = [lax.empty((slots,) + (s.shape if mode == "gather" else s.shape[1:]), s.dtype) for s in srcs]
    ns, nl = len(srcs), len(lands)
    nsem = nl * len(ks)

    def body(*refs):
        send, _ = _exchange_copies(mode, ks, refs[:ns], refs[ns:ns + nl], refs[ns + nl], refs[ns + nl + 1])
        for cp in send:
            cp.start()
        token = refs[-1]
        token[...] = jnp.zeros_like(token)

    both = srcs + list(lands)
    res = pl.pallas_call(
        body, name=name,
        out_shape=(pltpu.SemaphoreType.DMA((nsem,)), pltpu.SemaphoreType.DMA((nsem,)),
                   *[pltpu.HBM(a.shape, a.dtype) for a in both], SDS((8, 128), F32)),
        in_specs=[HBM] * (ns + nl), out_specs=(SEM, SEM, *([HBM] * (ns + nl)), pl.BlockSpec(memory_space=pltpu.VMEM)),
        input_output_aliases={i: 2 + i for i in range(ns + nl)}, compiler_params=SPLIT_EFFECT,
    )(*[pltpu.with_memory_space_constraint(a, pltpu.HBM) for a in both])
    return dict(mode=mode, ks=ks, send_sems=res[0], recv_sems=res[1], srcs=res[2:2 + ns], lands=res[2 + ns:2 + ns + nl],
                token=res[-1])


def _send_wait(started, after, name):
    ns, nl = len(started["srcs"]), len(started["lands"])

    def body(*refs):
        send, recv = _exchange_copies(started["mode"], started["ks"], refs[:ns], refs[ns:ns + nl],
                                      refs[ns + nl], refs[ns + nl + 1])
        for cp in send:
            cp.wait_send()
        for cp in recv:
            cp.wait_recv()

    both = list(started["srcs"]) + list(started["lands"])
    res = pl.pallas_call(
        body, name=name,
        out_shape=tuple(pltpu.HBM(a.shape, a.dtype) for a in both),
        in_specs=[HBM] * (ns + nl) + [SEM, SEM, ANY], out_specs=tuple([HBM] * (ns + nl)),
        input_output_aliases={i: i for i in range(ns + nl)}, compiler_params=SPLIT_EFFECT,
    )(*both, started["send_sems"], started["recv_sems"], after)
    return res[:ns], res[ns:]


def _exchange_sibling(gs):
    n = len(gs)

    def body(*refs):
        ins, outs = refs[:n], refs[n:2 * n]
        send_sems, recv_sems = refs[2 * n:]
        x, y, c = lax.axis_index("x"), lax.axis_index("y"), lax.axis_index("c")
        copies = []
        for a in range(n):
            for j in range(4):
                copies.append(pltpu.make_async_remote_copy(
                    src_ref=ins[a].at[2 * j + (1 - c)], dst_ref=outs[a].at[j],
                    send_sem=send_sems.at[a, j], recv_sem=recv_sems.at[a, j],
                    device_id=(x, y, 1 - c), device_id_type=MESH))
        for cp in copies:
            cp.start()
        for cp in copies:
            cp.wait_recv()
        for cp in copies:
            cp.wait_send()

    return pl.pallas_call(
        body, in_specs=[ANY] * n, out_specs=[ANY] * n,
        out_shape=[SDS((4,) + g.shape[1:], g.dtype) for g in gs],
        scratch_shapes=[pltpu.SemaphoreType.DMA((n, 4)), pltpu.SemaphoreType.DMA((n, 4))],
        name="reduce_scatter_sibling")(*gs)


def _add_pair(g, r1, core, name):
    _, rows, cols = g.shape
    tr = _row_tile(rows, cols, 3 * g.dtype.itemsize)

    def body(c_ref, g_ref, r_ref, o_ref):
        o_ref[...] = (g_ref[...].astype(F32) + r_ref[...].astype(F32)).astype(o_ref.dtype)

    return pl.pallas_call(
        body,
        grid_spec=pltpu.PrefetchScalarGridSpec(
            num_scalar_prefetch=1, grid=(4, rows // tr),
            in_specs=[pl.BlockSpec((1, tr, cols), lambda j, i, c_ref: (2 * j + c_ref[0], i, 0)),
                      pl.BlockSpec((1, tr, cols), lambda j, i, c_ref: (j, i, 0))],
            out_specs=pl.BlockSpec((1, tr, cols), lambda j, i, c_ref: (j, i, 0))),
        out_shape=SDS((4, rows, cols), g.dtype),
        compiler_params=_params("parallel", "parallel"), name=name)(core, g, r1)


def _row_tile(rows, cols, itemsize_total):
    budget = (4 << 20) // max(1, cols * itemsize_total)
    if rows <= budget:
        return rows
    t = rows
    while t > budget and t % 2 == 0 and (t // 2) % 16 == 0:
        t //= 2
    return t


def _adam_math(g, w, m, v):
    m_new = ADAM_B1 * m + (1.0 - ADAM_B1) * g
    v_new = ADAM_B2 * v + (1.0 - ADAM_B2) * (g * g)
    m_hat = m_new / (1.0 - ADAM_B1 ** ADAM_STEP)
    v_hat = v_new / (1.0 - ADAM_B2 ** ADAM_STEP)
    delta = -ADAM_LR * (m_hat / (jnp.sqrt(v_hat) + ADAM_EPS) + ADAM_WD * w)
    return delta, m_new, v_new


def _sum_adam(parts, w, m, v, name):
    rows, cols = w.shape
    nparts = parts.shape[0]
    tr = _row_tile(rows, cols, nparts * parts.dtype.itemsize + 7 * 4)

    def body(p_ref, w_ref, m_ref, v_ref, g_ref, d_ref, mo_ref, vo_ref):
        g = p_ref[0].astype(F32)
        for s in range(1, nparts):
            g = g + p_ref[s].astype(F32)
        delta, m_new, v_new = _adam_math(g, w_ref[...], m_ref[...], v_ref[...])
        g_ref[...] = g
        d_ref[...] = delta
        mo_ref[...] = m_new
        vo_ref[...] = v_new

    blk = pl.BlockSpec((tr, cols), lambda i: (i, 0))
    out = SDS((rows, cols), F32)
    return pl.pallas_call(
        body, grid=(rows // tr,),
        in_specs=[pl.BlockSpec((nparts, tr, cols), lambda i: (0, i, 0)), blk, blk, blk],
        out_specs=[blk, blk, blk, blk], out_shape=[out, out, out, out],
        compiler_params=_params("parallel"), name=name)(parts, w, m, v)


SMALL_ROWS = 72


def _small_allreduce_adam(gpart, w, m, v, dep=None):
    def body(g_ref, w_ref, m_ref, v_ref, go_ref, d_ref, mo_ref, vo_ref, gath, send_sems, recv_sems):
        x, y, c = lax.axis_index("x"), lax.axis_index("y"), lax.axis_index("c")
        me = 4 * x + 2 * y + c
        gath[me] = g_ref[...]
        copies = []
        for k in range(1, N_DEV):
            fx, fy, fc = (k >> 2) & 1, (k >> 1) & 1, k & 1
            peer = (x ^ fx, y ^ fy, c ^ fc)
            copies.append(pltpu.make_async_remote_copy(
                src_ref=gath.at[me], dst_ref=gath.at[me], send_sem=send_sems.at[k - 1], recv_sem=recv_sems.at[k - 1],
                device_id=peer, device_id_type=MESH))
        for cp in copies:
            cp.start()
        for cp in copies:
            cp.wait_recv()
        for cp in copies:
            cp.wait_send()
        g = gath[0]
        for d in range(1, N_DEV):
            g = g + gath[d]
        delta, m_new, v_new = _adam_math(g, w_ref[...], m_ref[...], v_ref[...])
        go_ref[...] = g
        d_ref[...] = delta
        mo_ref[...] = m_new
        vo_ref[...] = v_new

    vm = pl.BlockSpec(memory_space=pltpu.VMEM)
    out = SDS((SMALL_ROWS, 128), F32)
    body, dep_spec, dep_arg = _anchored(body, 4, dep)
    return pl.pallas_call(
        body, in_specs=[vm] * 4 + dep_spec, out_specs=[vm] * 4, out_shape=[out] * 4,
        scratch_shapes=[pltpu.VMEM((N_DEV, SMALL_ROWS, 128), F32), pltpu.SemaphoreType.DMA((N_DEV - 1,)),
                        pltpu.SemaphoreType.DMA((N_DEV - 1,))],
        name="small_allreduce_adam")(gpart, w, m, v, *dep_arg)


BIG = ("w_in", "conv_w", "w_conv_out", "w_attn_out", "w_o", "w_ffn_gate", "w_ffn_up", "w_ffn_down")
EARLY = ("w_in", "conv_w")
LATE_MERGE = ("w_conv_out", "w_attn_out", "w_o")
LATE_FFN = ("w_ffn_gate", "w_ffn_up", "w_ffn_down")
TRANSPOSED = ("w_in", "w_ffn_gate", "w_ffn_up")
COL_SHARDED = ("conv_w", "w_attn_out")
SMALL = ("norm1_g", "gate_b", "conv_b", "conv_ln_g", "conv_ln_b", "norm2_g", "norm_f_g")
WEIGHTS = ("norm1_g", "w_in", "gate_b", "conv_w", "conv_b", "conv_ln_g", "conv_ln_b", "w_conv_out", "w_attn_out", "w_o",
           "norm2_g", "w_ffn_gate", "w_ffn_up", "w_ffn_down", "norm_f_g")


def _shard2d(name, a):
    a = a.reshape(a.shape[-2], a.shape[-1])
    if name in TRANSPOSED:
        a = a.T
    if name == "conv_w":
        a = jnp.pad(a, ((0, CONV_PAD - CONV_K), (0, 0)))
    return a


def _from_shard2d(name, val, shape):
    if name in TRANSPOSED:
        val = val.T
    if name == "conv_w":
        val = val[:CONV_K]
    return val.reshape(shape)


def _gathered_to_full(name, g):
    if name in COL_SHARDED:
        return g.transpose(1, 0, 2).reshape(g.shape[1], N_DEV * g.shape[2])
    return g.reshape(N_DEV * g.shape[1], g.shape[2])


def _full_to_blocks(name, g):
    if name in COL_SHARDED:
        return g.reshape(g.shape[0], N_DEV, g.shape[1] // N_DEV).transpose(1, 0, 2)
    return g.reshape(N_DEV, g.shape[0] // N_DEV, g.shape[1])


def _pack_small(d, last_rows):
    vec = jnp.concatenate([d[n].reshape(-1) for n in SMALL]).reshape(SMALL_ROWS - SUBLANES, 128)
    return jnp.concatenate([vec, last_rows], axis=0)


def _unpack_small(p, like):
    flat = p.reshape(-1)
    out, off = {}, 0
    for n in SMALL:
        size = like[n].size
        out[n] = flat[off:off + size].reshape(like[n].shape)
        off += size
    return out


def kernel(x, norm1_g, w_in, gate_b, conv_w, conv_b, conv_ln_g, conv_ln_b, w_conv_out, w_attn_out, w_o, norm2_g, w_ffn_gate, w_ffn_up, w_ffn_down, norm_f_g, loss_target, m_norm1_g, m_w_in, m_gate_b, m_conv_w, m_conv_b, m_conv_ln_g, m_conv_ln_b, m_w_conv_out, m_w_attn_out, m_w_o, m_norm2_g, m_w_ffn_gate, m_w_ffn_up, m_w_ffn_down, m_norm_f_g, v_norm1_g, v_w_in, v_gate_b, v_conv_w, v_conv_b, v_conv_ln_g, v_conv_ln_b, v_w_conv_out, v_w_attn_out, v_w_o, v_norm2_g, v_w_ffn_gate, v_w_ffn_up, v_w_ffn_down, v_norm_f_g):
    wts = dict(norm1_g=norm1_g, w_in=w_in, gate_b=gate_b, conv_w=conv_w, conv_b=conv_b, conv_ln_g=conv_ln_g,
               conv_ln_b=conv_ln_b, w_conv_out=w_conv_out, w_attn_out=w_attn_out, w_o=w_o, norm2_g=norm2_g,
               w_ffn_gate=w_ffn_gate, w_ffn_up=w_ffn_up, w_ffn_down=w_ffn_down, norm_f_g=norm_f_g)
    mom1 = dict(norm1_g=m_norm1_g, w_in=m_w_in, gate_b=m_gate_b, conv_w=m_conv_w, conv_b=m_conv_b, conv_ln_g=m_conv_ln_g,
                conv_ln_b=m_conv_ln_b, w_conv_out=m_w_conv_out, w_attn_out=m_w_attn_out, w_o=m_w_o, norm2_g=m_norm2_g,
                w_ffn_gate=m_w_ffn_gate, w_ffn_up=m_w_ffn_up, w_ffn_down=m_w_ffn_down, norm_f_g=m_norm_f_g)
    mom2 = dict(norm1_g=v_norm1_g, w_in=v_w_in, gate_b=v_gate_b, conv_w=v_conv_w, conv_b=v_conv_b, conv_ln_g=v_conv_ln_g,
                conv_ln_b=v_conv_ln_b, w_conv_out=v_w_conv_out, w_attn_out=v_w_attn_out, w_o=v_w_o, norm2_g=v_norm2_g,
                w_ffn_gate=v_w_ffn_gate, w_ffn_up=v_w_ffn_up, w_ffn_down=v_w_ffn_down, norm_f_g=v_norm_f_g)

    T = x.shape[0] * x.shape[1]
    x2 = x.reshape(T, D_MODEL)
    t2 = loss_target.reshape(T, D_MODEL)

    me = 4 * lax.axis_index("x") + 2 * lax.axis_index("y") + lax.axis_index("c")
    shards = {n: _shard2d(n, wts[n]) for n in BIG}
    sent = {n: shards[n] if n == "conv_w" else shards[n].astype(BF16) for n in BIG}
    small = {n: wts[n].reshape(1, -1) for n in SMALL}

    gathered = _all_gather([sent[n] for n in EARLY])
    full = {n: _gathered_to_full(n, g) for n, g in zip(EARLY, gathered)}
    merge_gather = _send_start("gather", ALL_PEERS, "gather_start_merge", [sent[n] for n in LATE_MERGE])
    ffn_gather = _send_start("gather", (1,) + OTHER_CHIPS, "gather_start_ffn", [sent[n] for n in LATE_FFN])
    ffn_state = {}

    def filled(names, srcs, lands):
        return {n: _gathered_to_full(n, lax.dynamic_update_slice(land, src[None], (me, 0, 0)))
                for n, src, land in zip(names, srcs, lands)}

    def late_weights(names, after):
        if names is LATE_MERGE:
            srcs, lands = _send_wait(merge_gather, after, "gather_wait_merge")
            ffn_state["srcs"], ffn_lands = _send_wait(ffn_gather, after, "gather_wait_ffn")
            ffn_state["forward"] = _send_start("forward", OTHER_CHIPS, "forward_start_ffn", lands=ffn_lands)
            return {**filled(names, srcs, lands), "token": ffn_state["forward"]["token"]}
        _, lands = _send_wait(ffn_state["forward"], after, "forward_wait_ffn")
        return filled(names, ffn_state["srcs"], lands)

    scatters = []
    core = lax.axis_index("c").astype(jnp.int32).reshape(1)

    def emit(names, gw):
        blocks = [_full_to_blocks(n, gw[n]) for n in names]
        if "w_in" in names:
            sums = [_add_pair(g, r, core, "chip_sum_" + n) for n, g, r in zip(names, blocks, _exchange_sibling(blocks))]
            started = _send_start("chip_scatter", OTHER_CHIPS, "scatter_start_" + names[0], sums)
        else:
            started = _send_start("scatter", ALL_PEERS, "scatter_start_" + names[0], blocks)
        scatters.append((names, started))
        return started["token"]

    loss_part, grad_x, gw, gsmall = _local_step(x2, t2, full, small, ffn_gather["token"], late_weights, emit)

    grads, deltas, new_m, new_v = {}, {}, {}, {}
    after = grad_x
    for names, started in scatters:
        srcs, lands = _send_wait(started, after, "scatter_wait_" + names[0])
        mine = me >> 1 if started["mode"] == "chip_scatter" else me
        for n, src, land in zip(names, srcs, lands):
            parts = lax.dynamic_update_slice(land, lax.dynamic_slice_in_dim(src, mine, 1, axis=0), (mine, 0, 0))
            g, d, mo, vo = _sum_adam(parts, shards[n], _shard2d(n, mom1[n]), _shard2d(n, mom2[n]), "adam_" + n)
            for dst, val in ((grads, g), (deltas, d), (new_m, mo), (new_v, vo)):
                dst[n] = _from_shard2d(n, val, wts[n].shape)
            after = g

    zeros, ones = jnp.zeros((SUBLANES, 128), F32), jnp.ones((SUBLANES, 128), F32)
    sg, sd, sm, sv = _small_allreduce_adam(
        _pack_small(gsmall, jnp.broadcast_to(loss_part, (SUBLANES, 128))), _pack_small(wts, zeros),
        _pack_small(mom1, zeros), _pack_small(mom2, ones), after)
    for dst, val in ((grads, sg), (deltas, sd), (new_m, sm), (new_v, sv)):
        dst.update(_unpack_small(val, wts))
    loss = sg[SMALL_ROWS - SUBLANES, 0]
    return (loss, grad_x.reshape(x.shape), *[grads[n] for n in WEIGHTS], *[deltas[n] for n in WEIGHTS],
            *[new_m[n] for n in WEIGHTS], *[new_v[n] for n in WEIGHTS])
```

```python
import math

import numpy as np
import jax
import jax.numpy as jnp
from jax import lax
from jax.experimental import pallas as pl
from jax.experimental.pallas import tpu as pltpu

F32 = jnp.float32
BF16 = jnp.bfloat16
SDS = jax.ShapeDtypeStruct
MESH = pl.DeviceIdType.MESH

D_MODEL = 1024
SEQ = 2048
HEAD_DIM = 64
GROUPS = ((128, 1), (512, 4), (2048, 16))
HEADS_PER_GROUP = 8
N_HEADS = 24
ATTN_WIDTH = N_HEADS * HEAD_DIM
ATTN_OUT = HEADS_PER_GROUP * HEAD_DIM
CONV_K = 31
CONV_PAD = 32
D_FF = 2816
IN_WIDTH = 3 * ATTN_WIDTH + 2 * D_MODEL + 2 * D_MODEL
RMS_EPS = 1e-6
LN_EPS = 1e-5
Q_BLOCK = 128
LANES = 128
NEG = -1e30
N_DEV = 8

ADAM_LR = 0.001
ADAM_B1 = 0.9
ADAM_B2 = 0.999
ADAM_EPS = 1e-08
ADAM_WD = 0.01
ADAM_STEP = 10


def _alibi_slope_list(n):
    def pow2(m):
        start = 2.0 ** (-8.0 / m)
        return [start ** (i + 1) for i in range(m)]
    if math.log2(n).is_integer():
        return pow2(n)
    c = 2 ** math.floor(math.log2(n))
    return pow2(c) + _alibi_slope_list(2 * c)[0::2][: n - c]


def _slopes_times_dilation():
    s = np.asarray(sorted(_alibi_slope_list(N_HEADS), reverse=True), dtype=np.float32).reshape(3, HEADS_PER_GROUP)
    r = np.asarray([g[1] for g in GROUPS], dtype=np.float32)[:, None]
    return (s * r).reshape(N_HEADS)


def _sigmoid(x):
    return 1.0 / (1.0 + jnp.exp(-x))


def _dot(a, b):
    return jnp.dot(a, b, preferred_element_type=F32)


def _dot_nt(a, b):
    return lax.dot_general(a, b, (((1,), (1,)), ((), ())), preferred_element_type=F32)


def _dot_tn(a, b):
    return lax.dot_general(a, b, (((0,), (0,)), ((), ())), preferred_element_type=F32)


def _rowsum(x):
    return jnp.sum(x, axis=0, keepdims=True)


def _params(*sem):
    return pltpu.CompilerParams(dimension_semantics=sem)


def _anchored(body, n_in, dep):
    if dep is None:
        return body, [], []

    def wrapped(*refs):
        return body(*refs[:n_in], *refs[n_in + 1:])

    return wrapped, [pl.BlockSpec(memory_space=pl.ANY)], [dep]


IN_TM = 256
IN_CHUNK = 512


def _in_proj(x, g1, w_in_t, dep=None):
    T = x.shape[0]
    tm = IN_TM
    widths = (3 * ATTN_WIDTH, 2 * D_MODEL, 2 * D_MODEL)

    def body(x_ref, g_ref, w_hbm, h_ref, qkv_ref, u_ref, lg_ref, w_vmem, sem):
        @pl.when(pl.program_id(0) == 0)
        def _():
            cp = pltpu.make_async_copy(w_hbm, w_vmem, sem)
            cp.start()
            cp.wait()

        xv = x_ref[...]
        r = lax.rsqrt(jnp.mean(xv * xv, axis=-1, keepdims=True) + RMS_EPS)
        h = (xv * r * g_ref[...]).astype(BF16)
        h_ref[...] = h
        col = 0
        for o_ref, width in zip((qkv_ref, u_ref, lg_ref), widths):
            for j in range(width // IN_CHUNK):
                o_ref[:, j * IN_CHUNK:(j + 1) * IN_CHUNK] = _dot_nt(h, w_vmem[col:col + IN_CHUNK, :])
                col += IN_CHUNK

    row = lambda n: pl.BlockSpec((tm, n), lambda i: (i, 0))
    body, dep_spec, dep_arg = _anchored(body, 3, dep)
    return pl.pallas_call(
        body, grid=(T // tm,),
        in_specs=[row(D_MODEL), pl.BlockSpec((1, D_MODEL), lambda i: (0, 0)), pl.BlockSpec(memory_space=pl.ANY)] + dep_spec,
        out_specs=[row(D_MODEL)] + [row(n) for n in widths],
        out_shape=[SDS((T, D_MODEL), BF16)] + [SDS((T, n), F32) for n in widths],
        scratch_shapes=[pltpu.VMEM((IN_WIDTH, D_MODEL), BF16), pltpu.SemaphoreType.DMA],
        compiler_params=_params("arbitrary"), name="in_proj")(x, g1, w_in_t, *dep_arg)


def _mm_tn(a, b, out_dtype, name, tn, tt=1024):
    T, K = a.shape
    N = b.shape[1]
    nt = T // tt

    def body(a_ref, b_ref, o_ref, acc):
        t = pl.program_id(1)

        @pl.when(t == 0)
        def _():
            acc[...] = jnp.zeros_like(acc)

        acc[...] += _dot_tn(a_ref[...], b_ref[...])

        @pl.when(t == nt - 1)
        def _():
            o_ref[...] = acc[...].astype(o_ref.dtype)

    return pl.pallas_call(
        body, grid=(N // tn, nt),
        in_specs=[pl.BlockSpec((tt, K), lambda j, t: (t, 0)),
                  pl.BlockSpec((tt, tn), lambda j, t: (t, j))],
        out_specs=pl.BlockSpec((K, tn), lambda j, t: (0, j)),
        out_shape=SDS((K, N), out_dtype),
        scratch_shapes=[pltpu.VMEM((K, tn), F32)],
        compiler_params=_params("parallel", "arbitrary"), name=name)(a, b)


def _gather_classes(src_ref, dst, r, row0=0):
    L = SEQ // r
    for c in range(r):
        dst[row0 + c * L:row0 + (c + 1) * L, :] = src_ref[0, pl.ds(c, L, stride=r), :].astype(dst.dtype)


def _scatter_classes(src, dst, r, row0=0):
    L = SEQ // r
    for c in range(r):
        dst[pl.ds(c, L, stride=r), :] = src[row0 + c * L:row0 + (c + 1) * L, :].astype(dst.dtype)


def _attn_masks(slope_r):
    qi = lax.broadcasted_iota(jnp.int32, (Q_BLOCK, Q_BLOCK), 0)
    kj = lax.broadcasted_iota(jnp.int32, (Q_BLOCK, Q_BLOCK), 1)
    rel = (qi - kj).astype(F32)
    bias_cur = jnp.where(qi >= kj, -slope_r * rel, NEG)
    bias_prev = jnp.where(qi <= kj, -slope_r * (rel + float(Q_BLOCK)), NEG)
    return bias_cur, bias_prev


def _store_biases(bias, sl_ref, g, hp):
    for hh in range(2):
        cur, prev = _attn_masks(sl_ref[g * HEADS_PER_GROUP + 2 * hp + hh])
        rows = slice(hh * Q_BLOCK, (hh + 1) * Q_BLOCK)
        bias[0, rows, 0:Q_BLOCK] = prev
        bias[1, rows, 0:Q_BLOCK] = jnp.full((Q_BLOCK, Q_BLOCK), NEG, F32)
        bias[0, rows, Q_BLOCK:] = cur
        bias[1, rows, Q_BLOCK:] = cur


def _transpose_pairs(src, dst):
    dst[0, :, 0:Q_BLOCK] = jnp.zeros((LANES, Q_BLOCK), dst.dtype)
    nblk = SEQ // Q_BLOCK
    for b in range(nblk):
        t = src[(b + 1) * Q_BLOCK:(b + 2) * Q_BLOCK, :].T
        dst[b, :, Q_BLOCK:] = t
        if b + 1 < nblk:
            dst[b + 1, :, 0:Q_BLOCK] = t


def _stack_heads(t, low):
    z = jnp.zeros_like(t)
    return jnp.concatenate([jnp.where(low, t, z), jnp.where(low, z, t)], axis=0)


def _unstack_heads(t2, low):
    return jnp.where(low, t2[:Q_BLOCK], t2[Q_BLOCK:])


def _unit_offsets(u, nb):
    off = pl.multiple_of(u * Q_BLOCK, Q_BLOCK)
    n = u & (nb - 1)
    c = u >> int(math.log2(nb))
    return off, n == 0, c, n


ATTN_UNROLL = 4


def _attn_fwd(qkv, slopes_r, batch):
    nblk = SEQ // Q_BLOCK

    def body(sl_ref, *refs):
        qkv_refs = refs[:9]
        att_ref, lse_ref = refs[9:11]
        qd, kd, vd, kt, opos, lpos, bias = refs[11:]
        hp = pl.program_id(1)
        low = lax.broadcasted_iota(jnp.int32, (Q_BLOCK, LANES), 1) < HEAD_DIM

        for g in range(3):
            r = GROUPS[g][1]
            nb = SEQ // r // Q_BLOCK
            _gather_classes(qkv_refs[3 * g], qd, r)
            kd[0:Q_BLOCK, :] = jnp.zeros((Q_BLOCK, LANES), BF16)
            vd[0:Q_BLOCK, :] = jnp.zeros((Q_BLOCK, LANES), BF16)
            _gather_classes(qkv_refs[3 * g + 1], kd, r, Q_BLOCK)
            _gather_classes(qkv_refs[3 * g + 2], vd, r, Q_BLOCK)
            _transpose_pairs(kd, kt)
            _store_biases(bias, sl_ref, g, hp)

            def unit(u, carry, g=g, r=r, nb=nb):
                off, first, c, n = _unit_offsets(u, nb)
                q2 = _stack_heads(qd[pl.ds(off, Q_BLOCK), :], low)
                s = _dot(q2, kt[u]) * 0.125 + bias[first.astype(jnp.int32)]
                m = jnp.max(s, axis=-1, keepdims=True)
                p = jnp.exp(s - m)
                l = jnp.sum(p, axis=-1, keepdims=True)
                o2 = _dot(p.astype(BF16), vd[pl.ds(off, 2 * Q_BLOCK), :]) * (1.0 / l)
                lse2 = m + jnp.log(l)
                rows = pl.ds(c + n * (Q_BLOCK * r), Q_BLOCK, stride=r)
                opos[g, rows, :] = _unstack_heads(o2, low)
                lpos[g, rows, :] = jnp.where(low, lse2[:Q_BLOCK], lse2[Q_BLOCK:])
                return carry

            lax.fori_loop(0, nblk, unit, 0, unroll=ATTN_UNROLL)

        def merge(i, carry):
            rows = pl.ds(pl.multiple_of(i * 256, 256), 256)
            l0, l1, l2 = lpos[0, rows, :], lpos[1, rows, :], lpos[2, rows, :]
            m = jnp.maximum(jnp.maximum(l0, l1), l2)
            e0, e1, e2 = jnp.exp(l0 - m), jnp.exp(l1 - m), jnp.exp(l2 - m)
            den = e0 + e1 + e2
            att = (e0 * opos[0, rows, :] + e1 * opos[1, rows, :] + e2 * opos[2, rows, :]) / den
            att_ref[0, rows, :] = att.astype(att_ref.dtype)
            lse_ref[0, rows, :] = m + jnp.log(den)
            return carry

        lax.fori_loop(0, SEQ // 256, merge, 0)

    def col(sec, g):
        return pl.BlockSpec((1, SEQ, LANES), lambda b, hp: (b, 0, sec * 12 + g * 4 + hp))

    out = pl.BlockSpec((1, SEQ, LANES), lambda b, hp: (b, 0, hp))
    return pl.pallas_call(
        body, grid=(batch, 4),
        in_specs=[pl.BlockSpec(memory_space=pltpu.SMEM)] + [col(sec, g) for g in range(3) for sec in range(3)],
        out_specs=[out, out],
        out_shape=[SDS((batch, SEQ, ATTN_OUT), BF16), SDS((batch, SEQ, ATTN_OUT), F32)],
        scratch_shapes=[pltpu.VMEM((SEQ, LANES), BF16), pltpu.VMEM((Q_BLOCK + SEQ, LANES), BF16),
                        pltpu.VMEM((Q_BLOCK + SEQ, LANES), BF16), pltpu.VMEM((nblk, LANES, 2 * Q_BLOCK), BF16),
                        pltpu.VMEM((3, SEQ, LANES), F32), pltpu.VMEM((3, SEQ, LANES), F32),
                        pltpu.VMEM((2, 2 * Q_BLOCK, 2 * Q_BLOCK), F32)],
        compiler_params=_params("parallel", "parallel"), name="attn_fwd")(slopes_r, *([qkv] * 9))


def _attn_bwd(qkv, datt, lse, dsum, slopes_r, batch):
    nblk = SEQ // Q_BLOCK

    def body(sl_ref, q_ref, k_ref, v_ref, do_ref, l_ref, d_ref, dq_ref, dk_ref, dv_ref,
             qd, kd, vd, dod, kt, vt, ld, dd, dq_acc, dk_acc, dv_acc, dk_part, dv_part, stage, bias):
        gid, hp = pl.program_id(1), pl.program_id(2)
        low = lax.broadcasted_iota(jnp.int32, (Q_BLOCK, LANES), 1) < HEAD_DIM

        def section(g):
            r = GROUPS[g][1]
            nb = SEQ // r // Q_BLOCK
            _gather_classes(q_ref, qd, r)
            kd[0:Q_BLOCK, :] = jnp.zeros((Q_BLOCK, LANES), BF16)
            vd[0:Q_BLOCK, :] = jnp.zeros((Q_BLOCK, LANES), BF16)
            _gather_classes(k_ref, kd, r, Q_BLOCK)
            _gather_classes(v_ref, vd, r, Q_BLOCK)
            _gather_classes(do_ref, dod, r)
            _gather_classes(l_ref, ld, r)
            _gather_classes(d_ref, dd, r)
            _transpose_pairs(kd, kt)
            _transpose_pairs(vd, vt)
            _store_biases(bias, sl_ref, g, hp)

            def unit(u, carry):
                off, first, _, _ = _unit_offsets(u, nb)
                pair = pl.ds(off, 2 * Q_BLOCK)
                q2 = _stack_heads(qd[pl.ds(off, Q_BLOCK), :], low)
                do2 = _stack_heads(dod[pl.ds(off, Q_BLOCK), :], low)
                lse_t = ld[pl.ds(off, Q_BLOCK), :]
                dsum_t = dd[pl.ds(off, Q_BLOCK), :]
                lse2 = jnp.concatenate([lse_t[:, 0:1], lse_t[:, HEAD_DIM:HEAD_DIM + 1]], axis=0)
                dsum2 = jnp.concatenate([dsum_t[:, 0:1], dsum_t[:, HEAD_DIM:HEAD_DIM + 1]], axis=0)
                s = _dot(q2, kt[u]) * 0.125 + bias[first.astype(jnp.int32)]
                p = jnp.exp(s - lse2)
                ds = (p * (_dot(do2, vt[u]) - dsum2)).astype(BF16)
                dq_acc[pl.ds(off, Q_BLOCK), :] = _unstack_heads(_dot(ds, kd[pair, :]), low) * 0.125
                dk_part[u] = _dot_tn(ds, q2) * 0.125
                dv_part[u] = _dot_tn(p.astype(BF16), do2)
                return carry

            lax.fori_loop(0, nblk, unit, 0, unroll=ATTN_UNROLL)
            for part, acc in ((dk_part, dk_acc), (dv_part, dv_acc)):
                for b in range(nblk):
                    t = part[b, Q_BLOCK:, :]
                    if b + 1 < nblk:
                        t = t + part[b + 1, 0:Q_BLOCK, :]
                    acc[b * Q_BLOCK:(b + 1) * Q_BLOCK, :] = t
            for acc, out_ref in ((dq_acc, dq_ref), (dk_acc, dk_ref), (dv_acc, dv_ref)):
                _scatter_classes(acc, stage, r)
                out_ref[0] = stage[...].astype(out_ref.dtype)

        for g in range(3):
            pl.when(gid == g)(lambda g=g: section(g))

    def col(sec):
        return pl.BlockSpec((1, SEQ, LANES), lambda b, g, hp: (b, 0, sec * 12 + g * 4 + hp))

    pos = pl.BlockSpec((1, SEQ, LANES), lambda b, g, hp: (b, 0, hp))
    dout = pl.BlockSpec((1, SEQ, LANES), lambda b, g, hp: (b, 0, g * 4 + hp))
    out = SDS((batch, SEQ, ATTN_WIDTH), BF16)
    seq_bf = pltpu.VMEM((SEQ, LANES), BF16)
    seq_f = pltpu.VMEM((SEQ, LANES), F32)
    pad_bf = pltpu.VMEM((Q_BLOCK + SEQ, LANES), BF16)
    part = pltpu.VMEM((nblk, 2 * Q_BLOCK, LANES), F32)
    blk_t = pltpu.VMEM((nblk, LANES, 2 * Q_BLOCK), BF16)
    return pl.pallas_call(
        body, grid=(batch, 3, 4),
        in_specs=[pl.BlockSpec(memory_space=pltpu.SMEM), col(0), col(1), col(2), pos, pos, pos],
        out_specs=[dout, dout, dout],
        out_shape=[out, out, out],
        scratch_shapes=[seq_bf, pad_bf, pad_bf, seq_bf, blk_t, blk_t, seq_f, seq_f, seq_f, seq_f, seq_f, part, part, seq_f,
                        pltpu.VMEM((2, 2 * Q_BLOCK, 2 * Q_BLOCK), F32)],
        compiler_params=_params("parallel", "parallel", "parallel"), name="attn_bwd")(
            slopes_r, qkv, qkv, qkv, datt, lse, dsum)


CONV_TC = 128
CONV_ROWS = 128
SUBLANES = 8


def _fill_shifted(sh):
    n = SEQ + CONV_PAD - SUBLANES
    for s in range(1, SUBLANES):
        sh[s, 0:n, :] = sh[0, s:s + n, :]


def _tap(sh, base, offset):
    s = offset % SUBLANES
    return sh[s, pl.ds(pl.multiple_of(base + (offset - s), SUBLANES), CONV_ROWS), :]


def _conv_fwd(u, conv_w, conv_b, batch):
    nct = D_MODEL // CONV_TC

    def body(ua_ref, ub_ref, w_ref, b_ref, o_ref, sh):
        sh[0, 0:CONV_PAD, :] = jnp.zeros((CONV_PAD, CONV_TC), F32)
        sh[0, CONV_PAD:, :] = ua_ref[0] * _sigmoid(ub_ref[0])
        _fill_shifted(sh)

        def chunk(c, carry):
            base = pl.multiple_of(c * CONV_ROWS, CONV_ROWS)
            acc = jnp.broadcast_to(b_ref[...], (CONV_ROWS, CONV_TC))
            for t in range(CONV_K):
                acc = acc + _tap(sh, base, t + CONV_PAD - (CONV_K - 1)) * w_ref[t:t + 1, :]
            o_ref[0, pl.ds(base, CONV_ROWS), :] = acc
            return carry

        lax.fori_loop(0, SEQ // CONV_ROWS, chunk, 0)

    return pl.pallas_call(
        body, grid=(nct, batch),
        in_specs=[pl.BlockSpec((1, SEQ, CONV_TC), lambda j, b: (b, 0, j)),
                  pl.BlockSpec((1, SEQ, CONV_TC), lambda j, b: (b, 0, j + nct)),
                  pl.BlockSpec((CONV_PAD, CONV_TC), lambda j, b: (0, j)),
                  pl.BlockSpec((1, CONV_TC), lambda j, b: (0, j))],
        out_specs=pl.BlockSpec((1, SEQ, CONV_TC), lambda j, b: (b, 0, j)),
        out_shape=SDS((batch, SEQ, D_MODEL), F32),
        scratch_shapes=[pltpu.VMEM((SUBLANES, SEQ + CONV_PAD, CONV_TC), F32)],
        compiler_params=_params("parallel", "parallel"), name="conv_fwd")(u, u, conv_w, conv_b)


def _conv_bwd(u, dc1, conv_w, batch, dep=None):
    nct = D_MODEL // CONV_TC
    nchunk = SEQ // CONV_ROWS

    def body(ua_ref, ub_ref, d_ref, w_ref, dua_ref, dub_ref, gw_ref, gb_ref, shc, shd, gacc):
        b = pl.program_id(1)
        shc[0, 0:CONV_PAD, :] = jnp.zeros((CONV_PAD, CONV_TC), F32)
        shc[0, CONV_PAD:, :] = ua_ref[0] * _sigmoid(ub_ref[0])
        _fill_shifted(shc)
        shd[0, 0:SEQ, :] = d_ref[0]
        shd[0, SEQ:, :] = jnp.zeros((CONV_PAD, CONV_TC), F32)
        _fill_shifted(shd)

        @pl.when(b == 0)
        def _():
            gacc[...] = jnp.zeros_like(gacc)
            gb_ref[...] = jnp.zeros_like(gb_ref)

        gb_ref[...] += _rowsum(d_ref[0])

        def chunk(c, carry):
            base = pl.multiple_of(c * CONV_ROWS, CONV_ROWS)
            dcur = shd[0, pl.ds(base, CONV_ROWS), :]
            acc = jnp.zeros((CONV_ROWS, CONV_TC), F32)
            for t in range(CONV_K):
                acc = acc + _tap(shd, base, CONV_K - 1 - t) * w_ref[t:t + 1, :]
                prod = _tap(shc, base, t + CONV_PAD - (CONV_K - 1)) * dcur
                gacc[t] += jnp.sum(prod.reshape(CONV_ROWS // 8, 8, CONV_TC), axis=0)
            ua = ua_ref[0, pl.ds(base, CONV_ROWS), :]
            sg = _sigmoid(ub_ref[0, pl.ds(base, CONV_ROWS), :])
            dua_ref[0, pl.ds(base, CONV_ROWS), :] = (acc * sg).astype(dua_ref.dtype)
            dub_ref[0, pl.ds(base, CONV_ROWS), :] = (acc * ua * sg * (1.0 - sg)).astype(dub_ref.dtype)
            return carry

        lax.fori_loop(0, nchunk, chunk, 0)

        @pl.when(b == batch - 1)
        def _():
            for t in range(CONV_K):
                gw_ref[t:t + 1, :] = jnp.sum(gacc[t], axis=0, keepdims=True)
            gw_ref[CONV_K:CONV_PAD, :] = jnp.zeros((CONV_PAD - CONV_K, CONV_TC), F32)

    du = SDS((batch, SEQ, D_MODEL), BF16)
    body, dep_spec, dep_arg = _anchored(body, 4, dep)
    return pl.pallas_call(
        body, grid=(nct, batch),
        in_specs=[pl.BlockSpec((1, SEQ, CONV_TC), lambda j, b: (b, 0, j)),
                  pl.BlockSpec((1, SEQ, CONV_TC), lambda j, b: (b, 0, j + nct)),
                  pl.BlockSpec((1, SEQ, CONV_TC), lambda j, b: (b, 0, j)),
                  pl.BlockSpec((CONV_PAD, CONV_TC), lambda j, b: (0, j))] + dep_spec,
        out_specs=[pl.BlockSpec((1, SEQ, CONV_TC), lambda j, b: (b, 0, j)),
                   pl.BlockSpec((1, SEQ, CONV_TC), lambda j, b: (b, 0, j)),
                   pl.BlockSpec((CONV_PAD, CONV_TC), lambda j, b: (0, j)),
                   pl.BlockSpec((1, CONV_TC), lambda j, b: (0, j))],
        out_shape=[du, du, SDS((CONV_PAD, D_MODEL), F32), SDS((1, D_MODEL), F32)],
        scratch_shapes=[pltpu.VMEM((SUBLANES, SEQ + CONV_PAD, CONV_TC), F32),
                        pltpu.VMEM((SUBLANES, SEQ + CONV_PAD, CONV_TC), F32),
                        pltpu.VMEM((CONV_K, 8, CONV_TC), F32)],
        compiler_params=_params("parallel", "arbitrary"), name="conv_bwd")(u, u, dc1, conv_w, *dep_arg)


MID_TM = 256


def _layernorm_stats(c1):
    mu = jnp.mean(c1, axis=-1, keepdims=True)
    cen = c1 - mu
    rs = lax.rsqrt(jnp.mean(cen * cen, axis=-1, keepdims=True) + LN_EPS)
    return cen * rs, rs


def _mid_fwd(att, c1, logits, x, w_a, w_c, w_o, gate_b, ln_g, ln_b, g2, dep=None):
    T = x.shape[0]
    tm = MID_TM

    def body(att_ref, c1_ref, lg_ref, x_ref, wa_ref, wc_ref, wo_ref, gb_ref, lng_ref, lnb_ref, g2_ref,
             c3_ref, ya_ref, yc_ref, mix_ref, x1_ref, h2_ref):
        ya = _dot(att_ref[...], wa_ref[...])
        xh, _ = _layernorm_stats(c1_ref[...])
        c2 = xh * lng_ref[...] + lnb_ref[...]
        c3 = (c2 * _sigmoid(c2)).astype(BF16)
        c3_ref[...] = c3
        yc = _dot(c3, wc_ref[...])
        gates = _sigmoid(lg_ref[...] + gb_ref[...])
        mix = (gates[:, :D_MODEL] * ya + gates[:, D_MODEL:] * yc).astype(BF16)
        ya_ref[...] = ya.astype(BF16)
        yc_ref[...] = yc.astype(BF16)
        mix_ref[...] = mix
        x1 = x_ref[...] + _dot(mix, wo_ref[...])
        x1_ref[...] = x1
        r = lax.rsqrt(jnp.mean(x1 * x1, axis=-1, keepdims=True) + RMS_EPS)
        h2_ref[...] = (x1 * r * g2_ref[...]).astype(BF16)

    row = lambda n: pl.BlockSpec((tm, n), lambda i: (i, 0))
    full = lambda a, b: pl.BlockSpec((a, b), lambda i: (0, 0))
    body, dep_spec, dep_arg = _anchored(body, 11, dep)
    return pl.pallas_call(
        body, grid=(T // tm,),
        in_specs=[row(ATTN_OUT), row(D_MODEL), row(2 * D_MODEL), row(D_MODEL),
                  full(ATTN_OUT, D_MODEL), full(D_MODEL, D_MODEL), full(D_MODEL, D_MODEL),
                  full(1, 2 * D_MODEL), full(1, D_MODEL), full(1, D_MODEL), full(1, D_MODEL)] + dep_spec,
        out_specs=[row(D_MODEL), row(D_MODEL), row(D_MODEL), row(D_MODEL), row(D_MODEL), row(D_MODEL)],
        out_shape=[SDS((T, D_MODEL), BF16), SDS((T, D_MODEL), BF16), SDS((T, D_MODEL), BF16), SDS((T, D_MODEL), BF16),
                   SDS((T, D_MODEL), F32), SDS((T, D_MODEL), BF16)],
        compiler_params=_params("parallel"), name="mid_fwd")(att, c1, logits, x, w_a, w_c, w_o, gate_b, ln_g, ln_b, g2,
                                                             *dep_arg)


def _mid_bwd(dx1b, ya, yc, logits, att, c1, w_a, w_c, w_o, gate_b, ln_g, ln_b, head_ones, dep=None):
    T = dx1b.shape[0]
    tm = MID_TM

    def body(dx_ref, ya_ref, yc_ref, lg_ref, att_ref, c1_ref, wa_ref, wc_ref, wo_ref, gb_ref, lng_ref, lnb_ref, e_ref,
             dlg_ref, dya_ref, dyc_ref, datt_ref, dsum_ref, dc1_ref, ggb_ref, glg_ref, glb_ref):
        @pl.when(pl.program_id(0) == 0)
        def _():
            ggb_ref[...] = jnp.zeros_like(ggb_ref)
            glg_ref[...] = jnp.zeros_like(glg_ref)
            glb_ref[...] = jnp.zeros_like(glb_ref)

        dmix = _dot_nt(dx_ref[...], wo_ref[...])
        gates = _sigmoid(lg_ref[...] + gb_ref[...])
        ga, gc = gates[:, :D_MODEL], gates[:, D_MODEL:]
        dla = dmix * ya_ref[...].astype(F32) * ga * (1.0 - ga)
        dlc = dmix * yc_ref[...].astype(F32) * gc * (1.0 - gc)
        dlg_ref[:, :D_MODEL] = dla.astype(BF16)
        dlg_ref[:, D_MODEL:] = dlc.astype(BF16)
        ggb_ref[:, :D_MODEL] += _rowsum(dla)
        ggb_ref[:, D_MODEL:] += _rowsum(dlc)
        dya = (dmix * ga).astype(BF16)
        dyc = (dmix * gc).astype(BF16)
        dya_ref[...] = dya
        dyc_ref[...] = dyc
        datt = _dot_nt(dya, wa_ref[...])
        datt_ref[...] = datt
        dsum_ref[...] = jnp.dot(datt * att_ref[...].astype(F32), e_ref[...], preferred_element_type=F32,
                                precision=lax.Precision.HIGHEST)
        dc3 = _dot_nt(dyc, wc_ref[...])
        xh, rs = _layernorm_stats(c1_ref[...])
        c2 = xh * lng_ref[...] + lnb_ref[...]
        sg = _sigmoid(c2)
        dc2 = dc3 * (sg * (1.0 + c2 * (1.0 - sg)))
        glg_ref[...] += _rowsum(dc2 * xh)
        glb_ref[...] += _rowsum(dc2)
        dxh = dc2 * lng_ref[...]
        dc1_ref[...] = rs * (dxh - jnp.mean(dxh, axis=-1, keepdims=True) - xh * jnp.mean(dxh * xh, axis=-1, keepdims=True))

    row = lambda n: pl.BlockSpec((tm, n), lambda i: (i, 0))
    full = lambda a, b: pl.BlockSpec((a, b), lambda i: (0, 0))
    body, dep_spec, dep_arg = _anchored(body, 13, dep)
    return pl.pallas_call(
        body, grid=(T // tm,),
        in_specs=[row(D_MODEL), row(D_MODEL), row(D_MODEL), row(2 * D_MODEL), row(ATTN_OUT), row(D_MODEL),
                  full(ATTN_OUT, D_MODEL), full(D_MODEL, D_MODEL), full(D_MODEL, D_MODEL),
                  full(1, 2 * D_MODEL), full(1, D_MODEL), full(1, D_MODEL), full(ATTN_OUT, ATTN_OUT)] + dep_spec,
        out_specs=[row(2 * D_MODEL), row(D_MODEL), row(D_MODEL), row(ATTN_OUT), row(ATTN_OUT), row(D_MODEL),
                   full(1, 2 * D_MODEL), full(1, D_MODEL), full(1, D_MODEL)],
        out_shape=[SDS((T, 2 * D_MODEL), BF16), SDS((T, D_MODEL), BF16), SDS((T, D_MODEL), BF16), SDS((T, ATTN_OUT), F32),
                   SDS((T, ATTN_OUT), F32), SDS((T, D_MODEL), F32),
                   SDS((1, 2 * D_MODEL), F32), SDS((1, D_MODEL), F32), SDS((1, D_MODEL), F32)],
        compiler_params=_params("arbitrary"), name="mid_bwd")(dx1b, ya, yc, logits, att, c1, w_a, w_c, w_o, gate_b, ln_g, ln_b,
                                                               head_ones, *dep_arg)


FFN_TM = 512
FFN_TF = D_FF // 2


def _rms_bwd(dy_times_g, xh, r):
    return r * (dy_times_g - xh * jnp.mean(dy_times_g * xh, axis=-1, keepdims=True))


def _ffn_fwd(h2, x1, target, gf, w_g_t, w_u_t, w_d):
    T = h2.shape[0]
    tm, tf = FFN_TM, FFN_TF
    nf = D_FF // tf

    def body(h_ref, x1_ref, t_ref, gf_ref, wg_ref, wu_ref, wd_ref,
             a_ref, b_ref, f_ref, dx2_ref, dx2b_ref, loss_ref, gnf_ref, acc):
        i, j = pl.program_id(0), pl.program_id(1)
        h = h_ref[...]
        a = _dot_nt(h, wg_ref[...])
        b = _dot_nt(h, wu_ref[...])
        f = (a * _sigmoid(a) * b).astype(BF16)
        a_ref[...] = a.astype(BF16)
        b_ref[...] = b.astype(BF16)
        f_ref[...] = f
        p = _dot(f, wd_ref[...])

        @pl.when(j == 0)
        def _():
            acc[...] = x1_ref[...] + p

        @pl.when(j > 0)
        def _():
            acc[...] += p

        @pl.when((i == 0) & (j == nf - 1))
        def _():
            loss_ref[...] = jnp.zeros_like(loss_ref)
            gnf_ref[...] = jnp.zeros_like(gnf_ref)

        @pl.when(j == nf - 1)
        def _():
            x2 = acc[...]
            r = lax.rsqrt(jnp.mean(x2 * x2, axis=-1, keepdims=True) + RMS_EPS)
            xh = x2 * r
            err = xh * gf_ref[...] - t_ref[...]
            loss_ref[...] += (0.5 / D_MODEL) * jnp.sum(err * err)
            dy = err * (1.0 / D_MODEL)
            gnf_ref[...] += _rowsum(dy * xh)
            dx2 = _rms_bwd(dy * gf_ref[...], xh, r)
            dx2_ref[...] = dx2
            dx2b_ref[...] = dx2.astype(BF16)

    row = lambda n: pl.BlockSpec((tm, n), lambda i, j: (i, 0))
    ffb = pl.BlockSpec((tm, tf), lambda i, j: (i, j))
    wblk = pl.BlockSpec((tf, D_MODEL), lambda i, j: (j, 0))
    return pl.pallas_call(
        body, grid=(T // tm, nf),
        in_specs=[row(D_MODEL), row(D_MODEL), row(D_MODEL), pl.BlockSpec((1, D_MODEL), lambda i, j: (0, 0)),
                  wblk, wblk, wblk],
        out_specs=[ffb, ffb, ffb, row(D_MODEL), row(D_MODEL),
                   pl.BlockSpec((1, 128), lambda i, j: (0, 0)), pl.BlockSpec((1, D_MODEL), lambda i, j: (0, 0))],
        out_shape=[SDS((T, D_FF), BF16), SDS((T, D_FF), BF16), SDS((T, D_FF), BF16), SDS((T, D_MODEL), F32),
                   SDS((T, D_MODEL), BF16), SDS((1, 128), F32), SDS((1, D_MODEL), F32)],
        scratch_shapes=[pltpu.VMEM((tm, D_MODEL), F32)],
        compiler_params=_params("arbitrary", "arbitrary"), name="ffn_fwd")(h2, x1, target, gf, w_g_t, w_u_t, w_d)


def _ffn_bwd(dx2b, dx2, a, b, x1, g2, w_g_t, w_u_t, w_d):
    T = dx2.shape[0]
    tm, tf = FFN_TM, FFN_TF
    nf = D_FF // tf

    def body(dxb_ref, dx2_ref, a_ref, b_ref, x1_ref, g2_ref, wg_ref, wu_ref, wd_ref,
             da_ref, db_ref, dx1_ref, dx1b_ref, gn2_ref, acc):
        i, j = pl.program_id(0), pl.program_id(1)
        df = _dot_nt(dxb_ref[...], wd_ref[...])
        av = a_ref[...].astype(F32)
        bv = b_ref[...].astype(F32)
        sg = _sigmoid(av)
        db = (df * av * sg).astype(BF16)
        da = (df * bv * (sg * (1.0 + av * (1.0 - sg)))).astype(BF16)
        da_ref[...] = da
        db_ref[...] = db
        p = _dot(da, wg_ref[...]) + _dot(db, wu_ref[...])

        @pl.when(j == 0)
        def _():
            acc[...] = p

        @pl.when(j > 0)
        def _():
            acc[...] += p

        @pl.when((i == 0) & (j == nf - 1))
        def _():
            gn2_ref[...] = jnp.zeros_like(gn2_ref)

        @pl.when(j == nf - 1)
        def _():
            dh2 = acc[...]
            x1 = x1_ref[...]
            r = lax.rsqrt(jnp.mean(x1 * x1, axis=-1, keepdims=True) + RMS_EPS)
            xh = x1 * r
            gn2_ref[...] += _rowsum(dh2 * xh)
            dx1 = dx2_ref[...] + _rms_bwd(dh2 * g2_ref[...], xh, r)
            dx1_ref[...] = dx1
            dx1b_ref[...] = dx1.astype(BF16)

    row = lambda n: pl.BlockSpec((tm, n), lambda i, j: (i, 0))
    ffb = pl.BlockSpec((tm, tf), lambda i, j: (i, j))
    wblk = pl.BlockSpec((tf, D_MODEL), lambda i, j: (j, 0))
    return pl.pallas_call(
        body, grid=(T // tm, nf),
        in_specs=[row(D_MODEL), row(D_MODEL), ffb, ffb, row(D_MODEL), pl.BlockSpec((1, D_MODEL), lambda i, j: (0, 0)),
                  wblk, wblk, wblk],
        out_specs=[ffb, ffb, row(D_MODEL), row(D_MODEL), pl.BlockSpec((1, D_MODEL), lambda i, j: (0, 0))],
        out_shape=[SDS((T, D_FF), BF16), SDS((T, D_FF), BF16), SDS((T, D_MODEL), F32), SDS((T, D_MODEL), BF16),
                   SDS((1, D_MODEL), F32)],
        scratch_shapes=[pltpu.VMEM((tm, D_MODEL), F32)],
        compiler_params=_params("arbitrary", "arbitrary"), name="ffn_bwd")(dx2b, dx2, a, b, x1, g2, w_g_t, w_u_t, w_d)


def _in_bwd(pieces, w_in_t, x, dx1, g1, dep=None):
    T = x.shape[0]
    tm = IN_TM
    npc = len(pieces)
    assert sum(p.shape[1] for p in pieces) == IN_WIDTH

    def body(*refs):
        p_refs = refs[:npc]
        w_hbm, x_ref, dx1_ref, g_ref, dx_ref, gn1_ref, w_vmem, sem = refs[npc:]

        @pl.when(pl.program_id(0) == 0)
        def _():
            cp = pltpu.make_async_copy(w_hbm, w_vmem, sem)
            cp.start()
            cp.wait()
            gn1_ref[...] = jnp.zeros_like(gn1_ref)

        dh = jnp.zeros((tm, D_MODEL), F32)
        col = 0
        for p_ref in p_refs:
            for j in range(p_ref.shape[1] // IN_CHUNK):
                dh = dh + _dot(p_ref[:, j * IN_CHUNK:(j + 1) * IN_CHUNK], w_vmem[col:col + IN_CHUNK, :])
                col += IN_CHUNK
        xv = x_ref[...]
        r = lax.rsqrt(jnp.mean(xv * xv, axis=-1, keepdims=True) + RMS_EPS)
        xh = xv * r
        gn1_ref[...] += _rowsum(dh * xh)
        dx_ref[...] = dx1_ref[...] + _rms_bwd(dh * g_ref[...], xh, r)

    row = lambda n: pl.BlockSpec((tm, n), lambda i: (i, 0))
    body, dep_spec, dep_arg = _anchored(body, npc + 4, dep)
    return pl.pallas_call(
        body, grid=(T // tm,),
        in_specs=[row(p.shape[1]) for p in pieces]
        + [pl.BlockSpec(memory_space=pl.ANY), row(D_MODEL), row(D_MODEL), pl.BlockSpec((1, D_MODEL), lambda i: (0, 0))]
        + dep_spec,
        out_specs=[row(D_MODEL), pl.BlockSpec((1, D_MODEL), lambda i: (0, 0))],
        out_shape=[SDS((T, D_MODEL), F32), SDS((1, D_MODEL), F32)],
        scratch_shapes=[pltpu.VMEM((IN_WIDTH, D_MODEL), BF16), pltpu.SemaphoreType.DMA],
        compiler_params=_params("arbitrary"), name="in_bwd")(*pieces, w_in_t, x, dx1, g1, *dep_arg)


def _local_step(x, target, w, small, dep=None, late_weights=None, emit=None):
    T = x.shape[0]
    batch = T // SEQ
    slopes_r = jnp.asarray(_slopes_times_dilation())
    emit = emit or (lambda names, grads: None)

    h, qkv, u, logits = _in_proj(x, small["norm1_g"], w["w_in"], dep)

    qkv3 = qkv.reshape(batch, SEQ, 3 * ATTN_WIDTH)
    att, lse = _attn_fwd(qkv3, slopes_r, batch)
    att = att.reshape(T, ATTN_OUT)

    u3 = u.reshape(batch, SEQ, 2 * D_MODEL)
    c1 = _conv_fwd(u3, w["conv_w"], small["conv_b"], batch).reshape(T, D_MODEL)
    if late_weights is not None:
        w = {**w, **late_weights(LATE_MERGE, (att, c1))}

    c3, ya, yc, mix, x1, h2 = _mid_fwd(
        att, c1, logits, x, w["w_attn_out"], w["w_conv_out"], w["w_o"],
        small["gate_b"], small["conv_ln_g"], small["conv_ln_b"], small["norm2_g"], w.get("token"))
    if late_weights is not None:
        w = {**w, **late_weights(LATE_FFN, h2)}

    a, b, f, dx2, dx2b, loss, g_normf = _ffn_fwd(h2, x1, target, small["norm_f_g"],
                                                   w["w_ffn_gate"], w["w_ffn_up"], w["w_ffn_down"])

    da, db, dx1, dx1b, g_norm2 = _ffn_bwd(dx2b, dx2, a, b, x1, small["norm2_g"],
                                           w["w_ffn_gate"], w["w_ffn_up"], w["w_ffn_down"])
    gw = {}
    gw["w_ffn_down"] = _mm_tn(f, dx2b, BF16, "gw_ffn_down", tn=512)
    gw["w_ffn_gate"] = _mm_tn(da, h2, BF16, "gw_ffn_gate", tn=512)
    gw["w_ffn_up"] = _mm_tn(db, h2, BF16, "gw_ffn_up", tn=512)
    token = emit(("w_ffn_gate", "w_ffn_up", "w_ffn_down"), gw)

    head_ones = jnp.asarray(np.kron(np.eye(HEADS_PER_GROUP, dtype=np.float32), np.ones((HEAD_DIM, HEAD_DIM), np.float32)))
    dlogits, dya, dyc, datt, dsum, dc1, g_gate_b, g_ln_g, g_ln_b = _mid_bwd(
        dx1b, ya, yc, logits, att, c1, w["w_attn_out"], w["w_conv_out"], w["w_o"],
        small["gate_b"], small["conv_ln_g"], small["conv_ln_b"], head_ones, token)
    gw["w_o"] = _mm_tn(mix, dx1b, BF16, "gw_o", tn=512)
    gw["w_attn_out"] = _mm_tn(att, dya, BF16, "gw_attn_out", tn=512)
    gw["w_conv_out"] = _mm_tn(c3, dyc, BF16, "gw_conv_out", tn=512)
    token = emit(("w_conv_out", "w_attn_out", "w_o"), gw)

    dua, dub, g_conv_w, g_conv_b = _conv_bwd(u3, dc1.reshape(batch, SEQ, D_MODEL), w["conv_w"], batch, token)

    dq, dk, dv = _attn_bwd(qkv3, datt.reshape(batch, SEQ, ATTN_OUT), lse, dsum.reshape(batch, SEQ, ATTN_OUT),
                           slopes_r, batch)
    pieces = [dq.reshape(T, ATTN_WIDTH), dk.reshape(T, ATTN_WIDTH), dv.reshape(T, ATTN_WIDTH),
              dua.reshape(T, D_MODEL), dub.reshape(T, D_MODEL), dlogits]

    names = ("q", "k", "v", "ua", "ub", "gate")
    gw["w_in"] = jnp.concatenate([_mm_tn(p, h, BF16, "gw_in_" + nm, tn=512) for nm, p in zip(names, pieces)], axis=0)
    gw["conv_w"] = g_conv_w
    token = emit(("w_in", "conv_w"), gw)
    grad_x, g_norm1 = _in_bwd(pieces, w["w_in"], x, dx1, small["norm1_g"], token)

    gsmall = {"norm1_g": g_norm1, "gate_b": g_gate_b, "conv_b": g_conv_b, "conv_ln_g": g_ln_g, "conv_ln_b": g_ln_b,
              "norm2_g": g_norm2, "norm_f_g": g_normf}
    return loss, grad_x, gw, gsmall


ANY = pl.BlockSpec(memory_space=pl.ANY)


def _all_gather(arrs):
    n = len(arrs)

    def body(*refs):
        ins, outs = refs[:n], refs[n:2 * n]
        send_sems, recv_sems, local_sems = refs[2 * n:]
        x, y, c = lax.axis_index("x"), lax.axis_index("y"), lax.axis_index("c")
        me, sibling = (x, y, c), (x, y, 1 - c)
        chips = [(1 - x, y), (x, 1 - y), (1 - x, 1 - y)]

        def copy(a, k, block, to, src=None):
            px, py, pc = block
            dst = outs[a].at[4 * px + 2 * py + pc]
            return pltpu.make_async_remote_copy(
                src_ref=dst if src is None else src, dst_ref=dst,
                send_sem=send_sems.at[a, k], recv_sem=recv_sems.at[a, k], device_id=to, device_id_type=MESH)

        mine = [pltpu.make_async_copy(ins[a], outs[a].at[4 * x + 2 * y + c], local_sems.at[a]) for a in range(n)]
        for cp in mine:
            cp.start()
        first = []
        for j, chip in enumerate(chips):
            first += [copy(a, 1 + j, me, (*chip, c), src=ins[a]) for a in range(n)]
        first += [copy(a, 0, me, sibling, src=ins[a]) for a in range(n)]
        for cp in first:
            cp.start()
        passed = []
        for j, chip in enumerate(chips):
            for a in range(n):
                copy(a, 1 + j, (*chip, c), me).wait_recv()
                cp = copy(a, 4 + j, (*chip, c), sibling)
                cp.start()
                passed.append(cp)
        for a in range(n):
            copy(a, 0, sibling, me).wait_recv()
        for j, chip in enumerate(chips):
            for a in range(n):
                copy(a, 4 + j, (*chip, 1 - c), me).wait_recv()
        for cp in first + passed:
            cp.wait_send()
        for cp in mine:
            cp.wait()

    return pl.pallas_call(
        body, in_specs=[ANY] * n, out_specs=[ANY] * n,
        out_shape=[SDS((N_DEV,) + a.shape, a.dtype) for a in arrs],
        scratch_shapes=[pltpu.SemaphoreType.DMA((n, 7)), pltpu.SemaphoreType.DMA((n, 7)), pltpu.SemaphoreType.DMA((n,))],
        name="all_gather_weights")(*arrs)


HBM = pl.BlockSpec(memory_space=pltpu.HBM)
SEM = pl.BlockSpec(memory_space=pltpu.SEMAPHORE)
ALL_PEERS = tuple(range(1, N_DEV))
OTHER_CHIPS = (2, 4, 6)
SPLIT_EFFECT = pltpu.CompilerParams(has_side_effects=pltpu.SideEffectType.DATAFLOW_SIDE_EFFECTING)


def _exchange_copies(mode, ks, srcs, lands, send_sems, recv_sems):
    x, y, c = lax.axis_index("x"), lax.axis_index("y"), lax.axis_index("c")
    me = 4 * x + 2 * y + c
    send, recv = [], []
    for a in range(len(lands)):
        for i, k in enumerate(ks):
            peer = (x ^ ((k >> 2) & 1), y ^ ((k >> 1) & 1), c ^ (k & 1))
            pidx = 4 * peer[0] + 2 * peer[1] + peer[2]
            if mode == "gather":
                src, to, out_slot, in_slot = srcs[a], peer, me, pidx
            elif mode == "scatter":
                src, to, out_slot, in_slot = srcs[a].at[pidx], peer, me, pidx
            elif mode == "chip_scatter":
                src, to, out_slot, in_slot = srcs[a].at[pidx >> 1], peer, me >> 1, pidx >> 1
            else:
                src, to, out_slot, in_slot = lands[a].at[pidx], (x, y, 1 - c), pidx, pidx ^ 1
            s = a * len(ks) + i
            send.append(pltpu.make_async_remote_copy(
                src_ref=src, dst_ref=lands[a].at[out_slot], send_sem=send_sems.at[s], recv_sem=recv_sems.at[s],
                device_id=to, device_id_type=MESH))
            recv.append(pltpu.make_async_remote_copy(
                src_ref=src, dst_ref=lands[a].at[in_slot], send_sem=send_sems.at[s], recv_sem=recv_sems.at[s],
                device_id=to, device_id_type=MESH))
    return send, recv


def _send_start(mode, ks, name, srcs=(), lands=None, dep=None):
    srcs = list(srcs)
    if lands is None:
        slots = 4 if mode == "chip_scatter" else N_DEV
        lands = [lax.empty((slots,) + (s.shape if mode == "gather" else s.shape[1:]), s.dtype) for s in srcs]
    ns, nl = len(srcs), len(lands)
    nsem = nl * len(ks)

    def body(*refs):
        send, _ = _exchange_copies(mode, ks, refs[:ns], refs[ns:ns + nl], refs[ns + nl], refs[ns + nl + 1])
        for cp in send:
            cp.start()
        token = refs[-1]
        token[...] = jnp.zeros_like(token)

    both = srcs + list(lands)
    body, dep_spec, dep_arg = _anchored(body, ns + nl, dep)
    res = pl.pallas_call(
        body, name=name,
        out_shape=(pltpu.SemaphoreType.DMA((nsem,)), pltpu.SemaphoreType.DMA((nsem,)),
                   *[pltpu.HBM(a.shape, a.dtype) for a in both], SDS((8, 128), F32)),
        in_specs=[HBM] * (ns + nl) + dep_spec,
        out_specs=(SEM, SEM, *([HBM] * (ns + nl)), pl.BlockSpec(memory_space=pltpu.VMEM)),
        input_output_aliases={i: 2 + i for i in range(ns + nl)}, compiler_params=SPLIT_EFFECT,
    )(*[pltpu.with_memory_space_constraint(a, pltpu.HBM) for a in both], *dep_arg)
    return dict(mode=mode, ks=ks, send_sems=res[0], recv_sems=res[1], srcs=res[2:2 + ns], lands=res[2 + ns:2 + ns + nl],
                token=res[-1])


def _send_wait(started, after, name):
    ns, nl = len(started["srcs"]), len(started["lands"])

    def body(*refs):
        send, recv = _exchange_copies(started["mode"], started["ks"], refs[:ns], refs[ns:ns + nl],
                                      refs[ns + nl], refs[ns + nl + 1])
        for cp in send:
            cp.wait_send()
        for cp in recv:
            cp.wait_recv()

    both = list(started["srcs"]) + list(started["lands"])
    after = after if isinstance(after, (tuple, list)) else (after,)
    res = pl.pallas_call(
        body, name=name,
        out_shape=tuple(pltpu.HBM(a.shape, a.dtype) for a in both),
        in_specs=[HBM] * (ns + nl) + [SEM, SEM] + [ANY] * len(after), out_specs=tuple([HBM] * (ns + nl)),
        input_output_aliases={i: i for i in range(ns + nl)}, compiler_params=SPLIT_EFFECT,
    )(*both, started["send_sems"], started["recv_sems"], *after)
    return res[:ns], res[ns:]


def _exchange_sibling(gs):
    n = len(gs)

    def body(*refs):
        ins, outs = refs[:n], refs[n:2 * n]
        send_sems, recv_sems = refs[2 * n:]
        x, y, c = lax.axis_index("x"), lax.axis_index("y"), lax.axis_index("c")
        copies = []
        for a in range(n):
            for j in range(4):
                copies.append(pltpu.make_async_remote_copy(
                    src_ref=ins[a].at[2 * j + (1 - c)], dst_ref=outs[a].at[j],
                    send_sem=send_sems.at[a, j], recv_sem=recv_sems.at[a, j],
                    device_id=(x, y, 1 - c), device_id_type=MESH))
        for cp in copies:
            cp.start()
        for cp in copies:
            cp.wait_recv()
        for cp in copies:
            cp.wait_send()

    return pl.pallas_call(
        body, in_specs=[ANY] * n, out_specs=[ANY] * n,
        out_shape=[SDS((4,) + g.shape[1:], g.dtype) for g in gs],
        scratch_shapes=[pltpu.SemaphoreType.DMA((n, 4)), pltpu.SemaphoreType.DMA((n, 4))],
        name="reduce_scatter_sibling")(*gs)


def _add_pair(g, r1, core, name):
    _, rows, cols = g.shape
    tr = _row_tile(rows, cols, 3 * g.dtype.itemsize)

    def body(c_ref, g_ref, r_ref, o_ref):
        o_ref[...] = (g_ref[...].astype(F32) + r_ref[...].astype(F32)).astype(o_ref.dtype)

    return pl.pallas_call(
        body,
        grid_spec=pltpu.PrefetchScalarGridSpec(
            num_scalar_prefetch=1, grid=(4, rows // tr),
            in_specs=[pl.BlockSpec((1, tr, cols), lambda j, i, c_ref: (2 * j + c_ref[0], i, 0)),
                      pl.BlockSpec((1, tr, cols), lambda j, i, c_ref: (j, i, 0))],
            out_specs=pl.BlockSpec((1, tr, cols), lambda j, i, c_ref: (j, i, 0))),
        out_shape=SDS((4, rows, cols), g.dtype),
        compiler_params=_params("parallel", "parallel"), name=name)(core, g, r1)


def _row_tile(rows, cols, itemsize_total):
    budget = (4 << 20) // max(1, cols * itemsize_total)
    if rows <= budget:
        return rows
    t = rows
    while t > budget and t % 2 == 0 and (t // 2) % 16 == 0:
        t //= 2
    return t


def _adam_math(g, w, m, v):
    m_new = ADAM_B1 * m + (1.0 - ADAM_B1) * g
    v_new = ADAM_B2 * v + (1.0 - ADAM_B2) * (g * g)
    m_hat = m_new / (1.0 - ADAM_B1 ** ADAM_STEP)
    v_hat = v_new / (1.0 - ADAM_B2 ** ADAM_STEP)
    delta = -ADAM_LR * (m_hat / (jnp.sqrt(v_hat) + ADAM_EPS) + ADAM_WD * w)
    return delta, m_new, v_new


def _sum_adam(parts, w, m, v, name):
    rows, cols = w.shape
    nparts = parts.shape[0]
    tr = _row_tile(rows, cols, nparts * parts.dtype.itemsize + 7 * 4)

    def body(p_ref, w_ref, m_ref, v_ref, g_ref, d_ref, mo_ref, vo_ref):
        g = p_ref[0].astype(F32)
        for s in range(1, nparts):
            g = g + p_ref[s].astype(F32)
        delta, m_new, v_new = _adam_math(g, w_ref[...], m_ref[...], v_ref[...])
        g_ref[...] = g
        d_ref[...] = delta
        mo_ref[...] = m_new
        vo_ref[...] = v_new

    blk = pl.BlockSpec((tr, cols), lambda i: (i, 0))
    out = SDS((rows, cols), F32)
    return pl.pallas_call(
        body, grid=(rows // tr,),
        in_specs=[pl.BlockSpec((nparts, tr, cols), lambda i: (0, i, 0)), blk, blk, blk],
        out_specs=[blk, blk, blk, blk], out_shape=[out, out, out, out],
        compiler_params=_params("parallel"), name=name)(parts, w, m, v)


SMALL_ROWS = 72


def _small_allreduce_adam(gpart, w, m, v, dep=None):
    def body(g_ref, w_ref, m_ref, v_ref, go_ref, d_ref, mo_ref, vo_ref, gath, send_sems, recv_sems):
        x, y, c = lax.axis_index("x"), lax.axis_index("y"), lax.axis_index("c")
        me = 4 * x + 2 * y + c
        gath[me] = g_ref[...]
        copies = []
        for k in range(1, N_DEV):
            fx, fy, fc = (k >> 2) & 1, (k >> 1) & 1, k & 1
            peer = (x ^ fx, y ^ fy, c ^ fc)
            copies.append(pltpu.make_async_remote_copy(
                src_ref=gath.at[me], dst_ref=gath.at[me], send_sem=send_sems.at[k - 1], recv_sem=recv_sems.at[k - 1],
                device_id=peer, device_id_type=MESH))
        for cp in copies:
            cp.start()
        for cp in copies:
            cp.wait_recv()
        for cp in copies:
            cp.wait_send()
        g = gath[0]
        for d in range(1, N_DEV):
            g = g + gath[d]
        delta, m_new, v_new = _adam_math(g, w_ref[...], m_ref[...], v_ref[...])
        go_ref[...] = g
        d_ref[...] = delta
        mo_ref[...] = m_new
        vo_ref[...] = v_new

    vm = pl.BlockSpec(memory_space=pltpu.VMEM)
    out = SDS((SMALL_ROWS, 128), F32)
    body, dep_spec, dep_arg = _anchored(body, 4, dep)
    return pl.pallas_call(
        body, in_specs=[vm] * 4 + dep_spec, out_specs=[vm] * 4, out_shape=[out] * 4,
        scratch_shapes=[pltpu.VMEM((N_DEV, SMALL_ROWS, 128), F32), pltpu.SemaphoreType.DMA((N_DEV - 1,)),
                        pltpu.SemaphoreType.DMA((N_DEV - 1,))],
        name="small_allreduce_adam")(gpart, w, m, v, *dep_arg)


BIG = ("w_in", "conv_w", "w_conv_out", "w_attn_out", "w_o", "w_ffn_gate", "w_ffn_up", "w_ffn_down")
EARLY = ("w_in", "conv_w")
LATE_MERGE = ("w_conv_out", "w_attn_out", "w_o")
LATE_FFN = ("w_ffn_gate", "w_ffn_up", "w_ffn_down")
TRANSPOSED = ("w_in", "w_ffn_gate", "w_ffn_up")
COL_SHARDED = ("conv_w", "w_attn_out")
SMALL = ("norm1_g", "gate_b", "conv_b", "conv_ln_g", "conv_ln_b", "norm2_g", "norm_f_g")
WEIGHTS = ("norm1_g", "w_in", "gate_b", "conv_w", "conv_b", "conv_ln_g", "conv_ln_b", "w_conv_out", "w_attn_out", "w_o",
           "norm2_g", "w_ffn_gate", "w_ffn_up", "w_ffn_down", "norm_f_g")


def _shard2d(name, a):
    a = a.reshape(a.shape[-2], a.shape[-1])
    if name in TRANSPOSED:
        a = a.T
    if name == "conv_w":
        a = jnp.pad(a, ((0, CONV_PAD - CONV_K), (0, 0)))
    return a


def _from_shard2d(name, val, shape):
    if name in TRANSPOSED:
        val = val.T
    if name == "conv_w":
        val = val[:CONV_K]
    return val.reshape(shape)


def _gathered_to_full(name, g):
    if name in COL_SHARDED:
        return g.transpose(1, 0, 2).reshape(g.shape[1], N_DEV * g.shape[2])
    return g.reshape(N_DEV * g.shape[1], g.shape[2])


def _full_to_blocks(name, g):
    if name in COL_SHARDED:
        return g.reshape(g.shape[0], N_DEV, g.shape[1] // N_DEV).transpose(1, 0, 2)
    return g.reshape(N_DEV, g.shape[0] // N_DEV, g.shape[1])


def _pack_small(d, last_rows):
    vec = jnp.concatenate([d[n].reshape(-1) for n in SMALL]).reshape(SMALL_ROWS - SUBLANES, 128)
    return jnp.concatenate([vec, last_rows], axis=0)


def _unpack_small(p, like):
    flat = p.reshape(-1)
    out, off = {}, 0
    for n in SMALL:
        size = like[n].size
        out[n] = flat[off:off + size].reshape(like[n].shape)
        off += size
    return out


def kernel(x, norm1_g, w_in, gate_b, conv_w, conv_b, conv_ln_g, conv_ln_b, w_conv_out, w_attn_out, w_o, norm2_g, w_ffn_gate, w_ffn_up, w_ffn_down, norm_f_g, loss_target, m_norm1_g, m_w_in, m_gate_b, m_conv_w, m_conv_b, m_conv_ln_g, m_conv_ln_b, m_w_conv_out, m_w_attn_out, m_w_o, m_norm2_g, m_w_ffn_gate, m_w_ffn_up, m_w_ffn_down, m_norm_f_g, v_norm1_g, v_w_in, v_gate_b, v_conv_w, v_conv_b, v_conv_ln_g, v_conv_ln_b, v_w_conv_out, v_w_attn_out, v_w_o, v_norm2_g, v_w_ffn_gate, v_w_ffn_up, v_w_ffn_down, v_norm_f_g):
    wts = dict(norm1_g=norm1_g, w_in=w_in, gate_b=gate_b, conv_w=conv_w, conv_b=conv_b, conv_ln_g=conv_ln_g,
               conv_ln_b=conv_ln_b, w_conv_out=w_conv_out, w_attn_out=w_attn_out, w_o=w_o, norm2_g=norm2_g,
               w_ffn_gate=w_ffn_gate, w_ffn_up=w_ffn_up, w_ffn_down=w_ffn_down, norm_f_g=norm_f_g)
    mom1 = dict(norm1_g=m_norm1_g, w_in=m_w_in, gate_b=m_gate_b, conv_w=m_conv_w, conv_b=m_conv_b, conv_ln_g=m_conv_ln_g,
                conv_ln_b=m_conv_ln_b, w_conv_out=m_w_conv_out, w_attn_out=m_w_attn_out, w_o=m_w_o, norm2_g=m_norm2_g,
                w_ffn_gate=m_w_ffn_gate, w_ffn_up=m_w_ffn_up, w_ffn_down=m_w_ffn_down, norm_f_g=m_norm_f_g)
    mom2 = dict(norm1_g=v_norm1_g, w_in=v_w_in, gate_b=v_gate_b, conv_w=v_conv_w, conv_b=v_conv_b, conv_ln_g=v_conv_ln_g,
                conv_ln_b=v_conv_ln_b, w_conv_out=v_w_conv_out, w_attn_out=v_w_attn_out, w_o=v_w_o, norm2_g=v_norm2_g,
                w_ffn_gate=v_w_ffn_gate, w_ffn_up=v_w_ffn_up, w_ffn_down=v_w_ffn_down, norm_f_g=v_norm_f_g)

    T = x.shape[0] * x.shape[1]
    x2 = x.reshape(T, D_MODEL)
    t2 = loss_target.reshape(T, D_MODEL)

    me = 4 * lax.axis_index("x") + 2 * lax.axis_index("y") + lax.axis_index("c")
    shards = {n: _shard2d(n, wts[n]) for n in BIG}
    sent = {n: shards[n] if n == "conv_w" else shards[n].astype(BF16) for n in BIG}
    small = {n: wts[n].reshape(1, -1) for n in SMALL}

    gathered = _all_gather([sent[n] for n in EARLY])
    full = {n: _gathered_to_full(n, g) for n, g in zip(EARLY, gathered)}
    merge_gather = _send_start("gather", ALL_PEERS, "gather_start_merge", [sent[n] for n in LATE_MERGE], dep=gathered[0])
    ffn_gather = _send_start("gather", (1,) + OTHER_CHIPS, "gather_start_ffn", [sent[n] for n in LATE_FFN],
                             dep=merge_gather["token"])
    ffn_state = {}

    def filled(names, srcs, lands):
        return {n: _gathered_to_full(n, lax.dynamic_update_slice(land, src[None], (me, 0, 0)))
                for n, src, land in zip(names, srcs, lands)}

    def late_weights(names, after):
        if names is LATE_MERGE:
            srcs, lands = _send_wait(merge_gather, after, "gather_wait_merge")
            ffn_state["srcs"], ffn_lands = _send_wait(ffn_gather, after, "gather_wait_ffn")
            ffn_state["forward"] = _send_start("forward", OTHER_CHIPS, "forward_start_ffn", lands=ffn_lands)
            return {**filled(names, srcs, lands), "token": ffn_state["forward"]["token"]}
        _, lands = _send_wait(ffn_state["forward"], after, "forward_wait_ffn")
        return filled(names, ffn_state["srcs"], lands)

    scatters = []
    core = lax.axis_index("c").astype(jnp.int32).reshape(1)

    def emit(names, gw):
        blocks = [_full_to_blocks(n, gw[n]) for n in names]
        if "w_in" in names:
            sums = [_add_pair(g, r, core, "chip_sum_" + n) for n, g, r in zip(names, blocks, _exchange_sibling(blocks))]
            started = _send_start("chip_scatter", OTHER_CHIPS, "scatter_start_" + names[0], sums)
        else:
            started = _send_start("scatter", ALL_PEERS, "scatter_start_" + names[0], blocks)
        scatters.append((names, started))
        return started["token"]

    loss_part, grad_x, gw, gsmall = _local_step(x2, t2, full, small, ffn_gather["token"], late_weights, emit)

    grads, deltas, new_m, new_v = {}, {}, {}, {}
    after = grad_x
    for names, started in scatters:
        srcs, lands = _send_wait(started, after, "scatter_wait_" + names[0])
        mine = me >> 1 if started["mode"] == "chip_scatter" else me
        for n, src, land in zip(names, srcs, lands):
            parts = lax.dynamic_update_slice(land, lax.dynamic_slice_in_dim(src, mine, 1, axis=0), (mine, 0, 0))
            g, d, mo, vo = _sum_adam(parts, shards[n], _shard2d(n, mom1[n]), _shard2d(n, mom2[n]), "adam_" + n)
            for dst, val in ((grads, g), (deltas, d), (new_m, mo), (new_v, vo)):
                dst[n] = _from_shard2d(n, val, wts[n].shape)
            after = g

    zeros, ones = jnp.zeros((SUBLANES, 128), F32), jnp.ones((SUBLANES, 128), F32)
    sg, sd, sm, sv = _small_allreduce_adam(
        _pack_small(gsmall, jnp.broadcast_to(loss_part, (SUBLANES, 128))), _pack_small(wts, zeros),
        _pack_small(mom1, zeros), _pack_small(mom2, ones), after)
    for dst, val in ((grads, sg), (deltas, sd), (new_m, sm), (new_v, sv)):
        dst.update(_unpack_small(val, wts))
    loss = sg[SMALL_ROWS - SUBLANES, 0]
    return (loss, grad_x.reshape(x.shape), *[grads[n] for n in WEIGHTS], *[deltas[n] for n in WEIGHTS],
            *[new_m[n] for n in WEIGHTS], *[new_v[n] for n in WEIGHTS])
```

```python
import math

import numpy as np
import jax
import jax.numpy as jnp
from jax import lax
from jax.experimental import pallas as pl
from jax.experimental.pallas import tpu as pltpu

F32 = jnp.float32
BF16 = jnp.bfloat16
SDS = jax.ShapeDtypeStruct
MESH = pl.DeviceIdType.MESH

D_MODEL = 1024
SEQ = 2048
HEAD_DIM = 64
GROUPS = ((128, 1), (512, 4), (2048, 16))
HEADS_PER_GROUP = 8
N_HEADS = 24
ATTN_WIDTH = N_HEADS * HEAD_DIM
ATTN_OUT = HEADS_PER_GROUP * HEAD_DIM
CONV_K = 31
CONV_PAD = 32
D_FF = 2816
IN_WIDTH = 3 * ATTN_WIDTH + 2 * D_MODEL + 2 * D_MODEL
RMS_EPS = 1e-6
LN_EPS = 1e-5
Q_BLOCK = 128
LANES = 128
NEG = -1e30
N_DEV = 8

ADAM_LR = 0.001
ADAM_B1 = 0.9
ADAM_B2 = 0.999
ADAM_EPS = 1e-08
ADAM_WD = 0.01
ADAM_STEP = 10


def _alibi_slope_list(n):
    def pow2(m):
        start = 2.0 ** (-8.0 / m)
        return [start ** (i + 1) for i in range(m)]
    if math.log2(n).is_integer():
        return pow2(n)
    c = 2 ** math.floor(math.log2(n))
    return pow2(c) + _alibi_slope_list(2 * c)[0::2][: n - c]


def _slopes_times_dilation():
    s = np.asarray(sorted(_alibi_slope_list(N_HEADS), reverse=True), dtype=np.float32).reshape(3, HEADS_PER_GROUP)
    r = np.asarray([g[1] for g in GROUPS], dtype=np.float32)[:, None]
    return (s * r).reshape(N_HEADS)


def _sigmoid(x):
    return 0.5 * jnp.tanh(0.5 * x) + 0.5


def _dot(a, b):
    return jnp.dot(a, b, preferred_element_type=F32)


def _dot_nt(a, b):
    return lax.dot_general(a, b, (((1,), (1,)), ((), ())), preferred_element_type=F32)


def _dot_tn(a, b):
    return lax.dot_general(a, b, (((0,), (0,)), ((), ())), preferred_element_type=F32)


def _rowsum(x):
    return jnp.sum(x, axis=0, keepdims=True)


def _params(*sem):
    return pltpu.CompilerParams(dimension_semantics=sem)


def _anchored(body, n_in, dep):
    if dep is None:
        return body, [], []

    def wrapped(*refs):
        return body(*refs[:n_in], *refs[n_in + 1:])

    return wrapped, [pl.BlockSpec(memory_space=pl.ANY)], [dep]


IN_TM = 256
IN_CHUNK = 512


def _in_proj(x, g1, w_in_t, dep=None):
    T = x.shape[0]
    tm = IN_TM
    widths = (3 * ATTN_WIDTH, 2 * D_MODEL, 2 * D_MODEL)

    def body(x_ref, g_ref, w_hbm, h_ref, qkv_ref, u_ref, lg_ref, w_vmem, sem):
        @pl.when(pl.program_id(0) == 0)
        def _():
            cp = pltpu.make_async_copy(w_hbm, w_vmem, sem)
            cp.start()
            cp.wait()

        xv = x_ref[...]
        r = lax.rsqrt(jnp.mean(xv * xv, axis=-1, keepdims=True) + RMS_EPS)
        h = (xv * r * g_ref[...]).astype(BF16)
        h_ref[...] = h
        col = 0
        for o_ref, width in zip((qkv_ref, u_ref, lg_ref), widths):
            for j in range(width // IN_CHUNK):
                o_ref[:, j * IN_CHUNK:(j + 1) * IN_CHUNK] = _dot_nt(h, w_vmem[col:col + IN_CHUNK, :])
                col += IN_CHUNK

    row = lambda n: pl.BlockSpec((tm, n), lambda i: (i, 0))
    body, dep_spec, dep_arg = _anchored(body, 3, dep)
    return pl.pallas_call(
        body, grid=(T // tm,),
        in_specs=[row(D_MODEL), pl.BlockSpec((1, D_MODEL), lambda i: (0, 0)), pl.BlockSpec(memory_space=pl.ANY)] + dep_spec,
        out_specs=[row(D_MODEL)] + [row(n) for n in widths],
        out_shape=[SDS((T, D_MODEL), BF16)] + [SDS((T, n), F32) for n in widths],
        scratch_shapes=[pltpu.VMEM((IN_WIDTH, D_MODEL), BF16), pltpu.SemaphoreType.DMA],
        compiler_params=_params("arbitrary"), name="in_proj")(x, g1, w_in_t, *dep_arg)


def _mm_tn(a, b, out_dtype, name, tn, tt=1024):
    T, K = a.shape
    N = b.shape[1]
    nt = T // tt

    def body(a_ref, b_ref, o_ref, acc):
        t = pl.program_id(1)

        @pl.when(t == 0)
        def _():
            acc[...] = jnp.zeros_like(acc)

        acc[...] += _dot_tn(a_ref[...], b_ref[...])

        @pl.when(t == nt - 1)
        def _():
            o_ref[...] = acc[...].astype(o_ref.dtype)

    return pl.pallas_call(
        body, grid=(N // tn, nt),
        in_specs=[pl.BlockSpec((tt, K), lambda j, t: (t, 0)),
                  pl.BlockSpec((tt, tn), lambda j, t: (t, j))],
        out_specs=pl.BlockSpec((K, tn), lambda j, t: (0, j)),
        out_shape=SDS((K, N), out_dtype),
        scratch_shapes=[pltpu.VMEM((K, tn), F32)],
        compiler_params=_params("parallel", "arbitrary"), name=name)(a, b)


def _gather_classes(src_ref, dst, r, row0=0):
    L = SEQ // r
    for c in range(r):
        dst[row0 + c * L:row0 + (c + 1) * L, :] = src_ref[0, pl.ds(c, L, stride=r), :].astype(dst.dtype)


def _scatter_classes(src, dst, r, row0=0):
    L = SEQ // r
    for c in range(r):
        dst[pl.ds(c, L, stride=r), :] = src[row0 + c * L:row0 + (c + 1) * L, :].astype(dst.dtype)


def _attn_masks(slope_r):
    qi = lax.broadcasted_iota(jnp.int32, (Q_BLOCK, Q_BLOCK), 0)
    kj = lax.broadcasted_iota(jnp.int32, (Q_BLOCK, Q_BLOCK), 1)
    rel = (qi - kj).astype(F32)
    bias_cur = jnp.where(qi >= kj, -slope_r * rel, NEG)
    bias_prev = jnp.where(qi <= kj, -slope_r * (rel + float(Q_BLOCK)), NEG)
    return bias_cur, bias_prev


def _store_biases(bias, sl_ref, g, hp):
    for hh in range(2):
        cur, prev = _attn_masks(sl_ref[g * HEADS_PER_GROUP + 2 * hp + hh])
        rows = slice(hh * Q_BLOCK, (hh + 1) * Q_BLOCK)
        bias[0, rows, 0:Q_BLOCK] = prev
        bias[1, rows, 0:Q_BLOCK] = jnp.full((Q_BLOCK, Q_BLOCK), NEG, F32)
        bias[0, rows, Q_BLOCK:] = cur
        bias[1, rows, Q_BLOCK:] = cur


def _transpose_pairs(src, dst):
    dst[0, :, 0:Q_BLOCK] = jnp.zeros((LANES, Q_BLOCK), dst.dtype)
    nblk = SEQ // Q_BLOCK
    for b in range(nblk):
        t = src[(b + 1) * Q_BLOCK:(b + 2) * Q_BLOCK, :].T
        dst[b, :, Q_BLOCK:] = t
        if b + 1 < nblk:
            dst[b + 1, :, 0:Q_BLOCK] = t


def _stack_heads(t, low):
    z = jnp.zeros_like(t)
    return jnp.concatenate([jnp.where(low, t, z), jnp.where(low, z, t)], axis=0)


def _unstack_heads(t2, low):
    return jnp.where(low, t2[:Q_BLOCK], t2[Q_BLOCK:])


def _unit_offsets(u, nb):
    off = pl.multiple_of(u * Q_BLOCK, Q_BLOCK)
    n = u & (nb - 1)
    c = u >> int(math.log2(nb))
    return off, n == 0, c, n


ATTN_UNROLL = 4


def _attn_fwd(qkv, slopes_r, batch):
    nblk = SEQ // Q_BLOCK

    def body(sl_ref, *refs):
        qkv_refs = refs[:9]
        att_ref, lse_ref = refs[9:11]
        qd, kd, vd, kt, opos, lpos, bias = refs[11:]
        hp = pl.program_id(1)
        low = lax.broadcasted_iota(jnp.int32, (Q_BLOCK, LANES), 1) < HEAD_DIM

        for g in range(3):
            r = GROUPS[g][1]
            nb = SEQ // r // Q_BLOCK
            _gather_classes(qkv_refs[3 * g], qd, r)
            kd[0:Q_BLOCK, :] = jnp.zeros((Q_BLOCK, LANES), BF16)
            vd[0:Q_BLOCK, :] = jnp.zeros((Q_BLOCK, LANES), BF16)
            _gather_classes(qkv_refs[3 * g + 1], kd, r, Q_BLOCK)
            _gather_classes(qkv_refs[3 * g + 2], vd, r, Q_BLOCK)
            _transpose_pairs(kd, kt)
            _store_biases(bias, sl_ref, g, hp)

            def unit(u, carry, g=g, r=r, nb=nb):
                off, first, c, n = _unit_offsets(u, nb)
                q2 = _stack_heads(qd[pl.ds(off, Q_BLOCK), :], low)
                s = _dot(q2, kt[u]) * 0.125 + bias[first.astype(jnp.int32)]
                m = jnp.max(s, axis=-1, keepdims=True)
                p = jnp.exp(s - m)
                l = jnp.sum(p, axis=-1, keepdims=True)
                o2 = _dot(p.astype(BF16), vd[pl.ds(off, 2 * Q_BLOCK), :]) * (1.0 / l)
                lse2 = m + jnp.log(l)
                rows = pl.ds(c + n * (Q_BLOCK * r), Q_BLOCK, stride=r)
                opos[g, rows, :] = _unstack_heads(o2, low)
                lpos[g, rows, :] = jnp.where(low, lse2[:Q_BLOCK], lse2[Q_BLOCK:])
                return carry

            lax.fori_loop(0, nblk, unit, 0, unroll=ATTN_UNROLL)

        def merge(i, carry):
            rows = pl.ds(pl.multiple_of(i * 256, 256), 256)
            l0, l1, l2 = lpos[0, rows, :], lpos[1, rows, :], lpos[2, rows, :]
            m = jnp.maximum(jnp.maximum(l0, l1), l2)
            e0, e1, e2 = jnp.exp(l0 - m), jnp.exp(l1 - m), jnp.exp(l2 - m)
            den = e0 + e1 + e2
            att = (e0 * opos[0, rows, :] + e1 * opos[1, rows, :] + e2 * opos[2, rows, :]) / den
            att_ref[0, rows, :] = att.astype(att_ref.dtype)
            lse_ref[0, rows, :] = m + jnp.log(den)
            return carry

        lax.fori_loop(0, SEQ // 256, merge, 0)

    def col(sec, g):
        return pl.BlockSpec((1, SEQ, LANES), lambda b, hp: (b, 0, sec * 12 + g * 4 + hp))

    out = pl.BlockSpec((1, SEQ, LANES), lambda b, hp: (b, 0, hp))
    return pl.pallas_call(
        body, grid=(batch, 4),
        in_specs=[pl.BlockSpec(memory_space=pltpu.SMEM)] + [col(sec, g) for g in range(3) for sec in range(3)],
        out_specs=[out, out],
        out_shape=[SDS((batch, SEQ, ATTN_OUT), BF16), SDS((batch, SEQ, ATTN_OUT), F32)],
        scratch_shapes=[pltpu.VMEM((SEQ, LANES), BF16), pltpu.VMEM((Q_BLOCK + SEQ, LANES), BF16),
                        pltpu.VMEM((Q_BLOCK + SEQ, LANES), BF16), pltpu.VMEM((nblk, LANES, 2 * Q_BLOCK), BF16),
                        pltpu.VMEM((3, SEQ, LANES), F32), pltpu.VMEM((3, SEQ, LANES), F32),
                        pltpu.VMEM((2, 2 * Q_BLOCK, 2 * Q_BLOCK), F32)],
        compiler_params=_params("parallel", "parallel"), name="attn_fwd")(slopes_r, *([qkv] * 9))


def _attn_bwd(qkv, datt, lse, dsum, slopes_r, batch):
    nblk = SEQ // Q_BLOCK

    def body(sl_ref, q_ref, k_ref, v_ref, do_ref, l_ref, d_ref, dq_ref, dk_ref, dv_ref,
             qd, kd, vd, dod, kt, vt, ld, dd, dq_acc, dk_acc, dv_acc, dk_part, dv_part, stage, bias):
        gid, hp = pl.program_id(1), pl.program_id(2)
        low = lax.broadcasted_iota(jnp.int32, (Q_BLOCK, LANES), 1) < HEAD_DIM

        def section(g):
            r = GROUPS[g][1]
            nb = SEQ // r // Q_BLOCK
            _gather_classes(q_ref, qd, r)
            kd[0:Q_BLOCK, :] = jnp.zeros((Q_BLOCK, LANES), BF16)
            vd[0:Q_BLOCK, :] = jnp.zeros((Q_BLOCK, LANES), BF16)
            _gather_classes(k_ref, kd, r, Q_BLOCK)
            _gather_classes(v_ref, vd, r, Q_BLOCK)
            _gather_classes(do_ref, dod, r)
            _gather_classes(l_ref, ld, r)
            _gather_classes(d_ref, dd, r)
            _transpose_pairs(kd, kt)
            _transpose_pairs(vd, vt)
            _store_biases(bias, sl_ref, g, hp)

            def unit(u, carry):
                off, first, _, _ = _unit_offsets(u, nb)
                pair = pl.ds(off, 2 * Q_BLOCK)
                q2 = _stack_heads(qd[pl.ds(off, Q_BLOCK), :], low)
                do2 = _stack_heads(dod[pl.ds(off, Q_BLOCK), :], low)
                lse_t = ld[pl.ds(off, Q_BLOCK), :]
                dsum_t = dd[pl.ds(off, Q_BLOCK), :]
                lse2 = jnp.concatenate([lse_t[:, 0:1], lse_t[:, HEAD_DIM:HEAD_DIM + 1]], axis=0)
                dsum2 = jnp.concatenate([dsum_t[:, 0:1], dsum_t[:, HEAD_DIM:HEAD_DIM + 1]], axis=0)
                s = _dot(q2, kt[u]) * 0.125 + bias[first.astype(jnp.int32)]
                p = jnp.exp(s - lse2)
                ds = (p * (_dot(do2, vt[u]) - dsum2)).astype(BF16)
                dq_acc[pl.ds(off, Q_BLOCK), :] = _unstack_heads(_dot(ds, kd[pair, :]), low) * 0.125
                dk_part[u] = _dot_tn(ds, q2) * 0.125
                dv_part[u] = _dot_tn(p.astype(BF16), do2)
                return carry

            lax.fori_loop(0, nblk, unit, 0, unroll=ATTN_UNROLL)
            for part, acc in ((dk_part, dk_acc), (dv_part, dv_acc)):
                for b in range(nblk):
                    t = part[b, Q_BLOCK:, :]
                    if b + 1 < nblk:
                        t = t + part[b + 1, 0:Q_BLOCK, :]
                    acc[b * Q_BLOCK:(b + 1) * Q_BLOCK, :] = t
            for acc, out_ref in ((dq_acc, dq_ref), (dk_acc, dk_ref), (dv_acc, dv_ref)):
                _scatter_classes(acc, stage, r)
                out_ref[0] = stage[...].astype(out_ref.dtype)

        for g in range(3):
            pl.when(gid == g)(lambda g=g: section(g))

    def col(sec):
        return pl.BlockSpec((1, SEQ, LANES), lambda b, g, hp: (b, 0, sec * 12 + g * 4 + hp))

    pos = pl.BlockSpec((1, SEQ, LANES), lambda b, g, hp: (b, 0, hp))
    dout = pl.BlockSpec((1, SEQ, LANES), lambda b, g, hp: (b, 0, g * 4 + hp))
    out = SDS((batch, SEQ, ATTN_WIDTH), BF16)
    seq_bf = pltpu.VMEM((SEQ, LANES), BF16)
    seq_f = pltpu.VMEM((SEQ, LANES), F32)
    pad_bf = pltpu.VMEM((Q_BLOCK + SEQ, LANES), BF16)
    part = pltpu.VMEM((nblk, 2 * Q_BLOCK, LANES), F32)
    blk_t = pltpu.VMEM((nblk, LANES, 2 * Q_BLOCK), BF16)
    return pl.pallas_call(
        body, grid=(batch, 3, 4),
        in_specs=[pl.BlockSpec(memory_space=pltpu.SMEM), col(0), col(1), col(2), pos, pos, pos],
        out_specs=[dout, dout, dout],
        out_shape=[out, out, out],
        scratch_shapes=[seq_bf, pad_bf, pad_bf, seq_bf, blk_t, blk_t, seq_f, seq_f, seq_f, seq_f, seq_f, part, part, seq_f,
                        pltpu.VMEM((2, 2 * Q_BLOCK, 2 * Q_BLOCK), F32)],
        compiler_params=_params("parallel", "parallel", "parallel"), name="attn_bwd")(
            slopes_r, qkv, qkv, qkv, datt, lse, dsum)


CONV_TC = 128
CONV_ROWS = 128
SUBLANES = 8


def _fill_shifted(sh):
    n = SEQ + CONV_PAD - SUBLANES
    for s in range(1, SUBLANES):
        sh[s, 0:n, :] = sh[0, s:s + n, :]


def _tap(sh, base, offset):
    s = offset % SUBLANES
    return sh[s, pl.ds(pl.multiple_of(base + (offset - s), SUBLANES), CONV_ROWS), :]


def _conv_fwd(u, conv_w, conv_b, batch):
    nct = D_MODEL // CONV_TC

    def body(ua_ref, ub_ref, w_ref, b_ref, o_ref, sh):
        sh[0, 0:CONV_PAD, :] = jnp.zeros((CONV_PAD, CONV_TC), F32)
        sh[0, CONV_PAD:, :] = ua_ref[0] * _sigmoid(ub_ref[0])
        _fill_shifted(sh)

        def chunk(c, carry):
            base = pl.multiple_of(c * CONV_ROWS, CONV_ROWS)
            acc = jnp.broadcast_to(b_ref[...], (CONV_ROWS, CONV_TC))
            for t in range(CONV_K):
                acc = acc + _tap(sh, base, t + CONV_PAD - (CONV_K - 1)) * w_ref[t:t + 1, :]
            o_ref[0, pl.ds(base, CONV_ROWS), :] = acc
            return carry

        lax.fori_loop(0, SEQ // CONV_ROWS, chunk, 0)

    return pl.pallas_call(
        body, grid=(nct, batch),
        in_specs=[pl.BlockSpec((1, SEQ, CONV_TC), lambda j, b: (b, 0, j)),
                  pl.BlockSpec((1, SEQ, CONV_TC), lambda j, b: (b, 0, j + nct)),
                  pl.BlockSpec((CONV_PAD, CONV_TC), lambda j, b: (0, j)),
                  pl.BlockSpec((1, CONV_TC), lambda j, b: (0, j))],
        out_specs=pl.BlockSpec((1, SEQ, CONV_TC), lambda j, b: (b, 0, j)),
        out_shape=SDS((batch, SEQ, D_MODEL), F32),
        scratch_shapes=[pltpu.VMEM((SUBLANES, SEQ + CONV_PAD, CONV_TC), F32)],
        compiler_params=_params("parallel", "parallel"), name="conv_fwd")(u, u, conv_w, conv_b)


def _conv_bwd(u, dc1, conv_w, batch, dep=None):
    nct = D_MODEL // CONV_TC
    nchunk = SEQ // CONV_ROWS

    def body(ua_ref, ub_ref, d_ref, w_ref, dua_ref, dub_ref, gw_ref, gb_ref, shc, shd, gacc):
        b = pl.program_id(1)
        shc[0, 0:CONV_PAD, :] = jnp.zeros((CONV_PAD, CONV_TC), F32)
        shc[0, CONV_PAD:, :] = ua_ref[0] * _sigmoid(ub_ref[0])
        _fill_shifted(shc)
        shd[0, 0:SEQ, :] = d_ref[0]
        shd[0, SEQ:, :] = jnp.zeros((CONV_PAD, CONV_TC), F32)
        _fill_shifted(shd)

        @pl.when(b == 0)
        def _():
            gacc[...] = jnp.zeros_like(gacc)
            gb_ref[...] = jnp.zeros_like(gb_ref)

        gb_ref[...] += _rowsum(d_ref[0])

        def chunk(c, carry):
            base = pl.multiple_of(c * CONV_ROWS, CONV_ROWS)
            dcur = shd[0, pl.ds(base, CONV_ROWS), :]
            acc = jnp.zeros((CONV_ROWS, CONV_TC), F32)
            for t in range(CONV_K):
                acc = acc + _tap(shd, base, CONV_K - 1 - t) * w_ref[t:t + 1, :]
                prod = _tap(shc, base, t + CONV_PAD - (CONV_K - 1)) * dcur
                gacc[t] += jnp.sum(prod.reshape(CONV_ROWS // 8, 8, CONV_TC), axis=0)
            ua = ua_ref[0, pl.ds(base, CONV_ROWS), :]
            sg = _sigmoid(ub_ref[0, pl.ds(base, CONV_ROWS), :])
            dua_ref[0, pl.ds(base, CONV_ROWS), :] = (acc * sg).astype(dua_ref.dtype)
            dub_ref[0, pl.ds(base, CONV_ROWS), :] = (acc * ua * sg * (1.0 - sg)).astype(dub_ref.dtype)
            return carry

        lax.fori_loop(0, nchunk, chunk, 0)

        @pl.when(b == batch - 1)
        def _():
            for t in range(CONV_K):
                gw_ref[t:t + 1, :] = jnp.sum(gacc[t], axis=0, keepdims=True)
            gw_ref[CONV_K:CONV_PAD, :] = jnp.zeros((CONV_PAD - CONV_K, CONV_TC), F32)

    du = SDS((batch, SEQ, D_MODEL), BF16)
    body, dep_spec, dep_arg = _anchored(body, 4, dep)
    return pl.pallas_call(
        body, grid=(nct, batch),
        in_specs=[pl.BlockSpec((1, SEQ, CONV_TC), lambda j, b: (b, 0, j)),
                  pl.BlockSpec((1, SEQ, CONV_TC), lambda j, b: (b, 0, j + nct)),
                  pl.BlockSpec((1, SEQ, CONV_TC), lambda j, b: (b, 0, j)),
                  pl.BlockSpec((CONV_PAD, CONV_TC), lambda j, b: (0, j))] + dep_spec,
        out_specs=[pl.BlockSpec((1, SEQ, CONV_TC), lambda j, b: (b, 0, j)),
                   pl.BlockSpec((1, SEQ, CONV_TC), lambda j, b: (b, 0, j)),
                   pl.BlockSpec((CONV_PAD, CONV_TC), lambda j, b: (0, j)),
                   pl.BlockSpec((1, CONV_TC), lambda j, b: (0, j))],
        out_shape=[du, du, SDS((CONV_PAD, D_MODEL), F32), SDS((1, D_MODEL), F32)],
        scratch_shapes=[pltpu.VMEM((SUBLANES, SEQ + CONV_PAD, CONV_TC), F32),
                        pltpu.VMEM((SUBLANES, SEQ + CONV_PAD, CONV_TC), F32),
                        pltpu.VMEM((CONV_K, 8, CONV_TC), F32)],
        compiler_params=_params("parallel", "arbitrary"), name="conv_bwd")(u, u, dc1, conv_w, *dep_arg)


MID_TM = 256


def _layernorm_stats(c1):
    mu = jnp.mean(c1, axis=-1, keepdims=True)
    cen = c1 - mu
    rs = lax.rsqrt(jnp.mean(cen * cen, axis=-1, keepdims=True) + LN_EPS)
    return cen * rs, rs


def _mid_fwd(att, c1, logits, x, w_a, w_c, w_o, gate_b, ln_g, ln_b, g2, dep=None):
    T = x.shape[0]
    tm = MID_TM

    def body(att_ref, c1_ref, lg_ref, x_ref, wa_ref, wc_ref, wo_ref, gb_ref, lng_ref, lnb_ref, g2_ref,
             c3_ref, ya_ref, yc_ref, mix_ref, x1_ref, h2_ref):
        ya = _dot(att_ref[...], wa_ref[...])
        xh, _ = _layernorm_stats(c1_ref[...])
        c2 = xh * lng_ref[...] + lnb_ref[...]
        c3 = (c2 * _sigmoid(c2)).astype(BF16)
        c3_ref[...] = c3
        yc = _dot(c3, wc_ref[...])
        gates = _sigmoid(lg_ref[...] + gb_ref[...])
        mix = (gates[:, :D_MODEL] * ya + gates[:, D_MODEL:] * yc).astype(BF16)
        ya_ref[...] = ya.astype(BF16)
        yc_ref[...] = yc.astype(BF16)
        mix_ref[...] = mix
        x1 = x_ref[...] + _dot(mix, wo_ref[...])
        x1_ref[...] = x1
        r = lax.rsqrt(jnp.mean(x1 * x1, axis=-1, keepdims=True) + RMS_EPS)
        h2_ref[...] = (x1 * r * g2_ref[...]).astype(BF16)

    row = lambda n: pl.BlockSpec((tm, n), lambda i: (i, 0))
    full = lambda a, b: pl.BlockSpec((a, b), lambda i: (0, 0))
    body, dep_spec, dep_arg = _anchored(body, 11, dep)
    return pl.pallas_call(
        body, grid=(T // tm,),
        in_specs=[row(ATTN_OUT), row(D_MODEL), row(2 * D_MODEL), row(D_MODEL),
                  full(ATTN_OUT, D_MODEL), full(D_MODEL, D_MODEL), full(D_MODEL, D_MODEL),
                  full(1, 2 * D_MODEL), full(1, D_MODEL), full(1, D_MODEL), full(1, D_MODEL)] + dep_spec,
        out_specs=[row(D_MODEL), row(D_MODEL), row(D_MODEL), row(D_MODEL), row(D_MODEL), row(D_MODEL)],
        out_shape=[SDS((T, D_MODEL), BF16), SDS((T, D_MODEL), BF16), SDS((T, D_MODEL), BF16), SDS((T, D_MODEL), BF16),
                   SDS((T, D_MODEL), F32), SDS((T, D_MODEL), BF16)],
        compiler_params=_params("parallel"), name="mid_fwd")(att, c1, logits, x, w_a, w_c, w_o, gate_b, ln_g, ln_b, g2,
                                                             *dep_arg)


def _mid_bwd(dx1b, ya, yc, logits, att, c1, w_a, w_c, w_o, gate_b, ln_g, ln_b, head_ones, dep=None):
    T = dx1b.shape[0]
    tm = MID_TM

    def body(dx_ref, ya_ref, yc_ref, lg_ref, att_ref, c1_ref, wa_ref, wc_ref, wo_ref, gb_ref, lng_ref, lnb_ref, e_ref,
             dlg_ref, dya_ref, dyc_ref, datt_ref, dsum_ref, dc1_ref, ggb_ref, glg_ref, glb_ref):
        @pl.when(pl.program_id(0) == 0)
        def _():
            ggb_ref[...] = jnp.zeros_like(ggb_ref)
            glg_ref[...] = jnp.zeros_like(glg_ref)
            glb_ref[...] = jnp.zeros_like(glb_ref)

        dmix = _dot_nt(dx_ref[...], wo_ref[...])
        gates = _sigmoid(lg_ref[...] + gb_ref[...])
        ga, gc = gates[:, :D_MODEL], gates[:, D_MODEL:]
        dla = dmix * ya_ref[...].astype(F32) * ga * (1.0 - ga)
        dlc = dmix * yc_ref[...].astype(F32) * gc * (1.0 - gc)
        dlg_ref[:, :D_MODEL] = dla.astype(BF16)
        dlg_ref[:, D_MODEL:] = dlc.astype(BF16)
        ggb_ref[:, :D_MODEL] += _rowsum(dla)
        ggb_ref[:, D_MODEL:] += _rowsum(dlc)
        dya = (dmix * ga).astype(BF16)
        dyc = (dmix * gc).astype(BF16)
        dya_ref[...] = dya
        dyc_ref[...] = dyc
        datt = _dot_nt(dya, wa_ref[...])
        datt_ref[...] = datt
        dsum_ref[...] = jnp.dot(datt * att_ref[...].astype(F32), e_ref[...], preferred_element_type=F32,
                                precision=lax.Precision.HIGHEST)
        dc3 = _dot_nt(dyc, wc_ref[...])
        xh, rs = _layernorm_stats(c1_ref[...])
        c2 = xh * lng_ref[...] + lnb_ref[...]
        sg = _sigmoid(c2)
        dc2 = dc3 * (sg * (1.0 + c2 * (1.0 - sg)))
        glg_ref[...] += _rowsum(dc2 * xh)
        glb_ref[...] += _rowsum(dc2)
        dxh = dc2 * lng_ref[...]
        dc1_ref[...] = rs * (dxh - jnp.mean(dxh, axis=-1, keepdims=True) - xh * jnp.mean(dxh * xh, axis=-1, keepdims=True))

    row = lambda n: pl.BlockSpec((tm, n), lambda i: (i, 0))
    full = lambda a, b: pl.BlockSpec((a, b), lambda i: (0, 0))
    body, dep_spec, dep_arg = _anchored(body, 13, dep)
    return pl.pallas_call(
        body, grid=(T // tm,),
        in_specs=[row(D_MODEL), row(D_MODEL), row(D_MODEL), row(2 * D_MODEL), row(ATTN_OUT), row(D_MODEL),
                  full(ATTN_OUT, D_MODEL), full(D_MODEL, D_MODEL), full(D_MODEL, D_MODEL),
                  full(1, 2 * D_MODEL), full(1, D_MODEL), full(1, D_MODEL), full(ATTN_OUT, ATTN_OUT)] + dep_spec,
        out_specs=[row(2 * D_MODEL), row(D_MODEL), row(D_MODEL), row(ATTN_OUT), row(ATTN_OUT), row(D_MODEL),
                   full(1, 2 * D_MODEL), full(1, D_MODEL), full(1, D_MODEL)],
        out_shape=[SDS((T, 2 * D_MODEL), BF16), SDS((T, D_MODEL), BF16), SDS((T, D_MODEL), BF16), SDS((T, ATTN_OUT), F32),
                   SDS((T, ATTN_OUT), F32), SDS((T, D_MODEL), F32),
                   SDS((1, 2 * D_MODEL), F32), SDS((1, D_MODEL), F32), SDS((1, D_MODEL), F32)],
        compiler_params=_params("arbitrary"), name="mid_bwd")(dx1b, ya, yc, logits, att, c1, w_a, w_c, w_o, gate_b, ln_g, ln_b,
                                                               head_ones, *dep_arg)


FFN_TM = 512
FFN_TF = D_FF // 2


def _rms_bwd(dy_times_g, xh, r):
    return r * (dy_times_g - xh * jnp.mean(dy_times_g * xh, axis=-1, keepdims=True))


def _ffn_fwd(h2, x1, target, gf, w_g_t, w_u_t, w_d):
    T = h2.shape[0]
    tm, tf = FFN_TM, FFN_TF
    nf = D_FF // tf

    def body(h_ref, x1_ref, t_ref, gf_ref, wg_ref, wu_ref, wd_ref,
             a_ref, b_ref, f_ref, dx2_ref, dx2b_ref, loss_ref, gnf_ref, acc):
        i, j = pl.program_id(0), pl.program_id(1)
        h = h_ref[...]
        a = _dot_nt(h, wg_ref[...])
        b = _dot_nt(h, wu_ref[...])
        f = (a * _sigmoid(a) * b).astype(BF16)
        a_ref[...] = a.astype(BF16)
        b_ref[...] = b.astype(BF16)
        f_ref[...] = f
        p = _dot(f, wd_ref[...])

        @pl.when(j == 0)
        def _():
            acc[...] = x1_ref[...] + p

        @pl.when(j > 0)
        def _():
            acc[...] += p

        @pl.when((i == 0) & (j == nf - 1))
        def _():
            loss_ref[...] = jnp.zeros_like(loss_ref)
            gnf_ref[...] = jnp.zeros_like(gnf_ref)

        @pl.when(j == nf - 1)
        def _():
            x2 = acc[...]
            r = lax.rsqrt(jnp.mean(x2 * x2, axis=-1, keepdims=True) + RMS_EPS)
            xh = x2 * r
            err = xh * gf_ref[...] - t_ref[...]
            loss_ref[...] += (0.5 / D_MODEL) * jnp.sum(err * err)
            dy = err * (1.0 / D_MODEL)
            gnf_ref[...] += _rowsum(dy * xh)
            dx2 = _rms_bwd(dy * gf_ref[...], xh, r)
            dx2_ref[...] = dx2
            dx2b_ref[...] = dx2.astype(BF16)

    row = lambda n: pl.BlockSpec((tm, n), lambda i, j: (i, 0))
    ffb = pl.BlockSpec((tm, tf), lambda i, j: (i, j))
    wblk = pl.BlockSpec((tf, D_MODEL), lambda i, j: (j, 0))
    return pl.pallas_call(
        body, grid=(T // tm, nf),
        in_specs=[row(D_MODEL), row(D_MODEL), row(D_MODEL), pl.BlockSpec((1, D_MODEL), lambda i, j: (0, 0)),
                  wblk, wblk, wblk],
        out_specs=[ffb, ffb, ffb, row(D_MODEL), row(D_MODEL),
                   pl.BlockSpec((1, 128), lambda i, j: (0, 0)), pl.BlockSpec((1, D_MODEL), lambda i, j: (0, 0))],
        out_shape=[SDS((T, D_FF), BF16), SDS((T, D_FF), BF16), SDS((T, D_FF), BF16), SDS((T, D_MODEL), F32),
                   SDS((T, D_MODEL), BF16), SDS((1, 128), F32), SDS((1, D_MODEL), F32)],
        scratch_shapes=[pltpu.VMEM((tm, D_MODEL), F32)],
        compiler_params=_params("arbitrary", "arbitrary"), name="ffn_fwd")(h2, x1, target, gf, w_g_t, w_u_t, w_d)


def _ffn_bwd(dx2b, dx2, a, b, x1, g2, w_g_t, w_u_t, w_d):
    T = dx2.shape[0]
    tm, tf = FFN_TM, FFN_TF
    nf = D_FF // tf

    def body(dxb_ref, dx2_ref, a_ref, b_ref, x1_ref, g2_ref, wg_ref, wu_ref, wd_ref,
             da_ref, db_ref, dx1_ref, dx1b_ref, gn2_ref, acc):
        i, j = pl.program_id(0), pl.program_id(1)
        df = _dot_nt(dxb_ref[...], wd_ref[...])
        av = a_ref[...].astype(F32)
        bv = b_ref[...].astype(F32)
        sg = _sigmoid(av)
        db = (df * av * sg).astype(BF16)
        da = (df * bv * (sg * (1.0 + av * (1.0 - sg)))).astype(BF16)
        da_ref[...] = da
        db_ref[...] = db
        p = _dot(da, wg_ref[...]) + _dot(db, wu_ref[...])

        @pl.when(j == 0)
        def _():
            acc[...] = p

        @pl.when(j > 0)
        def _():
            acc[...] += p

        @pl.when((i == 0) & (j == nf - 1))
        def _():
            gn2_ref[...] = jnp.zeros_like(gn2_ref)

        @pl.when(j == nf - 1)
        def _():
            dh2 = acc[...]
            x1 = x1_ref[...]
            r = lax.rsqrt(jnp.mean(x1 * x1, axis=-1, keepdims=True) + RMS_EPS)
            xh = x1 * r
            gn2_ref[...] += _rowsum(dh2 * xh)
            dx1 = dx2_ref[...] + _rms_bwd(dh2 * g2_ref[...], xh, r)
            dx1_ref[...] = dx1
            dx1b_ref[...] = dx1.astype(BF16)

    row = lambda n: pl.BlockSpec((tm, n), lambda i, j: (i, 0))
    ffb = pl.BlockSpec((tm, tf), lambda i, j: (i, j))
    wblk = pl.BlockSpec((tf, D_MODEL), lambda i, j: (j, 0))
    return pl.pallas_call(
        body, grid=(T // tm, nf),
        in_specs=[row(D_MODEL), row(D_MODEL), ffb, ffb, row(D_MODEL), pl.BlockSpec((1, D_MODEL), lambda i, j: (0, 0)),
                  wblk, wblk, wblk],
        out_specs=[ffb, ffb, row(D_MODEL), row(D_MODEL), pl.BlockSpec((1, D_MODEL), lambda i, j: (0, 0))],
        out_shape=[SDS((T, D_FF), BF16), SDS((T, D_FF), BF16), SDS((T, D_MODEL), F32), SDS((T, D_MODEL), BF16),
                   SDS((1, D_MODEL), F32)],
        scratch_shapes=[pltpu.VMEM((tm, D_MODEL), F32)],
        compiler_params=_params("arbitrary", "arbitrary"), name="ffn_bwd")(dx2b, dx2, a, b, x1, g2, w_g_t, w_u_t, w_d)


def _in_bwd(pieces, w_in_t, x, dx1, g1, dep=None):
    T = x.shape[0]
    tm = IN_TM
    npc = len(pieces)
    assert sum(p.shape[1] for p in pieces) == IN_WIDTH

    def body(*refs):
        p_refs = refs[:npc]
        w_hbm, x_ref, dx1_ref, g_ref, dx_ref, gn1_ref, w_vmem, sem = refs[npc:]

        @pl.when(pl.program_id(0) == 0)
        def _():
            cp = pltpu.make_async_copy(w_hbm, w_vmem, sem)
            cp.start()
            cp.wait()
            gn1_ref[...] = jnp.zeros_like(gn1_ref)

        dh = jnp.zeros((tm, D_MODEL), F32)
        col = 0
        for p_ref in p_refs:
            for j in range(p_ref.shape[1] // IN_CHUNK):
                dh = dh + _dot(p_ref[:, j * IN_CHUNK:(j + 1) * IN_CHUNK], w_vmem[col:col + IN_CHUNK, :])
                col += IN_CHUNK
        xv = x_ref[...]
        r = lax.rsqrt(jnp.mean(xv * xv, axis=-1, keepdims=True) + RMS_EPS)
        xh = xv * r
        gn1_ref[...] += _rowsum(dh * xh)
        dx_ref[...] = dx1_ref[...] + _rms_bwd(dh * g_ref[...], xh, r)

    row = lambda n: pl.BlockSpec((tm, n), lambda i: (i, 0))
    body, dep_spec, dep_arg = _anchored(body, npc + 4, dep)
    return pl.pallas_call(
        body, grid=(T // tm,),
        in_specs=[row(p.shape[1]) for p in pieces]
        + [pl.BlockSpec(memory_space=pl.ANY), row(D_MODEL), row(D_MODEL), pl.BlockSpec((1, D_MODEL), lambda i: (0, 0))]
        + dep_spec,
        out_specs=[row(D_MODEL), pl.BlockSpec((1, D_MODEL), lambda i: (0, 0))],
        out_shape=[SDS((T, D_MODEL), F32), SDS((1, D_MODEL), F32)],
        scratch_shapes=[pltpu.VMEM((IN_WIDTH, D_MODEL), BF16), pltpu.SemaphoreType.DMA],
        compiler_params=_params("arbitrary"), name="in_bwd")(*pieces, w_in_t, x, dx1, g1, *dep_arg)


def _local_step(x, target, w, small, dep=None, late_weights=None, emit=None):
    T = x.shape[0]
    batch = T // SEQ
    slopes_r = jnp.asarray(_slopes_times_dilation())
    emit = emit or (lambda names, grads: None)

    h, qkv, u, logits = _in_proj(x, small["norm1_g"], w["w_in"], dep)

    qkv3 = qkv.reshape(batch, SEQ, 3 * ATTN_WIDTH)
    att, lse = _attn_fwd(qkv3, slopes_r, batch)
    att = att.reshape(T, ATTN_OUT)

    u3 = u.reshape(batch, SEQ, 2 * D_MODEL)
    c1 = _conv_fwd(u3, w["conv_w"], small["conv_b"], batch).reshape(T, D_MODEL)
    if late_weights is not None:
        w = {**w, **late_weights(LATE_MERGE, (att, c1))}

    c3, ya, yc, mix, x1, h2 = _mid_fwd(
        att, c1, logits, x, w["w_attn_out"], w["w_conv_out"], w["w_o"],
        small["gate_b"], small["conv_ln_g"], small["conv_ln_b"], small["norm2_g"], w.get("token"))
    if late_weights is not None:
        w = {**w, **late_weights(LATE_FFN, h2)}

    a, b, f, dx2, dx2b, loss, g_normf = _ffn_fwd(h2, x1, target, small["norm_f_g"],
                                                   w["w_ffn_gate"], w["w_ffn_up"], w["w_ffn_down"])

    da, db, dx1, dx1b, g_norm2 = _ffn_bwd(dx2b, dx2, a, b, x1, small["norm2_g"],
                                           w["w_ffn_gate"], w["w_ffn_up"], w["w_ffn_down"])
    gw = {}
    gw["w_ffn_down"] = _mm_tn(f, dx2b, BF16, "gw_ffn_down", tn=1024)
    gw["w_ffn_gate"] = _mm_tn(da, h2, BF16, "gw_ffn_gate", tn=1024)
    gw["w_ffn_up"] = _mm_tn(db, h2, BF16, "gw_ffn_up", tn=1024)
    token = emit(("w_ffn_gate", "w_ffn_up", "w_ffn_down"), gw)

    head_ones = jnp.asarray(np.kron(np.eye(HEADS_PER_GROUP, dtype=np.float32), np.ones((HEAD_DIM, HEAD_DIM), np.float32)))
    dlogits, dya, dyc, datt, dsum, dc1, g_gate_b, g_ln_g, g_ln_b = _mid_bwd(
        dx1b, ya, yc, logits, att, c1, w["w_attn_out"], w["w_conv_out"], w["w_o"],
        small["gate_b"], small["conv_ln_g"], small["conv_ln_b"], head_ones, token)
    gw["w_o"] = _mm_tn(mix, dx1b, BF16, "gw_o", tn=1024)
    gw["w_attn_out"] = _mm_tn(att, dya, BF16, "gw_attn_out", tn=1024)
    gw["w_conv_out"] = _mm_tn(c3, dyc, BF16, "gw_conv_out", tn=1024)
    token = emit(("w_conv_out", "w_attn_out", "w_o"), gw)

    dua, dub, g_conv_w, g_conv_b = _conv_bwd(u3, dc1.reshape(batch, SEQ, D_MODEL), w["conv_w"], batch, token)

    dq, dk, dv = _attn_bwd(qkv3, datt.reshape(batch, SEQ, ATTN_OUT), lse, dsum.reshape(batch, SEQ, ATTN_OUT),
                           slopes_r, batch)
    pieces = [dq.reshape(T, ATTN_WIDTH), dk.reshape(T, ATTN_WIDTH), dv.reshape(T, ATTN_WIDTH),
              dua.reshape(T, D_MODEL), dub.reshape(T, D_MODEL), dlogits]

    names = ("q", "k", "v", "ua", "ub", "gate")
    gw["w_in"] = jnp.concatenate([_mm_tn(p, h, BF16, "gw_in_" + nm, tn=1024) for nm, p in zip(names, pieces)], axis=0)
    gw["conv_w"] = g_conv_w
    token = emit(("w_in", "conv_w"), gw)
    grad_x, g_norm1 = _in_bwd(pieces, w["w_in"], x, dx1, small["norm1_g"], token)

    gsmall = {"norm1_g": g_norm1, "gate_b": g_gate_b, "conv_b": g_conv_b, "conv_ln_g": g_ln_g, "conv_ln_b": g_ln_b,
              "norm2_g": g_norm2, "norm_f_g": g_normf}
    return loss, grad_x, gw, gsmall


ANY = pl.BlockSpec(memory_space=pl.ANY)


def _all_gather(arrs):
    n = len(arrs)

    def body(*refs):
        ins, outs = refs[:n], refs[n:2 * n]
        send_sems, recv_sems, local_sems = refs[2 * n:]
        x, y, c = lax.axis_index("x"), lax.axis_index("y"), lax.axis_index("c")
        me, sibling = (x, y, c), (x, y, 1 - c)
        chips = [(1 - x, y), (x, 1 - y), (1 - x, 1 - y)]

        def copy(a, k, block, to, src=None):
            px, py, pc = block
            dst = outs[a].at[4 * px + 2 * py + pc]
            return pltpu.make_async_remote_copy(
                src_ref=dst if src is None else src, dst_ref=dst,
                send_sem=send_sems.at[a, k], recv_sem=recv_sems.at[a, k], device_id=to, device_id_type=MESH)

        mine = [pltpu.make_async_copy(ins[a], outs[a].at[4 * x + 2 * y + c], local_sems.at[a]) for a in range(n)]
        for cp in mine:
            cp.start()
        first = []
        for j, chip in enumerate(chips):
            first += [copy(a, 1 + j, me, (*chip, c), src=ins[a]) for a in range(n)]
        first += [copy(a, 0, me, sibling, src=ins[a]) for a in range(n)]
        for cp in first:
            cp.start()
        passed = []
        for j, chip in enumerate(chips):
            for a in range(n):
                copy(a, 1 + j, (*chip, c), me).wait_recv()
                cp = copy(a, 4 + j, (*chip, c), sibling)
                cp.start()
                passed.append(cp)
        for a in range(n):
            copy(a, 0, sibling, me).wait_recv()
        for j, chip in enumerate(chips):
            for a in range(n):
                copy(a, 4 + j, (*chip, 1 - c), me).wait_recv()
        for cp in first + passed:
            cp.wait_send()
        for cp in mine:
            cp.wait()

    return pl.pallas_call(
        body, in_specs=[ANY] * n, out_specs=[ANY] * n,
        out_shape=[SDS((N_DEV,) + a.shape, a.dtype) for a in arrs],
        scratch_shapes=[pltpu.SemaphoreType.DMA((n, 7)), pltpu.SemaphoreType.DMA((n, 7)), pltpu.SemaphoreType.DMA((n,))],
        name="all_gather_weights")(*arrs)


HBM = pl.BlockSpec(memory_space=pltpu.HBM)
SEM = pl.BlockSpec(memory_space=pltpu.SEMAPHORE)
ALL_PEERS = tuple(range(1, N_DEV))
OTHER_CHIPS = (2, 4, 6)
SPLIT_EFFECT = pltpu.CompilerParams(has_side_effects=pltpu.SideEffectType.DATAFLOW_SIDE_EFFECTING)


def _exchange_copies(mode, ks, srcs, lands, send_sems, recv_sems):
    x, y, c = lax.axis_index("x"), lax.axis_index("y"), lax.axis_index("c")
    me = 4 * x + 2 * y + c
    send, recv = [], []
    for a in range(len(lands)):
        for i, k in enumerate(ks):
            peer = (x ^ ((k >> 2) & 1), y ^ ((k >> 1) & 1), c ^ (k & 1))
            pidx = 4 * peer[0] + 2 * peer[1] + peer[2]
            if mode == "gather":
                src, to, out_slot, in_slot = srcs[a], peer, me, pidx
            elif mode == "scatter":
                src, to, out_slot, in_slot = srcs[a].at[pidx], peer, me, pidx
            elif mode == "chip_scatter":
                src, to, out_slot, in_slot = srcs[a].at[pidx >> 1], peer, me >> 1, pidx >> 1
            else:
                src, to, out_slot, in_slot = lands[a].at[pidx], (x, y, 1 - c), pidx, pidx ^ 1
            s = a * len(ks) + i
            send.append(pltpu.make_async_remote_copy(
                src_ref=src, dst_ref=lands[a].at[out_slot], send_sem=send_sems.at[s], recv_sem=recv_sems.at[s],
                device_id=to, device_id_type=MESH))
            recv.append(pltpu.make_async_remote_copy(
                src_ref=src, dst_ref=lands[a].at[in_slot], send_sem=send_sems.at[s], recv_sem=recv_sems.at[s],
                device_id=to, device_id_type=MESH))
    return send, recv


def _send_start(mode, ks, name, srcs=(), lands=None, dep=None):
    srcs = list(srcs)
    if lands is None:
        slots = 4 if mode == "chip_scatter" else N_DEV
        lands = [lax.empty((slots,) + (s.shape if mode == "gather" else s.shape[1:]), s.dtype) for s in srcs]
    ns, nl = len(srcs), len(lands)
    nsem = nl * len(ks)

    def body(*refs):
        send, _ = _exchange_copies(mode, ks, refs[:ns], refs[ns:ns + nl], refs[ns + nl], refs[ns + nl + 1])
        for cp in send:
            cp.start()
        token = refs[-1]
        token[...] = jnp.zeros_like(token)

    both = srcs + list(lands)
    body, dep_spec, dep_arg = _anchored(body, ns + nl, dep)
    res = pl.pallas_call(
        body, name=name,
        out_shape=(pltpu.SemaphoreType.DMA((nsem,)), pltpu.SemaphoreType.DMA((nsem,)),
                   *[pltpu.HBM(a.shape, a.dtype) for a in both], SDS((8, 128), F32)),
        in_specs=[HBM] * (ns + nl) + dep_spec,
        out_specs=(SEM, SEM, *([HBM] * (ns + nl)), pl.BlockSpec(memory_space=pltpu.VMEM)),
        input_output_aliases={i: 2 + i for i in range(ns + nl)}, compiler_params=SPLIT_EFFECT,
    )(*[pltpu.with_memory_space_constraint(a, pltpu.HBM) for a in both], *dep_arg)
    return dict(mode=mode, ks=ks, send_sems=res[0], recv_sems=res[1], srcs=res[2:2 + ns], lands=res[2 + ns:2 + ns + nl],
                token=res[-1])


def _send_wait(started, after, name):
    ns, nl = len(started["srcs"]), len(started["lands"])

    def body(*refs):
        send, recv = _exchange_copies(started["mode"], started["ks"], refs[:ns], refs[ns:ns + nl],
                                      refs[ns + nl], refs[ns + nl + 1])
        for cp in send:
            cp.wait_send()
        for cp in recv:
            cp.wait_recv()

    both = list(started["srcs"]) + list(started["lands"])
    after = after if isinstance(after, (tuple, list)) else (after,)
    res = pl.pallas_call(
        body, name=name,
        out_shape=tuple(pltpu.HBM(a.shape, a.dtype) for a in both),
        in_specs=[HBM] * (ns + nl) + [SEM, SEM] + [ANY] * len(after), out_specs=tuple([HBM] * (ns + nl)),
        input_output_aliases={i: i for i in range(ns + nl)}, compiler_params=SPLIT_EFFECT,
    )(*both, started["send_sems"], started["recv_sems"], *after)
    return res[:ns], res[ns:]


def _exchange_sibling(gs):
    n = len(gs)

    def body(*refs):
        ins, outs = refs[:n], refs[n:2 * n]
        send_sems, recv_sems = refs[2 * n:]
        x, y, c = lax.axis_index("x"), lax.axis_index("y"), lax.axis_index("c")
        copies = []
        for a in range(n):
            for j in range(4):
                copies.append(pltpu.make_async_remote_copy(
                    src_ref=ins[a].at[2 * j + (1 - c)], dst_ref=outs[a].at[j],
                    send_sem=send_sems.at[a, j], recv_sem=recv_sems.at[a, j],
                    device_id=(x, y, 1 - c), device_id_type=MESH))
        for cp in copies:
            cp.start()
        for cp in copies:
            cp.wait_recv()
        for cp in copies:
            cp.wait_send()

    return pl.pallas_call(
        body, in_specs=[ANY] * n, out_specs=[ANY] * n,
        out_shape=[SDS((4,) + g.shape[1:], g.dtype) for g in gs],
        scratch_shapes=[pltpu.SemaphoreType.DMA((n, 4)), pltpu.SemaphoreType.DMA((n, 4))],
        name="reduce_scatter_sibling")(*gs)


def _add_pair(g, r1, core, name):
    _, rows, cols = g.shape
    tr = _row_tile(rows, cols, 3 * g.dtype.itemsize)

    def body(c_ref, g_ref, r_ref, o_ref):
        o_ref[...] = (g_ref[...].astype(F32) + r_ref[...].astype(F32)).astype(o_ref.dtype)

    return pl.pallas_call(
        body,
        grid_spec=pltpu.PrefetchScalarGridSpec(
            num_scalar_prefetch=1, grid=(4, rows // tr),
            in_specs=[pl.BlockSpec((1, tr, cols), lambda j, i, c_ref: (2 * j + c_ref[0], i, 0)),
                      pl.BlockSpec((1, tr, cols), lambda j, i, c_ref: (j, i, 0))],
            out_specs=pl.BlockSpec((1, tr, cols), lambda j, i, c_ref: (j, i, 0))),
        out_shape=SDS((4, rows, cols), g.dtype),
        compiler_params=_params("parallel", "parallel"), name=name)(core, g, r1)


def _row_tile(rows, cols, itemsize_total):
    budget = (4 << 20) // max(1, cols * itemsize_total)
    if rows <= budget:
        return rows
    t = rows
    while t > budget and t % 2 == 0 and (t // 2) % 16 == 0:
        t //= 2
    return t


def _adam_math(g, w, m, v):
    m_new = ADAM_B1 * m + (1.0 - ADAM_B1) * g
    v_new = ADAM_B2 * v + (1.0 - ADAM_B2) * (g * g)
    m_hat = m_new / (1.0 - ADAM_B1 ** ADAM_STEP)
    v_hat = v_new / (1.0 - ADAM_B2 ** ADAM_STEP)
    delta = -ADAM_LR * (m_hat / (jnp.sqrt(v_hat) + ADAM_EPS) + ADAM_WD * w)
    return delta, m_new, v_new


def _sum_adam(parts, w, m, v, name):
    rows, cols = w.shape
    nparts = parts.shape[0]
    tr = _row_tile(rows, cols, nparts * parts.dtype.itemsize + 7 * 4)

    def body(p_ref, w_ref, m_ref, v_ref, g_ref, d_ref, mo_ref, vo_ref):
        g = p_ref[0].astype(F32)
        for s in range(1, nparts):
            g = g + p_ref[s].astype(F32)
        delta, m_new, v_new = _adam_math(g, w_ref[...], m_ref[...], v_ref[...])
        g_ref[...] = g
        d_ref[...] = delta
        mo_ref[...] = m_new
        vo_ref[...] = v_new

    blk = pl.BlockSpec((tr, cols), lambda i: (i, 0))
    out = SDS((rows, cols), F32)
    return pl.pallas_call(
        body, grid=(rows // tr,),
        in_specs=[pl.BlockSpec((nparts, tr, cols), lambda i: (0, i, 0)), blk, blk, blk],
        out_specs=[blk, blk, blk, blk], out_shape=[out, out, out, out],
        compiler_params=_params("parallel"), name=name)(parts, w, m, v)


SMALL_ROWS = 72


def _small_allreduce_adam(gpart, w, m, v, dep=None):
    def body(g_ref, w_ref, m_ref, v_ref, go_ref, d_ref, mo_ref, vo_ref, gath, send_sems, recv_sems):
        x, y, c = lax.axis_index("x"), lax.axis_index("y"), lax.axis_index("c")
        me = 4 * x + 2 * y + c
        gath[me] = g_ref[...]
        copies = []
        for k in range(1, N_DEV):
            fx, fy, fc = (k >> 2) & 1, (k >> 1) & 1, k & 1
            peer = (x ^ fx, y ^ fy, c ^ fc)
            copies.append(pltpu.make_async_remote_copy(
                src_ref=gath.at[me], dst_ref=gath.at[me], send_sem=send_sems.at[k - 1], recv_sem=recv_sems.at[k - 1],
                device_id=peer, device_id_type=MESH))
        for cp in copies:
            cp.start()
        for cp in copies:
            cp.wait_recv()
        for cp in copies:
            cp.wait_send()
        g = gath[0]
        for d in range(1, N_DEV):
            g = g + gath[d]
        delta, m_new, v_new = _adam_math(g, w_ref[...], m_ref[...], v_ref[...])
        go_ref[...] = g
        d_ref[...] = delta
        mo_ref[...] = m_new
        vo_ref[...] = v_new

    vm = pl.BlockSpec(memory_space=pltpu.VMEM)
    out = SDS((SMALL_ROWS, 128), F32)
    body, dep_spec, dep_arg = _anchored(body, 4, dep)
    return pl.pallas_call(
        body, in_specs=[vm] * 4 + dep_spec, out_specs=[vm] * 4, out_shape=[out] * 4,
        scratch_shapes=[pltpu.VMEM((N_DEV, SMALL_ROWS, 128), F32), pltpu.SemaphoreType.DMA((N_DEV - 1,)),
                        pltpu.SemaphoreType.DMA((N_DEV - 1,))],
        name="small_allreduce_adam")(gpart, w, m, v, *dep_arg)


BIG = ("w_in", "conv_w", "w_conv_out", "w_attn_out", "w_o", "w_ffn_gate", "w_ffn_up", "w_ffn_down")
EARLY = ("w_in", "conv_w")
LATE_MERGE = ("w_conv_out", "w_attn_out", "w_o")
LATE_FFN = ("w_ffn_gate", "w_ffn_up", "w_ffn_down")
TRANSPOSED = ("w_in", "w_ffn_gate", "w_ffn_up")
COL_SHARDED = ("conv_w", "w_attn_out")
SMALL = ("norm1_g", "gate_b", "conv_b", "conv_ln_g", "conv_ln_b", "norm2_g", "norm_f_g")
WEIGHTS = ("norm1_g", "w_in", "gate_b", "conv_w", "conv_b", "conv_ln_g", "conv_ln_b", "w_conv_out", "w_attn_out", "w_o",
           "norm2_g", "w_ffn_gate", "w_ffn_up", "w_ffn_down", "norm_f_g")


def _shard2d(name, a):
    a = a.reshape(a.shape[-2], a.shape[-1])
    if name in TRANSPOSED:
        a = a.T
    if name == "conv_w":
        a = jnp.pad(a, ((0, CONV_PAD - CONV_K), (0, 0)))
    return a


def _from_shard2d(name, val, shape):
    if name in TRANSPOSED:
        val = val.T
    if name == "conv_w":
        val = val[:CONV_K]
    return val.reshape(shape)


def _gathered_to_full(name, g):
    if name in COL_SHARDED:
        return g.transpose(1, 0, 2).reshape(g.shape[1], N_DEV * g.shape[2])
    return g.reshape(N_DEV * g.shape[1], g.shape[2])


def _full_to_blocks(name, g):
    if name in COL_SHARDED:
        return g.reshape(g.shape[0], N_DEV, g.shape[1] // N_DEV).transpose(1, 0, 2)
    return g.reshape(N_DEV, g.shape[0] // N_DEV, g.shape[1])


def _pack_small(d, last_rows):
    vec = jnp.concatenate([d[n].reshape(-1) for n in SMALL]).reshape(SMALL_ROWS - SUBLANES, 128)
    return jnp.concatenate([vec, last_rows], axis=0)


def _unpack_small(p, like):
    flat = p.reshape(-1)
    out, off = {}, 0
    for n in SMALL:
        size = like[n].size
        out[n] = flat[off:off + size].reshape(like[n].shape)
        off += size
    return out


def kernel(x, norm1_g, w_in, gate_b, conv_w, conv_b, conv_ln_g, conv_ln_b, w_conv_out, w_attn_out, w_o, norm2_g, w_ffn_gate, w_ffn_up, w_ffn_down, norm_f_g, loss_target, m_norm1_g, m_w_in, m_gate_b, m_conv_w, m_conv_b, m_conv_ln_g, m_conv_ln_b, m_w_conv_out, m_w_attn_out, m_w_o, m_norm2_g, m_w_ffn_gate, m_w_ffn_up, m_w_ffn_down, m_norm_f_g, v_norm1_g, v_w_in, v_gate_b, v_conv_w, v_conv_b, v_conv_ln_g, v_conv_ln_b, v_w_conv_out, v_w_attn_out, v_w_o, v_norm2_g, v_w_ffn_gate, v_w_ffn_up, v_w_ffn_down, v_norm_f_g):
    wts = dict(norm1_g=norm1_g, w_in=w_in, gate_b=gate_b, conv_w=conv_w, conv_b=conv_b, conv_ln_g=conv_ln_g,
               conv_ln_b=conv_ln_b, w_conv_out=w_conv_out, w_attn_out=w_attn_out, w_o=w_o, norm2_g=norm2_g,
               w_ffn_gate=w_ffn_gate, w_ffn_up=w_ffn_up, w_ffn_down=w_ffn_down, norm_f_g=norm_f_g)
    mom1 = dict(norm1_g=m_norm1_g, w_in=m_w_in, gate_b=m_gate_b, conv_w=m_conv_w, conv_b=m_conv_b, conv_ln_g=m_conv_ln_g,
                conv_ln_b=m_conv_ln_b, w_conv_out=m_w_conv_out, w_attn_out=m_w_attn_out, w_o=m_w_o, norm2_g=m_norm2_g,
                w_ffn_gate=m_w_ffn_gate, w_ffn_up=m_w_ffn_up, w_ffn_down=m_w_ffn_down, norm_f_g=m_norm_f_g)
    mom2 = dict(norm1_g=v_norm1_g, w_in=v_w_in, gate_b=v_gate_b, conv_w=v_conv_w, conv_b=v_conv_b, conv_ln_g=v_conv_ln_g,
                conv_ln_b=v_conv_ln_b, w_conv_out=v_w_conv_out, w_attn_out=v_w_attn_out, w_o=v_w_o, norm2_g=v_norm2_g,
                w_ffn_gate=v_w_ffn_gate, w_ffn_up=v_w_ffn_up, w_ffn_down=v_w_ffn_down, norm_f_g=v_norm_f_g)

    T = x.shape[0] * x.shape[1]
    x2 = x.reshape(T, D_MODEL)
    t2 = loss_target.reshape(T, D_MODEL)

    me = 4 * lax.axis_index("x") + 2 * lax.axis_index("y") + lax.axis_index("c")
    shards = {n: _shard2d(n, wts[n]) for n in BIG}
    sent = {n: shards[n] if n == "conv_w" else shards[n].astype(BF16) for n in BIG}
    small = {n: wts[n].reshape(1, -1) for n in SMALL}

    gathered = _all_gather([sent[n] for n in EARLY])
    full = {n: _gathered_to_full(n, g) for n, g in zip(EARLY, gathered)}
    merge_gather = _send_start("gather", ALL_PEERS, "gather_start_merge", [sent[n] for n in LATE_MERGE], dep=gathered[0])
    ffn_gather = _send_start("gather", (1,) + OTHER_CHIPS, "gather_start_ffn", [sent[n] for n in LATE_FFN],
                             dep=merge_gather["token"])
    ffn_state = {}

    def filled(names, srcs, lands):
        return {n: _gathered_to_full(n, lax.dynamic_update_slice(land, src[None], (me, 0, 0)))
                for n, src, land in zip(names, srcs, lands)}

    def late_weights(names, after):
        if names is LATE_MERGE:
            srcs, lands = _send_wait(merge_gather, after, "gather_wait_merge")
            ffn_state["srcs"], ffn_lands = _send_wait(ffn_gather, after, "gather_wait_ffn")
            ffn_state["forward"] = _send_start("forward", OTHER_CHIPS, "forward_start_ffn", lands=ffn_lands)
            return {**filled(names, srcs, lands), "token": ffn_state["forward"]["token"]}
        _, lands = _send_wait(ffn_state["forward"], after, "forward_wait_ffn")
        return filled(names, ffn_state["srcs"], lands)

    scatters = []
    core = lax.axis_index("c").astype(jnp.int32).reshape(1)

    def emit(names, gw):
        blocks = [_full_to_blocks(n, gw[n]) for n in names]
        if "w_in" in names:
            sums = [_add_pair(g, r, core, "chip_sum_" + n) for n, g, r in zip(names, blocks, _exchange_sibling(blocks))]
            started = _send_start("chip_scatter", OTHER_CHIPS, "scatter_start_" + names[0], sums)
        else:
            started = _send_start("scatter", ALL_PEERS, "scatter_start_" + names[0], blocks)
        scatters.append((names, started))
        return started["token"]

    loss_part, grad_x, gw, gsmall = _local_step(x2, t2, full, small, ffn_gather["token"], late_weights, emit)

    grads, deltas, new_m, new_v = {}, {}, {}, {}
    after = grad_x
    for names, started in scatters:
        srcs, lands = _send_wait(started, after, "scatter_wait_" + names[0])
        mine = me >> 1 if started["mode"] == "chip_scatter" else me
        for n, src, land in zip(names, srcs, lands):
            parts = lax.dynamic_update_slice(land, lax.dynamic_slice_in_dim(src, mine, 1, axis=0), (mine, 0, 0))
            g, d, mo, vo = _sum_adam(parts, shards[n], _shard2d(n, mom1[n]), _shard2d(n, mom2[n]), "adam_" + n)
            for dst, val in ((grads, g), (deltas, d), (new_m, mo), (new_v, vo)):
                dst[n] = _from_shard2d(n, val, wts[n].shape)
            after = g

    zeros, ones = jnp.zeros((SUBLANES, 128), F32), jnp.ones((SUBLANES, 128), F32)
    sg, sd, sm, sv = _small_allreduce_adam(
        _pack_small(gsmall, jnp.broadcast_to(loss_part, (SUBLANES, 128))), _pack_small(wts, zeros),
        _pack_small(mom1, zeros), _pack_small(mom2, ones), after)
    for dst, val in ((grads, sg), (deltas, sd), (new_m, sm), (new_v, sv)):
        dst.update(_unpack_small(val, wts))
    loss = sg[SMALL_ROWS - SUBLANES, 0]
    return (loss, grad_x.reshape(x.shape), *[grads[n] for n in WEIGHTS], *[deltas[n] for n in WEIGHTS],
            *[new_m[n] for n in WEIGHTS], *[new_v[n] for n in WEIGHTS])
```

```python
import math

import numpy as np
import jax
import jax.numpy as jnp
from jax import lax
from jax.experimental import pallas as pl
from jax.experimental.pallas import tpu as pltpu

F32 = jnp.float32
BF16 = jnp.bfloat16
SDS = jax.ShapeDtypeStruct
MESH = pl.DeviceIdType.MESH

D_MODEL = 1024
SEQ = 2048
HEAD_DIM = 64
GROUPS = ((128, 1), (512, 4), (2048, 16))
HEADS_PER_GROUP = 8
N_HEADS = 24
ATTN_WIDTH = N_HEADS * HEAD_DIM
ATTN_OUT = HEADS_PER_GROUP * HEAD_DIM
CONV_K = 31
CONV_PAD = 32
D_FF = 2816
IN_WIDTH = 3 * ATTN_WIDTH + 2 * D_MODEL + 2 * D_MODEL
RMS_EPS = 1e-6
LN_EPS = 1e-5
Q_BLOCK = 128
LANES = 128
NEG = -1e30
N_DEV = 8

ADAM_LR = 0.001
ADAM_B1 = 0.9
ADAM_B2 = 0.999
ADAM_EPS = 1e-08
ADAM_WD = 0.01
ADAM_STEP = 10


def _alibi_slope_list(n):
    def pow2(m):
        start = 2.0 ** (-8.0 / m)
        return [start ** (i + 1) for i in range(m)]
    if math.log2(n).is_integer():
        return pow2(n)
    c = 2 ** math.floor(math.log2(n))
    return pow2(c) + _alibi_slope_list(2 * c)[0::2][: n - c]


def _slopes_times_dilation():
    s = np.asarray(sorted(_alibi_slope_list(N_HEADS), reverse=True), dtype=np.float32).reshape(3, HEADS_PER_GROUP)
    r = np.asarray([g[1] for g in GROUPS], dtype=np.float32)[:, None]
    return (s * r).reshape(N_HEADS)


def _sigmoid(x):
    return 0.5 * jnp.tanh(0.5 * x) + 0.5


def _dot(a, b):
    return jnp.dot(a, b, preferred_element_type=F32)


def _dot_nt(a, b):
    return lax.dot_general(a, b, (((1,), (1,)), ((), ())), preferred_element_type=F32)


def _dot_tn(a, b):
    return lax.dot_general(a, b, (((0,), (0,)), ((), ())), preferred_element_type=F32)


def _rowsum(x):
    return jnp.sum(x, axis=0, keepdims=True)


def _params(*sem):
    return pltpu.CompilerParams(dimension_semantics=sem)


def _anchored(body, n_in, dep):
    if dep is None:
        return body, [], []

    def wrapped(*refs):
        return body(*refs[:n_in], *refs[n_in + 1:])

    return wrapped, [pl.BlockSpec(memory_space=pl.ANY)], [dep]


IN_TM = 256
IN_CHUNK = 512


def _in_proj(x, g1, w_in_t, dep=None):
    T = x.shape[0]
    tm = IN_TM
    widths = (3 * ATTN_WIDTH, 2 * D_MODEL, 2 * D_MODEL)

    def body(x_ref, g_ref, w_hbm, h_ref, qkv_ref, u_ref, lg_ref, w_vmem, sem):
        @pl.when(pl.program_id(0) == 0)
        def _():
            cp = pltpu.make_async_copy(w_hbm, w_vmem, sem)
            cp.start()
            cp.wait()

        xv = x_ref[...]
        r = lax.rsqrt(jnp.mean(xv * xv, axis=-1, keepdims=True) + RMS_EPS)
        h = (xv * r * g_ref[...]).astype(BF16)
        h_ref[...] = h
        col = 0
        for o_ref, width in zip((qkv_ref, u_ref, lg_ref), widths):
            for j in range(width // IN_CHUNK):
                o_ref[:, j * IN_CHUNK:(j + 1) * IN_CHUNK] = _dot_nt(h, w_vmem[col:col + IN_CHUNK, :])
                col += IN_CHUNK

    row = lambda n: pl.BlockSpec((tm, n), lambda i: (i, 0))
    body, dep_spec, dep_arg = _anchored(body, 3, dep)
    return pl.pallas_call(
        body, grid=(T // tm,),
        in_specs=[row(D_MODEL), pl.BlockSpec((1, D_MODEL), lambda i: (0, 0)), pl.BlockSpec(memory_space=pl.ANY)] + dep_spec,
        out_specs=[row(D_MODEL)] + [row(n) for n in widths],
        out_shape=[SDS((T, D_MODEL), BF16)] + [SDS((T, n), F32) for n in widths],
        scratch_shapes=[pltpu.VMEM((IN_WIDTH, D_MODEL), BF16), pltpu.SemaphoreType.DMA],
        compiler_params=_params("arbitrary"), name="in_proj")(x, g1, w_in_t, *dep_arg)


def _mm_tn(a, b, out_dtype, name, tn, tt=1024):
    T, K = a.shape
    N = b.shape[1]
    nt = T // tt

    def body(a_ref, b_ref, o_ref, acc):
        t = pl.program_id(1)

        @pl.when(t == 0)
        def _():
            acc[...] = jnp.zeros_like(acc)

        acc[...] += _dot_tn(a_ref[...], b_ref[...])

        @pl.when(t == nt - 1)
        def _():
            o_ref[...] = acc[...].astype(o_ref.dtype)

    return pl.pallas_call(
        body, grid=(N // tn, nt),
        in_specs=[pl.BlockSpec((tt, K), lambda j, t: (t, 0)),
                  pl.BlockSpec((tt, tn), lambda j, t: (t, j))],
        out_specs=pl.BlockSpec((K, tn), lambda j, t: (0, j)),
        out_shape=SDS((K, N), out_dtype),
        scratch_shapes=[pltpu.VMEM((K, tn), F32)],
        compiler_params=_params("parallel", "arbitrary"), name=name)(a, b)


def _gather_classes(src_ref, dst, r, row0=0):
    L = SEQ // r
    for c in range(r):
        dst[row0 + c * L:row0 + (c + 1) * L, :] = src_ref[0, pl.ds(c, L, stride=r), :].astype(dst.dtype)


def _scatter_classes(src, dst, r, row0=0):
    L = SEQ // r
    for c in range(r):
        dst[pl.ds(c, L, stride=r), :] = src[row0 + c * L:row0 + (c + 1) * L, :].astype(dst.dtype)


def _attn_masks(slope_r):
    qi = lax.broadcasted_iota(jnp.int32, (Q_BLOCK, Q_BLOCK), 0)
    kj = lax.broadcasted_iota(jnp.int32, (Q_BLOCK, Q_BLOCK), 1)
    rel = (qi - kj).astype(F32)
    bias_cur = jnp.where(qi >= kj, -slope_r * rel, NEG)
    bias_prev = jnp.where(qi <= kj, -slope_r * (rel + float(Q_BLOCK)), NEG)
    return bias_cur, bias_prev


def _store_biases(bias, sl_ref, g, hp):
    for hh in range(2):
        cur, prev = _attn_masks(sl_ref[g * HEADS_PER_GROUP + 2 * hp + hh])
        rows = slice(hh * Q_BLOCK, (hh + 1) * Q_BLOCK)
        bias[0, rows, 0:Q_BLOCK] = prev
        bias[1, rows, 0:Q_BLOCK] = jnp.full((Q_BLOCK, Q_BLOCK), NEG, F32)
        bias[0, rows, Q_BLOCK:] = cur
        bias[1, rows, Q_BLOCK:] = cur


def _transpose_pairs(src, dst):
    dst[0, :, 0:Q_BLOCK] = jnp.zeros((LANES, Q_BLOCK), dst.dtype)
    nblk = SEQ // Q_BLOCK
    for b in range(nblk):
        t = src[(b + 1) * Q_BLOCK:(b + 2) * Q_BLOCK, :].T
        dst[b, :, Q_BLOCK:] = t
        if b + 1 < nblk:
            dst[b + 1, :, 0:Q_BLOCK] = t


def _stack_heads(t, low):
    z = jnp.zeros_like(t)
    return jnp.concatenate([jnp.where(low, t, z), jnp.where(low, z, t)], axis=0)


def _unstack_heads(t2, low):
    return jnp.where(low, t2[:Q_BLOCK], t2[Q_BLOCK:])


def _unit_offsets(u, nb):
    off = pl.multiple_of(u * Q_BLOCK, Q_BLOCK)
    n = u & (nb - 1)
    c = u >> int(math.log2(nb))
    return off, n == 0, c, n


ATTN_UNROLL = 4


def _attn_fwd(qkv, slopes_r, batch):
    nblk = SEQ // Q_BLOCK

    def body(sl_ref, *refs):
        qkv_refs = refs[:9]
        att_ref, lse_ref = refs[9:11]
        qd, kd, vd, kt, opos, lpos, bias = refs[11:]
        hp = pl.program_id(1)
        low = lax.broadcasted_iota(jnp.int32, (Q_BLOCK, LANES), 1) < HEAD_DIM

        for g in range(3):
            r = GROUPS[g][1]
            nb = SEQ // r // Q_BLOCK
            _gather_classes(qkv_refs[3 * g], qd, r)
            kd[0:Q_BLOCK, :] = jnp.zeros((Q_BLOCK, LANES), BF16)
            vd[0:Q_BLOCK, :] = jnp.zeros((Q_BLOCK, LANES), BF16)
            _gather_classes(qkv_refs[3 * g + 1], kd, r, Q_BLOCK)
            _gather_classes(qkv_refs[3 * g + 2], vd, r, Q_BLOCK)
            _transpose_pairs(kd, kt)
            _store_biases(bias, sl_ref, g, hp)

            def unit(u, carry, g=g, r=r, nb=nb):
                off, first, c, n = _unit_offsets(u, nb)
                q2 = _stack_heads(qd[pl.ds(off, Q_BLOCK), :], low)
                s = _dot(q2, kt[u]) * 0.125 + bias[first.astype(jnp.int32)]
                m = jnp.max(s, axis=-1, keepdims=True)
                p = jnp.exp(s - m)
                l = jnp.sum(p, axis=-1, keepdims=True)
                o2 = _dot(p.astype(BF16), vd[pl.ds(off, 2 * Q_BLOCK), :]) * (1.0 / l)
                lse2 = m + jnp.log(l)
                rows = pl.ds(c + n * (Q_BLOCK * r), Q_BLOCK, stride=r)
                opos[g, rows, :] = _unstack_heads(o2, low)
                lpos[g, rows, :] = jnp.where(low, lse2[:Q_BLOCK], lse2[Q_BLOCK:])
                return carry

            lax.fori_loop(0, nblk, unit, 0, unroll=ATTN_UNROLL)

        def merge(i, carry):
            rows = pl.ds(pl.multiple_of(i * 256, 256), 256)
            l0, l1, l2 = lpos[0, rows, :], lpos[1, rows, :], lpos[2, rows, :]
            m = jnp.maximum(jnp.maximum(l0, l1), l2)
            e0, e1, e2 = jnp.exp(l0 - m), jnp.exp(l1 - m), jnp.exp(l2 - m)
            den = e0 + e1 + e2
            att = (e0 * opos[0, rows, :] + e1 * opos[1, rows, :] + e2 * opos[2, rows, :]) / den
            att_ref[0, rows, :] = att.astype(att_ref.dtype)
            lse_ref[0, rows, :] = m + jnp.log(den)
            return carry

        lax.fori_loop(0, SEQ // 256, merge, 0)

    def col(sec, g):
        return pl.BlockSpec((1, SEQ, LANES), lambda b, hp: (b, 0, sec * 12 + g * 4 + hp))

    out = pl.BlockSpec((1, SEQ, LANES), lambda b, hp: (b, 0, hp))
    return pl.pallas_call(
        body, grid=(batch, 4),
        in_specs=[pl.BlockSpec(memory_space=pltpu.SMEM)] + [col(sec, g) for g in range(3) for sec in range(3)],
        out_specs=[out, out],
        out_shape=[SDS((batch, SEQ, ATTN_OUT), BF16), SDS((batch, SEQ, ATTN_OUT), F32)],
        scratch_shapes=[pltpu.VMEM((SEQ, LANES), BF16), pltpu.VMEM((Q_BLOCK + SEQ, LANES), BF16),
                        pltpu.VMEM((Q_BLOCK + SEQ, LANES), BF16), pltpu.VMEM((nblk, LANES, 2 * Q_BLOCK), BF16),
                        pltpu.VMEM((3, SEQ, LANES), F32), pltpu.VMEM((3, SEQ, LANES), F32),
                        pltpu.VMEM((2, 2 * Q_BLOCK, 2 * Q_BLOCK), F32)],
        compiler_params=_params("parallel", "parallel"), name="attn_fwd")(slopes_r, *([qkv] * 9))


def _attn_bwd(qkv, datt, lse, dsum, slopes_r, batch):
    nblk = SEQ // Q_BLOCK

    def body(sl_ref, q_ref, k_ref, v_ref, do_ref, l_ref, d_ref, dq_ref, dk_ref, dv_ref,
             qd, kd, vd, dod, kt, vt, ld, dd, dq_acc, dk_acc, dv_acc, dk_part, dv_part, stage, bias):
        gid, hp = pl.program_id(1), pl.program_id(2)
        low = lax.broadcasted_iota(jnp.int32, (Q_BLOCK, LANES), 1) < HEAD_DIM

        def section(g):
            r = GROUPS[g][1]
            nb = SEQ // r // Q_BLOCK
            _gather_classes(q_ref, qd, r)
            kd[0:Q_BLOCK, :] = jnp.zeros((Q_BLOCK, LANES), BF16)
            vd[0:Q_BLOCK, :] = jnp.zeros((Q_BLOCK, LANES), BF16)
            _gather_classes(k_ref, kd, r, Q_BLOCK)
            _gather_classes(v_ref, vd, r, Q_BLOCK)
            _gather_classes(do_ref, dod, r)
            _gather_classes(l_ref, ld, r)
            _gather_classes(d_ref, dd, r)
            _transpose_pairs(kd, kt)
            _transpose_pairs(vd, vt)
            _store_biases(bias, sl_ref, g, hp)

            def unit(u, carry):
                off, first, _, _ = _unit_offsets(u, nb)
                pair = pl.ds(off, 2 * Q_BLOCK)
                q2 = _stack_heads(qd[pl.ds(off, Q_BLOCK), :], low)
                do2 = _stack_heads(dod[pl.ds(off, Q_BLOCK), :], low)
                lse_t = ld[pl.ds(off, Q_BLOCK), :]
                dsum_t = dd[pl.ds(off, Q_BLOCK), :]
                lse2 = jnp.concatenate([lse_t[:, 0:1], lse_t[:, HEAD_DIM:HEAD_DIM + 1]], axis=0)
                dsum2 = jnp.concatenate([dsum_t[:, 0:1], dsum_t[:, HEAD_DIM:HEAD_DIM + 1]], axis=0)
                s = _dot(q2, kt[u]) * 0.125 + bias[first.astype(jnp.int32)]
                p = jnp.exp(s - lse2)
                ds = (p * (_dot(do2, vt[u]) - dsum2)).astype(BF16)
                dq_acc[pl.ds(off, Q_BLOCK), :] = _unstack_heads(_dot(ds, kd[pair, :]), low) * 0.125
                dk_part[u] = _dot_tn(ds, q2) * 0.125
                dv_part[u] = _dot_tn(p.astype(BF16), do2)
                return carry

            lax.fori_loop(0, nblk, unit, 0, unroll=ATTN_UNROLL)
            for part, acc in ((dk_part, dk_acc), (dv_part, dv_acc)):
                for b in range(nblk):
                    t = part[b, Q_BLOCK:, :]
                    if b + 1 < nblk:
                        t = t + part[b + 1, 0:Q_BLOCK, :]
                    acc[b * Q_BLOCK:(b + 1) * Q_BLOCK, :] = t
            for acc, out_ref in ((dq_acc, dq_ref), (dk_acc, dk_ref), (dv_acc, dv_ref)):
                _scatter_classes(acc, stage, r)
                out_ref[0] = stage[...].astype(out_ref.dtype)

        for g in range(3):
            pl.when(gid == g)(lambda g=g: section(g))

    def col(sec):
        return pl.BlockSpec((1, SEQ, LANES), lambda b, g, hp: (b, 0, sec * 12 + g * 4 + hp))

    pos = pl.BlockSpec((1, SEQ, LANES), lambda b, g, hp: (b, 0, hp))
    dout = pl.BlockSpec((1, SEQ, LANES), lambda b, g, hp: (b, 0, g * 4 + hp))
    out = SDS((batch, SEQ, ATTN_WIDTH), BF16)
    seq_bf = pltpu.VMEM((SEQ, LANES), BF16)
    seq_f = pltpu.VMEM((SEQ, LANES), F32)
    pad_bf = pltpu.VMEM((Q_BLOCK + SEQ, LANES), BF16)
    part = pltpu.VMEM((nblk, 2 * Q_BLOCK, LANES), F32)
    blk_t = pltpu.VMEM((nblk, LANES, 2 * Q_BLOCK), BF16)
    return pl.pallas_call(
        body, grid=(batch, 3, 4),
        in_specs=[pl.BlockSpec(memory_space=pltpu.SMEM), col(0), col(1), col(2), pos, pos, pos],
        out_specs=[dout, dout, dout],
        out_shape=[out, out, out],
        scratch_shapes=[seq_bf, pad_bf, pad_bf, seq_bf, blk_t, blk_t, seq_f, seq_f, seq_f, seq_f, seq_f, part, part, seq_f,
                        pltpu.VMEM((2, 2 * Q_BLOCK, 2 * Q_BLOCK), F32)],
        compiler_params=_params("parallel", "parallel", "parallel"), name="attn_bwd")(
            slopes_r, qkv, qkv, qkv, datt, lse, dsum)


CONV_TC = 128
CONV_ROWS = 128
SUBLANES = 8


def _fill_shifted(sh):
    n = SEQ + CONV_PAD - SUBLANES
    for s in range(1, SUBLANES):
        sh[s, 0:n, :] = sh[0, s:s + n, :]


def _tap(sh, base, offset):
    s = offset % SUBLANES
    return sh[s, pl.ds(pl.multiple_of(base + (offset - s), SUBLANES), CONV_ROWS), :]


def _conv_fwd(u, conv_w, conv_b, batch):
    nct = D_MODEL // CONV_TC

    def body(ua_ref, ub_ref, w_ref, b_ref, o_ref, sh):
        sh[0, 0:CONV_PAD, :] = jnp.zeros((CONV_PAD, CONV_TC), F32)
        sh[0, CONV_PAD:, :] = ua_ref[0] * _sigmoid(ub_ref[0])
        _fill_shifted(sh)

        def chunk(c, carry):
            base = pl.multiple_of(c * CONV_ROWS, CONV_ROWS)
            acc = jnp.broadcast_to(b_ref[...], (CONV_ROWS, CONV_TC))
            for t in range(CONV_K):
                acc = acc + _tap(sh, base, t + CONV_PAD - (CONV_K - 1)) * w_ref[t:t + 1, :]
            o_ref[0, pl.ds(base, CONV_ROWS), :] = acc
            return carry

        lax.fori_loop(0, SEQ // CONV_ROWS, chunk, 0)

    return pl.pallas_call(
        body, grid=(nct, batch),
        in_specs=[pl.BlockSpec((1, SEQ, CONV_TC), lambda j, b: (b, 0, j)),
                  pl.BlockSpec((1, SEQ, CONV_TC), lambda j, b: (b, 0, j + nct)),
                  pl.BlockSpec((CONV_PAD, CONV_TC), lambda j, b: (0, j)),
                  pl.BlockSpec((1, CONV_TC), lambda j, b: (0, j))],
        out_specs=pl.BlockSpec((1, SEQ, CONV_TC), lambda j, b: (b, 0, j)),
        out_shape=SDS((batch, SEQ, D_MODEL), F32),
        scratch_shapes=[pltpu.VMEM((SUBLANES, SEQ + CONV_PAD, CONV_TC), F32)],
        compiler_params=_params("parallel", "parallel"), name="conv_fwd")(u, u, conv_w, conv_b)


def _conv_bwd(u, dc1, conv_w, batch, dep=None):
    nct = D_MODEL // CONV_TC
    nchunk = SEQ // CONV_ROWS

    def body(ua_ref, ub_ref, d_ref, w_ref, dua_ref, dub_ref, gw_ref, gb_ref, shc, shd, gacc):
        b = pl.program_id(1)
        shc[0, 0:CONV_PAD, :] = jnp.zeros((CONV_PAD, CONV_TC), F32)
        shc[0, CONV_PAD:, :] = ua_ref[0] * _sigmoid(ub_ref[0])
        _fill_shifted(shc)
        shd[0, 0:SEQ, :] = d_ref[0]
        shd[0, SEQ:, :] = jnp.zeros((CONV_PAD, CONV_TC), F32)
        _fill_shifted(shd)

        @pl.when(b == 0)
        def _():
            gacc[...] = jnp.zeros_like(gacc)
            gb_ref[...] = jnp.zeros_like(gb_ref)

        gb_ref[...] += _rowsum(d_ref[0])

        def chunk(c, carry):
            base = pl.multiple_of(c * CONV_ROWS, CONV_ROWS)
            dcur = shd[0, pl.ds(base, CONV_ROWS), :]
            acc = jnp.zeros((CONV_ROWS, CONV_TC), F32)
            for t in range(CONV_K):
                acc = acc + _tap(shd, base, CONV_K - 1 - t) * w_ref[t:t + 1, :]
                prod = _tap(shc, base, t + CONV_PAD - (CONV_K - 1)) * dcur
                gacc[t] += jnp.sum(prod.reshape(CONV_ROWS // 8, 8, CONV_TC), axis=0)
            ua = ua_ref[0, pl.ds(base, CONV_ROWS), :]
            sg = _sigmoid(ub_ref[0, pl.ds(base, CONV_ROWS), :])
            dua_ref[0, pl.ds(base, CONV_ROWS), :] = (acc * sg).astype(dua_ref.dtype)
            dub_ref[0, pl.ds(base, CONV_ROWS), :] = (acc * ua * sg * (1.0 - sg)).astype(dub_ref.dtype)
            return carry

        lax.fori_loop(0, nchunk, chunk, 0)

        @pl.when(b == batch - 1)
        def _():
            for t in range(CONV_K):
                gw_ref[t:t + 1, :] = jnp.sum(gacc[t], axis=0, keepdims=True)
            gw_ref[CONV_K:CONV_PAD, :] = jnp.zeros((CONV_PAD - CONV_K, CONV_TC), F32)

    du = SDS((batch, SEQ, D_MODEL), BF16)
    body, dep_spec, dep_arg = _anchored(body, 4, dep)
    return pl.pallas_call(
        body, grid=(nct, batch),
        in_specs=[pl.BlockSpec((1, SEQ, CONV_TC), lambda j, b: (b, 0, j)),
                  pl.BlockSpec((1, SEQ, CONV_TC), lambda j, b: (b, 0, j + nct)),
                  pl.BlockSpec((1, SEQ, CONV_TC), lambda j, b: (b, 0, j)),
                  pl.BlockSpec((CONV_PAD, CONV_TC), lambda j, b: (0, j))] + dep_spec,
        out_specs=[pl.BlockSpec((1, SEQ, CONV_TC), lambda j, b: (b, 0, j)),
                   pl.BlockSpec((1, SEQ, CONV_TC), lambda j, b: (b, 0, j)),
                   pl.BlockSpec((CONV_PAD, CONV_TC), lambda j, b: (0, j)),
                   pl.BlockSpec((1, CONV_TC), lambda j, b: (0, j))],
        out_shape=[du, du, SDS((CONV_PAD, D_MODEL), F32), SDS((1, D_MODEL), F32)],
        scratch_shapes=[pltpu.VMEM((SUBLANES, SEQ + CONV_PAD, CONV_TC), F32),
                        pltpu.VMEM((SUBLANES, SEQ + CONV_PAD, CONV_TC), F32),
                        pltpu.VMEM((CONV_K, 8, CONV_TC), F32)],
        compiler_params=_params("parallel", "arbitrary"), name="conv_bwd")(u, u, dc1, conv_w, *dep_arg)


MID_TM = 256


def _layernorm_stats(c1):
    mu = jnp.mean(c1, axis=-1, keepdims=True)
    cen = c1 - mu
    rs = lax.rsqrt(jnp.mean(cen * cen, axis=-1, keepdims=True) + LN_EPS)
    return cen * rs, rs


def _mid_fwd(att, c1, logits, x, w_a, w_c, w_o, gate_b, ln_g, ln_b, g2, dep=None):
    T = x.shape[0]
    tm = MID_TM

    def body(att_ref, c1_ref, lg_ref, x_ref, wa_ref, wc_ref, wo_ref, gb_ref, lng_ref, lnb_ref, g2_ref,
             c3_ref, ya_ref, yc_ref, mix_ref, x1_ref, h2_ref):
        ya = _dot(att_ref[...], wa_ref[...])
        xh, _ = _layernorm_stats(c1_ref[...])
        c2 = xh * lng_ref[...] + lnb_ref[...]
        c3 = (c2 * _sigmoid(c2)).astype(BF16)
        c3_ref[...] = c3
        yc = _dot(c3, wc_ref[...])
        gates = _sigmoid(lg_ref[...] + gb_ref[...])
        mix = (gates[:, :D_MODEL] * ya + gates[:, D_MODEL:] * yc).astype(BF16)
        ya_ref[...] = ya.astype(BF16)
        yc_ref[...] = yc.astype(BF16)
        mix_ref[...] = mix
        x1 = x_ref[...] + _dot(mix, wo_ref[...])
        x1_ref[...] = x1
        r = lax.rsqrt(jnp.mean(x1 * x1, axis=-1, keepdims=True) + RMS_EPS)
        h2_ref[...] = (x1 * r * g2_ref[...]).astype(BF16)

    row = lambda n: pl.BlockSpec((tm, n), lambda i: (i, 0))
    full = lambda a, b: pl.BlockSpec((a, b), lambda i: (0, 0))
    body, dep_spec, dep_arg = _anchored(body, 11, dep)
    return pl.pallas_call(
        body, grid=(T // tm,),
        in_specs=[row(ATTN_OUT), row(D_MODEL), row(2 * D_MODEL), row(D_MODEL),
                  full(ATTN_OUT, D_MODEL), full(D_MODEL, D_MODEL), full(D_MODEL, D_MODEL),
                  full(1, 2 * D_MODEL), full(1, D_MODEL), full(1, D_MODEL), full(1, D_MODEL)] + dep_spec,
        out_specs=[row(D_MODEL), row(D_MODEL), row(D_MODEL), row(D_MODEL), row(D_MODEL), row(D_MODEL)],
        out_shape=[SDS((T, D_MODEL), BF16), SDS((T, D_MODEL), BF16), SDS((T, D_MODEL), BF16), SDS((T, D_MODEL), BF16),
                   SDS((T, D_MODEL), F32), SDS((T, D_MODEL), BF16)],
        compiler_params=_params("parallel"), name="mid_fwd")(att, c1, logits, x, w_a, w_c, w_o, gate_b, ln_g, ln_b, g2,
                                                             *dep_arg)


def _mid_bwd(dx1b, ya, yc, logits, att, c1, w_a, w_c, w_o, gate_b, ln_g, ln_b, head_ones, dep=None):
    T = dx1b.shape[0]
    tm = MID_TM

    def body(dx_ref, ya_ref, yc_ref, lg_ref, att_ref, c1_ref, wa_ref, wc_ref, wo_ref, gb_ref, lng_ref, lnb_ref, e_ref,
             dlg_ref, dya_ref, dyc_ref, datt_ref, dsum_ref, dc1_ref, ggb_ref, glg_ref, glb_ref):
        @pl.when(pl.program_id(0) == 0)
        def _():
            ggb_ref[...] = jnp.zeros_like(ggb_ref)
            glg_ref[...] = jnp.zeros_like(glg_ref)
            glb_ref[...] = jnp.zeros_like(glb_ref)

        dmix = _dot_nt(dx_ref[...], wo_ref[...])
        gates = _sigmoid(lg_ref[...] + gb_ref[...])
        ga, gc = gates[:, :D_MODEL], gates[:, D_MODEL:]
        dla = dmix * ya_ref[...].astype(F32) * ga * (1.0 - ga)
        dlc = dmix * yc_ref[...].astype(F32) * gc * (1.0 - gc)
        dlg_ref[:, :D_MODEL] = dla.astype(BF16)
        dlg_ref[:, D_MODEL:] = dlc.astype(BF16)
        ggb_ref[:, :D_MODEL] += _rowsum(dla)
        ggb_ref[:, D_MODEL:] += _rowsum(dlc)
        dya = (dmix * ga).astype(BF16)
        dyc = (dmix * gc).astype(BF16)
        dya_ref[...] = dya
        dyc_ref[...] = dyc
        datt = _dot_nt(dya, wa_ref[...])
        datt_ref[...] = datt
        dsum_ref[...] = jnp.dot(datt * att_ref[...].astype(F32), e_ref[...], preferred_element_type=F32,
                                precision=lax.Precision.HIGHEST)
        dc3 = _dot_nt(dyc, wc_ref[...])
        xh, rs = _layernorm_stats(c1_ref[...])
        c2 = xh * lng_ref[...] + lnb_ref[...]
        sg = _sigmoid(c2)
        dc2 = dc3 * (sg * (1.0 + c2 * (1.0 - sg)))
        glg_ref[...] += _rowsum(dc2 * xh)
        glb_ref[...] += _rowsum(dc2)
        dxh = dc2 * lng_ref[...]
        dc1_ref[...] = rs * (dxh - jnp.mean(dxh, axis=-1, keepdims=True) - xh * jnp.mean(dxh * xh, axis=-1, keepdims=True))

    row = lambda n: pl.BlockSpec((tm, n), lambda i: (i, 0))
    full = lambda a, b: pl.BlockSpec((a, b), lambda i: (0, 0))
    body, dep_spec, dep_arg = _anchored(body, 13, dep)
    return pl.pallas_call(
        body, grid=(T // tm,),
        in_specs=[row(D_MODEL), row(D_MODEL), row(D_MODEL), row(2 * D_MODEL), row(ATTN_OUT), row(D_MODEL),
                  full(ATTN_OUT, D_MODEL), full(D_MODEL, D_MODEL), full(D_MODEL, D_MODEL),
                  full(1, 2 * D_MODEL), full(1, D_MODEL), full(1, D_MODEL), full(ATTN_OUT, ATTN_OUT)] + dep_spec,
        out_specs=[row(2 * D_MODEL), row(D_MODEL), row(D_MODEL), row(ATTN_OUT), row(ATTN_OUT), row(D_MODEL),
                   full(1, 2 * D_MODEL), full(1, D_MODEL), full(1, D_MODEL)],
        out_shape=[SDS((T, 2 * D_MODEL), BF16), SDS((T, D_MODEL), BF16), SDS((T, D_MODEL), BF16), SDS((T, ATTN_OUT), F32),
                   SDS((T, ATTN_OUT), F32), SDS((T, D_MODEL), F32),
                   SDS((1, 2 * D_MODEL), F32), SDS((1, D_MODEL), F32), SDS((1, D_MODEL), F32)],
        compiler_params=_params("arbitrary"), name="mid_bwd")(dx1b, ya, yc, logits, att, c1, w_a, w_c, w_o, gate_b, ln_g, ln_b,
                                                               head_ones, *dep_arg)


FFN_TM = 512
FFN_TF = D_FF // 2
FFN_SUB = ((0, 512), (512, 1024), (1024, FFN_TF))


def _rms_bwd(dy_times_g, xh, r):
    return r * (dy_times_g - xh * jnp.mean(dy_times_g * xh, axis=-1, keepdims=True))


def _ffn_fwd(h2, x1, target, gf, w_g_t, w_u_t, w_d):
    T = h2.shape[0]
    tm, tf = FFN_TM, FFN_TF
    nf = D_FF // tf

    def body(h_ref, x1_ref, t_ref, gf_ref, wg_ref, wu_ref, wd_ref,
             a_ref, b_ref, f_ref, dx2_ref, dx2b_ref, loss_ref, gnf_ref, acc):
        i, j = pl.program_id(0), pl.program_id(1)
        h = h_ref[...]

        @pl.when(j == 0)
        def _():
            acc[...] = x1_ref[...]

        for lo, hi in FFN_SUB:
            a = _dot_nt(h, wg_ref[lo:hi, :])
            b = _dot_nt(h, wu_ref[lo:hi, :])
            f = (a * _sigmoid(a) * b).astype(BF16)
            a_ref[:, lo:hi] = a.astype(BF16)
            b_ref[:, lo:hi] = b.astype(BF16)
            f_ref[:, lo:hi] = f
            acc[...] += _dot(f, wd_ref[lo:hi, :])

        @pl.when((i == 0) & (j == nf - 1))
        def _():
            loss_ref[...] = jnp.zeros_like(loss_ref)
            gnf_ref[...] = jnp.zeros_like(gnf_ref)

        @pl.when(j == nf - 1)
        def _():
            x2 = acc[...]
            r = lax.rsqrt(jnp.mean(x2 * x2, axis=-1, keepdims=True) + RMS_EPS)
            xh = x2 * r
            err = xh * gf_ref[...] - t_ref[...]
            loss_ref[...] += (0.5 / D_MODEL) * jnp.sum(err * err)
            dy = err * (1.0 / D_MODEL)
            gnf_ref[...] += _rowsum(dy * xh)
            dx2 = _rms_bwd(dy * gf_ref[...], xh, r)
            dx2_ref[...] = dx2
            dx2b_ref[...] = dx2.astype(BF16)

    row = lambda n: pl.BlockSpec((tm, n), lambda i, j: (i, 0))
    ffb = pl.BlockSpec((tm, tf), lambda i, j: (i, j))
    wblk = pl.BlockSpec((tf, D_MODEL), lambda i, j: (j, 0))
    return pl.pallas_call(
        body, grid=(T // tm, nf),
        in_specs=[row(D_MODEL), row(D_MODEL), row(D_MODEL), pl.BlockSpec((1, D_MODEL), lambda i, j: (0, 0)),
                  wblk, wblk, wblk],
        out_specs=[ffb, ffb, ffb, row(D_MODEL), row(D_MODEL),
                   pl.BlockSpec((1, 128), lambda i, j: (0, 0)), pl.BlockSpec((1, D_MODEL), lambda i, j: (0, 0))],
        out_shape=[SDS((T, D_FF), BF16), SDS((T, D_FF), BF16), SDS((T, D_FF), BF16), SDS((T, D_MODEL), F32),
                   SDS((T, D_MODEL), BF16), SDS((1, 128), F32), SDS((1, D_MODEL), F32)],
        scratch_shapes=[pltpu.VMEM((tm, D_MODEL), F32)],
        compiler_params=_params("arbitrary", "arbitrary"), name="ffn_fwd")(h2, x1, target, gf, w_g_t, w_u_t, w_d)


def _ffn_bwd(dx2b, dx2, a, b, x1, g2, w_g_t, w_u_t, w_d):
    T = dx2.shape[0]
    tm, tf = FFN_TM, FFN_TF
    nf = D_FF // tf

    def body(dxb_ref, dx2_ref, a_ref, b_ref, x1_ref, g2_ref, wg_ref, wu_ref, wd_ref,
             da_ref, db_ref, dx1_ref, dx1b_ref, gn2_ref, acc):
        i, j = pl.program_id(0), pl.program_id(1)
        @pl.when(j == 0)
        def _():
            acc[...] = jnp.zeros_like(acc)

        dxb = dxb_ref[...]
        for lo, hi in FFN_SUB:
            df = _dot_nt(dxb, wd_ref[lo:hi, :])
            av = a_ref[:, lo:hi].astype(F32)
            bv = b_ref[:, lo:hi].astype(F32)
            sg = _sigmoid(av)
            db = (df * av * sg).astype(BF16)
            da = (df * bv * (sg * (1.0 + av * (1.0 - sg)))).astype(BF16)
            da_ref[:, lo:hi] = da
            db_ref[:, lo:hi] = db
            acc[...] += _dot(da, wg_ref[lo:hi, :]) + _dot(db, wu_ref[lo:hi, :])

        @pl.when((i == 0) & (j == nf - 1))
        def _():
            gn2_ref[...] = jnp.zeros_like(gn2_ref)

        @pl.when(j == nf - 1)
        def _():
            dh2 = acc[...]
            x1 = x1_ref[...]
            r = lax.rsqrt(jnp.mean(x1 * x1, axis=-1, keepdims=True) + RMS_EPS)
            xh = x1 * r
            gn2_ref[...] += _rowsum(dh2 * xh)
            dx1 = dx2_ref[...] + _rms_bwd(dh2 * g2_ref[...], xh, r)
            dx1_ref[...] = dx1
            dx1b_ref[...] = dx1.astype(BF16)

    row = lambda n: pl.BlockSpec((tm, n), lambda i, j: (i, 0))
    ffb = pl.BlockSpec((tm, tf), lambda i, j: (i, j))
    wblk = pl.BlockSpec((tf, D_MODEL), lambda i, j: (j, 0))
    return pl.pallas_call(
        body, grid=(T // tm, nf),
        in_specs=[row(D_MODEL), row(D_MODEL), ffb, ffb, row(D_MODEL), pl.BlockSpec((1, D_MODEL), lambda i, j: (0, 0)),
                  wblk, wblk, wblk],
        out_specs=[ffb, ffb, row(D_MODEL), row(D_MODEL), pl.BlockSpec((1, D_MODEL), lambda i, j: (0, 0))],
        out_shape=[SDS((T, D_FF), BF16), SDS((T, D_FF), BF16), SDS((T, D_MODEL), F32), SDS((T, D_MODEL), BF16),
                   SDS((1, D_MODEL), F32)],
        scratch_shapes=[pltpu.VMEM((tm, D_MODEL), F32)],
        compiler_params=_params("arbitrary", "arbitrary"), name="ffn_bwd")(dx2b, dx2, a, b, x1, g2, w_g_t, w_u_t, w_d)


def _in_bwd(pieces, w_in_t, x, dx1, g1, dep=None):
    T = x.shape[0]
    tm = IN_TM
    npc = len(pieces)
    assert sum(p.shape[1] for p in pieces) == IN_WIDTH

    def body(*refs):
        p_refs = refs[:npc]
        w_hbm, x_ref, dx1_ref, g_ref, dx_ref, gn1_ref, w_vmem, sem = refs[npc:]

        @pl.when(pl.program_id(0) == 0)
        def _():
            cp = pltpu.make_async_copy(w_hbm, w_vmem, sem)
            cp.start()
            cp.wait()
            gn1_ref[...] = jnp.zeros_like(gn1_ref)

        dh = jnp.zeros((tm, D_MODEL), F32)
        col = 0
        for p_ref in p_refs:
            for j in range(p_ref.shape[1] // IN_CHUNK):
                dh = dh + _dot(p_ref[:, j * IN_CHUNK:(j + 1) * IN_CHUNK], w_vmem[col:col + IN_CHUNK, :])
                col += IN_CHUNK
        xv = x_ref[...]
        r = lax.rsqrt(jnp.mean(xv * xv, axis=-1, keepdims=True) + RMS_EPS)
        xh = xv * r
        gn1_ref[...] += _rowsum(dh * xh)
        dx_ref[...] = dx1_ref[...] + _rms_bwd(dh * g_ref[...], xh, r)

    row = lambda n: pl.BlockSpec((tm, n), lambda i: (i, 0))
    body, dep_spec, dep_arg = _anchored(body, npc + 4, dep)
    return pl.pallas_call(
        body, grid=(T // tm,),
        in_specs=[row(p.shape[1]) for p in pieces]
        + [pl.BlockSpec(memory_space=pl.ANY), row(D_MODEL), row(D_MODEL), pl.BlockSpec((1, D_MODEL), lambda i: (0, 0))]
        + dep_spec,
        out_specs=[row(D_MODEL), pl.BlockSpec((1, D_MODEL), lambda i: (0, 0))],
        out_shape=[SDS((T, D_MODEL), F32), SDS((1, D_MODEL), F32)],
        scratch_shapes=[pltpu.VMEM((IN_WIDTH, D_MODEL), BF16), pltpu.SemaphoreType.DMA],
        compiler_params=_params("arbitrary"), name="in_bwd")(*pieces, w_in_t, x, dx1, g1, *dep_arg)


def _local_step(x, target, w, small, dep=None, late_weights=None, emit=None):
    T = x.shape[0]
    batch = T // SEQ
    slopes_r = jnp.asarray(_slopes_times_dilation())
    emit = emit or (lambda names, grads: None)

    h, qkv, u, logits = _in_proj(x, small["norm1_g"], w["w_in"], dep)

    qkv3 = qkv.reshape(batch, SEQ, 3 * ATTN_WIDTH)
    att, lse = _attn_fwd(qkv3, slopes_r, batch)
    att = att.reshape(T, ATTN_OUT)

    u3 = u.reshape(batch, SEQ, 2 * D_MODEL)
    c1 = _conv_fwd(u3, w["conv_w"], small["conv_b"], batch).reshape(T, D_MODEL)
    if late_weights is not None:
        w = {**w, **late_weights(LATE_MERGE, (att, c1))}

    c3, ya, yc, mix, x1, h2 = _mid_fwd(
        att, c1, logits, x, w["w_attn_out"], w["w_conv_out"], w["w_o"],
        small["gate_b"], small["conv_ln_g"], small["conv_ln_b"], small["norm2_g"], w.get("token"))
    if late_weights is not None:
        w = {**w, **late_weights(LATE_FFN, h2)}

    a, b, f, dx2, dx2b, loss, g_normf = _ffn_fwd(h2, x1, target, small["norm_f_g"],
                                                   w["w_ffn_gate"], w["w_ffn_up"], w["w_ffn_down"])

    da, db, dx1, dx1b, g_norm2 = _ffn_bwd(dx2b, dx2, a, b, x1, small["norm2_g"],
                                           w["w_ffn_gate"], w["w_ffn_up"], w["w_ffn_down"])
    gw = {}
    gw["w_ffn_down"] = _mm_tn(f, dx2b, BF16, "gw_ffn_down", tn=1024)
    gw["w_ffn_gate"] = _mm_tn(da, h2, BF16, "gw_ffn_gate", tn=1024)
    gw["w_ffn_up"] = _mm_tn(db, h2, BF16, "gw_ffn_up", tn=1024)
    token = emit(("w_ffn_gate", "w_ffn_up", "w_ffn_down"), gw)

    head_ones = jnp.asarray(np.kron(np.eye(HEADS_PER_GROUP, dtype=np.float32), np.ones((HEAD_DIM, HEAD_DIM), np.float32)))
    dlogits, dya, dyc, datt, dsum, dc1, g_gate_b, g_ln_g, g_ln_b = _mid_bwd(
        dx1b, ya, yc, logits, att, c1, w["w_attn_out"], w["w_conv_out"], w["w_o"],
        small["gate_b"], small["conv_ln_g"], small["conv_ln_b"], head_ones, token)
    gw["w_o"] = _mm_tn(mix, dx1b, BF16, "gw_o", tn=1024)
    gw["w_attn_out"] = _mm_tn(att, dya, BF16, "gw_attn_out", tn=1024)
    gw["w_conv_out"] = _mm_tn(c3, dyc, BF16, "gw_conv_out", tn=1024)
    token = emit(("w_conv_out", "w_attn_out", "w_o"), gw)

    dua, dub, g_conv_w, g_conv_b = _conv_bwd(u3, dc1.reshape(batch, SEQ, D_MODEL), w["conv_w"], batch, token)

    dq, dk, dv = _attn_bwd(qkv3, datt.reshape(batch, SEQ, ATTN_OUT), lse, dsum.reshape(batch, SEQ, ATTN_OUT),
                           slopes_r, batch)
    pieces = [dq.reshape(T, ATTN_WIDTH), dk.reshape(T, ATTN_WIDTH), dv.reshape(T, ATTN_WIDTH),
              dua.reshape(T, D_MODEL), dub.reshape(T, D_MODEL), dlogits]

    names = ("q", "k", "v", "ua", "ub", "gate")
    gw["w_in"] = jnp.concatenate([_mm_tn(p, h, BF16, "gw_in_" + nm, tn=1024) for nm, p in zip(names, pieces)], axis=0)
    gw["conv_w"] = g_conv_w
    token = emit(("w_in", "conv_w"), gw)
    grad_x, g_norm1 = _in_bwd(pieces, w["w_in"], x, dx1, small["norm1_g"], token)

    gsmall = {"norm1_g": g_norm1, "gate_b": g_gate_b, "conv_b": g_conv_b, "conv_ln_g": g_ln_g, "conv_ln_b": g_ln_b,
              "norm2_g": g_norm2, "norm_f_g": g_normf}
    return loss, grad_x, gw, gsmall


ANY = pl.BlockSpec(memory_space=pl.ANY)


def _all_gather(arrs):
    n = len(arrs)

    def body(*refs):
        ins, outs = refs[:n], refs[n:2 * n]
        send_sems, recv_sems, local_sems = refs[2 * n:]
        x, y, c = lax.axis_index("x"), lax.axis_index("y"), lax.axis_index("c")
        me, sibling = (x, y, c), (x, y, 1 - c)
        chips = [(1 - x, y), (x, 1 - y), (1 - x, 1 - y)]

        def copy(a, k, block, to, src=None):
            px, py, pc = block
            dst = outs[a].at[4 * px + 2 * py + pc]
            return pltpu.make_async_remote_copy(
                src_ref=dst if src is None else src, dst_ref=dst,
                send_sem=send_sems.at[a, k], recv_sem=recv_sems.at[a, k], device_id=to, device_id_type=MESH)

        mine = [pltpu.make_async_copy(ins[a], outs[a].at[4 * x + 2 * y + c], local_sems.at[a]) for a in range(n)]
        for cp in mine:
            cp.start()
        first = []
        for j, chip in enumerate(chips):
            first += [copy(a, 1 + j, me, (*chip, c), src=ins[a]) for a in range(n)]
        first += [copy(a, 0, me, sibling, src=ins[a]) for a in range(n)]
        for cp in first:
            cp.start()
        passed = []
        for j, chip in enumerate(chips):
            for a in range(n):
                copy(a, 1 + j, (*chip, c), me).wait_recv()
                cp = copy(a, 4 + j, (*chip, c), sibling)
                cp.start()
                passed.append(cp)
        for a in range(n):
            copy(a, 0, sibling, me).wait_recv()
        for j, chip in enumerate(chips):
            for a in range(n):
                copy(a, 4 + j, (*chip, 1 - c), me).wait_recv()
        for cp in first + passed:
            cp.wait_send()
        for cp in mine:
            cp.wait()

    return pl.pallas_call(
        body, in_specs=[ANY] * n, out_specs=[ANY] * n,
        out_shape=[SDS((N_DEV,) + a.shape, a.dtype) for a in arrs],
        scratch_shapes=[pltpu.SemaphoreType.DMA((n, 7)), pltpu.SemaphoreType.DMA((n, 7)), pltpu.SemaphoreType.DMA((n,))],
        name="all_gather_weights")(*arrs)


HBM = pl.BlockSpec(memory_space=pltpu.HBM)
SEM = pl.BlockSpec(memory_space=pltpu.SEMAPHORE)
ALL_PEERS = tuple(range(1, N_DEV))
OTHER_CHIPS = (2, 4, 6)
SPLIT_EFFECT = pltpu.CompilerParams(has_side_effects=pltpu.SideEffectType.DATAFLOW_SIDE_EFFECTING)


def _exchange_copies(mode, ks, srcs, lands, send_sems, recv_sems):
    x, y, c = lax.axis_index("x"), lax.axis_index("y"), lax.axis_index("c")
    me = 4 * x + 2 * y + c
    send, recv = [], []
    for a in range(len(lands)):
        for i, k in enumerate(ks):
            peer = (x ^ ((k >> 2) & 1), y ^ ((k >> 1) & 1), c ^ (k & 1))
            pidx = 4 * peer[0] + 2 * peer[1] + peer[2]
            if mode == "gather":
                src, to, out_slot, in_slot = srcs[a], peer, me, pidx
            elif mode == "scatter":
                src, to, out_slot, in_slot = srcs[a].at[pidx], peer, me, pidx
            elif mode == "chip_scatter":
                src, to, out_slot, in_slot = srcs[a].at[pidx >> 1], peer, me >> 1, pidx >> 1
            else:
                src, to, out_slot, in_slot = lands[a].at[pidx], (x, y, 1 - c), pidx, pidx ^ 1
            s = a * len(ks) + i
            send.append(pltpu.make_async_remote_copy(
                src_ref=src, dst_ref=lands[a].at[out_slot], send_sem=send_sems.at[s], recv_sem=recv_sems.at[s],
                device_id=to, device_id_type=MESH))
            recv.append(pltpu.make_async_remote_copy(
                src_ref=src, dst_ref=lands[a].at[in_slot], send_sem=send_sems.at[s], recv_sem=recv_sems.at[s],
                device_id=to, device_id_type=MESH))
    return send, recv


def _send_start(mode, ks, name, srcs=(), lands=None, dep=None):
    srcs = list(srcs)
    if lands is None:
        slots = 4 if mode == "chip_scatter" else N_DEV
        lands = [lax.empty((slots,) + (s.shape if mode == "gather" else s.shape[1:]), s.dtype) for s in srcs]
    ns, nl = len(srcs), len(lands)
    nsem = nl * len(ks)

    def body(*refs):
        send, _ = _exchange_copies(mode, ks, refs[:ns], refs[ns:ns + nl], refs[ns + nl], refs[ns + nl + 1])
        for cp in send:
            cp.start()
        token = refs[-1]
        token[...] = jnp.zeros_like(token)

    both = srcs + list(lands)
    body, dep_spec, dep_arg = _anchored(body, ns + nl, dep)
    res = pl.pallas_call(
        body, name=name,
        out_shape=(pltpu.SemaphoreType.DMA((nsem,)), pltpu.SemaphoreType.DMA((nsem,)),
                   *[pltpu.HBM(a.shape, a.dtype) for a in both], SDS((8, 128), F32)),
        in_specs=[HBM] * (ns + nl) + dep_spec,
        out_specs=(SEM, SEM, *([HBM] * (ns + nl)), pl.BlockSpec(memory_space=pltpu.VMEM)),
        input_output_aliases={i: 2 + i for i in range(ns + nl)}, compiler_params=SPLIT_EFFECT,
    )(*[pltpu.with_memory_space_constraint(a, pltpu.HBM) for a in both], *dep_arg)
    return dict(mode=mode, ks=ks, send_sems=res[0], recv_sems=res[1], srcs=res[2:2 + ns], lands=res[2 + ns:2 + ns + nl],
                token=res[-1])


def _send_wait(started, after, name):
    ns, nl = len(started["srcs"]), len(started["lands"])

    def body(*refs):
        send, recv = _exchange_copies(started["mode"], started["ks"], refs[:ns], refs[ns:ns + nl],
                                      refs[ns + nl], refs[ns + nl + 1])
        for cp in send:
            cp.wait_send()
        for cp in recv:
            cp.wait_recv()

    both = list(started["srcs"]) + list(started["lands"])
    after = after if isinstance(after, (tuple, list)) else (after,)
    res = pl.pallas_call(
        body, name=name,
        out_shape=tuple(pltpu.HBM(a.shape, a.dtype) for a in both),
        in_specs=[HBM] * (ns + nl) + [SEM, SEM] + [ANY] * len(after), out_specs=tuple([HBM] * (ns + nl)),
        input_output_aliases={i: i for i in range(ns + nl)}, compiler_params=SPLIT_EFFECT,
    )(*both, started["send_sems"], started["recv_sems"], *after)
    return res[:ns], res[ns:]


def _exchange_sibling(gs):
    n = len(gs)

    def body(*refs):
        ins, outs = refs[:n], refs[n:2 * n]
        send_sems, recv_sems = refs[2 * n:]
        x, y, c = lax.axis_index("x"), lax.axis_index("y"), lax.axis_index("c")
        copies = []
        for a in range(n):
            for j in range(4):
                copies.append(pltpu.make_async_remote_copy(
                    src_ref=ins[a].at[2 * j + (1 - c)], dst_ref=outs[a].at[j],
                    send_sem=send_sems.at[a, j], recv_sem=recv_sems.at[a, j],
                    device_id=(x, y, 1 - c), device_id_type=MESH))
        for cp in copies:
            cp.start()
        for cp in copies:
            cp.wait_recv()
        for cp in copies:
            cp.wait_send()

    return pl.pallas_call(
        body, in_specs=[ANY] * n, out_specs=[ANY] * n,
        out_shape=[SDS((4,) + g.shape[1:], g.dtype) for g in gs],
        scratch_shapes=[pltpu.SemaphoreType.DMA((n, 4)), pltpu.SemaphoreType.DMA((n, 4))],
        name="reduce_scatter_sibling")(*gs)


def _add_pair(g, r1, core, name):
    _, rows, cols = g.shape
    tr = _row_tile(rows, cols, 3 * g.dtype.itemsize)

    def body(c_ref, g_ref, r_ref, o_ref):
        o_ref[...] = (g_ref[...].astype(F32) + r_ref[...].astype(F32)).astype(o_ref.dtype)

    return pl.pallas_call(
        body,
        grid_spec=pltpu.PrefetchScalarGridSpec(
            num_scalar_prefetch=1, grid=(4, rows // tr),
            in_specs=[pl.BlockSpec((1, tr, cols), lambda j, i, c_ref: (2 * j + c_ref[0], i, 0)),
                      pl.BlockSpec((1, tr, cols), lambda j, i, c_ref: (j, i, 0))],
            out_specs=pl.BlockSpec((1, tr, cols), lambda j, i, c_ref: (j, i, 0))),
        out_shape=SDS((4, rows, cols), g.dtype),
        compiler_params=_params("parallel", "parallel"), name=name)(core, g, r1)


def _row_tile(rows, cols, itemsize_total):
    budget = (4 << 20) // max(1, cols * itemsize_total)
    if rows <= budget:
        return rows
    t = rows
    while t > budget and t % 2 == 0 and (t // 2) % 16 == 0:
        t //= 2
    return t


def _adam_math(g, w, m, v):
    m_new = ADAM_B1 * m + (1.0 - ADAM_B1) * g
    v_new = ADAM_B2 * v + (1.0 - ADAM_B2) * (g * g)
    m_hat = m_new / (1.0 - ADAM_B1 ** ADAM_STEP)
    v_hat = v_new / (1.0 - ADAM_B2 ** ADAM_STEP)
    delta = -ADAM_LR * (m_hat / (jnp.sqrt(v_hat) + ADAM_EPS) + ADAM_WD * w)
    return delta, m_new, v_new


def _sum_adam(parts, own, mine, w, m, v, name):
    rows, cols = w.shape
    nparts = parts.shape[0]
    tr = _row_tile(rows, cols, (nparts + 1) * parts.dtype.itemsize + 7 * 4)

    def body(mine_ref, p_ref, own_ref, w_ref, m_ref, v_ref, g_ref, d_ref, mo_ref, vo_ref):
        g = None
        for s in range(nparts):
            part = jnp.where(mine_ref[0] == s, own_ref[0], p_ref[s]).astype(F32)
            g = part if g is None else g + part
        delta, m_new, v_new = _adam_math(g, w_ref[...], m_ref[...], v_ref[...])
        g_ref[...] = g
        d_ref[...] = delta
        mo_ref[...] = m_new
        vo_ref[...] = v_new

    blk = pl.BlockSpec((tr, cols), lambda i, mine_ref: (i, 0))
    out = SDS((rows, cols), F32)
    return pl.pallas_call(
        body,
        grid_spec=pltpu.PrefetchScalarGridSpec(
            num_scalar_prefetch=1, grid=(rows // tr,),
            in_specs=[pl.BlockSpec((nparts, tr, cols), lambda i, mine_ref: (0, i, 0)),
                      pl.BlockSpec((1, tr, cols), lambda i, mine_ref: (mine_ref[0], i, 0)), blk, blk, blk],
            out_specs=[blk, blk, blk, blk]),
        out_shape=[out, out, out, out],
        compiler_params=_params("parallel"), name=name)(mine, parts, own, w, m, v)


SMALL_ROWS = 72


def _small_allreduce_adam(gpart, w, m, v, row_counts, dep=None):
    def reduce_body(g_ref, go_ref, gath, send_sems, recv_sems):
        x, y, c = lax.axis_index("x"), lax.axis_index("y"), lax.axis_index("c")
        me = 4 * x + 2 * y + c
        gath[me] = g_ref[...]
        copies = []
        for k in range(1, N_DEV):
            fx, fy, fc = (k >> 2) & 1, (k >> 1) & 1, k & 1
            peer = (x ^ fx, y ^ fy, c ^ fc)
            copies.append(pltpu.make_async_remote_copy(
                src_ref=gath.at[me], dst_ref=gath.at[me], send_sem=send_sems.at[k - 1], recv_sem=recv_sems.at[k - 1],
                device_id=peer, device_id_type=MESH))
        for cp in copies:
            cp.start()
        for cp in copies:
            cp.wait_recv()
        for cp in copies:
            cp.wait_send()
        g = gath[0]
        for d in range(1, N_DEV):
            g = g + gath[d]
        go_ref[...] = g

    def adam_body(g_ref, w_ref, m_ref, v_ref, *out_refs):
        g = g_ref[...]
        delta, m_new, v_new = _adam_math(g, w_ref[...], m_ref[...], v_ref[...])
        outs = iter(out_refs)
        for val in (g, delta, m_new, v_new):
            lo = 0
            for r in row_counts:
                next(outs)[...] = val[lo:lo + r]
                lo += r
        next(outs)[...] = g[SMALL_ROWS - SUBLANES:]

    vm = pl.BlockSpec(memory_space=pltpu.VMEM)
    reduce_body, dep_spec, dep_arg = _anchored(reduce_body, 1, dep)
    total = pl.pallas_call(
        reduce_body, in_specs=[vm] + dep_spec, out_specs=vm, out_shape=SDS((SMALL_ROWS, 128), F32),
        scratch_shapes=[pltpu.VMEM((N_DEV, SMALL_ROWS, 128), F32), pltpu.SemaphoreType.DMA((N_DEV - 1,)),
                        pltpu.SemaphoreType.DMA((N_DEV - 1,))],
        name="small_allreduce")(gpart, *dep_arg)
    out_shape = [SDS((r, 128), F32) for _ in range(4) for r in row_counts] + [SDS((SUBLANES, 128), F32)]
    res = pl.pallas_call(adam_body, in_specs=[vm] * 4, out_specs=[vm] * len(out_shape), out_shape=out_shape,
                         name="small_adam")(total, w, m, v)
    k = len(row_counts)
    return [res[i * k:(i + 1) * k] for i in range(4)], res[-1]


BIG = ("w_in", "conv_w", "w_conv_out", "w_attn_out", "w_o", "w_ffn_gate", "w_ffn_up", "w_ffn_down")
EARLY = ("w_in", "conv_w")
LATE_MERGE = ("w_conv_out", "w_attn_out", "w_o")
LATE_FFN = ("w_ffn_gate", "w_ffn_up", "w_ffn_down")
TRANSPOSED = ("w_in", "w_ffn_gate", "w_ffn_up")
COL_SHARDED = ("conv_w", "w_attn_out")
SMALL = ("norm1_g", "gate_b", "conv_b", "conv_ln_g", "conv_ln_b", "norm2_g", "norm_f_g")
WEIGHTS = ("norm1_g", "w_in", "gate_b", "conv_w", "conv_b", "conv_ln_g", "conv_ln_b", "w_conv_out", "w_attn_out", "w_o",
           "norm2_g", "w_ffn_gate", "w_ffn_up", "w_ffn_down", "norm_f_g")


def _shard2d(name, a):
    a = a.reshape(a.shape[-2], a.shape[-1])
    if name in TRANSPOSED:
        a = a.T
    if name == "conv_w":
        a = jnp.pad(a, ((0, CONV_PAD - CONV_K), (0, 0)))
    return a


def _from_shard2d(name, val, shape):
    if name in TRANSPOSED:
        val = val.T
    if name == "conv_w":
        val = val[:CONV_K]
    return val.reshape(shape)


def _gathered_to_full(name, g):
    if name in COL_SHARDED:
        return g.transpose(1, 0, 2).reshape(g.shape[1], N_DEV * g.shape[2])
    return g.reshape(N_DEV * g.shape[1], g.shape[2])


def _full_to_blocks(name, g):
    if name in COL_SHARDED:
        return g.reshape(g.shape[0], N_DEV, g.shape[1] // N_DEV).transpose(1, 0, 2)
    return g.reshape(N_DEV, g.shape[0] // N_DEV, g.shape[1])


def _pack_small(d, last_rows):
    vec = jnp.concatenate([d[n].reshape(-1) for n in SMALL]).reshape(SMALL_ROWS - SUBLANES, 128)
    return jnp.concatenate([vec, last_rows], axis=0)


def kernel(x, norm1_g, w_in, gate_b, conv_w, conv_b, conv_ln_g, conv_ln_b, w_conv_out, w_attn_out, w_o, norm2_g, w_ffn_gate, w_ffn_up, w_ffn_down, norm_f_g, loss_target, m_norm1_g, m_w_in, m_gate_b, m_conv_w, m_conv_b, m_conv_ln_g, m_conv_ln_b, m_w_conv_out, m_w_attn_out, m_w_o, m_norm2_g, m_w_ffn_gate, m_w_ffn_up, m_w_ffn_down, m_norm_f_g, v_norm1_g, v_w_in, v_gate_b, v_conv_w, v_conv_b, v_conv_ln_g, v_conv_ln_b, v_w_conv_out, v_w_attn_out, v_w_o, v_norm2_g, v_w_ffn_gate, v_w_ffn_up, v_w_ffn_down, v_norm_f_g):
    wts = dict(norm1_g=norm1_g, w_in=w_in, gate_b=gate_b, conv_w=conv_w, conv_b=conv_b, conv_ln_g=conv_ln_g,
               conv_ln_b=conv_ln_b, w_conv_out=w_conv_out, w_attn_out=w_attn_out, w_o=w_o, norm2_g=norm2_g,
               w_ffn_gate=w_ffn_gate, w_ffn_up=w_ffn_up, w_ffn_down=w_ffn_down, norm_f_g=norm_f_g)
    mom1 = dict(norm1_g=m_norm1_g, w_in=m_w_in, gate_b=m_gate_b, conv_w=m_conv_w, conv_b=m_conv_b, conv_ln_g=m_conv_ln_g,
                conv_ln_b=m_conv_ln_b, w_conv_out=m_w_conv_out, w_attn_out=m_w_attn_out, w_o=m_w_o, norm2_g=m_norm2_g,
                w_ffn_gate=m_w_ffn_gate, w_ffn_up=m_w_ffn_up, w_ffn_down=m_w_ffn_down, norm_f_g=m_norm_f_g)
    mom2 = dict(norm1_g=v_norm1_g, w_in=v_w_in, gate_b=v_gate_b, conv_w=v_conv_w, conv_b=v_conv_b, conv_ln_g=v_conv_ln_g,
                conv_ln_b=v_conv_ln_b, w_conv_out=v_w_conv_out, w_attn_out=v_w_attn_out, w_o=v_w_o, norm2_g=v_norm2_g,
                w_ffn_gate=v_w_ffn_gate, w_ffn_up=v_w_ffn_up, w_ffn_down=v_w_ffn_down, norm_f_g=v_norm_f_g)

    T = x.shape[0] * x.shape[1]
    x2 = x.reshape(T, D_MODEL)
    t2 = loss_target.reshape(T, D_MODEL)

    me = 4 * lax.axis_index("x") + 2 * lax.axis_index("y") + lax.axis_index("c")
    shards = {n: _shard2d(n, wts[n]) for n in BIG}
    sent = {n: shards[n] if n == "conv_w" else shards[n].astype(BF16) for n in BIG}
    small = {n: wts[n].reshape(1, -1) for n in SMALL}

    gathered = _all_gather([sent[n] for n in EARLY])
    full = {n: _gathered_to_full(n, g) for n, g in zip(EARLY, gathered)}
    merge_gather = _send_start("gather", ALL_PEERS, "gather_start_merge", [sent[n] for n in LATE_MERGE], dep=gathered[0])
    ffn_gather = _send_start("gather", (1,) + OTHER_CHIPS, "gather_start_ffn", [sent[n] for n in LATE_FFN],
                             dep=merge_gather["token"])
    ffn_state = {}

    def filled(names, srcs, lands):
        return {n: _gathered_to_full(n, lax.dynamic_update_slice(land, src[None], (me, 0, 0)))
                for n, src, land in zip(names, srcs, lands)}

    def late_weights(names, after):
        if names is LATE_MERGE:
            srcs, lands = _send_wait(merge_gather, after, "gather_wait_merge")
            ffn_state["srcs"], ffn_lands = _send_wait(ffn_gather, after, "gather_wait_ffn")
            ffn_state["forward"] = _send_start("forward", OTHER_CHIPS, "forward_start_ffn", lands=ffn_lands)
            return {**filled(names, srcs, lands), "token": ffn_state["forward"]["token"]}
        _, lands = _send_wait(ffn_state["forward"], after, "forward_wait_ffn")
        return filled(names, ffn_state["srcs"], lands)

    scatters = []
    core = lax.axis_index("c").astype(jnp.int32).reshape(1)

    def emit(names, gw):
        blocks = [_full_to_blocks(n, gw[n]) for n in names]
        if "w_in" in names:
            sums = [_add_pair(g, r, core, "chip_sum_" + n) for n, g, r in zip(names, blocks, _exchange_sibling(blocks))]
            started = _send_start("chip_scatter", OTHER_CHIPS, "scatter_start_" + names[0], sums)
        else:
            started = _send_start("scatter", ALL_PEERS, "scatter_start_" + names[0], blocks)
        scatters.append((names, started))
        return started["token"]

    loss_part, grad_x, gw, gsmall = _local_step(x2, t2, full, small, ffn_gather["token"], late_weights, emit)

    grads, deltas, new_m, new_v = {}, {}, {}, {}
    after = grad_x
    for names, started in scatters:
        srcs, lands = _send_wait(started, after, "scatter_wait_" + names[0])
        mine = (me >> 1 if started["mode"] == "chip_scatter" else me).astype(jnp.int32).reshape(1)
        for n, src, land in zip(names, srcs, lands):
            g, d, mo, vo = _sum_adam(land, src, mine, shards[n], _shard2d(n, mom1[n]), _shard2d(n, mom2[n]), "adam_" + n)
            for dst, val in ((grads, g), (deltas, d), (new_m, mo), (new_v, vo)):
                dst[n] = _from_shard2d(n, val, wts[n].shape)
            after = g

    zeros, ones = jnp.zeros((SUBLANES, 128), F32), jnp.ones((SUBLANES, 128), F32)
    row_counts = [wts[n].size // 128 for n in SMALL]
    kinds, loss_rows = _small_allreduce_adam(
        _pack_small(gsmall, jnp.broadcast_to(loss_part, (SUBLANES, 128))), _pack_small(wts, zeros),
        _pack_small(mom1, zeros), _pack_small(mom2, ones), row_counts, after)
    for dst, vals in zip((grads, deltas, new_m, new_v), kinds):
        dst.update({n: val.reshape(wts[n].shape) for n, val in zip(SMALL, vals)})
    loss = loss_rows[0, 0]
    return (loss, grad_x.reshape(x.shape), *[grads[n] for n in WEIGHTS], *[deltas[n] for n in WEIGHTS],
            *[new_m[n] for n in WEIGHTS], *[new_v[n] for n in WEIGHTS])
```

```python
import math

import numpy as np
import jax
import jax.numpy as jnp
from jax import lax
from jax.experimental import pallas as pl
from jax.experimental.pallas import tpu as pltpu

F32 = jnp.float32
BF16 = jnp.bfloat16
SDS = jax.ShapeDtypeStruct
MESH = pl.DeviceIdType.MESH

D_MODEL = 1024
SEQ = 2048
HEAD_DIM = 64
GROUPS = ((128, 1), (512, 4), (2048, 16))
HEADS_PER_GROUP = 8
N_HEADS = 24
ATTN_WIDTH = N_HEADS * HEAD_DIM
ATTN_OUT = HEADS_PER_GROUP * HEAD_DIM
CONV_K = 31
CONV_PAD = 32
D_FF = 2816
IN_WIDTH = 3 * ATTN_WIDTH + 2 * D_MODEL + 2 * D_MODEL
RMS_EPS = 1e-6
LN_EPS = 1e-5
Q_BLOCK = 128
LANES = 128
NEG = -1e30
N_DEV = 8

ADAM_LR = 0.001
ADAM_B1 = 0.9
ADAM_B2 = 0.999
ADAM_EPS = 1e-08
ADAM_WD = 0.01
ADAM_STEP = 10


def _alibi_slope_list(n):
    def pow2(m):
        start = 2.0 ** (-8.0 / m)
        return [start ** (i + 1) for i in range(m)]
    if math.log2(n).is_integer():
        return pow2(n)
    c = 2 ** math.floor(math.log2(n))
    return pow2(c) + _alibi_slope_list(2 * c)[0::2][: n - c]


def _slopes_times_dilation():
    s = np.asarray(sorted(_alibi_slope_list(N_HEADS), reverse=True), dtype=np.float32).reshape(3, HEADS_PER_GROUP)
    r = np.asarray([g[1] for g in GROUPS], dtype=np.float32)[:, None]
    return (s * r).reshape(N_HEADS)


def _sigmoid(x):
    return 0.5 * jnp.tanh(0.5 * x) + 0.5


def _dot(a, b):
    return jnp.dot(a, b, preferred_element_type=F32)


def _dot_nt(a, b):
    return lax.dot_general(a, b, (((1,), (1,)), ((), ())), preferred_element_type=F32)


def _dot_tn(a, b):
    return lax.dot_general(a, b, (((0,), (0,)), ((), ())), preferred_element_type=F32)


def _rowsum(x):
    return jnp.sum(x, axis=0, keepdims=True)


def _params(*sem):
    return pltpu.CompilerParams(dimension_semantics=sem)


def _anchored(body, n_in, dep):
    if dep is None:
        return body, [], []

    def wrapped(*refs):
        return body(*refs[:n_in], *refs[n_in + 1:])

    return wrapped, [pl.BlockSpec(memory_space=pl.ANY)], [dep]


IN_TM = 256
IN_CHUNK = 512


BLOCK_COLS = IN_WIDTH // N_DEV
CHIP_COLS = 2 * BLOCK_COLS


def _gather_in_proj(x, g1, w_block_t, conv_block, chip_order):
    T = x.shape[0]
    tm = IN_TM
    nrow = T // tm

    def body(order_ref, x_ref, g_ref, w_hbm, cw_hbm, h_ref, proj_ref, gath, cgath,
             wv, send_sems, recv_sems, csend_sems, crecv_sems, local_sems, load_sems):
        p, i = pl.program_id(0), pl.program_id(1)
        xx, yy, c = lax.axis_index("x"), lax.axis_index("y"), lax.axis_index("c")
        me = 4 * xx + 2 * yy + c
        sibling = (xx, yy, 1 - c)
        peers = {1: (1 - xx, yy, c), 2: (xx, 1 - yy, c), 3: (1 - xx, 1 - yy, c)}

        def dev(t):
            return 4 * t[0] + 2 * t[1] + t[2]

        def block_copy(k, src, slot, to):
            return pltpu.make_async_remote_copy(src_ref=src, dst_ref=gath.at[slot], send_sem=send_sems.at[k],
                                                recv_sem=recv_sems.at[k], device_id=to, device_id_type=MESH)

        own_local = pltpu.make_async_copy(w_hbm, gath.at[me], local_sems.at[0])
        conv_local = pltpu.make_async_copy(cw_hbm, cgath.at[me], local_sems.at[1])
        to_sibling = block_copy(0, w_hbm, me, sibling)
        from_sibling = block_copy(0, w_hbm, me ^ 1, sibling)
        to_peer = {q: block_copy(q, w_hbm, me, peers[q]) for q in (1, 2, 3)}
        from_peer = {q: block_copy(q, w_hbm, dev(peers[q]), peers[q]) for q in (1, 2, 3)}
        forward = {q: block_copy(3 + q, gath.at[dev(peers[q])], dev(peers[q]), sibling) for q in (1, 2, 3)}
        from_forward = {q: block_copy(3 + q, w_hbm, dev(peers[q]) ^ 1, sibling) for q in (1, 2, 3)}
        conv_send, conv_recv = [], []
        for k in range(1, N_DEV):
            to = (xx ^ ((k >> 2) & 1), yy ^ ((k >> 1) & 1), c ^ (k & 1))
            for lst, slot in ((conv_send, me), (conv_recv, dev(to))):
                lst.append(pltpu.make_async_remote_copy(
                    src_ref=cw_hbm, dst_ref=cgath.at[slot], send_sem=csend_sems.at[k - 1], recv_sem=crecv_sems.at[k - 1],
                    device_id=to, device_id_type=MESH))

        @pl.when((p == 0) & (i == 0))
        def _():
            own_local.start()
            conv_local.start()
            for cp in conv_send:
                cp.start()
            to_sibling.start()
            for q in (1, 2, 3):
                to_peer[q].start()
            own_local.wait()
            conv_local.wait()
            from_sibling.wait_recv()

        for q in (1, 2, 3):
            @pl.when((p == q) & (i == 0))
            def _(q=q):
                from_peer[q].wait_recv()
                forward[q].start()
                from_forward[q].wait_recv()

        @pl.when(i == 0)
        def _():
            first = 2 * order_ref[p]
            loads = [pltpu.make_async_copy(gath.at[first + half], wv.at[pl.ds(half * BLOCK_COLS, BLOCK_COLS)],
                                           load_sems.at[half]) for half in range(2)]
            for cp in loads:
                cp.start()
            for cp in loads:
                cp.wait()

        xv = x_ref[...]
        r = lax.rsqrt(jnp.mean(xv * xv, axis=-1, keepdims=True) + RMS_EPS)
        h = (xv * r * g_ref[...]).astype(BF16)

        @pl.when(p == 0)
        def _():
            h_ref[...] = h

        for lo in range(0, CHIP_COLS, IN_CHUNK):
            hi = min(lo + IN_CHUNK, CHIP_COLS)
            proj_ref[:, lo:hi] = _dot_nt(h, wv[lo:hi, :])

        @pl.when((p == 3) & (i == nrow - 1))
        def _():
            to_sibling.wait_send()
            for q in (1, 2, 3):
                to_peer[q].wait_send()
                forward[q].wait_send()
            for cp in conv_recv:
                cp.wait_recv()
            for cp in conv_send:
                cp.wait_send()

    any_spec = pl.BlockSpec(memory_space=pl.ANY)
    return pl.pallas_call(
        body,
        grid_spec=pltpu.PrefetchScalarGridSpec(
            num_scalar_prefetch=1, grid=(4, nrow),
            in_specs=[pl.BlockSpec((tm, D_MODEL), lambda p, i, order: (i, 0)),
                      pl.BlockSpec((1, D_MODEL), lambda p, i, order: (0, 0)), any_spec, any_spec],
            out_specs=[pl.BlockSpec((tm, D_MODEL), lambda p, i, order: (jnp.where(p == 0, i, nrow - 1), 0)),
                       pl.BlockSpec((tm, CHIP_COLS), lambda p, i, order: (i, order[p])), any_spec, any_spec],
            scratch_shapes=[pltpu.VMEM((CHIP_COLS, D_MODEL), BF16),
                            pltpu.SemaphoreType.DMA((7,)), pltpu.SemaphoreType.DMA((7,)),
                            pltpu.SemaphoreType.DMA((N_DEV - 1,)), pltpu.SemaphoreType.DMA((N_DEV - 1,)),
                            pltpu.SemaphoreType.DMA((2,)), pltpu.SemaphoreType.DMA((2,))]),
        out_shape=[SDS((T, D_MODEL), BF16), SDS((T, IN_WIDTH), F32),
                   SDS((N_DEV,) + w_block_t.shape, w_block_t.dtype), SDS((N_DEV,) + conv_block.shape, conv_block.dtype)],
        compiler_params=_params("arbitrary", "arbitrary"), name="gather_in_proj")(chip_order, x, g1, w_block_t, conv_block)


def _mm_tn(a, b, out_dtype, name, tn, tt=1024):
    T, K = a.shape
    N = b.shape[1]
    nt = T // tt

    def body(a_ref, b_ref, o_ref, acc):
        t = pl.program_id(1)

        @pl.when(t == 0)
        def _():
            acc[...] = jnp.zeros_like(acc)

        acc[...] += _dot_tn(a_ref[...], b_ref[...])

        @pl.when(t == nt - 1)
        def _():
            o_ref[...] = acc[...].astype(o_ref.dtype)

    return pl.pallas_call(
        body, grid=(N // tn, nt),
        in_specs=[pl.BlockSpec((tt, K), lambda j, t: (t, 0)),
                  pl.BlockSpec((tt, tn), lambda j, t: (t, j))],
        out_specs=pl.BlockSpec((K, tn), lambda j, t: (0, j)),
        out_shape=SDS((K, N), out_dtype),
        scratch_shapes=[pltpu.VMEM((K, tn), F32)],
        compiler_params=_params("parallel", "arbitrary"), name=name)(a, b)


def _gather_classes(src_ref, dst, r, row0=0):
    L = SEQ // r
    for c in range(r):
        dst[row0 + c * L:row0 + (c + 1) * L, :] = src_ref[0, pl.ds(c, L, stride=r), :].astype(dst.dtype)


def _scatter_classes(src, dst, r, row0=0):
    L = SEQ // r
    for c in range(r):
        dst[pl.ds(c, L, stride=r), :] = src[row0 + c * L:row0 + (c + 1) * L, :].astype(dst.dtype)


def _attn_masks(slope_r):
    qi = lax.broadcasted_iota(jnp.int32, (Q_BLOCK, Q_BLOCK), 0)
    kj = lax.broadcasted_iota(jnp.int32, (Q_BLOCK, Q_BLOCK), 1)
    rel = (qi - kj).astype(F32)
    bias_cur = jnp.where(qi >= kj, -slope_r * rel, NEG)
    bias_prev = jnp.where(qi <= kj, -slope_r * (rel + float(Q_BLOCK)), NEG)
    return bias_cur, bias_prev


def _store_biases(bias, sl_ref, g, hp):
    for hh in range(2):
        cur, prev = _attn_masks(sl_ref[g * HEADS_PER_GROUP + 2 * hp + hh])
        rows = slice(hh * Q_BLOCK, (hh + 1) * Q_BLOCK)
        bias[0, rows, 0:Q_BLOCK] = prev
        bias[1, rows, 0:Q_BLOCK] = jnp.full((Q_BLOCK, Q_BLOCK), NEG, F32)
        bias[0, rows, Q_BLOCK:] = cur
        bias[1, rows, Q_BLOCK:] = cur


def _transpose_pairs(src, dst):
    dst[0, :, 0:Q_BLOCK] = jnp.zeros((LANES, Q_BLOCK), dst.dtype)
    nblk = SEQ // Q_BLOCK
    for b in range(nblk):
        t = src[(b + 1) * Q_BLOCK:(b + 2) * Q_BLOCK, :].T
        dst[b, :, Q_BLOCK:] = t
        if b + 1 < nblk:
            dst[b + 1, :, 0:Q_BLOCK] = t


def _stack_heads(t, low):
    z = jnp.zeros_like(t)
    return jnp.concatenate([jnp.where(low, t, z), jnp.where(low, z, t)], axis=0)


def _unstack_heads(t2, low):
    return jnp.where(low, t2[:Q_BLOCK], t2[Q_BLOCK:])


def _unit_offsets(u, nb):
    off = pl.multiple_of(u * Q_BLOCK, Q_BLOCK)
    n = u & (nb - 1)
    c = u >> int(math.log2(nb))
    return off, n == 0, c, n


ATTN_UNROLL = 4


def _attn_fwd(qkv, slopes_r, batch, dep=None):
    nblk = SEQ // Q_BLOCK

    def body(sl_ref, *refs):
        qkv_refs = refs[:9]
        att_ref, lse_ref = refs[9:11]
        qd, kd, vd, kt, opos, lpos, bias = refs[11:]
        hp = pl.program_id(1)
        low = lax.broadcasted_iota(jnp.int32, (Q_BLOCK, LANES), 1) < HEAD_DIM

        for g in range(3):
            r = GROUPS[g][1]
            nb = SEQ // r // Q_BLOCK
            _gather_classes(qkv_refs[3 * g], qd, r)
            kd[0:Q_BLOCK, :] = jnp.zeros((Q_BLOCK, LANES), BF16)
            vd[0:Q_BLOCK, :] = jnp.zeros((Q_BLOCK, LANES), BF16)
            _gather_classes(qkv_refs[3 * g + 1], kd, r, Q_BLOCK)
            _gather_classes(qkv_refs[3 * g + 2], vd, r, Q_BLOCK)
            _transpose_pairs(kd, kt)
            _store_biases(bias, sl_ref, g, hp)

            def unit(u, carry, g=g, r=r, nb=nb):
                off, first, c, n = _unit_offsets(u, nb)
                q2 = _stack_heads(qd[pl.ds(off, Q_BLOCK), :], low)
                s = _dot(q2, kt[u]) * 0.125 + bias[first.astype(jnp.int32)]
                m = jnp.max(s, axis=-1, keepdims=True)
                p = jnp.exp(s - m)
                l = jnp.sum(p, axis=-1, keepdims=True)
                o2 = _dot(p.astype(BF16), vd[pl.ds(off, 2 * Q_BLOCK), :]) * (1.0 / l)
                lse2 = m + jnp.log(l)
                rows = pl.ds(c + n * (Q_BLOCK * r), Q_BLOCK, stride=r)
                opos[g, rows, :] = _unstack_heads(o2, low)
                lpos[g, rows, :] = jnp.where(low, lse2[:Q_BLOCK], lse2[Q_BLOCK:])
                return carry

            lax.fori_loop(0, nblk, unit, 0, unroll=ATTN_UNROLL)

        def merge(i, carry):
            rows = pl.ds(pl.multiple_of(i * 256, 256), 256)
            l0, l1, l2 = lpos[0, rows, :], lpos[1, rows, :], lpos[2, rows, :]
            m = jnp.maximum(jnp.maximum(l0, l1), l2)
            e0, e1, e2 = jnp.exp(l0 - m), jnp.exp(l1 - m), jnp.exp(l2 - m)
            den = e0 + e1 + e2
            att = (e0 * opos[0, rows, :] + e1 * opos[1, rows, :] + e2 * opos[2, rows, :]) / den
            att_ref[0, rows, :] = att.astype(att_ref.dtype)
            lse_ref[0, rows, :] = m + jnp.log(den)
            return carry

        lax.fori_loop(0, SEQ // 256, merge, 0)

    def col(sec, g):
        return pl.BlockSpec((1, SEQ, LANES), lambda b, hp: (b, 0, sec * 12 + g * 4 + hp))

    out = pl.BlockSpec((1, SEQ, LANES), lambda b, hp: (b, 0, hp))
    body, dep_spec, dep_arg = _anchored(body, 10, dep)
    return pl.pallas_call(
        body, grid=(batch, 4),
        in_specs=[pl.BlockSpec(memory_space=pltpu.SMEM)] + [col(sec, g) for g in range(3) for sec in range(3)] + dep_spec,
        out_specs=[out, out],
        out_shape=[SDS((batch, SEQ, ATTN_OUT), BF16), SDS((batch, SEQ, ATTN_OUT), F32)],
        scratch_shapes=[pltpu.VMEM((SEQ, LANES), BF16), pltpu.VMEM((Q_BLOCK + SEQ, LANES), BF16),
                        pltpu.VMEM((Q_BLOCK + SEQ, LANES), BF16), pltpu.VMEM((nblk, LANES, 2 * Q_BLOCK), BF16),
                        pltpu.VMEM((3, SEQ, LANES), F32), pltpu.VMEM((3, SEQ, LANES), F32),
                        pltpu.VMEM((2, 2 * Q_BLOCK, 2 * Q_BLOCK), F32)],
        compiler_params=_params("parallel", "parallel"), name="attn_fwd")(slopes_r, *([qkv] * 9), *dep_arg)


def _attn_bwd(qkv, datt, lse, dsum, slopes_r, batch):
    nblk = SEQ // Q_BLOCK

    def body(sl_ref, q_ref, k_ref, v_ref, do_ref, l_ref, d_ref, dq_ref, dk_ref, dv_ref,
             qd, kd, vd, dod, kt, vt, ld, dd, dq_acc, dk_acc, dv_acc, dk_part, dv_part, stage, bias):
        gid, hp = pl.program_id(1), pl.program_id(2)
        low = lax.broadcasted_iota(jnp.int32, (Q_BLOCK, LANES), 1) < HEAD_DIM

        def section(g):
            r = GROUPS[g][1]
            nb = SEQ // r // Q_BLOCK
            _gather_classes(q_ref, qd, r)
            kd[0:Q_BLOCK, :] = jnp.zeros((Q_BLOCK, LANES), BF16)
            vd[0:Q_BLOCK, :] = jnp.zeros((Q_BLOCK, LANES), BF16)
            _gather_classes(k_ref, kd, r, Q_BLOCK)
            _gather_classes(v_ref, vd, r, Q_BLOCK)
            _gather_classes(do_ref, dod, r)
            _gather_classes(l_ref, ld, r)
            _gather_classes(d_ref, dd, r)
            _transpose_pairs(kd, kt)
            _transpose_pairs(vd, vt)
            _store_biases(bias, sl_ref, g, hp)

            def unit(u, carry):
                off, first, _, _ = _unit_offsets(u, nb)
                pair = pl.ds(off, 2 * Q_BLOCK)
                q2 = _stack_heads(qd[pl.ds(off, Q_BLOCK), :], low)
                do2 = _stack_heads(dod[pl.ds(off, Q_BLOCK), :], low)
                lse_t = ld[pl.ds(off, Q_BLOCK), :]
                dsum_t = dd[pl.ds(off, Q_BLOCK), :]
                lse2 = jnp.concatenate([lse_t[:, 0:1], lse_t[:, HEAD_DIM:HEAD_DIM + 1]], axis=0)
                dsum2 = jnp.concatenate([dsum_t[:, 0:1], dsum_t[:, HEAD_DIM:HEAD_DIM + 1]], axis=0)
                s = _dot(q2, kt[u]) * 0.125 + bias[first.astype(jnp.int32)]
                p = jnp.exp(s - lse2)
                ds = (p * (_dot(do2, vt[u]) - dsum2)).astype(BF16)
                dq_acc[pl.ds(off, Q_BLOCK), :] = _unstack_heads(_dot(ds, kd[pair, :]), low) * 0.125
                dk_part[u] = _dot_tn(ds, q2) * 0.125
                dv_part[u] = _dot_tn(p.astype(BF16), do2)
                return carry

            lax.fori_loop(0, nblk, unit, 0, unroll=ATTN_UNROLL)
            for part, acc in ((dk_part, dk_acc), (dv_part, dv_acc)):
                for b in range(nblk):
                    t = part[b, Q_BLOCK:, :]
                    if b + 1 < nblk:
                        t = t + part[b + 1, 0:Q_BLOCK, :]
                    acc[b * Q_BLOCK:(b + 1) * Q_BLOCK, :] = t
            for acc, out_ref in ((dq_acc, dq_ref), (dk_acc, dk_ref), (dv_acc, dv_ref)):
                _scatter_classes(acc, stage, r)
                out_ref[0] = stage[...].astype(out_ref.dtype)

        for g in range(3):
            pl.when(gid == g)(lambda g=g: section(g))

    def col(sec):
        return pl.BlockSpec((1, SEQ, LANES), lambda b, g, hp: (b, 0, sec * 12 + g * 4 + hp))

    pos = pl.BlockSpec((1, SEQ, LANES), lambda b, g, hp: (b, 0, hp))
    dout = pl.BlockSpec((1, SEQ, LANES), lambda b, g, hp: (b, 0, g * 4 + hp))
    out = SDS((batch, SEQ, ATTN_WIDTH), BF16)
    seq_bf = pltpu.VMEM((SEQ, LANES), BF16)
    seq_f = pltpu.VMEM((SEQ, LANES), F32)
    pad_bf = pltpu.VMEM((Q_BLOCK + SEQ, LANES), BF16)
    part = pltpu.VMEM((nblk, 2 * Q_BLOCK, LANES), F32)
    blk_t = pltpu.VMEM((nblk, LANES, 2 * Q_BLOCK), BF16)
    return pl.pallas_call(
        body, grid=(batch, 3, 4),
        in_specs=[pl.BlockSpec(memory_space=pltpu.SMEM), col(0), col(1), col(2), pos, pos, pos],
        out_specs=[dout, dout, dout],
        out_shape=[out, out, out],
        scratch_shapes=[seq_bf, pad_bf, pad_bf, seq_bf, blk_t, blk_t, seq_f, seq_f, seq_f, seq_f, seq_f, part, part, seq_f,
                        pltpu.VMEM((2, 2 * Q_BLOCK, 2 * Q_BLOCK), F32)],
        compiler_params=_params("parallel", "parallel", "parallel"), name="attn_bwd")(
            slopes_r, qkv, qkv, qkv, datt, lse, dsum)


CONV_TC = 128
U_BLOCK0 = 3 * ATTN_WIDTH // CONV_TC
CONV_ROWS = 128
SUBLANES = 8


def _fill_shifted(sh):
    n = SEQ + CONV_PAD - SUBLANES
    for s in range(1, SUBLANES):
        sh[s, 0:n, :] = sh[0, s:s + n, :]


def _tap(sh, base, offset):
    s = offset % SUBLANES
    return sh[s, pl.ds(pl.multiple_of(base + (offset - s), SUBLANES), CONV_ROWS), :]


def _conv_fwd(u, conv_w, conv_b, batch, dep=None):
    nct = D_MODEL // CONV_TC

    def body(ua_ref, ub_ref, w_ref, b_ref, o_ref, sh):
        sh[0, 0:CONV_PAD, :] = jnp.zeros((CONV_PAD, CONV_TC), F32)
        sh[0, CONV_PAD:, :] = ua_ref[0] * _sigmoid(ub_ref[0])
        _fill_shifted(sh)

        def chunk(c, carry):
            base = pl.multiple_of(c * CONV_ROWS, CONV_ROWS)
            acc = jnp.broadcast_to(b_ref[...], (CONV_ROWS, CONV_TC))
            for t in range(CONV_K):
                acc = acc + _tap(sh, base, t + CONV_PAD - (CONV_K - 1)) * w_ref[t:t + 1, :]
            o_ref[0, pl.ds(base, CONV_ROWS), :] = acc
            return carry

        lax.fori_loop(0, SEQ // CONV_ROWS, chunk, 0)

    body, dep_spec, dep_arg = _anchored(body, 4, dep)
    return pl.pallas_call(
        body, grid=(nct, batch),
        in_specs=[pl.BlockSpec((1, SEQ, CONV_TC), lambda j, b: (b, 0, U_BLOCK0 + j)),
                  pl.BlockSpec((1, SEQ, CONV_TC), lambda j, b: (b, 0, U_BLOCK0 + nct + j)),
                  pl.BlockSpec((CONV_PAD, CONV_TC), lambda j, b: (0, j)),
                  pl.BlockSpec((1, CONV_TC), lambda j, b: (0, j))] + dep_spec,
        out_specs=pl.BlockSpec((1, SEQ, CONV_TC), lambda j, b: (b, 0, j)),
        out_shape=SDS((batch, SEQ, D_MODEL), F32),
        scratch_shapes=[pltpu.VMEM((SUBLANES, SEQ + CONV_PAD, CONV_TC), F32)],
        compiler_params=_params("parallel", "parallel"), name="conv_fwd")(u, u, conv_w, conv_b, *dep_arg)


def _conv_bwd(u, dc1, conv_w, batch, dep=None):
    nct = D_MODEL // CONV_TC
    nchunk = SEQ // CONV_ROWS

    def body(ua_ref, ub_ref, d_ref, w_ref, dua_ref, dub_ref, gw_ref, gb_ref, shc, shd, gacc):
        b = pl.program_id(1)
        shc[0, 0:CONV_PAD, :] = jnp.zeros((CONV_PAD, CONV_TC), F32)
        shc[0, CONV_PAD:, :] = ua_ref[0] * _sigmoid(ub_ref[0])
        _fill_shifted(shc)
        shd[0, 0:SEQ, :] = d_ref[0]
        shd[0, SEQ:, :] = jnp.zeros((CONV_PAD, CONV_TC), F32)
        _fill_shifted(shd)

        @pl.when(b == 0)
        def _():
            gacc[...] = jnp.zeros_like(gacc)
            gb_ref[...] = jnp.zeros_like(gb_ref)

        gb_ref[...] += _rowsum(d_ref[0])

        def chunk(c, carry):
            base = pl.multiple_of(c * CONV_ROWS, CONV_ROWS)
            dcur = shd[0, pl.ds(base, CONV_ROWS), :]
            acc = jnp.zeros((CONV_ROWS, CONV_TC), F32)
            for t in range(CONV_K):
                acc = acc + _tap(shd, base, CONV_K - 1 - t) * w_ref[t:t + 1, :]
                prod = _tap(shc, base, t + CONV_PAD - (CONV_K - 1)) * dcur
                gacc[t] += jnp.sum(prod.reshape(CONV_ROWS // 8, 8, CONV_TC), axis=0)
            ua = ua_ref[0, pl.ds(base, CONV_ROWS), :]
            sg = _sigmoid(ub_ref[0, pl.ds(base, CONV_ROWS), :])
            dua_ref[0, pl.ds(base, CONV_ROWS), :] = (acc * sg).astype(dua_ref.dtype)
            dub_ref[0, pl.ds(base, CONV_ROWS), :] = (acc * ua * sg * (1.0 - sg)).astype(dub_ref.dtype)
            return carry

        lax.fori_loop(0, nchunk, chunk, 0)

        @pl.when(b == batch - 1)
        def _():
            for t in range(CONV_K):
                gw_ref[t:t + 1, :] = jnp.sum(gacc[t], axis=0, keepdims=True)
            gw_ref[CONV_K:CONV_PAD, :] = jnp.zeros((CONV_PAD - CONV_K, CONV_TC), F32)

    du = SDS((batch, SEQ, D_MODEL), BF16)
    body, dep_spec, dep_arg = _anchored(body, 4, dep)
    return pl.pallas_call(
        body, grid=(nct, batch),
        in_specs=[pl.BlockSpec((1, SEQ, CONV_TC), lambda j, b: (b, 0, U_BLOCK0 + j)),
                  pl.BlockSpec((1, SEQ, CONV_TC), lambda j, b: (b, 0, U_BLOCK0 + nct + j)),
                  pl.BlockSpec((1, SEQ, CONV_TC), lambda j, b: (b, 0, j)),
                  pl.BlockSpec((CONV_PAD, CONV_TC), lambda j, b: (0, j))] + dep_spec,
        out_specs=[pl.BlockSpec((1, SEQ, CONV_TC), lambda j, b: (b, 0, j)),
                   pl.BlockSpec((1, SEQ, CONV_TC), lambda j, b: (b, 0, j)),
                   pl.BlockSpec((CONV_PAD, CONV_TC), lambda j, b: (0, j)),
                   pl.BlockSpec((1, CONV_TC), lambda j, b: (0, j))],
        out_shape=[du, du, SDS((CONV_PAD, D_MODEL), F32), SDS((1, D_MODEL), F32)],
        scratch_shapes=[pltpu.VMEM((SUBLANES, SEQ + CONV_PAD, CONV_TC), F32),
                        pltpu.VMEM((SUBLANES, SEQ + CONV_PAD, CONV_TC), F32),
                        pltpu.VMEM((CONV_K, 8, CONV_TC), F32)],
        compiler_params=_params("parallel", "arbitrary"), name="conv_bwd")(u, u, dc1, conv_w, *dep_arg)


MID_TM = 256


def _layernorm_stats(c1):
    mu = jnp.mean(c1, axis=-1, keepdims=True)
    cen = c1 - mu
    rs = lax.rsqrt(jnp.mean(cen * cen, axis=-1, keepdims=True) + LN_EPS)
    return cen * rs, rs


GATE_PARTS = 4
GATE_PART = 2 * D_MODEL // GATE_PARTS
GATE_PART0 = (IN_WIDTH - 2 * D_MODEL) // GATE_PART


def _gate_specs(tm):
    return [pl.BlockSpec((tm, GATE_PART), lambda i, k=k: (i, GATE_PART0 + k)) for k in range(GATE_PARTS)]


def _mid_fwd(att, c1, proj, x, w_a, w_c, w_o, gate_b, ln_g, ln_b, g2, dep=None):
    T = x.shape[0]
    tm = MID_TM

    def body(att_ref, c1_ref, lg0, lg1, lg2, lg3, x_ref, wa_ref, wc_ref, wo_ref, gb_ref, lng_ref, lnb_ref, g2_ref,
             c3_ref, ya_ref, yc_ref, mix_ref, x1_ref, h2_ref):
        logits = jnp.concatenate([lg0[...], lg1[...], lg2[...], lg3[...]], axis=1)
        ya = _dot(att_ref[...], wa_ref[...])
        xh, _ = _layernorm_stats(c1_ref[...])
        c2 = xh * lng_ref[...] + lnb_ref[...]
        c3 = (c2 * _sigmoid(c2)).astype(BF16)
        c3_ref[...] = c3
        yc = _dot(c3, wc_ref[...])
        gates = _sigmoid(logits + gb_ref[...])
        mix = (gates[:, :D_MODEL] * ya + gates[:, D_MODEL:] * yc).astype(BF16)
        ya_ref[...] = ya.astype(BF16)
        yc_ref[...] = yc.astype(BF16)
        mix_ref[...] = mix
        x1 = x_ref[...] + _dot(mix, wo_ref[...])
        x1_ref[...] = x1
        r = lax.rsqrt(jnp.mean(x1 * x1, axis=-1, keepdims=True) + RMS_EPS)
        h2_ref[...] = (x1 * r * g2_ref[...]).astype(BF16)

    row = lambda n: pl.BlockSpec((tm, n), lambda i: (i, 0))
    full = lambda a, b: pl.BlockSpec((a, b), lambda i: (0, 0))
    body, dep_spec, dep_arg = _anchored(body, 10 + GATE_PARTS, dep)
    return pl.pallas_call(
        body, grid=(T // tm,),
        in_specs=[row(ATTN_OUT), row(D_MODEL)] + _gate_specs(tm) + [row(D_MODEL),
                  full(ATTN_OUT, D_MODEL), full(D_MODEL, D_MODEL), full(D_MODEL, D_MODEL),
                  full(1, 2 * D_MODEL), full(1, D_MODEL), full(1, D_MODEL), full(1, D_MODEL)] + dep_spec,
        out_specs=[row(D_MODEL), row(D_MODEL), row(D_MODEL), row(D_MODEL), row(D_MODEL), row(D_MODEL)],
        out_shape=[SDS((T, D_MODEL), BF16), SDS((T, D_MODEL), BF16), SDS((T, D_MODEL), BF16), SDS((T, D_MODEL), BF16),
                   SDS((T, D_MODEL), F32), SDS((T, D_MODEL), BF16)],
        compiler_params=_params("parallel"), name="mid_fwd")(att, c1, *([proj] * GATE_PARTS), x, w_a, w_c, w_o,
                                                             gate_b, ln_g, ln_b, g2, *dep_arg)


def _mid_bwd(dx1b, ya, yc, proj, att, c1, w_a, w_c, w_o, gate_b, ln_g, ln_b, head_ones, dep=None):
    T = dx1b.shape[0]
    tm = MID_TM

    def body(dx_ref, ya_ref, yc_ref, lg0, lg1, lg2, lg3, att_ref, c1_ref, wa_ref, wc_ref, wo_ref, gb_ref, lng_ref,
             lnb_ref, e_ref, dlg_ref, dya_ref, dyc_ref, datt_ref, dsum_ref, dc1_ref, ggb_ref, glg_ref, glb_ref):
        logits = jnp.concatenate([lg0[...], lg1[...], lg2[...], lg3[...]], axis=1)
        @pl.when(pl.program_id(0) == 0)
        def _():
            ggb_ref[...] = jnp.zeros_like(ggb_ref)
            glg_ref[...] = jnp.zeros_like(glg_ref)
            glb_ref[...] = jnp.zeros_like(glb_ref)

        dmix = _dot_nt(dx_ref[...], wo_ref[...])
        gates = _sigmoid(logits + gb_ref[...])
        ga, gc = gates[:, :D_MODEL], gates[:, D_MODEL:]
        dla = dmix * ya_ref[...].astype(F32) * ga * (1.0 - ga)
        dlc = dmix * yc_ref[...].astype(F32) * gc * (1.0 - gc)
        dlg_ref[:, :D_MODEL] = dla.astype(BF16)
        dlg_ref[:, D_MODEL:] = dlc.astype(BF16)
        ggb_ref[:, :D_MODEL] += _rowsum(dla)
        ggb_ref[:, D_MODEL:] += _rowsum(dlc)
        dya = (dmix * ga).astype(BF16)
        dyc = (dmix * gc).astype(BF16)
        dya_ref[...] = dya
        dyc_ref[...] = dyc
        datt = _dot_nt(dya, wa_ref[...])
        datt_ref[...] = datt
        dsum_ref[...] = jnp.dot(datt * att_ref[...].astype(F32), e_ref[...], preferred_element_type=F32,
                                precision=lax.Precision.HIGHEST)
        dc3 = _dot_nt(dyc, wc_ref[...])
        xh, rs = _layernorm_stats(c1_ref[...])
        c2 = xh * lng_ref[...] + lnb_ref[...]
        sg = _sigmoid(c2)
        dc2 = dc3 * (sg * (1.0 + c2 * (1.0 - sg)))
        glg_ref[...] += _rowsum(dc2 * xh)
        glb_ref[...] += _rowsum(dc2)
        dxh = dc2 * lng_ref[...]
        dc1_ref[...] = rs * (dxh - jnp.mean(dxh, axis=-1, keepdims=True) - xh * jnp.mean(dxh * xh, axis=-1, keepdims=True))

    row = lambda n: pl.BlockSpec((tm, n), lambda i: (i, 0))
    full = lambda a, b: pl.BlockSpec((a, b), lambda i: (0, 0))
    body, dep_spec, dep_arg = _anchored(body, 12 + GATE_PARTS, dep)
    return pl.pallas_call(
        body, grid=(T // tm,),
        in_specs=[row(D_MODEL), row(D_MODEL), row(D_MODEL)] + _gate_specs(tm) + [row(ATTN_OUT), row(D_MODEL),
                  full(ATTN_OUT, D_MODEL), full(D_MODEL, D_MODEL), full(D_MODEL, D_MODEL),
                  full(1, 2 * D_MODEL), full(1, D_MODEL), full(1, D_MODEL), full(ATTN_OUT, ATTN_OUT)] + dep_spec,
        out_specs=[row(2 * D_MODEL), row(D_MODEL), row(D_MODEL), row(ATTN_OUT), row(ATTN_OUT), row(D_MODEL),
                   full(1, 2 * D_MODEL), full(1, D_MODEL), full(1, D_MODEL)],
        out_shape=[SDS((T, 2 * D_MODEL), BF16), SDS((T, D_MODEL), BF16), SDS((T, D_MODEL), BF16), SDS((T, ATTN_OUT), F32),
                   SDS((T, ATTN_OUT), F32), SDS((T, D_MODEL), F32),
                   SDS((1, 2 * D_MODEL), F32), SDS((1, D_MODEL), F32), SDS((1, D_MODEL), F32)],
        compiler_params=_params("arbitrary"), name="mid_bwd")(dx1b, ya, yc, *([proj] * GATE_PARTS), att, c1, w_a, w_c, w_o,
                                                               gate_b, ln_g, ln_b, head_ones, *dep_arg)


FFN_TM = 512
FFN_TF = D_FF // 2
FFN_SUB = ((0, 512), (512, 1024), (1024, FFN_TF))


def _rms_bwd(dy_times_g, xh, r):
    return r * (dy_times_g - xh * jnp.mean(dy_times_g * xh, axis=-1, keepdims=True))


def _ffn_fwd(h2, x1, target, gf, w_g_t, w_u_t, w_d):
    T = h2.shape[0]
    tm, tf = FFN_TM, FFN_TF
    nf = D_FF // tf

    def body(h_ref, x1_ref, t_ref, gf_ref, wg_ref, wu_ref, wd_ref,
             a_ref, b_ref, f_ref, dx2_ref, dx2b_ref, loss_ref, gnf_ref, acc):
        i, j = pl.program_id(0), pl.program_id(1)
        h = h_ref[...]

        @pl.when(j == 0)
        def _():
            acc[...] = x1_ref[...]

        for lo, hi in FFN_SUB:
            a = _dot_nt(h, wg_ref[lo:hi, :])
            b = _dot_nt(h, wu_ref[lo:hi, :])
            f = (a * _sigmoid(a) * b).astype(BF16)
            a_ref[:, lo:hi] = a.astype(BF16)
            b_ref[:, lo:hi] = b.astype(BF16)
            f_ref[:, lo:hi] = f
            acc[...] += _dot(f, wd_ref[lo:hi, :])

        @pl.when((i == 0) & (j == nf - 1))
        def _():
            loss_ref[...] = jnp.zeros_like(loss_ref)
            gnf_ref[...] = jnp.zeros_like(gnf_ref)

        @pl.when(j == nf - 1)
        def _():
            x2 = acc[...]
            r = lax.rsqrt(jnp.mean(x2 * x2, axis=-1, keepdims=True) + RMS_EPS)
            xh = x2 * r
            err = xh * gf_ref[...] - t_ref[...]
            loss_ref[...] += (0.5 / D_MODEL) * jnp.sum(err * err)
            dy = err * (1.0 / D_MODEL)
            gnf_ref[...] += _rowsum(dy * xh)
            dx2 = _rms_bwd(dy * gf_ref[...], xh, r)
            dx2_ref[...] = dx2
            dx2b_ref[...] = dx2.astype(BF16)

    row = lambda n: pl.BlockSpec((tm, n), lambda i, j: (i, 0))
    ffb = pl.BlockSpec((tm, tf), lambda i, j: (i, j))
    wblk = pl.BlockSpec((tf, D_MODEL), lambda i, j: (j, 0))
    return pl.pallas_call(
        body, grid=(T // tm, nf),
        in_specs=[row(D_MODEL), row(D_MODEL), row(D_MODEL), pl.BlockSpec((1, D_MODEL), lambda i, j: (0, 0)),
                  wblk, wblk, wblk],
        out_specs=[ffb, ffb, ffb, row(D_MODEL), row(D_MODEL),
                   pl.BlockSpec((1, 128), lambda i, j: (0, 0)), pl.BlockSpec((1, D_MODEL), lambda i, j: (0, 0))],
        out_shape=[SDS((T, D_FF), BF16), SDS((T, D_FF), BF16), SDS((T, D_FF), BF16), SDS((T, D_MODEL), F32),
                   SDS((T, D_MODEL), BF16), SDS((1, 128), F32), SDS((1, D_MODEL), F32)],
        scratch_shapes=[pltpu.VMEM((tm, D_MODEL), F32)],
        compiler_params=_params("arbitrary", "arbitrary"), name="ffn_fwd")(h2, x1, target, gf, w_g_t, w_u_t, w_d)


def _ffn_bwd(dx2b, dx2, a, b, x1, g2, w_g_t, w_u_t, w_d):
    T = dx2.shape[0]
    tm, tf = FFN_TM, FFN_TF
    nf = D_FF // tf

    def body(dxb_ref, dx2_ref, a_ref, b_ref, x1_ref, g2_ref, wg_ref, wu_ref, wd_ref,
             da_ref, db_ref, dx1_ref, dx1b_ref, gn2_ref, acc):
        i, j = pl.program_id(0), pl.program_id(1)
        @pl.when(j == 0)
        def _():
            acc[...] = jnp.zeros_like(acc)

        dxb = dxb_ref[...]
        for lo, hi in FFN_SUB:
            df = _dot_nt(dxb, wd_ref[lo:hi, :])
            av = a_ref[:, lo:hi].astype(F32)
            bv = b_ref[:, lo:hi].astype(F32)
            sg = _sigmoid(av)
            db = (df * av * sg).astype(BF16)
            da = (df * bv * (sg * (1.0 + av * (1.0 - sg)))).astype(BF16)
            da_ref[:, lo:hi] = da
            db_ref[:, lo:hi] = db
            acc[...] += _dot(da, wg_ref[lo:hi, :]) + _dot(db, wu_ref[lo:hi, :])

        @pl.when((i == 0) & (j == nf - 1))
        def _():
            gn2_ref[...] = jnp.zeros_like(gn2_ref)

        @pl.when(j == nf - 1)
        def _():
            dh2 = acc[...]
            x1 = x1_ref[...]
            r = lax.rsqrt(jnp.mean(x1 * x1, axis=-1, keepdims=True) + RMS_EPS)
            xh = x1 * r
            gn2_ref[...] += _rowsum(dh2 * xh)
            dx1 = dx2_ref[...] + _rms_bwd(dh2 * g2_ref[...], xh, r)
            dx1_ref[...] = dx1
            dx1b_ref[...] = dx1.astype(BF16)

    row = lambda n: pl.BlockSpec((tm, n), lambda i, j: (i, 0))
    ffb = pl.BlockSpec((tm, tf), lambda i, j: (i, j))
    wblk = pl.BlockSpec((tf, D_MODEL), lambda i, j: (j, 0))
    return pl.pallas_call(
        body, grid=(T // tm, nf),
        in_specs=[row(D_MODEL), row(D_MODEL), ffb, ffb, row(D_MODEL), pl.BlockSpec((1, D_MODEL), lambda i, j: (0, 0)),
                  wblk, wblk, wblk],
        out_specs=[ffb, ffb, row(D_MODEL), row(D_MODEL), pl.BlockSpec((1, D_MODEL), lambda i, j: (0, 0))],
        out_shape=[SDS((T, D_FF), BF16), SDS((T, D_FF), BF16), SDS((T, D_MODEL), F32), SDS((T, D_MODEL), BF16),
                   SDS((1, D_MODEL), F32)],
        scratch_shapes=[pltpu.VMEM((tm, D_MODEL), F32)],
        compiler_params=_params("arbitrary", "arbitrary"), name="ffn_bwd")(dx2b, dx2, a, b, x1, g2, w_g_t, w_u_t, w_d)


def _in_bwd(pieces, w_in_t, x, dx1, g1, dep=None):
    T = x.shape[0]
    tm = IN_TM
    npc = len(pieces)
    assert sum(p.shape[1] for p in pieces) == IN_WIDTH

    def body(*refs):
        p_refs = refs[:npc]
        w_hbm, x_ref, dx1_ref, g_ref, dx_ref, gn1_ref, w_vmem, sem = refs[npc:]

        @pl.when(pl.program_id(0) == 0)
        def _():
            cp = pltpu.make_async_copy(w_hbm, w_vmem, sem)
            cp.start()
            cp.wait()
            gn1_ref[...] = jnp.zeros_like(gn1_ref)

        dh = jnp.zeros((tm, D_MODEL), F32)
        col = 0
        for p_ref in p_refs:
            for j in range(p_ref.shape[1] // IN_CHUNK):
                dh = dh + _dot(p_ref[:, j * IN_CHUNK:(j + 1) * IN_CHUNK], w_vmem[col:col + IN_CHUNK, :])
                col += IN_CHUNK
        xv = x_ref[...]
        r = lax.rsqrt(jnp.mean(xv * xv, axis=-1, keepdims=True) + RMS_EPS)
        xh = xv * r
        gn1_ref[...] += _rowsum(dh * xh)
        dx_ref[...] = dx1_ref[...] + _rms_bwd(dh * g_ref[...], xh, r)

    row = lambda n: pl.BlockSpec((tm, n), lambda i: (i, 0))
    body, dep_spec, dep_arg = _anchored(body, npc + 4, dep)
    return pl.pallas_call(
        body, grid=(T // tm,),
        in_specs=[row(p.shape[1]) for p in pieces]
        + [pl.BlockSpec(memory_space=pl.ANY), row(D_MODEL), row(D_MODEL), pl.BlockSpec((1, D_MODEL), lambda i: (0, 0))]
        + dep_spec,
        out_specs=[row(D_MODEL), pl.BlockSpec((1, D_MODEL), lambda i: (0, 0))],
        out_shape=[SDS((T, D_MODEL), F32), SDS((1, D_MODEL), F32)],
        scratch_shapes=[pltpu.VMEM((IN_WIDTH, D_MODEL), BF16), pltpu.SemaphoreType.DMA],
        compiler_params=_params("arbitrary"), name="in_bwd")(*pieces, w_in_t, x, dx1, g1, *dep_arg)


def _local_step(x, target, in_proj, small, late_weights=None, emit=None):
    T = x.shape[0]
    batch = T // SEQ
    slopes_r = jnp.asarray(_slopes_times_dilation())
    emit = emit or (lambda names, grads: None)

    h, proj, w = in_proj()
    proj3 = proj.reshape(batch, SEQ, IN_WIDTH)

    att, lse = _attn_fwd(proj3, slopes_r, batch, w.get("token"))
    att = att.reshape(T, ATTN_OUT)
    if late_weights is not None:
        w = {**w, **late_weights("after_attention", att)}

    c1 = _conv_fwd(proj3, w["conv_w"], small["conv_b"], batch, w.get("token")).reshape(T, D_MODEL)
    if late_weights is not None:
        w = {**w, **late_weights(LATE_MERGE, (att, c1))}

    c3, ya, yc, mix, x1, h2 = _mid_fwd(
        att, c1, proj, x, w["w_attn_out"], w["w_conv_out"], w["w_o"],
        small["gate_b"], small["conv_ln_g"], small["conv_ln_b"], small["norm2_g"], w.get("token"))
    if late_weights is not None:
        w = {**w, **late_weights(LATE_FFN, h2)}

    a, b, f, dx2, dx2b, loss, g_normf = _ffn_fwd(h2, x1, target, small["norm_f_g"],
                                                   w["w_ffn_gate"], w["w_ffn_up"], w["w_ffn_down"])

    da, db, dx1, dx1b, g_norm2 = _ffn_bwd(dx2b, dx2, a, b, x1, small["norm2_g"],
                                           w["w_ffn_gate"], w["w_ffn_up"], w["w_ffn_down"])
    gw = {}
    gw["w_ffn_down"] = _mm_tn(f, dx2b, BF16, "gw_ffn_down", tn=1024)
    gw["w_ffn_gate"] = _mm_tn(da, h2, BF16, "gw_ffn_gate", tn=1024)
    gw["w_ffn_up"] = _mm_tn(db, h2, BF16, "gw_ffn_up", tn=1024)
    token = emit(("w_ffn_gate", "w_ffn_up", "w_ffn_down"), gw)

    head_ones = jnp.asarray(np.kron(np.eye(HEADS_PER_GROUP, dtype=np.float32), np.ones((HEAD_DIM, HEAD_DIM), np.float32)))
    dlogits, dya, dyc, datt, dsum, dc1, g_gate_b, g_ln_g, g_ln_b = _mid_bwd(
        dx1b, ya, yc, proj, att, c1, w["w_attn_out"], w["w_conv_out"], w["w_o"],
        small["gate_b"], small["conv_ln_g"], small["conv_ln_b"], head_ones, token)
    gw["w_o"] = _mm_tn(mix, dx1b, BF16, "gw_o", tn=1024)
    gw["w_attn_out"] = _mm_tn(att, dya, BF16, "gw_attn_out", tn=1024)
    gw["w_conv_out"] = _mm_tn(c3, dyc, BF16, "gw_conv_out", tn=1024)
    token = emit(("w_conv_out", "w_attn_out", "w_o"), gw)

    dua, dub, g_conv_w, g_conv_b = _conv_bwd(proj3, dc1.reshape(batch, SEQ, D_MODEL), w["conv_w"], batch, token)

    dq, dk, dv = _attn_bwd(proj3, datt.reshape(batch, SEQ, ATTN_OUT), lse, dsum.reshape(batch, SEQ, ATTN_OUT),
                           slopes_r, batch)
    pieces = [dq.reshape(T, ATTN_WIDTH), dk.reshape(T, ATTN_WIDTH), dv.reshape(T, ATTN_WIDTH),
              dua.reshape(T, D_MODEL), dub.reshape(T, D_MODEL), dlogits]

    names = ("q", "k", "v", "ua", "ub", "gate")
    gw["w_in"] = jnp.concatenate([_mm_tn(p, h, BF16, "gw_in_" + nm, tn=1024) for nm, p in zip(names, pieces)], axis=0)
    gw["conv_w"] = g_conv_w
    token = emit(("w_in", "conv_w"), gw)
    grad_x, g_norm1 = _in_bwd(pieces, w["w_in"], x, dx1, small["norm1_g"], token)

    gsmall = {"norm1_g": g_norm1, "gate_b": g_gate_b, "conv_b": g_conv_b, "conv_ln_g": g_ln_g, "conv_ln_b": g_ln_b,
              "norm2_g": g_norm2, "norm_f_g": g_normf}
    return loss, grad_x, gw, gsmall


ANY = pl.BlockSpec(memory_space=pl.ANY)


HBM =pl.BlockSpec(memory_space=pltpu.HBM)
SEM = pl.BlockSpec(memory_space=pltpu.SEMAPHORE)
ALL_PEERS = tuple(range(1, N_DEV))
OTHER_CHIPS = (2, 4, 6)
SPLIT_EFFECT = pltpu.CompilerParams(has_side_effects=pltpu.SideEffectType.DATAFLOW_SIDE_EFFECTING)


def _exchange_copies(mode, ks, srcs, lands, send_sems, recv_sems):
    x, y, c = lax.axis_index("x"), lax.axis_index("y"), lax.axis_index("c")
    me = 4 * x + 2 * y + c
    send, recv = [], []
    for a in range(len(lands)):
        for i, k in enumerate(ks):
            peer = (x ^ ((k >> 2) & 1), y ^ ((k >> 1) & 1), c ^ (k & 1))
            pidx = 4 * peer[0] + 2 * peer[1] + peer[2]
            if mode == "gather":
                src, to, out_slot, in_slot = srcs[a], peer, me, pidx
            elif mode == "scatter":
                src, to, out_slot, in_slot = srcs[a].at[pidx], peer, me, pidx
            elif mode == "chip_scatter":
                src, to, out_slot, in_slot = srcs[a].at[pidx >> 1], peer, me >> 1, pidx >> 1
            else:
                src, to, out_slot, in_slot = lands[a].at[pidx], (x, y, 1 - c), pidx, pidx ^ 1
            s = a * len(ks) + i
            send.append(pltpu.make_async_remote_copy(
                src_ref=src, dst_ref=lands[a].at[out_slot], send_sem=send_sems.at[s], recv_sem=recv_sems.at[s],
                device_id=to, device_id_type=MESH))
            recv.append(pltpu.make_async_remote_copy(
                src_ref=src, dst_ref=lands[a].at[in_slot], send_sem=send_sems.at[s], recv_sem=recv_sems.at[s],
                device_id=to, device_id_type=MESH))
    return send, recv


def _send_start(mode, ks, name, srcs=(), lands=None, dep=None):
    srcs = list(srcs)
    if lands is None:
        slots = 4 if mode == "chip_scatter" else N_DEV
        lands = [lax.empty((slots,) + (s.shape if mode == "gather" else s.shape[1:]), s.dtype) for s in srcs]
    ns, nl = len(srcs), len(lands)
    nsem = nl * len(ks)

    def body(*refs):
        send, _ = _exchange_copies(mode, ks, refs[:ns], refs[ns:ns + nl], refs[ns + nl], refs[ns + nl + 1])
        for cp in send:
            cp.start()
        token = refs[-1]
        token[...] = jnp.zeros_like(token)

    both = srcs + list(lands)
    body, dep_spec, dep_arg = _anchored(body, ns + nl, dep)
    res = pl.pallas_call(
        body, name=name,
        out_shape=(pltpu.SemaphoreType.DMA((nsem,)), pltpu.SemaphoreType.DMA((nsem,)),
                   *[pltpu.HBM(a.shape, a.dtype) for a in both], SDS((8, 128), F32)),
        in_specs=[HBM] * (ns + nl) + dep_spec,
        out_specs=(SEM, SEM, *([HBM] * (ns + nl)), pl.BlockSpec(memory_space=pltpu.VMEM)),
        input_output_aliases={i: 2 + i for i in range(ns + nl)}, compiler_params=SPLIT_EFFECT,
    )(*[pltpu.with_memory_space_constraint(a, pltpu.HBM) for a in both], *dep_arg)
    return dict(mode=mode, ks=ks, send_sems=res[0], recv_sems=res[1], srcs=res[2:2 + ns], lands=res[2 + ns:2 + ns + nl],
                token=res[-1])


def _send_wait(started, after, name):
    ns, nl = len(started["srcs"]), len(started["lands"])

    def body(*refs):
        send, recv = _exchange_copies(started["mode"], started["ks"], refs[:ns], refs[ns:ns + nl],
                                      refs[ns + nl], refs[ns + nl + 1])
        for cp in send:
            cp.wait_send()
        for cp in recv:
            cp.wait_recv()

    both = list(started["srcs"]) + list(started["lands"])
    after = after if isinstance(after, (tuple, list)) else (after,)
    res = pl.pallas_call(
        body, name=name,
        out_shape=tuple(pltpu.HBM(a.shape, a.dtype) for a in both),
        in_specs=[HBM] * (ns + nl) + [SEM, SEM] + [ANY] * len(after), out_specs=tuple([HBM] * (ns + nl)),
        input_output_aliases={i: i for i in range(ns + nl)}, compiler_params=SPLIT_EFFECT,
    )(*both, started["send_sems"], started["recv_sems"], *after)
    return res[:ns], res[ns:]


def _exchange_sibling(gs):
    n = len(gs)

    def body(*refs):
        ins, outs = refs[:n], refs[n:2 * n]
        send_sems, recv_sems = refs[2 * n:]
        x, y, c = lax.axis_index("x"), lax.axis_index("y"), lax.axis_index("c")
        copies = []
        for a in range(n):
            for j in range(4):
                copies.append(pltpu.make_async_remote_copy(
                    src_ref=ins[a].at[2 * j + (1 - c)], dst_ref=outs[a].at[j],
                    send_sem=send_sems.at[a, j], recv_sem=recv_sems.at[a, j],
                    device_id=(x, y, 1 - c), device_id_type=MESH))
        for cp in copies:
            cp.start()
        for cp in copies:
            cp.wait_recv()
        for cp in copies:
            cp.wait_send()

    return pl.pallas_call(
        body, in_specs=[ANY] * n, out_specs=[ANY] * n,
        out_shape=[SDS((4,) + g.shape[1:], g.dtype) for g in gs],
        scratch_shapes=[pltpu.SemaphoreType.DMA((n, 4)), pltpu.SemaphoreType.DMA((n, 4))],
        name="reduce_scatter_sibling")(*gs)


def _add_pair(g, r1, core, name):
    _, rows, cols = g.shape
    tr = _row_tile(rows, cols, 3 * g.dtype.itemsize)

    def body(c_ref, g_ref, r_ref, o_ref):
        o_ref[...] = (g_ref[...].astype(F32) + r_ref[...].astype(F32)).astype(o_ref.dtype)

    return pl.pallas_call(
        body,
        grid_spec=pltpu.PrefetchScalarGridSpec(
            num_scalar_prefetch=1, grid=(4, rows // tr),
            in_specs=[pl.BlockSpec((1, tr, cols), lambda j, i, c_ref: (2 * j + c_ref[0], i, 0)),
                      pl.BlockSpec((1, tr, cols), lambda j, i, c_ref: (j, i, 0))],
            out_specs=pl.BlockSpec((1, tr, cols), lambda j, i, c_ref: (j, i, 0))),
        out_shape=SDS((4, rows, cols), g.dtype),
        compiler_params=_params("parallel", "parallel"), name=name)(core, g, r1)


def _row_tile(rows, cols, itemsize_total):
    budget = (4 << 20) // max(1, cols * itemsize_total)
    if rows <= budget:
        return rows
    t = rows
    while t > budget and t % 2 == 0 and (t // 2) % 16 == 0:
        t //= 2
    return t


def _adam_math(g, w, m, v):
    m_new = ADAM_B1 * m + (1.0 - ADAM_B1) * g
    v_new = ADAM_B2 * v + (1.0 - ADAM_B2) * (g * g)
    m_hat = m_new / (1.0 - ADAM_B1 ** ADAM_STEP)
    v_hat = v_new / (1.0 - ADAM_B2 ** ADAM_STEP)
    delta = -ADAM_LR * (m_hat / (jnp.sqrt(v_hat) + ADAM_EPS) + ADAM_WD * w)
    return delta, m_new, v_new


def _sum_adam(parts, own, mine, w, m, v, name):
    rows, cols = w.shape
    nparts = parts.shape[0]
    tr = _row_tile(rows, cols, (nparts + 1) * parts.dtype.itemsize + 7 * 4)

    def body(mine_ref, p_ref, own_ref, w_ref, m_ref, v_ref, g_ref, d_ref, mo_ref, vo_ref):
        g = None
        for s in range(nparts):
            part = jnp.where(mine_ref[0] == s, own_ref[0], p_ref[s]).astype(F32)
            g = part if g is None else g + part
        delta, m_new, v_new = _adam_math(g, w_ref[...], m_ref[...], v_ref[...])
        g_ref[...] = g
        d_ref[...] = delta
        mo_ref[...] = m_new
        vo_ref[...] = v_new

    blk = pl.BlockSpec((tr, cols), lambda i, mine_ref: (i, 0))
    out = SDS((rows, cols), F32)
    return pl.pallas_call(
        body,
        grid_spec=pltpu.PrefetchScalarGridSpec(
            num_scalar_prefetch=1, grid=(rows // tr,),
            in_specs=[pl.BlockSpec((nparts, tr, cols), lambda i, mine_ref: (0, i, 0)),
                      pl.BlockSpec((1, tr, cols), lambda i, mine_ref: (mine_ref[0], i, 0)), blk, blk, blk],
            out_specs=[blk, blk, blk, blk]),
        out_shape=[out, out, out, out],
        compiler_params=_params("parallel"), name=name)(mine, parts, own, w, m, v)


SMALL_ROWS = 72


def _small_allreduce_adam(gpart, w, m, v, row_counts, dep=None):
    def reduce_body(g_ref, go_ref, gath, send_sems, recv_sems):
        x, y, c = lax.axis_index("x"), lax.axis_index("y"), lax.axis_index("c")
        me = 4 * x + 2 * y + c
        gath[me] = g_ref[...]
        copies = []
        for k in range(1, N_DEV):
            fx, fy, fc = (k >> 2) & 1, (k >> 1) & 1, k & 1
            peer = (x ^ fx, y ^ fy, c ^ fc)
            copies.append(pltpu.make_async_remote_copy(
                src_ref=gath.at[me], dst_ref=gath.at[me], send_sem=send_sems.at[k - 1], recv_sem=recv_sems.at[k - 1],
                device_id=peer, device_id_type=MESH))
        for cp in copies:
            cp.start()
        for cp in copies:
            cp.wait_recv()
        for cp in copies:
            cp.wait_send()
        g = gath[0]
        for d in range(1, N_DEV):
            g = g + gath[d]
        go_ref[...] = g

    def adam_body(g_ref, w_ref, m_ref, v_ref, *out_refs):
        g = g_ref[...]
        delta, m_new, v_new = _adam_math(g, w_ref[...], m_ref[...], v_ref[...])
        outs = iter(out_refs)
        for val in (g, delta, m_new, v_new):
            lo = 0
            for r in row_counts:
                next(outs)[...] = val[lo:lo + r]
                lo += r
        next(outs)[...] = g[SMALL_ROWS - SUBLANES:]

    vm = pl.BlockSpec(memory_space=pltpu.VMEM)
    reduce_body, dep_spec, dep_arg = _anchored(reduce_body, 1, dep)
    total = pl.pallas_call(
        reduce_body, in_specs=[vm] + dep_spec, out_specs=vm, out_shape=SDS((SMALL_ROWS, 128), F32),
        scratch_shapes=[pltpu.VMEM((N_DEV, SMALL_ROWS, 128), F32), pltpu.SemaphoreType.DMA((N_DEV - 1,)),
                        pltpu.SemaphoreType.DMA((N_DEV - 1,))],
        name="small_allreduce")(gpart, *dep_arg)
    out_shape = [SDS((r, 128), F32) for _ in range(4) for r in row_counts] + [SDS((SUBLANES, 128), F32)]
    res = pl.pallas_call(adam_body, in_specs=[vm] * 4, out_specs=[vm] * len(out_shape), out_shape=out_shape,
                         name="small_adam")(total, w, m, v)
    k = len(row_counts)
    return [res[i * k:(i + 1) * k] for i in range(4)], res[-1]


BIG = ("w_in", "conv_w", "w_conv_out", "w_attn_out", "w_o", "w_ffn_gate", "w_ffn_up", "w_ffn_down")
LATE_MERGE = ("w_conv_out", "w_attn_out", "w_o")
LATE_FFN = ("w_ffn_gate", "w_ffn_up", "w_ffn_down")
TRANSPOSED = ("w_in", "w_ffn_gate", "w_ffn_up")
COL_SHARDED = ("conv_w", "w_attn_out")
SMALL = ("norm1_g", "gate_b", "conv_b", "conv_ln_g", "conv_ln_b", "norm2_g", "norm_f_g")
WEIGHTS = ("norm1_g", "w_in", "gate_b", "conv_w", "conv_b", "conv_ln_g", "conv_ln_b", "w_conv_out", "w_attn_out", "w_o",
           "norm2_g", "w_ffn_gate", "w_ffn_up", "w_ffn_down", "norm_f_g")


def _shard2d(name, a):
    a = a.reshape(a.shape[-2], a.shape[-1])
    if name in TRANSPOSED:
        a = a.T
    if name == "conv_w":
        a = jnp.pad(a, ((0, CONV_PAD - CONV_K), (0, 0)))
    return a


def _from_shard2d(name, val, shape):
    if name in TRANSPOSED:
        val = val.T
    if name == "conv_w":
        val = val[:CONV_K]
    return val.reshape(shape)


def _gathered_to_full(name, g):
    if name in COL_SHARDED:
        return g.transpose(1, 0, 2).reshape(g.shape[1], N_DEV * g.shape[2])
    return g.reshape(N_DEV * g.shape[1], g.shape[2])


def _full_to_blocks(name, g):
    if name in COL_SHARDED:
        return g.reshape(g.shape[0], N_DEV, g.shape[1] // N_DEV).transpose(1, 0, 2)
    return g.reshape(N_DEV, g.shape[0] // N_DEV, g.shape[1])


def _pack_small(d, last_rows):
    vec = jnp.concatenate([d[n].reshape(-1) for n in SMALL]).reshape(SMALL_ROWS - SUBLANES, 128)
    return jnp.concatenate([vec, last_rows], axis=0)


def kernel(x, norm1_g, w_in, gate_b, conv_w, conv_b, conv_ln_g, conv_ln_b, w_conv_out, w_attn_out, w_o, norm2_g, w_ffn_gate, w_ffn_up, w_ffn_down, norm_f_g, loss_target, m_norm1_g, m_w_in, m_gate_b, m_conv_w, m_conv_b, m_conv_ln_g, m_conv_ln_b, m_w_conv_out, m_w_attn_out, m_w_o, m_norm2_g, m_w_ffn_gate, m_w_ffn_up, m_w_ffn_down, m_norm_f_g, v_norm1_g, v_w_in, v_gate_b, v_conv_w, v_conv_b, v_conv_ln_g, v_conv_ln_b, v_w_conv_out, v_w_attn_out, v_w_o, v_norm2_g, v_w_ffn_gate, v_w_ffn_up, v_w_ffn_down, v_norm_f_g):
    wts = dict(norm1_g=norm1_g, w_in=w_in, gate_b=gate_b, conv_w=conv_w, conv_b=conv_b, conv_ln_g=conv_ln_g,
               conv_ln_b=conv_ln_b, w_conv_out=w_conv_out, w_attn_out=w_attn_out, w_o=w_o, norm2_g=norm2_g,
               w_ffn_gate=w_ffn_gate, w_ffn_up=w_ffn_up, w_ffn_down=w_ffn_down, norm_f_g=norm_f_g)
    mom1 = dict(norm1_g=m_norm1_g, w_in=m_w_in, gate_b=m_gate_b, conv_w=m_conv_w, conv_b=m_conv_b, conv_ln_g=m_conv_ln_g,
                conv_ln_b=m_conv_ln_b, w_conv_out=m_w_conv_out, w_attn_out=m_w_attn_out, w_o=m_w_o, norm2_g=m_norm2_g,
                w_ffn_gate=m_w_ffn_gate, w_ffn_up=m_w_ffn_up, w_ffn_down=m_w_ffn_down, norm_f_g=m_norm_f_g)
    mom2 = dict(norm1_g=v_norm1_g, w_in=v_w_in, gate_b=v_gate_b, conv_w=v_conv_w, conv_b=v_conv_b, conv_ln_g=v_conv_ln_g,
                conv_ln_b=v_conv_ln_b, w_conv_out=v_w_conv_out, w_attn_out=v_w_attn_out, w_o=v_w_o, norm2_g=v_norm2_g,
                w_ffn_gate=v_w_ffn_gate, w_ffn_up=v_w_ffn_up, w_ffn_down=v_w_ffn_down, norm_f_g=v_norm_f_g)

    T = x.shape[0] * x.shape[1]
    x2 = x.reshape(T, D_MODEL)
    t2 = loss_target.reshape(T, D_MODEL)

    me = 4 * lax.axis_index("x") + 2 * lax.axis_index("y") + lax.axis_index("c")
    shards = {n: _shard2d(n, wts[n]) for n in BIG}
    sent = {n: shards[n] if n == "conv_w" else shards[n].astype(BF16) for n in BIG}
    small = {n: wts[n].reshape(1, -1) for n in SMALL}

    xi, yi = lax.axis_index("x"), lax.axis_index("y")
    chip_order = jnp.stack([2 * xi + yi, 2 * (1 - xi) + yi, 2 * xi + (1 - yi), 2 * (1 - xi) + (1 - yi)]).astype(jnp.int32)
    stage = {}

    def in_proj():
        h, proj, w_in_blocks, conv_blocks = _gather_in_proj(x2, small["norm1_g"], sent["w_in"], sent["conv_w"], chip_order)
        near = (1,) + OTHER_CHIPS
        stage["merge"] = _send_start("gather", near, "gather_start_merge", [sent[n] for n in LATE_MERGE], dep=w_in_blocks)
        stage["ffn"] = _send_start("gather", near, "gather_start_ffn", [sent[n] for n in LATE_FFN],
                                   dep=stage["merge"]["token"])
        return h, proj, {"w_in": _gathered_to_full("w_in", w_in_blocks), "conv_w": _gathered_to_full("conv_w", conv_blocks),
                         "token": stage["ffn"]["token"]}

    def filled(names, srcs, lands):
        return {n: _gathered_to_full(n, lax.dynamic_update_slice(land, src[None], (me, 0, 0)))
                for n, src, land in zip(names, srcs, lands)}

    def pass_on(group, after):
        stage[group + "_srcs"], lands = _send_wait(stage[group], after, "gather_wait_" + group)
        stage[group + "_forward"] = _send_start("forward", OTHER_CHIPS, "forward_start_" + group, lands=lands)
        return stage[group + "_forward"]["token"]

    def arrived(group, names, after):
        _, lands = _send_wait(stage[group + "_forward"], after, "forward_wait_" + group)
        return filled(names, stage[group + "_srcs"], lands)

    def late_weights(which, after):
        if which == "after_attention":
            return {"token": pass_on("merge", after)}
        if which is LATE_MERGE:
            return {**arrived("merge", LATE_MERGE, after), "token": pass_on("ffn", after)}
        return arrived("ffn", LATE_FFN, after)

    scatters = []
    core = lax.axis_index("c").astype(jnp.int32).reshape(1)

    def emit(names, gw):
        blocks = [_full_to_blocks(n, gw[n]) for n in names]
        if "w_in" in names:
            sums = [_add_pair(g, r, core, "chip_sum_" + n) for n, g, r in zip(names, blocks, _exchange_sibling(blocks))]
            started = _send_start("chip_scatter", OTHER_CHIPS, "scatter_start_" + names[0], sums)
        else:
            started = _send_start("scatter", ALL_PEERS, "scatter_start_" + names[0], blocks)
        scatters.append((names, started))
        return started["token"]

    loss_part, grad_x, gw, gsmall = _local_step(x2, t2, in_proj, small, late_weights, emit)

    grads, deltas, new_m, new_v = {}, {}, {}, {}
    after = grad_x
    for names, started in scatters:
        srcs, lands = _send_wait(started, after, "scatter_wait_" + names[0])
        mine = (me >> 1 if started["mode"] == "chip_scatter" else me).astype(jnp.int32).reshape(1)
        for n, src, land in zip(names, srcs, lands):
            g, d, mo, vo = _sum_adam(land, src, mine, shards[n], _shard2d(n, mom1[n]), _shard2d(n, mom2[n]), "adam_" + n)
            for dst, val in ((grads, g), (deltas, d), (new_m, mo), (new_v, vo)):
                dst[n] = _from_shard2d(n, val, wts[n].shape)
            after = g

    zeros, ones = jnp.zeros((SUBLANES, 128), F32), jnp.ones((SUBLANES, 128), F32)
    row_counts = [wts[n].size // 128 for n in SMALL]
    kinds, loss_rows = _small_allreduce_adam(
        _pack_small(gsmall, jnp.broadcast_to(loss_part, (SUBLANES, 128))), _pack_small(wts, zeros),
        _pack_small(mom1, zeros), _pack_small(mom2, ones), row_counts, after)
    for dst, vals in zip((grads, deltas, new_m, new_v), kinds):
        dst.update({n: val.reshape(wts[n].shape) for n, val in zip(SMALL, vals)})
    loss = loss_rows[0, 0]
    return (loss, grad_x.reshape(x.shape), *[grads[n] for n in WEIGHTS], *[deltas[n] for n in WEIGHTS],
            *[new_m[n] for n in WEIGHTS], *[new_v[n] for n in WEIGHTS])
```

```python
import math

import numpy as np
import jax
import jax.numpy as jnp
from jax import lax
from jax.experimental import pallas as pl
from jax.experimental.pallas import tpu as pltpu

F32 = jnp.float32
BF16 = jnp.bfloat16
SDS = jax.ShapeDtypeStruct
MESH = pl.DeviceIdType.MESH

D_MODEL = 1024
SEQ = 2048
HEAD_DIM = 64
GROUPS = ((128, 1), (512, 4), (2048, 16))
HEADS_PER_GROUP = 8
N_HEADS = 24
ATTN_WIDTH = N_HEADS * HEAD_DIM
ATTN_OUT = HEADS_PER_GROUP * HEAD_DIM
CONV_K = 31
CONV_PAD = 32
D_FF = 2816
IN_WIDTH = 3 * ATTN_WIDTH + 2 * D_MODEL + 2 * D_MODEL
RMS_EPS = 1e-6
LN_EPS = 1e-5
Q_BLOCK = 128
LANES = 128
NEG = -1e30
N_DEV = 8

ADAM_LR = 0.001
ADAM_B1 = 0.9
ADAM_B2 = 0.999
ADAM_EPS = 1e-08
ADAM_WD = 0.01
ADAM_STEP = 10


def _alibi_slope_list(n):
    def pow2(m):
        start = 2.0 ** (-8.0 / m)
        return [start ** (i + 1) for i in range(m)]
    if math.log2(n).is_integer():
        return pow2(n)
    c = 2 ** math.floor(math.log2(n))
    return pow2(c) + _alibi_slope_list(2 * c)[0::2][: n - c]


def _slopes_times_dilation():
    s = np.asarray(sorted(_alibi_slope_list(N_HEADS), reverse=True), dtype=np.float32).reshape(3, HEADS_PER_GROUP)
    r = np.asarray([g[1] for g in GROUPS], dtype=np.float32)[:, None]
    return (s * r).reshape(N_HEADS)


def _sigmoid(x):
    return 0.5 * jnp.tanh(0.5 * x) + 0.5


def _dot(a, b):
    return jnp.dot(a, b, preferred_element_type=F32)


def _dot_nt(a, b):
    return lax.dot_general(a, b, (((1,), (1,)), ((), ())), preferred_element_type=F32)


def _dot_tn(a, b):
    return lax.dot_general(a, b, (((0,), (0,)), ((), ())), preferred_element_type=F32)


def _rowsum(x):
    return jnp.sum(x, axis=0, keepdims=True)


def _params(*sem):
    return pltpu.CompilerParams(dimension_semantics=sem)


def _anchored(body, n_in, dep):
    if dep is None:
        return body, [], []

    def wrapped(*refs):
        return body(*refs[:n_in], *refs[n_in + 1:])

    return wrapped, [pl.BlockSpec(memory_space=pl.ANY)], [dep]


IN_TM = 256
IN_CHUNK = 512


BLOCK_COLS = IN_WIDTH // N_DEV
CHIP_COLS = 2 * BLOCK_COLS


def _gather_in_proj(x, g1, w_block_t, conv_block, chip_order):
    T = x.shape[0]
    tm = IN_TM
    nrow = T // tm

    def body(order_ref, x_ref, g_ref, w_hbm, cw_hbm, h_ref, proj_ref, gath, cgath,
             wv, h_all, send_sems, recv_sems, csend_sems, crecv_sems, local_sems, load_sems):
        p, i = pl.program_id(0), pl.program_id(1)
        xx, yy, c = lax.axis_index("x"), lax.axis_index("y"), lax.axis_index("c")
        me = 4 * xx + 2 * yy + c
        sibling = (xx, yy, 1 - c)
        peers = {1: (1 - xx, yy, c), 2: (xx, 1 - yy, c), 3: (1 - xx, 1 - yy, c)}

        def dev(t):
            return 4 * t[0] + 2 * t[1] + t[2]

        def block_copy(k, src, slot, to):
            return pltpu.make_async_remote_copy(src_ref=src, dst_ref=gath.at[slot], send_sem=send_sems.at[k],
                                                recv_sem=recv_sems.at[k], device_id=to, device_id_type=MESH)

        own_local = pltpu.make_async_copy(w_hbm, gath.at[me], local_sems.at[0])
        conv_local = pltpu.make_async_copy(cw_hbm, cgath.at[me], local_sems.at[1])
        to_sibling = block_copy(0, w_hbm, me, sibling)
        from_sibling = block_copy(0, w_hbm, me ^ 1, sibling)
        to_peer = {q: block_copy(q, w_hbm, me, peers[q]) for q in (1, 2, 3)}
        from_peer = {q: block_copy(q, w_hbm, dev(peers[q]), peers[q]) for q in (1, 2, 3)}
        forward = {q: block_copy(3 + q, gath.at[dev(peers[q])], dev(peers[q]), sibling) for q in (1, 2, 3)}
        from_forward = {q: block_copy(3 + q, w_hbm, dev(peers[q]) ^ 1, sibling) for q in (1, 2, 3)}
        conv_send, conv_recv = [], []
        for k in range(1, N_DEV):
            to = (xx ^ ((k >> 2) & 1), yy ^ ((k >> 1) & 1), c ^ (k & 1))
            for lst, slot in ((conv_send, me), (conv_recv, dev(to))):
                lst.append(pltpu.make_async_remote_copy(
                    src_ref=cw_hbm, dst_ref=cgath.at[slot], send_sem=csend_sems.at[k - 1], recv_sem=crecv_sems.at[k - 1],
                    device_id=to, device_id_type=MESH))

        @pl.when((p == 0) & (i == 0))
        def _():
            own_local.start()
            conv_local.start()
            for cp in conv_send:
                cp.start()
            to_sibling.start()
            to_peer[1].start()
            to_peer[2].start()
            own_local.wait()
            conv_local.wait()
            from_sibling.wait_recv()

        for q in (1, 2, 3):
            @pl.when((p == q) & (i == 0))
            def _(q=q):
                from_peer[q].wait_recv()
                if q == 1:
                    to_peer[3].start()
                forward[q].start()
                from_forward[q].wait_recv()

        @pl.when(i == 0)
        def _():
            first = 2 * order_ref[p]
            loads = [pltpu.make_async_copy(gath.at[first + half], wv.at[pl.ds(half * BLOCK_COLS, BLOCK_COLS)],
                                           load_sems.at[half]) for half in range(2)]
            for cp in loads:
                cp.start()
            for cp in loads:
                cp.wait()

        @pl.when(p == 0)
        def _():
            xv = x_ref[...]
            r = lax.rsqrt(jnp.mean(xv * xv, axis=-1, keepdims=True) + RMS_EPS)
            hn = (xv * r * g_ref[...]).astype(BF16)
            h_ref[...] = hn
            h_all[i] = hn

        h = h_all[i]
        for lo in range(0, CHIP_COLS, IN_CHUNK):
            hi = min(lo + IN_CHUNK, CHIP_COLS)
            proj_ref[:, lo:hi] = _dot_nt(h, wv[lo:hi, :])

        @pl.when((p == 3) & (i == nrow - 1))
        def _():
            to_sibling.wait_send()
            for q in (1, 2, 3):
                to_peer[q].wait_send()
                forward[q].wait_send()
            for cp in conv_recv:
                cp.wait_recv()
            for cp in conv_send:
                cp.wait_send()

    any_spec = pl.BlockSpec(memory_space=pl.ANY)
    return pl.pallas_call(
        body,
        grid_spec=pltpu.PrefetchScalarGridSpec(
            num_scalar_prefetch=1, grid=(4, nrow),
            in_specs=[pl.BlockSpec((tm, D_MODEL), lambda p, i, order: (jnp.where(p == 0, i, nrow - 1), 0)),
                      pl.BlockSpec((1, D_MODEL), lambda p, i, order: (0, 0)), any_spec, any_spec],
            out_specs=[pl.BlockSpec((tm, D_MODEL), lambda p, i, order: (jnp.where(p == 0, i, nrow - 1), 0)),
                       pl.BlockSpec((tm, CHIP_COLS), lambda p, i, order: (i, order[p])), any_spec, any_spec],
            scratch_shapes=[pltpu.VMEM((CHIP_COLS, D_MODEL), BF16), pltpu.VMEM((nrow, tm, D_MODEL), BF16),
                            pltpu.SemaphoreType.DMA((7,)), pltpu.SemaphoreType.DMA((7,)),
                            pltpu.SemaphoreType.DMA((N_DEV - 1,)), pltpu.SemaphoreType.DMA((N_DEV - 1,)),
                            pltpu.SemaphoreType.DMA((2,)), pltpu.SemaphoreType.DMA((2,))]),
        out_shape=[SDS((T, D_MODEL), BF16), SDS((T, IN_WIDTH), F32),
                   SDS((N_DEV,) + w_block_t.shape, w_block_t.dtype), SDS((N_DEV,) + conv_block.shape, conv_block.dtype)],
        compiler_params=_params("arbitrary", "arbitrary"), name="gather_in_proj")(chip_order, x, g1, w_block_t, conv_block)


def _mm_tn(a, b, out_dtype, name, tn, tt=1024):
    T, K = a.shape
    N = b.shape[1]
    nt = T // tt

    def body(a_ref, b_ref, o_ref, acc):
        t = pl.program_id(1)

        @pl.when(t == 0)
        def _():
            acc[...] = jnp.zeros_like(acc)

        acc[...] += _dot_tn(a_ref[...], b_ref[...])

        @pl.when(t == nt - 1)
        def _():
            o_ref[...] = acc[...].astype(o_ref.dtype)

    return pl.pallas_call(
        body, grid=(N // tn, nt),
        in_specs=[pl.BlockSpec((tt, K), lambda j, t: (t, 0)),
                  pl.BlockSpec((tt, tn), lambda j, t: (t, j))],
        out_specs=pl.BlockSpec((K, tn), lambda j, t: (0, j)),
        out_shape=SDS((K, N), out_dtype),
        scratch_shapes=[pltpu.VMEM((K, tn), F32)],
        compiler_params=_params("parallel", "arbitrary"), name=name)(a, b)


def _gather_classes(src_ref, dst, r, row0=0):
    L = SEQ // r
    for c in range(r):
        dst[row0 + c * L:row0 + (c + 1) * L, :] = src_ref[0, pl.ds(c, L, stride=r), :].astype(dst.dtype)


def _scatter_classes(src, dst, r, row0=0):
    L = SEQ // r
    for c in range(r):
        dst[pl.ds(c, L, stride=r), :] = src[row0 + c * L:row0 + (c + 1) * L, :].astype(dst.dtype)


def _attn_masks(slope_r):
    qi = lax.broadcasted_iota(jnp.int32, (Q_BLOCK, Q_BLOCK), 0)
    kj = lax.broadcasted_iota(jnp.int32, (Q_BLOCK, Q_BLOCK), 1)
    rel = (qi - kj).astype(F32)
    bias_cur = jnp.where(qi >= kj, -slope_r * rel, NEG)
    bias_prev = jnp.where(qi <= kj, -slope_r * (rel + float(Q_BLOCK)), NEG)
    return bias_cur, bias_prev


def _store_biases(bias, sl_ref, g, hp):
    for hh in range(2):
        cur, prev = _attn_masks(sl_ref[g * HEADS_PER_GROUP + 2 * hp + hh])
        rows = slice(hh * Q_BLOCK, (hh + 1) * Q_BLOCK)
        bias[0, rows, 0:Q_BLOCK] = prev
        bias[1, rows, 0:Q_BLOCK] = jnp.full((Q_BLOCK, Q_BLOCK), NEG, F32)
        bias[0, rows, Q_BLOCK:] = cur
        bias[1, rows, Q_BLOCK:] = cur


def _transpose_pairs(src, dst):
    dst[0, :, 0:Q_BLOCK] = jnp.zeros((LANES, Q_BLOCK), dst.dtype)
    nblk = SEQ // Q_BLOCK
    for b in range(nblk):
        t = src[(b + 1) * Q_BLOCK:(b + 2) * Q_BLOCK, :].T
        dst[b, :, Q_BLOCK:] = t
        if b + 1 < nblk:
            dst[b + 1, :, 0:Q_BLOCK] = t


def _stack_heads(t, low):
    z = jnp.zeros_like(t)
    return jnp.concatenate([jnp.where(low, t, z), jnp.where(low, z, t)], axis=0)


def _unstack_heads(t2, low):
    return jnp.where(low, t2[:Q_BLOCK], t2[Q_BLOCK:])


def _unit_offsets(u, nb):
    off = pl.multiple_of(u * Q_BLOCK, Q_BLOCK)
    n = u & (nb - 1)
    c = u >> int(math.log2(nb))
    return off, n == 0, c, n


ATTN_UNROLL = 4


def _attn_fwd(qkv, slopes_r, batch, dep=None):
    nblk = SEQ // Q_BLOCK

    def body(sl_ref, *refs):
        qkv_refs = refs[:9]
        att_ref, lse_ref = refs[9:11]
        qd, kd, vd, kt, opos, lpos, bias = refs[11:]
        hp = pl.program_id(1)
        low = lax.broadcasted_iota(jnp.int32, (Q_BLOCK, LANES), 1) < HEAD_DIM

        for g in range(3):
            r = GROUPS[g][1]
            nb = SEQ // r // Q_BLOCK
            _gather_classes(qkv_refs[3 * g], qd, r)
            kd[0:Q_BLOCK, :] = jnp.zeros((Q_BLOCK, LANES), BF16)
            vd[0:Q_BLOCK, :] = jnp.zeros((Q_BLOCK, LANES), BF16)
            _gather_classes(qkv_refs[3 * g + 1], kd, r, Q_BLOCK)
            _gather_classes(qkv_refs[3 * g + 2], vd, r, Q_BLOCK)
            _transpose_pairs(kd, kt)
            _store_biases(bias, sl_ref, g, hp)

            def unit(u, carry, g=g, r=r, nb=nb):
                off, first, c, n = _unit_offsets(u, nb)
                q2 = _stack_heads(qd[pl.ds(off, Q_BLOCK), :], low)
                s = _dot(q2, kt[u]) * 0.125 + bias[first.astype(jnp.int32)]
                m = jnp.max(s, axis=-1, keepdims=True)
                p = jnp.exp(s - m)
                l = jnp.sum(p, axis=-1, keepdims=True)
                o2 = _dot(p.astype(BF16), vd[pl.ds(off, 2 * Q_BLOCK), :]) * (1.0 / l)
                lse2 = m + jnp.log(l)
                rows = pl.ds(c + n * (Q_BLOCK * r), Q_BLOCK, stride=r)
                opos[g, rows, :] = _unstack_heads(o2, low)
                lpos[g, rows, :] = jnp.where(low, lse2[:Q_BLOCK], lse2[Q_BLOCK:])
                return carry

            lax.fori_loop(0, nblk, unit, 0, unroll=ATTN_UNROLL)

        def merge(i, carry):
            rows = pl.ds(pl.multiple_of(i * 256, 256), 256)
            l0, l1, l2 = lpos[0, rows, :], lpos[1, rows, :], lpos[2, rows, :]
            m = jnp.maximum(jnp.maximum(l0, l1), l2)
            e0, e1, e2 = jnp.exp(l0 - m), jnp.exp(l1 - m), jnp.exp(l2 - m)
            den = e0 + e1 + e2
            att = (e0 * opos[0, rows, :] + e1 * opos[1, rows, :] + e2 * opos[2, rows, :]) / den
            att_ref[0, rows, :] = att.astype(att_ref.dtype)
            lse_ref[0, rows, :] = m + jnp.log(den)
            return carry

        lax.fori_loop(0, SEQ // 256, merge, 0)

    def col(sec, g):
        return pl.BlockSpec((1, SEQ, LANES), lambda b, hp: (b, 0, sec * 12 + g * 4 + hp))

    out = pl.BlockSpec((1, SEQ, LANES), lambda b, hp: (b, 0, hp))
    body, dep_spec, dep_arg = _anchored(body, 10, dep)
    return pl.pallas_call(
        body, grid=(batch, 4),
        in_specs=[pl.BlockSpec(memory_space=pltpu.SMEM)] + [col(sec, g) for g in range(3) for sec in range(3)] + dep_spec,
        out_specs=[out, out],
        out_shape=[SDS((batch, SEQ, ATTN_OUT), BF16), SDS((batch, SEQ, ATTN_OUT), F32)],
        scratch_shapes=[pltpu.VMEM((SEQ, LANES), BF16), pltpu.VMEM((Q_BLOCK + SEQ, LANES), BF16),
                        pltpu.VMEM((Q_BLOCK + SEQ, LANES), BF16), pltpu.VMEM((nblk, LANES, 2 * Q_BLOCK), BF16),
                        pltpu.VMEM((3, SEQ, LANES), F32), pltpu.VMEM((3, SEQ, LANES), F32),
                        pltpu.VMEM((2, 2 * Q_BLOCK, 2 * Q_BLOCK), F32)],
        compiler_params=_params("parallel", "parallel"), name="attn_fwd")(slopes_r, *([qkv] * 9), *dep_arg)


def _attn_bwd(qkv, datt, lse, dsum, slopes_r, batch):
    nblk = SEQ // Q_BLOCK

    def body(sl_ref, q_ref, k_ref, v_ref, do_ref, l_ref, d_ref, dq_ref, dk_ref, dv_ref,
             qd, kd, vd, dod, kt, vt, ld, dd, dq_acc, dk_acc, dv_acc, dk_part, dv_part, stage, bias):
        gid, hp = pl.program_id(1), pl.program_id(2)
        low = lax.broadcasted_iota(jnp.int32, (Q_BLOCK, LANES), 1) < HEAD_DIM

        def section(g):
            r = GROUPS[g][1]
            nb = SEQ // r // Q_BLOCK
            _gather_classes(q_ref, qd, r)
            kd[0:Q_BLOCK, :] = jnp.zeros((Q_BLOCK, LANES), BF16)
            vd[0:Q_BLOCK, :] = jnp.zeros((Q_BLOCK, LANES), BF16)
            _gather_classes(k_ref, kd, r, Q_BLOCK)
            _gather_classes(v_ref, vd, r, Q_BLOCK)
            _gather_classes(do_ref, dod, r)
            _gather_classes(l_ref, ld, r)
            _gather_classes(d_ref, dd, r)
            _transpose_pairs(kd, kt)
            _transpose_pairs(vd, vt)
            _store_biases(bias, sl_ref, g, hp)

            def unit(u, carry):
                off, first, _, _ = _unit_offsets(u, nb)
                pair = pl.ds(off, 2 * Q_BLOCK)
                q2 = _stack_heads(qd[pl.ds(off, Q_BLOCK), :], low)
                do2 = _stack_heads(dod[pl.ds(off, Q_BLOCK), :], low)
                lse_t = ld[pl.ds(off, Q_BLOCK), :]
                dsum_t = dd[pl.ds(off, Q_BLOCK), :]
                lse2 = jnp.concatenate([lse_t[:, 0:1], lse_t[:, HEAD_DIM:HEAD_DIM + 1]], axis=0)
                dsum2 = jnp.concatenate([dsum_t[:, 0:1], dsum_t[:, HEAD_DIM:HEAD_DIM + 1]], axis=0)
                s = _dot(q2, kt[u]) * 0.125 + bias[first.astype(jnp.int32)]
                p = jnp.exp(s - lse2)
                ds = (p * (_dot(do2, vt[u]) - dsum2)).astype(BF16)
                dq_acc[pl.ds(off, Q_BLOCK), :] = _unstack_heads(_dot(ds, kd[pair, :]), low) * 0.125
                dk_part[u] = _dot_tn(ds, q2) * 0.125
                dv_part[u] = _dot_tn(p.astype(BF16), do2)
                return carry

            lax.fori_loop(0, nblk, unit, 0, unroll=ATTN_UNROLL)
            for part, acc in ((dk_part, dk_acc), (dv_part, dv_acc)):
                for b in range(nblk):
                    t = part[b, Q_BLOCK:, :]
                    if b + 1 < nblk:
                        t = t + part[b + 1, 0:Q_BLOCK, :]
                    acc[b * Q_BLOCK:(b + 1) * Q_BLOCK, :] = t
            for acc, out_ref in ((dq_acc, dq_ref), (dk_acc, dk_ref), (dv_acc, dv_ref)):
                _scatter_classes(acc, stage, r)
                out_ref[0] = stage[...].astype(out_ref.dtype)

        for g in range(3):
            pl.when(gid == g)(lambda g=g: section(g))

    def col(sec):
        return pl.BlockSpec((1, SEQ, LANES), lambda b, g, hp: (b, 0, sec * 12 + g * 4 + hp))

    pos = pl.BlockSpec((1, SEQ, LANES), lambda b, g, hp: (b, 0, hp))
    dout = pl.BlockSpec((1, SEQ, LANES), lambda b, g, hp: (b, 0, g * 4 + hp))
    out = SDS((batch, SEQ, ATTN_WIDTH), BF16)
    seq_bf = pltpu.VMEM((SEQ, LANES), BF16)
    seq_f = pltpu.VMEM((SEQ, LANES), F32)
    pad_bf = pltpu.VMEM((Q_BLOCK + SEQ, LANES), BF16)
    part = pltpu.VMEM((nblk, 2 * Q_BLOCK, LANES), F32)
    blk_t = pltpu.VMEM((nblk, LANES, 2 * Q_BLOCK), BF16)
    return pl.pallas_call(
        body, grid=(batch, 3, 4),
        in_specs=[pl.BlockSpec(memory_space=pltpu.SMEM), col(0), col(1), col(2), pos, pos, pos],
        out_specs=[dout, dout, dout],
        out_shape=[out, out, out],
        scratch_shapes=[seq_bf, pad_bf, pad_bf, seq_bf, blk_t, blk_t, seq_f, seq_f, seq_f, seq_f, seq_f, part, part, seq_f,
                        pltpu.VMEM((2, 2 * Q_BLOCK, 2 * Q_BLOCK), F32)],
        compiler_params=_params("parallel", "parallel", "parallel"), name="attn_bwd")(
            slopes_r, qkv, qkv, qkv, datt, lse, dsum)


CONV_TC = 128
U_BLOCK0 = 3 * ATTN_WIDTH // CONV_TC
CONV_ROWS = 128
SUBLANES = 8


def _fill_shifted(sh):
    n = SEQ + CONV_PAD - SUBLANES
    for s in range(1, SUBLANES):
        sh[s, 0:n, :] = sh[0, s:s + n, :]


def _tap(sh, base, offset):
    s = offset % SUBLANES
    return sh[s, pl.ds(pl.multiple_of(base + (offset - s), SUBLANES), CONV_ROWS), :]


def _conv_fwd(u, conv_w, conv_b, batch, dep=None):
    nct = D_MODEL // CONV_TC

    def body(ua_ref, ub_ref, w_ref, b_ref, o_ref, sh):
        sh[0, 0:CONV_PAD, :] = jnp.zeros((CONV_PAD, CONV_TC), F32)
        sh[0, CONV_PAD:, :] = ua_ref[0] * _sigmoid(ub_ref[0])
        _fill_shifted(sh)

        def chunk(c, carry):
            base = pl.multiple_of(c * CONV_ROWS, CONV_ROWS)
            acc = jnp.broadcast_to(b_ref[...], (CONV_ROWS, CONV_TC))
            for t in range(CONV_K):
                acc = acc + _tap(sh, base, t + CONV_PAD - (CONV_K - 1)) * w_ref[t:t + 1, :]
            o_ref[0, pl.ds(base, CONV_ROWS), :] = acc
            return carry

        lax.fori_loop(0, SEQ // CONV_ROWS, chunk, 0)

    body, dep_spec, dep_arg = _anchored(body, 4, dep)
    return pl.pallas_call(
        body, grid=(nct, batch),
        in_specs=[pl.BlockSpec((1, SEQ, CONV_TC), lambda j, b: (b, 0, U_BLOCK0 + j)),
                  pl.BlockSpec((1, SEQ, CONV_TC), lambda j, b: (b, 0, U_BLOCK0 + nct + j)),
                  pl.BlockSpec((CONV_PAD, CONV_TC), lambda j, b: (0, j)),
                  pl.BlockSpec((1, CONV_TC), lambda j, b: (0, j))] + dep_spec,
        out_specs=pl.BlockSpec((1, SEQ, CONV_TC), lambda j, b: (b, 0, j)),
        out_shape=SDS((batch, SEQ, D_MODEL), F32),
        scratch_shapes=[pltpu.VMEM((SUBLANES, SEQ + CONV_PAD, CONV_TC), F32)],
        compiler_params=_params("parallel", "parallel"), name="conv_fwd")(u, u, conv_w, conv_b, *dep_arg)


def _conv_bwd(u, dc1, conv_w, batch, dep=None):
    nct = D_MODEL // CONV_TC
    nchunk = SEQ // CONV_ROWS

    def body(ua_ref, ub_ref, d_ref, w_ref, dua_ref, dub_ref, gw_ref, gb_ref, shc, shd, gacc):
        b = pl.program_id(1)
        shc[0, 0:CONV_PAD, :] = jnp.zeros((CONV_PAD, CONV_TC), F32)
        shc[0, CONV_PAD:, :] = ua_ref[0] * _sigmoid(ub_ref[0])
        _fill_shifted(shc)
        shd[0, 0:SEQ, :] = d_ref[0]
        shd[0, SEQ:, :] = jnp.zeros((CONV_PAD, CONV_TC), F32)
        _fill_shifted(shd)

        @pl.when(b == 0)
        def _():
            gacc[...] = jnp.zeros_like(gacc)
            gb_ref[...] = jnp.zeros_like(gb_ref)

        gb_ref[...] += _rowsum(d_ref[0])

        def chunk(c, carry):
            base = pl.multiple_of(c * CONV_ROWS, CONV_ROWS)
            dcur = shd[0, pl.ds(base, CONV_ROWS), :]
            acc = jnp.zeros((CONV_ROWS, CONV_TC), F32)
            for t in range(CONV_K):
                acc = acc + _tap(shd, base, CONV_K - 1 - t) * w_ref[t:t + 1, :]
                prod = _tap(shc, base, t + CONV_PAD - (CONV_K - 1)) * dcur
                gacc[t] += jnp.sum(prod.reshape(CONV_ROWS // 8, 8, CONV_TC), axis=0)
            ua = ua_ref[0, pl.ds(base, CONV_ROWS), :]
            sg = _sigmoid(ub_ref[0, pl.ds(base, CONV_ROWS), :])
            dua_ref[0, pl.ds(base, CONV_ROWS), :] = (acc * sg).astype(dua_ref.dtype)
            dub_ref[0, pl.ds(base, CONV_ROWS), :] = (acc * ua * sg * (1.0 - sg)).astype(dub_ref.dtype)
            return carry

        lax.fori_loop(0, nchunk, chunk, 0)

        @pl.when(b == batch - 1)
        def _():
            for t in range(CONV_K):
                gw_ref[t:t + 1, :] = jnp.sum(gacc[t], axis=0, keepdims=True)
            gw_ref[CONV_K:CONV_PAD, :] = jnp.zeros((CONV_PAD - CONV_K, CONV_TC), F32)

    du = SDS((batch, SEQ, D_MODEL), BF16)
    body, dep_spec, dep_arg = _anchored(body, 4, dep)
    return pl.pallas_call(
        body, grid=(nct, batch),
        in_specs=[pl.BlockSpec((1, SEQ, CONV_TC), lambda j, b: (b, 0, U_BLOCK0 + j)),
                  pl.BlockSpec((1, SEQ, CONV_TC), lambda j, b: (b, 0, U_BLOCK0 + nct + j)),
                  pl.BlockSpec((1, SEQ, CONV_TC), lambda j, b: (b, 0, j)),
                  pl.BlockSpec((CONV_PAD, CONV_TC), lambda j, b: (0, j))] + dep_spec,
        out_specs=[pl.BlockSpec((1, SEQ, CONV_TC), lambda j, b: (b, 0, j)),
                   pl.BlockSpec((1, SEQ, CONV_TC), lambda j, b: (b, 0, j)),
                   pl.BlockSpec((CONV_PAD, CONV_TC), lambda j, b: (0, j)),
                   pl.BlockSpec((1, CONV_TC), lambda j, b: (0, j))],
        out_shape=[du, du, SDS((CONV_PAD, D_MODEL), F32), SDS((1, D_MODEL), F32)],
        scratch_shapes=[pltpu.VMEM((SUBLANES, SEQ + CONV_PAD, CONV_TC), F32),
                        pltpu.VMEM((SUBLANES, SEQ + CONV_PAD, CONV_TC), F32),
                        pltpu.VMEM((CONV_K, 8, CONV_TC), F32)],
        compiler_params=_params("parallel", "arbitrary"), name="conv_bwd")(u, u, dc1, conv_w, *dep_arg)


MID_TM = 256


def _layernorm_stats(c1):
    mu = jnp.mean(c1, axis=-1, keepdims=True)
    cen = c1 - mu
    rs = lax.rsqrt(jnp.mean(cen * cen, axis=-1, keepdims=True) + LN_EPS)
    return cen * rs, rs


GATE_PARTS = 4
GATE_PART = 2 * D_MODEL // GATE_PARTS
GATE_PART0 = (IN_WIDTH - 2 * D_MODEL) // GATE_PART


def _gate_specs(tm):
    return [pl.BlockSpec((tm, GATE_PART), lambda i, k=k: (i, GATE_PART0 + k)) for k in range(GATE_PARTS)]


def _mid_fwd(att, c1, proj, x, w_a, w_c, w_o, gate_b, ln_g, ln_b, g2, dep=None):
    T = x.shape[0]
    tm = MID_TM

    def body(att_ref, c1_ref, lg0, lg1, lg2, lg3, x_ref, wa_ref, wc_ref, wo_ref, gb_ref, lng_ref, lnb_ref, g2_ref,
             c3_ref, ya_ref, yc_ref, mix_ref, x1_ref, h2_ref):
        logits = jnp.concatenate([lg0[...], lg1[...], lg2[...], lg3[...]], axis=1)
        ya = _dot(att_ref[...], wa_ref[...])
        xh, _ = _layernorm_stats(c1_ref[...])
        c2 = xh * lng_ref[...] + lnb_ref[...]
        c3 = (c2 * _sigmoid(c2)).astype(BF16)
        c3_ref[...] = c3
        yc = _dot(c3, wc_ref[...])
        gates = _sigmoid(logits + gb_ref[...])
        mix = (gates[:, :D_MODEL] * ya + gates[:, D_MODEL:] * yc).astype(BF16)
        ya_ref[...] = ya.astype(BF16)
        yc_ref[...] = yc.astype(BF16)
        mix_ref[...] = mix
        x1 = x_ref[...] + _dot(mix, wo_ref[...])
        x1_ref[...] = x1
        r = lax.rsqrt(jnp.mean(x1 * x1, axis=-1, keepdims=True) + RMS_EPS)
        h2_ref[...] = (x1 * r * g2_ref[...]).astype(BF16)

    row = lambda n: pl.BlockSpec((tm, n), lambda i: (i, 0))
    full = lambda a, b: pl.BlockSpec((a, b), lambda i: (0, 0))
    body, dep_spec, dep_arg = _anchored(body, 10 + GATE_PARTS, dep)
    return pl.pallas_call(
        body, grid=(T // tm,),
        in_specs=[row(ATTN_OUT), row(D_MODEL)] + _gate_specs(tm) + [row(D_MODEL),
                  full(ATTN_OUT, D_MODEL), full(D_MODEL, D_MODEL), full(D_MODEL, D_MODEL),
                  full(1, 2 * D_MODEL), full(1, D_MODEL), full(1, D_MODEL), full(1, D_MODEL)] + dep_spec,
        out_specs=[row(D_MODEL), row(D_MODEL), row(D_MODEL), row(D_MODEL), row(D_MODEL), row(D_MODEL)],
        out_shape=[SDS((T, D_MODEL), BF16), SDS((T, D_MODEL), BF16), SDS((T, D_MODEL), BF16), SDS((T, D_MODEL), BF16),
                   SDS((T, D_MODEL), F32), SDS((T, D_MODEL), BF16)],
        compiler_params=_params("parallel"), name="mid_fwd")(att, c1, *([proj] * GATE_PARTS), x, w_a, w_c, w_o,
                                                             gate_b, ln_g, ln_b, g2, *dep_arg)


def _mid_bwd(dx1b, ya, yc, proj, att, c1, w_a, w_c, w_o, gate_b, ln_g, ln_b, head_ones, dep=None):
    T = dx1b.shape[0]
    tm = MID_TM

    def body(dx_ref, ya_ref, yc_ref, lg0, lg1, lg2, lg3, att_ref, c1_ref, wa_ref, wc_ref, wo_ref, gb_ref, lng_ref,
             lnb_ref, e_ref, dlg_ref, dya_ref, dyc_ref, datt_ref, dsum_ref, dc1_ref, ggb_ref, glg_ref, glb_ref):
        logits = jnp.concatenate([lg0[...], lg1[...], lg2[...], lg3[...]], axis=1)
        @pl.when(pl.program_id(0) == 0)
        def _():
            ggb_ref[...] = jnp.zeros_like(ggb_ref)
            glg_ref[...] = jnp.zeros_like(glg_ref)
            glb_ref[...] = jnp.zeros_like(glb_ref)

        dmix = _dot_nt(dx_ref[...], wo_ref[...])
        gates = _sigmoid(logits + gb_ref[...])
        ga, gc = gates[:, :D_MODEL], gates[:, D_MODEL:]
        dla = dmix * ya_ref[...].astype(F32) * ga * (1.0 - ga)
        dlc = dmix * yc_ref[...].astype(F32) * gc * (1.0 - gc)
        dlg_ref[:, :D_MODEL] = dla.astype(BF16)
        dlg_ref[:, D_MODEL:] = dlc.astype(BF16)
        ggb_ref[:, :D_MODEL] += _rowsum(dla)
        ggb_ref[:, D_MODEL:] += _rowsum(dlc)
        dya = (dmix * ga).astype(BF16)
        dyc = (dmix * gc).astype(BF16)
        dya_ref[...] = dya
        dyc_ref[...] = dyc
        datt = _dot_nt(dya, wa_ref[...])
        datt_ref[...] = datt
        dsum_ref[...] = jnp.dot(datt * att_ref[...].astype(F32), e_ref[...], preferred_element_type=F32,
                                precision=lax.Precision.HIGHEST)
        dc3 = _dot_nt(dyc, wc_ref[...])
        xh, rs = _layernorm_stats(c1_ref[...])
        c2 = xh * lng_ref[...] + lnb_ref[...]
        sg = _sigmoid(c2)
        dc2 = dc3 * (sg * (1.0 + c2 * (1.0 - sg)))
        glg_ref[...] += _rowsum(dc2 * xh)
        glb_ref[...] += _rowsum(dc2)
        dxh = dc2 * lng_ref[...]
        dc1_ref[...] = rs * (dxh - jnp.mean(dxh, axis=-1, keepdims=True) - xh * jnp.mean(dxh * xh, axis=-1, keepdims=True))

    row = lambda n: pl.BlockSpec((tm, n), lambda i: (i, 0))
    full = lambda a, b: pl.BlockSpec((a, b), lambda i: (0, 0))
    body, dep_spec, dep_arg = _anchored(body, 12 + GATE_PARTS, dep)
    return pl.pallas_call(
        body, grid=(T // tm,),
        in_specs=[row(D_MODEL), row(D_MODEL), row(D_MODEL)] + _gate_specs(tm) + [row(ATTN_OUT), row(D_MODEL),
                  full(ATTN_OUT, D_MODEL), full(D_MODEL, D_MODEL), full(D_MODEL, D_MODEL),
                  full(1, 2 * D_MODEL), full(1, D_MODEL), full(1, D_MODEL), full(ATTN_OUT, ATTN_OUT)] + dep_spec,
        out_specs=[row(2 * D_MODEL), row(D_MODEL), row(D_MODEL), row(ATTN_OUT), row(ATTN_OUT), row(D_MODEL),
                   full(1, 2 * D_MODEL), full(1, D_MODEL), full(1, D_MODEL)],
        out_shape=[SDS((T, 2 * D_MODEL), BF16), SDS((T, D_MODEL), BF16), SDS((T, D_MODEL), BF16), SDS((T, ATTN_OUT), F32),
                   SDS((T, ATTN_OUT), F32), SDS((T, D_MODEL), F32),
                   SDS((1, 2 * D_MODEL), F32), SDS((1, D_MODEL), F32), SDS((1, D_MODEL), F32)],
        compiler_params=_params("arbitrary"), name="mid_bwd")(dx1b, ya, yc, *([proj] * GATE_PARTS), att, c1, w_a, w_c, w_o,
                                                               gate_b, ln_g, ln_b, head_ones, *dep_arg)


FFN_TM = 512
FFN_TF = D_FF // 2
FFN_SUB = ((0, 512), (512, 1024), (1024, FFN_TF))


def _rms_bwd(dy_times_g, xh, r):
    return r * (dy_times_g - xh * jnp.mean(dy_times_g * xh, axis=-1, keepdims=True))


def _ffn_fwd(h2, x1, target, gf, w_g_t, w_u_t, w_d):
    T = h2.shape[0]
    tm, tf = FFN_TM, FFN_TF
    nf = D_FF // tf

    def body(h_ref, x1_ref, t_ref, gf_ref, wg_ref, wu_ref, wd_ref,
             a_ref, b_ref, f_ref, dx2_ref, dx2b_ref, loss_ref, gnf_ref, acc):
        i, j = pl.program_id(0), pl.program_id(1)
        h = h_ref[...]

        @pl.when(j == 0)
        def _():
            acc[...] = x1_ref[...]

        for lo, hi in FFN_SUB:
            a = _dot_nt(h, wg_ref[lo:hi, :])
            b = _dot_nt(h, wu_ref[lo:hi, :])
            f = (a * _sigmoid(a) * b).astype(BF16)
            a_ref[:, lo:hi] = a.astype(BF16)
            b_ref[:, lo:hi] = b.astype(BF16)
            f_ref[:, lo:hi] = f
            acc[...] += _dot(f, wd_ref[lo:hi, :])

        @pl.when((i == 0) & (j == nf - 1))
        def _():
            loss_ref[...] = jnp.zeros_like(loss_ref)
            gnf_ref[...] = jnp.zeros_like(gnf_ref)

        @pl.when(j == nf - 1)
        def _():
            x2 = acc[...]
            r = lax.rsqrt(jnp.mean(x2 * x2, axis=-1, keepdims=True) + RMS_EPS)
            xh = x2 * r
            err = xh * gf_ref[...] - t_ref[...]
            loss_ref[...] += (0.5 / D_MODEL) * jnp.sum(err * err)
            dy = err * (1.0 / D_MODEL)
            gnf_ref[...] += _rowsum(dy * xh)
            dx2 = _rms_bwd(dy * gf_ref[...], xh, r)
            dx2_ref[...] = dx2
            dx2b_ref[...] = dx2.astype(BF16)

    row = lambda n: pl.BlockSpec((tm, n), lambda i, j: (i, 0))
    ffb = pl.BlockSpec((tm, tf), lambda i, j: (i, j))
    wblk = pl.BlockSpec((tf, D_MODEL), lambda i, j: (j, 0))
    return pl.pallas_call(
        body, grid=(T // tm, nf),
        in_specs=[row(D_MODEL), row(D_MODEL), row(D_MODEL), pl.BlockSpec((1, D_MODEL), lambda i, j: (0, 0)),
                  wblk, wblk, wblk],
        out_specs=[ffb, ffb, ffb, row(D_MODEL), row(D_MODEL),
                   pl.BlockSpec((1, 128), lambda i, j: (0, 0)), pl.BlockSpec((1, D_MODEL), lambda i, j: (0, 0))],
        out_shape=[SDS((T, D_FF), BF16), SDS((T, D_FF), BF16), SDS((T, D_FF), BF16), SDS((T, D_MODEL), F32),
                   SDS((T, D_MODEL), BF16), SDS((1, 128), F32), SDS((1, D_MODEL), F32)],
        scratch_shapes=[pltpu.VMEM((tm, D_MODEL), F32)],
        compiler_params=_params("arbitrary", "arbitrary"), name="ffn_fwd")(h2, x1, target, gf, w_g_t, w_u_t, w_d)


def _ffn_bwd(dx2b, dx2, a, b, x1, g2, w_g_t, w_u_t, w_d):
    T = dx2.shape[0]
    tm, tf = FFN_TM, FFN_TF
    nf = D_FF // tf

    def body(dxb_ref, dx2_ref, a_ref, b_ref, x1_ref, g2_ref, wg_ref, wu_ref, wd_ref,
             da_ref, db_ref, dx1_ref, dx1b_ref, gn2_ref, acc):
        i, j = pl.program_id(0), pl.program_id(1)
        @pl.when(j == 0)
        def _():
            acc[...] = jnp.zeros_like(acc)

        dxb = dxb_ref[...]
        for lo, hi in FFN_SUB:
            df = _dot_nt(dxb, wd_ref[lo:hi, :])
            av = a_ref[:, lo:hi].astype(F32)
            bv = b_ref[:, lo:hi].astype(F32)
            sg = _sigmoid(av)
            db = (df * av * sg).astype(BF16)
            da = (df * bv * (sg * (1.0 + av * (1.0 - sg)))).astype(BF16)
            da_ref[:, lo:hi] = da
            db_ref[:, lo:hi] = db
            acc[...] += _dot(da, wg_ref[lo:hi, :]) + _dot(db, wu_ref[lo:hi, :])

        @pl.when((i == 0) & (j == nf - 1))
        def _():
            gn2_ref[...] = jnp.zeros_like(gn2_ref)

        @pl.when(j == nf - 1)
        def _():
            dh2 = acc[...]
            x1 = x1_ref[...]
            r = lax.rsqrt(jnp.mean(x1 * x1, axis=-1, keepdims=True) + RMS_EPS)
            xh = x1 * r
            gn2_ref[...] += _rowsum(dh2 * xh)
            dx1 = dx2_ref[...] + _rms_bwd(dh2 * g2_ref[...], xh, r)
            dx1_ref[...] = dx1
            dx1b_ref[...] = dx1.astype(BF16)

    row = lambda n: pl.BlockSpec((tm, n), lambda i, j: (i, 0))
    ffb = pl.BlockSpec((tm, tf), lambda i, j: (i, j))
    wblk = pl.BlockSpec((tf, D_MODEL), lambda i, j: (j, 0))
    return pl.pallas_call(
        body, grid=(T // tm, nf),
        in_specs=[row(D_MODEL), row(D_MODEL), ffb, ffb, row(D_MODEL), pl.BlockSpec((1, D_MODEL), lambda i, j: (0, 0)),
                  wblk, wblk, wblk],
        out_specs=[ffb, ffb, row(D_MODEL), row(D_MODEL), pl.BlockSpec((1, D_MODEL), lambda i, j: (0, 0))],
        out_shape=[SDS((T, D_FF), BF16), SDS((T, D_FF), BF16), SDS((T, D_MODEL), F32), SDS((T, D_MODEL), BF16),
                   SDS((1, D_MODEL), F32)],
        scratch_shapes=[pltpu.VMEM((tm, D_MODEL), F32)],
        compiler_params=_params("arbitrary", "arbitrary"), name="ffn_bwd")(dx2b, dx2, a, b, x1, g2, w_g_t, w_u_t, w_d)


def _in_bwd(pieces, w_in_t, x, dx1, g1, dep=None):
    T = x.shape[0]
    tm = IN_TM
    npc = len(pieces)
    assert sum(p.shape[1] for p in pieces) == IN_WIDTH

    def body(*refs):
        p_refs = refs[:npc]
        w_hbm, x_ref, dx1_ref, g_ref, dx_ref, gn1_ref, w_vmem, sem = refs[npc:]

        @pl.when(pl.program_id(0) == 0)
        def _():
            cp = pltpu.make_async_copy(w_hbm, w_vmem, sem)
            cp.start()
            cp.wait()
            gn1_ref[...] = jnp.zeros_like(gn1_ref)

        dh = jnp.zeros((tm, D_MODEL), F32)
        col = 0
        for p_ref in p_refs:
            for j in range(p_ref.shape[1] // IN_CHUNK):
                dh = dh + _dot(p_ref[:, j * IN_CHUNK:(j + 1) * IN_CHUNK], w_vmem[col:col + IN_CHUNK, :])
                col += IN_CHUNK
        xv = x_ref[...]
        r = lax.rsqrt(jnp.mean(xv * xv, axis=-1, keepdims=True) + RMS_EPS)
        xh = xv * r
        gn1_ref[...] += _rowsum(dh * xh)
        dx_ref[...] = dx1_ref[...] + _rms_bwd(dh * g_ref[...], xh, r)

    row = lambda n: pl.BlockSpec((tm, n), lambda i: (i, 0))
    body, dep_spec, dep_arg = _anchored(body, npc + 4, dep)
    return pl.pallas_call(
        body, grid=(T // tm,),
        in_specs=[row(p.shape[1]) for p in pieces]
        + [pl.BlockSpec(memory_space=pl.ANY), row(D_MODEL), row(D_MODEL), pl.BlockSpec((1, D_MODEL), lambda i: (0, 0))]
        + dep_spec,
        out_specs=[row(D_MODEL), pl.BlockSpec((1, D_MODEL), lambda i: (0, 0))],
        out_shape=[SDS((T, D_MODEL), F32), SDS((1, D_MODEL), F32)],
        scratch_shapes=[pltpu.VMEM((IN_WIDTH, D_MODEL), BF16), pltpu.SemaphoreType.DMA],
        compiler_params=_params("arbitrary"), name="in_bwd")(*pieces, w_in_t, x, dx1, g1, *dep_arg)


def _local_step(x, target, in_proj, small, late_weights=None, emit=None):
    T = x.shape[0]
    batch = T // SEQ
    slopes_r = jnp.asarray(_slopes_times_dilation())
    emit = emit or (lambda names, grads: None)

    h, proj, w = in_proj()
    proj3 = proj.reshape(batch, SEQ, IN_WIDTH)

    att, lse = _attn_fwd(proj3, slopes_r, batch, w.get("token"))
    att = att.reshape(T, ATTN_OUT)
    if late_weights is not None:
        w = {**w, **late_weights("after_attention", att)}

    c1 = _conv_fwd(proj3, w["conv_w"], small["conv_b"], batch, w.get("token")).reshape(T, D_MODEL)
    if late_weights is not None:
        w = {**w, **late_weights(LATE_MERGE, (att, c1))}

    c3, ya, yc, mix, x1, h2 = _mid_fwd(
        att, c1, proj, x, w["w_attn_out"], w["w_conv_out"], w["w_o"],
        small["gate_b"], small["conv_ln_g"], small["conv_ln_b"], small["norm2_g"], w.get("token"))
    if late_weights is not None:
        w = {**w, **late_weights(LATE_FFN, h2)}

    a, b, f, dx2, dx2b, loss, g_normf = _ffn_fwd(h2, x1, target, small["norm_f_g"],
                                                   w["w_ffn_gate"], w["w_ffn_up"], w["w_ffn_down"])

    da, db, dx1, dx1b, g_norm2 = _ffn_bwd(dx2b, dx2, a, b, x1, small["norm2_g"],
                                           w["w_ffn_gate"], w["w_ffn_up"], w["w_ffn_down"])
    gw = {}
    gw["w_ffn_down"] = _mm_tn(f, dx2b, BF16, "gw_ffn_down", tn=1024)
    gw["w_ffn_gate"] = _mm_tn(da, h2, BF16, "gw_ffn_gate", tn=1024)
    gw["w_ffn_up"] = _mm_tn(db, h2, BF16, "gw_ffn_up", tn=1024)
    token = emit(("w_ffn_gate", "w_ffn_up", "w_ffn_down"), gw)

    head_ones = jnp.asarray(np.kron(np.eye(HEADS_PER_GROUP, dtype=np.float32), np.ones((HEAD_DIM, HEAD_DIM), np.float32)))
    dlogits, dya, dyc, datt, dsum, dc1, g_gate_b, g_ln_g, g_ln_b = _mid_bwd(
        dx1b, ya, yc, proj, att, c1, w["w_attn_out"], w["w_conv_out"], w["w_o"],
        small["gate_b"], small["conv_ln_g"], small["conv_ln_b"], head_ones, token)
    gw["w_o"] = _mm_tn(mix, dx1b, BF16, "gw_o", tn=1024)
    gw["w_attn_out"] = _mm_tn(att, dya, BF16, "gw_attn_out", tn=1024)
    gw["w_conv_out"] = _mm_tn(c3, dyc, BF16, "gw_conv_out", tn=1024)
    token = emit(("w_conv_out", "w_attn_out", "w_o"), gw)

    dua, dub, g_conv_w, g_conv_b = _conv_bwd(proj3, dc1.reshape(batch, SEQ, D_MODEL), w["conv_w"], batch, token)

    dq, dk, dv = _attn_bwd(proj3, datt.reshape(batch, SEQ, ATTN_OUT), lse, dsum.reshape(batch, SEQ, ATTN_OUT),
                           slopes_r, batch)
    pieces = [dq.reshape(T, ATTN_WIDTH), dk.reshape(T, ATTN_WIDTH), dv.reshape(T, ATTN_WIDTH),
              dua.reshape(T, D_MODEL), dub.reshape(T, D_MODEL), dlogits]

    names = ("q", "k", "v", "ua", "ub", "gate")
    gw["w_in"] = jnp.concatenate([_mm_tn(p, h, BF16, "gw_in_" + nm, tn=1024) for nm, p in zip(names, pieces)], axis=0)
    gw["conv_w"] = g_conv_w
    token = emit(("w_in", "conv_w"), gw)
    grad_x, g_norm1 = _in_bwd(pieces, w["w_in"], x, dx1, small["norm1_g"], token)

    gsmall = {"norm1_g": g_norm1, "gate_b": g_gate_b, "conv_b": g_conv_b, "conv_ln_g": g_ln_g, "conv_ln_b": g_ln_b,
              "norm2_g": g_norm2, "norm_f_g": g_normf}
    return loss, grad_x, gw, gsmall


ANY = pl.BlockSpec(memory_space=pl.ANY)


HBM =pl.BlockSpec(memory_space=pltpu.HBM)
SEM = pl.BlockSpec(memory_space=pltpu.SEMAPHORE)
ALL_PEERS = tuple(range(1, N_DEV))
OTHER_CHIPS = (2, 4, 6)
SPLIT_EFFECT = pltpu.CompilerParams(has_side_effects=pltpu.SideEffectType.DATAFLOW_SIDE_EFFECTING)


def _exchange_copies(mode, ks, srcs, lands, send_sems, recv_sems):
    x, y, c = lax.axis_index("x"), lax.axis_index("y"), lax.axis_index("c")
    me = 4 * x + 2 * y + c
    send, recv = [], []
    for a in range(len(lands)):
        for i, k in enumerate(ks):
            peer = (x ^ ((k >> 2) & 1), y ^ ((k >> 1) & 1), c ^ (k & 1))
            pidx = 4 * peer[0] + 2 * peer[1] + peer[2]
            if mode == "gather":
                src, to, out_slot, in_slot = srcs[a], peer, me, pidx
            elif mode == "scatter":
                src, to, out_slot, in_slot = srcs[a].at[pidx], peer, me, pidx
            elif mode == "chip_scatter":
                src, to, out_slot, in_slot = srcs[a].at[pidx >> 1], peer, me >> 1, pidx >> 1
            else:
                src, to, out_slot, in_slot = lands[a].at[pidx], (x, y, 1 - c), pidx, pidx ^ 1
            s = a * len(ks) + i
            send.append(pltpu.make_async_remote_copy(
                src_ref=src, dst_ref=lands[a].at[out_slot], send_sem=send_sems.at[s], recv_sem=recv_sems.at[s],
                device_id=to, device_id_type=MESH))
            recv.append(pltpu.make_async_remote_copy(
                src_ref=src, dst_ref=lands[a].at[in_slot], send_sem=send_sems.at[s], recv_sem=recv_sems.at[s],
                device_id=to, device_id_type=MESH))
    return send, recv


def _send_start(mode, ks, name, srcs=(), lands=None, dep=None):
    srcs = list(srcs)
    if lands is None:
        slots = 4 if mode == "chip_scatter" else N_DEV
        lands = [lax.empty((slots,) + (s.shape if mode == "gather" else s.shape[1:]), s.dtype) for s in srcs]
    ns, nl = len(srcs), len(lands)
    nsem = nl * len(ks)

    def body(*refs):
        send, _ = _exchange_copies(mode, ks, refs[:ns], refs[ns:ns + nl], refs[ns + nl], refs[ns + nl + 1])
        for cp in send:
            cp.start()
        token = refs[-1]
        token[...] = jnp.zeros_like(token)

    both = srcs + list(lands)
    body, dep_spec, dep_arg = _anchored(body, ns + nl, dep)
    res = pl.pallas_call(
        body, name=name,
        out_shape=(pltpu.SemaphoreType.DMA((nsem,)), pltpu.SemaphoreType.DMA((nsem,)),
                   *[pltpu.HBM(a.shape, a.dtype) for a in both], SDS((8, 128), F32)),
        in_specs=[HBM] * (ns + nl) + dep_spec,
        out_specs=(SEM, SEM, *([HBM] * (ns + nl)), pl.BlockSpec(memory_space=pltpu.VMEM)),
        input_output_aliases={i: 2 + i for i in range(ns + nl)}, compiler_params=SPLIT_EFFECT,
    )(*[pltpu.with_memory_space_constraint(a, pltpu.HBM) for a in both], *dep_arg)
    return dict(mode=mode, ks=ks, send_sems=res[0], recv_sems=res[1], srcs=res[2:2 + ns], lands=res[2 + ns:2 + ns + nl],
                token=res[-1])


def _send_wait(started, after, name):
    ns, nl = len(started["srcs"]), len(started["lands"])

    def body(*refs):
        send, recv = _exchange_copies(started["mode"], started["ks"], refs[:ns], refs[ns:ns + nl],
                                      refs[ns + nl], refs[ns + nl + 1])
        for cp in send:
            cp.wait_send()
        for cp in recv:
            cp.wait_recv()

    both = list(started["srcs"]) + list(started["lands"])
    after = after if isinstance(after, (tuple, list)) else (after,)
    res = pl.pallas_call(
        body, name=name,
        out_shape=tuple(pltpu.HBM(a.shape, a.dtype) for a in both),
        in_specs=[HBM] * (ns + nl) + [SEM, SEM] + [ANY] * len(after), out_specs=tuple([HBM] * (ns + nl)),
        input_output_aliases={i: i for i in range(ns + nl)}, compiler_params=SPLIT_EFFECT,
    )(*both, started["send_sems"], started["recv_sems"], *after)
    return res[:ns], res[ns:]


def _exchange_sibling(gs):
    n = len(gs)

    def body(*refs):
        ins, outs = refs[:n], refs[n:2 * n]
        send_sems, recv_sems = refs[2 * n:]
        x, y, c = lax.axis_index("x"), lax.axis_index("y"), lax.axis_index("c")
        copies = []
        for a in range(n):
            for j in range(4):
                copies.append(pltpu.make_async_remote_copy(
                    src_ref=ins[a].at[2 * j + (1 - c)], dst_ref=outs[a].at[j],
                    send_sem=send_sems.at[a, j], recv_sem=recv_sems.at[a, j],
                    device_id=(x, y, 1 - c), device_id_type=MESH))
        for cp in copies:
            cp.start()
        for cp in copies:
            cp.wait_recv()
        for cp in copies:
            cp.wait_send()

    return pl.pallas_call(
        body, in_specs=[ANY] * n, out_specs=[ANY] * n,
        out_shape=[SDS((4,) + g.shape[1:], g.dtype) for g in gs],
        scratch_shapes=[pltpu.SemaphoreType.DMA((n, 4)), pltpu.SemaphoreType.DMA((n, 4))],
        name="reduce_scatter_sibling")(*gs)


def _add_pair(g, r1, core, name):
    _, rows, cols = g.shape
    tr = _row_tile(rows, cols, 3 * g.dtype.itemsize)

    def body(c_ref, g_ref, r_ref, o_ref):
        o_ref[...] = (g_ref[...].astype(F32) + r_ref[...].astype(F32)).astype(o_ref.dtype)

    return pl.pallas_call(
        body,
        grid_spec=pltpu.PrefetchScalarGridSpec(
            num_scalar_prefetch=1, grid=(4, rows // tr),
            in_specs=[pl.BlockSpec((1, tr, cols), lambda j, i, c_ref: (2 * j + c_ref[0], i, 0)),
                      pl.BlockSpec((1, tr, cols), lambda j, i, c_ref: (j, i, 0))],
            out_specs=pl.BlockSpec((1, tr, cols), lambda j, i, c_ref: (j, i, 0))),
        out_shape=SDS((4, rows, cols), g.dtype),
        compiler_params=_params("parallel", "parallel"), name=name)(core, g, r1)


def _row_tile(rows, cols, itemsize_total):
    budget = (4 << 20) // max(1, cols * itemsize_total)
    if rows <= budget:
        return rows
    t = rows
    while t > budget and t % 2 == 0 and (t // 2) % 16 == 0:
        t //= 2
    return t


def _adam_math(g, w, m, v):
    m_new = ADAM_B1 * m + (1.0 - ADAM_B1) * g
    v_new = ADAM_B2 * v + (1.0 - ADAM_B2) * (g * g)
    m_hat = m_new / (1.0 - ADAM_B1 ** ADAM_STEP)
    v_hat = v_new / (1.0 - ADAM_B2 ** ADAM_STEP)
    delta = -ADAM_LR * (m_hat / (jnp.sqrt(v_hat) + ADAM_EPS) + ADAM_WD * w)
    return delta, m_new, v_new


def _sum_adam(parts, own, mine, w, m, v, name):
    rows, cols = w.shape
    nparts = parts.shape[0]
    tr = _row_tile(rows, cols, (nparts + 1) * parts.dtype.itemsize + 7 * 4)

    def body(mine_ref, p_ref, own_ref, w_ref, m_ref, v_ref, g_ref, d_ref, mo_ref, vo_ref):
        g = None
        for s in range(nparts):
            part = jnp.where(mine_ref[0] == s, own_ref[0], p_ref[s]).astype(F32)
            g = part if g is None else g + part
        delta, m_new, v_new = _adam_math(g, w_ref[...], m_ref[...], v_ref[...])
        g_ref[...] = g
        d_ref[...] = delta
        mo_ref[...] = m_new
        vo_ref[...] = v_new

    blk = pl.BlockSpec((tr, cols), lambda i, mine_ref: (i, 0))
    out = SDS((rows, cols), F32)
    return pl.pallas_call(
        body,
        grid_spec=pltpu.PrefetchScalarGridSpec(
            num_scalar_prefetch=1, grid=(rows // tr,),
            in_specs=[pl.BlockSpec((nparts, tr, cols), lambda i, mine_ref: (0, i, 0)),
                      pl.BlockSpec((1, tr, cols), lambda i, mine_ref: (mine_ref[0], i, 0)), blk, blk, blk],
            out_specs=[blk, blk, blk, blk]),
        out_shape=[out, out, out, out],
        compiler_params=_params("parallel"), name=name)(mine, parts, own, w, m, v)


SMALL_ROWS = 72


def _small_allreduce_adam(gpart, w, m, v, row_counts, dep=None):
    def reduce_body(g_ref, go_ref, gath, send_sems, recv_sems):
        x, y, c = lax.axis_index("x"), lax.axis_index("y"), lax.axis_index("c")
        me = 4 * x + 2 * y + c
        gath[me] = g_ref[...]
        copies = []
        for k in range(1, N_DEV):
            fx, fy, fc = (k >> 2) & 1, (k >> 1) & 1, k & 1
            peer = (x ^ fx, y ^ fy, c ^ fc)
            copies.append(pltpu.make_async_remote_copy(
                src_ref=gath.at[me], dst_ref=gath.at[me], send_sem=send_sems.at[k - 1], recv_sem=recv_sems.at[k - 1],
                device_id=peer, device_id_type=MESH))
        for cp in copies:
            cp.start()
        for cp in copies:
            cp.wait_recv()
        for cp in copies:
            cp.wait_send()
        g = gath[0]
        for d in range(1, N_DEV):
            g = g + gath[d]
        go_ref[...] = g

    def adam_body(g_ref, w_ref, m_ref, v_ref, *out_refs):
        g = g_ref[...]
        delta, m_new, v_new = _adam_math(g, w_ref[...], m_ref[...], v_ref[...])
        outs = iter(out_refs)
        for val in (g, delta, m_new, v_new):
            lo = 0
            for r in row_counts:
                next(outs)[...] = val[lo:lo + r]
                lo += r
        next(outs)[...] = g[SMALL_ROWS - SUBLANES:]

    vm = pl.BlockSpec(memory_space=pltpu.VMEM)
    reduce_body, dep_spec, dep_arg = _anchored(reduce_body, 1, dep)
    total = pl.pallas_call(
        reduce_body, in_specs=[vm] + dep_spec, out_specs=vm, out_shape=SDS((SMALL_ROWS, 128), F32),
        scratch_shapes=[pltpu.VMEM((N_DEV, SMALL_ROWS, 128), F32), pltpu.SemaphoreType.DMA((N_DEV - 1,)),
                        pltpu.SemaphoreType.DMA((N_DEV - 1,))],
        name="small_allreduce")(gpart, *dep_arg)
    out_shape = [SDS((r, 128), F32) for _ in range(4) for r in row_counts] + [SDS((SUBLANES, 128), F32)]
    res = pl.pallas_call(adam_body, in_specs=[vm] * 4, out_specs=[vm] * len(out_shape), out_shape=out_shape,
                         name="small_adam")(total, w, m, v)
    k = len(row_counts)
    return [res[i * k:(i + 1) * k] for i in range(4)], res[-1]


BIG = ("w_in", "conv_w", "w_conv_out", "w_attn_out", "w_o", "w_ffn_gate", "w_ffn_up", "w_ffn_down")
LATE_MERGE = ("w_conv_out", "w_attn_out", "w_o")
LATE_FFN = ("w_ffn_gate", "w_ffn_up", "w_ffn_down")
TRANSPOSED = ("w_in", "w_ffn_gate", "w_ffn_up")
COL_SHARDED = ("conv_w", "w_attn_out")
SMALL = ("norm1_g", "gate_b", "conv_b", "conv_ln_g", "conv_ln_b", "norm2_g", "norm_f_g")
WEIGHTS = ("norm1_g", "w_in", "gate_b", "conv_w", "conv_b", "conv_ln_g", "conv_ln_b", "w_conv_out", "w_attn_out", "w_o",
           "norm2_g", "w_ffn_gate", "w_ffn_up", "w_ffn_down", "norm_f_g")


def _shard2d(name, a):
    a = a.reshape(a.shape[-2], a.shape[-1])
    if name in TRANSPOSED:
        a = a.T
    if name == "conv_w":
        a = jnp.pad(a, ((0, CONV_PAD - CONV_K), (0, 0)))
    return a


def _from_shard2d(name, val, shape):
    if name in TRANSPOSED:
        val = val.T
    if name == "conv_w":
        val = val[:CONV_K]
    return val.reshape(shape)


def _gathered_to_full(name, g):
    if name in COL_SHARDED:
        return g.transpose(1, 0, 2).reshape(g.shape[1], N_DEV * g.shape[2])
    return g.reshape(N_DEV * g.shape[1], g.shape[2])


def _full_to_blocks(name, g):
    if name in COL_SHARDED:
        return g.reshape(g.shape[0], N_DEV, g.shape[1] // N_DEV).transpose(1, 0, 2)
    return g.reshape(N_DEV, g.shape[0] // N_DEV, g.shape[1])


def _pack_small(d, last_rows):
    vec = jnp.concatenate([d[n].reshape(-1) for n in SMALL]).reshape(SMALL_ROWS - SUBLANES, 128)
    return jnp.concatenate([vec, last_rows], axis=0)


def kernel(x, norm1_g, w_in, gate_b, conv_w, conv_b, conv_ln_g, conv_ln_b, w_conv_out, w_attn_out, w_o, norm2_g, w_ffn_gate, w_ffn_up, w_ffn_down, norm_f_g, loss_target, m_norm1_g, m_w_in, m_gate_b, m_conv_w, m_conv_b, m_conv_ln_g, m_conv_ln_b, m_w_conv_out, m_w_attn_out, m_w_o, m_norm2_g, m_w_ffn_gate, m_w_ffn_up, m_w_ffn_down, m_norm_f_g, v_norm1_g, v_w_in, v_gate_b, v_conv_w, v_conv_b, v_conv_ln_g, v_conv_ln_b, v_w_conv_out, v_w_attn_out, v_w_o, v_norm2_g, v_w_ffn_gate, v_w_ffn_up, v_w_ffn_down, v_norm_f_g):
    wts = dict(norm1_g=norm1_g, w_in=w_in, gate_b=gate_b, conv_w=conv_w, conv_b=conv_b, conv_ln_g=conv_ln_g,
               conv_ln_b=conv_ln_b, w_conv_out=w_conv_out, w_attn_out=w_attn_out, w_o=w_o, norm2_g=norm2_g,
               w_ffn_gate=w_ffn_gate, w_ffn_up=w_ffn_up, w_ffn_down=w_ffn_down, norm_f_g=norm_f_g)
    mom1 = dict(norm1_g=m_norm1_g, w_in=m_w_in, gate_b=m_gate_b, conv_w=m_conv_w, conv_b=m_conv_b, conv_ln_g=m_conv_ln_g,
                conv_ln_b=m_conv_ln_b, w_conv_out=m_w_conv_out, w_attn_out=m_w_attn_out, w_o=m_w_o, norm2_g=m_norm2_g,
                w_ffn_gate=m_w_ffn_gate, w_ffn_up=m_w_ffn_up, w_ffn_down=m_w_ffn_down, norm_f_g=m_norm_f_g)
    mom2 = dict(norm1_g=v_norm1_g, w_in=v_w_in, gate_b=v_gate_b, conv_w=v_conv_w, conv_b=v_conv_b, conv_ln_g=v_conv_ln_g,
                conv_ln_b=v_conv_ln_b, w_conv_out=v_w_conv_out, w_attn_out=v_w_attn_out, w_o=v_w_o, norm2_g=v_norm2_g,
                w_ffn_gate=v_w_ffn_gate, w_ffn_up=v_w_ffn_up, w_ffn_down=v_w_ffn_down, norm_f_g=v_norm_f_g)

    T = x.shape[0] * x.shape[1]
    x2 = x.reshape(T, D_MODEL)
    t2 = loss_target.reshape(T, D_MODEL)

    me = 4 * lax.axis_index("x") + 2 * lax.axis_index("y") + lax.axis_index("c")
    shards = {n: _shard2d(n, wts[n]) for n in BIG}
    sent = {n: shards[n] if n == "conv_w" else shards[n].astype(BF16) for n in BIG}
    small = {n: wts[n].reshape(1, -1) for n in SMALL}

    xi, yi = lax.axis_index("x"), lax.axis_index("y")
    chip_order = jnp.stack([2 * xi + yi, 2 * (1 - xi) + yi, 2 * xi + (1 - yi), 2 * (1 - xi) + (1 - yi)]).astype(jnp.int32)
    stage = {}

    def in_proj():
        h, proj, w_in_blocks, conv_blocks = _gather_in_proj(x2, small["norm1_g"], sent["w_in"], sent["conv_w"], chip_order)
        near = (1,) + OTHER_CHIPS
        stage["merge"] = _send_start("gather", near, "gather_start_merge", [sent[n] for n in LATE_MERGE], dep=w_in_blocks)
        stage["ffn"] = _send_start("gather", near, "gather_start_ffn", [sent[n] for n in LATE_FFN],
                                   dep=stage["merge"]["token"])
        return h, proj, {"w_in": _gathered_to_full("w_in", w_in_blocks), "conv_w": _gathered_to_full("conv_w", conv_blocks),
                         "token": stage["ffn"]["token"]}

    def filled(names, srcs, lands):
        return {n: _gathered_to_full(n, lax.dynamic_update_slice(land, src[None], (me, 0, 0)))
                for n, src, land in zip(names, srcs, lands)}

    def pass_on(group, after):
        stage[group + "_srcs"], lands = _send_wait(stage[group], after, "gather_wait_" + group)
        stage[group + "_forward"] = _send_start("forward", OTHER_CHIPS, "forward_start_" + group, lands=lands)
        return stage[group + "_forward"]["token"]

    def arrived(group, names, after):
        _, lands = _send_wait(stage[group + "_forward"], after, "forward_wait_" + group)
        return filled(names, stage[group + "_srcs"], lands)

    def late_weights(which, after):
        if which == "after_attention":
            return {"token": pass_on("merge", after)}
        if which is LATE_MERGE:
            return {**arrived("merge", LATE_MERGE, after), "token": pass_on("ffn", after)}
        return arrived("ffn", LATE_FFN, after)

    scatters = []
    core = lax.axis_index("c").astype(jnp.int32).reshape(1)

    def emit(names, gw):
        blocks = [_full_to_blocks(n, gw[n]) for n in names]
        if "w_in" in names:
            sums = [_add_pair(g, r, core, "chip_sum_" + n) for n, g, r in zip(names, blocks, _exchange_sibling(blocks))]
            started = _send_start("chip_scatter", OTHER_CHIPS, "scatter_start_" + names[0], sums)
        else:
            started = _send_start("scatter", ALL_PEERS, "scatter_start_" + names[0], blocks)
        scatters.append((names, started))
        return started["token"]

    loss_part, grad_x, gw, gsmall = _local_step(x2, t2, in_proj, small, late_weights, emit)

    grads, deltas, new_m, new_v = {}, {}, {}, {}
    after = grad_x
    for names, started in scatters:
        srcs, lands = _send_wait(started, after, "scatter_wait_" + names[0])
        mine = (me >> 1 if started["mode"] == "chip_scatter" else me).astype(jnp.int32).reshape(1)
        for n, src, land in zip(names, srcs, lands):
            g, d, mo, vo = _sum_adam(land, src, mine, shards[n], _shard2d(n, mom1[n]), _shard2d(n, mom2[n]), "adam_" + n)
            for dst, val in ((grads, g), (deltas, d), (new_m, mo), (new_v, vo)):
                dst[n] = _from_shard2d(n, val, wts[n].shape)
            after = g

    zeros, ones = jnp.zeros((SUBLANES, 128), F32), jnp.ones((SUBLANES, 128), F32)
    row_counts = [wts[n].size // 128 for n in SMALL]
    kinds, loss_rows = _small_allreduce_adam(
        _pack_small(gsmall, jnp.broadcast_to(loss_part, (SUBLANES, 128))), _pack_small(wts, zeros),
        _pack_small(mom1, zeros), _pack_small(mom2, ones), row_counts, after)
    for dst, vals in zip((grads, deltas, new_m, new_v), kinds):
        dst.update({n: val.reshape(wts[n].shape) for n, val in zip(SMALL, vals)})
    loss = loss_rows[0, 0]
    return (loss, grad_x.reshape(x.shape), *[grads[n] for n in WEIGHTS], *[deltas[n] for n in WEIGHTS],
            *[new_m[n] for n in WEIGHTS], *[new_v[n] for n in WEIGHTS])
```

```python
import math

import numpy as np
import jax
import jax.numpy as jnp
from jax import lax
from jax.experimental import pallas as pl
from jax.experimental.pallas import tpu as pltpu

F32 = jnp.float32
BF16 = jnp.bfloat16
SDS = jax.ShapeDtypeStruct
MESH = pl.DeviceIdType.MESH

D_MODEL = 1024
SEQ = 2048
HEAD_DIM = 64
GROUPS = ((128, 1), (512, 4), (2048, 16))
HEADS_PER_GROUP = 8
N_HEADS = 24
ATTN_WIDTH = N_HEADS * HEAD_DIM
ATTN_OUT = HEADS_PER_GROUP * HEAD_DIM
CONV_K = 31
CONV_PAD = 32
D_FF = 2816
IN_WIDTH = 3 * ATTN_WIDTH + 2 * D_MODEL + 2 * D_MODEL
RMS_EPS = 1e-6
LN_EPS = 1e-5
Q_BLOCK = 128
LANES = 128
NEG = -1e30
N_DEV = 8

ADAM_LR = 0.001
ADAM_B1 = 0.9
ADAM_B2 = 0.999
ADAM_EPS = 1e-08
ADAM_WD = 0.01
ADAM_STEP = 10


def _alibi_slope_list(n):
    def pow2(m):
        start = 2.0 ** (-8.0 / m)
        return [start ** (i + 1) for i in range(m)]
    if math.log2(n).is_integer():
        return pow2(n)
    c = 2 ** math.floor(math.log2(n))
    return pow2(c) + _alibi_slope_list(2 * c)[0::2][: n - c]


def _slopes_times_dilation():
    s = np.asarray(sorted(_alibi_slope_list(N_HEADS), reverse=True), dtype=np.float32).reshape(3, HEADS_PER_GROUP)
    r = np.asarray([g[1] for g in GROUPS], dtype=np.float32)[:, None]
    return (s * r).reshape(N_HEADS)


def _sigmoid(x):
    return 0.5 * jnp.tanh(0.5 * x) + 0.5


def _dot(a, b):
    return jnp.dot(a, b, preferred_element_type=F32)


def _dot_nt(a, b):
    return lax.dot_general(a, b, (((1,), (1,)), ((), ())), preferred_element_type=F32)


def _dot_tn(a, b):
    return lax.dot_general(a, b, (((0,), (0,)), ((), ())), preferred_element_type=F32)


def _rowsum(x):
    return jnp.sum(x, axis=0, keepdims=True)


def _params(*sem):
    return pltpu.CompilerParams(dimension_semantics=sem)


def _anchored(body, n_in, dep):
    if dep is None:
        return body, [], []

    def wrapped(*refs):
        return body(*refs[:n_in], *refs[n_in + 1:])

    return wrapped, [pl.BlockSpec(memory_space=pl.ANY)], [dep]


IN_TM = 256
IN_CHUNK = 512


def _in_proj(x, g1, w_in_t, dep=None):
    T = x.shape[0]
    tm = IN_TM

    def body(x_ref, g_ref, w_hbm, h_ref, proj_ref, w_vmem, sem):
        @pl.when(pl.program_id(0) == 0)
        def _():
            cp = pltpu.make_async_copy(w_hbm, w_vmem, sem)
            cp.start()
            cp.wait()

        xv = x_ref[...]
        r = lax.rsqrt(jnp.mean(xv * xv, axis=-1, keepdims=True) + RMS_EPS)
        h = (xv * r * g_ref[...]).astype(BF16)
        h_ref[...] = h
        for lo in range(0, IN_WIDTH, IN_CHUNK):
            proj_ref[:, lo:lo + IN_CHUNK] = _dot_nt(h, w_vmem[lo:lo + IN_CHUNK, :])

    row = lambda n: pl.BlockSpec((tm, n), lambda i: (i, 0))
    body, dep_spec, dep_arg = _anchored(body, 3, dep)
    return pl.pallas_call(
        body, grid=(T // tm,),
        in_specs=[row(D_MODEL), pl.BlockSpec((1, D_MODEL), lambda i: (0, 0)), pl.BlockSpec(memory_space=pl.ANY)] + dep_spec,
        out_specs=[row(D_MODEL), row(IN_WIDTH)],
        out_shape=[SDS((T, D_MODEL), BF16), SDS((T, IN_WIDTH), F32)],
        scratch_shapes=[pltpu.VMEM((IN_WIDTH, D_MODEL), BF16), pltpu.SemaphoreType.DMA],
        compiler_params=_params("arbitrary"), name="in_proj")(x, g1, w_in_t, *dep_arg)


def _mm_tn(a, b, out_dtype, name, tn, tt=1024):
    T, K = a.shape
    N = b.shape[1]
    nt = T // tt

    def body(a_ref, b_ref, o_ref, acc):
        t = pl.program_id(1)

        @pl.when(t == 0)
        def _():
            acc[...] = jnp.zeros_like(acc)

        acc[...] += _dot_tn(a_ref[...], b_ref[...])

        @pl.when(t == nt - 1)
        def _():
            o_ref[...] = acc[...].astype(o_ref.dtype)

    return pl.pallas_call(
        body, grid=(N // tn, nt),
        in_specs=[pl.BlockSpec((tt, K), lambda j, t: (t, 0)),
                  pl.BlockSpec((tt, tn), lambda j, t: (t, j))],
        out_specs=pl.BlockSpec((K, tn), lambda j, t: (0, j)),
        out_shape=SDS((K, N), out_dtype),
        scratch_shapes=[pltpu.VMEM((K, tn), F32)],
        compiler_params=_params("parallel", "arbitrary"), name=name)(a, b)


def _gather_classes(src_ref, dst, r, row0=0):
    L = SEQ // r
    for c in range(r):
        dst[row0 + c * L:row0 + (c + 1) * L, :] = src_ref[0, pl.ds(c, L, stride=r), :].astype(dst.dtype)


def _scatter_classes(src, dst, r, row0=0):
    L = SEQ // r
    for c in range(r):
        dst[pl.ds(c, L, stride=r), :] = src[row0 + c * L:row0 + (c + 1) * L, :].astype(dst.dtype)


def _attn_masks(slope_r):
    qi = lax.broadcasted_iota(jnp.int32, (Q_BLOCK, Q_BLOCK), 0)
    kj = lax.broadcasted_iota(jnp.int32, (Q_BLOCK, Q_BLOCK), 1)
    rel = (qi - kj).astype(F32)
    bias_cur = jnp.where(qi >= kj, -slope_r * rel, NEG)
    bias_prev = jnp.where(qi <= kj, -slope_r * (rel + float(Q_BLOCK)), NEG)
    return bias_cur, bias_prev


def _store_biases(bias, sl_ref, g, hp):
    for hh in range(2):
        cur, prev = _attn_masks(sl_ref[g * HEADS_PER_GROUP + 2 * hp + hh])
        rows = slice(hh * Q_BLOCK, (hh + 1) * Q_BLOCK)
        bias[0, rows, 0:Q_BLOCK] = prev
        bias[1, rows, 0:Q_BLOCK] = jnp.full((Q_BLOCK, Q_BLOCK), NEG, F32)
        bias[0, rows, Q_BLOCK:] = cur
        bias[1, rows, Q_BLOCK:] = cur


def _transpose_pairs(src, dst):
    dst[0, :, 0:Q_BLOCK] = jnp.zeros((LANES, Q_BLOCK), dst.dtype)
    nblk = SEQ // Q_BLOCK
    for b in range(nblk):
        t = src[(b + 1) * Q_BLOCK:(b + 2) * Q_BLOCK, :].T
        dst[b, :, Q_BLOCK:] = t
        if b + 1 < nblk:
            dst[b + 1, :, 0:Q_BLOCK] = t


def _stack_heads(t, low):
    z = jnp.zeros_like(t)
    return jnp.concatenate([jnp.where(low, t, z), jnp.where(low, z, t)], axis=0)


def _unstack_heads(t2, low):
    return jnp.where(low, t2[:Q_BLOCK], t2[Q_BLOCK:])


def _unit_offsets(u, nb):
    off = pl.multiple_of(u * Q_BLOCK, Q_BLOCK)
    n = u & (nb - 1)
    c = u >> int(math.log2(nb))
    return off, n == 0, c, n


ATTN_UNROLL = 4


def _attn_fwd(qkv, slopes_r, batch, dep=None):
    nblk = SEQ // Q_BLOCK

    def body(sl_ref, *refs):
        qkv_refs = refs[:9]
        att_ref, lse_ref = refs[9:11]
        qd, kd, vd, kt, opos, lpos, bias = refs[11:]
        hp = pl.program_id(1)
        low = lax.broadcasted_iota(jnp.int32, (Q_BLOCK, LANES), 1) < HEAD_DIM

        for g in range(3):
            r = GROUPS[g][1]
            nb = SEQ // r // Q_BLOCK
            _gather_classes(qkv_refs[3 * g], qd, r)
            kd[0:Q_BLOCK, :] = jnp.zeros((Q_BLOCK, LANES), BF16)
            vd[0:Q_BLOCK, :] = jnp.zeros((Q_BLOCK, LANES), BF16)
            _gather_classes(qkv_refs[3 * g + 1], kd, r, Q_BLOCK)
            _gather_classes(qkv_refs[3 * g + 2], vd, r, Q_BLOCK)
            _transpose_pairs(kd, kt)
            _store_biases(bias, sl_ref, g, hp)

            def unit(u, carry, g=g, r=r, nb=nb):
                off, first, c, n = _unit_offsets(u, nb)
                q2 = _stack_heads(qd[pl.ds(off, Q_BLOCK), :], low)
                s = _dot(q2, kt[u]) * 0.125 + bias[first.astype(jnp.int32)]
                m = jnp.max(s, axis=-1, keepdims=True)
                p = jnp.exp(s - m)
                l = jnp.sum(p, axis=-1, keepdims=True)
                o2 = _dot(p.astype(BF16), vd[pl.ds(off, 2 * Q_BLOCK), :]) * (1.0 / l)
                lse2 = m + jnp.log(l)
                rows = pl.ds(c + n * (Q_BLOCK * r), Q_BLOCK, stride=r)
                opos[g, rows, :] = _unstack_heads(o2, low)
                lpos[g, rows, :] = jnp.where(low, lse2[:Q_BLOCK], lse2[Q_BLOCK:])
                return carry

            lax.fori_loop(0, nblk, unit, 0, unroll=ATTN_UNROLL)

        def merge(i, carry):
            rows = pl.ds(pl.multiple_of(i * 256, 256), 256)
            l0, l1, l2 = lpos[0, rows, :], lpos[1, rows, :], lpos[2, rows, :]
            m = jnp.maximum(jnp.maximum(l0, l1), l2)
            e0, e1, e2 = jnp.exp(l0 - m), jnp.exp(l1 - m), jnp.exp(l2 - m)
            den = e0 + e1 + e2
            att = (e0 * opos[0, rows, :] + e1 * opos[1, rows, :] + e2 * opos[2, rows, :]) / den
            att_ref[0, rows, :] = att.astype(att_ref.dtype)
            lse_ref[0, rows, :] = m + jnp.log(den)
            return carry

        lax.fori_loop(0, SEQ // 256, merge, 0)

    def col(sec, g):
        return pl.BlockSpec((1, SEQ, LANES), lambda b, hp: (b, 0, sec * 12 + g * 4 + hp))

    out = pl.BlockSpec((1, SEQ, LANES), lambda b, hp: (b, 0, hp))
    body, dep_spec, dep_arg = _anchored(body, 10, dep)
    return pl.pallas_call(
        body, grid=(batch, 4),
        in_specs=[pl.BlockSpec(memory_space=pltpu.SMEM)] + [col(sec, g) for g in range(3) for sec in range(3)] + dep_spec,
        out_specs=[out, out],
        out_shape=[SDS((batch, SEQ, ATTN_OUT), BF16), SDS((batch, SEQ, ATTN_OUT), F32)],
        scratch_shapes=[pltpu.VMEM((SEQ, LANES), BF16), pltpu.VMEM((Q_BLOCK + SEQ, LANES), BF16),
                        pltpu.VMEM((Q_BLOCK + SEQ, LANES), BF16), pltpu.VMEM((nblk, LANES, 2 * Q_BLOCK), BF16),
                        pltpu.VMEM((3, SEQ, LANES), F32), pltpu.VMEM((3, SEQ, LANES), F32),
                        pltpu.VMEM((2, 2 * Q_BLOCK, 2 * Q_BLOCK), F32)],
        compiler_params=_params("parallel", "parallel"), name="attn_fwd")(slopes_r, *([qkv] * 9), *dep_arg)


def _attn_bwd(qkv, datt, lse, dsum, slopes_r, batch):
    nblk = SEQ // Q_BLOCK

    def body(sl_ref, q_ref, k_ref, v_ref, do_ref, l_ref, d_ref, dq_ref, dk_ref, dv_ref,
             qd, kd, vd, dod, kt, vt, ld, dd, dq_acc, dk_acc, dv_acc, dk_part, dv_part, stage, bias):
        gid, hp = pl.program_id(1), pl.program_id(2)
        low = lax.broadcasted_iota(jnp.int32, (Q_BLOCK, LANES), 1) < HEAD_DIM

        def section(g):
            r = GROUPS[g][1]
            nb = SEQ // r // Q_BLOCK
            _gather_classes(q_ref, qd, r)
            kd[0:Q_BLOCK, :] = jnp.zeros((Q_BLOCK, LANES), BF16)
            vd[0:Q_BLOCK, :] = jnp.zeros((Q_BLOCK, LANES), BF16)
            _gather_classes(k_ref, kd, r, Q_BLOCK)
            _gather_classes(v_ref, vd, r, Q_BLOCK)
            _gather_classes(do_ref, dod, r)
            _gather_classes(l_ref, ld, r)
            _gather_classes(d_ref, dd, r)
            _transpose_pairs(kd, kt)
            _transpose_pairs(vd, vt)
            _store_biases(bias, sl_ref, g, hp)

            def unit(u, carry):
                off, first, _, _ = _unit_offsets(u, nb)
                pair = pl.ds(off, 2 * Q_BLOCK)
                q2 = _stack_heads(qd[pl.ds(off, Q_BLOCK), :], low)
                do2 = _stack_heads(dod[pl.ds(off, Q_BLOCK), :], low)
                lse_t = ld[pl.ds(off, Q_BLOCK), :]
                dsum_t = dd[pl.ds(off, Q_BLOCK), :]
                lse2 = jnp.concatenate([lse_t[:, 0:1], lse_t[:, HEAD_DIM:HEAD_DIM + 1]], axis=0)
                dsum2 = jnp.concatenate([dsum_t[:, 0:1], dsum_t[:, HEAD_DIM:HEAD_DIM + 1]], axis=0)
                s = _dot(q2, kt[u]) * 0.125 + bias[first.astype(jnp.int32)]
                p = jnp.exp(s - lse2)
                ds = (p * (_dot(do2, vt[u]) - dsum2)).astype(BF16)
                dq_acc[pl.ds(off, Q_BLOCK), :] = _unstack_heads(_dot(ds, kd[pair, :]), low) * 0.125
                dk_part[u] = _dot_tn(ds, q2) * 0.125
                dv_part[u] = _dot_tn(p.astype(BF16), do2)
                return carry

            lax.fori_loop(0, nblk, unit, 0, unroll=ATTN_UNROLL)
            for part, acc in ((dk_part, dk_acc), (dv_part, dv_acc)):
                for b in range(nblk):
                    t = part[b, Q_BLOCK:, :]
                    if b + 1 < nblk:
                        t = t + part[b + 1, 0:Q_BLOCK, :]
                    acc[b * Q_BLOCK:(b + 1) * Q_BLOCK, :] = t
            for acc, out_ref in ((dq_acc, dq_ref), (dk_acc, dk_ref), (dv_acc, dv_ref)):
                _scatter_classes(acc, stage, r)
                out_ref[0] = stage[...].astype(out_ref.dtype)

        for g in range(3):
            pl.when(gid == g)(lambda g=g: section(g))

    def col(sec):
        return pl.BlockSpec((1, SEQ, LANES), lambda b, g, hp: (b, 0, sec * 12 + g * 4 + hp))

    pos = pl.BlockSpec((1, SEQ, LANES), lambda b, g, hp: (b, 0, hp))
    dout = pl.BlockSpec((1, SEQ, LANES), lambda b, g, hp: (b, 0, g * 4 + hp))
    out = SDS((batch, SEQ, ATTN_WIDTH), BF16)
    seq_bf = pltpu.VMEM((SEQ, LANES), BF16)
    seq_f = pltpu.VMEM((SEQ, LANES), F32)
    pad_bf = pltpu.VMEM((Q_BLOCK + SEQ, LANES), BF16)
    part = pltpu.VMEM((nblk, 2 * Q_BLOCK, LANES), F32)
    blk_t = pltpu.VMEM((nblk, LANES, 2 * Q_BLOCK), BF16)
    return pl.pallas_call(
        body, grid=(batch, 3, 4),
        in_specs=[pl.BlockSpec(memory_space=pltpu.SMEM), col(0), col(1), col(2), pos, pos, pos],
        out_specs=[dout, dout, dout],
        out_shape=[out, out, out],
        scratch_shapes=[seq_bf, pad_bf, pad_bf, seq_bf, blk_t, blk_t, seq_f, seq_f, seq_f, seq_f, seq_f, part, part, seq_f,
                        pltpu.VMEM((2, 2 * Q_BLOCK, 2 * Q_BLOCK), F32)],
        compiler_params=_params("parallel", "parallel", "parallel"), name="attn_bwd")(
            slopes_r, qkv, qkv, qkv, datt, lse, dsum)


CONV_TC = 128
U_BLOCK0 = 3 * ATTN_WIDTH // CONV_TC
CONV_ROWS = 128
SUBLANES = 8


def _fill_shifted(sh):
    n = SEQ + CONV_PAD - SUBLANES
    for s in range(1, SUBLANES):
        sh[s, 0:n, :] = sh[0, s:s + n, :]


def _tap(sh, base, offset):
    s = offset % SUBLANES
    return sh[s, pl.ds(pl.multiple_of(base + (offset - s), SUBLANES), CONV_ROWS), :]


def _conv_fwd(u, conv_w, conv_b, batch, dep=None):
    nct = D_MODEL // CONV_TC

    def body(ua_ref, ub_ref, w_ref, b_ref, o_ref, sh):
        sh[0, 0:CONV_PAD, :] = jnp.zeros((CONV_PAD, CONV_TC), F32)
        sh[0, CONV_PAD:, :] = ua_ref[0] * _sigmoid(ub_ref[0])
        _fill_shifted(sh)

        def chunk(c, carry):
            base = pl.multiple_of(c * CONV_ROWS, CONV_ROWS)
            acc = jnp.broadcast_to(b_ref[...], (CONV_ROWS, CONV_TC))
            for t in range(CONV_K):
                acc = acc + _tap(sh, base, t + CONV_PAD - (CONV_K - 1)) * w_ref[t:t + 1, :]
            o_ref[0, pl.ds(base, CONV_ROWS), :] = acc
            return carry

        lax.fori_loop(0, SEQ // CONV_ROWS, chunk, 0)

    body, dep_spec, dep_arg = _anchored(body, 4, dep)
    return pl.pallas_call(
        body, grid=(nct, batch),
        in_specs=[pl.BlockSpec((1, SEQ, CONV_TC), lambda j, b: (b, 0, U_BLOCK0 + j)),
                  pl.BlockSpec((1, SEQ, CONV_TC), lambda j, b: (b, 0, U_BLOCK0 + nct + j)),
                  pl.BlockSpec((CONV_PAD, CONV_TC), lambda j, b: (0, j)),
                  pl.BlockSpec((1, CONV_TC), lambda j, b: (0, j))] + dep_spec,
        out_specs=pl.BlockSpec((1, SEQ, CONV_TC), lambda j, b: (b, 0, j)),
        out_shape=SDS((batch, SEQ, D_MODEL), F32),
        scratch_shapes=[pltpu.VMEM((SUBLANES, SEQ + CONV_PAD, CONV_TC), F32)],
        compiler_params=_params("parallel", "parallel"), name="conv_fwd")(u, u, conv_w, conv_b, *dep_arg)


def _conv_bwd(u, dc1, conv_w, batch, dep=None):
    nct = D_MODEL // CONV_TC
    nchunk = SEQ // CONV_ROWS

    def body(ua_ref, ub_ref, d_ref, w_ref, dua_ref, dub_ref, gw_ref, gb_ref, shc, shd, gacc):
        b = pl.program_id(1)
        shc[0, 0:CONV_PAD, :] = jnp.zeros((CONV_PAD, CONV_TC), F32)
        shc[0, CONV_PAD:, :] = ua_ref[0] * _sigmoid(ub_ref[0])
        _fill_shifted(shc)
        shd[0, 0:SEQ, :] = d_ref[0]
        shd[0, SEQ:, :] = jnp.zeros((CONV_PAD, CONV_TC), F32)
        _fill_shifted(shd)

        @pl.when(b == 0)
        def _():
            gacc[...] = jnp.zeros_like(gacc)
            gb_ref[...] = jnp.zeros_like(gb_ref)

        gb_ref[...] += _rowsum(d_ref[0])

        def chunk(c, carry):
            base = pl.multiple_of(c * CONV_ROWS, CONV_ROWS)
            dcur = shd[0, pl.ds(base, CONV_ROWS), :]
            acc = jnp.zeros((CONV_ROWS, CONV_TC), F32)
            for t in range(CONV_K):
                acc = acc + _tap(shd, base, CONV_K - 1 - t) * w_ref[t:t + 1, :]
                prod = _tap(shc, base, t + CONV_PAD - (CONV_K - 1)) * dcur
                gacc[t] += jnp.sum(prod.reshape(CONV_ROWS // 8, 8, CONV_TC), axis=0)
            ua = ua_ref[0, pl.ds(base, CONV_ROWS), :]
            sg = _sigmoid(ub_ref[0, pl.ds(base, CONV_ROWS), :])
            dua_ref[0, pl.ds(base, CONV_ROWS), :] = (acc * sg).astype(dua_ref.dtype)
            dub_ref[0, pl.ds(base, CONV_ROWS), :] = (acc * ua * sg * (1.0 - sg)).astype(dub_ref.dtype)
            return carry

        lax.fori_loop(0, nchunk, chunk, 0)

        @pl.when(b == batch - 1)
        def _():
            for t in range(CONV_K):
                gw_ref[t:t + 1, :] = jnp.sum(gacc[t], axis=0, keepdims=True)
            gw_ref[CONV_K:CONV_PAD, :] = jnp.zeros((CONV_PAD - CONV_K, CONV_TC), F32)

    du = SDS((batch, SEQ, D_MODEL), BF16)
    body, dep_spec, dep_arg = _anchored(body, 4, dep)
    return pl.pallas_call(
        body, grid=(nct, batch),
        in_specs=[pl.BlockSpec((1, SEQ, CONV_TC), lambda j, b: (b, 0, U_BLOCK0 + j)),
                  pl.BlockSpec((1, SEQ, CONV_TC), lambda j, b: (b, 0, U_BLOCK0 + nct + j)),
                  pl.BlockSpec((1, SEQ, CONV_TC), lambda j, b: (b, 0, j)),
                  pl.BlockSpec((CONV_PAD, CONV_TC), lambda j, b: (0, j))] + dep_spec,
        out_specs=[pl.BlockSpec((1, SEQ, CONV_TC), lambda j, b: (b, 0, j)),
                   pl.BlockSpec((1, SEQ, CONV_TC), lambda j, b: (b, 0, j)),
                   pl.BlockSpec((CONV_PAD, CONV_TC), lambda j, b: (0, j)),
                   pl.BlockSpec((1, CONV_TC), lambda j, b: (0, j))],
        out_shape=[du, du, SDS((CONV_PAD, D_MODEL), F32), SDS((1, D_MODEL), F32)],
        scratch_shapes=[pltpu.VMEM((SUBLANES, SEQ + CONV_PAD, CONV_TC), F32),
                        pltpu.VMEM((SUBLANES, SEQ + CONV_PAD, CONV_TC), F32),
                        pltpu.VMEM((CONV_K, 8, CONV_TC), F32)],
        compiler_params=_params("parallel", "arbitrary"), name="conv_bwd")(u, u, dc1, conv_w, *dep_arg)


MID_TM = 256


def _layernorm_stats(c1):
    mu = jnp.mean(c1, axis=-1, keepdims=True)
    cen = c1 - mu
    rs = lax.rsqrt(jnp.mean(cen * cen, axis=-1, keepdims=True) + LN_EPS)
    return cen * rs, rs


GATE_PARTS = 4
GATE_PART = 2 * D_MODEL // GATE_PARTS
GATE_PART0 = (IN_WIDTH - 2 * D_MODEL) // GATE_PART


def _gate_specs(tm):
    return [pl.BlockSpec((tm, GATE_PART), lambda i, k=k: (i, GATE_PART0 + k)) for k in range(GATE_PARTS)]


def _mid_fwd(att, c1, proj, x, w_a, w_c, w_o, gate_b, ln_g, ln_b, g2, dep=None):
    T = x.shape[0]
    tm = MID_TM

    def body(att_ref, c1_ref, lg0, lg1, lg2, lg3, x_ref, wa_ref, wc_ref, wo_ref, gb_ref, lng_ref, lnb_ref, g2_ref,
             c3_ref, ya_ref, yc_ref, mix_ref, x1_ref, h2_ref):
        logits = jnp.concatenate([lg0[...], lg1[...], lg2[...], lg3[...]], axis=1)
        ya = _dot(att_ref[...], wa_ref[...])
        xh, _ = _layernorm_stats(c1_ref[...])
        c2 = xh * lng_ref[...] + lnb_ref[...]
        c3 = (c2 * _sigmoid(c2)).astype(BF16)
        c3_ref[...] = c3
        yc = _dot(c3, wc_ref[...])
        gates = _sigmoid(logits + gb_ref[...])
        mix = (gates[:, :D_MODEL] * ya + gates[:, D_MODEL:] * yc).astype(BF16)
        ya_ref[...] = ya.astype(BF16)
        yc_ref[...] = yc.astype(BF16)
        mix_ref[...] = mix
        x1 = x_ref[...] + _dot(mix, wo_ref[...])
        x1_ref[...] = x1
        r = lax.rsqrt(jnp.mean(x1 * x1, axis=-1, keepdims=True) + RMS_EPS)
        h2_ref[...] = (x1 * r * g2_ref[...]).astype(BF16)

    row = lambda n: pl.BlockSpec((tm, n), lambda i: (i, 0))
    full = lambda a, b: pl.BlockSpec((a, b), lambda i: (0, 0))
    body, dep_spec, dep_arg = _anchored(body, 10 + GATE_PARTS, dep)
    return pl.pallas_call(
        body, grid=(T // tm,),
        in_specs=[row(ATTN_OUT), row(D_MODEL)] + _gate_specs(tm) + [row(D_MODEL),
                  full(ATTN_OUT, D_MODEL), full(D_MODEL, D_MODEL), full(D_MODEL, D_MODEL),
                  full(1, 2 * D_MODEL), full(1, D_MODEL), full(1, D_MODEL), full(1, D_MODEL)] + dep_spec,
        out_specs=[row(D_MODEL), row(D_MODEL), row(D_MODEL), row(D_MODEL), row(D_MODEL), row(D_MODEL)],
        out_shape=[SDS((T, D_MODEL), BF16), SDS((T, D_MODEL), BF16), SDS((T, D_MODEL), BF16), SDS((T, D_MODEL), BF16),
                   SDS((T, D_MODEL), F32), SDS((T, D_MODEL), BF16)],
        compiler_params=_params("parallel"), name="mid_fwd")(att, c1, *([proj] * GATE_PARTS), x, w_a, w_c, w_o,
                                                             gate_b, ln_g, ln_b, g2, *dep_arg)


def _mid_bwd(dx1b, ya, yc, proj, att, c1, w_a, w_c, w_o, gate_b, ln_g, ln_b, head_ones, dep=None):
    T = dx1b.shape[0]
    tm = MID_TM

    def body(dx_ref, ya_ref, yc_ref, lg0, lg1, lg2, lg3, att_ref, c1_ref, wa_ref, wc_ref, wo_ref, gb_ref, lng_ref,
             lnb_ref, e_ref, dlg_ref, dya_ref, dyc_ref, datt_ref, dsum_ref, dc1_ref, ggb_ref, glg_ref, glb_ref):
        logits = jnp.concatenate([lg0[...], lg1[...], lg2[...], lg3[...]], axis=1)
        @pl.when(pl.program_id(0) == 0)
        def _():
            ggb_ref[...] = jnp.zeros_like(ggb_ref)
            glg_ref[...] = jnp.zeros_like(glg_ref)
            glb_ref[...] = jnp.zeros_like(glb_ref)

        dmix = _dot_nt(dx_ref[...], wo_ref[...])
        gates = _sigmoid(logits + gb_ref[...])
        ga, gc = gates[:, :D_MODEL], gates[:, D_MODEL:]
        dla = dmix * ya_ref[...].astype(F32) * ga * (1.0 - ga)
        dlc = dmix * yc_ref[...].astype(F32) * gc * (1.0 - gc)
        dlg_ref[:, :D_MODEL] = dla.astype(BF16)
        dlg_ref[:, D_MODEL:] = dlc.astype(BF16)
        ggb_ref[:, :D_MODEL] += _rowsum(dla)
        ggb_ref[:, D_MODEL:] += _rowsum(dlc)
        dya = (dmix * ga).astype(BF16)
        dyc = (dmix * gc).astype(BF16)
        dya_ref[...] = dya
        dyc_ref[...] = dyc
        datt = _dot_nt(dya, wa_ref[...])
        datt_ref[...] = datt
        dsum_ref[...] = jnp.dot(datt * att_ref[...].astype(F32), e_ref[...], preferred_element_type=F32,
                                precision=lax.Precision.HIGHEST)
        dc3 = _dot_nt(dyc, wc_ref[...])
        xh, rs = _layernorm_stats(c1_ref[...])
        c2 = xh * lng_ref[...] + lnb_ref[...]
        sg = _sigmoid(c2)
        dc2 = dc3 * (sg * (1.0 + c2 * (1.0 - sg)))
        glg_ref[...] += _rowsum(dc2 * xh)
        glb_ref[...] += _rowsum(dc2)
        dxh = dc2 * lng_ref[...]
        dc1_ref[...] = rs * (dxh - jnp.mean(dxh, axis=-1, keepdims=True) - xh * jnp.mean(dxh * xh, axis=-1, keepdims=True))

    row = lambda n: pl.BlockSpec((tm, n), lambda i: (i, 0))
    full = lambda a, b: pl.BlockSpec((a, b), lambda i: (0, 0))
    body, dep_spec, dep_arg = _anchored(body, 12 + GATE_PARTS, dep)
    return pl.pallas_call(
        body, grid=(T // tm,),
        in_specs=[row(D_MODEL), row(D_MODEL), row(D_MODEL)] + _gate_specs(tm) + [row(ATTN_OUT), row(D_MODEL),
                  full(ATTN_OUT, D_MODEL), full(D_MODEL, D_MODEL), full(D_MODEL, D_MODEL),
                  full(1, 2 * D_MODEL), full(1, D_MODEL), full(1, D_MODEL), full(ATTN_OUT, ATTN_OUT)] + dep_spec,
        out_specs=[row(2 * D_MODEL), row(D_MODEL), row(D_MODEL), row(ATTN_OUT), row(ATTN_OUT), row(D_MODEL),
                   full(1, 2 * D_MODEL), full(1, D_MODEL), full(1, D_MODEL)],
        out_shape=[SDS((T, 2 * D_MODEL), BF16), SDS((T, D_MODEL), BF16), SDS((T, D_MODEL), BF16), SDS((T, ATTN_OUT), F32),
                   SDS((T, ATTN_OUT), F32), SDS((T, D_MODEL), F32),
                   SDS((1, 2 * D_MODEL), F32), SDS((1, D_MODEL), F32), SDS((1, D_MODEL), F32)],
        compiler_params=_params("arbitrary"), name="mid_bwd")(dx1b, ya, yc, *([proj] * GATE_PARTS), att, c1, w_a, w_c, w_o,
                                                               gate_b, ln_g, ln_b, head_ones, *dep_arg)


FFN_TM = 512
FFN_TF = D_FF // 2
FFN_SUB = ((0, 512), (512, 1024), (1024, FFN_TF))


def _rms_bwd(dy_times_g, xh, r):
    return r * (dy_times_g - xh * jnp.mean(dy_times_g * xh, axis=-1, keepdims=True))


def _ffn_fwd(h2, x1, target, gf, w_g_t, w_u_t, w_d):
    T = h2.shape[0]
    tm, tf = FFN_TM, FFN_TF
    nf = D_FF // tf

    def body(h_ref, x1_ref, t_ref, gf_ref, wg_ref, wu_ref, wd_ref,
             a_ref, b_ref, f_ref, dx2_ref, dx2b_ref, loss_ref, gnf_ref, acc):
        i, j = pl.program_id(0), pl.program_id(1)
        h = h_ref[...]

        @pl.when(j == 0)
        def _():
            acc[...] = x1_ref[...]

        for lo, hi in FFN_SUB:
            a = _dot_nt(h, wg_ref[lo:hi, :])
            b = _dot_nt(h, wu_ref[lo:hi, :])
            f = (a * _sigmoid(a) * b).astype(BF16)
            a_ref[:, lo:hi] = a.astype(BF16)
            b_ref[:, lo:hi] = b.astype(BF16)
            f_ref[:, lo:hi] = f
            acc[...] += _dot(f, wd_ref[lo:hi, :])

        @pl.when((i == 0) & (j == nf - 1))
        def _():
            loss_ref[...] = jnp.zeros_like(loss_ref)
            gnf_ref[...] = jnp.zeros_like(gnf_ref)

        @pl.when(j == nf - 1)
        def _():
            x2 = acc[...]
            r = lax.rsqrt(jnp.mean(x2 * x2, axis=-1, keepdims=True) + RMS_EPS)
            xh = x2 * r
            err = xh * gf_ref[...] - t_ref[...]
            loss_ref[...] += (0.5 / D_MODEL) * jnp.sum(err * err)
            dy = err * (1.0 / D_MODEL)
            gnf_ref[...] += _rowsum(dy * xh)
            dx2 = _rms_bwd(dy * gf_ref[...], xh, r)
            dx2_ref[...] = dx2
            dx2b_ref[...] = dx2.astype(BF16)

    row = lambda n: pl.BlockSpec((tm, n), lambda i, j: (i, 0))
    ffb = pl.BlockSpec((tm, tf), lambda i, j: (i, j))
    wblk = pl.BlockSpec((tf, D_MODEL), lambda i, j: (j, 0))
    return pl.pallas_call(
        body, grid=(T // tm, nf),
        in_specs=[row(D_MODEL), row(D_MODEL), row(D_MODEL), pl.BlockSpec((1, D_MODEL), lambda i, j: (0, 0)),
                  wblk, wblk, wblk],
        out_specs=[ffb, ffb, ffb, row(D_MODEL), row(D_MODEL),
                   pl.BlockSpec((1, 128), lambda i, j: (0, 0)), pl.BlockSpec((1, D_MODEL), lambda i, j: (0, 0))],
        out_shape=[SDS((T, D_FF), BF16), SDS((T, D_FF), BF16), SDS((T, D_FF), BF16), SDS((T, D_MODEL), F32),
                   SDS((T, D_MODEL), BF16), SDS((1, 128), F32), SDS((1, D_MODEL), F32)],
        scratch_shapes=[pltpu.VMEM((tm, D_MODEL), F32)],
        compiler_params=_params("arbitrary", "arbitrary"), name="ffn_fwd")(h2, x1, target, gf, w_g_t, w_u_t, w_d)


def _ffn_bwd(dx2b, dx2, a, b, x1, g2, w_g_t, w_u_t, w_d):
    T = dx2.shape[0]
    tm, tf = FFN_TM, FFN_TF
    nf = D_FF // tf

    def body(dxb_ref, dx2_ref, a_ref, b_ref, x1_ref, g2_ref, wg_ref, wu_ref, wd_ref,
             da_ref, db_ref, dx1_ref, dx1b_ref, gn2_ref, acc):
        i, j = pl.program_id(0), pl.program_id(1)
        @pl.when(j == 0)
        def _():
            acc[...] = jnp.zeros_like(acc)

        dxb = dxb_ref[...]
        for lo, hi in FFN_SUB:
            df = _dot_nt(dxb, wd_ref[lo:hi, :])
            av = a_ref[:, lo:hi].astype(F32)
            bv = b_ref[:, lo:hi].astype(F32)
            sg = _sigmoid(av)
            db = (df * av * sg).astype(BF16)
            da = (df * bv * (sg * (1.0 + av * (1.0 - sg)))).astype(BF16)
            da_ref[:, lo:hi] = da
            db_ref[:, lo:hi] = db
            acc[...] += _dot(da, wg_ref[lo:hi, :]) + _dot(db, wu_ref[lo:hi, :])

        @pl.when((i == 0) & (j == nf - 1))
        def _():
            gn2_ref[...] = jnp.zeros_like(gn2_ref)

        @pl.when(j == nf - 1)
        def _():
            dh2 = acc[...]
            x1 = x1_ref[...]
            r = lax.rsqrt(jnp.mean(x1 * x1, axis=-1, keepdims=True) + RMS_EPS)
            xh = x1 * r
            gn2_ref[...] += _rowsum(dh2 * xh)
            dx1 = dx2_ref[...] + _rms_bwd(dh2 * g2_ref[...], xh, r)
            dx1_ref[...] = dx1
            dx1b_ref[...] = dx1.astype(BF16)

    row = lambda n: pl.BlockSpec((tm, n), lambda i, j: (i, 0))
    ffb = pl.BlockSpec((tm, tf), lambda i, j: (i, j))
    wblk = pl.BlockSpec((tf, D_MODEL), lambda i, j: (j, 0))
    return pl.pallas_call(
        body, grid=(T // tm, nf),
        in_specs=[row(D_MODEL), row(D_MODEL), ffb, ffb, row(D_MODEL), pl.BlockSpec((1, D_MODEL), lambda i, j: (0, 0)),
                  wblk, wblk, wblk],
        out_specs=[ffb, ffb, row(D_MODEL), row(D_MODEL), pl.BlockSpec((1, D_MODEL), lambda i, j: (0, 0))],
        out_shape=[SDS((T, D_FF), BF16), SDS((T, D_FF), BF16), SDS((T, D_MODEL), F32), SDS((T, D_MODEL), BF16),
                   SDS((1, D_MODEL), F32)],
        scratch_shapes=[pltpu.VMEM((tm, D_MODEL), F32)],
        compiler_params=_params("arbitrary", "arbitrary"), name="ffn_bwd")(dx2b, dx2, a, b, x1, g2, w_g_t, w_u_t, w_d)


def _in_bwd(pieces, w_in_t, x, dx1, g1, dep=None):
    T = x.shape[0]
    tm = IN_TM
    npc = len(pieces)
    assert sum(p.shape[1] for p in pieces) == IN_WIDTH

    def body(*refs):
        p_refs = refs[:npc]
        w_hbm, x_ref, dx1_ref, g_ref, dx_ref, gn1_ref, w_vmem, sem = refs[npc:]

        @pl.when(pl.program_id(0) == 0)
        def _():
            cp = pltpu.make_async_copy(w_hbm, w_vmem, sem)
            cp.start()
            cp.wait()
            gn1_ref[...] = jnp.zeros_like(gn1_ref)

        dh = jnp.zeros((tm, D_MODEL), F32)
        col = 0
        for p_ref in p_refs:
            for j in range(p_ref.shape[1] // IN_CHUNK):
                dh = dh + _dot(p_ref[:, j * IN_CHUNK:(j + 1) * IN_CHUNK], w_vmem[col:col + IN_CHUNK, :])
                col += IN_CHUNK
        xv = x_ref[...]
        r = lax.rsqrt(jnp.mean(xv * xv, axis=-1, keepdims=True) + RMS_EPS)
        xh = xv * r
        gn1_ref[...] += _rowsum(dh * xh)
        dx_ref[...] = dx1_ref[...] + _rms_bwd(dh * g_ref[...], xh, r)

    row = lambda n: pl.BlockSpec((tm, n), lambda i: (i, 0))
    body, dep_spec, dep_arg = _anchored(body, npc + 4, dep)
    return pl.pallas_call(
        body, grid=(T // tm,),
        in_specs=[row(p.shape[1]) for p in pieces]
        + [pl.BlockSpec(memory_space=pl.ANY), row(D_MODEL), row(D_MODEL), pl.BlockSpec((1, D_MODEL), lambda i: (0, 0))]
        + dep_spec,
        out_specs=[row(D_MODEL), pl.BlockSpec((1, D_MODEL), lambda i: (0, 0))],
        out_shape=[SDS((T, D_MODEL), F32), SDS((1, D_MODEL), F32)],
        scratch_shapes=[pltpu.VMEM((IN_WIDTH, D_MODEL), BF16), pltpu.SemaphoreType.DMA],
        compiler_params=_params("arbitrary"), name="in_bwd")(*pieces, w_in_t, x, dx1, g1, *dep_arg)


def _local_step(x, target, in_proj, small, late_weights=None, emit=None):
    T = x.shape[0]
    batch = T // SEQ
    slopes_r = jnp.asarray(_slopes_times_dilation())
    emit = emit or (lambda names, grads: None)

    h, proj, w = in_proj()
    proj3 = proj.reshape(batch, SEQ, IN_WIDTH)

    att, lse = _attn_fwd(proj3, slopes_r, batch, w.get("token"))
    att = att.reshape(T, ATTN_OUT)
    if late_weights is not None:
        w = {**w, **late_weights("after_attention", att)}

    c1 = _conv_fwd(proj3, w["conv_w"], small["conv_b"], batch, w.get("token")).reshape(T, D_MODEL)
    if late_weights is not None:
        w = {**w, **late_weights(LATE_MERGE, (att, c1))}

    c3, ya, yc, mix, x1, h2 = _mid_fwd(
        att, c1, proj, x, w["w_attn_out"], w["w_conv_out"], w["w_o"],
        small["gate_b"], small["conv_ln_g"], small["conv_ln_b"], small["norm2_g"], w.get("token"))
    if late_weights is not None:
        w = {**w, **late_weights(LATE_FFN, h2)}

    a, b, f, dx2, dx2b, loss, g_normf = _ffn_fwd(h2, x1, target, small["norm_f_g"],
                                                   w["w_ffn_gate"], w["w_ffn_up"], w["w_ffn_down"])

    da, db, dx1, dx1b, g_norm2 = _ffn_bwd(dx2b, dx2, a, b, x1, small["norm2_g"],
                                           w["w_ffn_gate"], w["w_ffn_up"], w["w_ffn_down"])
    gw = {}
    gw["w_ffn_down"] = _mm_tn(f, dx2b, BF16, "gw_ffn_down", tn=1024)
    gw["w_ffn_gate"] = _mm_tn(da, h2, BF16, "gw_ffn_gate", tn=1024)
    gw["w_ffn_up"] = _mm_tn(db, h2, BF16, "gw_ffn_up", tn=1024)
    token = emit(("w_ffn_gate", "w_ffn_up", "w_ffn_down"), gw)

    head_ones = jnp.asarray(np.kron(np.eye(HEADS_PER_GROUP, dtype=np.float32), np.ones((HEAD_DIM, HEAD_DIM), np.float32)))
    dlogits, dya, dyc, datt, dsum, dc1, g_gate_b, g_ln_g, g_ln_b = _mid_bwd(
        dx1b, ya, yc, proj, att, c1, w["w_attn_out"], w["w_conv_out"], w["w_o"],
        small["gate_b"], small["conv_ln_g"], small["conv_ln_b"], head_ones, token)
    gw["w_o"] = _mm_tn(mix, dx1b, BF16, "gw_o", tn=1024)
    gw["w_attn_out"] = _mm_tn(att, dya, BF16, "gw_attn_out", tn=1024)
    gw["w_conv_out"] = _mm_tn(c3, dyc, BF16, "gw_conv_out", tn=1024)
    token = emit(("w_conv_out", "w_attn_out", "w_o"), gw)

    dua, dub, g_conv_w, g_conv_b = _conv_bwd(proj3, dc1.reshape(batch, SEQ, D_MODEL), w["conv_w"], batch, token)

    dq, dk, dv = _attn_bwd(proj3, datt.reshape(batch, SEQ, ATTN_OUT), lse, dsum.reshape(batch, SEQ, ATTN_OUT),
                           slopes_r, batch)
    pieces = [dq.reshape(T, ATTN_WIDTH), dk.reshape(T, ATTN_WIDTH), dv.reshape(T, ATTN_WIDTH),
              dua.reshape(T, D_MODEL), dub.reshape(T, D_MODEL), dlogits]

    names = ("q", "k", "v", "ua", "ub", "gate")
    gw["w_in"] = jnp.concatenate([_mm_tn(p, h, BF16, "gw_in_" + nm, tn=1024) for nm, p in zip(names, pieces)], axis=0)
    gw["conv_w"] = g_conv_w
    token = emit(("w_in", "conv_w"), gw)
    grad_x, g_norm1 = _in_bwd(pieces, w["w_in"], x, dx1, small["norm1_g"], token)

    gsmall = {"norm1_g": g_norm1, "gate_b": g_gate_b, "conv_b": g_conv_b, "conv_ln_g": g_ln_g, "conv_ln_b": g_ln_b,
              "norm2_g": g_norm2, "norm_f_g": g_normf}
    return loss, grad_x, gw, gsmall


ANY = pl.BlockSpec(memory_space=pl.ANY)


def _all_gather(arrs):
    n = len(arrs)

    def body(*refs):
        ins, outs = refs[:n], refs[n:2 * n]
        send_sems, recv_sems, local_sems = refs[2 * n:]
        x, y, c = lax.axis_index("x"), lax.axis_index("y"), lax.axis_index("c")
        me, sibling = (x, y, c), (x, y, 1 - c)
        chips = [(1 - x, y), (x, 1 - y), (1 - x, 1 - y)]

        def copy(a, k, block, to, src=None):
            px, py, pc = block
            dst = outs[a].at[4 * px + 2 * py + pc]
            return pltpu.make_async_remote_copy(
                src_ref=dst if src is None else src, dst_ref=dst,
                send_sem=send_sems.at[a, k], recv_sem=recv_sems.at[a, k], device_id=to, device_id_type=MESH)

        mine = [pltpu.make_async_copy(ins[a], outs[a].at[4 * x + 2 * y + c], local_sems.at[a]) for a in range(n)]
        for cp in mine:
            cp.start()
        first = []
        for j, chip in enumerate(chips):
            first += [copy(a, 1 + j, me, (*chip, c), src=ins[a]) for a in range(n)]
        first += [copy(a, 0, me, sibling, src=ins[a]) for a in range(n)]
        for cp in first:
            cp.start()
        passed = []
        for j, chip in enumerate(chips):
            for a in range(n):
                copy(a, 1 + j, (*chip, c), me).wait_recv()
                cp = copy(a, 4 + j, (*chip, c), sibling)
                cp.start()
                passed.append(cp)
        for a in range(n):
            copy(a, 0, sibling, me).wait_recv()
        for j, chip in enumerate(chips):
            for a in range(n):
                copy(a, 4 + j, (*chip, 1 - c), me).wait_recv()
        for cp in first + passed:
            cp.wait_send()
        for cp in mine:
            cp.wait()

    return pl.pallas_call(
        body, in_specs=[ANY] * n, out_specs=[ANY] * n,
        out_shape=[SDS((N_DEV,) + a.shape, a.dtype) for a in arrs],
        scratch_shapes=[pltpu.SemaphoreType.DMA((n, 7)), pltpu.SemaphoreType.DMA((n, 7)), pltpu.SemaphoreType.DMA((n,))],
        name="all_gather_weights")(*arrs)


HBM =pl.BlockSpec(memory_space=pltpu.HBM)
SEM = pl.BlockSpec(memory_space=pltpu.SEMAPHORE)
ALL_PEERS = tuple(range(1, N_DEV))
OTHER_CHIPS = (2, 4, 6)
SPLIT_EFFECT = pltpu.CompilerParams(has_side_effects=pltpu.SideEffectType.DATAFLOW_SIDE_EFFECTING)


def _exchange_copies(mode, ks, srcs, lands, send_sems, recv_sems):
    x, y, c = lax.axis_index("x"), lax.axis_index("y"), lax.axis_index("c")
    me = 4 * x + 2 * y + c
    send, recv = [], []
    for a in range(len(lands)):
        for i, k in enumerate(ks):
            peer = (x ^ ((k >> 2) & 1), y ^ ((k >> 1) & 1), c ^ (k & 1))
            pidx = 4 * peer[0] + 2 * peer[1] + peer[2]
            if mode == "gather":
                src, to, out_slot, in_slot = srcs[a], peer, me, pidx
            elif mode == "scatter":
                src, to, out_slot, in_slot = srcs[a].at[pidx], peer, me, pidx
            elif mode == "chip_scatter":
                src, to, out_slot, in_slot = srcs[a].at[pidx >> 1], peer, me >> 1, pidx >> 1
            else:
                src, to, out_slot, in_slot = lands[a].at[pidx], (x, y, 1 - c), pidx, pidx ^ 1
            s = a * len(ks) + i
            send.append(pltpu.make_async_remote_copy(
                src_ref=src, dst_ref=lands[a].at[out_slot], send_sem=send_sems.at[s], recv_sem=recv_sems.at[s],
                device_id=to, device_id_type=MESH))
            recv.append(pltpu.make_async_remote_copy(
                src_ref=src, dst_ref=lands[a].at[in_slot], send_sem=send_sems.at[s], recv_sem=recv_sems.at[s],
                device_id=to, device_id_type=MESH))
    return send, recv


def _send_start(mode, ks, name, srcs=(), lands=None, dep=None):
    srcs = list(srcs)
    if lands is None:
        slots = 4 if mode == "chip_scatter" else N_DEV
        lands = [lax.empty((slots,) + (s.shape if mode == "gather" else s.shape[1:]), s.dtype) for s in srcs]
    ns, nl = len(srcs), len(lands)
    nsem = nl * len(ks)

    def body(*refs):
        send, _ = _exchange_copies(mode, ks, refs[:ns], refs[ns:ns + nl], refs[ns + nl], refs[ns + nl + 1])
        for cp in send:
            cp.start()
        token = refs[-1]
        token[...] = jnp.zeros_like(token)

    both = srcs + list(lands)
    body, dep_spec, dep_arg = _anchored(body, ns + nl, dep)
    res = pl.pallas_call(
        body, name=name,
        out_shape=(pltpu.SemaphoreType.DMA((nsem,)), pltpu.SemaphoreType.DMA((nsem,)),
                   *[pltpu.HBM(a.shape, a.dtype) for a in both], SDS((8, 128), F32)),
        in_specs=[HBM] * (ns + nl) + dep_spec,
        out_specs=(SEM, SEM, *([HBM] * (ns + nl)), pl.BlockSpec(memory_space=pltpu.VMEM)),
        input_output_aliases={i: 2 + i for i in range(ns + nl)}, compiler_params=SPLIT_EFFECT,
    )(*[pltpu.with_memory_space_constraint(a, pltpu.HBM) for a in both], *dep_arg)
    return dict(mode=mode, ks=ks, send_sems=res[0], recv_sems=res[1], srcs=res[2:2 + ns], lands=res[2 + ns:2 + ns + nl],
                token=res[-1])


def _send_wait(started, after, name):
    ns, nl = len(started["srcs"]), len(started["lands"])

    def body(*refs):
        send, recv = _exchange_copies(started["mode"], started["ks"], refs[:ns], refs[ns:ns + nl],
                                      refs[ns + nl], refs[ns + nl + 1])
        for cp in send:
            cp.wait_send()
        for cp in recv:
            cp.wait_recv()

    both = list(started["srcs"]) + list(started["lands"])
    after = after if isinstance(after, (tuple, list)) else (after,)
    res = pl.pallas_call(
        body, name=name,
        out_shape=tuple(pltpu.HBM(a.shape, a.dtype) for a in both),
        in_specs=[HBM] * (ns + nl) + [SEM, SEM] + [ANY] * len(after), out_specs=tuple([HBM] * (ns + nl)),
        input_output_aliases={i: i for i in range(ns + nl)}, compiler_params=SPLIT_EFFECT,
    )(*both, started["send_sems"], started["recv_sems"], *after)
    return res[:ns], res[ns:]


def _exchange_sibling(gs):
    n = len(gs)

    def body(*refs):
        ins, outs = refs[:n], refs[n:2 * n]
        send_sems, recv_sems = refs[2 * n:]
        x, y, c = lax.axis_index("x"), lax.axis_index("y"), lax.axis_index("c")
        copies = []
        for a in range(n):
            for j in range(4):
                copies.append(pltpu.make_async_remote_copy(
                    src_ref=ins[a].at[2 * j + (1 - c)], dst_ref=outs[a].at[j],
                    send_sem=send_sems.at[a, j], recv_sem=recv_sems.at[a, j],
                    device_id=(x, y, 1 - c), device_id_type=MESH))
        for cp in copies:
            cp.start()
        for cp in copies:
            cp.wait_recv()
        for cp in copies:
            cp.wait_send()

    return pl.pallas_call(
        body, in_specs=[ANY] * n, out_specs=[ANY] * n,
        out_shape=[SDS((4,) + g.shape[1:], g.dtype) for g in gs],
        scratch_shapes=[pltpu.SemaphoreType.DMA((n, 4)), pltpu.SemaphoreType.DMA((n, 4))],
        name="reduce_scatter_sibling")(*gs)


def _add_pair(g, r1, core, name):
    _, rows, cols = g.shape
    tr = _row_tile(rows, cols, 3 * g.dtype.itemsize)

    def body(c_ref, g_ref, r_ref, o_ref):
        o_ref[...] = (g_ref[...].astype(F32) + r_ref[...].astype(F32)).astype(o_ref.dtype)

    return pl.pallas_call(
        body,
        grid_spec=pltpu.PrefetchScalarGridSpec(
            num_scalar_prefetch=1, grid=(4, rows // tr),
            in_specs=[pl.BlockSpec((1, tr, cols), lambda j, i, c_ref: (2 * j + c_ref[0], i, 0)),
                      pl.BlockSpec((1, tr, cols), lambda j, i, c_ref: (j, i, 0))],
            out_specs=pl.BlockSpec((1, tr, cols), lambda j, i, c_ref: (j, i, 0))),
        out_shape=SDS((4, rows, cols), g.dtype),
        compiler_params=_params("parallel", "parallel"), name=name)(core, g, r1)


def _row_tile(rows, cols, itemsize_total):
    budget = (4 << 20) // max(1, cols * itemsize_total)
    if rows <= budget:
        return rows
    t = rows
    while t > budget and t % 2 == 0 and (t // 2) % 16 == 0:
        t //= 2
    return t


def _adam_math(g, w, m, v):
    m_new = ADAM_B1 * m + (1.0 - ADAM_B1) * g
    v_new = ADAM_B2 * v + (1.0 - ADAM_B2) * (g * g)
    m_hat = m_new / (1.0 - ADAM_B1 ** ADAM_STEP)
    v_hat = v_new / (1.0 - ADAM_B2 ** ADAM_STEP)
    delta = -ADAM_LR * (m_hat / (jnp.sqrt(v_hat) + ADAM_EPS) + ADAM_WD * w)
    return delta, m_new, v_new


def _sum_adam(parts, own, mine, w, m, v, name):
    rows, cols = w.shape
    nparts = parts.shape[0]
    tr = _row_tile(rows, cols, (nparts + 1) * parts.dtype.itemsize + 7 * 4)

    def body(mine_ref, p_ref, own_ref, w_ref, m_ref, v_ref, g_ref, d_ref, mo_ref, vo_ref):
        g = None
        for s in range(nparts):
            part = jnp.where(mine_ref[0] == s, own_ref[0], p_ref[s]).astype(F32)
            g = part if g is None else g + part
        delta, m_new, v_new = _adam_math(g, w_ref[...], m_ref[...], v_ref[...])
        g_ref[...] = g
        d_ref[...] = delta
        mo_ref[...] = m_new
        vo_ref[...] = v_new

    blk = pl.BlockSpec((tr, cols), lambda i, mine_ref: (i, 0))
    out = SDS((rows, cols), F32)
    return pl.pallas_call(
        body,
        grid_spec=pltpu.PrefetchScalarGridSpec(
            num_scalar_prefetch=1, grid=(rows // tr,),
            in_specs=[pl.BlockSpec((nparts, tr, cols), lambda i, mine_ref: (0, i, 0)),
                      pl.BlockSpec((1, tr, cols), lambda i, mine_ref: (mine_ref[0], i, 0)), blk, blk, blk],
            out_specs=[blk, blk, blk, blk]),
        out_shape=[out, out, out, out],
        compiler_params=_params("parallel"), name=name)(mine, parts, own, w, m, v)


SMALL_ROWS = 72


def _small_allreduce_adam(gpart, w, m, v, row_counts, dep=None):
    def reduce_body(g_ref, go_ref, gath, send_sems, recv_sems):
        x, y, c = lax.axis_index("x"), lax.axis_index("y"), lax.axis_index("c")
        me = 4 * x + 2 * y + c
        gath[me] = g_ref[...]
        copies = []
        for k in range(1, N_DEV):
            fx, fy, fc = (k >> 2) & 1, (k >> 1) & 1, k & 1
            peer = (x ^ fx, y ^ fy, c ^ fc)
            copies.append(pltpu.make_async_remote_copy(
                src_ref=gath.at[me], dst_ref=gath.at[me], send_sem=send_sems.at[k - 1], recv_sem=recv_sems.at[k - 1],
                device_id=peer, device_id_type=MESH))
        for cp in copies:
            cp.start()
        for cp in copies:
            cp.wait_recv()
        for cp in copies:
            cp.wait_send()
        g = gath[0]
        for d in range(1, N_DEV):
            g = g + gath[d]
        go_ref[...] = g

    def adam_body(g_ref, w_ref, m_ref, v_ref, *out_refs):
        g = g_ref[...]
        delta, m_new, v_new = _adam_math(g, w_ref[...], m_ref[...], v_ref[...])
        outs = iter(out_refs)
        for val in (g, delta, m_new, v_new):
            lo = 0
            for r in row_counts:
                next(outs)[...] = val[lo:lo + r]
                lo += r
        next(outs)[...] = g[SMALL_ROWS - SUBLANES:]

    vm = pl.BlockSpec(memory_space=pltpu.VMEM)
    reduce_body, dep_spec, dep_arg = _anchored(reduce_body, 1, dep)
    total = pl.pallas_call(
        reduce_body, in_specs=[vm] + dep_spec, out_specs=vm, out_shape=SDS((SMALL_ROWS, 128), F32),
        scratch_shapes=[pltpu.VMEM((N_DEV, SMALL_ROWS, 128), F32), pltpu.SemaphoreType.DMA((N_DEV - 1,)),
                        pltpu.SemaphoreType.DMA((N_DEV - 1,))],
        name="small_allreduce")(gpart, *dep_arg)
    out_shape = [SDS((r, 128), F32) for _ in range(4) for r in row_counts] + [SDS((SUBLANES, 128), F32)]
    res = pl.pallas_call(adam_body, in_specs=[vm] * 4, out_specs=[vm] * len(out_shape), out_shape=out_shape,
                         name="small_adam")(total, w, m, v)
    k = len(row_counts)
    return [res[i * k:(i + 1) * k] for i in range(4)], res[-1]


BIG = ("w_in", "conv_w", "w_conv_out", "w_attn_out", "w_o", "w_ffn_gate", "w_ffn_up", "w_ffn_down")
LATE_MERGE = ("w_conv_out", "w_attn_out", "w_o")
LATE_FFN = ("w_ffn_gate", "w_ffn_up", "w_ffn_down")
TRANSPOSED = ("w_in", "w_ffn_gate", "w_ffn_up")
COL_SHARDED = ("conv_w", "w_attn_out")
SMALL = ("norm1_g", "gate_b", "conv_b", "conv_ln_g", "conv_ln_b", "norm2_g", "norm_f_g")
WEIGHTS = ("norm1_g", "w_in", "gate_b", "conv_w", "conv_b", "conv_ln_g", "conv_ln_b", "w_conv_out", "w_attn_out", "w_o",
           "norm2_g", "w_ffn_gate", "w_ffn_up", "w_ffn_down", "norm_f_g")


def _shard2d(name, a):
    a = a.reshape(a.shape[-2], a.shape[-1])
    if name in TRANSPOSED:
        a = a.T
    if name == "conv_w":
        a = jnp.pad(a, ((0, CONV_PAD - CONV_K), (0, 0)))
    return a


def _from_shard2d(name, val, shape):
    if name in TRANSPOSED:
        val = val.T
    if name == "conv_w":
        val = val[:CONV_K]
    return val.reshape(shape)


def _gathered_to_full(name, g):
    if name in COL_SHARDED:
        return g.transpose(1, 0, 2).reshape(g.shape[1], N_DEV * g.shape[2])
    return g.reshape(N_DEV * g.shape[1], g.shape[2])


def _full_to_blocks(name, g):
    if name in COL_SHARDED:
        return g.reshape(g.shape[0], N_DEV, g.shape[1] // N_DEV).transpose(1, 0, 2)
    return g.reshape(N_DEV, g.shape[0] // N_DEV, g.shape[1])


def _pack_small(d, last_rows):
    vec = jnp.concatenate([d[n].reshape(-1) for n in SMALL]).reshape(SMALL_ROWS - SUBLANES, 128)
    return jnp.concatenate([vec, last_rows], axis=0)


def kernel(x, norm1_g, w_in, gate_b, conv_w, conv_b, conv_ln_g, conv_ln_b, w_conv_out, w_attn_out, w_o, norm2_g, w_ffn_gate, w_ffn_up, w_ffn_down, norm_f_g, loss_target, m_norm1_g, m_w_in, m_gate_b, m_conv_w, m_conv_b, m_conv_ln_g, m_conv_ln_b, m_w_conv_out, m_w_attn_out, m_w_o, m_norm2_g, m_w_ffn_gate, m_w_ffn_up, m_w_ffn_down, m_norm_f_g, v_norm1_g, v_w_in, v_gate_b, v_conv_w, v_conv_b, v_conv_ln_g, v_conv_ln_b, v_w_conv_out, v_w_attn_out, v_w_o, v_norm2_g, v_w_ffn_gate, v_w_ffn_up, v_w_ffn_down, v_norm_f_g):
    wts = dict(norm1_g=norm1_g, w_in=w_in, gate_b=gate_b, conv_w=conv_w, conv_b=conv_b, conv_ln_g=conv_ln_g,
               conv_ln_b=conv_ln_b, w_conv_out=w_conv_out, w_attn_out=w_attn_out, w_o=w_o, norm2_g=norm2_g,
               w_ffn_gate=w_ffn_gate, w_ffn_up=w_ffn_up, w_ffn_down=w_ffn_down, norm_f_g=norm_f_g)
    mom1 = dict(norm1_g=m_norm1_g, w_in=m_w_in, gate_b=m_gate_b, conv_w=m_conv_w, conv_b=m_conv_b, conv_ln_g=m_conv_ln_g,
                conv_ln_b=m_conv_ln_b, w_conv_out=m_w_conv_out, w_attn_out=m_w_attn_out, w_o=m_w_o, norm2_g=m_norm2_g,
                w_ffn_gate=m_w_ffn_gate, w_ffn_up=m_w_ffn_up, w_ffn_down=m_w_ffn_down, norm_f_g=m_norm_f_g)
    mom2 = dict(norm1_g=v_norm1_g, w_in=v_w_in, gate_b=v_gate_b, conv_w=v_conv_w, conv_b=v_conv_b, conv_ln_g=v_conv_ln_g,
                conv_ln_b=v_conv_ln_b, w_conv_out=v_w_conv_out, w_attn_out=v_w_attn_out, w_o=v_w_o, norm2_g=v_norm2_g,
                w_ffn_gate=v_w_ffn_gate, w_ffn_up=v_w_ffn_up, w_ffn_down=v_w_ffn_down, norm_f_g=v_norm_f_g)

    T = x.shape[0] * x.shape[1]
    x2 = x.reshape(T, D_MODEL)
    t2 = loss_target.reshape(T, D_MODEL)

    me = 4 * lax.axis_index("x") + 2 * lax.axis_index("y") + lax.axis_index("c")
    shards = {n: _shard2d(n, wts[n]) for n in BIG}
    sent = {n: shards[n] if n == "conv_w" else shards[n].astype(BF16) for n in BIG}
    small = {n: wts[n].reshape(1, -1) for n in SMALL}

    stage = {}

    def in_proj():
        w_in_blocks, conv_blocks = _all_gather([sent["w_in"], sent["conv_w"]])
        near = (1,) + OTHER_CHIPS
        stage["merge"] = _send_start("gather", near, "gather_start_merge", [sent[n] for n in LATE_MERGE], dep=w_in_blocks)
        stage["ffn"] = _send_start("gather", near, "gather_start_ffn", [sent[n] for n in LATE_FFN],
                                   dep=stage["merge"]["token"])
        w_in_t = _gathered_to_full("w_in", w_in_blocks)
        h, proj = _in_proj(x2, small["norm1_g"], w_in_t, stage["ffn"]["token"])
        return h, proj, {"w_in": w_in_t, "conv_w": _gathered_to_full("conv_w", conv_blocks)}

    def filled(names, srcs, lands):
        return {n: _gathered_to_full(n, lax.dynamic_update_slice(land, src[None], (me, 0, 0)))
                for n, src, land in zip(names, srcs, lands)}

    def pass_on(group, after):
        stage[group + "_srcs"], lands = _send_wait(stage[group], after, "gather_wait_" + group)
        stage[group + "_forward"] = _send_start("forward", OTHER_CHIPS, "forward_start_" + group, lands=lands)
        return stage[group + "_forward"]["token"]

    def arrived(group, names, after):
        _, lands = _send_wait(stage[group + "_forward"], after, "forward_wait_" + group)
        return filled(names, stage[group + "_srcs"], lands)

    def late_weights(which, after):
        if which == "after_attention":
            return {"token": pass_on("merge", after)}
        if which is LATE_MERGE:
            return {**arrived("merge", LATE_MERGE, after), "token": pass_on("ffn", after)}
        return arrived("ffn", LATE_FFN, after)

    scatters = []
    core = lax.axis_index("c").astype(jnp.int32).reshape(1)

    def emit(names, gw):
        blocks = [_full_to_blocks(n, gw[n]) for n in names]
        if "w_in" in names:
            sums = [_add_pair(g, r, core, "chip_sum_" + n) for n, g, r in zip(names, blocks, _exchange_sibling(blocks))]
            started = _send_start("chip_scatter", OTHER_CHIPS, "scatter_start_" + names[0], sums)
        else:
            started = _send_start("scatter", ALL_PEERS, "scatter_start_" + names[0], blocks)
        scatters.append((names, started))
        return started["token"]

    loss_part, grad_x, gw, gsmall = _local_step(x2, t2, in_proj, small, late_weights, emit)

    grads, deltas, new_m, new_v = {}, {}, {}, {}
    after = grad_x
    for names, started in scatters:
        srcs, lands = _send_wait(started, after, "scatter_wait_" + names[0])
        mine = (me >> 1 if started["mode"] == "chip_scatter" else me).astype(jnp.int32).reshape(1)
        for n, src, land in zip(names, srcs, lands):
            g, d, mo, vo = _sum_adam(land, src, mine, shards[n], _shard2d(n, mom1[n]), _shard2d(n, mom2[n]), "adam_" + n)
            for dst, val in ((grads, g), (deltas, d), (new_m, mo), (new_v, vo)):
                dst[n] = _from_shard2d(n, val, wts[n].shape)
            after = g

    zeros, ones = jnp.zeros((SUBLANES, 128), F32), jnp.ones((SUBLANES, 128), F32)
    row_counts = [wts[n].size // 128 for n in SMALL]
    kinds, loss_rows = _small_allreduce_adam(
        _pack_small(gsmall, jnp.broadcast_to(loss_part, (SUBLANES, 128))), _pack_small(wts, zeros),
        _pack_small(mom1, zeros), _pack_small(mom2, ones), row_counts, after)
    for dst, vals in zip((grads, deltas, new_m, new_v), kinds):
        dst.update({n: val.reshape(wts[n].shape) for n, val in zip(SMALL, vals)})
    loss = loss_rows[0, 0]
    return (loss, grad_x.reshape(x.shape), *[grads[n] for n in WEIGHTS], *[deltas[n] for n in WEIGHTS],
            *[new_m[n] for n in WEIGHTS], *[new_v[n] for n in WEIGHTS])
```

```python
import math

import numpy as np
import jax
import jax.numpy as jnp
from jax import lax
from jax.experimental import pallas as pl
from jax.experimental.pallas import tpu as pltpu

F32 = jnp.float32
BF16 = jnp.bfloat16
SDS = jax.ShapeDtypeStruct
MESH = pl.DeviceIdType.MESH

D_MODEL = 1024
SEQ = 2048
HEAD_DIM = 64
GROUPS = ((128, 1), (512, 4), (2048, 16))
HEADS_PER_GROUP = 8
N_HEADS = 24
ATTN_WIDTH = N_HEADS * HEAD_DIM
ATTN_OUT = HEADS_PER_GROUP * HEAD_DIM
CONV_K = 31
CONV_PAD = 32
D_FF = 2816
IN_WIDTH = 3 * ATTN_WIDTH + 2 * D_MODEL + 2 * D_MODEL
RMS_EPS = 1e-6
LN_EPS = 1e-5
Q_BLOCK = 128
LANES = 128
NEG = -1e30
N_DEV = 8

ADAM_LR = 0.001
ADAM_B1 = 0.9
ADAM_B2 = 0.999
ADAM_EPS = 1e-08
ADAM_WD = 0.01
ADAM_STEP = 10


def _alibi_slope_list(n):
    def pow2(m):
        start = 2.0 ** (-8.0 / m)
        return [start ** (i + 1) for i in range(m)]
    if math.log2(n).is_integer():
        return pow2(n)
    c = 2 ** math.floor(math.log2(n))
    return pow2(c) + _alibi_slope_list(2 * c)[0::2][: n - c]


def _slopes_times_dilation():
    s = np.asarray(sorted(_alibi_slope_list(N_HEADS), reverse=True), dtype=np.float32).reshape(3, HEADS_PER_GROUP)
    r = np.asarray([g[1] for g in GROUPS], dtype=np.float32)[:, None]
    return (s * r).reshape(N_HEADS)


def _sigmoid(x):
    return 0.5 * jnp.tanh(0.5 * x) + 0.5


def _dot(a, b):
    return jnp.dot(a, b, preferred_element_type=F32)


def _dot_nt(a, b):
    return lax.dot_general(a, b, (((1,), (1,)), ((), ())), preferred_element_type=F32)


def _dot_tn(a, b):
    return lax.dot_general(a, b, (((0,), (0,)), ((), ())), preferred_element_type=F32)


def _rowsum(x):
    return jnp.sum(x, axis=0, keepdims=True)


ANY_SPEC = pl.BlockSpec(memory_space=pl.ANY)


def _params(*sem):
    return pltpu.CompilerParams(dimension_semantics=sem)


def _anchored(body, n_in, dep):
    if dep is None:
        return body, [], []

    def wrapped(*refs):
        return body(*refs[:n_in], *refs[n_in + 1:])

    return wrapped, [pl.BlockSpec(memory_space=pl.ANY)], [dep]


IN_TM = 256
IN_CHUNK = 512


def _in_proj(x, g1, w_in_t, dep=None):
    T = x.shape[0]
    tm = IN_TM

    def body(x_ref, g_ref, w_hbm, h_ref, proj_ref, w_vmem, sem):
        @pl.when(pl.program_id(0) == 0)
        def _():
            cp = pltpu.make_async_copy(w_hbm, w_vmem, sem)
            cp.start()
            cp.wait()

        xv = x_ref[...]
        r = lax.rsqrt(jnp.mean(xv * xv, axis=-1, keepdims=True) + RMS_EPS)
        h = (xv * r * g_ref[...]).astype(BF16)
        h_ref[...] = h
        for lo in range(0, IN_WIDTH, IN_CHUNK):
            proj_ref[:, lo:lo + IN_CHUNK] = _dot_nt(h, w_vmem[lo:lo + IN_CHUNK, :])

    row = lambda n: pl.BlockSpec((tm, n), lambda i: (i, 0))
    body, dep_spec, dep_arg = _anchored(body, 3, dep)
    return pl.pallas_call(
        body, grid=(T // tm,),
        in_specs=[row(D_MODEL), pl.BlockSpec((1, D_MODEL), lambda i: (0, 0)), pl.BlockSpec(memory_space=pl.ANY)] + dep_spec,
        out_specs=[row(D_MODEL), row(IN_WIDTH)],
        out_shape=[SDS((T, D_MODEL), BF16), SDS((T, IN_WIDTH), F32)],
        scratch_shapes=[pltpu.VMEM((IN_WIDTH, D_MODEL), BF16), pltpu.SemaphoreType.DMA],
        compiler_params=_params("arbitrary"), name="in_proj")(x, g1, w_in_t, *dep_arg)


def _mm_tn(a, b, out_dtype, name, tn, tt=1024):
    T, K = a.shape
    N = b.shape[1]
    nt = T // tt

    def body(a_ref, b_ref, o_ref, acc):
        t = pl.program_id(1)

        @pl.when(t == 0)
        def _():
            acc[...] = jnp.zeros_like(acc)

        acc[...] += _dot_tn(a_ref[...], b_ref[...])

        @pl.when(t == nt - 1)
        def _():
            o_ref[...] = acc[...].astype(o_ref.dtype)

    return pl.pallas_call(
        body, grid=(N // tn, nt),
        in_specs=[pl.BlockSpec((tt, K), lambda j, t: (t, 0)),
                  pl.BlockSpec((tt, tn), lambda j, t: (t, j))],
        out_specs=pl.BlockSpec((K, tn), lambda j, t: (0, j)),
        out_shape=SDS((K, N), out_dtype),
        scratch_shapes=[pltpu.VMEM((K, tn), F32)],
        compiler_params=_params("parallel", "arbitrary"), name=name)(a, b)


def _gather_classes(src_ref, dst, r, row0=0):
    L = SEQ // r
    for c in range(r):
        dst[row0 + c * L:row0 + (c + 1) * L, :] = src_ref[0, pl.ds(c, L, stride=r), :].astype(dst.dtype)


def _scatter_classes(src, dst, r, row0=0):
    L = SEQ // r
    for c in range(r):
        dst[pl.ds(c, L, stride=r), :] = src[row0 + c * L:row0 + (c + 1) * L, :].astype(dst.dtype)


def _attn_masks(slope_r):
    qi = lax.broadcasted_iota(jnp.int32, (Q_BLOCK, Q_BLOCK), 0)
    kj = lax.broadcasted_iota(jnp.int32, (Q_BLOCK, Q_BLOCK), 1)
    rel = (qi - kj).astype(F32)
    bias_cur = jnp.where(qi >= kj, -slope_r * rel, NEG)
    bias_prev = jnp.where(qi <= kj, -slope_r * (rel + float(Q_BLOCK)), NEG)
    return bias_cur, bias_prev


def _store_biases(bias, sl_ref, g, hp):
    for hh in range(2):
        cur, prev = _attn_masks(sl_ref[g * HEADS_PER_GROUP + 2 * hp + hh])
        rows = slice(hh * Q_BLOCK, (hh + 1) * Q_BLOCK)
        bias[0, rows, 0:Q_BLOCK] = prev
        bias[1, rows, 0:Q_BLOCK] = jnp.full((Q_BLOCK, Q_BLOCK), NEG, F32)
        bias[0, rows, Q_BLOCK:] = cur
        bias[1, rows, Q_BLOCK:] = cur


def _transpose_pairs(src, dst):
    dst[0, :, 0:Q_BLOCK] = jnp.zeros((LANES, Q_BLOCK), dst.dtype)
    nblk = SEQ // Q_BLOCK
    for b in range(nblk):
        t = src[(b + 1) * Q_BLOCK:(b + 2) * Q_BLOCK, :].T
        dst[b, :, Q_BLOCK:] = t
        if b + 1 < nblk:
            dst[b + 1, :, 0:Q_BLOCK] = t


def _stack_heads(t, low):
    z = jnp.zeros_like(t)
    return jnp.concatenate([jnp.where(low, t, z), jnp.where(low, z, t)], axis=0)


def _unstack_heads(t2, low):
    return jnp.where(low, t2[:Q_BLOCK], t2[Q_BLOCK:])


def _unit_offsets(u, nb):
    off = pl.multiple_of(u * Q_BLOCK, Q_BLOCK)
    n = u & (nb - 1)
    c = u >> int(math.log2(nb))
    return off, n == 0, c, n


ATTN_UNROLL = 4


def _attn_fwd(qkv, slopes_r, batch, dep=None):
    nblk = SEQ // Q_BLOCK

    def body(sl_ref, *refs):
        qkv_refs = refs[:9]
        att_ref, lse_ref = refs[9:11]
        qd, kd, vd, kt, opos, lpos, bias = refs[11:]
        hp = pl.program_id(1)
        low = lax.broadcasted_iota(jnp.int32, (Q_BLOCK, LANES), 1) < HEAD_DIM

        for g in range(3):
            r = GROUPS[g][1]
            nb = SEQ // r // Q_BLOCK
            _gather_classes(qkv_refs[3 * g], qd, r)
            kd[0:Q_BLOCK, :] = jnp.zeros((Q_BLOCK, LANES), BF16)
            vd[0:Q_BLOCK, :] = jnp.zeros((Q_BLOCK, LANES), BF16)
            _gather_classes(qkv_refs[3 * g + 1], kd, r, Q_BLOCK)
            _gather_classes(qkv_refs[3 * g + 2], vd, r, Q_BLOCK)
            _transpose_pairs(kd, kt)
            _store_biases(bias, sl_ref, g, hp)

            def unit(u, carry, g=g, r=r, nb=nb):
                off, first, c, n = _unit_offsets(u, nb)
                q2 = _stack_heads(qd[pl.ds(off, Q_BLOCK), :], low)
                s = _dot(q2, kt[u]) * 0.125 + bias[first.astype(jnp.int32)]
                m = jnp.max(s, axis=-1, keepdims=True)
                p = jnp.exp(s - m)
                l = jnp.sum(p, axis=-1, keepdims=True)
                o2 = _dot(p.astype(BF16), vd[pl.ds(off, 2 * Q_BLOCK), :]) * (1.0 / l)
                lse2 = m + jnp.log(l)
                rows = pl.ds(c + n * (Q_BLOCK * r), Q_BLOCK, stride=r)
                opos[g, rows, :] = _unstack_heads(o2, low)
                lpos[g, rows, :] = jnp.where(low, lse2[:Q_BLOCK], lse2[Q_BLOCK:])
                return carry

            lax.fori_loop(0, nblk, unit, 0, unroll=ATTN_UNROLL)

        def merge(i, carry):
            rows = pl.ds(pl.multiple_of(i * 256, 256), 256)
            l0, l1, l2 = lpos[0, rows, :], lpos[1, rows, :], lpos[2, rows, :]
            m = jnp.maximum(jnp.maximum(l0, l1), l2)
            e0, e1, e2 = jnp.exp(l0 - m), jnp.exp(l1 - m), jnp.exp(l2 - m)
            den = e0 + e1 + e2
            att = (e0 * opos[0, rows, :] + e1 * opos[1, rows, :] + e2 * opos[2, rows, :]) / den
            att_ref[0, rows, :] = att.astype(att_ref.dtype)
            lse_ref[0, rows, :] = m + jnp.log(den)
            return carry

        lax.fori_loop(0, SEQ // 256, merge, 0)

    def col(sec, g):
        return pl.BlockSpec((1, SEQ, LANES), lambda b, hp: (b, 0, sec * 12 + g * 4 + hp))

    out = pl.BlockSpec((1, SEQ, LANES), lambda b, hp: (b, 0, hp))
    body, dep_spec, dep_arg = _anchored(body, 10, dep)
    return pl.pallas_call(
        body, grid=(batch, 4),
        in_specs=[pl.BlockSpec(memory_space=pltpu.SMEM)] + [col(sec, g) for g in range(3) for sec in range(3)] + dep_spec,
        out_specs=[out, out],
        out_shape=[SDS((batch, SEQ, ATTN_OUT), BF16), SDS((batch, SEQ, ATTN_OUT), F32)],
        scratch_shapes=[pltpu.VMEM((SEQ, LANES), BF16), pltpu.VMEM((Q_BLOCK + SEQ, LANES), BF16),
                        pltpu.VMEM((Q_BLOCK + SEQ, LANES), BF16), pltpu.VMEM((nblk, LANES, 2 * Q_BLOCK), BF16),
                        pltpu.VMEM((3, SEQ, LANES), F32), pltpu.VMEM((3, SEQ, LANES), F32),
                        pltpu.VMEM((2, 2 * Q_BLOCK, 2 * Q_BLOCK), F32)],
        compiler_params=_params("parallel", "parallel"), name="attn_fwd")(slopes_r, *([qkv] * 9), *dep_arg)


def _attn_bwd(qkv, datt, lse, dsum, slopes_r, batch):
    nblk = SEQ // Q_BLOCK

    def body(sl_ref, q_ref, k_ref, v_ref, do_ref, l_ref, d_ref, dq_ref, dk_ref, dv_ref,
             qd, kd, vd, dod, kt, vt, ld, dd, dq_acc, dk_acc, dv_acc, dk_part, dv_part, stage, bias):
        gid, hp = pl.program_id(1), pl.program_id(2)
        low = lax.broadcasted_iota(jnp.int32, (Q_BLOCK, LANES), 1) < HEAD_DIM

        def section(g):
            r = GROUPS[g][1]
            nb = SEQ // r // Q_BLOCK
            _gather_classes(q_ref, qd, r)
            kd[0:Q_BLOCK, :] = jnp.zeros((Q_BLOCK, LANES), BF16)
            vd[0:Q_BLOCK, :] = jnp.zeros((Q_BLOCK, LANES), BF16)
            _gather_classes(k_ref, kd, r, Q_BLOCK)
            _gather_classes(v_ref, vd, r, Q_BLOCK)
            _gather_classes(do_ref, dod, r)
            _gather_classes(l_ref, ld, r)
            _gather_classes(d_ref, dd, r)
            _transpose_pairs(kd, kt)
            _transpose_pairs(vd, vt)
            _store_biases(bias, sl_ref, g, hp)

            def unit(u, carry):
                off, first, _, _ = _unit_offsets(u, nb)
                pair = pl.ds(off, 2 * Q_BLOCK)
                q2 = _stack_heads(qd[pl.ds(off, Q_BLOCK), :], low)
                do2 = _stack_heads(dod[pl.ds(off, Q_BLOCK), :], low)
                lse_t = ld[pl.ds(off, Q_BLOCK), :]
                dsum_t = dd[pl.ds(off, Q_BLOCK), :]
                lse2 = jnp.concatenate([lse_t[:, 0:1], lse_t[:, HEAD_DIM:HEAD_DIM + 1]], axis=0)
                dsum2 = jnp.concatenate([dsum_t[:, 0:1], dsum_t[:, HEAD_DIM:HEAD_DIM + 1]], axis=0)
                s = _dot(q2, kt[u]) * 0.125 + bias[first.astype(jnp.int32)]
                p = jnp.exp(s - lse2)
                ds = (p * (_dot(do2, vt[u]) - dsum2)).astype(BF16)
                dq_acc[pl.ds(off, Q_BLOCK), :] = _unstack_heads(_dot(ds, kd[pair, :]), low) * 0.125
                dk_part[u] = _dot_tn(ds, q2) * 0.125
                dv_part[u] = _dot_tn(p.astype(BF16), do2)
                return carry

            lax.fori_loop(0, nblk, unit, 0, unroll=ATTN_UNROLL)
            for part, acc in ((dk_part, dk_acc), (dv_part, dv_acc)):
                for b in range(nblk):
                    t = part[b, Q_BLOCK:, :]
                    if b + 1 < nblk:
                        t = t + part[b + 1, 0:Q_BLOCK, :]
                    acc[b * Q_BLOCK:(b + 1) * Q_BLOCK, :] = t
            for acc, out_ref in ((dq_acc, dq_ref), (dk_acc, dk_ref), (dv_acc, dv_ref)):
                _scatter_classes(acc, stage, r)
                out_ref[0] = stage[...].astype(out_ref.dtype)

        for g in range(3):
            pl.when(gid == g)(lambda g=g: section(g))

    def col(sec):
        return pl.BlockSpec((1, SEQ, LANES), lambda b, g, hp: (b, 0, sec * 12 + g * 4 + hp))

    pos = pl.BlockSpec((1, SEQ, LANES), lambda b, g, hp: (b, 0, hp))
    dout = pl.BlockSpec((1, SEQ, LANES), lambda b, g, hp: (b, 0, g * 4 + hp))
    out = SDS((batch, SEQ, ATTN_WIDTH), BF16)
    seq_bf = pltpu.VMEM((SEQ, LANES), BF16)
    seq_f = pltpu.VMEM((SEQ, LANES), F32)
    pad_bf = pltpu.VMEM((Q_BLOCK + SEQ, LANES), BF16)
    part = pltpu.VMEM((nblk, 2 * Q_BLOCK, LANES), F32)
    blk_t = pltpu.VMEM((nblk, LANES, 2 * Q_BLOCK), BF16)
    return pl.pallas_call(
        body, grid=(batch, 3, 4),
        in_specs=[pl.BlockSpec(memory_space=pltpu.SMEM), col(0), col(1), col(2), pos, pos, pos],
        out_specs=[dout, dout, dout],
        out_shape=[out, out, out],
        scratch_shapes=[seq_bf, pad_bf, pad_bf, seq_bf, blk_t, blk_t, seq_f, seq_f, seq_f, seq_f, seq_f, part, part, seq_f,
                        pltpu.VMEM((2, 2 * Q_BLOCK, 2 * Q_BLOCK), F32)],
        compiler_params=_params("parallel", "parallel", "parallel"), name="attn_bwd")(
            slopes_r, qkv, qkv, qkv, datt, lse, dsum)


CONV_TC = 128
U_BLOCK0 = 3 * ATTN_WIDTH // CONV_TC
CONV_ROWS = 128
SUBLANES = 8


def _fill_shifted(sh):
    n = SEQ + CONV_PAD - SUBLANES
    for s in range(1, SUBLANES):
        sh[s, 0:n, :] = sh[0, s:s + n, :]


def _tap(sh, base, offset):
    s = offset % SUBLANES
    return sh[s, pl.ds(pl.multiple_of(base + (offset - s), SUBLANES), CONV_ROWS), :]


def _conv_fwd(u, conv_w, conv_b, batch, dep=None):
    nct = D_MODEL // CONV_TC

    def body(ua_ref, ub_ref, w_ref, b_ref, o_ref, sh):
        sh[0, 0:CONV_PAD, :] = jnp.zeros((CONV_PAD, CONV_TC), F32)
        sh[0, CONV_PAD:, :] = ua_ref[0] * _sigmoid(ub_ref[0])
        _fill_shifted(sh)

        def chunk(c, carry):
            base = pl.multiple_of(c * CONV_ROWS, CONV_ROWS)
            acc = jnp.broadcast_to(b_ref[...], (CONV_ROWS, CONV_TC))
            for t in range(CONV_K):
                acc = acc + _tap(sh, base, t + CONV_PAD - (CONV_K - 1)) * w_ref[t:t + 1, :]
            o_ref[0, pl.ds(base, CONV_ROWS), :] = acc
            return carry

        lax.fori_loop(0, SEQ // CONV_ROWS, chunk, 0)

    body, dep_spec, dep_arg = _anchored(body, 4, dep)
    return pl.pallas_call(
        body, grid=(nct, batch),
        in_specs=[pl.BlockSpec((1, SEQ, CONV_TC), lambda j, b: (b, 0, U_BLOCK0 + j)),
                  pl.BlockSpec((1, SEQ, CONV_TC), lambda j, b: (b, 0, U_BLOCK0 + nct + j)),
                  pl.BlockSpec((CONV_PAD, CONV_TC), lambda j, b: (0, j)),
                  pl.BlockSpec((1, CONV_TC), lambda j, b: (0, j))] + dep_spec,
        out_specs=pl.BlockSpec((1, SEQ, CONV_TC), lambda j, b: (b, 0, j)),
        out_shape=SDS((batch, SEQ, D_MODEL), F32),
        scratch_shapes=[pltpu.VMEM((SUBLANES, SEQ + CONV_PAD, CONV_TC), F32)],
        compiler_params=_params("parallel", "parallel"), name="conv_fwd")(u, u, conv_w, conv_b, *dep_arg)


def _conv_bwd(u, dc1, conv_w, batch, dep=None):
    nct = D_MODEL // CONV_TC
    nchunk = SEQ // CONV_ROWS

    def body(ua_ref, ub_ref, d_ref, w_ref, dua_ref, dub_ref, gw_ref, gb_ref, shc, shd, gacc):
        b = pl.program_id(1)
        shc[0, 0:CONV_PAD, :] = jnp.zeros((CONV_PAD, CONV_TC), F32)
        shc[0, CONV_PAD:, :] = ua_ref[0] * _sigmoid(ub_ref[0])
        _fill_shifted(shc)
        shd[0, 0:SEQ, :] = d_ref[0]
        shd[0, SEQ:, :] = jnp.zeros((CONV_PAD, CONV_TC), F32)
        _fill_shifted(shd)

        @pl.when(b == 0)
        def _():
            gacc[...] = jnp.zeros_like(gacc)
            gb_ref[...] = jnp.zeros_like(gb_ref)

        gb_ref[...] += _rowsum(d_ref[0])

        def chunk(c, carry):
            base = pl.multiple_of(c * CONV_ROWS, CONV_ROWS)
            dcur = shd[0, pl.ds(base, CONV_ROWS), :]
            acc = jnp.zeros((CONV_ROWS, CONV_TC), F32)
            for t in range(CONV_K):
                acc = acc + _tap(shd, base, CONV_K - 1 - t) * w_ref[t:t + 1, :]
                prod = _tap(shc, base, t + CONV_PAD - (CONV_K - 1)) * dcur
                gacc[t] += jnp.sum(prod.reshape(CONV_ROWS // 8, 8, CONV_TC), axis=0)
            ua = ua_ref[0, pl.ds(base, CONV_ROWS), :]
            sg = _sigmoid(ub_ref[0, pl.ds(base, CONV_ROWS), :])
            dua_ref[0, pl.ds(base, CONV_ROWS), :] = (acc * sg).astype(dua_ref.dtype)
            dub_ref[0, pl.ds(base, CONV_ROWS), :] = (acc * ua * sg * (1.0 - sg)).astype(dub_ref.dtype)
            return carry

        lax.fori_loop(0, nchunk, chunk, 0)

        @pl.when(b == batch - 1)
        def _():
            for t in range(CONV_K):
                gw_ref[t:t + 1, :] = jnp.sum(gacc[t], axis=0, keepdims=True)
            gw_ref[CONV_K:CONV_PAD, :] = jnp.zeros((CONV_PAD - CONV_K, CONV_TC), F32)

    du = SDS((batch, SEQ, D_MODEL), BF16)
    body, dep_spec, dep_arg = _anchored(body, 4, dep)
    return pl.pallas_call(
        body, grid=(nct, batch),
        in_specs=[pl.BlockSpec((1, SEQ, CONV_TC), lambda j, b: (b, 0, U_BLOCK0 + j)),
                  pl.BlockSpec((1, SEQ, CONV_TC), lambda j, b: (b, 0, U_BLOCK0 + nct + j)),
                  pl.BlockSpec((1, SEQ, CONV_TC), lambda j, b: (b, 0, j)),
                  pl.BlockSpec((CONV_PAD, CONV_TC), lambda j, b: (0, j))] + dep_spec,
        out_specs=[pl.BlockSpec((1, SEQ, CONV_TC), lambda j, b: (b, 0, j)),
                   pl.BlockSpec((1, SEQ, CONV_TC), lambda j, b: (b, 0, j)),
                   pl.BlockSpec((CONV_PAD, CONV_TC), lambda j, b: (0, j)),
                   pl.BlockSpec((1, CONV_TC), lambda j, b: (0, j))],
        out_shape=[du, du, SDS((CONV_PAD, D_MODEL), F32), SDS((1, D_MODEL), F32)],
        scratch_shapes=[pltpu.VMEM((SUBLANES, SEQ + CONV_PAD, CONV_TC), F32),
                        pltpu.VMEM((SUBLANES, SEQ + CONV_PAD, CONV_TC), F32),
                        pltpu.VMEM((CONV_K, 8, CONV_TC), F32)],
        compiler_params=_params("parallel", "arbitrary"), name="conv_bwd")(u, u, dc1, conv_w, *dep_arg)


MID_TM = 256


def _layernorm_stats(c1):
    mu = jnp.mean(c1, axis=-1, keepdims=True)
    cen = c1 - mu
    rs = lax.rsqrt(jnp.mean(cen * cen, axis=-1, keepdims=True) + LN_EPS)
    return cen * rs, rs


GATE_PARTS = 4
GATE_PART = 2 * D_MODEL // GATE_PARTS
GATE_PART0 = (IN_WIDTH - 2 * D_MODEL) // GATE_PART


def _gate_specs(tm):
    return [pl.BlockSpec((tm, GATE_PART), lambda i, k=k: (i, GATE_PART0 + k)) for k in range(GATE_PARTS)]


def _mid_fwd(att, c1, proj, x, w_a, w_c, w_o, gate_b, ln_g, ln_b, g2, dep=None):
    T = x.shape[0]
    tm = MID_TM

    def body(att_ref, c1_ref, lg0, lg1, lg2, lg3, x_ref, wa_ref, wc_ref, wo_ref, gb_ref, lng_ref, lnb_ref, g2_ref,
             c3_ref, ya_ref, yc_ref, mix_ref, x1_ref, h2_ref):
        logits = jnp.concatenate([lg0[...], lg1[...], lg2[...], lg3[...]], axis=1)
        ya = _dot(att_ref[...], wa_ref[...])
        xh, _ = _layernorm_stats(c1_ref[...])
        c2 = xh * lng_ref[...] + lnb_ref[...]
        c3 = (c2 * _sigmoid(c2)).astype(BF16)
        c3_ref[...] = c3
        yc = _dot(c3, wc_ref[...])
        gates = _sigmoid(logits + gb_ref[...])
        mix = (gates[:, :D_MODEL] * ya + gates[:, D_MODEL:] * yc).astype(BF16)
        ya_ref[...] = ya.astype(BF16)
        yc_ref[...] = yc.astype(BF16)
        mix_ref[...] = mix
        x1 = x_ref[...] + _dot(mix, wo_ref[...])
        x1_ref[...] = x1
        r = lax.rsqrt(jnp.mean(x1 * x1, axis=-1, keepdims=True) + RMS_EPS)
        h2_ref[...] = (x1 * r * g2_ref[...]).astype(BF16)

    row = lambda n: pl.BlockSpec((tm, n), lambda i: (i, 0))
    full = lambda a, b: pl.BlockSpec((a, b), lambda i: (0, 0))
    body, dep_spec, dep_arg = _anchored(body, 10 + GATE_PARTS, dep)
    return pl.pallas_call(
        body, grid=(T // tm,),
        in_specs=[row(ATTN_OUT), row(D_MODEL)] + _gate_specs(tm) + [row(D_MODEL),
                  full(ATTN_OUT, D_MODEL), full(D_MODEL, D_MODEL), full(D_MODEL, D_MODEL),
                  full(1, 2 * D_MODEL), full(1, D_MODEL), full(1, D_MODEL), full(1, D_MODEL)] + dep_spec,
        out_specs=[row(D_MODEL), row(D_MODEL), row(D_MODEL), row(D_MODEL), row(D_MODEL), row(D_MODEL)],
        out_shape=[SDS((T, D_MODEL), BF16), SDS((T, D_MODEL), BF16), SDS((T, D_MODEL), BF16), SDS((T, D_MODEL), BF16),
                   SDS((T, D_MODEL), F32), SDS((T, D_MODEL), BF16)],
        compiler_params=_params("parallel"), name="mid_fwd")(att, c1, *([proj] * GATE_PARTS), x, w_a, w_c, w_o,
                                                             gate_b, ln_g, ln_b, g2, *dep_arg)


def _mid_bwd(dx1b, ya, yc, proj, att, c1, w_a, w_c, w_o, gate_b, ln_g, ln_b, head_ones, dep=None):
    T = dx1b.shape[0]
    tm = MID_TM

    def body(dx_ref, ya_ref, yc_ref, lg0, lg1, lg2, lg3, att_ref, c1_ref, wa_ref, wc_ref, wo_ref, gb_ref, lng_ref,
             lnb_ref, e_ref, dlg_ref, dya_ref, dyc_ref, datt_ref, dsum_ref, dc1_ref, ggb_ref, glg_ref, glb_ref):
        logits = jnp.concatenate([lg0[...], lg1[...], lg2[...], lg3[...]], axis=1)
        @pl.when(pl.program_id(0) == 0)
        def _():
            ggb_ref[...] = jnp.zeros_like(ggb_ref)
            glg_ref[...] = jnp.zeros_like(glg_ref)
            glb_ref[...] = jnp.zeros_like(glb_ref)

        dmix = _dot_nt(dx_ref[...], wo_ref[...])
        gates = _sigmoid(logits + gb_ref[...])
        ga, gc = gates[:, :D_MODEL], gates[:, D_MODEL:]
        dla = dmix * ya_ref[...].astype(F32) * ga * (1.0 - ga)
        dlc = dmix * yc_ref[...].astype(F32) * gc * (1.0 - gc)
        dlg_ref[:, :D_MODEL] = dla.astype(BF16)
        dlg_ref[:, D_MODEL:] = dlc.astype(BF16)
        ggb_ref[:, :D_MODEL] += _rowsum(dla)
        ggb_ref[:, D_MODEL:] += _rowsum(dlc)
        dya = (dmix * ga).astype(BF16)
        dyc = (dmix * gc).astype(BF16)
        dya_ref[...] = dya
        dyc_ref[...] = dyc
        datt = _dot_nt(dya, wa_ref[...])
        datt_ref[...] = datt
        dsum_ref[...] = jnp.dot(datt * att_ref[...].astype(F32), e_ref[...], preferred_element_type=F32,
                                precision=lax.Precision.HIGHEST)
        dc3 = _dot_nt(dyc, wc_ref[...])
        xh, rs = _layernorm_stats(c1_ref[...])
        c2 = xh * lng_ref[...] + lnb_ref[...]
        sg = _sigmoid(c2)
        dc2 = dc3 * (sg * (1.0 + c2 * (1.0 - sg)))
        glg_ref[...] += _rowsum(dc2 * xh)
        glb_ref[...] += _rowsum(dc2)
        dxh = dc2 * lng_ref[...]
        dc1_ref[...] = rs * (dxh - jnp.mean(dxh, axis=-1, keepdims=True) - xh * jnp.mean(dxh * xh, axis=-1, keepdims=True))

    row = lambda n: pl.BlockSpec((tm, n), lambda i: (i, 0))
    full = lambda a, b: pl.BlockSpec((a, b), lambda i: (0, 0))
    body, dep_spec, dep_arg = _anchored(body, 12 + GATE_PARTS, dep)
    return pl.pallas_call(
        body, grid=(T // tm,),
        in_specs=[row(D_MODEL), row(D_MODEL), row(D_MODEL)] + _gate_specs(tm) + [row(ATTN_OUT), row(D_MODEL),
                  full(ATTN_OUT, D_MODEL), full(D_MODEL, D_MODEL), full(D_MODEL, D_MODEL),
                  full(1, 2 * D_MODEL), full(1, D_MODEL), full(1, D_MODEL), full(ATTN_OUT, ATTN_OUT)] + dep_spec,
        out_specs=[row(2 * D_MODEL), row(D_MODEL), row(D_MODEL), row(ATTN_OUT), row(ATTN_OUT), row(D_MODEL),
                   full(1, 2 * D_MODEL), full(1, D_MODEL), full(1, D_MODEL)],
        out_shape=[SDS((T, 2 * D_MODEL), BF16), SDS((T, D_MODEL), BF16), SDS((T, D_MODEL), BF16), SDS((T, ATTN_OUT), F32),
                   SDS((T, ATTN_OUT), F32), SDS((T, D_MODEL), F32),
                   SDS((1, 2 * D_MODEL), F32), SDS((1, D_MODEL), F32), SDS((1, D_MODEL), F32)],
        compiler_params=_params("arbitrary"), name="mid_bwd")(dx1b, ya, yc, *([proj] * GATE_PARTS), att, c1, w_a, w_c, w_o,
                                                               gate_b, ln_g, ln_b, head_ones, *dep_arg)


FFN_TM = 256
FFN_CHUNK = 512
FFN_SUB = tuple((lo, min(lo + FFN_CHUNK, D_FF)) for lo in range(0, D_FF, FFN_CHUNK))


def _rms_bwd(dy_times_g, xh, r):
    return r * (dy_times_g - xh * jnp.mean(dy_times_g * xh, axis=-1, keepdims=True))


def _load_resident(pairs, sems):
    @pl.when(pl.program_id(0) == 0)
    def _():
        copies = [pltpu.make_async_copy(src, dst, sems.at[k]) for k, (src, dst) in enumerate(pairs)]
        for cp in copies:
            cp.start()
        for cp in copies:
            cp.wait()


def _ffn_fwd(h2, x1, target, gf, w_g_t, w_u_t, w_d):
    T = h2.shape[0]
    tm = FFN_TM

    def body(h_ref, x1_ref, t_ref, gf_ref, wg_hbm, wu_hbm, wd_hbm,
             a_ref, b_ref, f_ref, dx2_ref, dx2b_ref, loss_ref, gnf_ref, wg, wu, wd, sems):
        _load_resident(((wg_hbm, wg), (wu_hbm, wu), (wd_hbm, wd)), sems)

        @pl.when(pl.program_id(0) == 0)
        def _():
            loss_ref[...] = jnp.zeros_like(loss_ref)
            gnf_ref[...] = jnp.zeros_like(gnf_ref)

        h = h_ref[...]
        x2 = x1_ref[...]
        for lo, hi in FFN_SUB:
            a = _dot_nt(h, wg[lo:hi, :])
            b = _dot_nt(h, wu[lo:hi, :])
            f = (a * _sigmoid(a) * b).astype(BF16)
            a_ref[:, lo:hi] = a.astype(BF16)
            b_ref[:, lo:hi] = b.astype(BF16)
            f_ref[:, lo:hi] = f
            x2 = x2 + _dot(f, wd[lo:hi, :])

        r = lax.rsqrt(jnp.mean(x2 * x2, axis=-1, keepdims=True) + RMS_EPS)
        xh = x2 * r
        err = xh * gf_ref[...] - t_ref[...]
        loss_ref[...] += (0.5 / D_MODEL) * jnp.sum(err * err)
        dy = err * (1.0 / D_MODEL)
        gnf_ref[...] += _rowsum(dy * xh)
        dx2 = _rms_bwd(dy * gf_ref[...], xh, r)
        dx2_ref[...] = dx2
        dx2b_ref[...] = dx2.astype(BF16)

    row = lambda n: pl.BlockSpec((tm, n), lambda i: (i, 0))
    const = lambda n: pl.BlockSpec((1, n), lambda i: (0, 0))
    wshape = pltpu.VMEM((D_FF, D_MODEL), BF16)
    return pl.pallas_call(
        body, grid=(T // tm,),
        in_specs=[row(D_MODEL), row(D_MODEL), row(D_MODEL), const(D_MODEL), ANY_SPEC, ANY_SPEC, ANY_SPEC],
        out_specs=[row(D_FF), row(D_FF), row(D_FF), row(D_MODEL), row(D_MODEL), const(128), const(D_MODEL)],
        out_shape=[SDS((T, D_FF), BF16), SDS((T, D_FF), BF16), SDS((T, D_FF), BF16), SDS((T, D_MODEL), F32),
                   SDS((T, D_MODEL), BF16), SDS((1, 128), F32), SDS((1, D_MODEL), F32)],
        scratch_shapes=[wshape, wshape, wshape, pltpu.SemaphoreType.DMA((3,))],
        compiler_params=_params("arbitrary"), name="ffn_fwd")(h2, x1, target, gf, w_g_t, w_u_t, w_d)


def _ffn_bwd(dx2b, dx2, a, b, x1, g2, w_g_t, w_u_t, w_d):
    T = dx2.shape[0]
    tm = FFN_TM

    def body(dxb_ref, dx2_ref, a_ref, b_ref, x1_ref, g2_ref, wg_hbm, wu_hbm, wd_hbm,
             da_ref, db_ref, dx1_ref, dx1b_ref, gn2_ref, wg, wu, wd, sems):
        _load_resident(((wg_hbm, wg), (wu_hbm, wu), (wd_hbm, wd)), sems)

        @pl.when(pl.program_id(0) == 0)
        def _():
            gn2_ref[...] = jnp.zeros_like(gn2_ref)

        dxb = dxb_ref[...]
        dh2 = jnp.zeros((tm, D_MODEL), F32)
        for lo, hi in FFN_SUB:
            df = _dot_nt(dxb, wd[lo:hi, :])
            av = a_ref[:, lo:hi].astype(F32)
            bv = b_ref[:, lo:hi].astype(F32)
            sg = _sigmoid(av)
            db = (df * av * sg).astype(BF16)
            da = (df * bv * (sg * (1.0 + av * (1.0 - sg)))).astype(BF16)
            da_ref[:, lo:hi] = da
            db_ref[:, lo:hi] = db
            dh2 = dh2 + _dot(da, wg[lo:hi, :]) + _dot(db, wu[lo:hi, :])

        x1 = x1_ref[...]
        r = lax.rsqrt(jnp.mean(x1 * x1, axis=-1, keepdims=True) + RMS_EPS)
        xh = x1 * r
        gn2_ref[...] += _rowsum(dh2 * xh)
        dx1 = dx2_ref[...] + _rms_bwd(dh2 * g2_ref[...], xh, r)
        dx1_ref[...] = dx1
        dx1b_ref[...] = dx1.astype(BF16)

    row = lambda n: pl.BlockSpec((tm, n), lambda i: (i, 0))
    const = lambda n: pl.BlockSpec((1, n), lambda i: (0, 0))
    wshape = pltpu.VMEM((D_FF, D_MODEL), BF16)
    return pl.pallas_call(
        body, grid=(T // tm,),
        in_specs=[row(D_MODEL), row(D_MODEL), row(D_FF), row(D_FF), row(D_MODEL), const(D_MODEL),
                  ANY_SPEC, ANY_SPEC, ANY_SPEC],
        out_specs=[row(D_FF), row(D_FF), row(D_MODEL), row(D_MODEL), const(D_MODEL)],
        out_shape=[SDS((T, D_FF), BF16), SDS((T, D_FF), BF16), SDS((T, D_MODEL), F32), SDS((T, D_MODEL), BF16),
                   SDS((1, D_MODEL), F32)],
        scratch_shapes=[wshape, wshape, wshape, pltpu.SemaphoreType.DMA((3,))],
        compiler_params=_params("arbitrary"), name="ffn_bwd")(dx2b, dx2, a, b, x1, g2, w_g_t, w_u_t, w_d)


def _in_bwd(pieces, w_in_t, x, dx1, g1, dep=None):
    T = x.shape[0]
    tm = IN_TM
    npc = len(pieces)
    assert sum(p.shape[1] for p in pieces) == IN_WIDTH

    def body(*refs):
        p_refs = refs[:npc]
        w_hbm, x_ref, dx1_ref, g_ref, dx_ref, gn1_ref, w_vmem, sem = refs[npc:]

        @pl.when(pl.program_id(0) == 0)
        def _():
            cp = pltpu.make_async_copy(w_hbm, w_vmem, sem)
            cp.start()
            cp.wait()
            gn1_ref[...] = jnp.zeros_like(gn1_ref)

        dh = jnp.zeros((tm, D_MODEL), F32)
        col = 0
        for p_ref in p_refs:
            for j in range(p_ref.shape[1] // IN_CHUNK):
                dh = dh + _dot(p_ref[:, j * IN_CHUNK:(j + 1) * IN_CHUNK], w_vmem[col:col + IN_CHUNK, :])
                col += IN_CHUNK
        xv = x_ref[...]
        r = lax.rsqrt(jnp.mean(xv * xv, axis=-1, keepdims=True) + RMS_EPS)
        xh = xv * r
        gn1_ref[...] += _rowsum(dh * xh)
        dx_ref[...] = dx1_ref[...] + _rms_bwd(dh * g_ref[...], xh, r)

    row = lambda n: pl.BlockSpec((tm, n), lambda i: (i, 0))
    body, dep_spec, dep_arg = _anchored(body, npc + 4, dep)
    return pl.pallas_call(
        body, grid=(T // tm,),
        in_specs=[row(p.shape[1]) for p in pieces]
        + [pl.BlockSpec(memory_space=pl.ANY), row(D_MODEL), row(D_MODEL), pl.BlockSpec((1, D_MODEL), lambda i: (0, 0))]
        + dep_spec,
        out_specs=[row(D_MODEL), pl.BlockSpec((1, D_MODEL), lambda i: (0, 0))],
        out_shape=[SDS((T, D_MODEL), F32), SDS((1, D_MODEL), F32)],
        scratch_shapes=[pltpu.VMEM((IN_WIDTH, D_MODEL), BF16), pltpu.SemaphoreType.DMA],
        compiler_params=_params("arbitrary"), name="in_bwd")(*pieces, w_in_t, x, dx1, g1, *dep_arg)


def _local_step(x, target, in_proj, small, late_weights=None, emit=None):
    T = x.shape[0]
    batch = T // SEQ
    slopes_r = jnp.asarray(_slopes_times_dilation())
    emit = emit or (lambda names, grads: None)

    h, proj, w = in_proj()
    proj3 = proj.reshape(batch, SEQ, IN_WIDTH)

    att, lse = _attn_fwd(proj3, slopes_r, batch, w.get("token"))
    att = att.reshape(T, ATTN_OUT)
    if late_weights is not None:
        w = {**w, **late_weights("after_attention", att)}

    c1 = _conv_fwd(proj3, w["conv_w"], small["conv_b"], batch, w.get("token")).reshape(T, D_MODEL)
    if late_weights is not None:
        w = {**w, **late_weights(LATE_MERGE, (att, c1))}

    c3, ya, yc, mix, x1, h2 = _mid_fwd(
        att, c1, proj, x, w["w_attn_out"], w["w_conv_out"], w["w_o"],
        small["gate_b"], small["conv_ln_g"], small["conv_ln_b"], small["norm2_g"], w.get("token"))
    if late_weights is not None:
        w = {**w, **late_weights(LATE_FFN, h2)}

    a, b, f, dx2, dx2b, loss, g_normf = _ffn_fwd(h2, x1, target, small["norm_f_g"],
                                                   w["w_ffn_gate"], w["w_ffn_up"], w["w_ffn_down"])

    da, db, dx1, dx1b, g_norm2 = _ffn_bwd(dx2b, dx2, a, b, x1, small["norm2_g"],
                                           w["w_ffn_gate"], w["w_ffn_up"], w["w_ffn_down"])
    gw = {}
    gw["w_ffn_down"] = _mm_tn(f, dx2b, BF16, "gw_ffn_down", tn=1024)
    gw["w_ffn_gate"] = _mm_tn(da, h2, BF16, "gw_ffn_gate", tn=1024)
    gw["w_ffn_up"] = _mm_tn(db, h2, BF16, "gw_ffn_up", tn=1024)
    token = emit(("w_ffn_gate", "w_ffn_up", "w_ffn_down"), gw)

    head_ones = jnp.asarray(np.kron(np.eye(HEADS_PER_GROUP, dtype=np.float32), np.ones((HEAD_DIM, HEAD_DIM), np.float32)))
    dlogits, dya, dyc, datt, dsum, dc1, g_gate_b, g_ln_g, g_ln_b = _mid_bwd(
        dx1b, ya, yc, proj, att, c1, w["w_attn_out"], w["w_conv_out"], w["w_o"],
        small["gate_b"], small["conv_ln_g"], small["conv_ln_b"], head_ones, token)
    gw["w_o"] = _mm_tn(mix, dx1b, BF16, "gw_o", tn=1024)
    gw["w_attn_out"] = _mm_tn(att, dya, BF16, "gw_attn_out", tn=1024)
    gw["w_conv_out"] = _mm_tn(c3, dyc, BF16, "gw_conv_out", tn=1024)
    token = emit(("w_conv_out", "w_attn_out", "w_o"), gw)

    dua, dub, g_conv_w, g_conv_b = _conv_bwd(proj3, dc1.reshape(batch, SEQ, D_MODEL), w["conv_w"], batch, token)

    dq, dk, dv = _attn_bwd(proj3, datt.reshape(batch, SEQ, ATTN_OUT), lse, dsum.reshape(batch, SEQ, ATTN_OUT),
                           slopes_r, batch)
    pieces = [dq.reshape(T, ATTN_WIDTH), dk.reshape(T, ATTN_WIDTH), dv.reshape(T, ATTN_WIDTH),
              dua.reshape(T, D_MODEL), dub.reshape(T, D_MODEL), dlogits]

    names = ("q", "k", "v", "ua", "ub", "gate")
    gw["w_in"] = jnp.concatenate([_mm_tn(p, h, BF16, "gw_in_" + nm, tn=1024) for nm, p in zip(names, pieces)], axis=0)
    gw["conv_w"] = g_conv_w
    token = emit(("w_in", "conv_w"), gw)
    grad_x, g_norm1 = _in_bwd(pieces, w["w_in"], x, dx1, small["norm1_g"], token)

    gsmall = {"norm1_g": g_norm1, "gate_b": g_gate_b, "conv_b": g_conv_b, "conv_ln_g": g_ln_g, "conv_ln_b": g_ln_b,
              "norm2_g": g_norm2, "norm_f_g": g_normf}
    return loss, grad_x, gw, gsmall


ANY = pl.BlockSpec(memory_space=pl.ANY)


def _all_gather(arrs):
    n = len(arrs)

    def body(*refs):
        ins, outs = refs[:n], refs[n:2 * n]
        send_sems, recv_sems, local_sems = refs[2 * n:]
        x, y, c = lax.axis_index("x"), lax.axis_index("y"), lax.axis_index("c")
        me, sibling = (x, y, c), (x, y, 1 - c)
        chips = [(1 - x, y), (x, 1 - y), (1 - x, 1 - y)]

        def copy(a, k, block, to, src=None):
            px, py, pc = block
            dst = outs[a].at[4 * px + 2 * py + pc]
            return pltpu.make_async_remote_copy(
                src_ref=dst if src is None else src, dst_ref=dst,
                send_sem=send_sems.at[a, k], recv_sem=recv_sems.at[a, k], device_id=to, device_id_type=MESH)

        mine = [pltpu.make_async_copy(ins[a], outs[a].at[4 * x + 2 * y + c], local_sems.at[a]) for a in range(n)]
        for cp in mine:
            cp.start()
        first = []
        for j, chip in enumerate(chips):
            first += [copy(a, 1 + j, me, (*chip, c), src=ins[a]) for a in range(n)]
        first += [copy(a, 0, me, sibling, src=ins[a]) for a in range(n)]
        for cp in first:
            cp.start()
        passed = []
        for j, chip in enumerate(chips):
            for a in range(n):
                copy(a, 1 + j, (*chip, c), me).wait_recv()
                cp = copy(a, 4 + j, (*chip, c), sibling)
                cp.start()
                passed.append(cp)
        for a in range(n):
            copy(a, 0, sibling, me).wait_recv()
        for j, chip in enumerate(chips):
            for a in range(n):
                copy(a, 4 + j, (*chip, 1 - c), me).wait_recv()
        for cp in first + passed:
            cp.wait_send()
        for cp in mine:
            cp.wait()

    return pl.pallas_call(
        body, in_specs=[ANY] * n, out_specs=[ANY] * n,
        out_shape=[SDS((N_DEV,) + a.shape, a.dtype) for a in arrs],
        scratch_shapes=[pltpu.SemaphoreType.DMA((n, 7)), pltpu.SemaphoreType.DMA((n, 7)), pltpu.SemaphoreType.DMA((n,))],
        name="all_gather_weights")(*arrs)


HBM =pl.BlockSpec(memory_space=pltpu.HBM)
SEM = pl.BlockSpec(memory_space=pltpu.SEMAPHORE)
ALL_PEERS = tuple(range(1, N_DEV))
OTHER_CHIPS = (2, 4, 6)
SPLIT_EFFECT = pltpu.CompilerParams(has_side_effects=pltpu.SideEffectType.DATAFLOW_SIDE_EFFECTING)


def _exchange_copies(mode, ks, srcs, lands, send_sems, recv_sems):
    x, y, c = lax.axis_index("x"), lax.axis_index("y"), lax.axis_index("c")
    me = 4 * x + 2 * y + c
    send, recv = [], []
    for a in range(len(lands)):
        for i, k in enumerate(ks):
            peer = (x ^ ((k >> 2) & 1), y ^ ((k >> 1) & 1), c ^ (k & 1))
            pidx = 4 * peer[0] + 2 * peer[1] + peer[2]
            if mode == "gather":
                src, to, out_slot, in_slot = srcs[a], peer, me, pidx
            elif mode == "scatter":
                src, to, out_slot, in_slot = srcs[a].at[pidx], peer, me, pidx
            elif mode == "chip_scatter":
                src, to, out_slot, in_slot = srcs[a].at[pidx >> 1], peer, me >> 1, pidx >> 1
            else:
                src, to, out_slot, in_slot = lands[a].at[pidx], (x, y, 1 - c), pidx, pidx ^ 1
            s = a * len(ks) + i
            send.append(pltpu.make_async_remote_copy(
                src_ref=src, dst_ref=lands[a].at[out_slot], send_sem=send_sems.at[s], recv_sem=recv_sems.at[s],
                device_id=to, device_id_type=MESH))
            recv.append(pltpu.make_async_remote_copy(
                src_ref=src, dst_ref=lands[a].at[in_slot], send_sem=send_sems.at[s], recv_sem=recv_sems.at[s],
                device_id=to, device_id_type=MESH))
    return send, recv


def _send_start(mode, ks, name, srcs=(), lands=None, dep=None):
    srcs = list(srcs)
    if lands is None:
        slots = 4 if mode == "chip_scatter" else N_DEV
        lands = [lax.empty((slots,) + (s.shape if mode == "gather" else s.shape[1:]), s.dtype) for s in srcs]
    ns, nl = len(srcs), len(lands)
    nsem = nl * len(ks)

    def body(*refs):
        send, _ = _exchange_copies(mode, ks, refs[:ns], refs[ns:ns + nl], refs[ns + nl], refs[ns + nl + 1])
        for cp in send:
            cp.start()
        token = refs[-1]
        token[...] = jnp.zeros_like(token)

    both = srcs + list(lands)
    body, dep_spec, dep_arg = _anchored(body, ns + nl, dep)
    res = pl.pallas_call(
        body, name=name,
        out_shape=(pltpu.SemaphoreType.DMA((nsem,)), pltpu.SemaphoreType.DMA((nsem,)),
                   *[pltpu.HBM(a.shape, a.dtype) for a in both], SDS((8, 128), F32)),
        in_specs=[HBM] * (ns + nl) + dep_spec,
        out_specs=(SEM, SEM, *([HBM] * (ns + nl)), pl.BlockSpec(memory_space=pltpu.VMEM)),
        input_output_aliases={i: 2 + i for i in range(ns + nl)}, compiler_params=SPLIT_EFFECT,
    )(*[pltpu.with_memory_space_constraint(a, pltpu.HBM) for a in both], *dep_arg)
    return dict(mode=mode, ks=ks, send_sems=res[0], recv_sems=res[1], srcs=res[2:2 + ns], lands=res[2 + ns:2 + ns + nl],
                token=res[-1])


def _send_wait(started, after, name):
    ns, nl = len(started["srcs"]), len(started["lands"])

    def body(*refs):
        send, recv = _exchange_copies(started["mode"], started["ks"], refs[:ns], refs[ns:ns + nl],
                                      refs[ns + nl], refs[ns + nl + 1])
        for cp in send:
            cp.wait_send()
        for cp in recv:
            cp.wait_recv()

    both = list(started["srcs"]) + list(started["lands"])
    after = after if isinstance(after, (tuple, list)) else (after,)
    res = pl.pallas_call(
        body, name=name,
        out_shape=tuple(pltpu.HBM(a.shape, a.dtype) for a in both),
        in_specs=[HBM] * (ns + nl) + [SEM, SEM] + [ANY] * len(after), out_specs=tuple([HBM] * (ns + nl)),
        input_output_aliases={i: i for i in range(ns + nl)}, compiler_params=SPLIT_EFFECT,
    )(*both, started["send_sems"], started["recv_sems"], *after)
    return res[:ns], res[ns:]


def _exchange_sibling(gs):
    n = len(gs)

    def body(*refs):
        ins, outs = refs[:n], refs[n:2 * n]
        send_sems, recv_sems = refs[2 * n:]
        x, y, c = lax.axis_index("x"), lax.axis_index("y"), lax.axis_index("c")
        copies = []
        for a in range(n):
            for j in range(4):
                copies.append(pltpu.make_async_remote_copy(
                    src_ref=ins[a].at[2 * j + (1 - c)], dst_ref=outs[a].at[j],
                    send_sem=send_sems.at[a, j], recv_sem=recv_sems.at[a, j],
                    device_id=(x, y, 1 - c), device_id_type=MESH))
        for cp in copies:
            cp.start()
        for cp in copies:
            cp.wait_recv()
        for cp in copies:
            cp.wait_send()

    return pl.pallas_call(
        body, in_specs=[ANY] * n, out_specs=[ANY] * n,
        out_shape=[SDS((4,) + g.shape[1:], g.dtype) for g in gs],
        scratch_shapes=[pltpu.SemaphoreType.DMA((n, 4)), pltpu.SemaphoreType.DMA((n, 4))],
        name="reduce_scatter_sibling")(*gs)


def _add_pair(g, r1, core, name):
    _, rows, cols = g.shape
    tr = _row_tile(rows, cols, 3 * g.dtype.itemsize)

    def body(c_ref, g_ref, r_ref, o_ref):
        o_ref[...] = (g_ref[...].astype(F32) + r_ref[...].astype(F32)).astype(o_ref.dtype)

    return pl.pallas_call(
        body,
        grid_spec=pltpu.PrefetchScalarGridSpec(
            num_scalar_prefetch=1, grid=(4, rows // tr),
            in_specs=[pl.BlockSpec((1, tr, cols), lambda j, i, c_ref: (2 * j + c_ref[0], i, 0)),
                      pl.BlockSpec((1, tr, cols), lambda j, i, c_ref: (j, i, 0))],
            out_specs=pl.BlockSpec((1, tr, cols), lambda j, i, c_ref: (j, i, 0))),
        out_shape=SDS((4, rows, cols), g.dtype),
        compiler_params=_params("parallel", "parallel"), name=name)(core, g, r1)


def _row_tile(rows, cols, itemsize_total):
    budget = (4 << 20) // max(1, cols * itemsize_total)
    if rows <= budget:
        return rows
    t = rows
    while t > budget and t % 2 == 0 and (t // 2) % 16 == 0:
        t //= 2
    return t


def _adam_math(g, w, m, v):
    m_new = ADAM_B1 * m + (1.0 - ADAM_B1) * g
    v_new = ADAM_B2 * v + (1.0 - ADAM_B2) * (g * g)
    m_hat = m_new / (1.0 - ADAM_B1 ** ADAM_STEP)
    v_hat = v_new / (1.0 - ADAM_B2 ** ADAM_STEP)
    delta = -ADAM_LR * (m_hat / (jnp.sqrt(v_hat) + ADAM_EPS) + ADAM_WD * w)
    return delta, m_new, v_new


def _sum_adam(parts, own, mine, w, m, v, name):
    rows, cols = w.shape
    nparts = parts.shape[0]
    tr = _row_tile(rows, cols, (nparts + 1) * parts.dtype.itemsize + 7 * 4)

    def body(mine_ref, p_ref, own_ref, w_ref, m_ref, v_ref, g_ref, d_ref, mo_ref, vo_ref):
        g = None
        for s in range(nparts):
            part = jnp.where(mine_ref[0] == s, own_ref[0], p_ref[s]).astype(F32)
            g = part if g is None else g + part
        delta, m_new, v_new = _adam_math(g, w_ref[...], m_ref[...], v_ref[...])
        g_ref[...] = g
        d_ref[...] = delta
        mo_ref[...] = m_new
        vo_ref[...] = v_new

    blk = pl.BlockSpec((tr, cols), lambda i, mine_ref: (i, 0))
    out = SDS((rows, cols), F32)
    return pl.pallas_call(
        body,
        grid_spec=pltpu.PrefetchScalarGridSpec(
            num_scalar_prefetch=1, grid=(rows // tr,),
            in_specs=[pl.BlockSpec((nparts, tr, cols), lambda i, mine_ref: (0, i, 0)),
                      pl.BlockSpec((1, tr, cols), lambda i, mine_ref: (mine_ref[0], i, 0)), blk, blk, blk],
            out_specs=[blk, blk, blk, blk]),
        out_shape=[out, out, out, out],
        compiler_params=_params("parallel"), name=name)(mine, parts, own, w, m, v)


SMALL_ROWS = 72


def _small_allreduce_adam(gpart, w, m, v, row_counts, dep=None):
    def reduce_body(g_ref, go_ref, gath, send_sems, recv_sems):
        x, y, c = lax.axis_index("x"), lax.axis_index("y"), lax.axis_index("c")
        me = 4 * x + 2 * y + c
        gath[me] = g_ref[...]
        copies = []
        for k in range(1, N_DEV):
            fx, fy, fc = (k >> 2) & 1, (k >> 1) & 1, k & 1
            peer = (x ^ fx, y ^ fy, c ^ fc)
            copies.append(pltpu.make_async_remote_copy(
                src_ref=gath.at[me], dst_ref=gath.at[me], send_sem=send_sems.at[k - 1], recv_sem=recv_sems.at[k - 1],
                device_id=peer, device_id_type=MESH))
        for cp in copies:
            cp.start()
        for cp in copies:
            cp.wait_recv()
        for cp in copies:
            cp.wait_send()
        g = gath[0]
        for d in range(1, N_DEV):
            g = g + gath[d]
        go_ref[...] = g

    def adam_body(g_ref, w_ref, m_ref, v_ref, *out_refs):
        g = g_ref[...]
        delta, m_new, v_new = _adam_math(g, w_ref[...], m_ref[...], v_ref[...])
        outs = iter(out_refs)
        for val in (g, delta, m_new, v_new):
            lo = 0
            for r in row_counts:
                next(outs)[...] = val[lo:lo + r]
                lo += r
        next(outs)[...] = g[SMALL_ROWS - SUBLANES:]

    vm = pl.BlockSpec(memory_space=pltpu.VMEM)
    reduce_body, dep_spec, dep_arg = _anchored(reduce_body, 1, dep)
    total = pl.pallas_call(
        reduce_body, in_specs=[vm] + dep_spec, out_specs=vm, out_shape=SDS((SMALL_ROWS, 128), F32),
        scratch_shapes=[pltpu.VMEM((N_DEV, SMALL_ROWS, 128), F32), pltpu.SemaphoreType.DMA((N_DEV - 1,)),
                        pltpu.SemaphoreType.DMA((N_DEV - 1,))],
        name="small_allreduce")(gpart, *dep_arg)
    out_shape = [SDS((r, 128), F32) for _ in range(4) for r in row_counts] + [SDS((SUBLANES, 128), F32)]
    res = pl.pallas_call(adam_body, in_specs=[vm] * 4, out_specs=[vm] * len(out_shape), out_shape=out_shape,
                         name="small_adam")(total, w, m, v)
    k = len(row_counts)
    return [res[i * k:(i + 1) * k] for i in range(4)], res[-1]


BIG = ("w_in", "conv_w", "w_conv_out", "w_attn_out", "w_o", "w_ffn_gate", "w_ffn_up", "w_ffn_down")
LATE_MERGE = ("w_conv_out", "w_attn_out", "w_o")
LATE_FFN = ("w_ffn_gate", "w_ffn_up", "w_ffn_down")
TRANSPOSED = ("w_in", "w_ffn_gate", "w_ffn_up")
COL_SHARDED = ("conv_w", "w_attn_out")
SMALL = ("norm1_g", "gate_b", "conv_b", "conv_ln_g", "conv_ln_b", "norm2_g", "norm_f_g")
WEIGHTS = ("norm1_g", "w_in", "gate_b", "conv_w", "conv_b", "conv_ln_g", "conv_ln_b", "w_conv_out", "w_attn_out", "w_o",
           "norm2_g", "w_ffn_gate", "w_ffn_up", "w_ffn_down", "norm_f_g")


def _shard2d(name, a):
    a = a.reshape(a.shape[-2], a.shape[-1])
    if name in TRANSPOSED:
        a = a.T
    if name == "conv_w":
        a = jnp.pad(a, ((0, CONV_PAD - CONV_K), (0, 0)))
    return a


def _from_shard2d(name, val, shape):
    if name in TRANSPOSED:
        val = val.T
    if name == "conv_w":
        val = val[:CONV_K]
    return val.reshape(shape)


def _gathered_to_full(name, g):
    if name in COL_SHARDED:
        return g.transpose(1, 0, 2).reshape(g.shape[1], N_DEV * g.shape[2])
    return g.reshape(N_DEV * g.shape[1], g.shape[2])


def _full_to_blocks(name, g):
    if name in COL_SHARDED:
        return g.reshape(g.shape[0], N_DEV, g.shape[1] // N_DEV).transpose(1, 0, 2)
    return g.reshape(N_DEV, g.shape[0] // N_DEV, g.shape[1])


def _pack_small(d, last_rows):
    vec = jnp.concatenate([d[n].reshape(-1) for n in SMALL]).reshape(SMALL_ROWS - SUBLANES, 128)
    return jnp.concatenate([vec, last_rows], axis=0)


def kernel(x, norm1_g, w_in, gate_b, conv_w, conv_b, conv_ln_g, conv_ln_b, w_conv_out, w_attn_out, w_o, norm2_g, w_ffn_gate, w_ffn_up, w_ffn_down, norm_f_g, loss_target, m_norm1_g, m_w_in, m_gate_b, m_conv_w, m_conv_b, m_conv_ln_g, m_conv_ln_b, m_w_conv_out, m_w_attn_out, m_w_o, m_norm2_g, m_w_ffn_gate, m_w_ffn_up, m_w_ffn_down, m_norm_f_g, v_norm1_g, v_w_in, v_gate_b, v_conv_w, v_conv_b, v_conv_ln_g, v_conv_ln_b, v_w_conv_out, v_w_attn_out, v_w_o, v_norm2_g, v_w_ffn_gate, v_w_ffn_up, v_w_ffn_down, v_norm_f_g):
    wts = dict(norm1_g=norm1_g, w_in=w_in, gate_b=gate_b, conv_w=conv_w, conv_b=conv_b, conv_ln_g=conv_ln_g,
               conv_ln_b=conv_ln_b, w_conv_out=w_conv_out, w_attn_out=w_attn_out, w_o=w_o, norm2_g=norm2_g,
               w_ffn_gate=w_ffn_gate, w_ffn_up=w_ffn_up, w_ffn_down=w_ffn_down, norm_f_g=norm_f_g)
    mom1 = dict(norm1_g=m_norm1_g, w_in=m_w_in, gate_b=m_gate_b, conv_w=m_conv_w, conv_b=m_conv_b, conv_ln_g=m_conv_ln_g,
                conv_ln_b=m_conv_ln_b, w_conv_out=m_w_conv_out, w_attn_out=m_w_attn_out, w_o=m_w_o, norm2_g=m_norm2_g,
                w_ffn_gate=m_w_ffn_gate, w_ffn_up=m_w_ffn_up, w_ffn_down=m_w_ffn_down, norm_f_g=m_norm_f_g)
    mom2 = dict(norm1_g=v_norm1_g, w_in=v_w_in, gate_b=v_gate_b, conv_w=v_conv_w, conv_b=v_conv_b, conv_ln_g=v_conv_ln_g,
                conv_ln_b=v_conv_ln_b, w_conv_out=v_w_conv_out, w_attn_out=v_w_attn_out, w_o=v_w_o, norm2_g=v_norm2_g,
                w_ffn_gate=v_w_ffn_gate, w_ffn_up=v_w_ffn_up, w_ffn_down=v_w_ffn_down, norm_f_g=v_norm_f_g)

    T = x.shape[0] * x.shape[1]
    x2 = x.reshape(T, D_MODEL)
    t2 = loss_target.reshape(T, D_MODEL)

    me = 4 * lax.axis_index("x") + 2 * lax.axis_index("y") + lax.axis_index("c")
    shards = {n: _shard2d(n, wts[n]) for n in BIG}
    sent = {n: shards[n] if n == "conv_w" else shards[n].astype(BF16) for n in BIG}
    small = {n: wts[n].reshape(1, -1) for n in SMALL}

    stage = {}

    def in_proj():
        w_in_blocks, conv_blocks = _all_gather([sent["w_in"], sent["conv_w"]])
        near = (1,) + OTHER_CHIPS
        stage["merge"] = _send_start("gather", near, "gather_start_merge", [sent[n] for n in LATE_MERGE], dep=w_in_blocks)
        stage["ffn"] = _send_start("gather", near, "gather_start_ffn", [sent[n] for n in LATE_FFN],
                                   dep=stage["merge"]["token"])
        w_in_t = _gathered_to_full("w_in", w_in_blocks)
        h, proj = _in_proj(x2, small["norm1_g"], w_in_t, stage["ffn"]["token"])
        return h, proj, {"w_in": w_in_t, "conv_w": _gathered_to_full("conv_w", conv_blocks)}

    def filled(names, srcs, lands):
        return {n: _gathered_to_full(n, lax.dynamic_update_slice(land, src[None], (me, 0, 0)))
                for n, src, land in zip(names, srcs, lands)}

    def pass_on(group, after):
        stage[group + "_srcs"], lands = _send_wait(stage[group], after, "gather_wait_" + group)
        stage[group + "_forward"] = _send_start("forward", OTHER_CHIPS, "forward_start_" + group, lands=lands)
        return stage[group + "_forward"]["token"]

    def arrived(group, names, after):
        _, lands = _send_wait(stage[group + "_forward"], after, "forward_wait_" + group)
        return filled(names, stage[group + "_srcs"], lands)

    def late_weights(which, after):
        if which == "after_attention":
            return {"token": pass_on("merge", after)}
        if which is LATE_MERGE:
            return {**arrived("merge", LATE_MERGE, after), "token": pass_on("ffn", after)}
        return arrived("ffn", LATE_FFN, after)

    scatters = []
    core = lax.axis_index("c").astype(jnp.int32).reshape(1)

    def emit(names, gw):
        blocks = [_full_to_blocks(n, gw[n]) for n in names]
        if "w_in" in names:
            sums = [_add_pair(g, r, core, "chip_sum_" + n) for n, g, r in zip(names, blocks, _exchange_sibling(blocks))]
            started = _send_start("chip_scatter", OTHER_CHIPS, "scatter_start_" + names[0], sums)
        else:
            started = _send_start("scatter", ALL_PEERS, "scatter_start_" + names[0], blocks)
        scatters.append((names, started))
        return started["token"]

    loss_part, grad_x, gw, gsmall = _local_step(x2, t2, in_proj, small, late_weights, emit)

    grads, deltas, new_m, new_v = {}, {}, {}, {}
    after = grad_x
    for names, started in scatters:
        srcs, lands = _send_wait(started, after, "scatter_wait_" + names[0])
        mine = (me >> 1 if started["mode"] == "chip_scatter" else me).astype(jnp.int32).reshape(1)
        for n, src, land in zip(names, srcs, lands):
            g, d, mo, vo = _sum_adam(land, src, mine, shards[n], _shard2d(n, mom1[n]), _shard2d(n, mom2[n]), "adam_" + n)
            for dst, val in ((grads, g), (deltas, d), (new_m, mo), (new_v, vo)):
                dst[n] = _from_shard2d(n, val, wts[n].shape)
            after = g

    zeros, ones = jnp.zeros((SUBLANES, 128), F32), jnp.ones((SUBLANES, 128), F32)
    row_counts = [wts[n].size // 128 for n in SMALL]
    kinds, loss_rows = _small_allreduce_adam(
        _pack_small(gsmall, jnp.broadcast_to(loss_part, (SUBLANES, 128))), _pack_small(wts, zeros),
        _pack_small(mom1, zeros), _pack_small(mom2, ones), row_counts, after)
    for dst, vals in zip((grads, deltas, new_m, new_v), kinds):
        dst.update({n: val.reshape(wts[n].shape) for n, val in zip(SMALL, vals)})
    loss = loss_rows[0, 0]
    return (loss, grad_x.reshape(x.shape), *[grads[n] for n in WEIGHTS], *[deltas[n] for n in WEIGHTS],
            *[new_m[n] for n in WEIGHTS], *[new_v[n] for n in WEIGHTS])
```

```python
import math

import numpy as np
import jax
import jax.numpy as jnp
from jax import lax
from jax.experimental import pallas as pl
from jax.experimental.pallas import tpu as pltpu

F32 = jnp.float32
BF16 = jnp.bfloat16
SDS = jax.ShapeDtypeStruct
MESH = pl.DeviceIdType.MESH

D_MODEL = 1024
SEQ = 2048
HEAD_DIM = 64
GROUPS = ((128, 1), (512, 4), (2048, 16))
HEADS_PER_GROUP = 8
N_HEADS = 24
ATTN_WIDTH = N_HEADS * HEAD_DIM
ATTN_OUT = HEADS_PER_GROUP * HEAD_DIM
CONV_K = 31
CONV_PAD = 32
D_FF = 2816
IN_WIDTH = 3 * ATTN_WIDTH + 2 * D_MODEL + 2 * D_MODEL
RMS_EPS = 1e-6
LN_EPS = 1e-5
Q_BLOCK = 128
LANES = 128
NEG = -1e30
N_DEV = 8

ADAM_LR = 0.001
ADAM_B1 = 0.9
ADAM_B2 = 0.999
ADAM_EPS = 1e-08
ADAM_WD = 0.01
ADAM_STEP = 10


def _alibi_slope_list(n):
    def pow2(m):
        start = 2.0 ** (-8.0 / m)
        return [start ** (i + 1) for i in range(m)]
    if math.log2(n).is_integer():
        return pow2(n)
    c = 2 ** math.floor(math.log2(n))
    return pow2(c) + _alibi_slope_list(2 * c)[0::2][: n - c]


def _slopes_times_dilation():
    s = np.asarray(sorted(_alibi_slope_list(N_HEADS), reverse=True), dtype=np.float32).reshape(3, HEADS_PER_GROUP)
    r = np.asarray([g[1] for g in GROUPS], dtype=np.float32)[:, None]
    return (s * r).reshape(N_HEADS)


def _sigmoid(x):
    return 0.5 * jnp.tanh(0.5 * x) + 0.5


def _dot(a, b):
    return jnp.dot(a, b, preferred_element_type=F32)


def _dot_nt(a, b):
    return lax.dot_general(a, b, (((1,), (1,)), ((), ())), preferred_element_type=F32)


def _dot_tn(a, b):
    return lax.dot_general(a, b, (((0,), (0,)), ((), ())), preferred_element_type=F32)


def _rowsum(x):
    return jnp.sum(x, axis=0, keepdims=True)


ANY_SPEC = pl.BlockSpec(memory_space=pl.ANY)


def _params(*sem):
    return pltpu.CompilerParams(dimension_semantics=sem)


def _anchored(body, n_in, dep):
    if dep is None:
        return body, [], []

    def wrapped(*refs):
        return body(*refs[:n_in], *refs[n_in + 1:])

    return wrapped, [pl.BlockSpec(memory_space=pl.ANY)], [dep]


IN_TM = 256
IN_CHUNK = 512


def _in_proj(x, g1, w_in_t, dep=None):
    T = x.shape[0]
    tm = IN_TM

    def body(x_ref, g_ref, w_hbm, h_ref, proj_ref, w_vmem, sem):
        @pl.when(pl.program_id(0) == 0)
        def _():
            cp = pltpu.make_async_copy(w_hbm, w_vmem, sem)
            cp.start()
            cp.wait()

        xv = x_ref[...]
        r = lax.rsqrt(jnp.mean(xv * xv, axis=-1, keepdims=True) + RMS_EPS)
        h = (xv * r * g_ref[...]).astype(BF16)
        h_ref[...] = h
        for lo in range(0, IN_WIDTH, IN_CHUNK):
            proj_ref[:, lo:lo + IN_CHUNK] = _dot_nt(h, w_vmem[lo:lo + IN_CHUNK, :])

    row = lambda n: pl.BlockSpec((tm, n), lambda i: (i, 0))
    body, dep_spec, dep_arg = _anchored(body, 3, dep)
    return pl.pallas_call(
        body, grid=(T // tm,),
        in_specs=[row(D_MODEL), pl.BlockSpec((1, D_MODEL), lambda i: (0, 0)), pl.BlockSpec(memory_space=pl.ANY)] + dep_spec,
        out_specs=[row(D_MODEL), row(IN_WIDTH)],
        out_shape=[SDS((T, D_MODEL), BF16), SDS((T, IN_WIDTH), F32)],
        scratch_shapes=[pltpu.VMEM((IN_WIDTH, D_MODEL), BF16), pltpu.SemaphoreType.DMA],
        compiler_params=_params("arbitrary"), name="in_proj")(x, g1, w_in_t, *dep_arg)


def _mm_tn(a, b, out_dtype, name, tn, tt=1024):
    T, K = a.shape
    N = b.shape[1]
    nt = T // tt

    def body(a_ref, b_ref, o_ref, acc):
        t = pl.program_id(1)

        @pl.when(t == 0)
        def _():
            acc[...] = jnp.zeros_like(acc)

        acc[...] += _dot_tn(a_ref[...], b_ref[...])

        @pl.when(t == nt - 1)
        def _():
            o_ref[...] = acc[...].astype(o_ref.dtype)

    return pl.pallas_call(
        body, grid=(N // tn, nt),
        in_specs=[pl.BlockSpec((tt, K), lambda j, t: (t, 0)),
                  pl.BlockSpec((tt, tn), lambda j, t: (t, j))],
        out_specs=pl.BlockSpec((K, tn), lambda j, t: (0, j)),
        out_shape=SDS((K, N), out_dtype),
        scratch_shapes=[pltpu.VMEM((K, tn), F32)],
        compiler_params=_params("parallel", "arbitrary"), name=name)(a, b)


def _gather_classes(src_ref, dst, r, row0=0):
    L = SEQ // r
    for c in range(r):
        dst[row0 + c * L:row0 + (c + 1) * L, :] = src_ref[0, pl.ds(c, L, stride=r), :].astype(dst.dtype)


def _scatter_classes(src, dst, r, row0=0):
    L = SEQ // r
    for c in range(r):
        dst[pl.ds(c, L, stride=r), :] = src[row0 + c * L:row0 + (c + 1) * L, :].astype(dst.dtype)


def _attn_masks(slope_r):
    qi = lax.broadcasted_iota(jnp.int32, (Q_BLOCK, Q_BLOCK), 0)
    kj = lax.broadcasted_iota(jnp.int32, (Q_BLOCK, Q_BLOCK), 1)
    rel = (qi - kj).astype(F32)
    bias_cur = jnp.where(qi >= kj, -slope_r * rel, NEG)
    bias_prev = jnp.where(qi <= kj, -slope_r * (rel + float(Q_BLOCK)), NEG)
    return bias_cur, bias_prev


def _store_biases(bias, sl_ref, g, hp):
    for hh in range(2):
        cur, prev = _attn_masks(sl_ref[g * HEADS_PER_GROUP + 2 * hp + hh])
        rows = slice(hh * Q_BLOCK, (hh + 1) * Q_BLOCK)
        bias[0, rows, 0:Q_BLOCK] = prev
        bias[1, rows, 0:Q_BLOCK] = jnp.full((Q_BLOCK, Q_BLOCK), NEG, F32)
        bias[0, rows, Q_BLOCK:] = cur
        bias[1, rows, Q_BLOCK:] = cur


def _transpose_pairs(src, dst):
    dst[0, :, 0:Q_BLOCK] = jnp.zeros((LANES, Q_BLOCK), dst.dtype)
    nblk = SEQ // Q_BLOCK
    for b in range(nblk):
        t = src[(b + 1) * Q_BLOCK:(b + 2) * Q_BLOCK, :].T
        dst[b, :, Q_BLOCK:] = t
        if b + 1 < nblk:
            dst[b + 1, :, 0:Q_BLOCK] = t


def _stack_heads(t, low):
    z = jnp.zeros_like(t)
    return jnp.concatenate([jnp.where(low, t, z), jnp.where(low, z, t)], axis=0)


def _unstack_heads(t2, low):
    return jnp.where(low, t2[:Q_BLOCK], t2[Q_BLOCK:])


def _unit_offsets(u, nb):
    off = pl.multiple_of(u * Q_BLOCK, Q_BLOCK)
    n = u & (nb - 1)
    c = u >> int(math.log2(nb))
    return off, n == 0, c, n


ATTN_UNROLL = 8


def _attn_fwd(qkv, slopes_r, batch, dep=None):
    nblk = SEQ // Q_BLOCK

    def body(sl_ref, *refs):
        qkv_refs = refs[:9]
        att_ref, lse_ref = refs[9:11]
        qd, kd, vd, kt, opos, lpos, bias = refs[11:]
        hp = pl.program_id(1)
        low = lax.broadcasted_iota(jnp.int32, (Q_BLOCK, LANES), 1) < HEAD_DIM

        for g in range(3):
            r = GROUPS[g][1]
            nb = SEQ // r // Q_BLOCK
            _gather_classes(qkv_refs[3 * g], qd, r)
            kd[0:Q_BLOCK, :] = jnp.zeros((Q_BLOCK, LANES), BF16)
            vd[0:Q_BLOCK, :] = jnp.zeros((Q_BLOCK, LANES), BF16)
            _gather_classes(qkv_refs[3 * g + 1], kd, r, Q_BLOCK)
            _gather_classes(qkv_refs[3 * g + 2], vd, r, Q_BLOCK)
            _transpose_pairs(kd, kt)
            _store_biases(bias, sl_ref, g, hp)

            def unit(u, carry, g=g, r=r, nb=nb):
                off, first, c, n = _unit_offsets(u, nb)
                q2 = _stack_heads(qd[pl.ds(off, Q_BLOCK), :], low)
                s = _dot(q2, kt[u]) * 0.125 + bias[first.astype(jnp.int32)]
                m = jnp.max(s, axis=-1, keepdims=True)
                p = jnp.exp(s - m)
                l = jnp.sum(p, axis=-1, keepdims=True)
                o2 = _dot(p.astype(BF16), vd[pl.ds(off, 2 * Q_BLOCK), :]) * (1.0 / l)
                lse2 = m + jnp.log(l)
                rows = pl.ds(c + n * (Q_BLOCK * r), Q_BLOCK, stride=r)
                opos[g, rows, :] = _unstack_heads(o2, low)
                lpos[g, rows, :] = jnp.where(low, lse2[:Q_BLOCK], lse2[Q_BLOCK:])
                return carry

            lax.fori_loop(0, nblk, unit, 0, unroll=ATTN_UNROLL)

        def merge(i, carry):
            rows = pl.ds(pl.multiple_of(i * 256, 256), 256)
            l0, l1, l2 = lpos[0, rows, :], lpos[1, rows, :], lpos[2, rows, :]
            m = jnp.maximum(jnp.maximum(l0, l1), l2)
            e0, e1, e2 = jnp.exp(l0 - m), jnp.exp(l1 - m), jnp.exp(l2 - m)
            den = e0 + e1 + e2
            att = (e0 * opos[0, rows, :] + e1 * opos[1, rows, :] + e2 * opos[2, rows, :]) / den
            att_ref[0, rows, :] = att.astype(att_ref.dtype)
            lse_ref[0, rows, :] = m + jnp.log(den)
            return carry

        lax.fori_loop(0, SEQ // 256, merge, 0)

    def col(sec, g):
        return pl.BlockSpec((1, SEQ, LANES), lambda b, hp: (b, 0, sec * 12 + g * 4 + hp))

    out = pl.BlockSpec((1, SEQ, LANES), lambda b, hp: (b, 0, hp))
    body, dep_spec, dep_arg = _anchored(body, 10, dep)
    return pl.pallas_call(
        body, grid=(batch, 4),
        in_specs=[pl.BlockSpec(memory_space=pltpu.SMEM)] + [col(sec, g) for g in range(3) for sec in range(3)] + dep_spec,
        out_specs=[out, out],
        out_shape=[SDS((batch, SEQ, ATTN_OUT), BF16), SDS((batch, SEQ, ATTN_OUT), F32)],
        scratch_shapes=[pltpu.VMEM((SEQ, LANES), BF16), pltpu.VMEM((Q_BLOCK + SEQ, LANES), BF16),
                        pltpu.VMEM((Q_BLOCK + SEQ, LANES), BF16), pltpu.VMEM((nblk, LANES, 2 * Q_BLOCK), BF16),
                        pltpu.VMEM((3, SEQ, LANES), F32), pltpu.VMEM((3, SEQ, LANES), F32),
                        pltpu.VMEM((2, 2 * Q_BLOCK, 2 * Q_BLOCK), F32)],
        compiler_params=_params("parallel", "parallel"), name="attn_fwd")(slopes_r, *([qkv] * 9), *dep_arg)


def _attn_bwd(qkv, datt, lse, dsum, slopes_r, batch):
    nblk = SEQ // Q_BLOCK

    def body(sl_ref, q_ref, k_ref, v_ref, do_ref, l_ref, d_ref, dq_ref, dk_ref, dv_ref,
             qd, kd, vd, dod, kt, vt, ld, dd, dq_acc, dk_acc, dv_acc, dk_part, dv_part, stage, bias):
        gid, hp = pl.program_id(1), pl.program_id(2)
        low = lax.broadcasted_iota(jnp.int32, (Q_BLOCK, LANES), 1) < HEAD_DIM

        def section(g):
            r = GROUPS[g][1]
            nb = SEQ // r // Q_BLOCK
            _gather_classes(q_ref, qd, r)
            kd[0:Q_BLOCK, :] = jnp.zeros((Q_BLOCK, LANES), BF16)
            vd[0:Q_BLOCK, :] = jnp.zeros((Q_BLOCK, LANES), BF16)
            _gather_classes(k_ref, kd, r, Q_BLOCK)
            _gather_classes(v_ref, vd, r, Q_BLOCK)
            _gather_classes(do_ref, dod, r)
            _gather_classes(l_ref, ld, r)
            _gather_classes(d_ref, dd, r)
            _transpose_pairs(kd, kt)
            _transpose_pairs(vd, vt)
            _store_biases(bias, sl_ref, g, hp)

            def unit(u, carry):
                off, first, _, _ = _unit_offsets(u, nb)
                pair = pl.ds(off, 2 * Q_BLOCK)
                q2 = _stack_heads(qd[pl.ds(off, Q_BLOCK), :], low)
                do2 = _stack_heads(dod[pl.ds(off, Q_BLOCK), :], low)
                lse_t = ld[pl.ds(off, Q_BLOCK), :]
                dsum_t = dd[pl.ds(off, Q_BLOCK), :]
                lse2 = jnp.concatenate([lse_t[:, 0:1], lse_t[:, HEAD_DIM:HEAD_DIM + 1]], axis=0)
                dsum2 = jnp.concatenate([dsum_t[:, 0:1], dsum_t[:, HEAD_DIM:HEAD_DIM + 1]], axis=0)
                s = _dot(q2, kt[u]) * 0.125 + bias[first.astype(jnp.int32)]
                p = jnp.exp(s - lse2)
                ds = (p * (_dot(do2, vt[u]) - dsum2)).astype(BF16)
                dq_acc[pl.ds(off, Q_BLOCK), :] = _unstack_heads(_dot(ds, kd[pair, :]), low) * 0.125
                dk_part[u] = _dot_tn(ds, q2) * 0.125
                dv_part[u] = _dot_tn(p.astype(BF16), do2)
                return carry

            lax.fori_loop(0, nblk, unit, 0, unroll=ATTN_UNROLL)
            for part, acc in ((dk_part, dk_acc), (dv_part, dv_acc)):
                for b in range(nblk):
                    t = part[b, Q_BLOCK:, :]
                    if b + 1 < nblk:
                        t = t + part[b + 1, 0:Q_BLOCK, :]
                    acc[b * Q_BLOCK:(b + 1) * Q_BLOCK, :] = t
            for acc, out_ref in ((dq_acc, dq_ref), (dk_acc, dk_ref), (dv_acc, dv_ref)):
                _scatter_classes(acc, stage, r)
                out_ref[0] = stage[...].astype(out_ref.dtype)

        for g in range(3):
            pl.when(gid == g)(lambda g=g: section(g))

    def col(sec):
        return pl.BlockSpec((1, SEQ, LANES), lambda b, g, hp: (b, 0, sec * 12 + g * 4 + hp))

    pos = pl.BlockSpec((1, SEQ, LANES), lambda b, g, hp: (b, 0, hp))
    dout = pl.BlockSpec((1, SEQ, LANES), lambda b, g, hp: (b, 0, g * 4 + hp))
    out = SDS((batch, SEQ, ATTN_WIDTH), BF16)
    seq_bf = pltpu.VMEM((SEQ, LANES), BF16)
    seq_f = pltpu.VMEM((SEQ, LANES), F32)
    pad_bf = pltpu.VMEM((Q_BLOCK + SEQ, LANES), BF16)
    part = pltpu.VMEM((nblk, 2 * Q_BLOCK, LANES), F32)
    blk_t = pltpu.VMEM((nblk, LANES, 2 * Q_BLOCK), BF16)
    return pl.pallas_call(
        body, grid=(batch, 3, 4),
        in_specs=[pl.BlockSpec(memory_space=pltpu.SMEM), col(0), col(1), col(2), pos, pos, pos],
        out_specs=[dout, dout, dout],
        out_shape=[out, out, out],
        scratch_shapes=[seq_bf, pad_bf, pad_bf, seq_bf, blk_t, blk_t, seq_f, seq_f, seq_f, seq_f, seq_f, part, part, seq_f,
                        pltpu.VMEM((2, 2 * Q_BLOCK, 2 * Q_BLOCK), F32)],
        compiler_params=_params("parallel", "parallel", "parallel"), name="attn_bwd")(
            slopes_r, qkv, qkv, qkv, datt, lse, dsum)


CONV_TC = 128
U_BLOCK0 = 3 * ATTN_WIDTH // CONV_TC
CONV_ROWS = 128
SUBLANES = 8


def _fill_shifted(sh):
    n = SEQ + CONV_PAD - SUBLANES
    for s in range(1, SUBLANES):
        sh[s, 0:n, :] = sh[0, s:s + n, :]


def _tap(sh, base, offset):
    s = offset % SUBLANES
    return sh[s, pl.ds(pl.multiple_of(base + (offset - s), SUBLANES), CONV_ROWS), :]


def _conv_fwd(u, conv_w, conv_b, batch, dep=None):
    nct = D_MODEL // CONV_TC

    def body(ua_ref, ub_ref, w_ref, b_ref, o_ref, sh):
        sh[0, 0:CONV_PAD, :] = jnp.zeros((CONV_PAD, CONV_TC), F32)
        sh[0, CONV_PAD:, :] = ua_ref[0] * _sigmoid(ub_ref[0])
        _fill_shifted(sh)

        def chunk(c, carry):
            base = pl.multiple_of(c * CONV_ROWS, CONV_ROWS)
            acc = jnp.broadcast_to(b_ref[...], (CONV_ROWS, CONV_TC))
            for t in range(CONV_K):
                acc = acc + _tap(sh, base, t + CONV_PAD - (CONV_K - 1)) * w_ref[t:t + 1, :]
            o_ref[0, pl.ds(base, CONV_ROWS), :] = acc
            return carry

        lax.fori_loop(0, SEQ // CONV_ROWS, chunk, 0)

    body, dep_spec, dep_arg = _anchored(body, 4, dep)
    return pl.pallas_call(
        body, grid=(nct, batch),
        in_specs=[pl.BlockSpec((1, SEQ, CONV_TC), lambda j, b: (b, 0, U_BLOCK0 + j)),
                  pl.BlockSpec((1, SEQ, CONV_TC), lambda j, b: (b, 0, U_BLOCK0 + nct + j)),
                  pl.BlockSpec((CONV_PAD, CONV_TC), lambda j, b: (0, j)),
                  pl.BlockSpec((1, CONV_TC), lambda j, b: (0, j))] + dep_spec,
        out_specs=pl.BlockSpec((1, SEQ, CONV_TC), lambda j, b: (b, 0, j)),
        out_shape=SDS((batch, SEQ, D_MODEL), F32),
        scratch_shapes=[pltpu.VMEM((SUBLANES, SEQ + CONV_PAD, CONV_TC), F32)],
        compiler_params=_params("parallel", "parallel"), name="conv_fwd")(u, u, conv_w, conv_b, *dep_arg)


def _conv_bwd(u, dc1, conv_w, batch, dep=None):
    nct = D_MODEL // CONV_TC
    nchunk = SEQ // CONV_ROWS

    def body(ua_ref, ub_ref, d_ref, w_ref, dua_ref, dub_ref, gw_ref, gb_ref, shc, shd, gacc):
        b = pl.program_id(1)
        shc[0, 0:CONV_PAD, :] = jnp.zeros((CONV_PAD, CONV_TC), F32)
        shc[0, CONV_PAD:, :] = ua_ref[0] * _sigmoid(ub_ref[0])
        _fill_shifted(shc)
        shd[0, 0:SEQ, :] = d_ref[0]
        shd[0, SEQ:, :] = jnp.zeros((CONV_PAD, CONV_TC), F32)
        _fill_shifted(shd)

        @pl.when(b == 0)
        def _():
            gacc[...] = jnp.zeros_like(gacc)
            gb_ref[...] = jnp.zeros_like(gb_ref)

        gb_ref[...] += _rowsum(d_ref[0])

        def chunk(c, carry):
            base = pl.multiple_of(c * CONV_ROWS, CONV_ROWS)
            dcur = shd[0, pl.ds(base, CONV_ROWS), :]
            acc = jnp.zeros((CONV_ROWS, CONV_TC), F32)
            for t in range(CONV_K):
                acc = acc + _tap(shd, base, CONV_K - 1 - t) * w_ref[t:t + 1, :]
                prod = _tap(shc, base, t + CONV_PAD - (CONV_K - 1)) * dcur
                gacc[t] += jnp.sum(prod.reshape(CONV_ROWS // 8, 8, CONV_TC), axis=0)
            ua = ua_ref[0, pl.ds(base, CONV_ROWS), :]
            sg = _sigmoid(ub_ref[0, pl.ds(base, CONV_ROWS), :])
            dua_ref[0, pl.ds(base, CONV_ROWS), :] = (acc * sg).astype(dua_ref.dtype)
            dub_ref[0, pl.ds(base, CONV_ROWS), :] = (acc * ua * sg * (1.0 - sg)).astype(dub_ref.dtype)
            return carry

        lax.fori_loop(0, nchunk, chunk, 0)

        @pl.when(b == batch - 1)
        def _():
            for t in range(CONV_K):
                gw_ref[t:t + 1, :] = jnp.sum(gacc[t], axis=0, keepdims=True)
            gw_ref[CONV_K:CONV_PAD, :] = jnp.zeros((CONV_PAD - CONV_K, CONV_TC), F32)

    du = SDS((batch, SEQ, D_MODEL), BF16)
    body, dep_spec, dep_arg = _anchored(body, 4, dep)
    return pl.pallas_call(
        body, grid=(nct, batch),
        in_specs=[pl.BlockSpec((1, SEQ, CONV_TC), lambda j, b: (b, 0, U_BLOCK0 + j)),
                  pl.BlockSpec((1, SEQ, CONV_TC), lambda j, b: (b, 0, U_BLOCK0 + nct + j)),
                  pl.BlockSpec((1, SEQ, CONV_TC), lambda j, b: (b, 0, j)),
                  pl.BlockSpec((CONV_PAD, CONV_TC), lambda j, b: (0, j))] + dep_spec,
        out_specs=[pl.BlockSpec((1, SEQ, CONV_TC), lambda j, b: (b, 0, j)),
                   pl.BlockSpec((1, SEQ, CONV_TC), lambda j, b: (b, 0, j)),
                   pl.BlockSpec((CONV_PAD, CONV_TC), lambda j, b: (0, j)),
                   pl.BlockSpec((1, CONV_TC), lambda j, b: (0, j))],
        out_shape=[du, du, SDS((CONV_PAD, D_MODEL), F32), SDS((1, D_MODEL), F32)],
        scratch_shapes=[pltpu.VMEM((SUBLANES, SEQ + CONV_PAD, CONV_TC), F32),
                        pltpu.VMEM((SUBLANES, SEQ + CONV_PAD, CONV_TC), F32),
                        pltpu.VMEM((CONV_K, 8, CONV_TC), F32)],
        compiler_params=_params("parallel", "arbitrary"), name="conv_bwd")(u, u, dc1, conv_w, *dep_arg)


MID_TM = 256


def _layernorm_stats(c1):
    mu = jnp.mean(c1, axis=-1, keepdims=True)
    cen = c1 - mu
    rs = lax.rsqrt(jnp.mean(cen * cen, axis=-1, keepdims=True) + LN_EPS)
    return cen * rs, rs


GATE_PARTS = 4
GATE_PART = 2 * D_MODEL // GATE_PARTS
GATE_PART0 = (IN_WIDTH - 2 * D_MODEL) // GATE_PART


def _gate_specs(tm):
    return [pl.BlockSpec((tm, GATE_PART), lambda i, k=k: (i, GATE_PART0 + k)) for k in range(GATE_PARTS)]


def _mid_fwd(att, c1, proj, x, w_a, w_c, w_o, gate_b, ln_g, ln_b, g2, dep=None):
    T = x.shape[0]
    tm = MID_TM

    def body(att_ref, c1_ref, lg0, lg1, lg2, lg3, x_ref, wa_ref, wc_ref, wo_ref, gb_ref, lng_ref, lnb_ref, g2_ref,
             c3_ref, ya_ref, yc_ref, mix_ref, x1_ref, h2_ref):
        logits = jnp.concatenate([lg0[...], lg1[...], lg2[...], lg3[...]], axis=1)
        ya = _dot(att_ref[...], wa_ref[...])
        xh, _ = _layernorm_stats(c1_ref[...])
        c2 = xh * lng_ref[...] + lnb_ref[...]
        c3 = (c2 * _sigmoid(c2)).astype(BF16)
        c3_ref[...] = c3
        yc = _dot(c3, wc_ref[...])
        gates = _sigmoid(logits + gb_ref[...])
        mix = (gates[:, :D_MODEL] * ya + gates[:, D_MODEL:] * yc).astype(BF16)
        ya_ref[...] = ya.astype(BF16)
        yc_ref[...] = yc.astype(BF16)
        mix_ref[...] = mix
        x1 = x_ref[...] + _dot(mix, wo_ref[...])
        x1_ref[...] = x1
        r = lax.rsqrt(jnp.mean(x1 * x1, axis=-1, keepdims=True) + RMS_EPS)
        h2_ref[...] = (x1 * r * g2_ref[...]).astype(BF16)

    row = lambda n: pl.BlockSpec((tm, n), lambda i: (i, 0))
    full = lambda a, b: pl.BlockSpec((a, b), lambda i: (0, 0))
    body, dep_spec, dep_arg = _anchored(body, 10 + GATE_PARTS, dep)
    return pl.pallas_call(
        body, grid=(T // tm,),
        in_specs=[row(ATTN_OUT), row(D_MODEL)] + _gate_specs(tm) + [row(D_MODEL),
                  full(ATTN_OUT, D_MODEL), full(D_MODEL, D_MODEL), full(D_MODEL, D_MODEL),
                  full(1, 2 * D_MODEL), full(1, D_MODEL), full(1, D_MODEL), full(1, D_MODEL)] + dep_spec,
        out_specs=[row(D_MODEL), row(D_MODEL), row(D_MODEL), row(D_MODEL), row(D_MODEL), row(D_MODEL)],
        out_shape=[SDS((T, D_MODEL), BF16), SDS((T, D_MODEL), BF16), SDS((T, D_MODEL), BF16), SDS((T, D_MODEL), BF16),
                   SDS((T, D_MODEL), F32), SDS((T, D_MODEL), BF16)],
        compiler_params=_params("parallel"), name="mid_fwd")(att, c1, *([proj] * GATE_PARTS), x, w_a, w_c, w_o,
                                                             gate_b, ln_g, ln_b, g2, *dep_arg)


def _mid_bwd(dx1b, ya, yc, proj, att, c1, w_a, w_c, w_o, gate_b, ln_g, ln_b, head_ones, dep=None):
    T = dx1b.shape[0]
    tm = MID_TM

    def body(dx_ref, ya_ref, yc_ref, lg0, lg1, lg2, lg3, att_ref, c1_ref, wa_ref, wc_ref, wo_ref, gb_ref, lng_ref,
             lnb_ref, e_ref, dlg_ref, dya_ref, dyc_ref, datt_ref, dsum_ref, dc1_ref, ggb_ref, glg_ref, glb_ref):
        logits = jnp.concatenate([lg0[...], lg1[...], lg2[...], lg3[...]], axis=1)
        @pl.when(pl.program_id(0) == 0)
        def _():
            ggb_ref[...] = jnp.zeros_like(ggb_ref)
            glg_ref[...] = jnp.zeros_like(glg_ref)
            glb_ref[...] = jnp.zeros_like(glb_ref)

        dmix = _dot_nt(dx_ref[...], wo_ref[...])
        gates = _sigmoid(logits + gb_ref[...])
        ga, gc = gates[:, :D_MODEL], gates[:, D_MODEL:]
        dla = dmix * ya_ref[...].astype(F32) * ga * (1.0 - ga)
        dlc = dmix * yc_ref[...].astype(F32) * gc * (1.0 - gc)
        dlg_ref[:, :D_MODEL] = dla.astype(BF16)
        dlg_ref[:, D_MODEL:] = dlc.astype(BF16)
        ggb_ref[:, :D_MODEL] += _rowsum(dla)
        ggb_ref[:, D_MODEL:] += _rowsum(dlc)
        dya = (dmix * ga).astype(BF16)
        dyc = (dmix * gc).astype(BF16)
        dya_ref[...] = dya
        dyc_ref[...] = dyc
        datt = _dot_nt(dya, wa_ref[...])
        datt_ref[...] = datt
        dsum_ref[...] = jnp.dot(datt * att_ref[...].astype(F32), e_ref[...], preferred_element_type=F32,
                                precision=lax.Precision.HIGHEST)
        dc3 = _dot_nt(dyc, wc_ref[...])
        xh, rs = _layernorm_stats(c1_ref[...])
        c2 = xh * lng_ref[...] + lnb_ref[...]
        sg = _sigmoid(c2)
        dc2 = dc3 * (sg * (1.0 + c2 * (1.0 - sg)))
        glg_ref[...] += _rowsum(dc2 * xh)
        glb_ref[...] += _rowsum(dc2)
        dxh = dc2 * lng_ref[...]
        dc1_ref[...] = rs * (dxh - jnp.mean(dxh, axis=-1, keepdims=True) - xh * jnp.mean(dxh * xh, axis=-1, keepdims=True))

    row = lambda n: pl.BlockSpec((tm, n), lambda i: (i, 0))
    full = lambda a, b: pl.BlockSpec((a, b), lambda i: (0, 0))
    body, dep_spec, dep_arg = _anchored(body, 12 + GATE_PARTS, dep)
    return pl.pallas_call(
        body, grid=(T // tm,),
        in_specs=[row(D_MODEL), row(D_MODEL), row(D_MODEL)] + _gate_specs(tm) + [row(ATTN_OUT), row(D_MODEL),
                  full(ATTN_OUT, D_MODEL), full(D_MODEL, D_MODEL), full(D_MODEL, D_MODEL),
                  full(1, 2 * D_MODEL), full(1, D_MODEL), full(1, D_MODEL), full(ATTN_OUT, ATTN_OUT)] + dep_spec,
        out_specs=[row(2 * D_MODEL), row(D_MODEL), row(D_MODEL), row(ATTN_OUT), row(ATTN_OUT), row(D_MODEL),
                   full(1, 2 * D_MODEL), full(1, D_MODEL), full(1, D_MODEL)],
        out_shape=[SDS((T, 2 * D_MODEL), BF16), SDS((T, D_MODEL), BF16), SDS((T, D_MODEL), BF16), SDS((T, ATTN_OUT), F32),
                   SDS((T, ATTN_OUT), F32), SDS((T, D_MODEL), F32),
                   SDS((1, 2 * D_MODEL), F32), SDS((1, D_MODEL), F32), SDS((1, D_MODEL), F32)],
        compiler_params=_params("arbitrary"), name="mid_bwd")(dx1b, ya, yc, *([proj] * GATE_PARTS), att, c1, w_a, w_c, w_o,
                                                               gate_b, ln_g, ln_b, head_ones, *dep_arg)


FFN_TM = 256
FFN_CHUNK = 512
FFN_SUB = tuple((lo, min(lo + FFN_CHUNK, D_FF)) for lo in range(0, D_FF, FFN_CHUNK))


def _rms_bwd(dy_times_g, xh, r):
    return r * (dy_times_g - xh * jnp.mean(dy_times_g * xh, axis=-1, keepdims=True))


def _load_resident(pairs, sems):
    @pl.when(pl.program_id(0) == 0)
    def _():
        copies = [pltpu.make_async_copy(src, dst, sems.at[k]) for k, (src, dst) in enumerate(pairs)]
        for cp in copies:
            cp.start()
        for cp in copies:
            cp.wait()


def _ffn_fwd(h2, x1, target, gf, w_g_t, w_u_t, w_d):
    T = h2.shape[0]
    tm = FFN_TM

    def body(h_ref, x1_ref, t_ref, gf_ref, wg_hbm, wu_hbm, wd_hbm,
             a_ref, b_ref, f_ref, dx2_ref, dx2b_ref, loss_ref, gnf_ref, wg, wu, wd, sems):
        _load_resident(((wg_hbm, wg), (wu_hbm, wu), (wd_hbm, wd)), sems)

        @pl.when(pl.program_id(0) == 0)
        def _():
            loss_ref[...] = jnp.zeros_like(loss_ref)
            gnf_ref[...] = jnp.zeros_like(gnf_ref)

        h = h_ref[...]
        x2 = x1_ref[...]
        for lo, hi in FFN_SUB:
            a = _dot_nt(h, wg[lo:hi, :])
            b = _dot_nt(h, wu[lo:hi, :])
            f = (a * _sigmoid(a) * b).astype(BF16)
            a_ref[:, lo:hi] = a.astype(BF16)
            b_ref[:, lo:hi] = b.astype(BF16)
            f_ref[:, lo:hi] = f
            x2 = x2 + _dot(f, wd[lo:hi, :])

        r = lax.rsqrt(jnp.mean(x2 * x2, axis=-1, keepdims=True) + RMS_EPS)
        xh = x2 * r
        err = xh * gf_ref[...] - t_ref[...]
        loss_ref[...] += (0.5 / D_MODEL) * jnp.sum(err * err)
        dy = err * (1.0 / D_MODEL)
        gnf_ref[...] += _rowsum(dy * xh)
        dx2 = _rms_bwd(dy * gf_ref[...], xh, r)
        dx2_ref[...] = dx2
        dx2b_ref[...] = dx2.astype(BF16)

    row = lambda n: pl.BlockSpec((tm, n), lambda i: (i, 0))
    const = lambda n: pl.BlockSpec((1, n), lambda i: (0, 0))
    wshape = pltpu.VMEM((D_FF, D_MODEL), BF16)
    return pl.pallas_call(
        body, grid=(T // tm,),
        in_specs=[row(D_MODEL), row(D_MODEL), row(D_MODEL), const(D_MODEL), ANY_SPEC, ANY_SPEC, ANY_SPEC],
        out_specs=[row(D_FF), row(D_FF), row(D_FF), row(D_MODEL), row(D_MODEL), const(128), const(D_MODEL)],
        out_shape=[SDS((T, D_FF), BF16), SDS((T, D_FF), BF16), SDS((T, D_FF), BF16), SDS((T, D_MODEL), F32),
                   SDS((T, D_MODEL), BF16), SDS((1, 128), F32), SDS((1, D_MODEL), F32)],
        scratch_shapes=[wshape, wshape, wshape, pltpu.SemaphoreType.DMA((3,))],
        compiler_params=_params("arbitrary"), name="ffn_fwd")(h2, x1, target, gf, w_g_t, w_u_t, w_d)


def _ffn_bwd(dx2b, dx2, a, b, x1, g2, w_g_t, w_u_t, w_d):
    T = dx2.shape[0]
    tm = FFN_TM

    def body(dxb_ref, dx2_ref, a_ref, b_ref, x1_ref, g2_ref, wg_hbm, wu_hbm, wd_hbm,
             da_ref, db_ref, dx1_ref, dx1b_ref, gn2_ref, wg, wu, wd, sems):
        _load_resident(((wg_hbm, wg), (wu_hbm, wu), (wd_hbm, wd)), sems)

        @pl.when(pl.program_id(0) == 0)
        def _():
            gn2_ref[...] = jnp.zeros_like(gn2_ref)

        dxb = dxb_ref[...]
        dh2 = jnp.zeros((tm, D_MODEL), F32)
        for lo, hi in FFN_SUB:
            df = _dot_nt(dxb, wd[lo:hi, :])
            av = a_ref[:, lo:hi].astype(F32)
            bv = b_ref[:, lo:hi].astype(F32)
            sg = _sigmoid(av)
            db = (df * av * sg).astype(BF16)
            da = (df * bv * (sg * (1.0 + av * (1.0 - sg)))).astype(BF16)
            da_ref[:, lo:hi] = da
            db_ref[:, lo:hi] = db
            dh2 = dh2 + _dot(da, wg[lo:hi, :]) + _dot(db, wu[lo:hi, :])

        x1 = x1_ref[...]
        r = lax.rsqrt(jnp.mean(x1 * x1, axis=-1, keepdims=True) + RMS_EPS)
        xh = x1 * r
        gn2_ref[...] += _rowsum(dh2 * xh)
        dx1 = dx2_ref[...] + _rms_bwd(dh2 * g2_ref[...], xh, r)
        dx1_ref[...] = dx1
        dx1b_ref[...] = dx1.astype(BF16)

    row = lambda n: pl.BlockSpec((tm, n), lambda i: (i, 0))
    const = lambda n: pl.BlockSpec((1, n), lambda i: (0, 0))
    wshape = pltpu.VMEM((D_FF, D_MODEL), BF16)
    return pl.pallas_call(
        body, grid=(T // tm,),
        in_specs=[row(D_MODEL), row(D_MODEL), row(D_FF), row(D_FF), row(D_MODEL), const(D_MODEL),
                  ANY_SPEC, ANY_SPEC, ANY_SPEC],
        out_specs=[row(D_FF), row(D_FF), row(D_MODEL), row(D_MODEL), const(D_MODEL)],
        out_shape=[SDS((T, D_FF), BF16), SDS((T, D_FF), BF16), SDS((T, D_MODEL), F32), SDS((T, D_MODEL), BF16),
                   SDS((1, D_MODEL), F32)],
        scratch_shapes=[wshape, wshape, wshape, pltpu.SemaphoreType.DMA((3,))],
        compiler_params=_params("arbitrary"), name="ffn_bwd")(dx2b, dx2, a, b, x1, g2, w_g_t, w_u_t, w_d)


def _in_bwd(pieces, w_in_t, x, dx1, g1, dep=None):
    T = x.shape[0]
    tm = IN_TM
    npc = len(pieces)
    assert sum(p.shape[1] for p in pieces) == IN_WIDTH

    def body(*refs):
        p_refs = refs[:npc]
        w_hbm, x_ref, dx1_ref, g_ref, dx_ref, gn1_ref, w_vmem, sem = refs[npc:]

        @pl.when(pl.program_id(0) == 0)
        def _():
            cp = pltpu.make_async_copy(w_hbm, w_vmem, sem)
            cp.start()
            cp.wait()
            gn1_ref[...] = jnp.zeros_like(gn1_ref)

        dh = jnp.zeros((tm, D_MODEL), F32)
        col = 0
        for p_ref in p_refs:
            for j in range(p_ref.shape[1] // IN_CHUNK):
                dh = dh + _dot(p_ref[:, j * IN_CHUNK:(j + 1) * IN_CHUNK], w_vmem[col:col + IN_CHUNK, :])
                col += IN_CHUNK
        xv = x_ref[...]
        r = lax.rsqrt(jnp.mean(xv * xv, axis=-1, keepdims=True) + RMS_EPS)
        xh = xv * r
        gn1_ref[...] += _rowsum(dh * xh)
        dx_ref[...] = dx1_ref[...] + _rms_bwd(dh * g_ref[...], xh, r)

    row = lambda n: pl.BlockSpec((tm, n), lambda i: (i, 0))
    body, dep_spec, dep_arg = _anchored(body, npc + 4, dep)
    return pl.pallas_call(
        body, grid=(T // tm,),
        in_specs=[row(p.shape[1]) for p in pieces]
        + [pl.BlockSpec(memory_space=pl.ANY), row(D_MODEL), row(D_MODEL), pl.BlockSpec((1, D_MODEL), lambda i: (0, 0))]
        + dep_spec,
        out_specs=[row(D_MODEL), pl.BlockSpec((1, D_MODEL), lambda i: (0, 0))],
        out_shape=[SDS((T, D_MODEL), F32), SDS((1, D_MODEL), F32)],
        scratch_shapes=[pltpu.VMEM((IN_WIDTH, D_MODEL), BF16), pltpu.SemaphoreType.DMA],
        compiler_params=_params("arbitrary"), name="in_bwd")(*pieces, w_in_t, x, dx1, g1, *dep_arg)


def _local_step(x, target, in_proj, small, late_weights=None, emit=None):
    T = x.shape[0]
    batch = T // SEQ
    slopes_r = jnp.asarray(_slopes_times_dilation())
    emit = emit or (lambda names, grads: None)

    h, proj, w = in_proj()
    proj3 = proj.reshape(batch, SEQ, IN_WIDTH)

    att, lse = _attn_fwd(proj3, slopes_r, batch, w.get("token"))
    att = att.reshape(T, ATTN_OUT)
    if late_weights is not None:
        w = {**w, **late_weights("after_attention", att)}

    c1 = _conv_fwd(proj3, w["conv_w"], small["conv_b"], batch, w.get("token")).reshape(T, D_MODEL)
    if late_weights is not None:
        w = {**w, **late_weights(LATE_MERGE, (att, c1))}

    c3, ya, yc, mix, x1, h2 = _mid_fwd(
        att, c1, proj, x, w["w_attn_out"], w["w_conv_out"], w["w_o"],
        small["gate_b"], small["conv_ln_g"], small["conv_ln_b"], small["norm2_g"], w.get("token"))
    if late_weights is not None:
        w = {**w, **late_weights(LATE_FFN, h2)}

    a, b, f, dx2, dx2b, loss, g_normf = _ffn_fwd(h2, x1, target, small["norm_f_g"],
                                                   w["w_ffn_gate"], w["w_ffn_up"], w["w_ffn_down"])

    da, db, dx1, dx1b, g_norm2 = _ffn_bwd(dx2b, dx2, a, b, x1, small["norm2_g"],
                                           w["w_ffn_gate"], w["w_ffn_up"], w["w_ffn_down"])
    gw = {}
    gw["w_ffn_down"] = _mm_tn(f, dx2b, BF16, "gw_ffn_down", tn=1024)
    gw["w_ffn_gate"] = _mm_tn(da, h2, BF16, "gw_ffn_gate", tn=1024)
    gw["w_ffn_up"] = _mm_tn(db, h2, BF16, "gw_ffn_up", tn=1024)
    token = emit(("w_ffn_gate", "w_ffn_up", "w_ffn_down"), gw)

    head_ones = jnp.asarray(np.kron(np.eye(HEADS_PER_GROUP, dtype=np.float32), np.ones((HEAD_DIM, HEAD_DIM), np.float32)))
    dlogits, dya, dyc, datt, dsum, dc1, g_gate_b, g_ln_g, g_ln_b = _mid_bwd(
        dx1b, ya, yc, proj, att, c1, w["w_attn_out"], w["w_conv_out"], w["w_o"],
        small["gate_b"], small["conv_ln_g"], small["conv_ln_b"], head_ones, token)
    gw["w_o"] = _mm_tn(mix, dx1b, BF16, "gw_o", tn=1024)
    gw["w_attn_out"] = _mm_tn(att, dya, BF16, "gw_attn_out", tn=1024)
    gw["w_conv_out"] = _mm_tn(c3, dyc, BF16, "gw_conv_out", tn=1024)
    token = emit(("w_conv_out", "w_attn_out", "w_o"), gw)

    dua, dub, g_conv_w, g_conv_b = _conv_bwd(proj3, dc1.reshape(batch, SEQ, D_MODEL), w["conv_w"], batch, token)

    dq, dk, dv = _attn_bwd(proj3, datt.reshape(batch, SEQ, ATTN_OUT), lse, dsum.reshape(batch, SEQ, ATTN_OUT),
                           slopes_r, batch)
    pieces = [dq.reshape(T, ATTN_WIDTH), dk.reshape(T, ATTN_WIDTH), dv.reshape(T, ATTN_WIDTH),
              dua.reshape(T, D_MODEL), dub.reshape(T, D_MODEL), dlogits]

    names = ("q", "k", "v", "ua", "ub", "gate")
    gw["w_in"] = jnp.concatenate([_mm_tn(p, h, BF16, "gw_in_" + nm, tn=1024) for nm, p in zip(names, pieces)], axis=0)
    gw["conv_w"] = g_conv_w
    token = emit(("w_in", "conv_w"), gw)
    grad_x, g_norm1 = _in_bwd(pieces, w["w_in"], x, dx1, small["norm1_g"], token)

    gsmall = {"norm1_g": g_norm1, "gate_b": g_gate_b, "conv_b": g_conv_b, "conv_ln_g": g_ln_g, "conv_ln_b": g_ln_b,
              "norm2_g": g_norm2, "norm_f_g": g_normf}
    return loss, grad_x, gw, gsmall


ANY = pl.BlockSpec(memory_space=pl.ANY)


def _all_gather(arrs):
    n = len(arrs)

    def body(*refs):
        ins, outs = refs[:n], refs[n:2 * n]
        send_sems, recv_sems, local_sems = refs[2 * n:]
        x, y, c = lax.axis_index("x"), lax.axis_index("y"), lax.axis_index("c")
        me, sibling = (x, y, c), (x, y, 1 - c)
        chips = [(1 - x, y), (x, 1 - y), (1 - x, 1 - y)]

        def copy(a, k, block, to, src=None):
            px, py, pc = block
            dst = outs[a].at[4 * px + 2 * py + pc]
            return pltpu.make_async_remote_copy(
                src_ref=dst if src is None else src, dst_ref=dst,
                send_sem=send_sems.at[a, k], recv_sem=recv_sems.at[a, k], device_id=to, device_id_type=MESH)

        mine = [pltpu.make_async_copy(ins[a], outs[a].at[4 * x + 2 * y + c], local_sems.at[a]) for a in range(n)]
        for cp in mine:
            cp.start()
        first = []
        for j, chip in enumerate(chips):
            first += [copy(a, 1 + j, me, (*chip, c), src=ins[a]) for a in range(n)]
        first += [copy(a, 0, me, sibling, src=ins[a]) for a in range(n)]
        for cp in first:
            cp.start()
        passed = []
        for j, chip in enumerate(chips):
            for a in range(n):
                copy(a, 1 + j, (*chip, c), me).wait_recv()
                cp = copy(a, 4 + j, (*chip, c), sibling)
                cp.start()
                passed.append(cp)
        for a in range(n):
            copy(a, 0, sibling, me).wait_recv()
        for j, chip in enumerate(chips):
            for a in range(n):
                copy(a, 4 + j, (*chip, 1 - c), me).wait_recv()
        for cp in first + passed:
            cp.wait_send()
        for cp in mine:
            cp.wait()

    return pl.pallas_call(
        body, in_specs=[ANY] * n, out_specs=[ANY] * n,
        out_shape=[SDS((N_DEV,) + a.shape, a.dtype) for a in arrs],
        scratch_shapes=[pltpu.SemaphoreType.DMA((n, 7)), pltpu.SemaphoreType.DMA((n, 7)), pltpu.SemaphoreType.DMA((n,))],
        name="all_gather_weights")(*arrs)


HBM =pl.BlockSpec(memory_space=pltpu.HBM)
SEM = pl.BlockSpec(memory_space=pltpu.SEMAPHORE)
ALL_PEERS = tuple(range(1, N_DEV))
OTHER_CHIPS = (2, 4, 6)
SPLIT_EFFECT = pltpu.CompilerParams(has_side_effects=pltpu.SideEffectType.DATAFLOW_SIDE_EFFECTING)


def _exchange_copies(mode, ks, srcs, lands, send_sems, recv_sems):
    x, y, c = lax.axis_index("x"), lax.axis_index("y"), lax.axis_index("c")
    me = 4 * x + 2 * y + c
    send, recv = [], []
    for a in range(len(lands)):
        for i, k in enumerate(ks):
            peer = (x ^ ((k >> 2) & 1), y ^ ((k >> 1) & 1), c ^ (k & 1))
            pidx = 4 * peer[0] + 2 * peer[1] + peer[2]
            if mode == "gather":
                src, to, out_slot, in_slot = srcs[a], peer, me, pidx
            elif mode == "scatter":
                src, to, out_slot, in_slot = srcs[a].at[pidx], peer, me, pidx
            elif mode == "chip_scatter":
                src, to, out_slot, in_slot = srcs[a].at[pidx >> 1], peer, me >> 1, pidx >> 1
            else:
                src, to, out_slot, in_slot = lands[a].at[pidx], (x, y, 1 - c), pidx, pidx ^ 1
            s = a * len(ks) + i
            send.append(pltpu.make_async_remote_copy(
                src_ref=src, dst_ref=lands[a].at[out_slot], send_sem=send_sems.at[s], recv_sem=recv_sems.at[s],
                device_id=to, device_id_type=MESH))
            recv.append(pltpu.make_async_remote_copy(
                src_ref=src, dst_ref=lands[a].at[in_slot], send_sem=send_sems.at[s], recv_sem=recv_sems.at[s],
                device_id=to, device_id_type=MESH))
    return send, recv


def _send_start(mode, ks, name, srcs=(), lands=None, dep=None):
    srcs = list(srcs)
    if lands is None:
        slots = 4 if mode == "chip_scatter" else N_DEV
        lands = [lax.empty((slots,) + (s.shape if mode == "gather" else s.shape[1:]), s.dtype) for s in srcs]
    ns, nl = len(srcs), len(lands)
    nsem = nl * len(ks)

    def body(*refs):
        send, _ = _exchange_copies(mode, ks, refs[:ns], refs[ns:ns + nl], refs[ns + nl], refs[ns + nl + 1])
        for cp in send:
            cp.start()
        token = refs[-1]
        token[...] = jnp.zeros_like(token)

    both = srcs + list(lands)
    body, dep_spec, dep_arg = _anchored(body, ns + nl, dep)
    res = pl.pallas_call(
        body, name=name,
        out_shape=(pltpu.SemaphoreType.DMA((nsem,)), pltpu.SemaphoreType.DMA((nsem,)),
                   *[pltpu.HBM(a.shape, a.dtype) for a in both], SDS((8, 128), F32)),
        in_specs=[HBM] * (ns + nl) + dep_spec,
        out_specs=(SEM, SEM, *([HBM] * (ns + nl)), pl.BlockSpec(memory_space=pltpu.VMEM)),
        input_output_aliases={i: 2 + i for i in range(ns + nl)}, compiler_params=SPLIT_EFFECT,
    )(*[pltpu.with_memory_space_constraint(a, pltpu.HBM) for a in both], *dep_arg)
    return dict(mode=mode, ks=ks, send_sems=res[0], recv_sems=res[1], srcs=res[2:2 + ns], lands=res[2 + ns:2 + ns + nl],
                token=res[-1])


def _send_wait(started, after, name):
    ns, nl = len(started["srcs"]), len(started["lands"])

    def body(*refs):
        send, recv = _exchange_copies(started["mode"], started["ks"], refs[:ns], refs[ns:ns + nl],
                                      refs[ns + nl], refs[ns + nl + 1])
        for cp in send:
            cp.wait_send()
        for cp in recv:
            cp.wait_recv()

    both = list(started["srcs"]) + list(started["lands"])
    after = after if isinstance(after, (tuple, list)) else (after,)
    res = pl.pallas_call(
        body, name=name,
        out_shape=tuple(pltpu.HBM(a.shape, a.dtype) for a in both),
        in_specs=[HBM] * (ns + nl) + [SEM, SEM] + [ANY] * len(after), out_specs=tuple([HBM] * (ns + nl)),
        input_output_aliases={i: i for i in range(ns + nl)}, compiler_params=SPLIT_EFFECT,
    )(*both, started["send_sems"], started["recv_sems"], *after)
    return res[:ns], res[ns:]


def _exchange_sibling(gs):
    n = len(gs)

    def body(*refs):
        ins, outs = refs[:n], refs[n:2 * n]
        send_sems, recv_sems = refs[2 * n:]
        x, y, c = lax.axis_index("x"), lax.axis_index("y"), lax.axis_index("c")
        copies = []
        for a in range(n):
            for j in range(4):
                copies.append(pltpu.make_async_remote_copy(
                    src_ref=ins[a].at[2 * j + (1 - c)], dst_ref=outs[a].at[j],
                    send_sem=send_sems.at[a, j], recv_sem=recv_sems.at[a, j],
                    device_id=(x, y, 1 - c), device_id_type=MESH))
        for cp in copies:
            cp.start()
        for cp in copies:
            cp.wait_recv()
        for cp in copies:
            cp.wait_send()

    return pl.pallas_call(
        body, in_specs=[ANY] * n, out_specs=[ANY] * n,
        out_shape=[SDS((4,) + g.shape[1:], g.dtype) for g in gs],
        scratch_shapes=[pltpu.SemaphoreType.DMA((n, 4)), pltpu.SemaphoreType.DMA((n, 4))],
        name="reduce_scatter_sibling")(*gs)


def _add_pair(g, r1, core, name):
    _, rows, cols = g.shape
    tr = _row_tile(rows, cols, 3 * g.dtype.itemsize)

    def body(c_ref, g_ref, r_ref, o_ref):
        o_ref[...] = (g_ref[...].astype(F32) + r_ref[...].astype(F32)).astype(o_ref.dtype)

    return pl.pallas_call(
        body,
        grid_spec=pltpu.PrefetchScalarGridSpec(
            num_scalar_prefetch=1, grid=(4, rows // tr),
            in_specs=[pl.BlockSpec((1, tr, cols), lambda j, i, c_ref: (2 * j + c_ref[0], i, 0)),
                      pl.BlockSpec((1, tr, cols), lambda j, i, c_ref: (j, i, 0))],
            out_specs=pl.BlockSpec((1, tr, cols), lambda j, i, c_ref: (j, i, 0))),
        out_shape=SDS((4, rows, cols), g.dtype),
        compiler_params=_params("parallel", "parallel"), name=name)(core, g, r1)


def _row_tile(rows, cols, itemsize_total):
    budget = (4 << 20) // max(1, cols * itemsize_total)
    if rows <= budget:
        return rows
    t = rows
    while t > budget and t % 2 == 0 and (t // 2) % 16 == 0:
        t //= 2
    return t


def _adam_math(g, w, m, v):
    m_new = ADAM_B1 * m + (1.0 - ADAM_B1) * g
    v_new = ADAM_B2 * v + (1.0 - ADAM_B2) * (g * g)
    m_hat = m_new / (1.0 - ADAM_B1 ** ADAM_STEP)
    v_hat = v_new / (1.0 - ADAM_B2 ** ADAM_STEP)
    delta = -ADAM_LR * (m_hat / (jnp.sqrt(v_hat) + ADAM_EPS) + ADAM_WD * w)
    return delta, m_new, v_new


def _sum_adam(parts, own, mine, w, m, v, name):
    rows, cols = w.shape
    nparts = parts.shape[0]
    tr = _row_tile(rows, cols, (nparts + 1) * parts.dtype.itemsize + 7 * 4)

    def body(mine_ref, p_ref, own_ref, w_ref, m_ref, v_ref, g_ref, d_ref, mo_ref, vo_ref):
        g = None
        for s in range(nparts):
            part = jnp.where(mine_ref[0] == s, own_ref[0], p_ref[s]).astype(F32)
            g = part if g is None else g + part
        delta, m_new, v_new = _adam_math(g, w_ref[...], m_ref[...], v_ref[...])
        g_ref[...] = g
        d_ref[...] = delta
        mo_ref[...] = m_new
        vo_ref[...] = v_new

    blk = pl.BlockSpec((tr, cols), lambda i, mine_ref: (i, 0))
    out = SDS((rows, cols), F32)
    return pl.pallas_call(
        body,
        grid_spec=pltpu.PrefetchScalarGridSpec(
            num_scalar_prefetch=1, grid=(rows // tr,),
            in_specs=[pl.BlockSpec((nparts, tr, cols), lambda i, mine_ref: (0, i, 0)),
                      pl.BlockSpec((1, tr, cols), lambda i, mine_ref: (mine_ref[0], i, 0)), blk, blk, blk],
            out_specs=[blk, blk, blk, blk]),
        out_shape=[out, out, out, out],
        compiler_params=_params("parallel"), name=name)(mine, parts, own, w, m, v)


SMALL_ROWS = 72


def _small_allreduce_adam(gpart, w, m, v, row_counts, dep=None):
    def reduce_body(g_ref, go_ref, gath, send_sems, recv_sems):
        x, y, c = lax.axis_index("x"), lax.axis_index("y"), lax.axis_index("c")
        me = 4 * x + 2 * y + c
        gath[me] = g_ref[...]
        copies = []
        for k in range(1, N_DEV):
            fx, fy, fc = (k >> 2) & 1, (k >> 1) & 1, k & 1
            peer = (x ^ fx, y ^ fy, c ^ fc)
            copies.append(pltpu.make_async_remote_copy(
                src_ref=gath.at[me], dst_ref=gath.at[me], send_sem=send_sems.at[k - 1], recv_sem=recv_sems.at[k - 1],
                device_id=peer, device_id_type=MESH))
        for cp in copies:
            cp.start()
        for cp in copies:
            cp.wait_recv()
        for cp in copies:
            cp.wait_send()
        g = gath[0]
        for d in range(1, N_DEV):
            g = g + gath[d]
        go_ref[...] = g

    def adam_body(g_ref, w_ref, m_ref, v_ref, *out_refs):
        g = g_ref[...]
        delta, m_new, v_new = _adam_math(g, w_ref[...], m_ref[...], v_ref[...])
        outs = iter(out_refs)
        for val in (g, delta, m_new, v_new):
            lo = 0
            for r in row_counts:
                next(outs)[...] = val[lo:lo + r]
                lo += r
        next(outs)[...] = g[SMALL_ROWS - SUBLANES:]

    vm = pl.BlockSpec(memory_space=pltpu.VMEM)
    reduce_body, dep_spec, dep_arg = _anchored(reduce_body, 1, dep)
    total = pl.pallas_call(
        reduce_body, in_specs=[vm] + dep_spec, out_specs=vm, out_shape=SDS((SMALL_ROWS, 128), F32),
        scratch_shapes=[pltpu.VMEM((N_DEV, SMALL_ROWS, 128), F32), pltpu.SemaphoreType.DMA((N_DEV - 1,)),
                        pltpu.SemaphoreType.DMA((N_DEV - 1,))],
        name="small_allreduce")(gpart, *dep_arg)
    out_shape = [SDS((r, 128), F32) for _ in range(4) for r in row_counts] + [SDS((SUBLANES, 128), F32)]
    res = pl.pallas_call(adam_body, in_specs=[vm] * 4, out_specs=[vm] * len(out_shape), out_shape=out_shape,
                         name="small_adam")(total, w, m, v)
    k = len(row_counts)
    return [res[i * k:(i + 1) * k] for i in range(4)], res[-1]


BIG = ("w_in", "conv_w", "w_conv_out", "w_attn_out", "w_o", "w_ffn_gate", "w_ffn_up", "w_ffn_down")
LATE_MERGE = ("w_conv_out", "w_attn_out", "w_o")
LATE_FFN = ("w_ffn_gate", "w_ffn_up", "w_ffn_down")
TRANSPOSED = ("w_in", "w_ffn_gate", "w_ffn_up")
COL_SHARDED = ("conv_w", "w_attn_out")
SMALL = ("norm1_g", "gate_b", "conv_b", "conv_ln_g", "conv_ln_b", "norm2_g", "norm_f_g")
WEIGHTS = ("norm1_g", "w_in", "gate_b", "conv_w", "conv_b", "conv_ln_g", "conv_ln_b", "w_conv_out", "w_attn_out", "w_o",
           "norm2_g", "w_ffn_gate", "w_ffn_up", "w_ffn_down", "norm_f_g")


def _shard2d(name, a):
    a = a.reshape(a.shape[-2], a.shape[-1])
    if name in TRANSPOSED:
        a = a.T
    if name == "conv_w":
        a = jnp.pad(a, ((0, CONV_PAD - CONV_K), (0, 0)))
    return a


def _from_shard2d(name, val, shape):
    if name in TRANSPOSED:
        val = val.T
    if name == "conv_w":
        val = val[:CONV_K]
    return val.reshape(shape)


def _gathered_to_full(name, g):
    if name in COL_SHARDED:
        return g.transpose(1, 0, 2).reshape(g.shape[1], N_DEV * g.shape[2])
    return g.reshape(N_DEV * g.shape[1], g.shape[2])


def _full_to_blocks(name, g):
    if name in COL_SHARDED:
        return g.reshape(g.shape[0], N_DEV, g.shape[1] // N_DEV).transpose(1, 0, 2)
    return g.reshape(N_DEV, g.shape[0] // N_DEV, g.shape[1])


def _pack_small(d, last_rows):
    vec = jnp.concatenate([d[n].reshape(-1) for n in SMALL]).reshape(SMALL_ROWS - SUBLANES, 128)
    return jnp.concatenate([vec, last_rows], axis=0)


def kernel(x, norm1_g, w_in, gate_b, conv_w, conv_b, conv_ln_g, conv_ln_b, w_conv_out, w_attn_out, w_o, norm2_g, w_ffn_gate, w_ffn_up, w_ffn_down, norm_f_g, loss_target, m_norm1_g, m_w_in, m_gate_b, m_conv_w, m_conv_b, m_conv_ln_g, m_conv_ln_b, m_w_conv_out, m_w_attn_out, m_w_o, m_norm2_g, m_w_ffn_gate, m_w_ffn_up, m_w_ffn_down, m_norm_f_g, v_norm1_g, v_w_in, v_gate_b, v_conv_w, v_conv_b, v_conv_ln_g, v_conv_ln_b, v_w_conv_out, v_w_attn_out, v_w_o, v_norm2_g, v_w_ffn_gate, v_w_ffn_up, v_w_ffn_down, v_norm_f_g):
    wts = dict(norm1_g=norm1_g, w_in=w_in, gate_b=gate_b, conv_w=conv_w, conv_b=conv_b, conv_ln_g=conv_ln_g,
               conv_ln_b=conv_ln_b, w_conv_out=w_conv_out, w_attn_out=w_attn_out, w_o=w_o, norm2_g=norm2_g,
               w_ffn_gate=w_ffn_gate, w_ffn_up=w_ffn_up, w_ffn_down=w_ffn_down, norm_f_g=norm_f_g)
    mom1 = dict(norm1_g=m_norm1_g, w_in=m_w_in, gate_b=m_gate_b, conv_w=m_conv_w, conv_b=m_conv_b, conv_ln_g=m_conv_ln_g,
                conv_ln_b=m_conv_ln_b, w_conv_out=m_w_conv_out, w_attn_out=m_w_attn_out, w_o=m_w_o, norm2_g=m_norm2_g,
                w_ffn_gate=m_w_ffn_gate, w_ffn_up=m_w_ffn_up, w_ffn_down=m_w_ffn_down, norm_f_g=m_norm_f_g)
    mom2 = dict(norm1_g=v_norm1_g, w_in=v_w_in, gate_b=v_gate_b, conv_w=v_conv_w, conv_b=v_conv_b, conv_ln_g=v_conv_ln_g,
                conv_ln_b=v_conv_ln_b, w_conv_out=v_w_conv_out, w_attn_out=v_w_attn_out, w_o=v_w_o, norm2_g=v_norm2_g,
                w_ffn_gate=v_w_ffn_gate, w_ffn_up=v_w_ffn_up, w_ffn_down=v_w_ffn_down, norm_f_g=v_norm_f_g)

    T = x.shape[0] * x.shape[1]
    x2 = x.reshape(T, D_MODEL)
    t2 = loss_target.reshape(T, D_MODEL)

    me = 4 * lax.axis_index("x") + 2 * lax.axis_index("y") + lax.axis_index("c")
    shards = {n: _shard2d(n, wts[n]) for n in BIG}
    sent = {n: shards[n] if n == "conv_w" else shards[n].astype(BF16) for n in BIG}
    small = {n: wts[n].reshape(1, -1) for n in SMALL}

    stage = {}

    def in_proj():
        w_in_blocks, conv_blocks = _all_gather([sent["w_in"], sent["conv_w"]])
        near = (1,) + OTHER_CHIPS
        stage["merge"] = _send_start("gather", near, "gather_start_merge", [sent[n] for n in LATE_MERGE], dep=w_in_blocks)
        stage["ffn"] = _send_start("gather", near, "gather_start_ffn", [sent[n] for n in LATE_FFN],
                                   dep=stage["merge"]["token"])
        w_in_t = _gathered_to_full("w_in", w_in_blocks)
        h, proj = _in_proj(x2, small["norm1_g"], w_in_t, stage["ffn"]["token"])
        return h, proj, {"w_in": w_in_t, "conv_w": _gathered_to_full("conv_w", conv_blocks)}

    def filled(names, srcs, lands):
        return {n: _gathered_to_full(n, lax.dynamic_update_slice(land, src[None], (me, 0, 0)))
                for n, src, land in zip(names, srcs, lands)}

    def pass_on(group, after):
        stage[group + "_srcs"], lands = _send_wait(stage[group], after, "gather_wait_" + group)
        stage[group + "_forward"] = _send_start("forward", OTHER_CHIPS, "forward_start_" + group, lands=lands)
        return stage[group + "_forward"]["token"]

    def arrived(group, names, after):
        _, lands = _send_wait(stage[group + "_forward"], after, "forward_wait_" + group)
        return filled(names, stage[group + "_srcs"], lands)

    def late_weights(which, after):
        if which == "after_attention":
            return {"token": pass_on("merge", after)}
        if which is LATE_MERGE:
            return {**arrived("merge", LATE_MERGE, after), "token": pass_on("ffn", after)}
        return arrived("ffn", LATE_FFN, after)

    scatters = []
    core = lax.axis_index("c").astype(jnp.int32).reshape(1)

    def emit(names, gw):
        blocks = [_full_to_blocks(n, gw[n]) for n in names]
        if "w_in" in names:
            sums = [_add_pair(g, r, core, "chip_sum_" + n) for n, g, r in zip(names, blocks, _exchange_sibling(blocks))]
            started = _send_start("chip_scatter", OTHER_CHIPS, "scatter_start_" + names[0], sums)
        else:
            started = _send_start("scatter", ALL_PEERS, "scatter_start_" + names[0], blocks)
        scatters.append((names, started))
        return started["token"]

    loss_part, grad_x, gw, gsmall = _local_step(x2, t2, in_proj, small, late_weights, emit)

    grads, deltas, new_m, new_v = {}, {}, {}, {}
    after = grad_x
    for names, started in scatters:
        srcs, lands = _send_wait(started, after, "scatter_wait_" + names[0])
        mine = (me >> 1 if started["mode"] == "chip_scatter" else me).astype(jnp.int32).reshape(1)
        for n, src, land in zip(names, srcs, lands):
            g, d, mo, vo = _sum_adam(land, src, mine, shards[n], _shard2d(n, mom1[n]), _shard2d(n, mom2[n]), "adam_" + n)
            for dst, val in ((grads, g), (deltas, d), (new_m, mo), (new_v, vo)):
                dst[n] = _from_shard2d(n, val, wts[n].shape)
            after = g

    zeros, ones = jnp.zeros((SUBLANES, 128), F32), jnp.ones((SUBLANES, 128), F32)
    row_counts = [wts[n].size // 128 for n in SMALL]
    kinds, loss_rows = _small_allreduce_adam(
        _pack_small(gsmall, jnp.broadcast_to(loss_part, (SUBLANES, 128))), _pack_small(wts, zeros),
        _pack_small(mom1, zeros), _pack_small(mom2, ones), row_counts, after)
    for dst, vals in zip((grads, deltas, new_m, new_v), kinds):
        dst.update({n: val.reshape(wts[n].shape) for n, val in zip(SMALL, vals)})
    loss = loss_rows[0, 0]
    return (loss, grad_x.reshape(x.shape), *[grads[n] for n in WEIGHTS], *[deltas[n] for n in WEIGHTS],
            *[new_m[n] for n in WEIGHTS], *[new_v[n] for n in WEIGHTS])
```

```python
import math

import numpy as np
import jax
import jax.numpy as jnp
from jax import lax
from jax.experimental import pallas as pl
from jax.experimental.pallas import tpu as pltpu

F32 = jnp.float32
BF16 = jnp.bfloat16
SDS = jax.ShapeDtypeStruct
MESH = pl.DeviceIdType.MESH

D_MODEL = 1024
SEQ = 2048
HEAD_DIM = 64
GROUPS = ((128, 1), (512, 4), (2048, 16))
HEADS_PER_GROUP = 8
N_HEADS = 24
ATTN_WIDTH = N_HEADS * HEAD_DIM
ATTN_OUT = HEADS_PER_GROUP * HEAD_DIM
CONV_K = 31
CONV_PAD = 32
D_FF = 2816
IN_WIDTH = 3 * ATTN_WIDTH + 2 * D_MODEL + 2 * D_MODEL
RMS_EPS = 1e-6
LN_EPS = 1e-5
Q_BLOCK = 128
LANES = 128
NEG = -1e30
N_DEV = 8

ADAM_LR = 0.001
ADAM_B1 = 0.9
ADAM_B2 = 0.999
ADAM_EPS = 1e-08
ADAM_WD = 0.01
ADAM_STEP = 10


def _alibi_slope_list(n):
    def pow2(m):
        start = 2.0 ** (-8.0 / m)
        return [start ** (i + 1) for i in range(m)]
    if math.log2(n).is_integer():
        return pow2(n)
    c = 2 ** math.floor(math.log2(n))
    return pow2(c) + _alibi_slope_list(2 * c)[0::2][: n - c]


def _slopes_times_dilation():
    s = np.asarray(sorted(_alibi_slope_list(N_HEADS), reverse=True), dtype=np.float32).reshape(3, HEADS_PER_GROUP)
    r = np.asarray([g[1] for g in GROUPS], dtype=np.float32)[:, None]
    return (s * r).reshape(N_HEADS)


def _sigmoid(x):
    return 0.5 * jnp.tanh(0.5 * x) + 0.5


def _dot(a, b):
    return jnp.dot(a, b, preferred_element_type=F32)


def _dot_nt(a, b):
    return lax.dot_general(a, b, (((1,), (1,)), ((), ())), preferred_element_type=F32)


def _dot_tn(a, b):
    return lax.dot_general(a, b, (((0,), (0,)), ((), ())), preferred_element_type=F32)


def _rowsum(x):
    return jnp.sum(x, axis=0, keepdims=True)


ANY_SPEC = pl.BlockSpec(memory_space=pl.ANY)


def _params(*sem):
    return pltpu.CompilerParams(dimension_semantics=sem)


def _anchored(body, n_in, dep):
    if dep is None:
        return body, [], []

    def wrapped(*refs):
        return body(*refs[:n_in], *refs[n_in + 1:])

    return wrapped, [pl.BlockSpec(memory_space=pl.ANY)], [dep]


IN_TM = 256
IN_CHUNK = 512


def _in_proj(x, g1, w_in_t, dep=None):
    T = x.shape[0]
    tm = IN_TM

    def body(x_ref, g_ref, w_hbm, h_ref, proj_ref, w_vmem, sem):
        @pl.when(pl.program_id(0) == 0)
        def _():
            cp = pltpu.make_async_copy(w_hbm, w_vmem, sem)
            cp.start()
            cp.wait()

        xv = x_ref[...]
        r = lax.rsqrt(jnp.mean(xv * xv, axis=-1, keepdims=True) + RMS_EPS)
        h = (xv * r * g_ref[...]).astype(BF16)
        h_ref[...] = h
        for lo in range(0, IN_WIDTH, IN_CHUNK):
            proj_ref[:, lo:lo + IN_CHUNK] = _dot_nt(h, w_vmem[lo:lo + IN_CHUNK, :])

    row = lambda n: pl.BlockSpec((tm, n), lambda i: (i, 0))
    body, dep_spec, dep_arg = _anchored(body, 3, dep)
    return pl.pallas_call(
        body, grid=(T // tm,),
        in_specs=[row(D_MODEL), pl.BlockSpec((1, D_MODEL), lambda i: (0, 0)), pl.BlockSpec(memory_space=pl.ANY)] + dep_spec,
        out_specs=[row(D_MODEL), row(IN_WIDTH)],
        out_shape=[SDS((T, D_MODEL), BF16), SDS((T, IN_WIDTH), F32)],
        scratch_shapes=[pltpu.VMEM((IN_WIDTH, D_MODEL), BF16), pltpu.SemaphoreType.DMA],
        compiler_params=_params("arbitrary"), name="in_proj")(x, g1, w_in_t, *dep_arg)


def _mm_tn(a, b, out_dtype, name, tn, tt=1024):
    T, K = a.shape
    N = b.shape[1]
    nt = T // tt

    def body(a_ref, b_ref, o_ref, acc):
        t = pl.program_id(1)

        @pl.when(t == 0)
        def _():
            acc[...] = jnp.zeros_like(acc)

        acc[...] += _dot_tn(a_ref[...], b_ref[...])

        @pl.when(t == nt - 1)
        def _():
            o_ref[...] = acc[...].astype(o_ref.dtype)

    return pl.pallas_call(
        body, grid=(N // tn, nt),
        in_specs=[pl.BlockSpec((tt, K), lambda j, t: (t, 0)),
                  pl.BlockSpec((tt, tn), lambda j, t: (t, j))],
        out_specs=pl.BlockSpec((K, tn), lambda j, t: (0, j)),
        out_shape=SDS((K, N), out_dtype),
        scratch_shapes=[pltpu.VMEM((K, tn), F32)],
        compiler_params=_params("parallel", "arbitrary"), name=name)(a, b)


def _gather_classes(src_ref, dst, r, row0=0):
    L = SEQ // r
    for c in range(r):
        dst[row0 + c * L:row0 + (c + 1) * L, :] = src_ref[0, pl.ds(c, L, stride=r), :].astype(dst.dtype)


def _scatter_classes(src, dst, r, row0=0):
    L = SEQ // r
    for c in range(r):
        dst[pl.ds(c, L, stride=r), :] = src[row0 + c * L:row0 + (c + 1) * L, :].astype(dst.dtype)


def _attn_masks(slope_r):
    qi = lax.broadcasted_iota(jnp.int32, (Q_BLOCK, Q_BLOCK), 0)
    kj = lax.broadcasted_iota(jnp.int32, (Q_BLOCK, Q_BLOCK), 1)
    rel = (qi - kj).astype(F32)
    bias_cur = jnp.where(qi >= kj, -slope_r * rel, NEG)
    bias_prev = jnp.where(qi <= kj, -slope_r * (rel + float(Q_BLOCK)), NEG)
    return bias_cur, bias_prev


def _store_biases(bias, sl_ref, g, hp):
    for hh in range(2):
        cur, prev = _attn_masks(sl_ref[g * HEADS_PER_GROUP + 2 * hp + hh])
        rows = slice(hh * Q_BLOCK, (hh + 1) * Q_BLOCK)
        bias[0, rows, 0:Q_BLOCK] = prev
        bias[1, rows, 0:Q_BLOCK] = jnp.full((Q_BLOCK, Q_BLOCK), NEG, F32)
        bias[0, rows, Q_BLOCK:] = cur
        bias[1, rows, Q_BLOCK:] = cur


def _transpose_pairs(src, dst):
    dst[0, :, 0:Q_BLOCK] = jnp.zeros((LANES, Q_BLOCK), dst.dtype)
    nblk = SEQ // Q_BLOCK
    for b in range(nblk):
        t = src[(b + 1) * Q_BLOCK:(b + 2) * Q_BLOCK, :].T
        dst[b, :, Q_BLOCK:] = t
        if b + 1 < nblk:
            dst[b + 1, :, 0:Q_BLOCK] = t


def _stack_heads(t, low):
    z = jnp.zeros_like(t)
    return jnp.concatenate([jnp.where(low, t, z), jnp.where(low, z, t)], axis=0)


def _unstack_heads(t2, low):
    return jnp.where(low, t2[:Q_BLOCK], t2[Q_BLOCK:])


def _unit_offsets(u, nb):
    off = pl.multiple_of(u * Q_BLOCK, Q_BLOCK)
    n = u & (nb - 1)
    c = u >> int(math.log2(nb))
    return off, n == 0, c, n


ATTN_UNROLL = 16


def _attn_fwd(qkv, slopes_r, batch, dep=None):
    nblk = SEQ // Q_BLOCK

    def body(sl_ref, *refs):
        qkv_refs = refs[:9]
        att_ref, lse_ref = refs[9:11]
        qd, kd, vd, kt, opos, lpos, bias = refs[11:]
        hp = pl.program_id(1)
        low = lax.broadcasted_iota(jnp.int32, (Q_BLOCK, LANES), 1) < HEAD_DIM

        for g in range(3):
            r = GROUPS[g][1]
            nb = SEQ // r // Q_BLOCK
            _gather_classes(qkv_refs[3 * g], qd, r)
            kd[0:Q_BLOCK, :] = jnp.zeros((Q_BLOCK, LANES), BF16)
            vd[0:Q_BLOCK, :] = jnp.zeros((Q_BLOCK, LANES), BF16)
            _gather_classes(qkv_refs[3 * g + 1], kd, r, Q_BLOCK)
            _gather_classes(qkv_refs[3 * g + 2], vd, r, Q_BLOCK)
            _transpose_pairs(kd, kt)
            _store_biases(bias, sl_ref, g, hp)

            def unit(u, carry, g=g, r=r, nb=nb):
                off, first, c, n = _unit_offsets(u, nb)
                q2 = _stack_heads(qd[pl.ds(off, Q_BLOCK), :], low)
                s = _dot(q2, kt[u]) * 0.125 + bias[first.astype(jnp.int32)]
                m = jnp.max(s, axis=-1, keepdims=True)
                p = jnp.exp(s - m)
                l = jnp.sum(p, axis=-1, keepdims=True)
                o2 = _dot(p.astype(BF16), vd[pl.ds(off, 2 * Q_BLOCK), :]) * (1.0 / l)
                lse2 = m + jnp.log(l)
                rows = pl.ds(c + n * (Q_BLOCK * r), Q_BLOCK, stride=r)
                opos[g, rows, :] = _unstack_heads(o2, low)
                lpos[g, rows, :] = jnp.where(low, lse2[:Q_BLOCK], lse2[Q_BLOCK:])
                return carry

            lax.fori_loop(0, nblk, unit, 0, unroll=ATTN_UNROLL)

        def merge(i, carry):
            rows = pl.ds(pl.multiple_of(i * 256, 256), 256)
            l0, l1, l2 = lpos[0, rows, :], lpos[1, rows, :], lpos[2, rows, :]
            m = jnp.maximum(jnp.maximum(l0, l1), l2)
            e0, e1, e2 = jnp.exp(l0 - m), jnp.exp(l1 - m), jnp.exp(l2 - m)
            den = e0 + e1 + e2
            att = (e0 * opos[0, rows, :] + e1 * opos[1, rows, :] + e2 * opos[2, rows, :]) / den
            att_ref[0, rows, :] = att.astype(att_ref.dtype)
            lse_ref[0, rows, :] = m + jnp.log(den)
            return carry

        lax.fori_loop(0, SEQ // 256, merge, 0)

    def col(sec, g):
        return pl.BlockSpec((1, SEQ, LANES), lambda b, hp: (b, 0, sec * 12 + g * 4 + hp))

    out = pl.BlockSpec((1, SEQ, LANES), lambda b, hp: (b, 0, hp))
    body, dep_spec, dep_arg = _anchored(body, 10, dep)
    return pl.pallas_call(
        body, grid=(batch, 4),
        in_specs=[pl.BlockSpec(memory_space=pltpu.SMEM)] + [col(sec, g) for g in range(3) for sec in range(3)] + dep_spec,
        out_specs=[out, out],
        out_shape=[SDS((batch, SEQ, ATTN_OUT), BF16), SDS((batch, SEQ, ATTN_OUT), F32)],
        scratch_shapes=[pltpu.VMEM((SEQ, LANES), BF16), pltpu.VMEM((Q_BLOCK + SEQ, LANES), BF16),
                        pltpu.VMEM((Q_BLOCK + SEQ, LANES), BF16), pltpu.VMEM((nblk, LANES, 2 * Q_BLOCK), BF16),
                        pltpu.VMEM((3, SEQ, LANES), F32), pltpu.VMEM((3, SEQ, LANES), F32),
                        pltpu.VMEM((2, 2 * Q_BLOCK, 2 * Q_BLOCK), F32)],
        compiler_params=_params("parallel", "parallel"), name="attn_fwd")(slopes_r, *([qkv] * 9), *dep_arg)


def _attn_bwd(qkv, datt, lse, dsum, slopes_r, batch):
    nblk = SEQ // Q_BLOCK

    def body(sl_ref, q_ref, k_ref, v_ref, do_ref, l_ref, d_ref, dq_ref, dk_ref, dv_ref,
             qd, kd, vd, dod, kt, vt, ld, dd, dq_acc, dk_acc, dv_acc, dk_part, dv_part, stage, bias):
        gid, hp = pl.program_id(1), pl.program_id(2)
        low = lax.broadcasted_iota(jnp.int32, (Q_BLOCK, LANES), 1) < HEAD_DIM

        def section(g):
            r = GROUPS[g][1]
            nb = SEQ // r // Q_BLOCK
            _gather_classes(q_ref, qd, r)
            kd[0:Q_BLOCK, :] = jnp.zeros((Q_BLOCK, LANES), BF16)
            vd[0:Q_BLOCK, :] = jnp.zeros((Q_BLOCK, LANES), BF16)
            _gather_classes(k_ref, kd, r, Q_BLOCK)
            _gather_classes(v_ref, vd, r, Q_BLOCK)
            _gather_classes(do_ref, dod, r)
            _gather_classes(l_ref, ld, r)
            _gather_classes(d_ref, dd, r)
            _transpose_pairs(kd, kt)
            _transpose_pairs(vd, vt)
            _store_biases(bias, sl_ref, g, hp)

            def unit(u, carry):
                off, first, _, _ = _unit_offsets(u, nb)
                pair = pl.ds(off, 2 * Q_BLOCK)
                q2 = _stack_heads(qd[pl.ds(off, Q_BLOCK), :], low)
                do2 = _stack_heads(dod[pl.ds(off, Q_BLOCK), :], low)
                lse_t = ld[pl.ds(off, Q_BLOCK), :]
                dsum_t = dd[pl.ds(off, Q_BLOCK), :]
                lse2 = jnp.concatenate([lse_t[:, 0:1], lse_t[:, HEAD_DIM:HEAD_DIM + 1]], axis=0)
                dsum2 = jnp.concatenate([dsum_t[:, 0:1], dsum_t[:, HEAD_DIM:HEAD_DIM + 1]], axis=0)
                s = _dot(q2, kt[u]) * 0.125 + bias[first.astype(jnp.int32)]
                p = jnp.exp(s - lse2)
                ds = (p * (_dot(do2, vt[u]) - dsum2)).astype(BF16)
                dq_acc[pl.ds(off, Q_BLOCK), :] = _unstack_heads(_dot(ds, kd[pair, :]), low) * 0.125
                dk_part[u] = _dot_tn(ds, q2) * 0.125
                dv_part[u] = _dot_tn(p.astype(BF16), do2)
                return carry

            lax.fori_loop(0, nblk, unit, 0, unroll=ATTN_UNROLL)
            for part, acc in ((dk_part, dk_acc), (dv_part, dv_acc)):
                for b in range(nblk):
                    t = part[b, Q_BLOCK:, :]
                    if b + 1 < nblk:
                        t = t + part[b + 1, 0:Q_BLOCK, :]
                    acc[b * Q_BLOCK:(b + 1) * Q_BLOCK, :] = t
            for acc, out_ref in ((dq_acc, dq_ref), (dk_acc, dk_ref), (dv_acc, dv_ref)):
                _scatter_classes(acc, stage, r)
                out_ref[0] = stage[...].astype(out_ref.dtype)

        for g in range(3):
            pl.when(gid == g)(lambda g=g: section(g))

    def col(sec):
        return pl.BlockSpec((1, SEQ, LANES), lambda b, g, hp: (b, 0, sec * 12 + g * 4 + hp))

    pos = pl.BlockSpec((1, SEQ, LANES), lambda b, g, hp: (b, 0, hp))
    dout = pl.BlockSpec((1, SEQ, LANES), lambda b, g, hp: (b, 0, g * 4 + hp))
    out = SDS((batch, SEQ, ATTN_WIDTH), BF16)
    seq_bf = pltpu.VMEM((SEQ, LANES), BF16)
    seq_f = pltpu.VMEM((SEQ, LANES), F32)
    pad_bf = pltpu.VMEM((Q_BLOCK + SEQ, LANES), BF16)
    part = pltpu.VMEM((nblk, 2 * Q_BLOCK, LANES), F32)
    blk_t = pltpu.VMEM((nblk, LANES, 2 * Q_BLOCK), BF16)
    return pl.pallas_call(
        body, grid=(batch, 3, 4),
        in_specs=[pl.BlockSpec(memory_space=pltpu.SMEM), col(0), col(1), col(2), pos, pos, pos],
        out_specs=[dout, dout, dout],
        out_shape=[out, out, out],
        scratch_shapes=[seq_bf, pad_bf, pad_bf, seq_bf, blk_t, blk_t, seq_f, seq_f, seq_f, seq_f, seq_f, part, part, seq_f,
                        pltpu.VMEM((2, 2 * Q_BLOCK, 2 * Q_BLOCK), F32)],
        compiler_params=_params("parallel", "parallel", "parallel"), name="attn_bwd")(
            slopes_r, qkv, qkv, qkv, datt, lse, dsum)


CONV_TC = 128
U_BLOCK0 = 3 * ATTN_WIDTH // CONV_TC
CONV_ROWS = 128
SUBLANES = 8


def _fill_shifted(sh):
    n = SEQ + CONV_PAD - SUBLANES
    for s in range(1, SUBLANES):
        sh[s, 0:n, :] = sh[0, s:s + n, :]


def _tap(sh, base, offset):
    s = offset % SUBLANES
    return sh[s, pl.ds(pl.multiple_of(base + (offset - s), SUBLANES), CONV_ROWS), :]


def _conv_fwd(u, conv_w, conv_b, batch, dep=None):
    nct = D_MODEL // CONV_TC

    def body(ua_ref, ub_ref, w_ref, b_ref, o_ref, sh):
        sh[0, 0:CONV_PAD, :] = jnp.zeros((CONV_PAD, CONV_TC), F32)
        sh[0, CONV_PAD:, :] = ua_ref[0] * _sigmoid(ub_ref[0])
        _fill_shifted(sh)

        def chunk(c, carry):
            base = pl.multiple_of(c * CONV_ROWS, CONV_ROWS)
            acc = jnp.broadcast_to(b_ref[...], (CONV_ROWS, CONV_TC))
            for t in range(CONV_K):
                acc = acc + _tap(sh, base, t + CONV_PAD - (CONV_K - 1)) * w_ref[t:t + 1, :]
            o_ref[0, pl.ds(base, CONV_ROWS), :] = acc
            return carry

        lax.fori_loop(0, SEQ // CONV_ROWS, chunk, 0)

    body, dep_spec, dep_arg = _anchored(body, 4, dep)
    return pl.pallas_call(
        body, grid=(nct, batch),
        in_specs=[pl.BlockSpec((1, SEQ, CONV_TC), lambda j, b: (b, 0, U_BLOCK0 + j)),
                  pl.BlockSpec((1, SEQ, CONV_TC), lambda j, b: (b, 0, U_BLOCK0 + nct + j)),
                  pl.BlockSpec((CONV_PAD, CONV_TC), lambda j, b: (0, j)),
                  pl.BlockSpec((1, CONV_TC), lambda j, b: (0, j))] + dep_spec,
        out_specs=pl.BlockSpec((1, SEQ, CONV_TC), lambda j, b: (b, 0, j)),
        out_shape=SDS((batch, SEQ, D_MODEL), F32),
        scratch_shapes=[pltpu.VMEM((SUBLANES, SEQ + CONV_PAD, CONV_TC), F32)],
        compiler_params=_params("parallel", "parallel"), name="conv_fwd")(u, u, conv_w, conv_b, *dep_arg)


def _conv_bwd(u, dc1, conv_w, batch, dep=None):
    nct = D_MODEL // CONV_TC
    nchunk = SEQ // CONV_ROWS

    def body(ua_ref, ub_ref, d_ref, w_ref, dua_ref, dub_ref, gw_ref, gb_ref, shc, shd, gacc):
        b = pl.program_id(1)
        shc[0, 0:CONV_PAD, :] = jnp.zeros((CONV_PAD, CONV_TC), F32)
        shc[0, CONV_PAD:, :] = ua_ref[0] * _sigmoid(ub_ref[0])
        _fill_shifted(shc)
        shd[0, 0:SEQ, :] = d_ref[0]
        shd[0, SEQ:, :] = jnp.zeros((CONV_PAD, CONV_TC), F32)
        _fill_shifted(shd)

        @pl.when(b == 0)
        def _():
            gacc[...] = jnp.zeros_like(gacc)
            gb_ref[...] = jnp.zeros_like(gb_ref)

        gb_ref[...] += _rowsum(d_ref[0])

        def chunk(c, carry):
            base = pl.multiple_of(c * CONV_ROWS, CONV_ROWS)
            dcur = shd[0, pl.ds(base, CONV_ROWS), :]
            acc = jnp.zeros((CONV_ROWS, CONV_TC), F32)
            for t in range(CONV_K):
                acc = acc + _tap(shd, base, CONV_K - 1 - t) * w_ref[t:t + 1, :]
                prod = _tap(shc, base, t + CONV_PAD - (CONV_K - 1)) * dcur
                gacc[t] += jnp.sum(prod.reshape(CONV_ROWS // 8, 8, CONV_TC), axis=0)
            ua = ua_ref[0, pl.ds(base, CONV_ROWS), :]
            sg = _sigmoid(ub_ref[0, pl.ds(base, CONV_ROWS), :])
            dua_ref[0, pl.ds(base, CONV_ROWS), :] = (acc * sg).astype(dua_ref.dtype)
            dub_ref[0, pl.ds(base, CONV_ROWS), :] = (acc * ua * sg * (1.0 - sg)).astype(dub_ref.dtype)
            return carry

        lax.fori_loop(0, nchunk, chunk, 0)

        @pl.when(b == batch - 1)
        def _():
            for t in range(CONV_K):
                gw_ref[t:t + 1, :] = jnp.sum(gacc[t], axis=0, keepdims=True)
            gw_ref[CONV_K:CONV_PAD, :] = jnp.zeros((CONV_PAD - CONV_K, CONV_TC), F32)

    du = SDS((batch, SEQ, D_MODEL), BF16)
    body, dep_spec, dep_arg = _anchored(body, 4, dep)
    return pl.pallas_call(
        body, grid=(nct, batch),
        in_specs=[pl.BlockSpec((1, SEQ, CONV_TC), lambda j, b: (b, 0, U_BLOCK0 + j)),
                  pl.BlockSpec((1, SEQ, CONV_TC), lambda j, b: (b, 0, U_BLOCK0 + nct + j)),
                  pl.BlockSpec((1, SEQ, CONV_TC), lambda j, b: (b, 0, j)),
                  pl.BlockSpec((CONV_PAD, CONV_TC), lambda j, b: (0, j))] + dep_spec,
        out_specs=[pl.BlockSpec((1, SEQ, CONV_TC), lambda j, b: (b, 0, j)),
                   pl.BlockSpec((1, SEQ, CONV_TC), lambda j, b: (b, 0, j)),
                   pl.BlockSpec((CONV_PAD, CONV_TC), lambda j, b: (0, j)),
                   pl.BlockSpec((1, CONV_TC), lambda j, b: (0, j))],
        out_shape=[du, du, SDS((CONV_PAD, D_MODEL), F32), SDS((1, D_MODEL), F32)],
        scratch_shapes=[pltpu.VMEM((SUBLANES, SEQ + CONV_PAD, CONV_TC), F32),
                        pltpu.VMEM((SUBLANES, SEQ + CONV_PAD, CONV_TC), F32),
                        pltpu.VMEM((CONV_K, 8, CONV_TC), F32)],
        compiler_params=_params("parallel", "arbitrary"), name="conv_bwd")(u, u, dc1, conv_w, *dep_arg)


MID_TM = 256


def _layernorm_stats(c1):
    mu = jnp.mean(c1, axis=-1, keepdims=True)
    cen = c1 - mu
    rs = lax.rsqrt(jnp.mean(cen * cen, axis=-1, keepdims=True) + LN_EPS)
    return cen * rs, rs


GATE_PARTS = 4
GATE_PART = 2 * D_MODEL // GATE_PARTS
GATE_PART0 = (IN_WIDTH - 2 * D_MODEL) // GATE_PART


def _gate_specs(tm):
    return [pl.BlockSpec((tm, GATE_PART), lambda i, k=k: (i, GATE_PART0 + k)) for k in range(GATE_PARTS)]


def _mid_fwd(att, c1, proj, x, w_a, w_c, w_o, gate_b, ln_g, ln_b, g2, dep=None):
    T = x.shape[0]
    tm = MID_TM

    def body(att_ref, c1_ref, lg0, lg1, lg2, lg3, x_ref, wa_ref, wc_ref, wo_ref, gb_ref, lng_ref, lnb_ref, g2_ref,
             c3_ref, ya_ref, yc_ref, mix_ref, x1_ref, h2_ref):
        logits = jnp.concatenate([lg0[...], lg1[...], lg2[...], lg3[...]], axis=1)
        ya = _dot(att_ref[...], wa_ref[...])
        xh, _ = _layernorm_stats(c1_ref[...])
        c2 = xh * lng_ref[...] + lnb_ref[...]
        c3 = (c2 * _sigmoid(c2)).astype(BF16)
        c3_ref[...] = c3
        yc = _dot(c3, wc_ref[...])
        gates = _sigmoid(logits + gb_ref[...])
        mix = (gates[:, :D_MODEL] * ya + gates[:, D_MODEL:] * yc).astype(BF16)
        ya_ref[...] = ya.astype(BF16)
        yc_ref[...] = yc.astype(BF16)
        mix_ref[...] = mix
        x1 = x_ref[...] + _dot(mix, wo_ref[...])
        x1_ref[...] = x1
        r = lax.rsqrt(jnp.mean(x1 * x1, axis=-1, keepdims=True) + RMS_EPS)
        h2_ref[...] = (x1 * r * g2_ref[...]).astype(BF16)

    row = lambda n: pl.BlockSpec((tm, n), lambda i: (i, 0))
    full = lambda a, b: pl.BlockSpec((a, b), lambda i: (0, 0))
    body, dep_spec, dep_arg = _anchored(body, 10 + GATE_PARTS, dep)
    return pl.pallas_call(
        body, grid=(T // tm,),
        in_specs=[row(ATTN_OUT), row(D_MODEL)] + _gate_specs(tm) + [row(D_MODEL),
                  full(ATTN_OUT, D_MODEL), full(D_MODEL, D_MODEL), full(D_MODEL, D_MODEL),
                  full(1, 2 * D_MODEL), full(1, D_MODEL), full(1, D_MODEL), full(1, D_MODEL)] + dep_spec,
        out_specs=[row(D_MODEL), row(D_MODEL), row(D_MODEL), row(D_MODEL), row(D_MODEL), row(D_MODEL)],
        out_shape=[SDS((T, D_MODEL), BF16), SDS((T, D_MODEL), BF16), SDS((T, D_MODEL), BF16), SDS((T, D_MODEL), BF16),
                   SDS((T, D_MODEL), F32), SDS((T, D_MODEL), BF16)],
        compiler_params=_params("parallel"), name="mid_fwd")(att, c1, *([proj] * GATE_PARTS), x, w_a, w_c, w_o,
                                                             gate_b, ln_g, ln_b, g2, *dep_arg)


def _mid_bwd(dx1b, ya, yc, proj, att, c1, w_a, w_c, w_o, gate_b, ln_g, ln_b, head_ones, dep=None):
    T = dx1b.shape[0]
    tm = MID_TM

    def body(dx_ref, ya_ref, yc_ref, lg0, lg1, lg2, lg3, att_ref, c1_ref, wa_ref, wc_ref, wo_ref, gb_ref, lng_ref,
             lnb_ref, e_ref, dlg_ref, dya_ref, dyc_ref, datt_ref, dsum_ref, dc1_ref, ggb_ref, glg_ref, glb_ref):
        logits = jnp.concatenate([lg0[...], lg1[...], lg2[...], lg3[...]], axis=1)
        @pl.when(pl.program_id(0) == 0)
        def _():
            ggb_ref[...] = jnp.zeros_like(ggb_ref)
            glg_ref[...] = jnp.zeros_like(glg_ref)
            glb_ref[...] = jnp.zeros_like(glb_ref)

        dmix = _dot_nt(dx_ref[...], wo_ref[...])
        gates = _sigmoid(logits + gb_ref[...])
        ga, gc = gates[:, :D_MODEL], gates[:, D_MODEL:]
        dla = dmix * ya_ref[...].astype(F32) * ga * (1.0 - ga)
        dlc = dmix * yc_ref[...].astype(F32) * gc * (1.0 - gc)
        dlg_ref[:, :D_MODEL] = dla.astype(BF16)
        dlg_ref[:, D_MODEL:] = dlc.astype(BF16)
        ggb_ref[:, :D_MODEL] += _rowsum(dla)
        ggb_ref[:, D_MODEL:] += _rowsum(dlc)
        dya = (dmix * ga).astype(BF16)
        dyc = (dmix * gc).astype(BF16)
        dya_ref[...] = dya
        dyc_ref[...] = dyc
        datt = _dot_nt(dya, wa_ref[...])
        datt_ref[...] = datt
        dsum_ref[...] = jnp.dot(datt * att_ref[...].astype(F32), e_ref[...], preferred_element_type=F32,
                                precision=lax.Precision.HIGHEST)
        dc3 = _dot_nt(dyc, wc_ref[...])
        xh, rs = _layernorm_stats(c1_ref[...])
        c2 = xh * lng_ref[...] + lnb_ref[...]
        sg = _sigmoid(c2)
        dc2 = dc3 * (sg * (1.0 + c2 * (1.0 - sg)))
        glg_ref[...] += _rowsum(dc2 * xh)
        glb_ref[...] += _rowsum(dc2)
        dxh = dc2 * lng_ref[...]
        dc1_ref[...] = rs * (dxh - jnp.mean(dxh, axis=-1, keepdims=True) - xh * jnp.mean(dxh * xh, axis=-1, keepdims=True))

    row = lambda n: pl.BlockSpec((tm, n), lambda i: (i, 0))
    full = lambda a, b: pl.BlockSpec((a, b), lambda i: (0, 0))
    body, dep_spec, dep_arg = _anchored(body, 12 + GATE_PARTS, dep)
    return pl.pallas_call(
        body, grid=(T // tm,),
        in_specs=[row(D_MODEL), row(D_MODEL), row(D_MODEL)] + _gate_specs(tm) + [row(ATTN_OUT), row(D_MODEL),
                  full(ATTN_OUT, D_MODEL), full(D_MODEL, D_MODEL), full(D_MODEL, D_MODEL),
                  full(1, 2 * D_MODEL), full(1, D_MODEL), full(1, D_MODEL), full(ATTN_OUT, ATTN_OUT)] + dep_spec,
        out_specs=[row(2 * D_MODEL), row(D_MODEL), row(D_MODEL), row(ATTN_OUT), row(ATTN_OUT), row(D_MODEL),
                   full(1, 2 * D_MODEL), full(1, D_MODEL), full(1, D_MODEL)],
        out_shape=[SDS((T, 2 * D_MODEL), BF16), SDS((T, D_MODEL), BF16), SDS((T, D_MODEL), BF16), SDS((T, ATTN_OUT), F32),
                   SDS((T, ATTN_OUT), F32), SDS((T, D_MODEL), F32),
                   SDS((1, 2 * D_MODEL), F32), SDS((1, D_MODEL), F32), SDS((1, D_MODEL), F32)],
        compiler_params=_params("arbitrary"), name="mid_bwd")(dx1b, ya, yc, *([proj] * GATE_PARTS), att, c1, w_a, w_c, w_o,
                                                               gate_b, ln_g, ln_b, head_ones, *dep_arg)


FFN_TM = 256
FFN_CHUNK = 512
FFN_SUB = tuple((lo, min(lo + FFN_CHUNK, D_FF)) for lo in range(0, D_FF, FFN_CHUNK))


def _rms_bwd(dy_times_g, xh, r):
    return r * (dy_times_g - xh * jnp.mean(dy_times_g * xh, axis=-1, keepdims=True))


def _load_resident(pairs, sems):
    @pl.when(pl.program_id(0) == 0)
    def _():
        copies = [pltpu.make_async_copy(src, dst, sems.at[k]) for k, (src, dst) in enumerate(pairs)]
        for cp in copies:
            cp.start()
        for cp in copies:
            cp.wait()


def _ffn_fwd(h2, x1, target, gf, w_g_t, w_u_t, w_d):
    T = h2.shape[0]
    tm = FFN_TM

    def body(h_ref, x1_ref, t_ref, gf_ref, wg_hbm, wu_hbm, wd_hbm,
             a_ref, b_ref, f_ref, dx2_ref, dx2b_ref, loss_ref, gnf_ref, wg, wu, wd, sems):
        _load_resident(((wg_hbm, wg), (wu_hbm, wu), (wd_hbm, wd)), sems)

        @pl.when(pl.program_id(0) == 0)
        def _():
            loss_ref[...] = jnp.zeros_like(loss_ref)
            gnf_ref[...] = jnp.zeros_like(gnf_ref)

        h = h_ref[...]
        x2 = x1_ref[...]
        for lo, hi in FFN_SUB:
            a = _dot_nt(h, wg[lo:hi, :])
            b = _dot_nt(h, wu[lo:hi, :])
            f = (a * _sigmoid(a) * b).astype(BF16)
            a_ref[:, lo:hi] = a.astype(BF16)
            b_ref[:, lo:hi] = b.astype(BF16)
            f_ref[:, lo:hi] = f
            x2 = x2 + _dot(f, wd[lo:hi, :])

        r = lax.rsqrt(jnp.mean(x2 * x2, axis=-1, keepdims=True) + RMS_EPS)
        xh = x2 * r
        err = xh * gf_ref[...] - t_ref[...]
        loss_ref[...] += (0.5 / D_MODEL) * jnp.sum(err * err)
        dy = err * (1.0 / D_MODEL)
        gnf_ref[...] += _rowsum(dy * xh)
        dx2 = _rms_bwd(dy * gf_ref[...], xh, r)
        dx2_ref[...] = dx2
        dx2b_ref[...] = dx2.astype(BF16)

    row = lambda n: pl.BlockSpec((tm, n), lambda i: (i, 0))
    const = lambda n: pl.BlockSpec((1, n), lambda i: (0, 0))
    wshape = pltpu.VMEM((D_FF, D_MODEL), BF16)
    return pl.pallas_call(
        body, grid=(T // tm,),
        in_specs=[row(D_MODEL), row(D_MODEL), row(D_MODEL), const(D_MODEL), ANY_SPEC, ANY_SPEC, ANY_SPEC],
        out_specs=[row(D_FF), row(D_FF), row(D_FF), row(D_MODEL), row(D_MODEL), const(128), const(D_MODEL)],
        out_shape=[SDS((T, D_FF), BF16), SDS((T, D_FF), BF16), SDS((T, D_FF), BF16), SDS((T, D_MODEL), F32),
                   SDS((T, D_MODEL), BF16), SDS((1, 128), F32), SDS((1, D_MODEL), F32)],
        scratch_shapes=[wshape, wshape, wshape, pltpu.SemaphoreType.DMA((3,))],
        compiler_params=_params("arbitrary"), name="ffn_fwd")(h2, x1, target, gf, w_g_t, w_u_t, w_d)


def _ffn_bwd(dx2b, dx2, a, b, x1, g2, w_g_t, w_u_t, w_d):
    T = dx2.shape[0]
    tm = FFN_TM

    def body(dxb_ref, dx2_ref, a_ref, b_ref, x1_ref, g2_ref, wg_hbm, wu_hbm, wd_hbm,
             da_ref, db_ref, dx1_ref, dx1b_ref, gn2_ref, wg, wu, wd, sems):
        _load_resident(((wg_hbm, wg), (wu_hbm, wu), (wd_hbm, wd)), sems)

        @pl.when(pl.program_id(0) == 0)
        def _():
            gn2_ref[...] = jnp.zeros_like(gn2_ref)

        dxb = dxb_ref[...]
        dh2 = jnp.zeros((tm, D_MODEL), F32)
        for lo, hi in FFN_SUB:
            df = _dot_nt(dxb, wd[lo:hi, :])
            av = a_ref[:, lo:hi].astype(F32)
            bv = b_ref[:, lo:hi].astype(F32)
            sg = _sigmoid(av)
            db = (df * av * sg).astype(BF16)
            da = (df * bv * (sg * (1.0 + av * (1.0 - sg)))).astype(BF16)
            da_ref[:, lo:hi] = da
            db_ref[:, lo:hi] = db
            dh2 = dh2 + _dot(da, wg[lo:hi, :]) + _dot(db, wu[lo:hi, :])

        x1 = x1_ref[...]
        r = lax.rsqrt(jnp.mean(x1 * x1, axis=-1, keepdims=True) + RMS_EPS)
        xh = x1 * r
        gn2_ref[...] += _rowsum(dh2 * xh)
        dx1 = dx2_ref[...] + _rms_bwd(dh2 * g2_ref[...], xh, r)
        dx1_ref[...] = dx1
        dx1b_ref[...] = dx1.astype(BF16)

    row = lambda n: pl.BlockSpec((tm, n), lambda i: (i, 0))
    const = lambda n: pl.BlockSpec((1, n), lambda i: (0, 0))
    wshape = pltpu.VMEM((D_FF, D_MODEL), BF16)
    return pl.pallas_call(
        body, grid=(T // tm,),
        in_specs=[row(D_MODEL), row(D_MODEL), row(D_FF), row(D_FF), row(D_MODEL), const(D_MODEL),
                  ANY_SPEC, ANY_SPEC, ANY_SPEC],
        out_specs=[row(D_FF), row(D_FF), row(D_MODEL), row(D_MODEL), const(D_MODEL)],
        out_shape=[SDS((T, D_FF), BF16), SDS((T, D_FF), BF16), SDS((T, D_MODEL), F32), SDS((T, D_MODEL), BF16),
                   SDS((1, D_MODEL), F32)],
        scratch_shapes=[wshape, wshape, wshape, pltpu.SemaphoreType.DMA((3,))],
        compiler_params=_params("arbitrary"), name="ffn_bwd")(dx2b, dx2, a, b, x1, g2, w_g_t, w_u_t, w_d)


def _in_bwd(pieces, w_in_t, x, dx1, g1, dep=None):
    T = x.shape[0]
    tm = IN_TM
    npc = len(pieces)
    assert sum(p.shape[1] for p in pieces) == IN_WIDTH

    def body(*refs):
        p_refs = refs[:npc]
        w_hbm, x_ref, dx1_ref, g_ref, dx_ref, gn1_ref, w_vmem, sem = refs[npc:]

        @pl.when(pl.program_id(0) == 0)
        def _():
            cp = pltpu.make_async_copy(w_hbm, w_vmem, sem)
            cp.start()
            cp.wait()
            gn1_ref[...] = jnp.zeros_like(gn1_ref)

        dh = jnp.zeros((tm, D_MODEL), F32)
        col = 0
        for p_ref in p_refs:
            for j in range(p_ref.shape[1] // IN_CHUNK):
                dh = dh + _dot(p_ref[:, j * IN_CHUNK:(j + 1) * IN_CHUNK], w_vmem[col:col + IN_CHUNK, :])
                col += IN_CHUNK
        xv = x_ref[...]
        r = lax.rsqrt(jnp.mean(xv * xv, axis=-1, keepdims=True) + RMS_EPS)
        xh = xv * r
        gn1_ref[...] += _rowsum(dh * xh)
        dx_ref[...] = dx1_ref[...] + _rms_bwd(dh * g_ref[...], xh, r)

    row = lambda n: pl.BlockSpec((tm, n), lambda i: (i, 0))
    body, dep_spec, dep_arg = _anchored(body, npc + 4, dep)
    return pl.pallas_call(
        body, grid=(T // tm,),
        in_specs=[row(p.shape[1]) for p in pieces]
        + [pl.BlockSpec(memory_space=pl.ANY), row(D_MODEL), row(D_MODEL), pl.BlockSpec((1, D_MODEL), lambda i: (0, 0))]
        + dep_spec,
        out_specs=[row(D_MODEL), pl.BlockSpec((1, D_MODEL), lambda i: (0, 0))],
        out_shape=[SDS((T, D_MODEL), F32), SDS((1, D_MODEL), F32)],
        scratch_shapes=[pltpu.VMEM((IN_WIDTH, D_MODEL), BF16), pltpu.SemaphoreType.DMA],
        compiler_params=_params("arbitrary"), name="in_bwd")(*pieces, w_in_t, x, dx1, g1, *dep_arg)


def _local_step(x, target, in_proj, small, late_weights=None, emit=None):
    T = x.shape[0]
    batch = T // SEQ
    slopes_r = jnp.asarray(_slopes_times_dilation())
    emit = emit or (lambda names, grads: None)

    h, proj, w = in_proj()
    proj3 = proj.reshape(batch, SEQ, IN_WIDTH)

    att, lse = _attn_fwd(proj3, slopes_r, batch, w.get("token"))
    att = att.reshape(T, ATTN_OUT)
    if late_weights is not None:
        w = {**w, **late_weights("after_attention", att)}

    c1 = _conv_fwd(proj3, w["conv_w"], small["conv_b"], batch, w.get("token")).reshape(T, D_MODEL)
    if late_weights is not None:
        w = {**w, **late_weights(LATE_MERGE, (att, c1))}

    c3, ya, yc, mix, x1, h2 = _mid_fwd(
        att, c1, proj, x, w["w_attn_out"], w["w_conv_out"], w["w_o"],
        small["gate_b"], small["conv_ln_g"], small["conv_ln_b"], small["norm2_g"], w.get("token"))
    if late_weights is not None:
        w = {**w, **late_weights(LATE_FFN, h2)}

    a, b, f, dx2, dx2b, loss, g_normf = _ffn_fwd(h2, x1, target, small["norm_f_g"],
                                                   w["w_ffn_gate"], w["w_ffn_up"], w["w_ffn_down"])

    da, db, dx1, dx1b, g_norm2 = _ffn_bwd(dx2b, dx2, a, b, x1, small["norm2_g"],
                                           w["w_ffn_gate"], w["w_ffn_up"], w["w_ffn_down"])
    gw = {}
    gw["w_ffn_down"] = _mm_tn(f, dx2b, BF16, "gw_ffn_down", tn=1024)
    gw["w_ffn_gate"] = _mm_tn(da, h2, BF16, "gw_ffn_gate", tn=1024)
    gw["w_ffn_up"] = _mm_tn(db, h2, BF16, "gw_ffn_up", tn=1024)
    token = emit(("w_ffn_gate", "w_ffn_up", "w_ffn_down"), gw)

    head_ones = jnp.asarray(np.kron(np.eye(HEADS_PER_GROUP, dtype=np.float32), np.ones((HEAD_DIM, HEAD_DIM), np.float32)))
    dlogits, dya, dyc, datt, dsum, dc1, g_gate_b, g_ln_g, g_ln_b = _mid_bwd(
        dx1b, ya, yc, proj, att, c1, w["w_attn_out"], w["w_conv_out"], w["w_o"],
        small["gate_b"], small["conv_ln_g"], small["conv_ln_b"], head_ones, token)
    gw["w_o"] = _mm_tn(mix, dx1b, BF16, "gw_o", tn=1024)
    gw["w_attn_out"] = _mm_tn(att, dya, BF16, "gw_attn_out", tn=1024)
    gw["w_conv_out"] = _mm_tn(c3, dyc, BF16, "gw_conv_out", tn=1024)
    token = emit(("w_conv_out", "w_attn_out", "w_o"), gw)

    dua, dub, g_conv_w, g_conv_b = _conv_bwd(proj3, dc1.reshape(batch, SEQ, D_MODEL), w["conv_w"], batch, token)

    dq, dk, dv = _attn_bwd(proj3, datt.reshape(batch, SEQ, ATTN_OUT), lse, dsum.reshape(batch, SEQ, ATTN_OUT),
                           slopes_r, batch)
    pieces = [dq.reshape(T, ATTN_WIDTH), dk.reshape(T, ATTN_WIDTH), dv.reshape(T, ATTN_WIDTH),
              dua.reshape(T, D_MODEL), dub.reshape(T, D_MODEL), dlogits]

    names = ("q", "k", "v", "ua", "ub", "gate")
    gw["w_in"] = jnp.concatenate([_mm_tn(p, h, BF16, "gw_in_" + nm, tn=1024) for nm, p in zip(names, pieces)], axis=0)
    gw["conv_w"] = g_conv_w
    token = emit(("w_in", "conv_w"), gw)
    grad_x, g_norm1 = _in_bwd(pieces, w["w_in"], x, dx1, small["norm1_g"], token)

    gsmall = {"norm1_g": g_norm1, "gate_b": g_gate_b, "conv_b": g_conv_b, "conv_ln_g": g_ln_g, "conv_ln_b": g_ln_b,
              "norm2_g": g_norm2, "norm_f_g": g_normf}
    return loss, grad_x, gw, gsmall


ANY = pl.BlockSpec(memory_space=pl.ANY)


def _all_gather(arrs):
    n = len(arrs)

    def body(*refs):
        ins, outs = refs[:n], refs[n:2 * n]
        send_sems, recv_sems, local_sems = refs[2 * n:]
        x, y, c = lax.axis_index("x"), lax.axis_index("y"), lax.axis_index("c")
        me, sibling = (x, y, c), (x, y, 1 - c)
        chips = [(1 - x, y), (x, 1 - y), (1 - x, 1 - y)]

        def copy(a, k, block, to, src=None):
            px, py, pc = block
            dst = outs[a].at[4 * px + 2 * py + pc]
            return pltpu.make_async_remote_copy(
                src_ref=dst if src is None else src, dst_ref=dst,
                send_sem=send_sems.at[a, k], recv_sem=recv_sems.at[a, k], device_id=to, device_id_type=MESH)

        mine = [pltpu.make_async_copy(ins[a], outs[a].at[4 * x + 2 * y + c], local_sems.at[a]) for a in range(n)]
        for cp in mine:
            cp.start()
        first = []
        for j, chip in enumerate(chips):
            first += [copy(a, 1 + j, me, (*chip, c), src=ins[a]) for a in range(n)]
        first += [copy(a, 0, me, sibling, src=ins[a]) for a in range(n)]
        for cp in first:
            cp.start()
        passed = []
        for j, chip in enumerate(chips):
            for a in range(n):
                copy(a, 1 + j, (*chip, c), me).wait_recv()
                cp = copy(a, 4 + j, (*chip, c), sibling)
                cp.start()
                passed.append(cp)
        for a in range(n):
            copy(a, 0, sibling, me).wait_recv()
        for j, chip in enumerate(chips):
            for a in range(n):
                copy(a, 4 + j, (*chip, 1 - c), me).wait_recv()
        for cp in first + passed:
            cp.wait_send()
        for cp in mine:
            cp.wait()

    return pl.pallas_call(
        body, in_specs=[ANY] * n, out_specs=[ANY] * n,
        out_shape=[SDS((N_DEV,) + a.shape, a.dtype) for a in arrs],
        scratch_shapes=[pltpu.SemaphoreType.DMA((n, 7)), pltpu.SemaphoreType.DMA((n, 7)), pltpu.SemaphoreType.DMA((n,))],
        name="all_gather_weights")(*arrs)


HBM =pl.BlockSpec(memory_space=pltpu.HBM)
SEM = pl.BlockSpec(memory_space=pltpu.SEMAPHORE)
ALL_PEERS = tuple(range(1, N_DEV))
OTHER_CHIPS = (2, 4, 6)
SPLIT_EFFECT = pltpu.CompilerParams(has_side_effects=pltpu.SideEffectType.DATAFLOW_SIDE_EFFECTING)


def _exchange_copies(mode, ks, srcs, lands, send_sems, recv_sems):
    x, y, c = lax.axis_index("x"), lax.axis_index("y"), lax.axis_index("c")
    me = 4 * x + 2 * y + c
    send, recv = [], []
    for a in range(len(lands)):
        for i, k in enumerate(ks):
            peer = (x ^ ((k >> 2) & 1), y ^ ((k >> 1) & 1), c ^ (k & 1))
            pidx = 4 * peer[0] + 2 * peer[1] + peer[2]
            if mode == "gather":
                src, to, out_slot, in_slot = srcs[a], peer, me, pidx
            elif mode == "scatter":
                src, to, out_slot, in_slot = srcs[a].at[pidx], peer, me, pidx
            elif mode == "chip_scatter":
                src, to, out_slot, in_slot = srcs[a].at[pidx >> 1], peer, me >> 1, pidx >> 1
            else:
                src, to, out_slot, in_slot = lands[a].at[pidx], (x, y, 1 - c), pidx, pidx ^ 1
            s = a * len(ks) + i
            send.append(pltpu.make_async_remote_copy(
                src_ref=src, dst_ref=lands[a].at[out_slot], send_sem=send_sems.at[s], recv_sem=recv_sems.at[s],
                device_id=to, device_id_type=MESH))
            recv.append(pltpu.make_async_remote_copy(
                src_ref=src, dst_ref=lands[a].at[in_slot], send_sem=send_sems.at[s], recv_sem=recv_sems.at[s],
                device_id=to, device_id_type=MESH))
    return send, recv


def _send_start(mode, ks, name, srcs=(), lands=None, dep=None):
    srcs = list(srcs)
    if lands is None:
        slots = 4 if mode == "chip_scatter" else N_DEV
        lands = [lax.empty((slots,) + (s.shape if mode == "gather" else s.shape[1:]), s.dtype) for s in srcs]
    ns, nl = len(srcs), len(lands)
    nsem = nl * len(ks)

    def body(*refs):
        send, _ = _exchange_copies(mode, ks, refs[:ns], refs[ns:ns + nl], refs[ns + nl], refs[ns + nl + 1])
        for cp in send:
            cp.start()
        token = refs[-1]
        token[...] = jnp.zeros_like(token)

    both = srcs + list(lands)
    body, dep_spec, dep_arg = _anchored(body, ns + nl, dep)
    res = pl.pallas_call(
        body, name=name,
        out_shape=(pltpu.SemaphoreType.DMA((nsem,)), pltpu.SemaphoreType.DMA((nsem,)),
                   *[pltpu.HBM(a.shape, a.dtype) for a in both], SDS((8, 128), F32)),
        in_specs=[HBM] * (ns + nl) + dep_spec,
        out_specs=(SEM, SEM, *([HBM] * (ns + nl)), pl.BlockSpec(memory_space=pltpu.VMEM)),
        input_output_aliases={i: 2 + i for i in range(ns + nl)}, compiler_params=SPLIT_EFFECT,
    )(*[pltpu.with_memory_space_constraint(a, pltpu.HBM) for a in both], *dep_arg)
    return dict(mode=mode, ks=ks, send_sems=res[0], recv_sems=res[1], srcs=res[2:2 + ns], lands=res[2 + ns:2 + ns + nl],
                token=res[-1])


def _send_wait(started, after, name):
    ns, nl = len(started["srcs"]), len(started["lands"])

    def body(*refs):
        send, recv = _exchange_copies(started["mode"], started["ks"], refs[:ns], refs[ns:ns + nl],
                                      refs[ns + nl], refs[ns + nl + 1])
        for cp in send:
            cp.wait_send()
        for cp in recv:
            cp.wait_recv()

    both = list(started["srcs"]) + list(started["lands"])
    after = after if isinstance(after, (tuple, list)) else (after,)
    res = pl.pallas_call(
        body, name=name,
        out_shape=tuple(pltpu.HBM(a.shape, a.dtype) for a in both),
        in_specs=[HBM] * (ns + nl) + [SEM, SEM] + [ANY] * len(after), out_specs=tuple([HBM] * (ns + nl)),
        input_output_aliases={i: i for i in range(ns + nl)}, compiler_params=SPLIT_EFFECT,
    )(*both, started["send_sems"], started["recv_sems"], *after)
    return res[:ns], res[ns:]


def _exchange_sibling(gs):
    n = len(gs)

    def body(*refs):
        ins, outs = refs[:n], refs[n:2 * n]
        send_sems, recv_sems = refs[2 * n:]
        x, y, c = lax.axis_index("x"), lax.axis_index("y"), lax.axis_index("c")
        copies = []
        for a in range(n):
            for j in range(4):
                copies.append(pltpu.make_async_remote_copy(
                    src_ref=ins[a].at[2 * j + (1 - c)], dst_ref=outs[a].at[j],
                    send_sem=send_sems.at[a, j], recv_sem=recv_sems.at[a, j],
                    device_id=(x, y, 1 - c), device_id_type=MESH))
        for cp in copies:
            cp.start()
        for cp in copies:
            cp.wait_recv()
        for cp in copies:
            cp.wait_send()

    return pl.pallas_call(
        body, in_specs=[ANY] * n, out_specs=[ANY] * n,
        out_shape=[SDS((4,) + g.shape[1:], g.dtype) for g in gs],
        scratch_shapes=[pltpu.SemaphoreType.DMA((n, 4)), pltpu.SemaphoreType.DMA((n, 4))],
        name="reduce_scatter_sibling")(*gs)


def _add_pair(g, r1, core, name):
    _, rows, cols = g.shape
    tr = _row_tile(rows, cols, 3 * g.dtype.itemsize)

    def body(c_ref, g_ref, r_ref, o_ref):
        o_ref[...] = (g_ref[...].astype(F32) + r_ref[...].astype(F32)).astype(o_ref.dtype)

    return pl.pallas_call(
        body,
        grid_spec=pltpu.PrefetchScalarGridSpec(
            num_scalar_prefetch=1, grid=(4, rows // tr),
            in_specs=[pl.BlockSpec((1, tr, cols), lambda j, i, c_ref: (2 * j + c_ref[0], i, 0)),
                      pl.BlockSpec((1, tr, cols), lambda j, i, c_ref: (j, i, 0))],
            out_specs=pl.BlockSpec((1, tr, cols), lambda j, i, c_ref: (j, i, 0))),
        out_shape=SDS((4, rows, cols), g.dtype),
        compiler_params=_params("parallel", "parallel"), name=name)(core, g, r1)


def _row_tile(rows, cols, itemsize_total):
    budget = (4 << 20) // max(1, cols * itemsize_total)
    if rows <= budget:
        return rows
    t = rows
    while t > budget and t % 2 == 0 and (t // 2) % 16 == 0:
        t //= 2
    return t


def _adam_math(g, w, m, v):
    m_new = ADAM_B1 * m + (1.0 - ADAM_B1) * g
    v_new = ADAM_B2 * v + (1.0 - ADAM_B2) * (g * g)
    m_hat = m_new / (1.0 - ADAM_B1 ** ADAM_STEP)
    v_hat = v_new / (1.0 - ADAM_B2 ** ADAM_STEP)
    delta = -ADAM_LR * (m_hat / (jnp.sqrt(v_hat) + ADAM_EPS) + ADAM_WD * w)
    return delta, m_new, v_new


def _sum_adam(parts, own, mine, w, m, v, name):
    rows, cols = w.shape
    nparts = parts.shape[0]
    tr = _row_tile(rows, cols, (nparts + 1) * parts.dtype.itemsize + 7 * 4)

    def body(mine_ref, p_ref, own_ref, w_ref, m_ref, v_ref, g_ref, d_ref, mo_ref, vo_ref):
        g = None
        for s in range(nparts):
            part = jnp.where(mine_ref[0] == s, own_ref[0], p_ref[s]).astype(F32)
            g = part if g is None else g + part
        delta, m_new, v_new = _adam_math(g, w_ref[...], m_ref[...], v_ref[...])
        g_ref[...] = g
        d_ref[...] = delta
        mo_ref[...] = m_new
        vo_ref[...] = v_new

    blk = pl.BlockSpec((tr, cols), lambda i, mine_ref: (i, 0))
    out = SDS((rows, cols), F32)
    return pl.pallas_call(
        body,
        grid_spec=pltpu.PrefetchScalarGridSpec(
            num_scalar_prefetch=1, grid=(rows // tr,),
            in_specs=[pl.BlockSpec((nparts, tr, cols), lambda i, mine_ref: (0, i, 0)),
                      pl.BlockSpec((1, tr, cols), lambda i, mine_ref: (mine_ref[0], i, 0)), blk, blk, blk],
            out_specs=[blk, blk, blk, blk]),
        out_shape=[out, out, out, out],
        compiler_params=_params("parallel"), name=name)(mine, parts, own, w, m, v)


SMALL_ROWS = 72


def _small_allreduce_adam(gpart, w, m, v, row_counts, dep=None):
    def reduce_body(g_ref, go_ref, gath, send_sems, recv_sems):
        x, y, c = lax.axis_index("x"), lax.axis_index("y"), lax.axis_index("c")
        me = 4 * x + 2 * y + c
        gath[me] = g_ref[...]
        copies = []
        for k in range(1, N_DEV):
            fx, fy, fc = (k >> 2) & 1, (k >> 1) & 1, k & 1
            peer = (x ^ fx, y ^ fy, c ^ fc)
            copies.append(pltpu.make_async_remote_copy(
                src_ref=gath.at[me], dst_ref=gath.at[me], send_sem=send_sems.at[k - 1], recv_sem=recv_sems.at[k - 1],
                device_id=peer, device_id_type=MESH))
        for cp in copies:
            cp.start()
        for cp in copies:
            cp.wait_recv()
        for cp in copies:
            cp.wait_send()
        g = gath[0]
        for d in range(1, N_DEV):
            g = g + gath[d]
        go_ref[...] = g

    def adam_body(g_ref, w_ref, m_ref, v_ref, *out_refs):
        g = g_ref[...]
        delta, m_new, v_new = _adam_math(g, w_ref[...], m_ref[...], v_ref[...])
        outs = iter(out_refs)
        for val in (g, delta, m_new, v_new):
            lo = 0
            for r in row_counts:
                next(outs)[...] = val[lo:lo + r]
                lo += r
        next(outs)[...] = g[SMALL_ROWS - SUBLANES:]

    vm = pl.BlockSpec(memory_space=pltpu.VMEM)
    reduce_body, dep_spec, dep_arg = _anchored(reduce_body, 1, dep)
    total = pl.pallas_call(
        reduce_body, in_specs=[vm] + dep_spec, out_specs=vm, out_shape=SDS((SMALL_ROWS, 128), F32),
        scratch_shapes=[pltpu.VMEM((N_DEV, SMALL_ROWS, 128), F32), pltpu.SemaphoreType.DMA((N_DEV - 1,)),
                        pltpu.SemaphoreType.DMA((N_DEV - 1,))],
        name="small_allreduce")(gpart, *dep_arg)
    out_shape = [SDS((r, 128), F32) for _ in range(4) for r in row_counts] + [SDS((SUBLANES, 128), F32)]
    res = pl.pallas_call(adam_body, in_specs=[vm] * 4, out_specs=[vm] * len(out_shape), out_shape=out_shape,
                         name="small_adam")(total, w, m, v)
    k = len(row_counts)
    return [res[i * k:(i + 1) * k] for i in range(4)], res[-1]


BIG = ("w_in", "conv_w", "w_conv_out", "w_attn_out", "w_o", "w_ffn_gate", "w_ffn_up", "w_ffn_down")
LATE_MERGE = ("w_conv_out", "w_attn_out", "w_o")
LATE_FFN = ("w_ffn_gate", "w_ffn_up", "w_ffn_down")
TRANSPOSED = ("w_in", "w_ffn_gate", "w_ffn_up")
COL_SHARDED = ("conv_w", "w_attn_out")
SMALL = ("norm1_g", "gate_b", "conv_b", "conv_ln_g", "conv_ln_b", "norm2_g", "norm_f_g")
WEIGHTS = ("norm1_g", "w_in", "gate_b", "conv_w", "conv_b", "conv_ln_g", "conv_ln_b", "w_conv_out", "w_attn_out", "w_o",
           "norm2_g", "w_ffn_gate", "w_ffn_up", "w_ffn_down", "norm_f_g")


def _shard2d(name, a):
    a = a.reshape(a.shape[-2], a.shape[-1])
    if name in TRANSPOSED:
        a = a.T
    if name == "conv_w":
        a = jnp.pad(a, ((0, CONV_PAD - CONV_K), (0, 0)))
    return a


def _from_shard2d(name, val, shape):
    if name in TRANSPOSED:
        val = val.T
    if name == "conv_w":
        val = val[:CONV_K]
    return val.reshape(shape)


def _gathered_to_full(name, g):
    if name in COL_SHARDED:
        return g.transpose(1, 0, 2).reshape(g.shape[1], N_DEV * g.shape[2])
    return g.reshape(N_DEV * g.shape[1], g.shape[2])


def _full_to_blocks(name, g):
    if name in COL_SHARDED:
        return g.reshape(g.shape[0], N_DEV, g.shape[1] // N_DEV).transpose(1, 0, 2)
    return g.reshape(N_DEV, g.shape[0] // N_DEV, g.shape[1])


def _pack_small(d, last_rows):
    vec = jnp.concatenate([d[n].reshape(-1) for n in SMALL]).reshape(SMALL_ROWS - SUBLANES, 128)
    return jnp.concatenate([vec, last_rows], axis=0)


def kernel(x, norm1_g, w_in, gate_b, conv_w, conv_b, conv_ln_g, conv_ln_b, w_conv_out, w_attn_out, w_o, norm2_g, w_ffn_gate, w_ffn_up, w_ffn_down, norm_f_g, loss_target, m_norm1_g, m_w_in, m_gate_b, m_conv_w, m_conv_b, m_conv_ln_g, m_conv_ln_b, m_w_conv_out, m_w_attn_out, m_w_o, m_norm2_g, m_w_ffn_gate, m_w_ffn_up, m_w_ffn_down, m_norm_f_g, v_norm1_g, v_w_in, v_gate_b, v_conv_w, v_conv_b, v_conv_ln_g, v_conv_ln_b, v_w_conv_out, v_w_attn_out, v_w_o, v_norm2_g, v_w_ffn_gate, v_w_ffn_up, v_w_ffn_down, v_norm_f_g):
    wts = dict(norm1_g=norm1_g, w_in=w_in, gate_b=gate_b, conv_w=conv_w, conv_b=conv_b, conv_ln_g=conv_ln_g,
               conv_ln_b=conv_ln_b, w_conv_out=w_conv_out, w_attn_out=w_attn_out, w_o=w_o, norm2_g=norm2_g,
               w_ffn_gate=w_ffn_gate, w_ffn_up=w_ffn_up, w_ffn_down=w_ffn_down, norm_f_g=norm_f_g)
    mom1 = dict(norm1_g=m_norm1_g, w_in=m_w_in, gate_b=m_gate_b, conv_w=m_conv_w, conv_b=m_conv_b, conv_ln_g=m_conv_ln_g,
                conv_ln_b=m_conv_ln_b, w_conv_out=m_w_conv_out, w_attn_out=m_w_attn_out, w_o=m_w_o, norm2_g=m_norm2_g,
                w_ffn_gate=m_w_ffn_gate, w_ffn_up=m_w_ffn_up, w_ffn_down=m_w_ffn_down, norm_f_g=m_norm_f_g)
    mom2 = dict(norm1_g=v_norm1_g, w_in=v_w_in, gate_b=v_gate_b, conv_w=v_conv_w, conv_b=v_conv_b, conv_ln_g=v_conv_ln_g,
                conv_ln_b=v_conv_ln_b, w_conv_out=v_w_conv_out, w_attn_out=v_w_attn_out, w_o=v_w_o, norm2_g=v_norm2_g,
                w_ffn_gate=v_w_ffn_gate, w_ffn_up=v_w_ffn_up, w_ffn_down=v_w_ffn_down, norm_f_g=v_norm_f_g)

    T = x.shape[0] * x.shape[1]
    x2 = x.reshape(T, D_MODEL)
    t2 = loss_target.reshape(T, D_MODEL)

    me = 4 * lax.axis_index("x") + 2 * lax.axis_index("y") + lax.axis_index("c")
    shards = {n: _shard2d(n, wts[n]) for n in BIG}
    sent = {n: shards[n] if n == "conv_w" else shards[n].astype(BF16) for n in BIG}
    small = {n: wts[n].reshape(1, -1) for n in SMALL}

    stage = {}

    def in_proj():
        w_in_blocks, conv_blocks = _all_gather([sent["w_in"], sent["conv_w"]])
        near = (1,) + OTHER_CHIPS
        stage["merge"] = _send_start("gather", near, "gather_start_merge", [sent[n] for n in LATE_MERGE], dep=w_in_blocks)
        stage["ffn"] = _send_start("gather", near, "gather_start_ffn", [sent[n] for n in LATE_FFN],
                                   dep=stage["merge"]["token"])
        w_in_t = _gathered_to_full("w_in", w_in_blocks)
        h, proj = _in_proj(x2, small["norm1_g"], w_in_t, stage["ffn"]["token"])
        return h, proj, {"w_in": w_in_t, "conv_w": _gathered_to_full("conv_w", conv_blocks)}

    def filled(names, srcs, lands):
        return {n: _gathered_to_full(n, lax.dynamic_update_slice(land, src[None], (me, 0, 0)))
                for n, src, land in zip(names, srcs, lands)}

    def pass_on(group, after):
        stage[group + "_srcs"], lands = _send_wait(stage[group], after, "gather_wait_" + group)
        stage[group + "_forward"] = _send_start("forward", OTHER_CHIPS, "forward_start_" + group, lands=lands)
        return stage[group + "_forward"]["token"]

    def arrived(group, names, after):
        _, lands = _send_wait(stage[group + "_forward"], after, "forward_wait_" + group)
        return filled(names, stage[group + "_srcs"], lands)

    def late_weights(which, after):
        if which == "after_attention":
            return {"token": pass_on("merge", after)}
        if which is LATE_MERGE:
            return {**arrived("merge", LATE_MERGE, after), "token": pass_on("ffn", after)}
        return arrived("ffn", LATE_FFN, after)

    scatters = []
    core = lax.axis_index("c").astype(jnp.int32).reshape(1)

    def emit(names, gw):
        blocks = [_full_to_blocks(n, gw[n]) for n in names]
        if "w_in" in names:
            sums = [_add_pair(g, r, core, "chip_sum_" + n) for n, g, r in zip(names, blocks, _exchange_sibling(blocks))]
            started = _send_start("chip_scatter", OTHER_CHIPS, "scatter_start_" + names[0], sums)
        else:
            started = _send_start("scatter", ALL_PEERS, "scatter_start_" + names[0], blocks)
        scatters.append((names, started))
        return started["token"]

    loss_part, grad_x, gw, gsmall = _local_step(x2, t2, in_proj, small, late_weights, emit)

    grads, deltas, new_m, new_v = {}, {}, {}, {}
    after = grad_x
    for names, started in scatters:
        srcs, lands = _send_wait(started, after, "scatter_wait_" + names[0])
        mine = (me >> 1 if started["mode"] == "chip_scatter" else me).astype(jnp.int32).reshape(1)
        for n, src, land in zip(names, srcs, lands):
            g, d, mo, vo = _sum_adam(land, src, mine, shards[n], _shard2d(n, mom1[n]), _shard2d(n, mom2[n]), "adam_" + n)
            for dst, val in ((grads, g), (deltas, d), (new_m, mo), (new_v, vo)):
                dst[n] = _from_shard2d(n, val, wts[n].shape)
            after = g

    zeros, ones = jnp.zeros((SUBLANES, 128), F32), jnp.ones((SUBLANES, 128), F32)
    row_counts = [wts[n].size // 128 for n in SMALL]
    kinds, loss_rows = _small_allreduce_adam(
        _pack_small(gsmall, jnp.broadcast_to(loss_part, (SUBLANES, 128))), _pack_small(wts, zeros),
        _pack_small(mom1, zeros), _pack_small(mom2, ones), row_counts, after)
    for dst, vals in zip((grads, deltas, new_m, new_v), kinds):
        dst.update({n: val.reshape(wts[n].shape) for n, val in zip(SMALL, vals)})
    loss = loss_rows[0, 0]
    return (loss, grad_x.reshape(x.shape), *[grads[n] for n in WEIGHTS], *[deltas[n] for n in WEIGHTS],
            *[new_m[n] for n in WEIGHTS], *[new_v[n] for n in WEIGHTS])
```

```python
import math

import numpy as np
import jax
import jax.numpy as jnp
from jax import lax
from jax.experimental import pallas as pl
from jax.experimental.pallas import tpu as pltpu

F32 = jnp.float32
BF16 = jnp.bfloat16
SDS = jax.ShapeDtypeStruct
MESH = pl.DeviceIdType.MESH

D_MODEL = 1024
SEQ = 2048
HEAD_DIM = 64
GROUPS = ((128, 1), (512, 4), (2048, 16))
HEADS_PER_GROUP = 8
N_HEADS = 24
ATTN_WIDTH = N_HEADS * HEAD_DIM
ATTN_OUT = HEADS_PER_GROUP * HEAD_DIM
CONV_K = 31
CONV_PAD = 32
D_FF = 2816
IN_WIDTH = 3 * ATTN_WIDTH + 2 * D_MODEL + 2 * D_MODEL
RMS_EPS = 1e-6
LN_EPS = 1e-5
Q_BLOCK = 128
LANES = 128
NEG = -1e30
N_DEV = 8

ADAM_LR = 0.001
ADAM_B1 = 0.9
ADAM_B2 = 0.999
ADAM_EPS = 1e-08
ADAM_WD = 0.01
ADAM_STEP = 10


def _alibi_slope_list(n):
    def pow2(m):
        start = 2.0 ** (-8.0 / m)
        return [start ** (i + 1) for i in range(m)]
    if math.log2(n).is_integer():
        return pow2(n)
    c = 2 ** math.floor(math.log2(n))
    return pow2(c) + _alibi_slope_list(2 * c)[0::2][: n - c]


def _slopes_times_dilation():
    s = np.asarray(sorted(_alibi_slope_list(N_HEADS), reverse=True), dtype=np.float32).reshape(3, HEADS_PER_GROUP)
    r = np.asarray([g[1] for g in GROUPS], dtype=np.float32)[:, None]
    return (s * r).reshape(N_HEADS)


def _sigmoid(x):
    return 0.5 * jnp.tanh(0.5 * x) + 0.5


def _dot(a, b):
    return jnp.dot(a, b, preferred_element_type=F32)


def _dot_nt(a, b):
    return lax.dot_general(a, b, (((1,), (1,)), ((), ())), preferred_element_type=F32)


def _dot_tn(a, b):
    return lax.dot_general(a, b, (((0,), (0,)), ((), ())), preferred_element_type=F32)


def _rowsum(x):
    return jnp.sum(x, axis=0, keepdims=True)


ANY_SPEC = pl.BlockSpec(memory_space=pl.ANY)


def _params(*sem):
    return pltpu.CompilerParams(dimension_semantics=sem)


def _anchored(body, n_in, dep):
    if dep is None:
        return body, [], []

    def wrapped(*refs):
        return body(*refs[:n_in], *refs[n_in + 1:])

    return wrapped, [pl.BlockSpec(memory_space=pl.ANY)], [dep]


IN_TM = 256
IN_CHUNK = 512


def _in_proj(x, g1, w_in_t, dep=None):
    T = x.shape[0]
    tm = IN_TM

    def body(x_ref, g_ref, w_hbm, h_ref, proj_ref, w_vmem, sem):
        @pl.when(pl.program_id(0) == 0)
        def _():
            cp = pltpu.make_async_copy(w_hbm, w_vmem, sem)
            cp.start()
            cp.wait()

        xv = x_ref[...]
        r = lax.rsqrt(jnp.mean(xv * xv, axis=-1, keepdims=True) + RMS_EPS)
        h = (xv * r * g_ref[...]).astype(BF16)
        h_ref[...] = h
        for lo in range(0, IN_WIDTH, IN_CHUNK):
            proj_ref[:, lo:lo + IN_CHUNK] = _dot_nt(h, w_vmem[lo:lo + IN_CHUNK, :])

    row = lambda n: pl.BlockSpec((tm, n), lambda i: (i, 0))
    body, dep_spec, dep_arg = _anchored(body, 3, dep)
    return pl.pallas_call(
        body, grid=(T // tm,),
        in_specs=[row(D_MODEL), pl.BlockSpec((1, D_MODEL), lambda i: (0, 0)), pl.BlockSpec(memory_space=pl.ANY)] + dep_spec,
        out_specs=[row(D_MODEL), row(IN_WIDTH)],
        out_shape=[SDS((T, D_MODEL), BF16), SDS((T, IN_WIDTH), F32)],
        scratch_shapes=[pltpu.VMEM((IN_WIDTH, D_MODEL), BF16), pltpu.SemaphoreType.DMA],
        compiler_params=_params("arbitrary"), name="in_proj")(x, g1, w_in_t, *dep_arg)


def _mm_tn(a, b, out_dtype, name, tn, tt=1024):
    T, K = a.shape
    N = b.shape[1]
    nt = T // tt

    def body(a_ref, b_ref, o_ref, acc):
        t = pl.program_id(1)

        @pl.when(t == 0)
        def _():
            acc[...] = jnp.zeros_like(acc)

        acc[...] += _dot_tn(a_ref[...], b_ref[...])

        @pl.when(t == nt - 1)
        def _():
            o_ref[...] = acc[...].astype(o_ref.dtype)

    return pl.pallas_call(
        body, grid=(N // tn, nt),
        in_specs=[pl.BlockSpec((tt, K), lambda j, t: (t, 0)),
                  pl.BlockSpec((tt, tn), lambda j, t: (t, j))],
        out_specs=pl.BlockSpec((K, tn), lambda j, t: (0, j)),
        out_shape=SDS((K, N), out_dtype),
        scratch_shapes=[pltpu.VMEM((K, tn), F32)],
        compiler_params=_params("parallel", "arbitrary"), name=name)(a, b)


def _gather_classes(src_ref, dst, r, row0=0):
    L = SEQ // r
    for c in range(r):
        dst[row0 + c * L:row0 + (c + 1) * L, :] = src_ref[0, pl.ds(c, L, stride=r), :].astype(dst.dtype)


def _scatter_classes(src, dst, r, row0=0):
    L = SEQ // r
    for c in range(r):
        dst[pl.ds(c, L, stride=r), :] = src[row0 + c * L:row0 + (c + 1) * L, :].astype(dst.dtype)


def _attn_masks(slope_r):
    qi = lax.broadcasted_iota(jnp.int32, (Q_BLOCK, Q_BLOCK), 0)
    kj = lax.broadcasted_iota(jnp.int32, (Q_BLOCK, Q_BLOCK), 1)
    rel = (qi - kj).astype(F32)
    bias_cur = jnp.where(qi >= kj, -slope_r * rel, NEG)
    bias_prev = jnp.where(qi <= kj, -slope_r * (rel + float(Q_BLOCK)), NEG)
    return bias_cur, bias_prev


def _store_biases(bias, sl_ref, g, hp):
    for hh in range(2):
        cur, prev = _attn_masks(sl_ref[g * HEADS_PER_GROUP + 2 * hp + hh])
        rows = slice(hh * Q_BLOCK, (hh + 1) * Q_BLOCK)
        bias[0, rows, 0:Q_BLOCK] = prev
        bias[1, rows, 0:Q_BLOCK] = jnp.full((Q_BLOCK, Q_BLOCK), NEG, F32)
        bias[0, rows, Q_BLOCK:] = cur
        bias[1, rows, Q_BLOCK:] = cur


def _transpose_pairs(src, dst):
    dst[0, :, 0:Q_BLOCK] = jnp.zeros((LANES, Q_BLOCK), dst.dtype)
    nblk = SEQ // Q_BLOCK
    for b in range(nblk):
        t = src[(b + 1) * Q_BLOCK:(b + 2) * Q_BLOCK, :].T
        dst[b, :, Q_BLOCK:] = t
        if b + 1 < nblk:
            dst[b + 1, :, 0:Q_BLOCK] = t


def _stack_heads(t, low):
    z = jnp.zeros_like(t)
    return jnp.concatenate([jnp.where(low, t, z), jnp.where(low, z, t)], axis=0)


def _unstack_heads(t2, low):
    return jnp.where(low, t2[:Q_BLOCK], t2[Q_BLOCK:])


def _unit_offsets(u, nb):
    off = pl.multiple_of(u * Q_BLOCK, Q_BLOCK)
    n = u & (nb - 1)
    c = u >> int(math.log2(nb))
    return off, n == 0, c, n


ATTN_UNROLL = 16


def _attn_fwd(qkv, slopes_r, batch, dep=None):
    nblk = SEQ // Q_BLOCK

    def body(sl_ref, *refs):
        qkv_refs = refs[:9]
        att_ref, lse_ref = refs[9:11]
        qd, kd, vd, kt, opos, lpos, bias = refs[11:]
        hp = pl.program_id(1)
        low = lax.broadcasted_iota(jnp.int32, (Q_BLOCK, LANES), 1) < HEAD_DIM

        for g in range(3):
            r = GROUPS[g][1]
            nb = SEQ // r // Q_BLOCK
            _gather_classes(qkv_refs[3 * g], qd, r)
            kd[0:Q_BLOCK, :] = jnp.zeros((Q_BLOCK, LANES), BF16)
            vd[0:Q_BLOCK, :] = jnp.zeros((Q_BLOCK, LANES), BF16)
            _gather_classes(qkv_refs[3 * g + 1], kd, r, Q_BLOCK)
            _gather_classes(qkv_refs[3 * g + 2], vd, r, Q_BLOCK)
            _transpose_pairs(kd, kt)
            _store_biases(bias, sl_ref, g, hp)

            def unit(u, carry, g=g, r=r, nb=nb):
                off, first, c, n = _unit_offsets(u, nb)
                q2 = _stack_heads(qd[pl.ds(off, Q_BLOCK), :], low)
                s = _dot(q2, kt[u]) * 0.125 + bias[first.astype(jnp.int32)]
                m = jnp.max(s, axis=-1, keepdims=True)
                p = jnp.exp(s - m)
                l = jnp.sum(p, axis=-1, keepdims=True)
                o2 = _dot(p.astype(BF16), vd[pl.ds(off, 2 * Q_BLOCK), :]) * (1.0 / l)
                lse2 = m + jnp.log(l)
                rows = pl.ds(c + n * (Q_BLOCK * r), Q_BLOCK, stride=r)
                opos[g, rows, :] = _unstack_heads(o2, low)
                lpos[g, rows, :] = jnp.where(low, lse2[:Q_BLOCK], lse2[Q_BLOCK:])
                return carry

            lax.fori_loop(0, nblk, unit, 0, unroll=ATTN_UNROLL)

        def merge(i, carry):
            rows = pl.ds(pl.multiple_of(i * 256, 256), 256)
            l0, l1, l2 = lpos[0, rows, :], lpos[1, rows, :], lpos[2, rows, :]
            m = jnp.maximum(jnp.maximum(l0, l1), l2)
            e0, e1, e2 = jnp.exp(l0 - m), jnp.exp(l1 - m), jnp.exp(l2 - m)
            den = e0 + e1 + e2
            att = (e0 * opos[0, rows, :] + e1 * opos[1, rows, :] + e2 * opos[2, rows, :]) / den
            att_ref[0, rows, :] = att.astype(att_ref.dtype)
            lse_ref[0, rows, :] = m + jnp.log(den)
            return carry

        lax.fori_loop(0, SEQ // 256, merge, 0)

    def col(sec, g):
        return pl.BlockSpec((1, SEQ, LANES), lambda b, hp: (b, 0, sec * 12 + g * 4 + hp))

    out = pl.BlockSpec((1, SEQ, LANES), lambda b, hp: (b, 0, hp))
    body, dep_spec, dep_arg = _anchored(body, 10, dep)
    return pl.pallas_call(
        body, grid=(batch, 4),
        in_specs=[pl.BlockSpec(memory_space=pltpu.SMEM)] + [col(sec, g) for g in range(3) for sec in range(3)] + dep_spec,
        out_specs=[out, out],
        out_shape=[SDS((batch, SEQ, ATTN_OUT), BF16), SDS((batch, SEQ, ATTN_OUT), F32)],
        scratch_shapes=[pltpu.VMEM((SEQ, LANES), BF16), pltpu.VMEM((Q_BLOCK + SEQ, LANES), BF16),
                        pltpu.VMEM((Q_BLOCK + SEQ, LANES), BF16), pltpu.VMEM((nblk, LANES, 2 * Q_BLOCK), BF16),
                        pltpu.VMEM((3, SEQ, LANES), F32), pltpu.VMEM((3, SEQ, LANES), F32),
                        pltpu.VMEM((2, 2 * Q_BLOCK, 2 * Q_BLOCK), F32)],
        compiler_params=_params("parallel", "parallel"), name="attn_fwd")(slopes_r, *([qkv] * 9), *dep_arg)


def _attn_bwd(qkv, datt, lse, dsum, slopes_r, batch):
    nblk = SEQ // Q_BLOCK

    def body(sl_ref, q_ref, k_ref, v_ref, do_ref, l_ref, d_ref, dq_ref, dk_ref, dv_ref,
             qd, kd, vd, dod, ld, dd, dq_acc, dk_acc, dv_acc, dk_part, dv_part, stage, bias):
        gid, hp = pl.program_id(1), pl.program_id(2)
        low = lax.broadcasted_iota(jnp.int32, (Q_BLOCK, LANES), 1) < HEAD_DIM

        def section(g):
            r = GROUPS[g][1]
            nb = SEQ // r // Q_BLOCK
            _gather_classes(q_ref, qd, r)
            kd[0:Q_BLOCK, :] = jnp.zeros((Q_BLOCK, LANES), BF16)
            vd[0:Q_BLOCK, :] = jnp.zeros((Q_BLOCK, LANES), BF16)
            _gather_classes(k_ref, kd, r, Q_BLOCK)
            _gather_classes(v_ref, vd, r, Q_BLOCK)
            _gather_classes(do_ref, dod, r)
            _gather_classes(l_ref, ld, r)
            _gather_classes(d_ref, dd, r)
            _store_biases(bias, sl_ref, g, hp)

            def unit(u, carry):
                off, first, _, _ = _unit_offsets(u, nb)
                pair = pl.ds(off, 2 * Q_BLOCK)
                q2 = _stack_heads(qd[pl.ds(off, Q_BLOCK), :], low)
                do2 = _stack_heads(dod[pl.ds(off, Q_BLOCK), :], low)
                lse_t = ld[pl.ds(off, Q_BLOCK), :]
                dsum_t = dd[pl.ds(off, Q_BLOCK), :]
                lse2 = jnp.concatenate([lse_t[:, 0:1], lse_t[:, HEAD_DIM:HEAD_DIM + 1]], axis=0)
                dsum2 = jnp.concatenate([dsum_t[:, 0:1], dsum_t[:, HEAD_DIM:HEAD_DIM + 1]], axis=0)
                s = _dot_nt(q2, kd[pair, :]) * 0.125 + bias[first.astype(jnp.int32)]
                p = jnp.exp(s - lse2)
                ds = (p * (_dot_nt(do2, vd[pair, :]) - dsum2)).astype(BF16)
                dq_acc[pl.ds(off, Q_BLOCK), :] = _unstack_heads(_dot(ds, kd[pair, :]), low) * 0.125
                dk_part[u] = _dot_tn(ds, q2) * 0.125
                dv_part[u] = _dot_tn(p.astype(BF16), do2)
                return carry

            lax.fori_loop(0, nblk, unit, 0, unroll=ATTN_UNROLL)
            for part, acc in ((dk_part, dk_acc), (dv_part, dv_acc)):
                for b in range(nblk):
                    t = part[b, Q_BLOCK:, :]
                    if b + 1 < nblk:
                        t = t + part[b + 1, 0:Q_BLOCK, :]
                    acc[b * Q_BLOCK:(b + 1) * Q_BLOCK, :] = t
            for acc, out_ref in ((dq_acc, dq_ref), (dk_acc, dk_ref), (dv_acc, dv_ref)):
                _scatter_classes(acc, stage, r)
                out_ref[0] = stage[...].astype(out_ref.dtype)

        for g in range(3):
            pl.when(gid == g)(lambda g=g: section(g))

    def col(sec):
        return pl.BlockSpec((1, SEQ, LANES), lambda b, g, hp: (b, 0, sec * 12 + g * 4 + hp))

    pos = pl.BlockSpec((1, SEQ, LANES), lambda b, g, hp: (b, 0, hp))
    dout = pl.BlockSpec((1, SEQ, LANES), lambda b, g, hp: (b, 0, g * 4 + hp))
    out = SDS((batch, SEQ, ATTN_WIDTH), BF16)
    seq_bf = pltpu.VMEM((SEQ, LANES), BF16)
    seq_f = pltpu.VMEM((SEQ, LANES), F32)
    pad_bf = pltpu.VMEM((Q_BLOCK + SEQ, LANES), BF16)
    part = pltpu.VMEM((nblk, 2 * Q_BLOCK, LANES), F32)
    return pl.pallas_call(
        body, grid=(batch, 3, 4),
        in_specs=[pl.BlockSpec(memory_space=pltpu.SMEM), col(0), col(1), col(2), pos, pos, pos],
        out_specs=[dout, dout, dout],
        out_shape=[out, out, out],
        scratch_shapes=[seq_bf, pad_bf, pad_bf, seq_bf, seq_f, seq_f, seq_f, seq_f, seq_f, part, part, seq_f,
                        pltpu.VMEM((2, 2 * Q_BLOCK, 2 * Q_BLOCK), F32)],
        compiler_params=_params("parallel", "parallel", "parallel"), name="attn_bwd")(
            slopes_r, qkv, qkv, qkv, datt, lse, dsum)


CONV_TC = 128
U_BLOCK0 = 3 * ATTN_WIDTH // CONV_TC
CONV_ROWS = 128
SUBLANES = 8


def _fill_shifted(sh):
    n = SEQ + CONV_PAD - SUBLANES
    for s in range(1, SUBLANES):
        sh[s, 0:n, :] = sh[0, s:s + n, :]


def _tap(sh, base, offset):
    s = offset % SUBLANES
    return sh[s, pl.ds(pl.multiple_of(base + (offset - s), SUBLANES), CONV_ROWS), :]


def _conv_fwd(u, conv_w, conv_b, batch, dep=None):
    nct = D_MODEL // CONV_TC

    def body(ua_ref, ub_ref, w_ref, b_ref, o_ref, sh):
        sh[0, 0:CONV_PAD, :] = jnp.zeros((CONV_PAD, CONV_TC), F32)
        sh[0, CONV_PAD:, :] = ua_ref[0] * _sigmoid(ub_ref[0])
        _fill_shifted(sh)

        def chunk(c, carry):
            base = pl.multiple_of(c * CONV_ROWS, CONV_ROWS)
            acc = jnp.broadcast_to(b_ref[...], (CONV_ROWS, CONV_TC))
            for t in range(CONV_K):
                acc = acc + _tap(sh, base, t + CONV_PAD - (CONV_K - 1)) * w_ref[t:t + 1, :]
            o_ref[0, pl.ds(base, CONV_ROWS), :] = acc
            return carry

        lax.fori_loop(0, SEQ // CONV_ROWS, chunk, 0)

    body, dep_spec, dep_arg = _anchored(body, 4, dep)
    return pl.pallas_call(
        body, grid=(nct, batch),
        in_specs=[pl.BlockSpec((1, SEQ, CONV_TC), lambda j, b: (b, 0, U_BLOCK0 + j)),
                  pl.BlockSpec((1, SEQ, CONV_TC), lambda j, b: (b, 0, U_BLOCK0 + nct + j)),
                  pl.BlockSpec((CONV_PAD, CONV_TC), lambda j, b: (0, j)),
                  pl.BlockSpec((1, CONV_TC), lambda j, b: (0, j))] + dep_spec,
        out_specs=pl.BlockSpec((1, SEQ, CONV_TC), lambda j, b: (b, 0, j)),
        out_shape=SDS((batch, SEQ, D_MODEL), F32),
        scratch_shapes=[pltpu.VMEM((SUBLANES, SEQ + CONV_PAD, CONV_TC), F32)],
        compiler_params=_params("parallel", "parallel"), name="conv_fwd")(u, u, conv_w, conv_b, *dep_arg)


def _conv_bwd(u, dc1, conv_w, batch, dep=None):
    nct = D_MODEL // CONV_TC
    nchunk = SEQ // CONV_ROWS

    def body(ua_ref, ub_ref, d_ref, w_ref, dua_ref, dub_ref, gw_ref, gb_ref, shc, shd, gacc):
        b = pl.program_id(1)
        shc[0, 0:CONV_PAD, :] = jnp.zeros((CONV_PAD, CONV_TC), F32)
        shc[0, CONV_PAD:, :] = ua_ref[0] * _sigmoid(ub_ref[0])
        _fill_shifted(shc)
        shd[0, 0:SEQ, :] = d_ref[0]
        shd[0, SEQ:, :] = jnp.zeros((CONV_PAD, CONV_TC), F32)
        _fill_shifted(shd)

        @pl.when(b == 0)
        def _():
            gacc[...] = jnp.zeros_like(gacc)
            gb_ref[...] = jnp.zeros_like(gb_ref)

        gb_ref[...] += _rowsum(d_ref[0])

        def chunk(c, carry):
            base = pl.multiple_of(c * CONV_ROWS, CONV_ROWS)
            dcur = shd[0, pl.ds(base, CONV_ROWS), :]
            acc = jnp.zeros((CONV_ROWS, CONV_TC), F32)
            for t in range(CONV_K):
                acc = acc + _tap(shd, base, CONV_K - 1 - t) * w_ref[t:t + 1, :]
                prod = _tap(shc, base, t + CONV_PAD - (CONV_K - 1)) * dcur
                gacc[t] += jnp.sum(prod.reshape(CONV_ROWS // 8, 8, CONV_TC), axis=0)
            ua = ua_ref[0, pl.ds(base, CONV_ROWS), :]
            sg = _sigmoid(ub_ref[0, pl.ds(base, CONV_ROWS), :])
            dua_ref[0, pl.ds(base, CONV_ROWS), :] = (acc * sg).astype(dua_ref.dtype)
            dub_ref[0, pl.ds(base, CONV_ROWS), :] = (acc * ua * sg * (1.0 - sg)).astype(dub_ref.dtype)
            return carry

        lax.fori_loop(0, nchunk, chunk, 0)

        @pl.when(b == batch - 1)
        def _():
            for t in range(CONV_K):
                gw_ref[t:t + 1, :] = jnp.sum(gacc[t], axis=0, keepdims=True)
            gw_ref[CONV_K:CONV_PAD, :] = jnp.zeros((CONV_PAD - CONV_K, CONV_TC), F32)

    du = SDS((batch, SEQ, D_MODEL), BF16)
    body, dep_spec, dep_arg = _anchored(body, 4, dep)
    return pl.pallas_call(
        body, grid=(nct, batch),
        in_specs=[pl.BlockSpec((1, SEQ, CONV_TC), lambda j, b: (b, 0, U_BLOCK0 + j)),
                  pl.BlockSpec((1, SEQ, CONV_TC), lambda j, b: (b, 0, U_BLOCK0 + nct + j)),
                  pl.BlockSpec((1, SEQ, CONV_TC), lambda j, b: (b, 0, j)),
                  pl.BlockSpec((CONV_PAD, CONV_TC), lambda j, b: (0, j))] + dep_spec,
        out_specs=[pl.BlockSpec((1, SEQ, CONV_TC), lambda j, b: (b, 0, j)),
                   pl.BlockSpec((1, SEQ, CONV_TC), lambda j, b: (b, 0, j)),
                   pl.BlockSpec((CONV_PAD, CONV_TC), lambda j, b: (0, j)),
                   pl.BlockSpec((1, CONV_TC), lambda j, b: (0, j))],
        out_shape=[du, du, SDS((CONV_PAD, D_MODEL), F32), SDS((1, D_MODEL), F32)],
        scratch_shapes=[pltpu.VMEM((SUBLANES, SEQ + CONV_PAD, CONV_TC), F32),
                        pltpu.VMEM((SUBLANES, SEQ + CONV_PAD, CONV_TC), F32),
                        pltpu.VMEM((CONV_K, 8, CONV_TC), F32)],
        compiler_params=_params("parallel", "arbitrary"), name="conv_bwd")(u, u, dc1, conv_w, *dep_arg)


MID_TM = 256


def _layernorm_stats(c1):
    mu = jnp.mean(c1, axis=-1, keepdims=True)
    cen = c1 - mu
    rs = lax.rsqrt(jnp.mean(cen * cen, axis=-1, keepdims=True) + LN_EPS)
    return cen * rs, rs


GATE_PARTS = 4
GATE_PART = 2 * D_MODEL // GATE_PARTS
GATE_PART0 = (IN_WIDTH - 2 * D_MODEL) // GATE_PART


def _gate_specs(tm):
    return [pl.BlockSpec((tm, GATE_PART), lambda i, k=k: (i, GATE_PART0 + k)) for k in range(GATE_PARTS)]


def _mid_fwd(att, c1, proj, x, w_a, w_c, w_o, gate_b, ln_g, ln_b, g2, dep=None):
    T = x.shape[0]
    tm = MID_TM

    def body(att_ref, c1_ref, lg0, lg1, lg2, lg3, x_ref, wa_ref, wc_ref, wo_ref, gb_ref, lng_ref, lnb_ref, g2_ref,
             c3_ref, ya_ref, yc_ref, mix_ref, x1_ref, h2_ref):
        logits = jnp.concatenate([lg0[...], lg1[...], lg2[...], lg3[...]], axis=1)
        ya = _dot(att_ref[...], wa_ref[...])
        xh, _ = _layernorm_stats(c1_ref[...])
        c2 = xh * lng_ref[...] + lnb_ref[...]
        c3 = (c2 * _sigmoid(c2)).astype(BF16)
        c3_ref[...] = c3
        yc = _dot(c3, wc_ref[...])
        gates = _sigmoid(logits + gb_ref[...])
        mix = (gates[:, :D_MODEL] * ya + gates[:, D_MODEL:] * yc).astype(BF16)
        ya_ref[...] = ya.astype(BF16)
        yc_ref[...] = yc.astype(BF16)
        mix_ref[...] = mix
        x1 = x_ref[...] + _dot(mix, wo_ref[...])
        x1_ref[...] = x1
        r = lax.rsqrt(jnp.mean(x1 * x1, axis=-1, keepdims=True) + RMS_EPS)
        h2_ref[...] = (x1 * r * g2_ref[...]).astype(BF16)

    row = lambda n: pl.BlockSpec((tm, n), lambda i: (i, 0))
    full = lambda a, b: pl.BlockSpec((a, b), lambda i: (0, 0))
    body, dep_spec, dep_arg = _anchored(body, 10 + GATE_PARTS, dep)
    return pl.pallas_call(
        body, grid=(T // tm,),
        in_specs=[row(ATTN_OUT), row(D_MODEL)] + _gate_specs(tm) + [row(D_MODEL),
                  full(ATTN_OUT, D_MODEL), full(D_MODEL, D_MODEL), full(D_MODEL, D_MODEL),
                  full(1, 2 * D_MODEL), full(1, D_MODEL), full(1, D_MODEL), full(1, D_MODEL)] + dep_spec,
        out_specs=[row(D_MODEL), row(D_MODEL), row(D_MODEL), row(D_MODEL), row(D_MODEL), row(D_MODEL)],
        out_shape=[SDS((T, D_MODEL), BF16), SDS((T, D_MODEL), BF16), SDS((T, D_MODEL), BF16), SDS((T, D_MODEL), BF16),
                   SDS((T, D_MODEL), F32), SDS((T, D_MODEL), BF16)],
        compiler_params=_params("parallel"), name="mid_fwd")(att, c1, *([proj] * GATE_PARTS), x, w_a, w_c, w_o,
                                                             gate_b, ln_g, ln_b, g2, *dep_arg)


def _mid_bwd(dx1b, ya, yc, proj, att, c1, w_a, w_c, w_o, gate_b, ln_g, ln_b, head_ones, dep=None):
    T = dx1b.shape[0]
    tm = MID_TM

    def body(dx_ref, ya_ref, yc_ref, lg0, lg1, lg2, lg3, att_ref, c1_ref, wa_ref, wc_ref, wo_ref, gb_ref, lng_ref,
             lnb_ref, e_ref, dlg_ref, dya_ref, dyc_ref, datt_ref, dsum_ref, dc1_ref, ggb_ref, glg_ref, glb_ref):
        logits = jnp.concatenate([lg0[...], lg1[...], lg2[...], lg3[...]], axis=1)
        @pl.when(pl.program_id(0) == 0)
        def _():
            ggb_ref[...] = jnp.zeros_like(ggb_ref)
            glg_ref[...] = jnp.zeros_like(glg_ref)
            glb_ref[...] = jnp.zeros_like(glb_ref)

        dmix = _dot_nt(dx_ref[...], wo_ref[...])
        gates = _sigmoid(logits + gb_ref[...])
        ga, gc = gates[:, :D_MODEL], gates[:, D_MODEL:]
        dla = dmix * ya_ref[...].astype(F32) * ga * (1.0 - ga)
        dlc = dmix * yc_ref[...].astype(F32) * gc * (1.0 - gc)
        dlg_ref[:, :D_MODEL] = dla.astype(BF16)
        dlg_ref[:, D_MODEL:] = dlc.astype(BF16)
        ggb_ref[:, :D_MODEL] += _rowsum(dla)
        ggb_ref[:, D_MODEL:] += _rowsum(dlc)
        dya = (dmix * ga).astype(BF16)
        dyc = (dmix * gc).astype(BF16)
        dya_ref[...] = dya
        dyc_ref[...] = dyc
        datt = _dot_nt(dya, wa_ref[...])
        datt_ref[...] = datt
        dsum_ref[...] = jnp.dot(datt * att_ref[...].astype(F32), e_ref[...], preferred_element_type=F32,
                                precision=lax.Precision.HIGHEST)
        dc3 = _dot_nt(dyc, wc_ref[...])
        xh, rs = _layernorm_stats(c1_ref[...])
        c2 = xh * lng_ref[...] + lnb_ref[...]
        sg = _sigmoid(c2)
        dc2 = dc3 * (sg * (1.0 + c2 * (1.0 - sg)))
        glg_ref[...] += _rowsum(dc2 * xh)
        glb_ref[...] += _rowsum(dc2)
        dxh = dc2 * lng_ref[...]
        dc1_ref[...] = rs * (dxh - jnp.mean(dxh, axis=-1, keepdims=True) - xh * jnp.mean(dxh * xh, axis=-1, keepdims=True))

    row = lambda n: pl.BlockSpec((tm, n), lambda i: (i, 0))
    full = lambda a, b: pl.BlockSpec((a, b), lambda i: (0, 0))
    body, dep_spec, dep_arg = _anchored(body, 12 + GATE_PARTS, dep)
    return pl.pallas_call(
        body, grid=(T // tm,),
        in_specs=[row(D_MODEL), row(D_MODEL), row(D_MODEL)] + _gate_specs(tm) + [row(ATTN_OUT), row(D_MODEL),
                  full(ATTN_OUT, D_MODEL), full(D_MODEL, D_MODEL), full(D_MODEL, D_MODEL),
                  full(1, 2 * D_MODEL), full(1, D_MODEL), full(1, D_MODEL), full(ATTN_OUT, ATTN_OUT)] + dep_spec,
        out_specs=[row(2 * D_MODEL), row(D_MODEL), row(D_MODEL), row(ATTN_OUT), row(ATTN_OUT), row(D_MODEL),
                   full(1, 2 * D_MODEL), full(1, D_MODEL), full(1, D_MODEL)],
        out_shape=[SDS((T, 2 * D_MODEL), BF16), SDS((T, D_MODEL), BF16), SDS((T, D_MODEL), BF16), SDS((T, ATTN_OUT), F32),
                   SDS((T, ATTN_OUT), F32), SDS((T, D_MODEL), F32),
                   SDS((1, 2 * D_MODEL), F32), SDS((1, D_MODEL), F32), SDS((1, D_MODEL), F32)],
        compiler_params=_params("arbitrary"), name="mid_bwd")(dx1b, ya, yc, *([proj] * GATE_PARTS), att, c1, w_a, w_c, w_o,
                                                               gate_b, ln_g, ln_b, head_ones, *dep_arg)


FFN_TM = 256
FFN_CHUNK = 512
FFN_SUB = tuple((lo, min(lo + FFN_CHUNK, D_FF)) for lo in range(0, D_FF, FFN_CHUNK))


def _rms_bwd(dy_times_g, xh, r):
    return r * (dy_times_g - xh * jnp.mean(dy_times_g * xh, axis=-1, keepdims=True))


def _load_resident(pairs, sems):
    @pl.when(pl.program_id(0) == 0)
    def _():
        copies = [pltpu.make_async_copy(src, dst, sems.at[k]) for k, (src, dst) in enumerate(pairs)]
        for cp in copies:
            cp.start()
        for cp in copies:
            cp.wait()


def _ffn_fwd(h2, x1, target, gf, w_g_t, w_u_t, w_d):
    T = h2.shape[0]
    tm = FFN_TM

    def body(h_ref, x1_ref, t_ref, gf_ref, wg_hbm, wu_hbm, wd_hbm,
             a_ref, b_ref, f_ref, dx2_ref, dx2b_ref, loss_ref, gnf_ref, wg, wu, wd, sems):
        _load_resident(((wg_hbm, wg), (wu_hbm, wu), (wd_hbm, wd)), sems)

        @pl.when(pl.program_id(0) == 0)
        def _():
            loss_ref[...] = jnp.zeros_like(loss_ref)
            gnf_ref[...] = jnp.zeros_like(gnf_ref)

        h = h_ref[...]
        x2 = x1_ref[...]
        for lo, hi in FFN_SUB:
            a = _dot_nt(h, wg[lo:hi, :])
            b = _dot_nt(h, wu[lo:hi, :])
            f = (a * _sigmoid(a) * b).astype(BF16)
            a_ref[:, lo:hi] = a.astype(BF16)
            b_ref[:, lo:hi] = b.astype(BF16)
            f_ref[:, lo:hi] = f
            x2 = x2 + _dot(f, wd[lo:hi, :])

        r = lax.rsqrt(jnp.mean(x2 * x2, axis=-1, keepdims=True) + RMS_EPS)
        xh = x2 * r
        err = xh * gf_ref[...] - t_ref[...]
        loss_ref[...] += (0.5 / D_MODEL) * jnp.sum(err * err)
        dy = err * (1.0 / D_MODEL)
        gnf_ref[...] += _rowsum(dy * xh)
        dx2 = _rms_bwd(dy * gf_ref[...], xh, r)
        dx2_ref[...] = dx2
        dx2b_ref[...] = dx2.astype(BF16)

    row = lambda n: pl.BlockSpec((tm, n), lambda i: (i, 0))
    const = lambda n: pl.BlockSpec((1, n), lambda i: (0, 0))
    wshape = pltpu.VMEM((D_FF, D_MODEL), BF16)
    return pl.pallas_call(
        body, grid=(T // tm,),
        in_specs=[row(D_MODEL), row(D_MODEL), row(D_MODEL), const(D_MODEL), ANY_SPEC, ANY_SPEC, ANY_SPEC],
        out_specs=[row(D_FF), row(D_FF), row(D_FF), row(D_MODEL), row(D_MODEL), const(128), const(D_MODEL)],
        out_shape=[SDS((T, D_FF), BF16), SDS((T, D_FF), BF16), SDS((T, D_FF), BF16), SDS((T, D_MODEL), F32),
                   SDS((T, D_MODEL), BF16), SDS((1, 128), F32), SDS((1, D_MODEL), F32)],
        scratch_shapes=[wshape, wshape, wshape, pltpu.SemaphoreType.DMA((3,))],
        compiler_params=_params("arbitrary"), name="ffn_fwd")(h2, x1, target, gf, w_g_t, w_u_t, w_d)


def _ffn_bwd(dx2b, dx2, a, b, x1, g2, w_g_t, w_u_t, w_d):
    T = dx2.shape[0]
    tm = FFN_TM

    def body(dxb_ref, dx2_ref, a_ref, b_ref, x1_ref, g2_ref, wg_hbm, wu_hbm, wd_hbm,
             da_ref, db_ref, dx1_ref, dx1b_ref, gn2_ref, wg, wu, wd, sems):
        _load_resident(((wg_hbm, wg), (wu_hbm, wu), (wd_hbm, wd)), sems)

        @pl.when(pl.program_id(0) == 0)
        def _():
            gn2_ref[...] = jnp.zeros_like(gn2_ref)

        dxb = dxb_ref[...]
        dh2 = jnp.zeros((tm, D_MODEL), F32)
        for lo, hi in FFN_SUB:
            df = _dot_nt(dxb, wd[lo:hi, :])
            av = a_ref[:, lo:hi].astype(F32)
            bv = b_ref[:, lo:hi].astype(F32)
            sg = _sigmoid(av)
            db = (df * av * sg).astype(BF16)
            da = (df * bv * (sg * (1.0 + av * (1.0 - sg)))).astype(BF16)
            da_ref[:, lo:hi] = da
            db_ref[:, lo:hi] = db
            dh2 = dh2 + _dot(da, wg[lo:hi, :]) + _dot(db, wu[lo:hi, :])

        x1 = x1_ref[...]
        r = lax.rsqrt(jnp.mean(x1 * x1, axis=-1, keepdims=True) + RMS_EPS)
        xh = x1 * r
        gn2_ref[...] += _rowsum(dh2 * xh)
        dx1 = dx2_ref[...] + _rms_bwd(dh2 * g2_ref[...], xh, r)
        dx1_ref[...] = dx1
        dx1b_ref[...] = dx1.astype(BF16)

    row = lambda n: pl.BlockSpec((tm, n), lambda i: (i, 0))
    const = lambda n: pl.BlockSpec((1, n), lambda i: (0, 0))
    wshape = pltpu.VMEM((D_FF, D_MODEL), BF16)
    return pl.pallas_call(
        body, grid=(T // tm,),
        in_specs=[row(D_MODEL), row(D_MODEL), row(D_FF), row(D_FF), row(D_MODEL), const(D_MODEL),
                  ANY_SPEC, ANY_SPEC, ANY_SPEC],
        out_specs=[row(D_FF), row(D_FF), row(D_MODEL), row(D_MODEL), const(D_MODEL)],
        out_shape=[SDS((T, D_FF), BF16), SDS((T, D_FF), BF16), SDS((T, D_MODEL), F32), SDS((T, D_MODEL), BF16),
                   SDS((1, D_MODEL), F32)],
        scratch_shapes=[wshape, wshape, wshape, pltpu.SemaphoreType.DMA((3,))],
        compiler_params=_params("arbitrary"), name="ffn_bwd")(dx2b, dx2, a, b, x1, g2, w_g_t, w_u_t, w_d)


def _in_bwd(pieces, w_in_t, x, dx1, g1, dep=None):
    T = x.shape[0]
    tm = IN_TM
    npc = len(pieces)
    assert sum(p.shape[1] for p in pieces) == IN_WIDTH

    def body(*refs):
        p_refs = refs[:npc]
        w_hbm, x_ref, dx1_ref, g_ref, dx_ref, gn1_ref, w_vmem, sem = refs[npc:]

        @pl.when(pl.program_id(0) == 0)
        def _():
            cp = pltpu.make_async_copy(w_hbm, w_vmem, sem)
            cp.start()
            cp.wait()
            gn1_ref[...] = jnp.zeros_like(gn1_ref)

        dh = jnp.zeros((tm, D_MODEL), F32)
        col = 0
        for p_ref in p_refs:
            for j in range(p_ref.shape[1] // IN_CHUNK):
                dh = dh + _dot(p_ref[:, j * IN_CHUNK:(j + 1) * IN_CHUNK], w_vmem[col:col + IN_CHUNK, :])
                col += IN_CHUNK
        xv = x_ref[...]
        r = lax.rsqrt(jnp.mean(xv * xv, axis=-1, keepdims=True) + RMS_EPS)
        xh = xv * r
        gn1_ref[...] += _rowsum(dh * xh)
        dx_ref[...] = dx1_ref[...] + _rms_bwd(dh * g_ref[...], xh, r)

    row = lambda n: pl.BlockSpec((tm, n), lambda i: (i, 0))
    body, dep_spec, dep_arg = _anchored(body, npc + 4, dep)
    return pl.pallas_call(
        body, grid=(T // tm,),
        in_specs=[row(p.shape[1]) for p in pieces]
        + [pl.BlockSpec(memory_space=pl.ANY), row(D_MODEL), row(D_MODEL), pl.BlockSpec((1, D_MODEL), lambda i: (0, 0))]
        + dep_spec,
        out_specs=[row(D_MODEL), pl.BlockSpec((1, D_MODEL), lambda i: (0, 0))],
        out_shape=[SDS((T, D_MODEL), F32), SDS((1, D_MODEL), F32)],
        scratch_shapes=[pltpu.VMEM((IN_WIDTH, D_MODEL), BF16), pltpu.SemaphoreType.DMA],
        compiler_params=_params("arbitrary"), name="in_bwd")(*pieces, w_in_t, x, dx1, g1, *dep_arg)


def _local_step(x, target, in_proj, small, late_weights=None, emit=None):
    T = x.shape[0]
    batch = T // SEQ
    slopes_r = jnp.asarray(_slopes_times_dilation())
    emit = emit or (lambda names, grads: None)

    h, proj, w = in_proj()
    proj3 = proj.reshape(batch, SEQ, IN_WIDTH)

    att, lse = _attn_fwd(proj3, slopes_r, batch, w.get("token"))
    att = att.reshape(T, ATTN_OUT)
    if late_weights is not None:
        w = {**w, **late_weights("after_attention", att)}

    c1 = _conv_fwd(proj3, w["conv_w"], small["conv_b"], batch, w.get("token")).reshape(T, D_MODEL)
    if late_weights is not None:
        w = {**w, **late_weights(LATE_MERGE, (att, c1))}

    c3, ya, yc, mix, x1, h2 = _mid_fwd(
        att, c1, proj, x, w["w_attn_out"], w["w_conv_out"], w["w_o"],
        small["gate_b"], small["conv_ln_g"], small["conv_ln_b"], small["norm2_g"], w.get("token"))
    if late_weights is not None:
        w = {**w, **late_weights(LATE_FFN, h2)}

    a, b, f, dx2, dx2b, loss, g_normf = _ffn_fwd(h2, x1, target, small["norm_f_g"],
                                                   w["w_ffn_gate"], w["w_ffn_up"], w["w_ffn_down"])

    da, db, dx1, dx1b, g_norm2 = _ffn_bwd(dx2b, dx2, a, b, x1, small["norm2_g"],
                                           w["w_ffn_gate"], w["w_ffn_up"], w["w_ffn_down"])
    gw = {}
    gw["w_ffn_down"] = _mm_tn(f, dx2b, BF16, "gw_ffn_down", tn=1024)
    gw["w_ffn_gate"] = _mm_tn(da, h2, BF16, "gw_ffn_gate", tn=1024)
    gw["w_ffn_up"] = _mm_tn(db, h2, BF16, "gw_ffn_up", tn=1024)
    token = emit(("w_ffn_gate", "w_ffn_up", "w_ffn_down"), gw)

    head_ones = jnp.asarray(np.kron(np.eye(HEADS_PER_GROUP, dtype=np.float32), np.ones((HEAD_DIM, HEAD_DIM), np.float32)))
    dlogits, dya, dyc, datt, dsum, dc1, g_gate_b, g_ln_g, g_ln_b = _mid_bwd(
        dx1b, ya, yc, proj, att, c1, w["w_attn_out"], w["w_conv_out"], w["w_o"],
        small["gate_b"], small["conv_ln_g"], small["conv_ln_b"], head_ones, token)
    gw["w_o"] = _mm_tn(mix, dx1b, BF16, "gw_o", tn=1024)
    gw["w_attn_out"] = _mm_tn(att, dya, BF16, "gw_attn_out", tn=1024)
    gw["w_conv_out"] = _mm_tn(c3, dyc, BF16, "gw_conv_out", tn=1024)
    token = emit(("w_conv_out", "w_attn_out", "w_o"), gw)

    dua, dub, g_conv_w, g_conv_b = _conv_bwd(proj3, dc1.reshape(batch, SEQ, D_MODEL), w["conv_w"], batch, token)

    dq, dk, dv = _attn_bwd(proj3, datt.reshape(batch, SEQ, ATTN_OUT), lse, dsum.reshape(batch, SEQ, ATTN_OUT),
                           slopes_r, batch)
    pieces = [dq.reshape(T, ATTN_WIDTH), dk.reshape(T, ATTN_WIDTH), dv.reshape(T, ATTN_WIDTH),
              dua.reshape(T, D_MODEL), dub.reshape(T, D_MODEL), dlogits]

    names = ("q", "k", "v", "ua", "ub", "gate")
    gw["w_in"] = jnp.concatenate([_mm_tn(p, h, BF16, "gw_in_" + nm, tn=1024) for nm, p in zip(names, pieces)], axis=0)
    gw["conv_w"] = g_conv_w
    token = emit(("w_in", "conv_w"), gw)
    grad_x, g_norm1 = _in_bwd(pieces, w["w_in"], x, dx1, small["norm1_g"], token)

    gsmall = {"norm1_g": g_norm1, "gate_b": g_gate_b, "conv_b": g_conv_b, "conv_ln_g": g_ln_g, "conv_ln_b": g_ln_b,
              "norm2_g": g_norm2, "norm_f_g": g_normf}
    return loss, grad_x, gw, gsmall


ANY = pl.BlockSpec(memory_space=pl.ANY)


def _all_gather(arrs):
    n = len(arrs)

    def body(*refs):
        ins, outs = refs[:n], refs[n:2 * n]
        send_sems, recv_sems, local_sems = refs[2 * n:]
        x, y, c = lax.axis_index("x"), lax.axis_index("y"), lax.axis_index("c")
        me, sibling = (x, y, c), (x, y, 1 - c)
        chips = [(1 - x, y), (x, 1 - y), (1 - x, 1 - y)]

        def copy(a, k, block, to, src=None):
            px, py, pc = block
            dst = outs[a].at[4 * px + 2 * py + pc]
            return pltpu.make_async_remote_copy(
                src_ref=dst if src is None else src, dst_ref=dst,
                send_sem=send_sems.at[a, k], recv_sem=recv_sems.at[a, k], device_id=to, device_id_type=MESH)

        mine = [pltpu.make_async_copy(ins[a], outs[a].at[4 * x + 2 * y + c], local_sems.at[a]) for a in range(n)]
        for cp in mine:
            cp.start()
        first = []
        for j, chip in enumerate(chips):
            first += [copy(a, 1 + j, me, (*chip, c), src=ins[a]) for a in range(n)]
        first += [copy(a, 0, me, sibling, src=ins[a]) for a in range(n)]
        for cp in first:
            cp.start()
        passed = []
        for j, chip in enumerate(chips):
            for a in range(n):
                copy(a, 1 + j, (*chip, c), me).wait_recv()
                cp = copy(a, 4 + j, (*chip, c), sibling)
                cp.start()
                passed.append(cp)
        for a in range(n):
            copy(a, 0, sibling, me).wait_recv()
        for j, chip in enumerate(chips):
            for a in range(n):
                copy(a, 4 + j, (*chip, 1 - c), me).wait_recv()
        for cp in first + passed:
            cp.wait_send()
        for cp in mine:
            cp.wait()

    return pl.pallas_call(
        body, in_specs=[ANY] * n, out_specs=[ANY] * n,
        out_shape=[SDS((N_DEV,) + a.shape, a.dtype) for a in arrs],
        scratch_shapes=[pltpu.SemaphoreType.DMA((n, 7)), pltpu.SemaphoreType.DMA((n, 7)), pltpu.SemaphoreType.DMA((n,))],
        name="all_gather_weights")(*arrs)


HBM =pl.BlockSpec(memory_space=pltpu.HBM)
SEM = pl.BlockSpec(memory_space=pltpu.SEMAPHORE)
ALL_PEERS = tuple(range(1, N_DEV))
OTHER_CHIPS = (2, 4, 6)
SPLIT_EFFECT = pltpu.CompilerParams(has_side_effects=pltpu.SideEffectType.DATAFLOW_SIDE_EFFECTING)


def _exchange_copies(mode, ks, srcs, lands, send_sems, recv_sems):
    x, y, c = lax.axis_index("x"), lax.axis_index("y"), lax.axis_index("c")
    me = 4 * x + 2 * y + c
    send, recv = [], []
    for a in range(len(lands)):
        for i, k in enumerate(ks):
            peer = (x ^ ((k >> 2) & 1), y ^ ((k >> 1) & 1), c ^ (k & 1))
            pidx = 4 * peer[0] + 2 * peer[1] + peer[2]
            if mode == "gather":
                src, to, out_slot, in_slot = srcs[a], peer, me, pidx
            elif mode == "scatter":
                src, to, out_slot, in_slot = srcs[a].at[pidx], peer, me, pidx
            elif mode == "chip_scatter":
                src, to, out_slot, in_slot = srcs[a].at[pidx >> 1], peer, me >> 1, pidx >> 1
            else:
                src, to, out_slot, in_slot = lands[a].at[pidx], (x, y, 1 - c), pidx, pidx ^ 1
            s = a * len(ks) + i
            send.append(pltpu.make_async_remote_copy(
                src_ref=src, dst_ref=lands[a].at[out_slot], send_sem=send_sems.at[s], recv_sem=recv_sems.at[s],
                device_id=to, device_id_type=MESH))
            recv.append(pltpu.make_async_remote_copy(
                src_ref=src, dst_ref=lands[a].at[in_slot], send_sem=send_sems.at[s], recv_sem=recv_sems.at[s],
                device_id=to, device_id_type=MESH))
    return send, recv


def _send_start(mode, ks, name, srcs=(), lands=None, dep=None):
    srcs = list(srcs)
    if lands is None:
        slots = 4 if mode == "chip_scatter" else N_DEV
        lands = [lax.empty((slots,) + (s.shape if mode == "gather" else s.shape[1:]), s.dtype) for s in srcs]
    ns, nl = len(srcs), len(lands)
    nsem = nl * len(ks)

    def body(*refs):
        send, _ = _exchange_copies(mode, ks, refs[:ns], refs[ns:ns + nl], refs[ns + nl], refs[ns + nl + 1])
        for cp in send:
            cp.start()
        token = refs[-1]
        token[...] = jnp.zeros_like(token)

    both = srcs + list(lands)
    body, dep_spec, dep_arg = _anchored(body, ns + nl, dep)
    res = pl.pallas_call(
        body, name=name,
        out_shape=(pltpu.SemaphoreType.DMA((nsem,)), pltpu.SemaphoreType.DMA((nsem,)),
                   *[pltpu.HBM(a.shape, a.dtype) for a in both], SDS((8, 128), F32)),
        in_specs=[HBM] * (ns + nl) + dep_spec,
        out_specs=(SEM, SEM, *([HBM] * (ns + nl)), pl.BlockSpec(memory_space=pltpu.VMEM)),
        input_output_aliases={i: 2 + i for i in range(ns + nl)}, compiler_params=SPLIT_EFFECT,
    )(*[pltpu.with_memory_space_constraint(a, pltpu.HBM) for a in both], *dep_arg)
    return dict(mode=mode, ks=ks, send_sems=res[0], recv_sems=res[1], srcs=res[2:2 + ns], lands=res[2 + ns:2 + ns + nl],
                token=res[-1])


def _send_wait(started, after, name):
    ns, nl = len(started["srcs"]), len(started["lands"])

    def body(*refs):
        send, recv = _exchange_copies(started["mode"], started["ks"], refs[:ns], refs[ns:ns + nl],
                                      refs[ns + nl], refs[ns + nl + 1])
        for cp in send:
            cp.wait_send()
        for cp in recv:
            cp.wait_recv()

    both = list(started["srcs"]) + list(started["lands"])
    after = after if isinstance(after, (tuple, list)) else (after,)
    res = pl.pallas_call(
        body, name=name,
        out_shape=tuple(pltpu.HBM(a.shape, a.dtype) for a in both),
        in_specs=[HBM] * (ns + nl) + [SEM, SEM] + [ANY] * len(after), out_specs=tuple([HBM] * (ns + nl)),
        input_output_aliases={i: i for i in range(ns + nl)}, compiler_params=SPLIT_EFFECT,
    )(*both, started["send_sems"], started["recv_sems"], *after)
    return res[:ns], res[ns:]


def _exchange_sibling(gs):
    n = len(gs)

    def body(*refs):
        ins, outs = refs[:n], refs[n:2 * n]
        send_sems, recv_sems = refs[2 * n:]
        x, y, c = lax.axis_index("x"), lax.axis_index("y"), lax.axis_index("c")
        copies = []
        for a in range(n):
            for j in range(4):
                copies.append(pltpu.make_async_remote_copy(
                    src_ref=ins[a].at[2 * j + (1 - c)], dst_ref=outs[a].at[j],
                    send_sem=send_sems.at[a, j], recv_sem=recv_sems.at[a, j],
                    device_id=(x, y, 1 - c), device_id_type=MESH))
        for cp in copies:
            cp.start()
        for cp in copies:
            cp.wait_recv()
        for cp in copies:
            cp.wait_send()

    return pl.pallas_call(
        body, in_specs=[ANY] * n, out_specs=[ANY] * n,
        out_shape=[SDS((4,) + g.shape[1:], g.dtype) for g in gs],
        scratch_shapes=[pltpu.SemaphoreType.DMA((n, 4)), pltpu.SemaphoreType.DMA((n, 4))],
        name="reduce_scatter_sibling")(*gs)


def _add_pair(g, r1, core, name):
    _, rows, cols = g.shape
    tr = _row_tile(rows, cols, 3 * g.dtype.itemsize)

    def body(c_ref, g_ref, r_ref, o_ref):
        o_ref[...] = (g_ref[...].astype(F32) + r_ref[...].astype(F32)).astype(o_ref.dtype)

    return pl.pallas_call(
        body,
        grid_spec=pltpu.PrefetchScalarGridSpec(
            num_scalar_prefetch=1, grid=(4, rows // tr),
            in_specs=[pl.BlockSpec((1, tr, cols), lambda j, i, c_ref: (2 * j + c_ref[0], i, 0)),
                      pl.BlockSpec((1, tr, cols), lambda j, i, c_ref: (j, i, 0))],
            out_specs=pl.BlockSpec((1, tr, cols), lambda j, i, c_ref: (j, i, 0))),
        out_shape=SDS((4, rows, cols), g.dtype),
        compiler_params=_params("parallel", "parallel"), name=name)(core, g, r1)


def _row_tile(rows, cols, itemsize_total):
    budget = (4 << 20) // max(1, cols * itemsize_total)
    if rows <= budget:
        return rows
    t = rows
    while t > budget and t % 2 == 0 and (t // 2) % 16 == 0:
        t //= 2
    return t


def _adam_math(g, w, m, v):
    m_new = ADAM_B1 * m + (1.0 - ADAM_B1) * g
    v_new = ADAM_B2 * v + (1.0 - ADAM_B2) * (g * g)
    m_hat = m_new / (1.0 - ADAM_B1 ** ADAM_STEP)
    v_hat = v_new / (1.0 - ADAM_B2 ** ADAM_STEP)
    delta = -ADAM_LR * (m_hat / (jnp.sqrt(v_hat) + ADAM_EPS) + ADAM_WD * w)
    return delta, m_new, v_new


def _sum_adam(parts, own, mine, w, m, v, name):
    rows, cols = w.shape
    nparts = parts.shape[0]
    tr = _row_tile(rows, cols, (nparts + 1) * parts.dtype.itemsize + 7 * 4)

    def body(mine_ref, p_ref, own_ref, w_ref, m_ref, v_ref, g_ref, d_ref, mo_ref, vo_ref):
        g = None
        for s in range(nparts):
            part = jnp.where(mine_ref[0] == s, own_ref[0], p_ref[s]).astype(F32)
            g = part if g is None else g + part
        delta, m_new, v_new = _adam_math(g, w_ref[...], m_ref[...], v_ref[...])
        g_ref[...] = g
        d_ref[...] = delta
        mo_ref[...] = m_new
        vo_ref[...] = v_new

    blk = pl.BlockSpec((tr, cols), lambda i, mine_ref: (i, 0))
    out = SDS((rows, cols), F32)
    return pl.pallas_call(
        body,
        grid_spec=pltpu.PrefetchScalarGridSpec(
            num_scalar_prefetch=1, grid=(rows // tr,),
            in_specs=[pl.BlockSpec((nparts, tr, cols), lambda i, mine_ref: (0, i, 0)),
                      pl.BlockSpec((1, tr, cols), lambda i, mine_ref: (mine_ref[0], i, 0)), blk, blk, blk],
            out_specs=[blk, blk, blk, blk]),
        out_shape=[out, out, out, out],
        compiler_params=_params("parallel"), name=name)(mine, parts, own, w, m, v)


SMALL_ROWS = 72


def _small_allreduce_adam(gpart, w, m, v, row_counts, dep=None):
    def reduce_body(g_ref, go_ref, gath, send_sems, recv_sems):
        x, y, c = lax.axis_index("x"), lax.axis_index("y"), lax.axis_index("c")
        me = 4 * x + 2 * y + c
        gath[me] = g_ref[...]
        copies = []
        for k in range(1, N_DEV):
            fx, fy, fc = (k >> 2) & 1, (k >> 1) & 1, k & 1
            peer = (x ^ fx, y ^ fy, c ^ fc)
            copies.append(pltpu.make_async_remote_copy(
                src_ref=gath.at[me], dst_ref=gath.at[me], send_sem=send_sems.at[k - 1], recv_sem=recv_sems.at[k - 1],
                device_id=peer, device_id_type=MESH))
        for cp in copies:
            cp.start()
        for cp in copies:
            cp.wait_recv()
        for cp in copies:
            cp.wait_send()
        g = gath[0]
        for d in range(1, N_DEV):
            g = g + gath[d]
        go_ref[...] = g

    def adam_body(g_ref, w_ref, m_ref, v_ref, *out_refs):
        g = g_ref[...]
        delta, m_new, v_new = _adam_math(g, w_ref[...], m_ref[...], v_ref[...])
        outs = iter(out_refs)
        for val in (g, delta, m_new, v_new):
            lo = 0
            for r in row_counts:
                next(outs)[...] = val[lo:lo + r]
                lo += r
        next(outs)[...] = g[SMALL_ROWS - SUBLANES:]

    vm = pl.BlockSpec(memory_space=pltpu.VMEM)
    reduce_body, dep_spec, dep_arg = _anchored(reduce_body, 1, dep)
    total = pl.pallas_call(
        reduce_body, in_specs=[vm] + dep_spec, out_specs=vm, out_shape=SDS((SMALL_ROWS, 128), F32),
        scratch_shapes=[pltpu.VMEM((N_DEV, SMALL_ROWS, 128), F32), pltpu.SemaphoreType.DMA((N_DEV - 1,)),
                        pltpu.SemaphoreType.DMA((N_DEV - 1,))],
        name="small_allreduce")(gpart, *dep_arg)
    out_shape = [SDS((r, 128), F32) for _ in range(4) for r in row_counts] + [SDS((SUBLANES, 128), F32)]
    res = pl.pallas_call(adam_body, in_specs=[vm] * 4, out_specs=[vm] * len(out_shape), out_shape=out_shape,
                         name="small_adam")(total, w, m, v)
    k = len(row_counts)
    return [res[i * k:(i + 1) * k] for i in range(4)], res[-1]


BIG = ("w_in", "conv_w", "w_conv_out", "w_attn_out", "w_o", "w_ffn_gate", "w_ffn_up", "w_ffn_down")
LATE_MERGE = ("w_conv_out", "w_attn_out", "w_o")
LATE_FFN = ("w_ffn_gate", "w_ffn_up", "w_ffn_down")
TRANSPOSED = ("w_in", "w_ffn_gate", "w_ffn_up")
COL_SHARDED = ("conv_w", "w_attn_out")
SMALL = ("norm1_g", "gate_b", "conv_b", "conv_ln_g", "conv_ln_b", "norm2_g", "norm_f_g")
WEIGHTS = ("norm1_g", "w_in", "gate_b", "conv_w", "conv_b", "conv_ln_g", "conv_ln_b", "w_conv_out", "w_attn_out", "w_o",
           "norm2_g", "w_ffn_gate", "w_ffn_up", "w_ffn_down", "norm_f_g")


def _shard2d(name, a):
    a = a.reshape(a.shape[-2], a.shape[-1])
    if name in TRANSPOSED:
        a = a.T
    if name == "conv_w":
        a = jnp.pad(a, ((0, CONV_PAD - CONV_K), (0, 0)))
    return a


def _from_shard2d(name, val, shape):
    if name in TRANSPOSED:
        val = val.T
    if name == "conv_w":
        val = val[:CONV_K]
    return val.reshape(shape)


def _gathered_to_full(name, g):
    if name in COL_SHARDED:
        return g.transpose(1, 0, 2).reshape(g.shape[1], N_DEV * g.shape[2])
    return g.reshape(N_DEV * g.shape[1], g.shape[2])


def _full_to_blocks(name, g):
    if name in COL_SHARDED:
        return g.reshape(g.shape[0], N_DEV, g.shape[1] // N_DEV).transpose(1, 0, 2)
    return g.reshape(N_DEV, g.shape[0] // N_DEV, g.shape[1])


def _pack_small(d, last_rows):
    vec = jnp.concatenate([d[n].reshape(-1) for n in SMALL]).reshape(SMALL_ROWS - SUBLANES, 128)
    return jnp.concatenate([vec, last_rows], axis=0)


def kernel(x, norm1_g, w_in, gate_b, conv_w, conv_b, conv_ln_g, conv_ln_b, w_conv_out, w_attn_out, w_o, norm2_g, w_ffn_gate, w_ffn_up, w_ffn_down, norm_f_g, loss_target, m_norm1_g, m_w_in, m_gate_b, m_conv_w, m_conv_b, m_conv_ln_g, m_conv_ln_b, m_w_conv_out, m_w_attn_out, m_w_o, m_norm2_g, m_w_ffn_gate, m_w_ffn_up, m_w_ffn_down, m_norm_f_g, v_norm1_g, v_w_in, v_gate_b, v_conv_w, v_conv_b, v_conv_ln_g, v_conv_ln_b, v_w_conv_out, v_w_attn_out, v_w_o, v_norm2_g, v_w_ffn_gate, v_w_ffn_up, v_w_ffn_down, v_norm_f_g):
    wts = dict(norm1_g=norm1_g, w_in=w_in, gate_b=gate_b, conv_w=conv_w, conv_b=conv_b, conv_ln_g=conv_ln_g,
               conv_ln_b=conv_ln_b, w_conv_out=w_conv_out, w_attn_out=w_attn_out, w_o=w_o, norm2_g=norm2_g,
               w_ffn_gate=w_ffn_gate, w_ffn_up=w_ffn_up, w_ffn_down=w_ffn_down, norm_f_g=norm_f_g)
    mom1 = dict(norm1_g=m_norm1_g, w_in=m_w_in, gate_b=m_gate_b, conv_w=m_conv_w, conv_b=m_conv_b, conv_ln_g=m_conv_ln_g,
                conv_ln_b=m_conv_ln_b, w_conv_out=m_w_conv_out, w_attn_out=m_w_attn_out, w_o=m_w_o, norm2_g=m_norm2_g,
                w_ffn_gate=m_w_ffn_gate, w_ffn_up=m_w_ffn_up, w_ffn_down=m_w_ffn_down, norm_f_g=m_norm_f_g)
    mom2 = dict(norm1_g=v_norm1_g, w_in=v_w_in, gate_b=v_gate_b, conv_w=v_conv_w, conv_b=v_conv_b, conv_ln_g=v_conv_ln_g,
                conv_ln_b=v_conv_ln_b, w_conv_out=v_w_conv_out, w_attn_out=v_w_attn_out, w_o=v_w_o, norm2_g=v_norm2_g,
                w_ffn_gate=v_w_ffn_gate, w_ffn_up=v_w_ffn_up, w_ffn_down=v_w_ffn_down, norm_f_g=v_norm_f_g)

    T = x.shape[0] * x.shape[1]
    x2 = x.reshape(T, D_MODEL)
    t2 = loss_target.reshape(T, D_MODEL)

    me = 4 * lax.axis_index("x") + 2 * lax.axis_index("y") + lax.axis_index("c")
    shards = {n: _shard2d(n, wts[n]) for n in BIG}
    sent = {n: shards[n] if n == "conv_w" else shards[n].astype(BF16) for n in BIG}
    small = {n: wts[n].reshape(1, -1) for n in SMALL}

    stage = {}

    def in_proj():
        w_in_blocks, conv_blocks = _all_gather([sent["w_in"], sent["conv_w"]])
        near = (1,) + OTHER_CHIPS
        stage["merge"] = _send_start("gather", near, "gather_start_merge", [sent[n] for n in LATE_MERGE], dep=w_in_blocks)
        stage["ffn"] = _send_start("gather", near, "gather_start_ffn", [sent[n] for n in LATE_FFN],
                                   dep=stage["merge"]["token"])
        w_in_t = _gathered_to_full("w_in", w_in_blocks)
        h, proj = _in_proj(x2, small["norm1_g"], w_in_t, stage["ffn"]["token"])
        return h, proj, {"w_in": w_in_t, "conv_w": _gathered_to_full("conv_w", conv_blocks)}

    def filled(names, srcs, lands):
        return {n: _gathered_to_full(n, lax.dynamic_update_slice(land, src[None], (me, 0, 0)))
                for n, src, land in zip(names, srcs, lands)}

    def pass_on(group, after):
        stage[group + "_srcs"], lands = _send_wait(stage[group], after, "gather_wait_" + group)
        stage[group + "_forward"] = _send_start("forward", OTHER_CHIPS, "forward_start_" + group, lands=lands)
        return stage[group + "_forward"]["token"]

    def arrived(group, names, after):
        _, lands = _send_wait(stage[group + "_forward"], after, "forward_wait_" + group)
        return filled(names, stage[group + "_srcs"], lands)

    def late_weights(which, after):
        if which == "after_attention":
            return {"token": pass_on("merge", after)}
        if which is LATE_MERGE:
            return {**arrived("merge", LATE_MERGE, after), "token": pass_on("ffn", after)}
        return arrived("ffn", LATE_FFN, after)

    scatters = []
    core = lax.axis_index("c").astype(jnp.int32).reshape(1)

    def emit(names, gw):
        blocks = [_full_to_blocks(n, gw[n]) for n in names]
        if "w_in" in names:
            sums = [_add_pair(g, r, core, "chip_sum_" + n) for n, g, r in zip(names, blocks, _exchange_sibling(blocks))]
            started = _send_start("chip_scatter", OTHER_CHIPS, "scatter_start_" + names[0], sums)
        else:
            started = _send_start("scatter", ALL_PEERS, "scatter_start_" + names[0], blocks)
        scatters.append((names, started))
        return started["token"]

    loss_part, grad_x, gw, gsmall = _local_step(x2, t2, in_proj, small, late_weights, emit)

    grads, deltas, new_m, new_v = {}, {}, {}, {}
    after = grad_x
    for names, started in scatters:
        srcs, lands = _send_wait(started, after, "scatter_wait_" + names[0])
        mine = (me >> 1 if started["mode"] == "chip_scatter" else me).astype(jnp.int32).reshape(1)
        for n, src, land in zip(names, srcs, lands):
            g, d, mo, vo = _sum_adam(land, src, mine, shards[n], _shard2d(n, mom1[n]), _shard2d(n, mom2[n]), "adam_" + n)
            for dst, val in ((grads, g), (deltas, d), (new_m, mo), (new_v, vo)):
                dst[n] = _from_shard2d(n, val, wts[n].shape)
            after = g

    zeros, ones = jnp.zeros((SUBLANES, 128), F32), jnp.ones((SUBLANES, 128), F32)
    row_counts = [wts[n].size // 128 for n in SMALL]
    kinds, loss_rows = _small_allreduce_adam(
        _pack_small(gsmall, jnp.broadcast_to(loss_part, (SUBLANES, 128))), _pack_small(wts, zeros),
        _pack_small(mom1, zeros), _pack_small(mom2, ones), row_counts, after)
    for dst, vals in zip((grads, deltas, new_m, new_v), kinds):
        dst.update({n: val.reshape(wts[n].shape) for n, val in zip(SMALL, vals)})
    loss = loss_rows[0, 0]
    return (loss, grad_x.reshape(x.shape), *[grads[n] for n in WEIGHTS], *[deltas[n] for n in WEIGHTS],
            *[new_m[n] for n in WEIGHTS], *[new_v[n] for n in WEIGHTS])
```

```python
import math

import numpy as np
import jax
import jax.numpy as jnp
from jax import lax
from jax.experimental import pallas as pl
from jax.experimental.pallas import tpu as pltpu

F32 = jnp.float32
BF16 = jnp.bfloat16
SDS = jax.ShapeDtypeStruct
MESH = pl.DeviceIdType.MESH

D_MODEL = 1024
SEQ = 2048
HEAD_DIM = 64
GROUPS = ((128, 1), (512, 4), (2048, 16))
HEADS_PER_GROUP = 8
N_HEADS = 24
ATTN_WIDTH = N_HEADS * HEAD_DIM
ATTN_OUT = HEADS_PER_GROUP * HEAD_DIM
CONV_K = 31
CONV_PAD = 32
D_FF = 2816
IN_WIDTH = 3 * ATTN_WIDTH + 2 * D_MODEL + 2 * D_MODEL
RMS_EPS = 1e-6
LN_EPS = 1e-5
Q_BLOCK = 128
LANES = 128
NEG = -1e30
N_DEV = 8

ADAM_LR = 0.001
ADAM_B1 = 0.9
ADAM_B2 = 0.999
ADAM_EPS = 1e-08
ADAM_WD = 0.01
ADAM_STEP = 10


def _alibi_slope_list(n):
    def pow2(m):
        start = 2.0 ** (-8.0 / m)
        return [start ** (i + 1) for i in range(m)]
    if math.log2(n).is_integer():
        return pow2(n)
    c = 2 ** math.floor(math.log2(n))
    return pow2(c) + _alibi_slope_list(2 * c)[0::2][: n - c]


def _slopes_times_dilation():
    s = np.asarray(sorted(_alibi_slope_list(N_HEADS), reverse=True), dtype=np.float32).reshape(3, HEADS_PER_GROUP)
    r = np.asarray([g[1] for g in GROUPS], dtype=np.float32)[:, None]
    return (s * r).reshape(N_HEADS)


def _sigmoid(x):
    return 0.5 * jnp.tanh(0.5 * x) + 0.5


def _dot(a, b):
    return jnp.dot(a, b, preferred_element_type=F32)


def _dot_nt(a, b):
    return lax.dot_general(a, b, (((1,), (1,)), ((), ())), preferred_element_type=F32)


def _dot_tn(a, b):
    return lax.dot_general(a, b, (((0,), (0,)), ((), ())), preferred_element_type=F32)


def _rowsum(x):
    return jnp.sum(x, axis=0, keepdims=True)


ANY_SPEC = pl.BlockSpec(memory_space=pl.ANY)


def _params(*sem):
    return pltpu.CompilerParams(dimension_semantics=sem)


def _anchored(body, n_in, dep):
    if dep is None:
        return body, [], []

    def wrapped(*refs):
        return body(*refs[:n_in], *refs[n_in + 1:])

    return wrapped, [pl.BlockSpec(memory_space=pl.ANY)], [dep]


IN_TM = 256
IN_CHUNK = 512


def _in_proj(x, g1, w_in_t, dep=None):
    T = x.shape[0]
    tm = IN_TM

    def body(x_ref, g_ref, w_hbm, h_ref, proj_ref, w_vmem, sem):
        @pl.when(pl.program_id(0) == 0)
        def _():
            cp = pltpu.make_async_copy(w_hbm, w_vmem, sem)
            cp.start()
            cp.wait()

        xv = x_ref[...]
        r = lax.rsqrt(jnp.mean(xv * xv, axis=-1, keepdims=True) + RMS_EPS)
        h = (xv * r * g_ref[...]).astype(BF16)
        h_ref[...] = h
        for lo in range(0, IN_WIDTH, IN_CHUNK):
            proj_ref[:, lo:lo + IN_CHUNK] = _dot_nt(h, w_vmem[lo:lo + IN_CHUNK, :])

    row = lambda n: pl.BlockSpec((tm, n), lambda i: (i, 0))
    body, dep_spec, dep_arg = _anchored(body, 3, dep)
    return pl.pallas_call(
        body, grid=(T // tm,),
        in_specs=[row(D_MODEL), pl.BlockSpec((1, D_MODEL), lambda i: (0, 0)), pl.BlockSpec(memory_space=pl.ANY)] + dep_spec,
        out_specs=[row(D_MODEL), row(IN_WIDTH)],
        out_shape=[SDS((T, D_MODEL), BF16), SDS((T, IN_WIDTH), F32)],
        scratch_shapes=[pltpu.VMEM((IN_WIDTH, D_MODEL), BF16), pltpu.SemaphoreType.DMA],
        compiler_params=_params("arbitrary"), name="in_proj")(x, g1, w_in_t, *dep_arg)


def _mm_tn(a, b, out_dtype, name, tn, tt=1024):
    T, K = a.shape
    N = b.shape[1]
    nt = T // tt

    def body(a_ref, b_ref, o_ref, acc):
        t = pl.program_id(1)

        @pl.when(t == 0)
        def _():
            acc[...] = jnp.zeros_like(acc)

        acc[...] += _dot_tn(a_ref[...], b_ref[...])

        @pl.when(t == nt - 1)
        def _():
            o_ref[...] = acc[...].astype(o_ref.dtype)

    return pl.pallas_call(
        body, grid=(N // tn, nt),
        in_specs=[pl.BlockSpec((tt, K), lambda j, t: (t, 0)),
                  pl.BlockSpec((tt, tn), lambda j, t: (t, j))],
        out_specs=pl.BlockSpec((K, tn), lambda j, t: (0, j)),
        out_shape=SDS((K, N), out_dtype),
        scratch_shapes=[pltpu.VMEM((K, tn), F32)],
        compiler_params=_params("parallel", "arbitrary"), name=name)(a, b)


def _gather_classes(src_ref, dst, r, row0=0):
    L = SEQ // r
    for c in range(r):
        dst[row0 + c * L:row0 + (c + 1) * L, :] = src_ref[0, pl.ds(c, L, stride=r), :].astype(dst.dtype)


def _scatter_classes(src, dst, r, row0=0):
    L = SEQ // r
    for c in range(r):
        dst[pl.ds(c, L, stride=r), :] = src[row0 + c * L:row0 + (c + 1) * L, :].astype(dst.dtype)


def _attn_masks(slope_r):
    qi = lax.broadcasted_iota(jnp.int32, (Q_BLOCK, Q_BLOCK), 0)
    kj = lax.broadcasted_iota(jnp.int32, (Q_BLOCK, Q_BLOCK), 1)
    rel = (qi - kj).astype(F32)
    bias_cur = jnp.where(qi >= kj, -slope_r * rel, NEG)
    bias_prev = jnp.where(qi <= kj, -slope_r * (rel + float(Q_BLOCK)), NEG)
    return bias_cur, bias_prev


def _store_biases(bias, sl_ref, g, hp):
    for hh in range(2):
        cur, prev = _attn_masks(sl_ref[g * HEADS_PER_GROUP + 2 * hp + hh])
        rows = slice(hh * Q_BLOCK, (hh + 1) * Q_BLOCK)
        bias[0, rows, 0:Q_BLOCK] = prev
        bias[1, rows, 0:Q_BLOCK] = jnp.full((Q_BLOCK, Q_BLOCK), NEG, F32)
        bias[0, rows, Q_BLOCK:] = cur
        bias[1, rows, Q_BLOCK:] = cur


def _transpose_pairs(src, dst):
    dst[0, :, 0:Q_BLOCK] = jnp.zeros((LANES, Q_BLOCK), dst.dtype)
    nblk = SEQ // Q_BLOCK
    for b in range(nblk):
        t = src[(b + 1) * Q_BLOCK:(b + 2) * Q_BLOCK, :].T
        dst[b, :, Q_BLOCK:] = t
        if b + 1 < nblk:
            dst[b + 1, :, 0:Q_BLOCK] = t


def _stack_heads(t, low):
    z = jnp.zeros_like(t)
    return jnp.concatenate([jnp.where(low, t, z), jnp.where(low, z, t)], axis=0)


def _unstack_heads(t2, low):
    return jnp.where(low, t2[:Q_BLOCK], t2[Q_BLOCK:])


def _unit_offsets(u, nb):
    off = pl.multiple_of(u * Q_BLOCK, Q_BLOCK)
    n = u & (nb - 1)
    c = u >> int(math.log2(nb))
    return off, n == 0, c, n


ATTN_UNROLL = 16


def _attn_fwd(qkv, slopes_r, batch, dep=None):
    nblk = SEQ // Q_BLOCK

    def body(sl_ref, *refs):
        qkv_refs = refs[:9]
        att_ref, lse_ref = refs[9:11]
        qd, kd, vd, kt, opos, lpos, bias = refs[11:]
        hp = pl.program_id(1)
        low = lax.broadcasted_iota(jnp.int32, (Q_BLOCK, LANES), 1) < HEAD_DIM

        for g in range(3):
            r = GROUPS[g][1]
            nb = SEQ // r // Q_BLOCK
            _gather_classes(qkv_refs[3 * g], qd, r)
            kd[0:Q_BLOCK, :] = jnp.zeros((Q_BLOCK, LANES), BF16)
            vd[0:Q_BLOCK, :] = jnp.zeros((Q_BLOCK, LANES), BF16)
            _gather_classes(qkv_refs[3 * g + 1], kd, r, Q_BLOCK)
            _gather_classes(qkv_refs[3 * g + 2], vd, r, Q_BLOCK)
            _transpose_pairs(kd, kt)
            _store_biases(bias, sl_ref, g, hp)

            def unit(u, carry, g=g, r=r, nb=nb):
                off, first, c, n = _unit_offsets(u, nb)
                q2 = _stack_heads(qd[pl.ds(off, Q_BLOCK), :], low)
                s = _dot(q2, kt[u]) * 0.125 + bias[first.astype(jnp.int32)]
                m = jnp.max(s, axis=-1, keepdims=True)
                p = jnp.exp(s - m)
                l = jnp.sum(p, axis=-1, keepdims=True)
                o2 = _dot(p.astype(BF16), vd[pl.ds(off, 2 * Q_BLOCK), :]) * (1.0 / l)
                lse2 = m + jnp.log(l)
                rows = pl.ds(c + n * (Q_BLOCK * r), Q_BLOCK, stride=r)
                opos[g, rows, :] = _unstack_heads(o2, low)
                lpos[g, rows, :] = jnp.where(low, lse2[:Q_BLOCK], lse2[Q_BLOCK:])
                return carry

            lax.fori_loop(0, nblk, unit, 0, unroll=ATTN_UNROLL)

        def merge(i, carry):
            rows = pl.ds(pl.multiple_of(i * 256, 256), 256)
            l0, l1, l2 = lpos[0, rows, :], lpos[1, rows, :], lpos[2, rows, :]
            m = jnp.maximum(jnp.maximum(l0, l1), l2)
            e0, e1, e2 = jnp.exp(l0 - m), jnp.exp(l1 - m), jnp.exp(l2 - m)
            den = e0 + e1 + e2
            att = (e0 * opos[0, rows, :] + e1 * opos[1, rows, :] + e2 * opos[2, rows, :]) / den
            att_ref[0, rows, :] = att.astype(att_ref.dtype)
            lse_ref[0, rows, :] = m + jnp.log(den)
            return carry

        lax.fori_loop(0, SEQ // 256, merge, 0)

    def col(sec, g):
        return pl.BlockSpec((1, SEQ, LANES), lambda b, hp: (b, 0, sec * 12 + g * 4 + hp))

    out = pl.BlockSpec((1, SEQ, LANES), lambda b, hp: (b, 0, hp))
    body, dep_spec, dep_arg = _anchored(body, 10, dep)
    return pl.pallas_call(
        body, grid=(batch, 4),
        in_specs=[pl.BlockSpec(memory_space=pltpu.SMEM)] + [col(sec, g) for g in range(3) for sec in range(3)] + dep_spec,
        out_specs=[out, out],
        out_shape=[SDS((batch, SEQ, ATTN_OUT), BF16), SDS((batch, SEQ, ATTN_OUT), F32)],
        scratch_shapes=[pltpu.VMEM((SEQ, LANES), BF16), pltpu.VMEM((Q_BLOCK + SEQ, LANES), BF16),
                        pltpu.VMEM((Q_BLOCK + SEQ, LANES), BF16), pltpu.VMEM((nblk, LANES, 2 * Q_BLOCK), BF16),
                        pltpu.VMEM((3, SEQ, LANES), F32), pltpu.VMEM((3, SEQ, LANES), F32),
                        pltpu.VMEM((2, 2 * Q_BLOCK, 2 * Q_BLOCK), F32)],
        compiler_params=_params("parallel", "parallel"), name="attn_fwd")(slopes_r, *([qkv] * 9), *dep_arg)


def _attn_bwd(qkv, datt, lse, dsum, slopes_r, batch):
    nblk = SEQ // Q_BLOCK

    def body(sl_ref, q_ref, k_ref, v_ref, do_ref, l_ref, d_ref, dq_ref, dk_ref, dv_ref,
             qd, kd, vd, dod, ld, dd, dq_acc, dk_acc, dv_acc, dk_part, dv_part, stage, bias):
        gid, hp = pl.program_id(1), pl.program_id(2)
        low = lax.broadcasted_iota(jnp.int32, (Q_BLOCK, LANES), 1) < HEAD_DIM

        def section(g):
            r = GROUPS[g][1]
            nb = SEQ // r // Q_BLOCK
            _gather_classes(q_ref, qd, r)
            kd[0:Q_BLOCK, :] = jnp.zeros((Q_BLOCK, LANES), BF16)
            vd[0:Q_BLOCK, :] = jnp.zeros((Q_BLOCK, LANES), BF16)
            _gather_classes(k_ref, kd, r, Q_BLOCK)
            _gather_classes(v_ref, vd, r, Q_BLOCK)
            _gather_classes(do_ref, dod, r)
            _gather_classes(l_ref, ld, r)
            _gather_classes(d_ref, dd, r)
            _store_biases(bias, sl_ref, g, hp)

            def unit(u, carry):
                off, first, _, _ = _unit_offsets(u, nb)
                pair = pl.ds(off, 2 * Q_BLOCK)
                q2 = _stack_heads(qd[pl.ds(off, Q_BLOCK), :], low)
                do2 = _stack_heads(dod[pl.ds(off, Q_BLOCK), :], low)
                lse_t = ld[pl.ds(off, Q_BLOCK), :]
                dsum_t = dd[pl.ds(off, Q_BLOCK), :]
                lse2 = jnp.concatenate([lse_t[:, 0:1], lse_t[:, HEAD_DIM:HEAD_DIM + 1]], axis=0)
                dsum2 = jnp.concatenate([dsum_t[:, 0:1], dsum_t[:, HEAD_DIM:HEAD_DIM + 1]], axis=0)
                s = _dot_nt(q2, kd[pair, :]) * 0.125 + bias[first.astype(jnp.int32)]
                p = jnp.exp(s - lse2)
                ds = (p * (_dot_nt(do2, vd[pair, :]) - dsum2)).astype(BF16)
                dq_acc[pl.ds(off, Q_BLOCK), :] = _unstack_heads(_dot(ds, kd[pair, :]), low) * 0.125
                dk_part[u] = _dot_tn(ds, q2) * 0.125
                dv_part[u] = _dot_tn(p.astype(BF16), do2)
                return carry

            lax.fori_loop(0, nblk, unit, 0, unroll=ATTN_UNROLL)
            for part, acc in ((dk_part, dk_acc), (dv_part, dv_acc)):
                for b in range(nblk):
                    t = part[b, Q_BLOCK:, :]
                    if b + 1 < nblk:
                        t = t + part[b + 1, 0:Q_BLOCK, :]
                    acc[b * Q_BLOCK:(b + 1) * Q_BLOCK, :] = t
            for acc, out_ref in ((dq_acc, dq_ref), (dk_acc, dk_ref), (dv_acc, dv_ref)):
                _scatter_classes(acc, stage, r)
                out_ref[0] = stage[...].astype(out_ref.dtype)

        for g in range(3):
            pl.when(gid == g)(lambda g=g: section(g))

    def col(sec):
        return pl.BlockSpec((1, SEQ, LANES), lambda b, g, hp: (b, 0, sec * 12 + g * 4 + hp))

    pos = pl.BlockSpec((1, SEQ, LANES), lambda b, g, hp: (b, 0, hp))
    dout = pl.BlockSpec((1, SEQ, LANES), lambda b, g, hp: (b, 0, g * 4 + hp))
    out = SDS((batch, SEQ, ATTN_WIDTH), BF16)
    seq_bf = pltpu.VMEM((SEQ, LANES), BF16)
    seq_f = pltpu.VMEM((SEQ, LANES), F32)
    pad_bf = pltpu.VMEM((Q_BLOCK + SEQ, LANES), BF16)
    part = pltpu.VMEM((nblk, 2 * Q_BLOCK, LANES), F32)
    return pl.pallas_call(
        body, grid=(batch, 3, 4),
        in_specs=[pl.BlockSpec(memory_space=pltpu.SMEM), col(0), col(1), col(2), pos, pos, pos],
        out_specs=[dout, dout, dout],
        out_shape=[out, out, out],
        scratch_shapes=[seq_bf, pad_bf, pad_bf, seq_bf, seq_f, seq_f, seq_f, seq_f, seq_f, part, part, seq_f,
                        pltpu.VMEM((2, 2 * Q_BLOCK, 2 * Q_BLOCK), F32)],
        compiler_params=_params("parallel", "parallel", "parallel"), name="attn_bwd")(
            slopes_r, qkv, qkv, qkv, datt, lse, dsum)


CONV_TC = 128
U_BLOCK0 = 3 * ATTN_WIDTH // CONV_TC
CONV_ROWS = 128
SUBLANES = 8


SHIFT_TAIL = CONV_PAD - SUBLANES
CONV_CHUNKS = SEQ // CONV_ROWS


def _fill_shifted_rows(sh, c):
    lo = c * CONV_ROWS + (SHIFT_TAIL if c else 0)
    hi = (c + 1) * CONV_ROWS + SHIFT_TAIL
    for s in range(1, SUBLANES):
        sh[s, lo:hi, :] = sh[0, lo + s:hi + s, :]


def _tap(sh, base, offset):
    s = offset % SUBLANES
    lo = base + offset - s
    return sh[s, lo:lo + CONV_ROWS, :]


def _conv_fwd(u, conv_w, conv_b, batch, dep=None):
    nct = D_MODEL // CONV_TC

    def body(ua_ref, ub_ref, w_ref, b_ref, o_ref, sh):
        sh[0, 0:CONV_PAD, :] = jnp.zeros((CONV_PAD, CONV_TC), F32)
        for c in range(CONV_CHUNKS):
            base = c * CONV_ROWS
            rows = slice(base, base + CONV_ROWS)
            sh[0, CONV_PAD + base:CONV_PAD + base + CONV_ROWS, :] = ua_ref[0, rows, :] * _sigmoid(ub_ref[0, rows, :])
            _fill_shifted_rows(sh, c)
            acc = jnp.broadcast_to(b_ref[...], (CONV_ROWS, CONV_TC))
            for t in range(CONV_K):
                acc = acc + _tap(sh, base, t + CONV_PAD - (CONV_K - 1)) * w_ref[t:t + 1, :]
            o_ref[0, rows, :] = acc

    body, dep_spec, dep_arg = _anchored(body, 4, dep)
    return pl.pallas_call(
        body, grid=(nct, batch),
        in_specs=[pl.BlockSpec((1, SEQ, CONV_TC), lambda j, b: (b, 0, U_BLOCK0 + j)),
                  pl.BlockSpec((1, SEQ, CONV_TC), lambda j, b: (b, 0, U_BLOCK0 + nct + j)),
                  pl.BlockSpec((CONV_PAD, CONV_TC), lambda j, b: (0, j)),
                  pl.BlockSpec((1, CONV_TC), lambda j, b: (0, j))] + dep_spec,
        out_specs=pl.BlockSpec((1, SEQ, CONV_TC), lambda j, b: (b, 0, j)),
        out_shape=SDS((batch, SEQ, D_MODEL), F32),
        scratch_shapes=[pltpu.VMEM((SUBLANES, SEQ + CONV_PAD, CONV_TC), F32)],
        compiler_params=_params("parallel", "parallel"), name="conv_fwd")(u, u, conv_w, conv_b, *dep_arg)


def _conv_bwd(u, dc1, conv_w, batch, dep=None):
    nct = D_MODEL // CONV_TC

    def body(ua_ref, ub_ref, d_ref, w_ref, dua_ref, dub_ref, gw_ref, gb_ref, shc, shd, gacc):
        b = pl.program_id(1)
        shc[0, 0:CONV_PAD, :] = jnp.zeros((CONV_PAD, CONV_TC), F32)
        shd[0, 0:SEQ, :] = d_ref[0]
        shd[0, SEQ:, :] = jnp.zeros((CONV_PAD, CONV_TC), F32)

        @pl.when(b == 0)
        def _():
            gacc[...] = jnp.zeros_like(gacc)
            gb_ref[...] = jnp.zeros_like(gb_ref)

        gb_ref[...] += _rowsum(d_ref[0])

        for c in range(CONV_CHUNKS):
            base = c * CONV_ROWS
            rows = slice(base, base + CONV_ROWS)
            ua = ua_ref[0, rows, :]
            sg = _sigmoid(ub_ref[0, rows, :])
            shc[0, CONV_PAD + base:CONV_PAD + base + CONV_ROWS, :] = ua * sg
            _fill_shifted_rows(shc, c)
            _fill_shifted_rows(shd, c)
            dcur = shd[0, rows, :]
            acc = jnp.zeros((CONV_ROWS, CONV_TC), F32)
            for t in range(CONV_K):
                acc = acc + _tap(shd, base, CONV_K - 1 - t) * w_ref[t:t + 1, :]
                prod = _tap(shc, base, t + CONV_PAD - (CONV_K - 1)) * dcur
                gacc[t] += jnp.sum(prod.reshape(CONV_ROWS // 8, 8, CONV_TC), axis=0)
            dua_ref[0, rows, :] = (acc * sg).astype(dua_ref.dtype)
            dub_ref[0, rows, :] = (acc * ua * sg * (1.0 - sg)).astype(dub_ref.dtype)

        @pl.when(b == batch - 1)
        def _():
            for t in range(CONV_K):
                gw_ref[t:t + 1, :] = jnp.sum(gacc[t], axis=0, keepdims=True)
            gw_ref[CONV_K:CONV_PAD, :] = jnp.zeros((CONV_PAD - CONV_K, CONV_TC), F32)

    du = SDS((batch, SEQ, D_MODEL), BF16)
    body, dep_spec, dep_arg = _anchored(body, 4, dep)
    return pl.pallas_call(
        body, grid=(nct, batch),
        in_specs=[pl.BlockSpec((1, SEQ, CONV_TC), lambda j, b: (b, 0, U_BLOCK0 + j)),
                  pl.BlockSpec((1, SEQ, CONV_TC), lambda j, b: (b, 0, U_BLOCK0 + nct + j)),
                  pl.BlockSpec((1, SEQ, CONV_TC), lambda j, b: (b, 0, j)),
                  pl.BlockSpec((CONV_PAD, CONV_TC), lambda j, b: (0, j))] + dep_spec,
        out_specs=[pl.BlockSpec((1, SEQ, CONV_TC), lambda j, b: (b, 0, j)),
                   pl.BlockSpec((1, SEQ, CONV_TC), lambda j, b: (b, 0, j)),
                   pl.BlockSpec((CONV_PAD, CONV_TC), lambda j, b: (0, j)),
                   pl.BlockSpec((1, CONV_TC), lambda j, b: (0, j))],
        out_shape=[du, du, SDS((CONV_PAD, D_MODEL), F32), SDS((1, D_MODEL), F32)],
        scratch_shapes=[pltpu.VMEM((SUBLANES, SEQ + CONV_PAD, CONV_TC), F32),
                        pltpu.VMEM((SUBLANES, SEQ + CONV_PAD, CONV_TC), F32),
                        pltpu.VMEM((CONV_K, 8, CONV_TC), F32)],
        compiler_params=_params("parallel", "arbitrary"), name="conv_bwd")(u, u, dc1, conv_w, *dep_arg)


MID_TM = 256


def _layernorm_stats(c1):
    mu = jnp.mean(c1, axis=-1, keepdims=True)
    cen = c1 - mu
    rs = lax.rsqrt(jnp.mean(cen * cen, axis=-1, keepdims=True) + LN_EPS)
    return cen * rs, rs


GATE_PARTS = 4
GATE_PART = 2 * D_MODEL // GATE_PARTS
GATE_PART0 = (IN_WIDTH - 2 * D_MODEL) // GATE_PART


def _gate_specs(tm):
    return [pl.BlockSpec((tm, GATE_PART), lambda i, k=k: (i, GATE_PART0 + k)) for k in range(GATE_PARTS)]


def _mid_fwd(att, c1, proj, x, w_a, w_c, w_o, gate_b, ln_g, ln_b, g2, dep=None):
    T = x.shape[0]
    tm = MID_TM

    def body(att_ref, c1_ref, lg0, lg1, lg2, lg3, x_ref, wa_ref, wc_ref, wo_ref, gb_ref, lng_ref, lnb_ref, g2_ref,
             c3_ref, ya_ref, yc_ref, mix_ref, x1_ref, h2_ref):
        logits = jnp.concatenate([lg0[...], lg1[...], lg2[...], lg3[...]], axis=1)
        ya = _dot(att_ref[...], wa_ref[...])
        xh, _ = _layernorm_stats(c1_ref[...])
        c2 = xh * lng_ref[...] + lnb_ref[...]
        c3 = (c2 * _sigmoid(c2)).astype(BF16)
        c3_ref[...] = c3
        yc = _dot(c3, wc_ref[...])
        gates = _sigmoid(logits + gb_ref[...])
        mix = (gates[:, :D_MODEL] * ya + gates[:, D_MODEL:] * yc).astype(BF16)
        ya_ref[...] = ya.astype(BF16)
        yc_ref[...] = yc.astype(BF16)
        mix_ref[...] = mix
        x1 = x_ref[...] + _dot(mix, wo_ref[...])
        x1_ref[...] = x1
        r = lax.rsqrt(jnp.mean(x1 * x1, axis=-1, keepdims=True) + RMS_EPS)
        h2_ref[...] = (x1 * r * g2_ref[...]).astype(BF16)

    row = lambda n: pl.BlockSpec((tm, n), lambda i: (i, 0))
    full = lambda a, b: pl.BlockSpec((a, b), lambda i: (0, 0))
    body, dep_spec, dep_arg = _anchored(body, 10 + GATE_PARTS, dep)
    return pl.pallas_call(
        body, grid=(T // tm,),
        in_specs=[row(ATTN_OUT), row(D_MODEL)] + _gate_specs(tm) + [row(D_MODEL),
                  full(ATTN_OUT, D_MODEL), full(D_MODEL, D_MODEL), full(D_MODEL, D_MODEL),
                  full(1, 2 * D_MODEL), full(1, D_MODEL), full(1, D_MODEL), full(1, D_MODEL)] + dep_spec,
        out_specs=[row(D_MODEL), row(D_MODEL), row(D_MODEL), row(D_MODEL), row(D_MODEL), row(D_MODEL)],
        out_shape=[SDS((T, D_MODEL), BF16), SDS((T, D_MODEL), BF16), SDS((T, D_MODEL), BF16), SDS((T, D_MODEL), BF16),
                   SDS((T, D_MODEL), F32), SDS((T, D_MODEL), BF16)],
        compiler_params=_params("parallel"), name="mid_fwd")(att, c1, *([proj] * GATE_PARTS), x, w_a, w_c, w_o,
                                                             gate_b, ln_g, ln_b, g2, *dep_arg)


def _mid_bwd(dx1b, ya, yc, proj, att, c1, w_a, w_c, w_o, gate_b, ln_g, ln_b, head_ones, dep=None):
    T = dx1b.shape[0]
    tm = MID_TM

    def body(dx_ref, ya_ref, yc_ref, lg0, lg1, lg2, lg3, att_ref, c1_ref, wa_ref, wc_ref, wo_ref, gb_ref, lng_ref,
             lnb_ref, e_ref, dlg_ref, dya_ref, dyc_ref, datt_ref, dsum_ref, dc1_ref, ggb_ref, glg_ref, glb_ref):
        logits = jnp.concatenate([lg0[...], lg1[...], lg2[...], lg3[...]], axis=1)
        @pl.when(pl.program_id(0) == 0)
        def _():
            ggb_ref[...] = jnp.zeros_like(ggb_ref)
            glg_ref[...] = jnp.zeros_like(glg_ref)
            glb_ref[...] = jnp.zeros_like(glb_ref)

        dmix = _dot_nt(dx_ref[...], wo_ref[...])
        gates = _sigmoid(logits + gb_ref[...])
        ga, gc = gates[:, :D_MODEL], gates[:, D_MODEL:]
        dla = dmix * ya_ref[...].astype(F32) * ga * (1.0 - ga)
        dlc = dmix * yc_ref[...].astype(F32) * gc * (1.0 - gc)
        dlg_ref[:, :D_MODEL] = dla.astype(BF16)
        dlg_ref[:, D_MODEL:] = dlc.astype(BF16)
        ggb_ref[:, :D_MODEL] += _rowsum(dla)
        ggb_ref[:, D_MODEL:] += _rowsum(dlc)
        dya = (dmix * ga).astype(BF16)
        dyc = (dmix * gc).astype(BF16)
        dya_ref[...] = dya
        dyc_ref[...] = dyc
        datt = _dot_nt(dya, wa_ref[...])
        datt_ref[...] = datt
        dsum_ref[...] = jnp.dot(datt * att_ref[...].astype(F32), e_ref[...], preferred_element_type=F32,
                                precision=lax.Precision.HIGHEST)
        dc3 = _dot_nt(dyc, wc_ref[...])
        xh, rs = _layernorm_stats(c1_ref[...])
        c2 = xh * lng_ref[...] + lnb_ref[...]
        sg = _sigmoid(c2)
        dc2 = dc3 * (sg * (1.0 + c2 * (1.0 - sg)))
        glg_ref[...] += _rowsum(dc2 * xh)
        glb_ref[...] += _rowsum(dc2)
        dxh = dc2 * lng_ref[...]
        dc1_ref[...] = rs * (dxh - jnp.mean(dxh, axis=-1, keepdims=True) - xh * jnp.mean(dxh * xh, axis=-1, keepdims=True))

    row = lambda n: pl.BlockSpec((tm, n), lambda i: (i, 0))
    full = lambda a, b: pl.BlockSpec((a, b), lambda i: (0, 0))
    body, dep_spec, dep_arg = _anchored(body, 12 + GATE_PARTS, dep)
    return pl.pallas_call(
        body, grid=(T // tm,),
        in_specs=[row(D_MODEL), row(D_MODEL), row(D_MODEL)] + _gate_specs(tm) + [row(ATTN_OUT), row(D_MODEL),
                  full(ATTN_OUT, D_MODEL), full(D_MODEL, D_MODEL), full(D_MODEL, D_MODEL),
                  full(1, 2 * D_MODEL), full(1, D_MODEL), full(1, D_MODEL), full(ATTN_OUT, ATTN_OUT)] + dep_spec,
        out_specs=[row(2 * D_MODEL), row(D_MODEL), row(D_MODEL), row(ATTN_OUT), row(ATTN_OUT), row(D_MODEL),
                   full(1, 2 * D_MODEL), full(1, D_MODEL), full(1, D_MODEL)],
        out_shape=[SDS((T, 2 * D_MODEL), BF16), SDS((T, D_MODEL), BF16), SDS((T, D_MODEL), BF16), SDS((T, ATTN_OUT), F32),
                   SDS((T, ATTN_OUT), F32), SDS((T, D_MODEL), F32),
                   SDS((1, 2 * D_MODEL), F32), SDS((1, D_MODEL), F32), SDS((1, D_MODEL), F32)],
        compiler_params=_params("arbitrary"), name="mid_bwd")(dx1b, ya, yc, *([proj] * GATE_PARTS), att, c1, w_a, w_c, w_o,
                                                               gate_b, ln_g, ln_b, head_ones, *dep_arg)


FFN_TM = 256
FFN_CHUNK = 512
FFN_SUB = tuple((lo, min(lo + FFN_CHUNK, D_FF)) for lo in range(0, D_FF, FFN_CHUNK))


def _rms_bwd(dy_times_g, xh, r):
    return r * (dy_times_g - xh * jnp.mean(dy_times_g * xh, axis=-1, keepdims=True))


def _load_resident(pairs, sems):
    @pl.when(pl.program_id(0) == 0)
    def _():
        copies = [pltpu.make_async_copy(src, dst, sems.at[k]) for k, (src, dst) in enumerate(pairs)]
        for cp in copies:
            cp.start()
        for cp in copies:
            cp.wait()


def _ffn_fwd(h2, x1, target, gf, w_g_t, w_u_t, w_d):
    T = h2.shape[0]
    tm = FFN_TM

    def body(h_ref, x1_ref, t_ref, gf_ref, wg_hbm, wu_hbm, wd_hbm,
             a_ref, b_ref, f_ref, dx2_ref, dx2b_ref, loss_ref, gnf_ref, wg, wu, wd, sems):
        _load_resident(((wg_hbm, wg), (wu_hbm, wu), (wd_hbm, wd)), sems)

        @pl.when(pl.program_id(0) == 0)
        def _():
            loss_ref[...] = jnp.zeros_like(loss_ref)
            gnf_ref[...] = jnp.zeros_like(gnf_ref)

        h = h_ref[...]
        x2 = x1_ref[...]
        for lo, hi in FFN_SUB:
            a = _dot_nt(h, wg[lo:hi, :])
            b = _dot_nt(h, wu[lo:hi, :])
            f = (a * _sigmoid(a) * b).astype(BF16)
            a_ref[:, lo:hi] = a.astype(BF16)
            b_ref[:, lo:hi] = b.astype(BF16)
            f_ref[:, lo:hi] = f
            x2 = x2 + _dot(f, wd[lo:hi, :])

        r = lax.rsqrt(jnp.mean(x2 * x2, axis=-1, keepdims=True) + RMS_EPS)
        xh = x2 * r
        err = xh * gf_ref[...] - t_ref[...]
        loss_ref[...] += (0.5 / D_MODEL) * jnp.sum(err * err)
        dy = err * (1.0 / D_MODEL)
        gnf_ref[...] += _rowsum(dy * xh)
        dx2 = _rms_bwd(dy * gf_ref[...], xh, r)
        dx2_ref[...] = dx2
        dx2b_ref[...] = dx2.astype(BF16)

    row = lambda n: pl.BlockSpec((tm, n), lambda i: (i, 0))
    const = lambda n: pl.BlockSpec((1, n), lambda i: (0, 0))
    wshape = pltpu.VMEM((D_FF, D_MODEL), BF16)
    return pl.pallas_call(
        body, grid=(T // tm,),
        in_specs=[row(D_MODEL), row(D_MODEL), row(D_MODEL), const(D_MODEL), ANY_SPEC, ANY_SPEC, ANY_SPEC],
        out_specs=[row(D_FF), row(D_FF), row(D_FF), row(D_MODEL), row(D_MODEL), const(128), const(D_MODEL)],
        out_shape=[SDS((T, D_FF), BF16), SDS((T, D_FF), BF16), SDS((T, D_FF), BF16), SDS((T, D_MODEL), F32),
                   SDS((T, D_MODEL), BF16), SDS((1, 128), F32), SDS((1, D_MODEL), F32)],
        scratch_shapes=[wshape, wshape, wshape, pltpu.SemaphoreType.DMA((3,))],
        compiler_params=_params("arbitrary"), name="ffn_fwd")(h2, x1, target, gf, w_g_t, w_u_t, w_d)


def _ffn_bwd(dx2b, dx2, a, b, x1, g2, w_g_t, w_u_t, w_d):
    T = dx2.shape[0]
    tm = FFN_TM

    def body(dxb_ref, dx2_ref, a_ref, b_ref, x1_ref, g2_ref, wg_hbm, wu_hbm, wd_hbm,
             da_ref, db_ref, dx1_ref, dx1b_ref, gn2_ref, wg, wu, wd, sems):
        _load_resident(((wg_hbm, wg), (wu_hbm, wu), (wd_hbm, wd)), sems)

        @pl.when(pl.program_id(0) == 0)
        def _():
            gn2_ref[...] = jnp.zeros_like(gn2_ref)

        dxb = dxb_ref[...]
        dh2 = jnp.zeros((tm, D_MODEL), F32)
        for lo, hi in FFN_SUB:
            df = _dot_nt(dxb, wd[lo:hi, :])
            av = a_ref[:, lo:hi].astype(F32)
            bv = b_ref[:, lo:hi].astype(F32)
            sg = _sigmoid(av)
            db = (df * av * sg).astype(BF16)
            da = (df * bv * (sg * (1.0 + av * (1.0 - sg)))).astype(BF16)
            da_ref[:, lo:hi] = da
            db_ref[:, lo:hi] = db
            dh2 = dh2 + _dot(da, wg[lo:hi, :]) + _dot(db, wu[lo:hi, :])

        x1 = x1_ref[...]
        r = lax.rsqrt(jnp.mean(x1 * x1, axis=-1, keepdims=True) + RMS_EPS)
        xh = x1 * r
        gn2_ref[...] += _rowsum(dh2 * xh)
        dx1 = dx2_ref[...] + _rms_bwd(dh2 * g2_ref[...], xh, r)
        dx1_ref[...] = dx1
        dx1b_ref[...] = dx1.astype(BF16)

    row = lambda n: pl.BlockSpec((tm, n), lambda i: (i, 0))
    const = lambda n: pl.BlockSpec((1, n), lambda i: (0, 0))
    wshape = pltpu.VMEM((D_FF, D_MODEL), BF16)
    return pl.pallas_call(
        body, grid=(T // tm,),
        in_specs=[row(D_MODEL), row(D_MODEL), row(D_FF), row(D_FF), row(D_MODEL), const(D_MODEL),
                  ANY_SPEC, ANY_SPEC, ANY_SPEC],
        out_specs=[row(D_FF), row(D_FF), row(D_MODEL), row(D_MODEL), const(D_MODEL)],
        out_shape=[SDS((T, D_FF), BF16), SDS((T, D_FF), BF16), SDS((T, D_MODEL), F32), SDS((T, D_MODEL), BF16),
                   SDS((1, D_MODEL), F32)],
        scratch_shapes=[wshape, wshape, wshape, pltpu.SemaphoreType.DMA((3,))],
        compiler_params=_params("arbitrary"), name="ffn_bwd")(dx2b, dx2, a, b, x1, g2, w_g_t, w_u_t, w_d)


def _in_bwd(pieces, w_in_t, x, dx1, g1, dep=None):
    T = x.shape[0]
    tm = IN_TM
    npc = len(pieces)
    assert sum(p.shape[1] for p in pieces) == IN_WIDTH

    def body(*refs):
        p_refs = refs[:npc]
        w_hbm, x_ref, dx1_ref, g_ref, dx_ref, gn1_ref, w_vmem, sem = refs[npc:]

        @pl.when(pl.program_id(0) == 0)
        def _():
            cp = pltpu.make_async_copy(w_hbm, w_vmem, sem)
            cp.start()
            cp.wait()
            gn1_ref[...] = jnp.zeros_like(gn1_ref)

        dh = jnp.zeros((tm, D_MODEL), F32)
        col = 0
        for p_ref in p_refs:
            for j in range(p_ref.shape[1] // IN_CHUNK):
                dh = dh + _dot(p_ref[:, j * IN_CHUNK:(j + 1) * IN_CHUNK], w_vmem[col:col + IN_CHUNK, :])
                col += IN_CHUNK
        xv = x_ref[...]
        r = lax.rsqrt(jnp.mean(xv * xv, axis=-1, keepdims=True) + RMS_EPS)
        xh = xv * r
        gn1_ref[...] += _rowsum(dh * xh)
        dx_ref[...] = dx1_ref[...] + _rms_bwd(dh * g_ref[...], xh, r)

    row = lambda n: pl.BlockSpec((tm, n), lambda i: (i, 0))
    body, dep_spec, dep_arg = _anchored(body, npc + 4, dep)
    return pl.pallas_call(
        body, grid=(T // tm,),
        in_specs=[row(p.shape[1]) for p in pieces]
        + [pl.BlockSpec(memory_space=pl.ANY), row(D_MODEL), row(D_MODEL), pl.BlockSpec((1, D_MODEL), lambda i: (0, 0))]
        + dep_spec,
        out_specs=[row(D_MODEL), pl.BlockSpec((1, D_MODEL), lambda i: (0, 0))],
        out_shape=[SDS((T, D_MODEL), F32), SDS((1, D_MODEL), F32)],
        scratch_shapes=[pltpu.VMEM((IN_WIDTH, D_MODEL), BF16), pltpu.SemaphoreType.DMA],
        compiler_params=_params("arbitrary"), name="in_bwd")(*pieces, w_in_t, x, dx1, g1, *dep_arg)


def _local_step(x, target, in_proj, small, late_weights=None, emit=None):
    T = x.shape[0]
    batch = T // SEQ
    slopes_r = jnp.asarray(_slopes_times_dilation())
    emit = emit or (lambda names, grads: None)

    h, proj, w = in_proj()
    proj3 = proj.reshape(batch, SEQ, IN_WIDTH)

    att, lse = _attn_fwd(proj3, slopes_r, batch, w.get("token"))
    att = att.reshape(T, ATTN_OUT)
    if late_weights is not None:
        w = {**w, **late_weights("after_attention", att)}

    c1 = _conv_fwd(proj3, w["conv_w"], small["conv_b"], batch, w.get("token")).reshape(T, D_MODEL)
    if late_weights is not None:
        w = {**w, **late_weights(LATE_MERGE, (att, c1))}

    c3, ya, yc, mix, x1, h2 = _mid_fwd(
        att, c1, proj, x, w["w_attn_out"], w["w_conv_out"], w["w_o"],
        small["gate_b"], small["conv_ln_g"], small["conv_ln_b"], small["norm2_g"], w.get("token"))
    if late_weights is not None:
        w = {**w, **late_weights(LATE_FFN, h2)}

    a, b, f, dx2, dx2b, loss, g_normf = _ffn_fwd(h2, x1, target, small["norm_f_g"],
                                                   w["w_ffn_gate"], w["w_ffn_up"], w["w_ffn_down"])

    da, db, dx1, dx1b, g_norm2 = _ffn_bwd(dx2b, dx2, a, b, x1, small["norm2_g"],
                                           w["w_ffn_gate"], w["w_ffn_up"], w["w_ffn_down"])
    gw = {}
    gw["w_ffn_down"] = _mm_tn(f, dx2b, BF16, "gw_ffn_down", tn=1024)
    gw["w_ffn_gate"] = _mm_tn(da, h2, BF16, "gw_ffn_gate", tn=1024)
    gw["w_ffn_up"] = _mm_tn(db, h2, BF16, "gw_ffn_up", tn=1024)
    token = emit(("w_ffn_gate", "w_ffn_up", "w_ffn_down"), gw)

    head_ones = jnp.asarray(np.kron(np.eye(HEADS_PER_GROUP, dtype=np.float32), np.ones((HEAD_DIM, HEAD_DIM), np.float32)))
    dlogits, dya, dyc, datt, dsum, dc1, g_gate_b, g_ln_g, g_ln_b = _mid_bwd(
        dx1b, ya, yc, proj, att, c1, w["w_attn_out"], w["w_conv_out"], w["w_o"],
        small["gate_b"], small["conv_ln_g"], small["conv_ln_b"], head_ones, token)
    gw["w_o"] = _mm_tn(mix, dx1b, BF16, "gw_o", tn=1024)
    gw["w_attn_out"] = _mm_tn(att, dya, BF16, "gw_attn_out", tn=1024)
    gw["w_conv_out"] = _mm_tn(c3, dyc, BF16, "gw_conv_out", tn=1024)
    token = emit(("w_conv_out", "w_attn_out", "w_o"), gw)

    dua, dub, g_conv_w, g_conv_b = _conv_bwd(proj3, dc1.reshape(batch, SEQ, D_MODEL), w["conv_w"], batch, token)

    dq, dk, dv = _attn_bwd(proj3, datt.reshape(batch, SEQ, ATTN_OUT), lse, dsum.reshape(batch, SEQ, ATTN_OUT),
                           slopes_r, batch)
    pieces = [dq.reshape(T, ATTN_WIDTH), dk.reshape(T, ATTN_WIDTH), dv.reshape(T, ATTN_WIDTH),
              dua.reshape(T, D_MODEL), dub.reshape(T, D_MODEL), dlogits]

    names = ("q", "k", "v", "ua", "ub", "gate")
    gw["w_in"] = jnp.concatenate([_mm_tn(p, h, BF16, "gw_in_" + nm, tn=1024) for nm, p in zip(names, pieces)], axis=0)
    gw["conv_w"] = g_conv_w
    token = emit(("w_in", "conv_w"), gw)
    grad_x, g_norm1 = _in_bwd(pieces, w["w_in"], x, dx1, small["norm1_g"], token)

    gsmall = {"norm1_g": g_norm1, "gate_b": g_gate_b, "conv_b": g_conv_b, "conv_ln_g": g_ln_g, "conv_ln_b": g_ln_b,
              "norm2_g": g_norm2, "norm_f_g": g_normf}
    return loss, grad_x, gw, gsmall


ANY = pl.BlockSpec(memory_space=pl.ANY)


def _all_gather(arrs):
    n = len(arrs)

    def body(*refs):
        ins, outs = refs[:n], refs[n:2 * n]
        send_sems, recv_sems, local_sems = refs[2 * n:]
        x, y, c = lax.axis_index("x"), lax.axis_index("y"), lax.axis_index("c")
        me, sibling = (x, y, c), (x, y, 1 - c)
        chips = [(1 - x, y), (x, 1 - y), (1 - x, 1 - y)]

        def copy(a, k, block, to, src=None):
            px, py, pc = block
            dst = outs[a].at[4 * px + 2 * py + pc]
            return pltpu.make_async_remote_copy(
                src_ref=dst if src is None else src, dst_ref=dst,
                send_sem=send_sems.at[a, k], recv_sem=recv_sems.at[a, k], device_id=to, device_id_type=MESH)

        mine = [pltpu.make_async_copy(ins[a], outs[a].at[4 * x + 2 * y + c], local_sems.at[a]) for a in range(n)]
        for cp in mine:
            cp.start()
        first = []
        for j, chip in enumerate(chips):
            first += [copy(a, 1 + j, me, (*chip, c), src=ins[a]) for a in range(n)]
        first += [copy(a, 0, me, sibling, src=ins[a]) for a in range(n)]
        for cp in first:
            cp.start()
        passed = []
        for j, chip in enumerate(chips):
            for a in range(n):
                copy(a, 1 + j, (*chip, c), me).wait_recv()
                cp = copy(a, 4 + j, (*chip, c), sibling)
                cp.start()
                passed.append(cp)
        for a in range(n):
            copy(a, 0, sibling, me).wait_recv()
        for j, chip in enumerate(chips):
            for a in range(n):
                copy(a, 4 + j, (*chip, 1 - c), me).wait_recv()
        for cp in first + passed:
            cp.wait_send()
        for cp in mine:
            cp.wait()

    return pl.pallas_call(
        body, in_specs=[ANY] * n, out_specs=[ANY] * n,
        out_shape=[SDS((N_DEV,) + a.shape, a.dtype) for a in arrs],
        scratch_shapes=[pltpu.SemaphoreType.DMA((n, 7)), pltpu.SemaphoreType.DMA((n, 7)), pltpu.SemaphoreType.DMA((n,))],
        name="all_gather_weights")(*arrs)


HBM =pl.BlockSpec(memory_space=pltpu.HBM)
SEM = pl.BlockSpec(memory_space=pltpu.SEMAPHORE)
ALL_PEERS = tuple(range(1, N_DEV))
OTHER_CHIPS = (2, 4, 6)
SPLIT_EFFECT = pltpu.CompilerParams(has_side_effects=pltpu.SideEffectType.DATAFLOW_SIDE_EFFECTING)


def _exchange_copies(mode, ks, srcs, lands, send_sems, recv_sems):
    x, y, c = lax.axis_index("x"), lax.axis_index("y"), lax.axis_index("c")
    me = 4 * x + 2 * y + c
    send, recv = [], []
    for a in range(len(lands)):
        for i, k in enumerate(ks):
            peer = (x ^ ((k >> 2) & 1), y ^ ((k >> 1) & 1), c ^ (k & 1))
            pidx = 4 * peer[0] + 2 * peer[1] + peer[2]
            if mode == "gather":
                src, to, out_slot, in_slot = srcs[a], peer, me, pidx
            elif mode == "scatter":
                src, to, out_slot, in_slot = srcs[a].at[pidx], peer, me, pidx
            elif mode == "chip_scatter":
                src, to, out_slot, in_slot = srcs[a].at[pidx >> 1], peer, me >> 1, pidx >> 1
            else:
                src, to, out_slot, in_slot = lands[a].at[pidx], (x, y, 1 - c), pidx, pidx ^ 1
            s = a * len(ks) + i
            send.append(pltpu.make_async_remote_copy(
                src_ref=src, dst_ref=lands[a].at[out_slot], send_sem=send_sems.at[s], recv_sem=recv_sems.at[s],
                device_id=to, device_id_type=MESH))
            recv.append(pltpu.make_async_remote_copy(
                src_ref=src, dst_ref=lands[a].at[in_slot], send_sem=send_sems.at[s], recv_sem=recv_sems.at[s],
                device_id=to, device_id_type=MESH))
    return send, recv


def _send_start(mode, ks, name, srcs=(), lands=None, dep=None):
    srcs = list(srcs)
    if lands is None:
        slots = 4 if mode == "chip_scatter" else N_DEV
        lands = [lax.empty((slots,) + (s.shape if mode == "gather" else s.shape[1:]), s.dtype) for s in srcs]
    ns, nl = len(srcs), len(lands)
    nsem = nl * len(ks)

    def body(*refs):
        send, _ = _exchange_copies(mode, ks, refs[:ns], refs[ns:ns + nl], refs[ns + nl], refs[ns + nl + 1])
        for cp in send:
            cp.start()
        token = refs[-1]
        token[...] = jnp.zeros_like(token)

    both = srcs + list(lands)
    body, dep_spec, dep_arg = _anchored(body, ns + nl, dep)
    res = pl.pallas_call(
        body, name=name,
        out_shape=(pltpu.SemaphoreType.DMA((nsem,)), pltpu.SemaphoreType.DMA((nsem,)),
                   *[pltpu.HBM(a.shape, a.dtype) for a in both], SDS((8, 128), F32)),
        in_specs=[HBM] * (ns + nl) + dep_spec,
        out_specs=(SEM, SEM, *([HBM] * (ns + nl)), pl.BlockSpec(memory_space=pltpu.VMEM)),
        input_output_aliases={i: 2 + i for i in range(ns + nl)}, compiler_params=SPLIT_EFFECT,
    )(*[pltpu.with_memory_space_constraint(a, pltpu.HBM) for a in both], *dep_arg)
    return dict(mode=mode, ks=ks, send_sems=res[0], recv_sems=res[1], srcs=res[2:2 + ns], lands=res[2 + ns:2 + ns + nl],
                token=res[-1])


def _send_wait(started, after, name):
    ns, nl = len(started["srcs"]), len(started["lands"])

    def body(*refs):
        send, recv = _exchange_copies(started["mode"], started["ks"], refs[:ns], refs[ns:ns + nl],
                                      refs[ns + nl], refs[ns + nl + 1])
        for cp in send:
            cp.wait_send()
        for cp in recv:
            cp.wait_recv()

    both = list(started["srcs"]) + list(started["lands"])
    after = after if isinstance(after, (tuple, list)) else (after,)
    res = pl.pallas_call(
        body, name=name,
        out_shape=tuple(pltpu.HBM(a.shape, a.dtype) for a in both),
        in_specs=[HBM] * (ns + nl) + [SEM, SEM] + [ANY] * len(after), out_specs=tuple([HBM] * (ns + nl)),
        input_output_aliases={i: i for i in range(ns + nl)}, compiler_params=SPLIT_EFFECT,
    )(*both, started["send_sems"], started["recv_sems"], *after)
    return res[:ns], res[ns:]


def _exchange_sibling(gs):
    n = len(gs)

    def body(*refs):
        ins, outs = refs[:n], refs[n:2 * n]
        send_sems, recv_sems = refs[2 * n:]
        x, y, c = lax.axis_index("x"), lax.axis_index("y"), lax.axis_index("c")
        copies = []
        for a in range(n):
            for j in range(4):
                copies.append(pltpu.make_async_remote_copy(
                    src_ref=ins[a].at[2 * j + (1 - c)], dst_ref=outs[a].at[j],
                    send_sem=send_sems.at[a, j], recv_sem=recv_sems.at[a, j],
                    device_id=(x, y, 1 - c), device_id_type=MESH))
        for cp in copies:
            cp.start()
        for cp in copies:
            cp.wait_recv()
        for cp in copies:
            cp.wait_send()

    return pl.pallas_call(
        body, in_specs=[ANY] * n, out_specs=[ANY] * n,
        out_shape=[SDS((4,) + g.shape[1:], g.dtype) for g in gs],
        scratch_shapes=[pltpu.SemaphoreType.DMA((n, 4)), pltpu.SemaphoreType.DMA((n, 4))],
        name="reduce_scatter_sibling")(*gs)


def _add_pair(g, r1, core, name):
    _, rows, cols = g.shape
    tr = _row_tile(rows, cols, 3 * g.dtype.itemsize)

    def body(c_ref, g_ref, r_ref, o_ref):
        o_ref[...] = (g_ref[...].astype(F32) + r_ref[...].astype(F32)).astype(o_ref.dtype)

    return pl.pallas_call(
        body,
        grid_spec=pltpu.PrefetchScalarGridSpec(
            num_scalar_prefetch=1, grid=(4, rows // tr),
            in_specs=[pl.BlockSpec((1, tr, cols), lambda j, i, c_ref: (2 * j + c_ref[0], i, 0)),
                      pl.BlockSpec((1, tr, cols), lambda j, i, c_ref: (j, i, 0))],
            out_specs=pl.BlockSpec((1, tr, cols), lambda j, i, c_ref: (j, i, 0))),
        out_shape=SDS((4, rows, cols), g.dtype),
        compiler_params=_params("parallel", "parallel"), name=name)(core, g, r1)


def _row_tile(rows, cols, itemsize_total):
    budget = (4 << 20) // max(1, cols * itemsize_total)
    if rows <= budget:
        return rows
    t = rows
    while t > budget and t % 2 == 0 and (t // 2) % 16 == 0:
        t //= 2
    return t


def _adam_math(g, w, m, v):
    m_new = ADAM_B1 * m + (1.0 - ADAM_B1) * g
    v_new = ADAM_B2 * v + (1.0 - ADAM_B2) * (g * g)
    m_hat = m_new / (1.0 - ADAM_B1 ** ADAM_STEP)
    v_hat = v_new / (1.0 - ADAM_B2 ** ADAM_STEP)
    delta = -ADAM_LR * (m_hat / (jnp.sqrt(v_hat) + ADAM_EPS) + ADAM_WD * w)
    return delta, m_new, v_new


def _sum_adam(parts, own, mine, w, m, v, name):
    rows, cols = w.shape
    nparts = parts.shape[0]
    tr = _row_tile(rows, cols, (nparts + 1) * parts.dtype.itemsize + 7 * 4)

    def body(mine_ref, p_ref, own_ref, w_ref, m_ref, v_ref, g_ref, d_ref, mo_ref, vo_ref):
        g = None
        for s in range(nparts):
            part = jnp.where(mine_ref[0] == s, own_ref[0], p_ref[s]).astype(F32)
            g = part if g is None else g + part
        delta, m_new, v_new = _adam_math(g, w_ref[...], m_ref[...], v_ref[...])
        g_ref[...] = g
        d_ref[...] = delta
        mo_ref[...] = m_new
        vo_ref[...] = v_new

    blk = pl.BlockSpec((tr, cols), lambda i, mine_ref: (i, 0))
    out = SDS((rows, cols), F32)
    return pl.pallas_call(
        body,
        grid_spec=pltpu.PrefetchScalarGridSpec(
            num_scalar_prefetch=1, grid=(rows // tr,),
            in_specs=[pl.BlockSpec((nparts, tr, cols), lambda i, mine_ref: (0, i, 0)),
                      pl.BlockSpec((1, tr, cols), lambda i, mine_ref: (mine_ref[0], i, 0)), blk, blk, blk],
            out_specs=[blk, blk, blk, blk]),
        out_shape=[out, out, out, out],
        compiler_params=_params("parallel"), name=name)(mine, parts, own, w, m, v)


SMALL_ROWS = 72


def _small_allreduce_adam(gpart, w, m, v, row_counts, dep=None):
    def reduce_body(g_ref, go_ref, gath, send_sems, recv_sems):
        x, y, c = lax.axis_index("x"), lax.axis_index("y"), lax.axis_index("c")
        me = 4 * x + 2 * y + c
        gath[me] = g_ref[...]
        copies = []
        for k in range(1, N_DEV):
            fx, fy, fc = (k >> 2) & 1, (k >> 1) & 1, k & 1
            peer = (x ^ fx, y ^ fy, c ^ fc)
            copies.append(pltpu.make_async_remote_copy(
                src_ref=gath.at[me], dst_ref=gath.at[me], send_sem=send_sems.at[k - 1], recv_sem=recv_sems.at[k - 1],
                device_id=peer, device_id_type=MESH))
        for cp in copies:
            cp.start()
        for cp in copies:
            cp.wait_recv()
        for cp in copies:
            cp.wait_send()
        g = gath[0]
        for d in range(1, N_DEV):
            g = g + gath[d]
        go_ref[...] = g

    def adam_body(g_ref, w_ref, m_ref, v_ref, *out_refs):
        g = g_ref[...]
        delta, m_new, v_new = _adam_math(g, w_ref[...], m_ref[...], v_ref[...])
        outs = iter(out_refs)
        for val in (g, delta, m_new, v_new):
            lo = 0
            for r in row_counts:
                next(outs)[...] = val[lo:lo + r]
                lo += r
        next(outs)[...] = g[SMALL_ROWS - SUBLANES:]

    vm = pl.BlockSpec(memory_space=pltpu.VMEM)
    reduce_body, dep_spec, dep_arg = _anchored(reduce_body, 1, dep)
    total = pl.pallas_call(
        reduce_body, in_specs=[vm] + dep_spec, out_specs=vm, out_shape=SDS((SMALL_ROWS, 128), F32),
        scratch_shapes=[pltpu.VMEM((N_DEV, SMALL_ROWS, 128), F32), pltpu.SemaphoreType.DMA((N_DEV - 1,)),
                        pltpu.SemaphoreType.DMA((N_DEV - 1,))],
        name="small_allreduce")(gpart, *dep_arg)
    out_shape = [SDS((r, 128), F32) for _ in range(4) for r in row_counts] + [SDS((SUBLANES, 128), F32)]
    res = pl.pallas_call(adam_body, in_specs=[vm] * 4, out_specs=[vm] * len(out_shape), out_shape=out_shape,
                         name="small_adam")(total, w, m, v)
    k = len(row_counts)
    return [res[i * k:(i + 1) * k] for i in range(4)], res[-1]


BIG = ("w_in", "conv_w", "w_conv_out", "w_attn_out", "w_o", "w_ffn_gate", "w_ffn_up", "w_ffn_down")
LATE_MERGE = ("w_conv_out", "w_attn_out", "w_o")
LATE_FFN = ("w_ffn_gate", "w_ffn_up", "w_ffn_down")
TRANSPOSED = ("w_in", "w_ffn_gate", "w_ffn_up")
COL_SHARDED = ("conv_w", "w_attn_out")
SMALL = ("norm1_g", "gate_b", "conv_b", "conv_ln_g", "conv_ln_b", "norm2_g", "norm_f_g")
WEIGHTS = ("norm1_g", "w_in", "gate_b", "conv_w", "conv_b", "conv_ln_g", "conv_ln_b", "w_conv_out", "w_attn_out", "w_o",
           "norm2_g", "w_ffn_gate", "w_ffn_up", "w_ffn_down", "norm_f_g")


def _shard2d(name, a):
    a = a.reshape(a.shape[-2], a.shape[-1])
    if name in TRANSPOSED:
        a = a.T
    if name == "conv_w":
        a = jnp.pad(a, ((0, CONV_PAD - CONV_K), (0, 0)))
    return a


def _from_shard2d(name, val, shape):
    if name in TRANSPOSED:
        val = val.T
    if name == "conv_w":
        val = val[:CONV_K]
    return val.reshape(shape)


def _gathered_to_full(name, g):
    if name in COL_SHARDED:
        return g.transpose(1, 0, 2).reshape(g.shape[1], N_DEV * g.shape[2])
    return g.reshape(N_DEV * g.shape[1], g.shape[2])


def _full_to_blocks(name, g):
    if name in COL_SHARDED:
        return g.reshape(g.shape[0], N_DEV, g.shape[1] // N_DEV).transpose(1, 0, 2)
    return g.reshape(N_DEV, g.shape[0] // N_DEV, g.shape[1])


def _pack_small(d, last_rows):
    vec = jnp.concatenate([d[n].reshape(-1) for n in SMALL]).reshape(SMALL_ROWS - SUBLANES, 128)
    return jnp.concatenate([vec, last_rows], axis=0)


def kernel(x, norm1_g, w_in, gate_b, conv_w, conv_b, conv_ln_g, conv_ln_b, w_conv_out, w_attn_out, w_o, norm2_g, w_ffn_gate, w_ffn_up, w_ffn_down, norm_f_g, loss_target, m_norm1_g, m_w_in, m_gate_b, m_conv_w, m_conv_b, m_conv_ln_g, m_conv_ln_b, m_w_conv_out, m_w_attn_out, m_w_o, m_norm2_g, m_w_ffn_gate, m_w_ffn_up, m_w_ffn_down, m_norm_f_g, v_norm1_g, v_w_in, v_gate_b, v_conv_w, v_conv_b, v_conv_ln_g, v_conv_ln_b, v_w_conv_out, v_w_attn_out, v_w_o, v_norm2_g, v_w_ffn_gate, v_w_ffn_up, v_w_ffn_down, v_norm_f_g):
    wts = dict(norm1_g=norm1_g, w_in=w_in, gate_b=gate_b, conv_w=conv_w, conv_b=conv_b, conv_ln_g=conv_ln_g,
               conv_ln_b=conv_ln_b, w_conv_out=w_conv_out, w_attn_out=w_attn_out, w_o=w_o, norm2_g=norm2_g,
               w_ffn_gate=w_ffn_gate, w_ffn_up=w_ffn_up, w_ffn_down=w_ffn_down, norm_f_g=norm_f_g)
    mom1 = dict(norm1_g=m_norm1_g, w_in=m_w_in, gate_b=m_gate_b, conv_w=m_conv_w, conv_b=m_conv_b, conv_ln_g=m_conv_ln_g,
                conv_ln_b=m_conv_ln_b, w_conv_out=m_w_conv_out, w_attn_out=m_w_attn_out, w_o=m_w_o, norm2_g=m_norm2_g,
                w_ffn_gate=m_w_ffn_gate, w_ffn_up=m_w_ffn_up, w_ffn_down=m_w_ffn_down, norm_f_g=m_norm_f_g)
    mom2 = dict(norm1_g=v_norm1_g, w_in=v_w_in, gate_b=v_gate_b, conv_w=v_conv_w, conv_b=v_conv_b, conv_ln_g=v_conv_ln_g,
                conv_ln_b=v_conv_ln_b, w_conv_out=v_w_conv_out, w_attn_out=v_w_attn_out, w_o=v_w_o, norm2_g=v_norm2_g,
                w_ffn_gate=v_w_ffn_gate, w_ffn_up=v_w_ffn_up, w_ffn_down=v_w_ffn_down, norm_f_g=v_norm_f_g)

    T = x.shape[0] * x.shape[1]
    x2 = x.reshape(T, D_MODEL)
    t2 = loss_target.reshape(T, D_MODEL)

    me = 4 * lax.axis_index("x") + 2 * lax.axis_index("y") + lax.axis_index("c")
    shards = {n: _shard2d(n, wts[n]) for n in BIG}
    sent = {n: shards[n] if n == "conv_w" else shards[n].astype(BF16) for n in BIG}
    small = {n: wts[n].reshape(1, -1) for n in SMALL}

    stage = {}

    def in_proj():
        w_in_blocks, conv_blocks = _all_gather([sent["w_in"], sent["conv_w"]])
        near = (1,) + OTHER_CHIPS
        stage["merge"] = _send_start("gather", near, "gather_start_merge", [sent[n] for n in LATE_MERGE], dep=w_in_blocks)
        stage["ffn"] = _send_start("gather", near, "gather_start_ffn", [sent[n] for n in LATE_FFN],
                                   dep=stage["merge"]["token"])
        w_in_t = _gathered_to_full("w_in", w_in_blocks)
        h, proj = _in_proj(x2, small["norm1_g"], w_in_t, stage["ffn"]["token"])
        return h, proj, {"w_in": w_in_t, "conv_w": _gathered_to_full("conv_w", conv_blocks)}

    def filled(names, srcs, lands):
        return {n: _gathered_to_full(n, lax.dynamic_update_slice(land, src[None], (me, 0, 0)))
                for n, src, land in zip(names, srcs, lands)}

    def pass_on(group, after):
        stage[group + "_srcs"], lands = _send_wait(stage[group], after, "gather_wait_" + group)
        stage[group + "_forward"] = _send_start("forward", OTHER_CHIPS, "forward_start_" + group, lands=lands)
        return stage[group + "_forward"]["token"]

    def arrived(group, names, after):
        _, lands = _send_wait(stage[group + "_forward"], after, "forward_wait_" + group)
        return filled(names, stage[group + "_srcs"], lands)

    def late_weights(which, after):
        if which == "after_attention":
            return {"token": pass_on("merge", after)}
        if which is LATE_MERGE:
            return {**arrived("merge", LATE_MERGE, after), "token": pass_on("ffn", after)}
        return arrived("ffn", LATE_FFN, after)

    scatters = []
    core = lax.axis_index("c").astype(jnp.int32).reshape(1)

    def emit(names, gw):
        blocks = [_full_to_blocks(n, gw[n]) for n in names]
        if "w_in" in names:
            sums = [_add_pair(g, r, core, "chip_sum_" + n) for n, g, r in zip(names, blocks, _exchange_sibling(blocks))]
            started = _send_start("chip_scatter", OTHER_CHIPS, "scatter_start_" + names[0], sums)
        else:
            started = _send_start("scatter", ALL_PEERS, "scatter_start_" + names[0], blocks)
        scatters.append((names, started))
        return started["token"]

    loss_part, grad_x, gw, gsmall = _local_step(x2, t2, in_proj, small, late_weights, emit)

    grads, deltas, new_m, new_v = {}, {}, {}, {}
    after = grad_x
    for names, started in scatters:
        srcs, lands = _send_wait(started, after, "scatter_wait_" + names[0])
        mine = (me >> 1 if started["mode"] == "chip_scatter" else me).astype(jnp.int32).reshape(1)
        for n, src, land in zip(names, srcs, lands):
            g, d, mo, vo = _sum_adam(land, src, mine, shards[n], _shard2d(n, mom1[n]), _shard2d(n, mom2[n]), "adam_" + n)
            for dst, val in ((grads, g), (deltas, d), (new_m, mo), (new_v, vo)):
                dst[n] = _from_shard2d(n, val, wts[n].shape)
            after = g

    zeros, ones = jnp.zeros((SUBLANES, 128), F32), jnp.ones((SUBLANES, 128), F32)
    row_counts = [wts[n].size // 128 for n in SMALL]
    kinds, loss_rows = _small_allreduce_adam(
        _pack_small(gsmall, jnp.broadcast_to(loss_part, (SUBLANES, 128))), _pack_small(wts, zeros),
        _pack_small(mom1, zeros), _pack_small(mom2, ones), row_counts, after)
    for dst, vals in zip((grads, deltas, new_m, new_v), kinds):
        dst.update({n: val.reshape(wts[n].shape) for n, val in zip(SMALL, vals)})
    loss = loss_rows[0, 0]
    return (loss, grad_x.reshape(x.shape), *[grads[n] for n in WEIGHTS], *[deltas[n] for n in WEIGHTS],
            *[new_m[n] for n in WEIGHTS], *[new_v[n] for n in WEIGHTS])
```

```python
import math

import numpy as np
import jax
import jax.numpy as jnp
from jax import lax
from jax.experimental import pallas as pl
from jax.experimental.pallas import tpu as pltpu

F32 = jnp.float32
BF16 = jnp.bfloat16
SDS = jax.ShapeDtypeStruct
MESH = pl.DeviceIdType.MESH

D_MODEL = 1024
SEQ = 2048
HEAD_DIM = 64
GROUPS = ((128, 1), (512, 4), (2048, 16))
HEADS_PER_GROUP = 8
N_HEADS = 24
ATTN_WIDTH = N_HEADS * HEAD_DIM
ATTN_OUT = HEADS_PER_GROUP * HEAD_DIM
CONV_K = 31
CONV_PAD = 32
D_FF = 2816
IN_WIDTH = 3 * ATTN_WIDTH + 2 * D_MODEL + 2 * D_MODEL
RMS_EPS = 1e-6
LN_EPS = 1e-5
Q_BLOCK = 128
LANES = 128
NEG = -1e30
N_DEV = 8

ADAM_LR = 0.001
ADAM_B1 = 0.9
ADAM_B2 = 0.999
ADAM_EPS = 1e-08
ADAM_WD = 0.01
ADAM_STEP = 10


def _alibi_slope_list(n):
    def pow2(m):
        start = 2.0 ** (-8.0 / m)
        return [start ** (i + 1) for i in range(m)]
    if math.log2(n).is_integer():
        return pow2(n)
    c = 2 ** math.floor(math.log2(n))
    return pow2(c) + _alibi_slope_list(2 * c)[0::2][: n - c]


def _slopes_times_dilation():
    s = np.asarray(sorted(_alibi_slope_list(N_HEADS), reverse=True), dtype=np.float32).reshape(3, HEADS_PER_GROUP)
    r = np.asarray([g[1] for g in GROUPS], dtype=np.float32)[:, None]
    return (s * r).reshape(N_HEADS)


def _sigmoid(x):
    return 0.5 * jnp.tanh(0.5 * x) + 0.5


def _dot(a, b):
    return jnp.dot(a, b, preferred_element_type=F32)


def _dot_nt(a, b):
    return lax.dot_general(a, b, (((1,), (1,)), ((), ())), preferred_element_type=F32)


def _dot_tn(a, b):
    return lax.dot_general(a, b, (((0,), (0,)), ((), ())), preferred_element_type=F32)


def _rowsum(x):
    return jnp.sum(x, axis=0, keepdims=True)


ANY_SPEC = pl.BlockSpec(memory_space=pl.ANY)


def _params(*sem):
    return pltpu.CompilerParams(dimension_semantics=sem)


def _anchored(body, n_in, dep):
    if dep is None:
        return body, [], []

    def wrapped(*refs):
        return body(*refs[:n_in], *refs[n_in + 1:])

    return wrapped, [pl.BlockSpec(memory_space=pl.ANY)], [dep]


IN_TM = 256
IN_CHUNK = 512


def _in_proj(x, g1, w_in_t, dep=None):
    T = x.shape[0]
    tm = IN_TM

    def body(x_ref, g_ref, w_hbm, h_ref, proj_ref, w_vmem, sem):
        @pl.when(pl.program_id(0) == 0)
        def _():
            cp = pltpu.make_async_copy(w_hbm, w_vmem, sem)
            cp.start()
            cp.wait()

        xv = x_ref[...]
        r = lax.rsqrt(jnp.mean(xv * xv, axis=-1, keepdims=True) + RMS_EPS)
        h = (xv * r * g_ref[...]).astype(BF16)
        h_ref[...] = h
        for lo in range(0, IN_WIDTH, IN_CHUNK):
            proj_ref[:, lo:lo + IN_CHUNK] = _dot_nt(h, w_vmem[lo:lo + IN_CHUNK, :])

    row = lambda n: pl.BlockSpec((tm, n), lambda i: (i, 0))
    body, dep_spec, dep_arg = _anchored(body, 3, dep)
    return pl.pallas_call(
        body, grid=(T // tm,),
        in_specs=[row(D_MODEL), pl.BlockSpec((1, D_MODEL), lambda i: (0, 0)), pl.BlockSpec(memory_space=pl.ANY)] + dep_spec,
        out_specs=[row(D_MODEL), row(IN_WIDTH)],
        out_shape=[SDS((T, D_MODEL), BF16), SDS((T, IN_WIDTH), F32)],
        scratch_shapes=[pltpu.VMEM((IN_WIDTH, D_MODEL), BF16), pltpu.SemaphoreType.DMA],
        compiler_params=_params("arbitrary"), name="in_proj")(x, g1, w_in_t, *dep_arg)


TN_WIDE_K = 2048


def _mm_tn(a, b, out_dtype, name, tn, tt=1024):
    T, K = a.shape
    N = b.shape[1]
    if K <= TN_WIDE_K:
        tt = 2 * tt
    nt = T // tt

    def body(a_ref, b_ref, o_ref, acc):
        t = pl.program_id(1)

        @pl.when(t == 0)
        def _():
            acc[...] = jnp.zeros_like(acc)

        acc[...] += _dot_tn(a_ref[...], b_ref[...])

        @pl.when(t == nt - 1)
        def _():
            o_ref[...] = acc[...].astype(o_ref.dtype)

    return pl.pallas_call(
        body, grid=(N // tn, nt),
        in_specs=[pl.BlockSpec((tt, K), lambda j, t: (t, 0)),
                  pl.BlockSpec((tt, tn), lambda j, t: (t, j))],
        out_specs=pl.BlockSpec((K, tn), lambda j, t: (0, j)),
        out_shape=SDS((K, N), out_dtype),
        scratch_shapes=[pltpu.VMEM((K, tn), F32)],
        compiler_params=_params("parallel", "arbitrary"), name=name)(a, b)


def _gather_classes(src_ref, dst, r, row0=0):
    L = SEQ // r
    for c in range(r):
        dst[row0 + c * L:row0 + (c + 1) * L, :] = src_ref[0, pl.ds(c, L, stride=r), :].astype(dst.dtype)


def _scatter_classes(src, dst, r, row0=0):
    L = SEQ // r
    for c in range(r):
        dst[pl.ds(c, L, stride=r), :] = src[row0 + c * L:row0 + (c + 1) * L, :].astype(dst.dtype)


def _attn_masks(slope_r):
    qi = lax.broadcasted_iota(jnp.int32, (Q_BLOCK, Q_BLOCK), 0)
    kj = lax.broadcasted_iota(jnp.int32, (Q_BLOCK, Q_BLOCK), 1)
    rel = (qi - kj).astype(F32)
    bias_cur = jnp.where(qi >= kj, -slope_r * rel, NEG)
    bias_prev = jnp.where(qi <= kj, -slope_r * (rel + float(Q_BLOCK)), NEG)
    return bias_cur, bias_prev


def _store_biases(bias, sl_ref, g, hp):
    for hh in range(2):
        cur, prev = _attn_masks(sl_ref[g * HEADS_PER_GROUP + 2 * hp + hh])
        rows = slice(hh * Q_BLOCK, (hh + 1) * Q_BLOCK)
        bias[0, rows, 0:Q_BLOCK] = prev
        bias[1, rows, 0:Q_BLOCK] = jnp.full((Q_BLOCK, Q_BLOCK), NEG, F32)
        bias[0, rows, Q_BLOCK:] = cur
        bias[1, rows, Q_BLOCK:] = cur


def _transpose_pairs(src, dst):
    dst[0, :, 0:Q_BLOCK] = jnp.zeros((LANES, Q_BLOCK), dst.dtype)
    nblk = SEQ // Q_BLOCK
    for b in range(nblk):
        t = src[(b + 1) * Q_BLOCK:(b + 2) * Q_BLOCK, :].T
        dst[b, :, Q_BLOCK:] = t
        if b + 1 < nblk:
            dst[b + 1, :, 0:Q_BLOCK] = t


def _stack_heads(t, low):
    z = jnp.zeros_like(t)
    return jnp.concatenate([jnp.where(low, t, z), jnp.where(low, z, t)], axis=0)


def _unstack_heads(t2, low):
    return jnp.where(low, t2[:Q_BLOCK], t2[Q_BLOCK:])


def _unit_offsets(u, nb):
    off = pl.multiple_of(u * Q_BLOCK, Q_BLOCK)
    n = u & (nb - 1)
    c = u >> int(math.log2(nb))
    return off, n == 0, c, n


ATTN_UNROLL = 16


def _attn_fwd(qkv, slopes_r, batch, dep=None):
    nblk = SEQ // Q_BLOCK

    def body(sl_ref, *refs):
        qkv_refs = refs[:9]
        att_ref, lse_ref = refs[9:11]
        qd, kd, vd, kt, opos, lpos, bias = refs[11:]
        hp = pl.program_id(1)
        low = lax.broadcasted_iota(jnp.int32, (Q_BLOCK, LANES), 1) < HEAD_DIM

        for g in range(3):
            r = GROUPS[g][1]
            nb = SEQ // r // Q_BLOCK
            _gather_classes(qkv_refs[3 * g], qd, r)
            kd[0:Q_BLOCK, :] = jnp.zeros((Q_BLOCK, LANES), BF16)
            vd[0:Q_BLOCK, :] = jnp.zeros((Q_BLOCK, LANES), BF16)
            _gather_classes(qkv_refs[3 * g + 1], kd, r, Q_BLOCK)
            _gather_classes(qkv_refs[3 * g + 2], vd, r, Q_BLOCK)
            _transpose_pairs(kd, kt)
            _store_biases(bias, sl_ref, g, hp)

            def unit(u, carry, g=g, r=r, nb=nb):
                off, first, c, n = _unit_offsets(u, nb)
                q2 = _stack_heads(qd[pl.ds(off, Q_BLOCK), :], low)
                s = _dot(q2, kt[u]) * 0.125 + bias[first.astype(jnp.int32)]
                m = jnp.max(s, axis=-1, keepdims=True)
                p = jnp.exp(s - m)
                l = jnp.sum(p, axis=-1, keepdims=True)
                o2 = _dot(p.astype(BF16), vd[pl.ds(off, 2 * Q_BLOCK), :]) * (1.0 / l)
                lse2 = m + jnp.log(l)
                rows = pl.ds(c + n * (Q_BLOCK * r), Q_BLOCK, stride=r)
                opos[g, rows, :] = _unstack_heads(o2, low)
                lpos[g, rows, :] = jnp.where(low, lse2[:Q_BLOCK], lse2[Q_BLOCK:])
                return carry

            lax.fori_loop(0, nblk, unit, 0, unroll=ATTN_UNROLL)

        def merge(i, carry):
            rows = pl.ds(pl.multiple_of(i * 256, 256), 256)
            l0, l1, l2 = lpos[0, rows, :], lpos[1, rows, :], lpos[2, rows, :]
            m = jnp.maximum(jnp.maximum(l0, l1), l2)
            e0, e1, e2 = jnp.exp(l0 - m), jnp.exp(l1 - m), jnp.exp(l2 - m)
            den = e0 + e1 + e2
            att = (e0 * opos[0, rows, :] + e1 * opos[1, rows, :] + e2 * opos[2, rows, :]) / den
            att_ref[0, rows, :] = att.astype(att_ref.dtype)
            lse_ref[0, rows, :] = m + jnp.log(den)
            return carry

        lax.fori_loop(0, SEQ // 256, merge, 0)

    def col(sec, g):
        return pl.BlockSpec((1, SEQ, LANES), lambda b, hp: (b, 0, sec * 12 + g * 4 + hp))

    out = pl.BlockSpec((1, SEQ, LANES), lambda b, hp: (b, 0, hp))
    body, dep_spec, dep_arg = _anchored(body, 10, dep)
    return pl.pallas_call(
        body, grid=(batch, 4),
        in_specs=[pl.BlockSpec(memory_space=pltpu.SMEM)] + [col(sec, g) for g in range(3) for sec in range(3)] + dep_spec,
        out_specs=[out, out],
        out_shape=[SDS((batch, SEQ, ATTN_OUT), BF16), SDS((batch, SEQ, ATTN_OUT), F32)],
        scratch_shapes=[pltpu.VMEM((SEQ, LANES), BF16), pltpu.VMEM((Q_BLOCK + SEQ, LANES), BF16),
                        pltpu.VMEM((Q_BLOCK + SEQ, LANES), BF16), pltpu.VMEM((nblk, LANES, 2 * Q_BLOCK), BF16),
                        pltpu.VMEM((3, SEQ, LANES), F32), pltpu.VMEM((3, SEQ, LANES), F32),
                        pltpu.VMEM((2, 2 * Q_BLOCK, 2 * Q_BLOCK), F32)],
        compiler_params=_params("parallel", "parallel"), name="attn_fwd")(slopes_r, *([qkv] * 9), *dep_arg)


def _attn_bwd(qkv, datt, lse, dsum, slopes_r, batch):
    nblk = SEQ // Q_BLOCK

    def body(sl_ref, q_ref, k_ref, v_ref, do_ref, l_ref, d_ref, dq_ref, dk_ref, dv_ref,
             qd, kd, vd, dod, ld, dd, dq_acc, dk_acc, dv_acc, dk_part, dv_part, stage, bias):
        gid, hp = pl.program_id(1), pl.program_id(2)
        low = lax.broadcasted_iota(jnp.int32, (Q_BLOCK, LANES), 1) < HEAD_DIM

        def section(g):
            r = GROUPS[g][1]
            nb = SEQ // r // Q_BLOCK
            _gather_classes(q_ref, qd, r)
            kd[0:Q_BLOCK, :] = jnp.zeros((Q_BLOCK, LANES), BF16)
            vd[0:Q_BLOCK, :] = jnp.zeros((Q_BLOCK, LANES), BF16)
            _gather_classes(k_ref, kd, r, Q_BLOCK)
            _gather_classes(v_ref, vd, r, Q_BLOCK)
            _gather_classes(do_ref, dod, r)
            _gather_classes(l_ref, ld, r)
            _gather_classes(d_ref, dd, r)
            _store_biases(bias, sl_ref, g, hp)

            def unit(u, carry):
                off, first, _, _ = _unit_offsets(u, nb)
                pair = pl.ds(off, 2 * Q_BLOCK)
                q2 = _stack_heads(qd[pl.ds(off, Q_BLOCK), :], low)
                do2 = _stack_heads(dod[pl.ds(off, Q_BLOCK), :], low)
                lse_t = ld[pl.ds(off, Q_BLOCK), :]
                dsum_t = dd[pl.ds(off, Q_BLOCK), :]
                lse2 = jnp.concatenate([lse_t[:, 0:1], lse_t[:, HEAD_DIM:HEAD_DIM + 1]], axis=0)
                dsum2 = jnp.concatenate([dsum_t[:, 0:1], dsum_t[:, HEAD_DIM:HEAD_DIM + 1]], axis=0)
                s = _dot_nt(q2, kd[pair, :]) * 0.125 + bias[first.astype(jnp.int32)]
                p = jnp.exp(s - lse2)
                ds = (p * (_dot_nt(do2, vd[pair, :]) - dsum2)).astype(BF16)
                dq_acc[pl.ds(off, Q_BLOCK), :] = _unstack_heads(_dot(ds, kd[pair, :]), low) * 0.125
                dk_part[u] = _dot_tn(ds, q2) * 0.125
                dv_part[u] = _dot_tn(p.astype(BF16), do2)
                return carry

            lax.fori_loop(0, nblk, unit, 0, unroll=ATTN_UNROLL)
            for part, acc in ((dk_part, dk_acc), (dv_part, dv_acc)):
                for b in range(nblk):
                    t = part[b, Q_BLOCK:, :]
                    if b + 1 < nblk:
                        t = t + part[b + 1, 0:Q_BLOCK, :]
                    acc[b * Q_BLOCK:(b + 1) * Q_BLOCK, :] = t
            for acc, out_ref in ((dq_acc, dq_ref), (dk_acc, dk_ref), (dv_acc, dv_ref)):
                _scatter_classes(acc, stage, r)
                out_ref[0] = stage[...].astype(out_ref.dtype)

        for g in range(3):
            pl.when(gid == g)(lambda g=g: section(g))

    def col(sec):
        return pl.BlockSpec((1, SEQ, LANES), lambda b, g, hp: (b, 0, sec * 12 + g * 4 + hp))

    pos = pl.BlockSpec((1, SEQ, LANES), lambda b, g, hp: (b, 0, hp))
    dout = pl.BlockSpec((1, SEQ, LANES), lambda b, g, hp: (b, 0, g * 4 + hp))
    out = SDS((batch, SEQ, ATTN_WIDTH), BF16)
    seq_bf = pltpu.VMEM((SEQ, LANES), BF16)
    seq_f = pltpu.VMEM((SEQ, LANES), F32)
    pad_bf = pltpu.VMEM((Q_BLOCK + SEQ, LANES), BF16)
    part = pltpu.VMEM((nblk, 2 * Q_BLOCK, LANES), F32)
    return pl.pallas_call(
        body, grid=(batch, 3, 4),
        in_specs=[pl.BlockSpec(memory_space=pltpu.SMEM), col(0), col(1), col(2), pos, pos, pos],
        out_specs=[dout, dout, dout],
        out_shape=[out, out, out],
        scratch_shapes=[seq_bf, pad_bf, pad_bf, seq_bf, seq_f, seq_f, seq_f, seq_f, seq_f, part, part, seq_f,
                        pltpu.VMEM((2, 2 * Q_BLOCK, 2 * Q_BLOCK), F32)],
        compiler_params=_params("parallel", "parallel", "parallel"), name="attn_bwd")(
            slopes_r, qkv, qkv, qkv, datt, lse, dsum)


CONV_TC = 128
U_BLOCK0 = 3 * ATTN_WIDTH // CONV_TC
CONV_ROWS = 128
SUBLANES = 8


SHIFT_TAIL = CONV_PAD - SUBLANES
CONV_CHUNKS = SEQ // CONV_ROWS


def _fill_shifted_rows(sh, c):
    lo = c * CONV_ROWS + (SHIFT_TAIL if c else 0)
    hi = (c + 1) * CONV_ROWS + SHIFT_TAIL
    for s in range(1, SUBLANES):
        sh[s, lo:hi, :] = sh[0, lo + s:hi + s, :]


def _tap(sh, base, offset):
    s = offset % SUBLANES
    lo = base + offset - s
    return sh[s, lo:lo + CONV_ROWS, :]


def _conv_fwd(u, conv_w, conv_b, batch, dep=None):
    nct = D_MODEL // CONV_TC

    def body(ua_ref, ub_ref, w_ref, b_ref, o_ref, sh):
        sh[0, 0:CONV_PAD, :] = jnp.zeros((CONV_PAD, CONV_TC), F32)
        for c in range(CONV_CHUNKS):
            base = c * CONV_ROWS
            rows = slice(base, base + CONV_ROWS)
            sh[0, CONV_PAD + base:CONV_PAD + base + CONV_ROWS, :] = ua_ref[0, rows, :] * _sigmoid(ub_ref[0, rows, :])
            _fill_shifted_rows(sh, c)
            acc = jnp.broadcast_to(b_ref[...], (CONV_ROWS, CONV_TC))
            for t in range(CONV_K):
                acc = acc + _tap(sh, base, t + CONV_PAD - (CONV_K - 1)) * w_ref[t:t + 1, :]
            o_ref[0, rows, :] = acc

    body, dep_spec, dep_arg = _anchored(body, 4, dep)
    return pl.pallas_call(
        body, grid=(nct, batch),
        in_specs=[pl.BlockSpec((1, SEQ, CONV_TC), lambda j, b: (b, 0, U_BLOCK0 + j)),
                  pl.BlockSpec((1, SEQ, CONV_TC), lambda j, b: (b, 0, U_BLOCK0 + nct + j)),
                  pl.BlockSpec((CONV_PAD, CONV_TC), lambda j, b: (0, j)),
                  pl.BlockSpec((1, CONV_TC), lambda j, b: (0, j))] + dep_spec,
        out_specs=pl.BlockSpec((1, SEQ, CONV_TC), lambda j, b: (b, 0, j)),
        out_shape=SDS((batch, SEQ, D_MODEL), F32),
        scratch_shapes=[pltpu.VMEM((SUBLANES, SEQ + CONV_PAD, CONV_TC), F32)],
        compiler_params=_params("parallel", "parallel"), name="conv_fwd")(u, u, conv_w, conv_b, *dep_arg)


def _conv_bwd(u, dc1, conv_w, batch, dep=None):
    nct = D_MODEL // CONV_TC

    def body(ua_ref, ub_ref, d_ref, w_ref, dua_ref, dub_ref, gw_ref, gb_ref, shc, shd, gacc):
        b = pl.program_id(1)
        shc[0, 0:CONV_PAD, :] = jnp.zeros((CONV_PAD, CONV_TC), F32)
        shd[0, 0:SEQ, :] = d_ref[0]
        shd[0, SEQ:, :] = jnp.zeros((CONV_PAD, CONV_TC), F32)

        @pl.when(b == 0)
        def _():
            gacc[...] = jnp.zeros_like(gacc)
            gb_ref[...] = jnp.zeros_like(gb_ref)

        gb_ref[...] += _rowsum(d_ref[0])

        for c in range(CONV_CHUNKS):
            base = c * CONV_ROWS
            rows = slice(base, base + CONV_ROWS)
            ua = ua_ref[0, rows, :]
            sg = _sigmoid(ub_ref[0, rows, :])
            shc[0, CONV_PAD + base:CONV_PAD + base + CONV_ROWS, :] = ua * sg
            _fill_shifted_rows(shc, c)
            _fill_shifted_rows(shd, c)
            dcur = shd[0, rows, :]
            acc = jnp.zeros((CONV_ROWS, CONV_TC), F32)
            for t in range(CONV_K):
                acc = acc + _tap(shd, base, CONV_K - 1 - t) * w_ref[t:t + 1, :]
                prod = _tap(shc, base, t + CONV_PAD - (CONV_K - 1)) * dcur
                gacc[t] += jnp.sum(prod.reshape(CONV_ROWS // 8, 8, CONV_TC), axis=0)
            dua_ref[0, rows, :] = (acc * sg).astype(dua_ref.dtype)
            dub_ref[0, rows, :] = (acc * ua * sg * (1.0 - sg)).astype(dub_ref.dtype)

        @pl.when(b == batch - 1)
        def _():
            for t in range(CONV_K):
                gw_ref[t:t + 1, :] = jnp.sum(gacc[t], axis=0, keepdims=True)
            gw_ref[CONV_K:CONV_PAD, :] = jnp.zeros((CONV_PAD - CONV_K, CONV_TC), F32)

    du = SDS((batch, SEQ, D_MODEL), BF16)
    body, dep_spec, dep_arg = _anchored(body, 4, dep)
    return pl.pallas_call(
        body, grid=(nct, batch),
        in_specs=[pl.BlockSpec((1, SEQ, CONV_TC), lambda j, b: (b, 0, U_BLOCK0 + j)),
                  pl.BlockSpec((1, SEQ, CONV_TC), lambda j, b: (b, 0, U_BLOCK0 + nct + j)),
                  pl.BlockSpec((1, SEQ, CONV_TC), lambda j, b: (b, 0, j)),
                  pl.BlockSpec((CONV_PAD, CONV_TC), lambda j, b: (0, j))] + dep_spec,
        out_specs=[pl.BlockSpec((1, SEQ, CONV_TC), lambda j, b: (b, 0, j)),
                   pl.BlockSpec((1, SEQ, CONV_TC), lambda j, b: (b, 0, j)),
                   pl.BlockSpec((CONV_PAD, CONV_TC), lambda j, b: (0, j)),
                   pl.BlockSpec((1, CONV_TC), lambda j, b: (0, j))],
        out_shape=[du, du, SDS((CONV_PAD, D_MODEL), F32), SDS((1, D_MODEL), F32)],
        scratch_shapes=[pltpu.VMEM((SUBLANES, SEQ + CONV_PAD, CONV_TC), F32),
                        pltpu.VMEM((SUBLANES, SEQ + CONV_PAD, CONV_TC), F32),
                        pltpu.VMEM((CONV_K, 8, CONV_TC), F32)],
        compiler_params=_params("parallel", "arbitrary"), name="conv_bwd")(u, u, dc1, conv_w, *dep_arg)


MID_TM = 256


def _layernorm_stats(c1):
    mu = jnp.mean(c1, axis=-1, keepdims=True)
    cen = c1 - mu
    rs = lax.rsqrt(jnp.mean(cen * cen, axis=-1, keepdims=True) + LN_EPS)
    return cen * rs, rs


GATE_PARTS = 4
GATE_PART = 2 * D_MODEL // GATE_PARTS
GATE_PART0 = (IN_WIDTH - 2 * D_MODEL) // GATE_PART


def _gate_specs(tm):
    return [pl.BlockSpec((tm, GATE_PART), lambda i, k=k: (i, GATE_PART0 + k)) for k in range(GATE_PARTS)]


def _mid_fwd(att, c1, proj, x, w_a, w_c, w_o, gate_b, ln_g, ln_b, g2, dep=None):
    T = x.shape[0]
    tm = MID_TM

    def body(att_ref, c1_ref, lg0, lg1, lg2, lg3, x_ref, wa_ref, wc_ref, wo_ref, gb_ref, lng_ref, lnb_ref, g2_ref,
             c3_ref, ya_ref, yc_ref, mix_ref, x1_ref, h2_ref):
        logits = jnp.concatenate([lg0[...], lg1[...], lg2[...], lg3[...]], axis=1)
        ya = _dot(att_ref[...], wa_ref[...])
        xh, _ = _layernorm_stats(c1_ref[...])
        c2 = xh * lng_ref[...] + lnb_ref[...]
        c3 = (c2 * _sigmoid(c2)).astype(BF16)
        c3_ref[...] = c3
        yc = _dot(c3, wc_ref[...])
        gates = _sigmoid(logits + gb_ref[...])
        mix = (gates[:, :D_MODEL] * ya + gates[:, D_MODEL:] * yc).astype(BF16)
        ya_ref[...] = ya.astype(BF16)
        yc_ref[...] = yc.astype(BF16)
        mix_ref[...] = mix
        x1 = x_ref[...] + _dot(mix, wo_ref[...])
        x1_ref[...] = x1
        r = lax.rsqrt(jnp.mean(x1 * x1, axis=-1, keepdims=True) + RMS_EPS)
        h2_ref[...] = (x1 * r * g2_ref[...]).astype(BF16)

    row = lambda n: pl.BlockSpec((tm, n), lambda i: (i, 0))
    full = lambda a, b: pl.BlockSpec((a, b), lambda i: (0, 0))
    body, dep_spec, dep_arg = _anchored(body, 10 + GATE_PARTS, dep)
    return pl.pallas_call(
        body, grid=(T // tm,),
        in_specs=[row(ATTN_OUT), row(D_MODEL)] + _gate_specs(tm) + [row(D_MODEL),
                  full(ATTN_OUT, D_MODEL), full(D_MODEL, D_MODEL), full(D_MODEL, D_MODEL),
                  full(1, 2 * D_MODEL), full(1, D_MODEL), full(1, D_MODEL), full(1, D_MODEL)] + dep_spec,
        out_specs=[row(D_MODEL), row(D_MODEL), row(D_MODEL), row(D_MODEL), row(D_MODEL), row(D_MODEL)],
        out_shape=[SDS((T, D_MODEL), BF16), SDS((T, D_MODEL), BF16), SDS((T, D_MODEL), BF16), SDS((T, D_MODEL), BF16),
                   SDS((T, D_MODEL), F32), SDS((T, D_MODEL), BF16)],
        compiler_params=_params("parallel"), name="mid_fwd")(att, c1, *([proj] * GATE_PARTS), x, w_a, w_c, w_o,
                                                             gate_b, ln_g, ln_b, g2, *dep_arg)


def _mid_bwd(dx1b, ya, yc, proj, att, c1, w_a, w_c, w_o, gate_b, ln_g, ln_b, head_ones, dep=None):
    T = dx1b.shape[0]
    tm = MID_TM

    def body(dx_ref, ya_ref, yc_ref, lg0, lg1, lg2, lg3, att_ref, c1_ref, wa_ref, wc_ref, wo_ref, gb_ref, lng_ref,
             lnb_ref, e_ref, dlg_ref, dya_ref, dyc_ref, datt_ref, dsum_ref, dc1_ref, ggb_ref, glg_ref, glb_ref):
        logits = jnp.concatenate([lg0[...], lg1[...], lg2[...], lg3[...]], axis=1)
        @pl.when(pl.program_id(0) == 0)
        def _():
            ggb_ref[...] = jnp.zeros_like(ggb_ref)
            glg_ref[...] = jnp.zeros_like(glg_ref)
            glb_ref[...] = jnp.zeros_like(glb_ref)

        dmix = _dot_nt(dx_ref[...], wo_ref[...])
        gates = _sigmoid(logits + gb_ref[...])
        ga, gc = gates[:, :D_MODEL], gates[:, D_MODEL:]
        dla = dmix * ya_ref[...].astype(F32) * ga * (1.0 - ga)
        dlc = dmix * yc_ref[...].astype(F32) * gc * (1.0 - gc)
        dlg_ref[:, :D_MODEL] = dla.astype(BF16)
        dlg_ref[:, D_MODEL:] = dlc.astype(BF16)
        ggb_ref[:, :D_MODEL] += _rowsum(dla)
        ggb_ref[:, D_MODEL:] += _rowsum(dlc)
        dya = (dmix * ga).astype(BF16)
        dyc = (dmix * gc).astype(BF16)
        dya_ref[...] = dya
        dyc_ref[...] = dyc
        datt = _dot_nt(dya, wa_ref[...])
        datt_ref[...] = datt
        dsum_ref[...] = jnp.dot(datt * att_ref[...].astype(F32), e_ref[...], preferred_element_type=F32,
                                precision=lax.Precision.HIGHEST)
        dc3 = _dot_nt(dyc, wc_ref[...])
        xh, rs = _layernorm_stats(c1_ref[...])
        c2 = xh * lng_ref[...] + lnb_ref[...]
        sg = _sigmoid(c2)
        dc2 = dc3 * (sg * (1.0 + c2 * (1.0 - sg)))
        glg_ref[...] += _rowsum(dc2 * xh)
        glb_ref[...] += _rowsum(dc2)
        dxh = dc2 * lng_ref[...]
        dc1_ref[...] = rs * (dxh - jnp.mean(dxh, axis=-1, keepdims=True) - xh * jnp.mean(dxh * xh, axis=-1, keepdims=True))

    row = lambda n: pl.BlockSpec((tm, n), lambda i: (i, 0))
    full = lambda a, b: pl.BlockSpec((a, b), lambda i: (0, 0))
    body, dep_spec, dep_arg = _anchored(body, 12 + GATE_PARTS, dep)
    return pl.pallas_call(
        body, grid=(T // tm,),
        in_specs=[row(D_MODEL), row(D_MODEL), row(D_MODEL)] + _gate_specs(tm) + [row(ATTN_OUT), row(D_MODEL),
                  full(ATTN_OUT, D_MODEL), full(D_MODEL, D_MODEL), full(D_MODEL, D_MODEL),
                  full(1, 2 * D_MODEL), full(1, D_MODEL), full(1, D_MODEL), full(ATTN_OUT, ATTN_OUT)] + dep_spec,
        out_specs=[row(2 * D_MODEL), row(D_MODEL), row(D_MODEL), row(ATTN_OUT), row(ATTN_OUT), row(D_MODEL),
                   full(1, 2 * D_MODEL), full(1, D_MODEL), full(1, D_MODEL)],
        out_shape=[SDS((T, 2 * D_MODEL), BF16), SDS((T, D_MODEL), BF16), SDS((T, D_MODEL), BF16), SDS((T, ATTN_OUT), F32),
                   SDS((T, ATTN_OUT), F32), SDS((T, D_MODEL), F32),
                   SDS((1, 2 * D_MODEL), F32), SDS((1, D_MODEL), F32), SDS((1, D_MODEL), F32)],
        compiler_params=_params("arbitrary"), name="mid_bwd")(dx1b, ya, yc, *([proj] * GATE_PARTS), att, c1, w_a, w_c, w_o,
                                                               gate_b, ln_g, ln_b, head_ones, *dep_arg)


FFN_TM = 256
FFN_CHUNK = 512
FFN_SUB = tuple((lo, min(lo + FFN_CHUNK, D_FF)) for lo in range(0, D_FF, FFN_CHUNK))


def _rms_bwd(dy_times_g, xh, r):
    return r * (dy_times_g - xh * jnp.mean(dy_times_g * xh, axis=-1, keepdims=True))


def _load_resident(pairs, sems):
    @pl.when(pl.program_id(0) == 0)
    def _():
        copies = [pltpu.make_async_copy(src, dst, sems.at[k]) for k, (src, dst) in enumerate(pairs)]
        for cp in copies:
            cp.start()
        for cp in copies:
            cp.wait()


def _ffn_fwd(h2, x1, target, gf, w_g_t, w_u_t, w_d):
    T = h2.shape[0]
    tm = FFN_TM

    def body(h_ref, x1_ref, t_ref, gf_ref, wg_hbm, wu_hbm, wd_hbm,
             a_ref, b_ref, f_ref, dx2_ref, dx2b_ref, loss_ref, gnf_ref, wg, wu, wd, sems):
        _load_resident(((wg_hbm, wg), (wu_hbm, wu), (wd_hbm, wd)), sems)

        @pl.when(pl.program_id(0) == 0)
        def _():
            loss_ref[...] = jnp.zeros_like(loss_ref)
            gnf_ref[...] = jnp.zeros_like(gnf_ref)

        h = h_ref[...]
        x2 = x1_ref[...]
        for lo, hi in FFN_SUB:
            a = _dot_nt(h, wg[lo:hi, :])
            b = _dot_nt(h, wu[lo:hi, :])
            f = (a * _sigmoid(a) * b).astype(BF16)
            a_ref[:, lo:hi] = a.astype(BF16)
            b_ref[:, lo:hi] = b.astype(BF16)
            f_ref[:, lo:hi] = f
            x2 = x2 + _dot(f, wd[lo:hi, :])

        r = lax.rsqrt(jnp.mean(x2 * x2, axis=-1, keepdims=True) + RMS_EPS)
        xh = x2 * r
        err = xh * gf_ref[...] - t_ref[...]
        loss_ref[...] += (0.5 / D_MODEL) * jnp.sum(err * err)
        dy = err * (1.0 / D_MODEL)
        gnf_ref[...] += _rowsum(dy * xh)
        dx2 = _rms_bwd(dy * gf_ref[...], xh, r)
        dx2_ref[...] = dx2
        dx2b_ref[...] = dx2.astype(BF16)

    row = lambda n: pl.BlockSpec((tm, n), lambda i: (i, 0))
    const = lambda n: pl.BlockSpec((1, n), lambda i: (0, 0))
    wshape = pltpu.VMEM((D_FF, D_MODEL), BF16)
    return pl.pallas_call(
        body, grid=(T // tm,),
        in_specs=[row(D_MODEL), row(D_MODEL), row(D_MODEL), const(D_MODEL), ANY_SPEC, ANY_SPEC, ANY_SPEC],
        out_specs=[row(D_FF), row(D_FF), row(D_FF), row(D_MODEL), row(D_MODEL), const(128), const(D_MODEL)],
        out_shape=[SDS((T, D_FF), BF16), SDS((T, D_FF), BF16), SDS((T, D_FF), BF16), SDS((T, D_MODEL), F32),
                   SDS((T, D_MODEL), BF16), SDS((1, 128), F32), SDS((1, D_MODEL), F32)],
        scratch_shapes=[wshape, wshape, wshape, pltpu.SemaphoreType.DMA((3,))],
        compiler_params=_params("arbitrary"), name="ffn_fwd")(h2, x1, target, gf, w_g_t, w_u_t, w_d)


def _ffn_bwd(dx2b, dx2, a, b, x1, g2, w_g_t, w_u_t, w_d):
    T = dx2.shape[0]
    tm = FFN_TM

    def body(dxb_ref, dx2_ref, a_ref, b_ref, x1_ref, g2_ref, wg_hbm, wu_hbm, wd_hbm,
             da_ref, db_ref, dx1_ref, dx1b_ref, gn2_ref, wg, wu, wd, sems):
        _load_resident(((wg_hbm, wg), (wu_hbm, wu), (wd_hbm, wd)), sems)

        @pl.when(pl.program_id(0) == 0)
        def _():
            gn2_ref[...] = jnp.zeros_like(gn2_ref)

        dxb = dxb_ref[...]
        dh2 = jnp.zeros((tm, D_MODEL), F32)
        for lo, hi in FFN_SUB:
            df = _dot_nt(dxb, wd[lo:hi, :])
            av = a_ref[:, lo:hi].astype(F32)
            bv = b_ref[:, lo:hi].astype(F32)
            sg = _sigmoid(av)
            db = (df * av * sg).astype(BF16)
            da = (df * bv * (sg * (1.0 + av * (1.0 - sg)))).astype(BF16)
            da_ref[:, lo:hi] = da
            db_ref[:, lo:hi] = db
            dh2 = dh2 + _dot(da, wg[lo:hi, :]) + _dot(db, wu[lo:hi, :])

        x1 = x1_ref[...]
        r = lax.rsqrt(jnp.mean(x1 * x1, axis=-1, keepdims=True) + RMS_EPS)
        xh = x1 * r
        gn2_ref[...] += _rowsum(dh2 * xh)
        dx1 = dx2_ref[...] + _rms_bwd(dh2 * g2_ref[...], xh, r)
        dx1_ref[...] = dx1
        dx1b_ref[...] = dx1.astype(BF16)

    row = lambda n: pl.BlockSpec((tm, n), lambda i: (i, 0))
    const = lambda n: pl.BlockSpec((1, n), lambda i: (0, 0))
    wshape = pltpu.VMEM((D_FF, D_MODEL), BF16)
    return pl.pallas_call(
        body, grid=(T // tm,),
        in_specs=[row(D_MODEL), row(D_MODEL), row(D_FF), row(D_FF), row(D_MODEL), const(D_MODEL),
                  ANY_SPEC, ANY_SPEC, ANY_SPEC],
        out_specs=[row(D_FF), row(D_FF), row(D_MODEL), row(D_MODEL), const(D_MODEL)],
        out_shape=[SDS((T, D_FF), BF16), SDS((T, D_FF), BF16), SDS((T, D_MODEL), F32), SDS((T, D_MODEL), BF16),
                   SDS((1, D_MODEL), F32)],
        scratch_shapes=[wshape, wshape, wshape, pltpu.SemaphoreType.DMA((3,))],
        compiler_params=_params("arbitrary"), name="ffn_bwd")(dx2b, dx2, a, b, x1, g2, w_g_t, w_u_t, w_d)


def _in_bwd(pieces, w_in_t, x, dx1, g1, dep=None):
    T = x.shape[0]
    tm = IN_TM
    npc = len(pieces)
    assert sum(p.shape[1] for p in pieces) == IN_WIDTH

    def body(*refs):
        p_refs = refs[:npc]
        w_hbm, x_ref, dx1_ref, g_ref, dx_ref, gn1_ref, w_vmem, sem = refs[npc:]

        @pl.when(pl.program_id(0) == 0)
        def _():
            cp = pltpu.make_async_copy(w_hbm, w_vmem, sem)
            cp.start()
            cp.wait()
            gn1_ref[...] = jnp.zeros_like(gn1_ref)

        dh = jnp.zeros((tm, D_MODEL), F32)
        col = 0
        for p_ref in p_refs:
            for j in range(p_ref.shape[1] // IN_CHUNK):
                dh = dh + _dot(p_ref[:, j * IN_CHUNK:(j + 1) * IN_CHUNK], w_vmem[col:col + IN_CHUNK, :])
                col += IN_CHUNK
        xv = x_ref[...]
        r = lax.rsqrt(jnp.mean(xv * xv, axis=-1, keepdims=True) + RMS_EPS)
        xh = xv * r
        gn1_ref[...] += _rowsum(dh * xh)
        dx_ref[...] = dx1_ref[...] + _rms_bwd(dh * g_ref[...], xh, r)

    row = lambda n: pl.BlockSpec((tm, n), lambda i: (i, 0))
    body, dep_spec, dep_arg = _anchored(body, npc + 4, dep)
    return pl.pallas_call(
        body, grid=(T // tm,),
        in_specs=[row(p.shape[1]) for p in pieces]
        + [pl.BlockSpec(memory_space=pl.ANY), row(D_MODEL), row(D_MODEL), pl.BlockSpec((1, D_MODEL), lambda i: (0, 0))]
        + dep_spec,
        out_specs=[row(D_MODEL), pl.BlockSpec((1, D_MODEL), lambda i: (0, 0))],
        out_shape=[SDS((T, D_MODEL), F32), SDS((1, D_MODEL), F32)],
        scratch_shapes=[pltpu.VMEM((IN_WIDTH, D_MODEL), BF16), pltpu.SemaphoreType.DMA],
        compiler_params=_params("arbitrary"), name="in_bwd")(*pieces, w_in_t, x, dx1, g1, *dep_arg)


def _local_step(x, target, in_proj, small, late_weights=None, emit=None):
    T = x.shape[0]
    batch = T // SEQ
    slopes_r = jnp.asarray(_slopes_times_dilation())
    emit = emit or (lambda names, grads: None)

    h, proj, w = in_proj()
    proj3 = proj.reshape(batch, SEQ, IN_WIDTH)

    att, lse = _attn_fwd(proj3, slopes_r, batch, w.get("token"))
    att = att.reshape(T, ATTN_OUT)
    if late_weights is not None:
        w = {**w, **late_weights("after_attention", att)}

    c1 = _conv_fwd(proj3, w["conv_w"], small["conv_b"], batch, w.get("token")).reshape(T, D_MODEL)
    if late_weights is not None:
        w = {**w, **late_weights(LATE_MERGE, (att, c1))}

    c3, ya, yc, mix, x1, h2 = _mid_fwd(
        att, c1, proj, x, w["w_attn_out"], w["w_conv_out"], w["w_o"],
        small["gate_b"], small["conv_ln_g"], small["conv_ln_b"], small["norm2_g"], w.get("token"))
    if late_weights is not None:
        w = {**w, **late_weights(LATE_FFN, h2)}

    a, b, f, dx2, dx2b, loss, g_normf = _ffn_fwd(h2, x1, target, small["norm_f_g"],
                                                   w["w_ffn_gate"], w["w_ffn_up"], w["w_ffn_down"])

    da, db, dx1, dx1b, g_norm2 = _ffn_bwd(dx2b, dx2, a, b, x1, small["norm2_g"],
                                           w["w_ffn_gate"], w["w_ffn_up"], w["w_ffn_down"])
    gw = {}
    gw["w_ffn_down"] = _mm_tn(f, dx2b, BF16, "gw_ffn_down", tn=1024)
    gw["w_ffn_gate"] = _mm_tn(da, h2, BF16, "gw_ffn_gate", tn=1024)
    gw["w_ffn_up"] = _mm_tn(db, h2, BF16, "gw_ffn_up", tn=1024)
    token = emit(("w_ffn_gate", "w_ffn_up", "w_ffn_down"), gw)

    head_ones = jnp.asarray(np.kron(np.eye(HEADS_PER_GROUP, dtype=np.float32), np.ones((HEAD_DIM, HEAD_DIM), np.float32)))
    dlogits, dya, dyc, datt, dsum, dc1, g_gate_b, g_ln_g, g_ln_b = _mid_bwd(
        dx1b, ya, yc, proj, att, c1, w["w_attn_out"], w["w_conv_out"], w["w_o"],
        small["gate_b"], small["conv_ln_g"], small["conv_ln_b"], head_ones, token)
    gw["w_o"] = _mm_tn(mix, dx1b, BF16, "gw_o", tn=1024)
    gw["w_attn_out"] = _mm_tn(att, dya, BF16, "gw_attn_out", tn=1024)
    gw["w_conv_out"] = _mm_tn(c3, dyc, BF16, "gw_conv_out", tn=1024)
    token = emit(("w_conv_out", "w_attn_out", "w_o"), gw)

    dua, dub, g_conv_w, g_conv_b = _conv_bwd(proj3, dc1.reshape(batch, SEQ, D_MODEL), w["conv_w"], batch, token)

    dq, dk, dv = _attn_bwd(proj3, datt.reshape(batch, SEQ, ATTN_OUT), lse, dsum.reshape(batch, SEQ, ATTN_OUT),
                           slopes_r, batch)
    pieces = [dq.reshape(T, ATTN_WIDTH), dk.reshape(T, ATTN_WIDTH), dv.reshape(T, ATTN_WIDTH),
              dua.reshape(T, D_MODEL), dub.reshape(T, D_MODEL), dlogits]

    names = ("q", "k", "v", "ua", "ub", "gate")
    gw["w_in"] = jnp.concatenate([_mm_tn(p, h, BF16, "gw_in_" + nm, tn=1024) for nm, p in zip(names, pieces)], axis=0)
    gw["conv_w"] = g_conv_w
    token = emit(("w_in", "conv_w"), gw)
    grad_x, g_norm1 = _in_bwd(pieces, w["w_in"], x, dx1, small["norm1_g"], token)

    gsmall = {"norm1_g": g_norm1, "gate_b": g_gate_b, "conv_b": g_conv_b, "conv_ln_g": g_ln_g, "conv_ln_b": g_ln_b,
              "norm2_g": g_norm2, "norm_f_g": g_normf}
    return loss, grad_x, gw, gsmall


ANY = pl.BlockSpec(memory_space=pl.ANY)


def _all_gather(arrs):
    n = len(arrs)

    def body(*refs):
        ins, outs = refs[:n], refs[n:2 * n]
        send_sems, recv_sems, local_sems = refs[2 * n:]
        x, y, c = lax.axis_index("x"), lax.axis_index("y"), lax.axis_index("c")
        me, sibling = (x, y, c), (x, y, 1 - c)
        chips = [(1 - x, y), (x, 1 - y), (1 - x, 1 - y)]

        def copy(a, k, block, to, src=None):
            px, py, pc = block
            dst = outs[a].at[4 * px + 2 * py + pc]
            return pltpu.make_async_remote_copy(
                src_ref=dst if src is None else src, dst_ref=dst,
                send_sem=send_sems.at[a, k], recv_sem=recv_sems.at[a, k], device_id=to, device_id_type=MESH)

        mine = [pltpu.make_async_copy(ins[a], outs[a].at[4 * x + 2 * y + c], local_sems.at[a]) for a in range(n)]
        for cp in mine:
            cp.start()
        first = []
        for j, chip in enumerate(chips):
            first += [copy(a, 1 + j, me, (*chip, c), src=ins[a]) for a in range(n)]
        first += [copy(a, 0, me, sibling, src=ins[a]) for a in range(n)]
        for cp in first:
            cp.start()
        passed = []
        for j, chip in enumerate(chips):
            for a in range(n):
                copy(a, 1 + j, (*chip, c), me).wait_recv()
                cp = copy(a, 4 + j, (*chip, c), sibling)
                cp.start()
                passed.append(cp)
        for a in range(n):
            copy(a, 0, sibling, me).wait_recv()
        for j, chip in enumerate(chips):
            for a in range(n):
                copy(a, 4 + j, (*chip, 1 - c), me).wait_recv()
        for cp in first + passed:
            cp.wait_send()
        for cp in mine:
            cp.wait()

    return pl.pallas_call(
        body, in_specs=[ANY] * n, out_specs=[ANY] * n,
        out_shape=[SDS((N_DEV,) + a.shape, a.dtype) for a in arrs],
        scratch_shapes=[pltpu.SemaphoreType.DMA((n, 7)), pltpu.SemaphoreType.DMA((n, 7)), pltpu.SemaphoreType.DMA((n,))],
        name="all_gather_weights")(*arrs)


HBM =pl.BlockSpec(memory_space=pltpu.HBM)
SEM = pl.BlockSpec(memory_space=pltpu.SEMAPHORE)
ALL_PEERS = tuple(range(1, N_DEV))
OTHER_CHIPS = (2, 4, 6)
SPLIT_EFFECT = pltpu.CompilerParams(has_side_effects=pltpu.SideEffectType.DATAFLOW_SIDE_EFFECTING)


def _exchange_copies(mode, ks, srcs, lands, send_sems, recv_sems):
    x, y, c = lax.axis_index("x"), lax.axis_index("y"), lax.axis_index("c")
    me = 4 * x + 2 * y + c
    send, recv = [], []
    for a in range(len(lands)):
        for i, k in enumerate(ks):
            peer = (x ^ ((k >> 2) & 1), y ^ ((k >> 1) & 1), c ^ (k & 1))
            pidx = 4 * peer[0] + 2 * peer[1] + peer[2]
            if mode == "gather":
                src, to, out_slot, in_slot = srcs[a], peer, me, pidx
            elif mode == "scatter":
                src, to, out_slot, in_slot = srcs[a].at[pidx], peer, me, pidx
            elif mode == "chip_scatter":
                src, to, out_slot, in_slot = srcs[a].at[pidx >> 1], peer, me >> 1, pidx >> 1
            else:
                src, to, out_slot, in_slot = lands[a].at[pidx], (x, y, 1 - c), pidx, pidx ^ 1
            s = a * len(ks) + i
            send.append(pltpu.make_async_remote_copy(
                src_ref=src, dst_ref=lands[a].at[out_slot], send_sem=send_sems.at[s], recv_sem=recv_sems.at[s],
                device_id=to, device_id_type=MESH))
            recv.append(pltpu.make_async_remote_copy(
                src_ref=src, dst_ref=lands[a].at[in_slot], send_sem=send_sems.at[s], recv_sem=recv_sems.at[s],
                device_id=to, device_id_type=MESH))
    return send, recv


def _send_start(mode, ks, name, srcs=(), lands=None, dep=None):
    srcs = list(srcs)
    if lands is None:
        slots = 4 if mode == "chip_scatter" else N_DEV
        lands = [lax.empty((slots,) + (s.shape if mode == "gather" else s.shape[1:]), s.dtype) for s in srcs]
    ns, nl = len(srcs), len(lands)
    nsem = nl * len(ks)

    def body(*refs):
        send, _ = _exchange_copies(mode, ks, refs[:ns], refs[ns:ns + nl], refs[ns + nl], refs[ns + nl + 1])
        for cp in send:
            cp.start()
        token = refs[-1]
        token[...] = jnp.zeros_like(token)

    both = srcs + list(lands)
    body, dep_spec, dep_arg = _anchored(body, ns + nl, dep)
    res = pl.pallas_call(
        body, name=name,
        out_shape=(pltpu.SemaphoreType.DMA((nsem,)), pltpu.SemaphoreType.DMA((nsem,)),
                   *[pltpu.HBM(a.shape, a.dtype) for a in both], SDS((8, 128), F32)),
        in_specs=[HBM] * (ns + nl) + dep_spec,
        out_specs=(SEM, SEM, *([HBM] * (ns + nl)), pl.BlockSpec(memory_space=pltpu.VMEM)),
        input_output_aliases={i: 2 + i for i in range(ns + nl)}, compiler_params=SPLIT_EFFECT,
    )(*[pltpu.with_memory_space_constraint(a, pltpu.HBM) for a in both], *dep_arg)
    return dict(mode=mode, ks=ks, send_sems=res[0], recv_sems=res[1], srcs=res[2:2 + ns], lands=res[2 + ns:2 + ns + nl],
                token=res[-1])


def _send_wait(started, after, name):
    ns, nl = len(started["srcs"]), len(started["lands"])

    def body(*refs):
        send, recv = _exchange_copies(started["mode"], started["ks"], refs[:ns], refs[ns:ns + nl],
                                      refs[ns + nl], refs[ns + nl + 1])
        for cp in send:
            cp.wait_send()
        for cp in recv:
            cp.wait_recv()

    both = list(started["srcs"]) + list(started["lands"])
    after = after if isinstance(after, (tuple, list)) else (after,)
    res = pl.pallas_call(
        body, name=name,
        out_shape=tuple(pltpu.HBM(a.shape, a.dtype) for a in both),
        in_specs=[HBM] * (ns + nl) + [SEM, SEM] + [ANY] * len(after), out_specs=tuple([HBM] * (ns + nl)),
        input_output_aliases={i: i for i in range(ns + nl)}, compiler_params=SPLIT_EFFECT,
    )(*both, started["send_sems"], started["recv_sems"], *after)
    return res[:ns], res[ns:]


def _exchange_sibling(gs):
    n = len(gs)

    def body(*refs):
        ins, outs = refs[:n], refs[n:2 * n]
        send_sems, recv_sems = refs[2 * n:]
        x, y, c = lax.axis_index("x"), lax.axis_index("y"), lax.axis_index("c")
        copies = []
        for a in range(n):
            for j in range(4):
                copies.append(pltpu.make_async_remote_copy(
                    src_ref=ins[a].at[2 * j + (1 - c)], dst_ref=outs[a].at[j],
                    send_sem=send_sems.at[a, j], recv_sem=recv_sems.at[a, j],
                    device_id=(x, y, 1 - c), device_id_type=MESH))
        for cp in copies:
            cp.start()
        for cp in copies:
            cp.wait_recv()
        for cp in copies:
            cp.wait_send()

    return pl.pallas_call(
        body, in_specs=[ANY] * n, out_specs=[ANY] * n,
        out_shape=[SDS((4,) + g.shape[1:], g.dtype) for g in gs],
        scratch_shapes=[pltpu.SemaphoreType.DMA((n, 4)), pltpu.SemaphoreType.DMA((n, 4))],
        name="reduce_scatter_sibling")(*gs)


def _add_pair(g, r1, core, name):
    _, rows, cols = g.shape
    tr = _row_tile(rows, cols, 3 * g.dtype.itemsize)

    def body(c_ref, g_ref, r_ref, o_ref):
        o_ref[...] = (g_ref[...].astype(F32) + r_ref[...].astype(F32)).astype(o_ref.dtype)

    return pl.pallas_call(
        body,
        grid_spec=pltpu.PrefetchScalarGridSpec(
            num_scalar_prefetch=1, grid=(4, rows // tr),
            in_specs=[pl.BlockSpec((1, tr, cols), lambda j, i, c_ref: (2 * j + c_ref[0], i, 0)),
                      pl.BlockSpec((1, tr, cols), lambda j, i, c_ref: (j, i, 0))],
            out_specs=pl.BlockSpec((1, tr, cols), lambda j, i, c_ref: (j, i, 0))),
        out_shape=SDS((4, rows, cols), g.dtype),
        compiler_params=_params("parallel", "parallel"), name=name)(core, g, r1)


def _row_tile(rows, cols, itemsize_total):
    budget = (4 << 20) // max(1, cols * itemsize_total)
    if rows <= budget:
        return rows
    t = rows
    while t > budget and t % 2 == 0 and (t // 2) % 16 == 0:
        t //= 2
    return t


def _adam_math(g, w, m, v):
    m_new = ADAM_B1 * m + (1.0 - ADAM_B1) * g
    v_new = ADAM_B2 * v + (1.0 - ADAM_B2) * (g * g)
    m_hat = m_new / (1.0 - ADAM_B1 ** ADAM_STEP)
    v_hat = v_new / (1.0 - ADAM_B2 ** ADAM_STEP)
    delta = -ADAM_LR * (m_hat / (jnp.sqrt(v_hat) + ADAM_EPS) + ADAM_WD * w)
    return delta, m_new, v_new


def _sum_adam(parts, own, mine, w, m, v, name):
    rows, cols = w.shape
    nparts = parts.shape[0]
    tr = _row_tile(rows, cols, (nparts + 1) * parts.dtype.itemsize + 7 * 4)

    def body(mine_ref, p_ref, own_ref, w_ref, m_ref, v_ref, g_ref, d_ref, mo_ref, vo_ref):
        g = None
        for s in range(nparts):
            part = jnp.where(mine_ref[0] == s, own_ref[0], p_ref[s]).astype(F32)
            g = part if g is None else g + part
        delta, m_new, v_new = _adam_math(g, w_ref[...], m_ref[...], v_ref[...])
        g_ref[...] = g
        d_ref[...] = delta
        mo_ref[...] = m_new
        vo_ref[...] = v_new

    blk = pl.BlockSpec((tr, cols), lambda i, mine_ref: (i, 0))
    out = SDS((rows, cols), F32)
    return pl.pallas_call(
        body,
        grid_spec=pltpu.PrefetchScalarGridSpec(
            num_scalar_prefetch=1, grid=(rows // tr,),
            in_specs=[pl.BlockSpec((nparts, tr, cols), lambda i, mine_ref: (0, i, 0)),
                      pl.BlockSpec((1, tr, cols), lambda i, mine_ref: (mine_ref[0], i, 0)), blk, blk, blk],
            out_specs=[blk, blk, blk, blk]),
        out_shape=[out, out, out, out],
        compiler_params=_params("parallel"), name=name)(mine, parts, own, w, m, v)


SMALL_ROWS = 72


def _small_allreduce_adam(gpart, w, m, v, row_counts, dep=None):
    def reduce_body(g_ref, go_ref, gath, send_sems, recv_sems):
        x, y, c = lax.axis_index("x"), lax.axis_index("y"), lax.axis_index("c")
        me = 4 * x + 2 * y + c
        gath[me] = g_ref[...]
        copies = []
        for k in range(1, N_DEV):
            fx, fy, fc = (k >> 2) & 1, (k >> 1) & 1, k & 1
            peer = (x ^ fx, y ^ fy, c ^ fc)
            copies.append(pltpu.make_async_remote_copy(
                src_ref=gath.at[me], dst_ref=gath.at[me], send_sem=send_sems.at[k - 1], recv_sem=recv_sems.at[k - 1],
                device_id=peer, device_id_type=MESH))
        for cp in copies:
            cp.start()
        for cp in copies:
            cp.wait_recv()
        for cp in copies:
            cp.wait_send()
        g = gath[0]
        for d in range(1, N_DEV):
            g = g + gath[d]
        go_ref[...] = g

    def adam_body(g_ref, w_ref, m_ref, v_ref, *out_refs):
        g = g_ref[...]
        delta, m_new, v_new = _adam_math(g, w_ref[...], m_ref[...], v_ref[...])
        outs = iter(out_refs)
        for val in (g, delta, m_new, v_new):
            lo = 0
            for r in row_counts:
                next(outs)[...] = val[lo:lo + r]
                lo += r
        next(outs)[...] = g[SMALL_ROWS - SUBLANES:]

    vm = pl.BlockSpec(memory_space=pltpu.VMEM)
    reduce_body, dep_spec, dep_arg = _anchored(reduce_body, 1, dep)
    total = pl.pallas_call(
        reduce_body, in_specs=[vm] + dep_spec, out_specs=vm, out_shape=SDS((SMALL_ROWS, 128), F32),
        scratch_shapes=[pltpu.VMEM((N_DEV, SMALL_ROWS, 128), F32), pltpu.SemaphoreType.DMA((N_DEV - 1,)),
                        pltpu.SemaphoreType.DMA((N_DEV - 1,))],
        name="small_allreduce")(gpart, *dep_arg)
    out_shape = [SDS((r, 128), F32) for _ in range(4) for r in row_counts] + [SDS((SUBLANES, 128), F32)]
    res = pl.pallas_call(adam_body, in_specs=[vm] * 4, out_specs=[vm] * len(out_shape), out_shape=out_shape,
                         name="small_adam")(total, w, m, v)
    k = len(row_counts)
    return [res[i * k:(i + 1) * k] for i in range(4)], res[-1]


BIG = ("w_in", "conv_w", "w_conv_out", "w_attn_out", "w_o", "w_ffn_gate", "w_ffn_up", "w_ffn_down")
LATE_MERGE = ("w_conv_out", "w_attn_out", "w_o")
LATE_FFN = ("w_ffn_gate", "w_ffn_up", "w_ffn_down")
TRANSPOSED = ("w_in", "w_ffn_gate", "w_ffn_up")
COL_SHARDED = ("conv_w", "w_attn_out")
SMALL = ("norm1_g", "gate_b", "conv_b", "conv_ln_g", "conv_ln_b", "norm2_g", "norm_f_g")
WEIGHTS = ("norm1_g", "w_in", "gate_b", "conv_w", "conv_b", "conv_ln_g", "conv_ln_b", "w_conv_out", "w_attn_out", "w_o",
           "norm2_g", "w_ffn_gate", "w_ffn_up", "w_ffn_down", "norm_f_g")


def _shard2d(name, a):
    a = a.reshape(a.shape[-2], a.shape[-1])
    if name in TRANSPOSED:
        a = a.T
    if name == "conv_w":
        a = jnp.pad(a, ((0, CONV_PAD - CONV_K), (0, 0)))
    return a


def _from_shard2d(name, val, shape):
    if name in TRANSPOSED:
        val = val.T
    if name == "conv_w":
        val = val[:CONV_K]
    return val.reshape(shape)


def _gathered_to_full(name, g):
    if name in COL_SHARDED:
        return g.transpose(1, 0, 2).reshape(g.shape[1], N_DEV * g.shape[2])
    return g.reshape(N_DEV * g.shape[1], g.shape[2])


def _full_to_blocks(name, g):
    if name in COL_SHARDED:
        return g.reshape(g.shape[0], N_DEV, g.shape[1] // N_DEV).transpose(1, 0, 2)
    return g.reshape(N_DEV, g.shape[0] // N_DEV, g.shape[1])


def _pack_small(d, last_rows):
    vec = jnp.concatenate([d[n].reshape(-1) for n in SMALL]).reshape(SMALL_ROWS - SUBLANES, 128)
    return jnp.concatenate([vec, last_rows], axis=0)


def kernel(x, norm1_g, w_in, gate_b, conv_w, conv_b, conv_ln_g, conv_ln_b, w_conv_out, w_attn_out, w_o, norm2_g, w_ffn_gate, w_ffn_up, w_ffn_down, norm_f_g, loss_target, m_norm1_g, m_w_in, m_gate_b, m_conv_w, m_conv_b, m_conv_ln_g, m_conv_ln_b, m_w_conv_out, m_w_attn_out, m_w_o, m_norm2_g, m_w_ffn_gate, m_w_ffn_up, m_w_ffn_down, m_norm_f_g, v_norm1_g, v_w_in, v_gate_b, v_conv_w, v_conv_b, v_conv_ln_g, v_conv_ln_b, v_w_conv_out, v_w_attn_out, v_w_o, v_norm2_g, v_w_ffn_gate, v_w_ffn_up, v_w_ffn_down, v_norm_f_g):
    wts = dict(norm1_g=norm1_g, w_in=w_in, gate_b=gate_b, conv_w=conv_w, conv_b=conv_b, conv_ln_g=conv_ln_g,
               conv_ln_b=conv_ln_b, w_conv_out=w_conv_out, w_attn_out=w_attn_out, w_o=w_o, norm2_g=norm2_g,
               w_ffn_gate=w_ffn_gate, w_ffn_up=w_ffn_up, w_ffn_down=w_ffn_down, norm_f_g=norm_f_g)
    mom1 = dict(norm1_g=m_norm1_g, w_in=m_w_in, gate_b=m_gate_b, conv_w=m_conv_w, conv_b=m_conv_b, conv_ln_g=m_conv_ln_g,
                conv_ln_b=m_conv_ln_b, w_conv_out=m_w_conv_out, w_attn_out=m_w_attn_out, w_o=m_w_o, norm2_g=m_norm2_g,
                w_ffn_gate=m_w_ffn_gate, w_ffn_up=m_w_ffn_up, w_ffn_down=m_w_ffn_down, norm_f_g=m_norm_f_g)
    mom2 = dict(norm1_g=v_norm1_g, w_in=v_w_in, gate_b=v_gate_b, conv_w=v_conv_w, conv_b=v_conv_b, conv_ln_g=v_conv_ln_g,
                conv_ln_b=v_conv_ln_b, w_conv_out=v_w_conv_out, w_attn_out=v_w_attn_out, w_o=v_w_o, norm2_g=v_norm2_g,
                w_ffn_gate=v_w_ffn_gate, w_ffn_up=v_w_ffn_up, w_ffn_down=v_w_ffn_down, norm_f_g=v_norm_f_g)

    T = x.shape[0] * x.shape[1]
    x2 = x.reshape(T, D_MODEL)
    t2 = loss_target.reshape(T, D_MODEL)

    me = 4 * lax.axis_index("x") + 2 * lax.axis_index("y") + lax.axis_index("c")
    shards = {n: _shard2d(n, wts[n]) for n in BIG}
    sent = {n: shards[n] if n == "conv_w" else shards[n].astype(BF16) for n in BIG}
    small = {n: wts[n].reshape(1, -1) for n in SMALL}

    stage = {}

    def in_proj():
        w_in_blocks, conv_blocks = _all_gather([sent["w_in"], sent["conv_w"]])
        near = (1,) + OTHER_CHIPS
        stage["merge"] = _send_start("gather", near, "gather_start_merge", [sent[n] for n in LATE_MERGE], dep=w_in_blocks)
        stage["ffn"] = _send_start("gather", near, "gather_start_ffn", [sent[n] for n in LATE_FFN],
                                   dep=stage["merge"]["token"])
        w_in_t = _gathered_to_full("w_in", w_in_blocks)
        h, proj = _in_proj(x2, small["norm1_g"], w_in_t, stage["ffn"]["token"])
        return h, proj, {"w_in": w_in_t, "conv_w": _gathered_to_full("conv_w", conv_blocks)}

    def filled(names, srcs, lands):
        return {n: _gathered_to_full(n, lax.dynamic_update_slice(land, src[None], (me, 0, 0)))
                for n, src, land in zip(names, srcs, lands)}

    def pass_on(group, after):
        stage[group + "_srcs"], lands = _send_wait(stage[group], after, "gather_wait_" + group)
        stage[group + "_forward"] = _send_start("forward", OTHER_CHIPS, "forward_start_" + group, lands=lands)
        return stage[group + "_forward"]["token"]

    def arrived(group, names, after):
        _, lands = _send_wait(stage[group + "_forward"], after, "forward_wait_" + group)
        return filled(names, stage[group + "_srcs"], lands)

    def late_weights(which, after):
        if which == "after_attention":
            return {"token": pass_on("merge", after)}
        if which is LATE_MERGE:
            return {**arrived("merge", LATE_MERGE, after), "token": pass_on("ffn", after)}
        return arrived("ffn", LATE_FFN, after)

    scatters = []
    core = lax.axis_index("c").astype(jnp.int32).reshape(1)

    def emit(names, gw):
        blocks = [_full_to_blocks(n, gw[n]) for n in names]
        if "w_in" in names:
            sums = [_add_pair(g, r, core, "chip_sum_" + n) for n, g, r in zip(names, blocks, _exchange_sibling(blocks))]
            started = _send_start("chip_scatter", OTHER_CHIPS, "scatter_start_" + names[0], sums)
        else:
            started = _send_start("scatter", ALL_PEERS, "scatter_start_" + names[0], blocks)
        scatters.append((names, started))
        return started["token"]

    loss_part, grad_x, gw, gsmall = _local_step(x2, t2, in_proj, small, late_weights, emit)

    grads, deltas, new_m, new_v = {}, {}, {}, {}
    after = grad_x
    for names, started in scatters:
        srcs, lands = _send_wait(started, after, "scatter_wait_" + names[0])
        mine = (me >> 1 if started["mode"] == "chip_scatter" else me).astype(jnp.int32).reshape(1)
        for n, src, land in zip(names, srcs, lands):
            g, d, mo, vo = _sum_adam(land, src, mine, shards[n], _shard2d(n, mom1[n]), _shard2d(n, mom2[n]), "adam_" + n)
            for dst, val in ((grads, g), (deltas, d), (new_m, mo), (new_v, vo)):
                dst[n] = _from_shard2d(n, val, wts[n].shape)
            after = g

    zeros, ones = jnp.zeros((SUBLANES, 128), F32), jnp.ones((SUBLANES, 128), F32)
    row_counts = [wts[n].size // 128 for n in SMALL]
    kinds, loss_rows = _small_allreduce_adam(
        _pack_small(gsmall, jnp.broadcast_to(loss_part, (SUBLANES, 128))), _pack_small(wts, zeros),
        _pack_small(mom1, zeros), _pack_small(mom2, ones), row_counts, after)
    for dst, vals in zip((grads, deltas, new_m, new_v), kinds):
        dst.update({n: val.reshape(wts[n].shape) for n, val in zip(SMALL, vals)})
    loss = loss_rows[0, 0]
    return (loss, grad_x.reshape(x.shape), *[grads[n] for n in WEIGHTS], *[deltas[n] for n in WEIGHTS],
            *[new_m[n] for n in WEIGHTS], *[new_v[n] for n in WEIGHTS])
```

```python
import math

import numpy as np
import jax
import jax.numpy as jnp
from jax import lax
from jax.experimental import pallas as pl
from jax.experimental.pallas import tpu as pltpu

F32 = jnp.float32
BF16 = jnp.bfloat16
SDS = jax.ShapeDtypeStruct
MESH = pl.DeviceIdType.MESH

D_MODEL = 1024
SEQ = 2048
HEAD_DIM = 64
GROUPS = ((128, 1), (512, 4), (2048, 16))
HEADS_PER_GROUP = 8
N_HEADS = 24
ATTN_WIDTH = N_HEADS * HEAD_DIM
ATTN_OUT = HEADS_PER_GROUP * HEAD_DIM
CONV_K = 31
CONV_PAD = 32
D_FF = 2816
IN_WIDTH = 3 * ATTN_WIDTH + 2 * D_MODEL + 2 * D_MODEL
RMS_EPS = 1e-6
LN_EPS = 1e-5
Q_BLOCK = 128
LANES = 128
NEG = -1e30
N_DEV = 8

ADAM_LR = 0.001
ADAM_B1 = 0.9
ADAM_B2 = 0.999
ADAM_EPS = 1e-08
ADAM_WD = 0.01
ADAM_STEP = 10


def _alibi_slope_list(n):
    def pow2(m):
        start = 2.0 ** (-8.0 / m)
        return [start ** (i + 1) for i in range(m)]
    if math.log2(n).is_integer():
        return pow2(n)
    c = 2 ** math.floor(math.log2(n))
    return pow2(c) + _alibi_slope_list(2 * c)[0::2][: n - c]


def _slopes_times_dilation():
    s = np.asarray(sorted(_alibi_slope_list(N_HEADS), reverse=True), dtype=np.float32).reshape(3, HEADS_PER_GROUP)
    r = np.asarray([g[1] for g in GROUPS], dtype=np.float32)[:, None]
    return (s * r).reshape(N_HEADS)


def _sigmoid(x):
    return 0.5 * jnp.tanh(0.5 * x) + 0.5


def _dot(a, b):
    return jnp.dot(a, b, preferred_element_type=F32)


def _dot_nt(a, b):
    return lax.dot_general(a, b, (((1,), (1,)), ((), ())), preferred_element_type=F32)


def _dot_tn(a, b):
    return lax.dot_general(a, b, (((0,), (0,)), ((), ())), preferred_element_type=F32)


def _rowsum(x):
    return jnp.sum(x, axis=0, keepdims=True)


ANY_SPEC = pl.BlockSpec(memory_space=pl.ANY)


def _params(*sem):
    return pltpu.CompilerParams(dimension_semantics=sem)


def _anchored(body, n_in, dep):
    if dep is None:
        return body, [], []

    def wrapped(*refs):
        return body(*refs[:n_in], *refs[n_in + 1:])

    return wrapped, [pl.BlockSpec(memory_space=pl.ANY)], [dep]


IN_TM = 256
IN_CHUNK = 512


def _in_proj(x, g1, w_in_t, dep=None):
    T = x.shape[0]
    tm = IN_TM

    def body(x_ref, g_ref, w_hbm, h_ref, proj_ref, w_vmem, sem):
        @pl.when(pl.program_id(0) == 0)
        def _():
            cp = pltpu.make_async_copy(w_hbm, w_vmem, sem)
            cp.start()
            cp.wait()

        xv = x_ref[...]
        r = lax.rsqrt(jnp.mean(xv * xv, axis=-1, keepdims=True) + RMS_EPS)
        h = (xv * r * g_ref[...]).astype(BF16)
        h_ref[...] = h
        for lo in range(0, IN_WIDTH, IN_CHUNK):
            proj_ref[:, lo:lo + IN_CHUNK] = _dot_nt(h, w_vmem[lo:lo + IN_CHUNK, :])

    row = lambda n: pl.BlockSpec((tm, n), lambda i: (i, 0))
    body, dep_spec, dep_arg = _anchored(body, 3, dep)
    return pl.pallas_call(
        body, grid=(T // tm,),
        in_specs=[row(D_MODEL), pl.BlockSpec((1, D_MODEL), lambda i: (0, 0)), pl.BlockSpec(memory_space=pl.ANY)] + dep_spec,
        out_specs=[row(D_MODEL), row(IN_WIDTH)],
        out_shape=[SDS((T, D_MODEL), BF16), SDS((T, IN_WIDTH), F32)],
        scratch_shapes=[pltpu.VMEM((IN_WIDTH, D_MODEL), BF16), pltpu.SemaphoreType.DMA],
        compiler_params=_params("arbitrary"), name="in_proj")(x, g1, w_in_t, *dep_arg)


def _mm_tn(a, b, out_dtype, name, tn, tt=1024):
    T, K = a.shape
    N = b.shape[1]
    nt = T // tt

    def body(a_ref, b_ref, o_ref, acc):
        t = pl.program_id(1)

        @pl.when(t == 0)
        def _():
            acc[...] = jnp.zeros_like(acc)

        acc[...] += _dot_tn(a_ref[...], b_ref[...])

        @pl.when(t == nt - 1)
        def _():
            o_ref[...] = acc[...].astype(o_ref.dtype)

    return pl.pallas_call(
        body, grid=(N // tn, nt),
        in_specs=[pl.BlockSpec((tt, K), lambda j, t: (t, 0)),
                  pl.BlockSpec((tt, tn), lambda j, t: (t, j))],
        out_specs=pl.BlockSpec((K, tn), lambda j, t: (0, j)),
        out_shape=SDS((K, N), out_dtype),
        scratch_shapes=[pltpu.VMEM((K, tn), F32)],
        compiler_params=_params("parallel", "arbitrary"), name=name)(a, b)


def _gather_classes(src_ref, dst, r, row0=0):
    L = SEQ // r
    for c in range(r):
        dst[row0 + c * L:row0 + (c + 1) * L, :] = src_ref[0, pl.ds(c, L, stride=r), :].astype(dst.dtype)


def _scatter_classes(src, dst, r, row0=0):
    L = SEQ // r
    for c in range(r):
        dst[pl.ds(c, L, stride=r), :] = src[row0 + c * L:row0 + (c + 1) * L, :].astype(dst.dtype)


def _attn_masks(slope_r):
    qi = lax.broadcasted_iota(jnp.int32, (Q_BLOCK, Q_BLOCK), 0)
    kj = lax.broadcasted_iota(jnp.int32, (Q_BLOCK, Q_BLOCK), 1)
    rel = (qi - kj).astype(F32)
    bias_cur = jnp.where(qi >= kj, -slope_r * rel, NEG)
    bias_prev = jnp.where(qi <= kj, -slope_r * (rel + float(Q_BLOCK)), NEG)
    return bias_cur, bias_prev


def _store_biases(bias, sl_ref, g, hp):
    for hh in range(2):
        cur, prev = _attn_masks(sl_ref[g * HEADS_PER_GROUP + 2 * hp + hh])
        rows = slice(hh * Q_BLOCK, (hh + 1) * Q_BLOCK)
        bias[0, rows, 0:Q_BLOCK] = prev
        bias[1, rows, 0:Q_BLOCK] = jnp.full((Q_BLOCK, Q_BLOCK), NEG, F32)
        bias[0, rows, Q_BLOCK:] = cur
        bias[1, rows, Q_BLOCK:] = cur


def _transpose_pairs(src, dst):
    dst[0, :, 0:Q_BLOCK] = jnp.zeros((LANES, Q_BLOCK), dst.dtype)
    nblk = SEQ // Q_BLOCK
    for b in range(nblk):
        t = src[(b + 1) * Q_BLOCK:(b + 2) * Q_BLOCK, :].T
        dst[b, :, Q_BLOCK:] = t
        if b + 1 < nblk:
            dst[b + 1, :, 0:Q_BLOCK] = t


def _stack_heads(t, low):
    z = jnp.zeros_like(t)
    return jnp.concatenate([jnp.where(low, t, z), jnp.where(low, z, t)], axis=0)


def _unstack_heads(t2, low):
    return jnp.where(low, t2[:Q_BLOCK], t2[Q_BLOCK:])


def _unit_offsets(u, nb):
    off = pl.multiple_of(u * Q_BLOCK, Q_BLOCK)
    n = u & (nb - 1)
    c = u >> int(math.log2(nb))
    return off, n == 0, c, n


ATTN_UNROLL = 16


def _attn_fwd(qkv, slopes_r, batch, dep=None):
    nblk = SEQ // Q_BLOCK

    def body(sl_ref, *refs):
        qkv_refs = refs[:9]
        att_ref, lse_ref = refs[9:11]
        qd, kd, vd, kt, opos, lpos, bias = refs[11:]
        hp = pl.program_id(1)
        low = lax.broadcasted_iota(jnp.int32, (Q_BLOCK, LANES), 1) < HEAD_DIM

        for g in range(3):
            r = GROUPS[g][1]
            nb = SEQ // r // Q_BLOCK
            _gather_classes(qkv_refs[3 * g], qd, r)
            kd[0:Q_BLOCK, :] = jnp.zeros((Q_BLOCK, LANES), BF16)
            vd[0:Q_BLOCK, :] = jnp.zeros((Q_BLOCK, LANES), BF16)
            _gather_classes(qkv_refs[3 * g + 1], kd, r, Q_BLOCK)
            _gather_classes(qkv_refs[3 * g + 2], vd, r, Q_BLOCK)
            _transpose_pairs(kd, kt)
            _store_biases(bias, sl_ref, g, hp)

            def unit(u, carry, g=g, r=r, nb=nb):
                off, first, c, n = _unit_offsets(u, nb)
                q2 = _stack_heads(qd[pl.ds(off, Q_BLOCK), :], low)
                s = _dot(q2, kt[u]) * 0.125 + bias[first.astype(jnp.int32)]
                m = jnp.max(s, axis=-1, keepdims=True)
                p = jnp.exp(s - m)
                l = jnp.sum(p, axis=-1, keepdims=True)
                o2 = _dot(p.astype(BF16), vd[pl.ds(off, 2 * Q_BLOCK), :]) * (1.0 / l)
                lse2 = m + jnp.log(l)
                rows = pl.ds(c + n * (Q_BLOCK * r), Q_BLOCK, stride=r)
                opos[g, rows, :] = _unstack_heads(o2, low)
                lpos[g, rows, :] = jnp.where(low, lse2[:Q_BLOCK], lse2[Q_BLOCK:])
                return carry

            lax.fori_loop(0, nblk, unit, 0, unroll=ATTN_UNROLL)

        def merge(i, carry):
            rows = pl.ds(pl.multiple_of(i * 256, 256), 256)
            l0, l1, l2 = lpos[0, rows, :], lpos[1, rows, :], lpos[2, rows, :]
            m = jnp.maximum(jnp.maximum(l0, l1), l2)
            e0, e1, e2 = jnp.exp(l0 - m), jnp.exp(l1 - m), jnp.exp(l2 - m)
            den = e0 + e1 + e2
            att = (e0 * opos[0, rows, :] + e1 * opos[1, rows, :] + e2 * opos[2, rows, :]) / den
            att_ref[0, rows, :] = att.astype(att_ref.dtype)
            lse_ref[0, rows, :] = m + jnp.log(den)
            return carry

        lax.fori_loop(0, SEQ // 256, merge, 0)

    def col(sec, g):
        return pl.BlockSpec((1, SEQ, LANES), lambda b, hp: (b, 0, sec * 12 + g * 4 + hp))

    out = pl.BlockSpec((1, SEQ, LANES), lambda b, hp: (b, 0, hp))
    body, dep_spec, dep_arg = _anchored(body, 10, dep)
    return pl.pallas_call(
        body, grid=(batch, 4),
        in_specs=[pl.BlockSpec(memory_space=pltpu.SMEM)] + [col(sec, g) for g in range(3) for sec in range(3)] + dep_spec,
        out_specs=[out, out],
        out_shape=[SDS((batch, SEQ, ATTN_OUT), BF16), SDS((batch, SEQ, ATTN_OUT), F32)],
        scratch_shapes=[pltpu.VMEM((SEQ, LANES), BF16), pltpu.VMEM((Q_BLOCK + SEQ, LANES), BF16),
                        pltpu.VMEM((Q_BLOCK + SEQ, LANES), BF16), pltpu.VMEM((nblk, LANES, 2 * Q_BLOCK), BF16),
                        pltpu.VMEM((3, SEQ, LANES), F32), pltpu.VMEM((3, SEQ, LANES), F32),
                        pltpu.VMEM((2, 2 * Q_BLOCK, 2 * Q_BLOCK), F32)],
        compiler_params=_params("parallel", "parallel"), name="attn_fwd")(slopes_r, *([qkv] * 9), *dep_arg)


def _attn_bwd(qkv, datt, lse, dsum, slopes_r, batch):
    nblk = SEQ // Q_BLOCK

    def body(sl_ref, q_ref, k_ref, v_ref, do_ref, l_ref, d_ref, dq_ref, dk_ref, dv_ref,
             qd, kd, vd, dod, ld, dd, dq_acc, dk_acc, dv_acc, dk_part, dv_part, stage, bias):
        gid, hp = pl.program_id(1), pl.program_id(2)
        low = lax.broadcasted_iota(jnp.int32, (Q_BLOCK, LANES), 1) < HEAD_DIM

        def section(g):
            r = GROUPS[g][1]
            nb = SEQ // r // Q_BLOCK
            _gather_classes(q_ref, qd, r)
            kd[0:Q_BLOCK, :] = jnp.zeros((Q_BLOCK, LANES), BF16)
            vd[0:Q_BLOCK, :] = jnp.zeros((Q_BLOCK, LANES), BF16)
            _gather_classes(k_ref, kd, r, Q_BLOCK)
            _gather_classes(v_ref, vd, r, Q_BLOCK)
            _gather_classes(do_ref, dod, r)
            _gather_classes(l_ref, ld, r)
            _gather_classes(d_ref, dd, r)
            _store_biases(bias, sl_ref, g, hp)

            def unit(u, carry):
                off, first, _, _ = _unit_offsets(u, nb)
                pair = pl.ds(off, 2 * Q_BLOCK)
                q2 = _stack_heads(qd[pl.ds(off, Q_BLOCK), :], low)
                do2 = _stack_heads(dod[pl.ds(off, Q_BLOCK), :], low)
                lse_t = ld[pl.ds(off, Q_BLOCK), :]
                dsum_t = dd[pl.ds(off, Q_BLOCK), :]
                lse2 = jnp.concatenate([lse_t[:, 0:1], lse_t[:, HEAD_DIM:HEAD_DIM + 1]], axis=0)
                dsum2 = jnp.concatenate([dsum_t[:, 0:1], dsum_t[:, HEAD_DIM:HEAD_DIM + 1]], axis=0)
                s = _dot_nt(q2, kd[pair, :]) * 0.125 + bias[first.astype(jnp.int32)]
                p = jnp.exp(s - lse2)
                ds = (p * (_dot_nt(do2, vd[pair, :]) - dsum2)).astype(BF16)
                dq_acc[pl.ds(off, Q_BLOCK), :] = _unstack_heads(_dot(ds, kd[pair, :]), low) * 0.125
                dk_part[u] = _dot_tn(ds, q2) * 0.125
                dv_part[u] = _dot_tn(p.astype(BF16), do2)
                return carry

            lax.fori_loop(0, nblk, unit, 0, unroll=ATTN_UNROLL)
            for part, acc in ((dk_part, dk_acc), (dv_part, dv_acc)):
                for b in range(nblk):
                    t = part[b, Q_BLOCK:, :]
                    if b + 1 < nblk:
                        t = t + part[b + 1, 0:Q_BLOCK, :]
                    acc[b * Q_BLOCK:(b + 1) * Q_BLOCK, :] = t
            for acc, out_ref in ((dq_acc, dq_ref), (dk_acc, dk_ref), (dv_acc, dv_ref)):
                _scatter_classes(acc, stage, r)
                out_ref[0] = stage[...].astype(out_ref.dtype)

        for g in range(3):
            pl.when(gid == g)(lambda g=g: section(g))

    def col(sec):
        return pl.BlockSpec((1, SEQ, LANES), lambda b, g, hp: (b, 0, sec * 12 + g * 4 + hp))

    pos = pl.BlockSpec((1, SEQ, LANES), lambda b, g, hp: (b, 0, hp))
    dout = pl.BlockSpec((1, SEQ, LANES), lambda b, g, hp: (b, 0, g * 4 + hp))
    out = SDS((batch, SEQ, ATTN_WIDTH), BF16)
    seq_bf = pltpu.VMEM((SEQ, LANES), BF16)
    seq_f = pltpu.VMEM((SEQ, LANES), F32)
    pad_bf = pltpu.VMEM((Q_BLOCK + SEQ, LANES), BF16)
    part = pltpu.VMEM((nblk, 2 * Q_BLOCK, LANES), F32)
    return pl.pallas_call(
        body, grid=(batch, 3, 4),
        in_specs=[pl.BlockSpec(memory_space=pltpu.SMEM), col(0), col(1), col(2), pos, pos, pos],
        out_specs=[dout, dout, dout],
        out_shape=[out, out, out],
        scratch_shapes=[seq_bf, pad_bf, pad_bf, seq_bf, seq_f, seq_f, seq_f, seq_f, seq_f, part, part, seq_f,
                        pltpu.VMEM((2, 2 * Q_BLOCK, 2 * Q_BLOCK), F32)],
        compiler_params=_params("parallel", "parallel", "parallel"), name="attn_bwd")(
            slopes_r, qkv, qkv, qkv, datt, lse, dsum)


CONV_TC = 128
U_BLOCK0 = 3 * ATTN_WIDTH // CONV_TC
CONV_ROWS = 128
SUBLANES = 8


SHIFT_TAIL = CONV_PAD - SUBLANES
CONV_CHUNKS = SEQ // CONV_ROWS


def _fill_shifted_rows(sh, c):
    lo = c * CONV_ROWS + (SHIFT_TAIL if c else 0)
    hi = (c + 1) * CONV_ROWS + SHIFT_TAIL
    for s in range(1, SUBLANES):
        sh[s, lo:hi, :] = sh[0, lo + s:hi + s, :]


def _tap(sh, base, offset):
    s = offset % SUBLANES
    lo = base + offset - s
    return sh[s, lo:lo + CONV_ROWS, :]


def _conv_fwd(u, conv_w, conv_b, batch, dep=None):
    nct = D_MODEL // CONV_TC

    def body(ua_ref, ub_ref, w_ref, b_ref, o_ref, sh):
        sh[0, 0:CONV_PAD, :] = jnp.zeros((CONV_PAD, CONV_TC), F32)
        for c in range(CONV_CHUNKS):
            base = c * CONV_ROWS
            rows = slice(base, base + CONV_ROWS)
            sh[0, CONV_PAD + base:CONV_PAD + base + CONV_ROWS, :] = ua_ref[0, rows, :] * _sigmoid(ub_ref[0, rows, :])
            _fill_shifted_rows(sh, c)
            acc = jnp.broadcast_to(b_ref[...], (CONV_ROWS, CONV_TC))
            for t in range(CONV_K):
                acc = acc + _tap(sh, base, t + CONV_PAD - (CONV_K - 1)) * w_ref[t:t + 1, :]
            o_ref[0, rows, :] = acc

    body, dep_spec, dep_arg = _anchored(body, 4, dep)
    return pl.pallas_call(
        body, grid=(nct, batch),
        in_specs=[pl.BlockSpec((1, SEQ, CONV_TC), lambda j, b: (b, 0, U_BLOCK0 + j)),
                  pl.BlockSpec((1, SEQ, CONV_TC), lambda j, b: (b, 0, U_BLOCK0 + nct + j)),
                  pl.BlockSpec((CONV_PAD, CONV_TC), lambda j, b: (0, j)),
                  pl.BlockSpec((1, CONV_TC), lambda j, b: (0, j))] + dep_spec,
        out_specs=pl.BlockSpec((1, SEQ, CONV_TC), lambda j, b: (b, 0, j)),
        out_shape=SDS((batch, SEQ, D_MODEL), F32),
        scratch_shapes=[pltpu.VMEM((SUBLANES, SEQ + CONV_PAD, CONV_TC), F32)],
        compiler_params=_params("parallel", "parallel"), name="conv_fwd")(u, u, conv_w, conv_b, *dep_arg)


def _conv_bwd(u, dc1, conv_w, batch, dep=None):
    nct = D_MODEL // CONV_TC

    def body(ua_ref, ub_ref, d_ref, w_ref, dua_ref, dub_ref, gw_ref, gb_ref, shc, shd, gacc):
        b = pl.program_id(1)
        shc[0, 0:CONV_PAD, :] = jnp.zeros((CONV_PAD, CONV_TC), F32)
        shd[0, 0:SEQ, :] = d_ref[0]
        shd[0, SEQ:, :] = jnp.zeros((CONV_PAD, CONV_TC), F32)

        @pl.when(b == 0)
        def _():
            gacc[...] = jnp.zeros_like(gacc)
            gb_ref[...] = jnp.zeros_like(gb_ref)

        gb_ref[...] += _rowsum(d_ref[0])

        for c in range(CONV_CHUNKS):
            base = c * CONV_ROWS
            rows = slice(base, base + CONV_ROWS)
            ua = ua_ref[0, rows, :]
            sg = _sigmoid(ub_ref[0, rows, :])
            shc[0, CONV_PAD + base:CONV_PAD + base + CONV_ROWS, :] = ua * sg
            _fill_shifted_rows(shc, c)
            _fill_shifted_rows(shd, c)
            dcur = shd[0, rows, :]
            acc = jnp.zeros((CONV_ROWS, CONV_TC), F32)
            for t in range(CONV_K):
                acc = acc + _tap(shd, base, CONV_K - 1 - t) * w_ref[t:t + 1, :]
                prod = _tap(shc, base, t + CONV_PAD - (CONV_K - 1)) * dcur
                gacc[t] += jnp.sum(prod.reshape(CONV_ROWS // 8, 8, CONV_TC), axis=0)
            dua_ref[0, rows, :] = (acc * sg).astype(dua_ref.dtype)
            dub_ref[0, rows, :] = (acc * ua * sg * (1.0 - sg)).astype(dub_ref.dtype)

        @pl.when(b == batch - 1)
        def _():
            for t in range(CONV_K):
                gw_ref[t:t + 1, :] = jnp.sum(gacc[t], axis=0, keepdims=True)
            gw_ref[CONV_K:CONV_PAD, :] = jnp.zeros((CONV_PAD - CONV_K, CONV_TC), F32)

    du = SDS((batch, SEQ, D_MODEL), BF16)
    body, dep_spec, dep_arg = _anchored(body, 4, dep)
    return pl.pallas_call(
        body, grid=(nct, batch),
        in_specs=[pl.BlockSpec((1, SEQ, CONV_TC), lambda j, b: (b, 0, U_BLOCK0 + j)),
                  pl.BlockSpec((1, SEQ, CONV_TC), lambda j, b: (b, 0, U_BLOCK0 + nct + j)),
                  pl.BlockSpec((1, SEQ, CONV_TC), lambda j, b: (b, 0, j)),
                  pl.BlockSpec((CONV_PAD, CONV_TC), lambda j, b: (0, j))] + dep_spec,
        out_specs=[pl.BlockSpec((1, SEQ, CONV_TC), lambda j, b: (b, 0, j)),
                   pl.BlockSpec((1, SEQ, CONV_TC), lambda j, b: (b, 0, j)),
                   pl.BlockSpec((CONV_PAD, CONV_TC), lambda j, b: (0, j)),
                   pl.BlockSpec((1, CONV_TC), lambda j, b: (0, j))],
        out_shape=[du, du, SDS((CONV_PAD, D_MODEL), F32), SDS((1, D_MODEL), F32)],
        scratch_shapes=[pltpu.VMEM((SUBLANES, SEQ + CONV_PAD, CONV_TC), F32),
                        pltpu.VMEM((SUBLANES, SEQ + CONV_PAD, CONV_TC), F32),
                        pltpu.VMEM((CONV_K, 8, CONV_TC), F32)],
        compiler_params=_params("parallel", "arbitrary"), name="conv_bwd")(u, u, dc1, conv_w, *dep_arg)


MID_TM = 256


def _layernorm_stats(c1):
    mu = jnp.mean(c1, axis=-1, keepdims=True)
    cen = c1 - mu
    rs = lax.rsqrt(jnp.mean(cen * cen, axis=-1, keepdims=True) + LN_EPS)
    return cen * rs, rs


GATE_PARTS = 4
GATE_PART = 2 * D_MODEL // GATE_PARTS
GATE_PART0 = (IN_WIDTH - 2 * D_MODEL) // GATE_PART


def _gate_specs(tm):
    return [pl.BlockSpec((tm, GATE_PART), lambda i, k=k: (i, GATE_PART0 + k)) for k in range(GATE_PARTS)]


def _mid_fwd(att, c1, proj, x, w_a, w_c, w_o, gate_b, ln_g, ln_b, g2, dep=None):
    T = x.shape[0]
    tm = MID_TM

    def body(att_ref, c1_ref, lg0, lg1, lg2, lg3, x_ref, wa_ref, wc_ref, wo_ref, gb_ref, lng_ref, lnb_ref, g2_ref,
             c3_ref, ya_ref, yc_ref, mix_ref, x1_ref, h2_ref):
        logits = jnp.concatenate([lg0[...], lg1[...], lg2[...], lg3[...]], axis=1)
        ya = _dot(att_ref[...], wa_ref[...])
        xh, _ = _layernorm_stats(c1_ref[...])
        c2 = xh * lng_ref[...] + lnb_ref[...]
        c3 = (c2 * _sigmoid(c2)).astype(BF16)
        c3_ref[...] = c3
        yc = _dot(c3, wc_ref[...])
        gates = _sigmoid(logits + gb_ref[...])
        mix = (gates[:, :D_MODEL] * ya + gates[:, D_MODEL:] * yc).astype(BF16)
        ya_ref[...] = ya.astype(BF16)
        yc_ref[...] = yc.astype(BF16)
        mix_ref[...] = mix
        x1 = x_ref[...] + _dot(mix, wo_ref[...])
        x1_ref[...] = x1
        r = lax.rsqrt(jnp.mean(x1 * x1, axis=-1, keepdims=True) + RMS_EPS)
        h2_ref[...] = (x1 * r * g2_ref[...]).astype(BF16)

    row = lambda n: pl.BlockSpec((tm, n), lambda i: (i, 0))
    full = lambda a, b: pl.BlockSpec((a, b), lambda i: (0, 0))
    body, dep_spec, dep_arg = _anchored(body, 10 + GATE_PARTS, dep)
    return pl.pallas_call(
        body, grid=(T // tm,),
        in_specs=[row(ATTN_OUT), row(D_MODEL)] + _gate_specs(tm) + [row(D_MODEL),
                  full(ATTN_OUT, D_MODEL), full(D_MODEL, D_MODEL), full(D_MODEL, D_MODEL),
                  full(1, 2 * D_MODEL), full(1, D_MODEL), full(1, D_MODEL), full(1, D_MODEL)] + dep_spec,
        out_specs=[row(D_MODEL), row(D_MODEL), row(D_MODEL), row(D_MODEL), row(D_MODEL), row(D_MODEL)],
        out_shape=[SDS((T, D_MODEL), BF16), SDS((T, D_MODEL), BF16), SDS((T, D_MODEL), BF16), SDS((T, D_MODEL), BF16),
                   SDS((T, D_MODEL), F32), SDS((T, D_MODEL), BF16)],
        compiler_params=_params("parallel"), name="mid_fwd")(att, c1, *([proj] * GATE_PARTS), x, w_a, w_c, w_o,
                                                             gate_b, ln_g, ln_b, g2, *dep_arg)


def _mid_bwd(dx1b, ya, yc, proj, att, c1, w_a, w_c, w_o, gate_b, ln_g, ln_b, head_ones, dep=None):
    T = dx1b.shape[0]
    tm = MID_TM

    def body(dx_ref, ya_ref, yc_ref, lg0, lg1, lg2, lg3, att_ref, c1_ref, wa_ref, wc_ref, wo_ref, gb_ref, lng_ref,
             lnb_ref, e_ref, dlg_ref, dya_ref, dyc_ref, datt_ref, dsum_ref, dc1_ref, ggb_ref, glg_ref, glb_ref):
        logits = jnp.concatenate([lg0[...], lg1[...], lg2[...], lg3[...]], axis=1)
        @pl.when(pl.program_id(0) == 0)
        def _():
            ggb_ref[...] = jnp.zeros_like(ggb_ref)
            glg_ref[...] = jnp.zeros_like(glg_ref)
            glb_ref[...] = jnp.zeros_like(glb_ref)

        dmix = _dot_nt(dx_ref[...], wo_ref[...])
        gates = _sigmoid(logits + gb_ref[...])
        ga, gc = gates[:, :D_MODEL], gates[:, D_MODEL:]
        dla = dmix * ya_ref[...].astype(F32) * ga * (1.0 - ga)
        dlc = dmix * yc_ref[...].astype(F32) * gc * (1.0 - gc)
        dlg_ref[:, :D_MODEL] = dla.astype(BF16)
        dlg_ref[:, D_MODEL:] = dlc.astype(BF16)
        ggb_ref[:, :D_MODEL] += _rowsum(dla)
        ggb_ref[:, D_MODEL:] += _rowsum(dlc)
        dya = (dmix * ga).astype(BF16)
        dyc = (dmix * gc).astype(BF16)
        dya_ref[...] = dya
        dyc_ref[...] = dyc
        datt = _dot_nt(dya, wa_ref[...])
        datt_ref[...] = datt
        dsum_ref[...] = jnp.dot(datt * att_ref[...].astype(F32), e_ref[...], preferred_element_type=F32,
                                precision=lax.Precision.HIGHEST)
        dc3 = _dot_nt(dyc, wc_ref[...])
        xh, rs = _layernorm_stats(c1_ref[...])
        c2 = xh * lng_ref[...] + lnb_ref[...]
        sg = _sigmoid(c2)
        dc2 = dc3 * (sg * (1.0 + c2 * (1.0 - sg)))
        glg_ref[...] += _rowsum(dc2 * xh)
        glb_ref[...] += _rowsum(dc2)
        dxh = dc2 * lng_ref[...]
        dc1_ref[...] = rs * (dxh - jnp.mean(dxh, axis=-1, keepdims=True) - xh * jnp.mean(dxh * xh, axis=-1, keepdims=True))

    row = lambda n: pl.BlockSpec((tm, n), lambda i: (i, 0))
    full = lambda a, b: pl.BlockSpec((a, b), lambda i: (0, 0))
    body, dep_spec, dep_arg = _anchored(body, 12 + GATE_PARTS, dep)
    return pl.pallas_call(
        body, grid=(T // tm,),
        in_specs=[row(D_MODEL), row(D_MODEL), row(D_MODEL)] + _gate_specs(tm) + [row(ATTN_OUT), row(D_MODEL),
                  full(ATTN_OUT, D_MODEL), full(D_MODEL, D_MODEL), full(D_MODEL, D_MODEL),
                  full(1, 2 * D_MODEL), full(1, D_MODEL), full(1, D_MODEL), full(ATTN_OUT, ATTN_OUT)] + dep_spec,
        out_specs=[row(2 * D_MODEL), row(D_MODEL), row(D_MODEL), row(ATTN_OUT), row(ATTN_OUT), row(D_MODEL),
                   full(1, 2 * D_MODEL), full(1, D_MODEL), full(1, D_MODEL)],
        out_shape=[SDS((T, 2 * D_MODEL), BF16), SDS((T, D_MODEL), BF16), SDS((T, D_MODEL), BF16), SDS((T, ATTN_OUT), F32),
                   SDS((T, ATTN_OUT), F32), SDS((T, D_MODEL), F32),
                   SDS((1, 2 * D_MODEL), F32), SDS((1, D_MODEL), F32), SDS((1, D_MODEL), F32)],
        compiler_params=_params("arbitrary"), name="mid_bwd")(dx1b, ya, yc, *([proj] * GATE_PARTS), att, c1, w_a, w_c, w_o,
                                                               gate_b, ln_g, ln_b, head_ones, *dep_arg)


FFN_TM = 256
FFN_CHUNK = 512
FFN_SUB = tuple((lo, min(lo + FFN_CHUNK, D_FF)) for lo in range(0, D_FF, FFN_CHUNK))


def _rms_bwd(dy_times_g, xh, r):
    return r * (dy_times_g - xh * jnp.mean(dy_times_g * xh, axis=-1, keepdims=True))


def _load_resident(pairs, sems):
    @pl.when(pl.program_id(0) == 0)
    def _():
        copies = [pltpu.make_async_copy(src, dst, sems.at[k]) for k, (src, dst) in enumerate(pairs)]
        for cp in copies:
            cp.start()
        for cp in copies:
            cp.wait()


def _ffn_fwd(h2, x1, target, gf, w_g_t, w_u_t, w_d):
    T = h2.shape[0]
    tm = FFN_TM

    def body(h_ref, x1_ref, t_ref, gf_ref, wg_hbm, wu_hbm, wd_hbm,
             a_ref, b_ref, f_ref, dx2_ref, dx2b_ref, loss_ref, gnf_ref, wg, wu, wd, sems):
        _load_resident(((wg_hbm, wg), (wu_hbm, wu), (wd_hbm, wd)), sems)

        @pl.when(pl.program_id(0) == 0)
        def _():
            loss_ref[...] = jnp.zeros_like(loss_ref)
            gnf_ref[...] = jnp.zeros_like(gnf_ref)

        h = h_ref[...]
        x2 = x1_ref[...]
        for lo, hi in FFN_SUB:
            a = _dot_nt(h, wg[lo:hi, :])
            b = _dot_nt(h, wu[lo:hi, :])
            f = (a * _sigmoid(a) * b).astype(BF16)
            a_ref[:, lo:hi] = a.astype(BF16)
            b_ref[:, lo:hi] = b.astype(BF16)
            f_ref[:, lo:hi] = f
            x2 = x2 + _dot(f, wd[lo:hi, :])

        r = lax.rsqrt(jnp.mean(x2 * x2, axis=-1, keepdims=True) + RMS_EPS)
        xh = x2 * r
        err = xh * gf_ref[...] - t_ref[...]
        loss_ref[...] += (0.5 / D_MODEL) * jnp.sum(err * err)
        dy = err * (1.0 / D_MODEL)
        gnf_ref[...] += _rowsum(dy * xh)
        dx2 = _rms_bwd(dy * gf_ref[...], xh, r)
        dx2_ref[...] = dx2
        dx2b_ref[...] = dx2.astype(BF16)

    row = lambda n: pl.BlockSpec((tm, n), lambda i: (i, 0))
    const = lambda n: pl.BlockSpec((1, n), lambda i: (0, 0))
    wshape = pltpu.VMEM((D_FF, D_MODEL), BF16)
    return pl.pallas_call(
        body, grid=(T // tm,),
        in_specs=[row(D_MODEL), row(D_MODEL), row(D_MODEL), const(D_MODEL), ANY_SPEC, ANY_SPEC, ANY_SPEC],
        out_specs=[row(D_FF), row(D_FF), row(D_FF), row(D_MODEL), row(D_MODEL), const(128), const(D_MODEL)],
        out_shape=[SDS((T, D_FF), BF16), SDS((T, D_FF), BF16), SDS((T, D_FF), BF16), SDS((T, D_MODEL), F32),
                   SDS((T, D_MODEL), BF16), SDS((1, 128), F32), SDS((1, D_MODEL), F32)],
        scratch_shapes=[wshape, wshape, wshape, pltpu.SemaphoreType.DMA((3,))],
        compiler_params=_params("arbitrary"), name="ffn_fwd")(h2, x1, target, gf, w_g_t, w_u_t, w_d)


def _ffn_bwd(dx2b, dx2, a, b, x1, g2, w_g_t, w_u_t, w_d):
    T = dx2.shape[0]
    tm = FFN_TM

    def body(dxb_ref, dx2_ref, a_ref, b_ref, x1_ref, g2_ref, wg_hbm, wu_hbm, wd_hbm,
             da_ref, db_ref, dx1_ref, dx1b_ref, gn2_ref, wg, wu, wd, sems):
        _load_resident(((wg_hbm, wg), (wu_hbm, wu), (wd_hbm, wd)), sems)

        @pl.when(pl.program_id(0) == 0)
        def _():
            gn2_ref[...] = jnp.zeros_like(gn2_ref)

        dxb = dxb_ref[...]
        dh2 = jnp.zeros((tm, D_MODEL), F32)
        for lo, hi in FFN_SUB:
            df = _dot_nt(dxb, wd[lo:hi, :])
            av = a_ref[:, lo:hi].astype(F32)
            bv = b_ref[:, lo:hi].astype(F32)
            sg = _sigmoid(av)
            db = (df * av * sg).astype(BF16)
            da = (df * bv * (sg * (1.0 + av * (1.0 - sg)))).astype(BF16)
            da_ref[:, lo:hi] = da
            db_ref[:, lo:hi] = db
            dh2 = dh2 + _dot(da, wg[lo:hi, :]) + _dot(db, wu[lo:hi, :])

        x1 = x1_ref[...]
        r = lax.rsqrt(jnp.mean(x1 * x1, axis=-1, keepdims=True) + RMS_EPS)
        xh = x1 * r
        gn2_ref[...] += _rowsum(dh2 * xh)
        dx1 = dx2_ref[...] + _rms_bwd(dh2 * g2_ref[...], xh, r)
        dx1_ref[...] = dx1
        dx1b_ref[...] = dx1.astype(BF16)

    row = lambda n: pl.BlockSpec((tm, n), lambda i: (i, 0))
    const = lambda n: pl.BlockSpec((1, n), lambda i: (0, 0))
    wshape = pltpu.VMEM((D_FF, D_MODEL), BF16)
    return pl.pallas_call(
        body, grid=(T // tm,),
        in_specs=[row(D_MODEL), row(D_MODEL), row(D_FF), row(D_FF), row(D_MODEL), const(D_MODEL),
                  ANY_SPEC, ANY_SPEC, ANY_SPEC],
        out_specs=[row(D_FF), row(D_FF), row(D_MODEL), row(D_MODEL), const(D_MODEL)],
        out_shape=[SDS((T, D_FF), BF16), SDS((T, D_FF), BF16), SDS((T, D_MODEL), F32), SDS((T, D_MODEL), BF16),
                   SDS((1, D_MODEL), F32)],
        scratch_shapes=[wshape, wshape, wshape, pltpu.SemaphoreType.DMA((3,))],
        compiler_params=_params("arbitrary"), name="ffn_bwd")(dx2b, dx2, a, b, x1, g2, w_g_t, w_u_t, w_d)


def _in_bwd(pieces, w_in_t, x, dx1, g1, dep=None):
    T = x.shape[0]
    tm = IN_TM
    npc = len(pieces)
    assert sum(p.shape[1] for p in pieces) == IN_WIDTH

    def body(*refs):
        p_refs = refs[:npc]
        w_hbm, x_ref, dx1_ref, g_ref, dx_ref, gn1_ref, w_vmem, sem = refs[npc:]

        @pl.when(pl.program_id(0) == 0)
        def _():
            cp = pltpu.make_async_copy(w_hbm, w_vmem, sem)
            cp.start()
            cp.wait()
            gn1_ref[...] = jnp.zeros_like(gn1_ref)

        dh = jnp.zeros((tm, D_MODEL), F32)
        col = 0
        for p_ref in p_refs:
            for j in range(p_ref.shape[1] // IN_CHUNK):
                dh = dh + _dot(p_ref[:, j * IN_CHUNK:(j + 1) * IN_CHUNK], w_vmem[col:col + IN_CHUNK, :])
                col += IN_CHUNK
        xv = x_ref[...]
        r = lax.rsqrt(jnp.mean(xv * xv, axis=-1, keepdims=True) + RMS_EPS)
        xh = xv * r
        gn1_ref[...] += _rowsum(dh * xh)
        dx_ref[...] = dx1_ref[...] + _rms_bwd(dh * g_ref[...], xh, r)

    row = lambda n: pl.BlockSpec((tm, n), lambda i: (i, 0))
    body, dep_spec, dep_arg = _anchored(body, npc + 4, dep)
    return pl.pallas_call(
        body, grid=(T // tm,),
        in_specs=[row(p.shape[1]) for p in pieces]
        + [pl.BlockSpec(memory_space=pl.ANY), row(D_MODEL), row(D_MODEL), pl.BlockSpec((1, D_MODEL), lambda i: (0, 0))]
        + dep_spec,
        out_specs=[row(D_MODEL), pl.BlockSpec((1, D_MODEL), lambda i: (0, 0))],
        out_shape=[SDS((T, D_MODEL), F32), SDS((1, D_MODEL), F32)],
        scratch_shapes=[pltpu.VMEM((IN_WIDTH, D_MODEL), BF16), pltpu.SemaphoreType.DMA],
        compiler_params=_params("arbitrary"), name="in_bwd")(*pieces, w_in_t, x, dx1, g1, *dep_arg)


def _local_step(x, target, in_proj, small, late_weights=None, emit=None):
    T = x.shape[0]
    batch = T // SEQ
    slopes_r = jnp.asarray(_slopes_times_dilation())
    emit = emit or (lambda names, grads: None)

    h, proj, w = in_proj()
    proj3 = proj.reshape(batch, SEQ, IN_WIDTH)

    att, lse = _attn_fwd(proj3, slopes_r, batch, w.get("token"))
    att = att.reshape(T, ATTN_OUT)
    if late_weights is not None:
        w = {**w, **late_weights("after_attention", att)}

    c1 = _conv_fwd(proj3, w["conv_w"], small["conv_b"], batch, w.get("token")).reshape(T, D_MODEL)
    if late_weights is not None:
        w = {**w, **late_weights(LATE_MERGE, (att, c1))}

    c3, ya, yc, mix, x1, h2 = _mid_fwd(
        att, c1, proj, x, w["w_attn_out"], w["w_conv_out"], w["w_o"],
        small["gate_b"], small["conv_ln_g"], small["conv_ln_b"], small["norm2_g"], w.get("token"))
    if late_weights is not None:
        w = {**w, **late_weights(LATE_FFN, h2)}

    a, b, f, dx2, dx2b, loss, g_normf = _ffn_fwd(h2, x1, target, small["norm_f_g"],
                                                   w["w_ffn_gate"], w["w_ffn_up"], w["w_ffn_down"])

    da, db, dx1, dx1b, g_norm2 = _ffn_bwd(dx2b, dx2, a, b, x1, small["norm2_g"],
                                           w["w_ffn_gate"], w["w_ffn_up"], w["w_ffn_down"])
    gw = {}
    gw["w_ffn_down"] = _mm_tn(f, dx2b, BF16, "gw_ffn_down", tn=1024)
    gw["w_ffn_gate"] = _mm_tn(da, h2, BF16, "gw_ffn_gate", tn=1024)
    gw["w_ffn_up"] = _mm_tn(db, h2, BF16, "gw_ffn_up", tn=1024)
    token = emit(("w_ffn_gate", "w_ffn_up", "w_ffn_down"), gw)

    head_ones = jnp.asarray(np.kron(np.eye(HEADS_PER_GROUP, dtype=np.float32), np.ones((HEAD_DIM, HEAD_DIM), np.float32)))
    dlogits, dya, dyc, datt, dsum, dc1, g_gate_b, g_ln_g, g_ln_b = _mid_bwd(
        dx1b, ya, yc, proj, att, c1, w["w_attn_out"], w["w_conv_out"], w["w_o"],
        small["gate_b"], small["conv_ln_g"], small["conv_ln_b"], head_ones, token)
    gw["w_o"] = _mm_tn(mix, dx1b, BF16, "gw_o", tn=1024)
    gw["w_attn_out"] = _mm_tn(att, dya, BF16, "gw_attn_out", tn=1024)
    gw["w_conv_out"] = _mm_tn(c3, dyc, BF16, "gw_conv_out", tn=1024)
    token = emit(("w_conv_out", "w_attn_out", "w_o"), gw)

    dua, dub, g_conv_w, g_conv_b = _conv_bwd(proj3, dc1.reshape(batch, SEQ, D_MODEL), w["conv_w"], batch, token)

    dq, dk, dv = _attn_bwd(proj3, datt.reshape(batch, SEQ, ATTN_OUT), lse, dsum.reshape(batch, SEQ, ATTN_OUT),
                           slopes_r, batch)
    pieces = [dq.reshape(T, ATTN_WIDTH), dk.reshape(T, ATTN_WIDTH), dv.reshape(T, ATTN_WIDTH),
              dua.reshape(T, D_MODEL), dub.reshape(T, D_MODEL), dlogits]

    names = ("q", "k", "v", "ua", "ub", "gate")
    gw["w_in"] = jnp.concatenate([_mm_tn(p, h, BF16, "gw_in_" + nm, tn=1024) for nm, p in zip(names, pieces)], axis=0)
    gw["conv_w"] = g_conv_w
    token = emit(("w_in", "conv_w"), gw)
    grad_x, g_norm1 = _in_bwd(pieces, w["w_in"], x, dx1, small["norm1_g"], token)

    gsmall = {"norm1_g": g_norm1, "gate_b": g_gate_b, "conv_b": g_conv_b, "conv_ln_g": g_ln_g, "conv_ln_b": g_ln_b,
              "norm2_g": g_norm2, "norm_f_g": g_normf}
    return loss, grad_x, gw, gsmall


ANY = pl.BlockSpec(memory_space=pl.ANY)


def _all_gather(arrs):
    n = len(arrs)

    def body(*refs):
        ins, outs = refs[:n], refs[n:2 * n]
        send_sems, recv_sems, local_sems = refs[2 * n:]
        x, y, c = lax.axis_index("x"), lax.axis_index("y"), lax.axis_index("c")
        me, sibling = (x, y, c), (x, y, 1 - c)
        chips = [(1 - x, y), (x, 1 - y), (1 - x, 1 - y)]

        def copy(a, k, block, to, src=None):
            px, py, pc = block
            dst = outs[a].at[4 * px + 2 * py + pc]
            return pltpu.make_async_remote_copy(
                src_ref=dst if src is None else src, dst_ref=dst,
                send_sem=send_sems.at[a, k], recv_sem=recv_sems.at[a, k], device_id=to, device_id_type=MESH)

        mine = [pltpu.make_async_copy(ins[a], outs[a].at[4 * x + 2 * y + c], local_sems.at[a]) for a in range(n)]
        for cp in mine:
            cp.start()
        first = []
        for j, chip in enumerate(chips):
            first += [copy(a, 1 + j, me, (*chip, c), src=ins[a]) for a in range(n)]
        first += [copy(a, 0, me, sibling, src=ins[a]) for a in range(n)]
        for cp in first:
            cp.start()
        passed = []
        for j, chip in enumerate(chips):
            for a in range(n):
                copy(a, 1 + j, (*chip, c), me).wait_recv()
                cp = copy(a, 4 + j, (*chip, c), sibling)
                cp.start()
                passed.append(cp)
        for a in range(n):
            copy(a, 0, sibling, me).wait_recv()
        for j, chip in enumerate(chips):
            for a in range(n):
                copy(a, 4 + j, (*chip, 1 - c), me).wait_recv()
        for cp in first + passed:
            cp.wait_send()
        for cp in mine:
            cp.wait()

    return pl.pallas_call(
        body, in_specs=[ANY] * n, out_specs=[ANY] * n,
        out_shape=[SDS((N_DEV,) + a.shape, a.dtype) for a in arrs],
        scratch_shapes=[pltpu.SemaphoreType.DMA((n, 7)), pltpu.SemaphoreType.DMA((n, 7)), pltpu.SemaphoreType.DMA((n,))],
        name="all_gather_weights")(*arrs)


HBM =pl.BlockSpec(memory_space=pltpu.HBM)
SEM = pl.BlockSpec(memory_space=pltpu.SEMAPHORE)
ALL_PEERS = tuple(range(1, N_DEV))
OTHER_CHIPS = (2, 4, 6)
SPLIT_EFFECT = pltpu.CompilerParams(has_side_effects=pltpu.SideEffectType.DATAFLOW_SIDE_EFFECTING)


def _exchange_copies(mode, ks, srcs, lands, send_sems, recv_sems):
    x, y, c = lax.axis_index("x"), lax.axis_index("y"), lax.axis_index("c")
    me = 4 * x + 2 * y + c
    send, recv = [], []
    for a in range(len(lands)):
        for i, k in enumerate(ks):
            peer = (x ^ ((k >> 2) & 1), y ^ ((k >> 1) & 1), c ^ (k & 1))
            pidx = 4 * peer[0] + 2 * peer[1] + peer[2]
            if mode == "gather":
                src, to, out_slot, in_slot = srcs[a], peer, me, pidx
            elif mode == "scatter":
                src, to, out_slot, in_slot = srcs[a].at[pidx], peer, me, pidx
            elif mode == "chip_scatter":
                src, to, out_slot, in_slot = srcs[a].at[pidx >> 1], peer, me >> 1, pidx >> 1
            else:
                src, to, out_slot, in_slot = lands[a].at[pidx], (x, y, 1 - c), pidx, pidx ^ 1
            s = a * len(ks) + i
            send.append(pltpu.make_async_remote_copy(
                src_ref=src, dst_ref=lands[a].at[out_slot], send_sem=send_sems.at[s], recv_sem=recv_sems.at[s],
                device_id=to, device_id_type=MESH))
            recv.append(pltpu.make_async_remote_copy(
                src_ref=src, dst_ref=lands[a].at[in_slot], send_sem=send_sems.at[s], recv_sem=recv_sems.at[s],
                device_id=to, device_id_type=MESH))
    return send, recv


def _send_start(mode, ks, name, srcs=(), lands=None, dep=None):
    srcs = list(srcs)
    if lands is None:
        slots = 4 if mode == "chip_scatter" else N_DEV
        lands = [lax.empty((slots,) + (s.shape if mode == "gather" else s.shape[1:]), s.dtype) for s in srcs]
    ns, nl = len(srcs), len(lands)
    nsem = nl * len(ks)

    def body(*refs):
        send, _ = _exchange_copies(mode, ks, refs[:ns], refs[ns:ns + nl], refs[ns + nl], refs[ns + nl + 1])
        for cp in send:
            cp.start()
        token = refs[-1]
        token[...] = jnp.zeros_like(token)

    both = srcs + list(lands)
    body, dep_spec, dep_arg = _anchored(body, ns + nl, dep)
    res = pl.pallas_call(
        body, name=name,
        out_shape=(pltpu.SemaphoreType.DMA((nsem,)), pltpu.SemaphoreType.DMA((nsem,)),
                   *[pltpu.HBM(a.shape, a.dtype) for a in both], SDS((8, 128), F32)),
        in_specs=[HBM] * (ns + nl) + dep_spec,
        out_specs=(SEM, SEM, *([HBM] * (ns + nl)), pl.BlockSpec(memory_space=pltpu.VMEM)),
        input_output_aliases={i: 2 + i for i in range(ns + nl)}, compiler_params=SPLIT_EFFECT,
    )(*[pltpu.with_memory_space_constraint(a, pltpu.HBM) for a in both], *dep_arg)
    return dict(mode=mode, ks=ks, send_sems=res[0], recv_sems=res[1], srcs=res[2:2 + ns], lands=res[2 + ns:2 + ns + nl],
                token=res[-1])


def _send_wait(started, after, name):
    ns, nl = len(started["srcs"]), len(started["lands"])

    def body(*refs):
        send, recv = _exchange_copies(started["mode"], started["ks"], refs[:ns], refs[ns:ns + nl],
                                      refs[ns + nl], refs[ns + nl + 1])
        for cp in send:
            cp.wait_send()
        for cp in recv:
            cp.wait_recv()

    both = list(started["srcs"]) + list(started["lands"])
    after = after if isinstance(after, (tuple, list)) else (after,)
    res = pl.pallas_call(
        body, name=name,
        out_shape=tuple(pltpu.HBM(a.shape, a.dtype) for a in both),
        in_specs=[HBM] * (ns + nl) + [SEM, SEM] + [ANY] * len(after), out_specs=tuple([HBM] * (ns + nl)),
        input_output_aliases={i: i for i in range(ns + nl)}, compiler_params=SPLIT_EFFECT,
    )(*both, started["send_sems"], started["recv_sems"], *after)
    return res[:ns], res[ns:]


def _exchange_sibling(gs):
    n = len(gs)

    def body(*refs):
        ins, outs = refs[:n], refs[n:2 * n]
        send_sems, recv_sems = refs[2 * n:]
        x, y, c = lax.axis_index("x"), lax.axis_index("y"), lax.axis_index("c")
        copies = []
        for a in range(n):
            for j in range(4):
                copies.append(pltpu.make_async_remote_copy(
                    src_ref=ins[a].at[2 * j + (1 - c)], dst_ref=outs[a].at[j],
                    send_sem=send_sems.at[a, j], recv_sem=recv_sems.at[a, j],
                    device_id=(x, y, 1 - c), device_id_type=MESH))
        for cp in copies:
            cp.start()
        for cp in copies:
            cp.wait_recv()
        for cp in copies:
            cp.wait_send()

    return pl.pallas_call(
        body, in_specs=[ANY] * n, out_specs=[ANY] * n,
        out_shape=[SDS((4,) + g.shape[1:], g.dtype) for g in gs],
        scratch_shapes=[pltpu.SemaphoreType.DMA((n, 4)), pltpu.SemaphoreType.DMA((n, 4))],
        name="reduce_scatter_sibling")(*gs)


def _add_pair(g, r1, core, name):
    _, rows, cols = g.shape
    tr = _row_tile(rows, cols, 3 * g.dtype.itemsize)

    def body(c_ref, g_ref, r_ref, o_ref):
        o_ref[...] = (g_ref[...].astype(F32) + r_ref[...].astype(F32)).astype(o_ref.dtype)

    return pl.pallas_call(
        body,
        grid_spec=pltpu.PrefetchScalarGridSpec(
            num_scalar_prefetch=1, grid=(4, rows // tr),
            in_specs=[pl.BlockSpec((1, tr, cols), lambda j, i, c_ref: (2 * j + c_ref[0], i, 0)),
                      pl.BlockSpec((1, tr, cols), lambda j, i, c_ref: (j, i, 0))],
            out_specs=pl.BlockSpec((1, tr, cols), lambda j, i, c_ref: (j, i, 0))),
        out_shape=SDS((4, rows, cols), g.dtype),
        compiler_params=_params("parallel", "parallel"), name=name)(core, g, r1)


def _row_tile(rows, cols, itemsize_total):
    budget = (4 << 20) // max(1, cols * itemsize_total)
    if rows <= budget:
        return rows
    t = rows
    while t > budget and t % 2 == 0 and (t // 2) % 16 == 0:
        t //= 2
    return t


def _adam_math(g, w, m, v):
    m_new = ADAM_B1 * m + (1.0 - ADAM_B1) * g
    v_new = ADAM_B2 * v + (1.0 - ADAM_B2) * (g * g)
    m_hat = m_new / (1.0 - ADAM_B1 ** ADAM_STEP)
    v_hat = v_new / (1.0 - ADAM_B2 ** ADAM_STEP)
    delta = -ADAM_LR * (m_hat / (jnp.sqrt(v_hat) + ADAM_EPS) + ADAM_WD * w)
    return delta, m_new, v_new


def _sum_adam(parts, own, mine, w, m, v, name):
    rows, cols = w.shape
    nparts = parts.shape[0]
    tr = _row_tile(rows, cols, (nparts + 1) * parts.dtype.itemsize + 7 * 4)

    def body(mine_ref, p_ref, own_ref, w_ref, m_ref, v_ref, g_ref, d_ref, mo_ref, vo_ref):
        g = None
        for s in range(nparts):
            part = jnp.where(mine_ref[0] == s, own_ref[0], p_ref[s]).astype(F32)
            g = part if g is None else g + part
        delta, m_new, v_new = _adam_math(g, w_ref[...], m_ref[...], v_ref[...])
        g_ref[...] = g
        d_ref[...] = delta
        mo_ref[...] = m_new
        vo_ref[...] = v_new

    blk = pl.BlockSpec((tr, cols), lambda i, mine_ref: (i, 0))
    out = SDS((rows, cols), F32)
    return pl.pallas_call(
        body,
        grid_spec=pltpu.PrefetchScalarGridSpec(
            num_scalar_prefetch=1, grid=(rows // tr,),
            in_specs=[pl.BlockSpec((nparts, tr, cols), lambda i, mine_ref: (0, i, 0)),
                      pl.BlockSpec((1, tr, cols), lambda i, mine_ref: (mine_ref[0], i, 0)), blk, blk, blk],
            out_specs=[blk, blk, blk, blk]),
        out_shape=[out, out, out, out],
        compiler_params=_params("parallel"), name=name)(mine, parts, own, w, m, v)


SMALL_ROWS = 72


def _small_allreduce_adam(gpart, w, m, v, row_counts, dep=None):
    def reduce_body(g_ref, go_ref, gath, send_sems, recv_sems):
        x, y, c = lax.axis_index("x"), lax.axis_index("y"), lax.axis_index("c")
        me = 4 * x + 2 * y + c
        gath[me] = g_ref[...]
        copies = []
        for k in range(1, N_DEV):
            fx, fy, fc = (k >> 2) & 1, (k >> 1) & 1, k & 1
            peer = (x ^ fx, y ^ fy, c ^ fc)
            copies.append(pltpu.make_async_remote_copy(
                src_ref=gath.at[me], dst_ref=gath.at[me], send_sem=send_sems.at[k - 1], recv_sem=recv_sems.at[k - 1],
                device_id=peer, device_id_type=MESH))
        for cp in copies:
            cp.start()
        for cp in copies:
            cp.wait_recv()
        for cp in copies:
            cp.wait_send()
        g = gath[0]
        for d in range(1, N_DEV):
            g = g + gath[d]
        go_ref[...] = g

    def adam_body(g_ref, w_ref, m_ref, v_ref, *out_refs):
        g = g_ref[...]
        delta, m_new, v_new = _adam_math(g, w_ref[...], m_ref[...], v_ref[...])
        outs = iter(out_refs)
        for val in (g, delta, m_new, v_new):
            lo = 0
            for r in row_counts:
                next(outs)[...] = val[lo:lo + r]
                lo += r
        next(outs)[...] = g[SMALL_ROWS - SUBLANES:]

    vm = pl.BlockSpec(memory_space=pltpu.VMEM)
    reduce_body, dep_spec, dep_arg = _anchored(reduce_body, 1, dep)
    total = pl.pallas_call(
        reduce_body, in_specs=[vm] + dep_spec, out_specs=vm, out_shape=SDS((SMALL_ROWS, 128), F32),
        scratch_shapes=[pltpu.VMEM((N_DEV, SMALL_ROWS, 128), F32), pltpu.SemaphoreType.DMA((N_DEV - 1,)),
                        pltpu.SemaphoreType.DMA((N_DEV - 1,))],
        name="small_allreduce")(gpart, *dep_arg)
    out_shape = [SDS((r, 128), F32) for _ in range(4) for r in row_counts] + [SDS((SUBLANES, 128), F32)]
    res = pl.pallas_call(adam_body, in_specs=[vm] * 4, out_specs=[vm] * len(out_shape), out_shape=out_shape,
                         name="small_adam")(total, w, m, v)
    k = len(row_counts)
    return [res[i * k:(i + 1) * k] for i in range(4)], res[-1]


BIG = ("w_in", "conv_w", "w_conv_out", "w_attn_out", "w_o", "w_ffn_gate", "w_ffn_up", "w_ffn_down")
LATE_MERGE = ("w_conv_out", "w_attn_out", "w_o")
LATE_FFN = ("w_ffn_gate", "w_ffn_up", "w_ffn_down")
TRANSPOSED = ("w_in", "w_ffn_gate", "w_ffn_up")
COL_SHARDED = ("conv_w", "w_attn_out")
SMALL = ("norm1_g", "gate_b", "conv_b", "conv_ln_g", "conv_ln_b", "norm2_g", "norm_f_g")
WEIGHTS = ("norm1_g", "w_in", "gate_b", "conv_w", "conv_b", "conv_ln_g", "conv_ln_b", "w_conv_out", "w_attn_out", "w_o",
           "norm2_g", "w_ffn_gate", "w_ffn_up", "w_ffn_down", "norm_f_g")


def _shard2d(name, a):
    a = a.reshape(a.shape[-2], a.shape[-1])
    if name in TRANSPOSED:
        a = a.T
    if name == "conv_w":
        a = jnp.pad(a, ((0, CONV_PAD - CONV_K), (0, 0)))
    return a


def _from_shard2d(name, val, shape):
    if name in TRANSPOSED:
        val = val.T
    if name == "conv_w":
        val = val[:CONV_K]
    return val.reshape(shape)


def _gathered_to_full(name, g):
    if name in COL_SHARDED:
        return g.transpose(1, 0, 2).reshape(g.shape[1], N_DEV * g.shape[2])
    return g.reshape(N_DEV * g.shape[1], g.shape[2])


def _full_to_blocks(name, g):
    if name in COL_SHARDED:
        return g.reshape(g.shape[0], N_DEV, g.shape[1] // N_DEV).transpose(1, 0, 2)
    return g.reshape(N_DEV, g.shape[0] // N_DEV, g.shape[1])


def _pack_small(d, last_rows):
    vec = jnp.concatenate([d[n].reshape(-1) for n in SMALL]).reshape(SMALL_ROWS - SUBLANES, 128)
    return jnp.concatenate([vec, last_rows], axis=0)


def kernel(x, norm1_g, w_in, gate_b, conv_w, conv_b, conv_ln_g, conv_ln_b, w_conv_out, w_attn_out, w_o, norm2_g, w_ffn_gate, w_ffn_up, w_ffn_down, norm_f_g, loss_target, m_norm1_g, m_w_in, m_gate_b, m_conv_w, m_conv_b, m_conv_ln_g, m_conv_ln_b, m_w_conv_out, m_w_attn_out, m_w_o, m_norm2_g, m_w_ffn_gate, m_w_ffn_up, m_w_ffn_down, m_norm_f_g, v_norm1_g, v_w_in, v_gate_b, v_conv_w, v_conv_b, v_conv_ln_g, v_conv_ln_b, v_w_conv_out, v_w_attn_out, v_w_o, v_norm2_g, v_w_ffn_gate, v_w_ffn_up, v_w_ffn_down, v_norm_f_g):
    wts = dict(norm1_g=norm1_g, w_in=w_in, gate_b=gate_b, conv_w=conv_w, conv_b=conv_b, conv_ln_g=conv_ln_g,
               conv_ln_b=conv_ln_b, w_conv_out=w_conv_out, w_attn_out=w_attn_out, w_o=w_o, norm2_g=norm2_g,
               w_ffn_gate=w_ffn_gate, w_ffn_up=w_ffn_up, w_ffn_down=w_ffn_down, norm_f_g=norm_f_g)
    mom1 = dict(norm1_g=m_norm1_g, w_in=m_w_in, gate_b=m_gate_b, conv_w=m_conv_w, conv_b=m_conv_b, conv_ln_g=m_conv_ln_g,
                conv_ln_b=m_conv_ln_b, w_conv_out=m_w_conv_out, w_attn_out=m_w_attn_out, w_o=m_w_o, norm2_g=m_norm2_g,
                w_ffn_gate=m_w_ffn_gate, w_ffn_up=m_w_ffn_up, w_ffn_down=m_w_ffn_down, norm_f_g=m_norm_f_g)
    mom2 = dict(norm1_g=v_norm1_g, w_in=v_w_in, gate_b=v_gate_b, conv_w=v_conv_w, conv_b=v_conv_b, conv_ln_g=v_conv_ln_g,
                conv_ln_b=v_conv_ln_b, w_conv_out=v_w_conv_out, w_attn_out=v_w_attn_out, w_o=v_w_o, norm2_g=v_norm2_g,
                w_ffn_gate=v_w_ffn_gate, w_ffn_up=v_w_ffn_up, w_ffn_down=v_w_ffn_down, norm_f_g=v_norm_f_g)

    T = x.shape[0] * x.shape[1]
    x2 = x.reshape(T, D_MODEL)
    t2 = loss_target.reshape(T, D_MODEL)

    me = 4 * lax.axis_index("x") + 2 * lax.axis_index("y") + lax.axis_index("c")
    shards = {n: _shard2d(n, wts[n]) for n in BIG}
    sent = {n: shards[n] if n == "conv_w" else shards[n].astype(BF16) for n in BIG}
    small = {n: wts[n].reshape(1, -1) for n in SMALL}

    stage = {}

    def in_proj():
        w_in_blocks, conv_blocks = _all_gather([sent["w_in"], sent["conv_w"]])
        stage["merge"] = _send_start("gather", ALL_PEERS, "gather_start_merge", [sent[n] for n in LATE_MERGE],
                                     dep=w_in_blocks)
        stage["ffn"] = _send_start("gather", (1,) + OTHER_CHIPS, "gather_start_ffn", [sent[n] for n in LATE_FFN],
                                   dep=stage["merge"]["token"])
        w_in_t = _gathered_to_full("w_in", w_in_blocks)
        h, proj = _in_proj(x2, small["norm1_g"], w_in_t, stage["ffn"]["token"])
        return h, proj, {"w_in": w_in_t, "conv_w": _gathered_to_full("conv_w", conv_blocks)}

    def filled(names, srcs, lands):
        return {n: _gathered_to_full(n, lax.dynamic_update_slice(land, src[None], (me, 0, 0)))
                for n, src, land in zip(names, srcs, lands)}

    def pass_on(group, after):
        stage[group + "_srcs"], lands = _send_wait(stage[group], after, "gather_wait_" + group)
        stage[group + "_forward"] = _send_start("forward", OTHER_CHIPS, "forward_start_" + group, lands=lands)
        return stage[group + "_forward"]["token"]

    def arrived(group, names, after):
        _, lands = _send_wait(stage[group + "_forward"], after, "forward_wait_" + group)
        return filled(names, stage[group + "_srcs"], lands)

    def late_weights(which, after):
        if which == "after_attention":
            return {}
        if which is LATE_MERGE:
            srcs, lands = _send_wait(stage["merge"], after, "gather_wait_merge")
            return {**filled(LATE_MERGE, srcs, lands), "token": pass_on("ffn", after)}
        return arrived("ffn", LATE_FFN, after)

    scatters = []
    core = lax.axis_index("c").astype(jnp.int32).reshape(1)

    def emit(names, gw):
        blocks = [_full_to_blocks(n, gw[n]) for n in names]
        if "w_in" in names:
            sums = [_add_pair(g, r, core, "chip_sum_" + n) for n, g, r in zip(names, blocks, _exchange_sibling(blocks))]
            started = _send_start("chip_scatter", OTHER_CHIPS, "scatter_start_" + names[0], sums)
        else:
            started = _send_start("scatter", ALL_PEERS, "scatter_start_" + names[0], blocks)
        scatters.append((names, started))
        return started["token"]

    loss_part, grad_x, gw, gsmall = _local_step(x2, t2, in_proj, small, late_weights, emit)

    grads, deltas, new_m, new_v = {}, {}, {}, {}
    after = grad_x
    for names, started in scatters:
        srcs, lands = _send_wait(started, after, "scatter_wait_" + names[0])
        mine = (me >> 1 if started["mode"] == "chip_scatter" else me).astype(jnp.int32).reshape(1)
        for n, src, land in zip(names, srcs, lands):
            g, d, mo, vo = _sum_adam(land, src, mine, shards[n], _shard2d(n, mom1[n]), _shard2d(n, mom2[n]), "adam_" + n)
            for dst, val in ((grads, g), (deltas, d), (new_m, mo), (new_v, vo)):
                dst[n] = _from_shard2d(n, val, wts[n].shape)
            after = g

    zeros, ones = jnp.zeros((SUBLANES, 128), F32), jnp.ones((SUBLANES, 128), F32)
    row_counts = [wts[n].size // 128 for n in SMALL]
    kinds, loss_rows = _small_allreduce_adam(
        _pack_small(gsmall, jnp.broadcast_to(loss_part, (SUBLANES, 128))), _pack_small(wts, zeros),
        _pack_small(mom1, zeros), _pack_small(mom2, ones), row_counts, after)
    for dst, vals in zip((grads, deltas, new_m, new_v), kinds):
        dst.update({n: val.reshape(wts[n].shape) for n, val in zip(SMALL, vals)})
    loss = loss_rows[0, 0]
    return (loss, grad_x.reshape(x.shape), *[grads[n] for n in WEIGHTS], *[deltas[n] for n in WEIGHTS],
            *[new_m[n] for n in WEIGHTS], *[new_v[n] for n in WEIGHTS])
```

```python
import math

import numpy as np
import jax
import jax.numpy as jnp
from jax import lax
from jax.experimental import pallas as pl
from jax.experimental.pallas import tpu as pltpu

F32 = jnp.float32
BF16 = jnp.bfloat16
SDS = jax.ShapeDtypeStruct
MESH = pl.DeviceIdType.MESH

D_MODEL = 1024
SEQ = 2048
HEAD_DIM = 64
GROUPS = ((128, 1), (512, 4), (2048, 16))
HEADS_PER_GROUP = 8
N_HEADS = 24
ATTN_WIDTH = N_HEADS * HEAD_DIM
ATTN_OUT = HEADS_PER_GROUP * HEAD_DIM
CONV_K = 31
CONV_PAD = 32
D_FF = 2816
IN_WIDTH = 3 * ATTN_WIDTH + 2 * D_MODEL + 2 * D_MODEL
RMS_EPS = 1e-6
LN_EPS = 1e-5
Q_BLOCK = 128
LANES = 128
NEG = -1e30
N_DEV = 8

ADAM_LR = 0.001
ADAM_B1 = 0.9
ADAM_B2 = 0.999
ADAM_EPS = 1e-08
ADAM_WD = 0.01
ADAM_STEP = 10


def _alibi_slope_list(n):
    def pow2(m):
        start = 2.0 ** (-8.0 / m)
        return [start ** (i + 1) for i in range(m)]
    if math.log2(n).is_integer():
        return pow2(n)
    c = 2 ** math.floor(math.log2(n))
    return pow2(c) + _alibi_slope_list(2 * c)[0::2][: n - c]


def _slopes_times_dilation():
    s = np.asarray(sorted(_alibi_slope_list(N_HEADS), reverse=True), dtype=np.float32).reshape(3, HEADS_PER_GROUP)
    r = np.asarray([g[1] for g in GROUPS], dtype=np.float32)[:, None]
    return (s * r).reshape(N_HEADS)


def _sigmoid(x):
    return 0.5 * jnp.tanh(0.5 * x) + 0.5


def _dot(a, b):
    return jnp.dot(a, b, preferred_element_type=F32)


def _dot_nt(a, b):
    return lax.dot_general(a, b, (((1,), (1,)), ((), ())), preferred_element_type=F32)


def _dot_tn(a, b):
    return lax.dot_general(a, b, (((0,), (0,)), ((), ())), preferred_element_type=F32)


def _rowsum(x):
    return jnp.sum(x, axis=0, keepdims=True)


ANY_SPEC = pl.BlockSpec(memory_space=pl.ANY)


def _params(*sem):
    return pltpu.CompilerParams(dimension_semantics=sem)


def _anchored(body, n_in, dep):
    if dep is None:
        return body, [], []

    def wrapped(*refs):
        return body(*refs[:n_in], *refs[n_in + 1:])

    return wrapped, [pl.BlockSpec(memory_space=pl.ANY)], [dep]


IN_TM = 256
IN_CHUNK = 512


def _in_proj(x, g1, w_in_t, dep=None):
    T = x.shape[0]
    tm = IN_TM

    def body(x_ref, g_ref, w_hbm, h_ref, proj_ref, w_vmem, sem):
        @pl.when(pl.program_id(0) == 0)
        def _():
            cp = pltpu.make_async_copy(w_hbm, w_vmem, sem)
            cp.start()
            cp.wait()

        xv = x_ref[...]
        r = lax.rsqrt(jnp.mean(xv * xv, axis=-1, keepdims=True) + RMS_EPS)
        h = (xv * r * g_ref[...]).astype(BF16)
        h_ref[...] = h
        for lo in range(0, IN_WIDTH, IN_CHUNK):
            proj_ref[:, lo:lo + IN_CHUNK] = _dot_nt(h, w_vmem[lo:lo + IN_CHUNK, :])

    row = lambda n: pl.BlockSpec((tm, n), lambda i: (i, 0))
    body, dep_spec, dep_arg = _anchored(body, 3, dep)
    return pl.pallas_call(
        body, grid=(T // tm,),
        in_specs=[row(D_MODEL), pl.BlockSpec((1, D_MODEL), lambda i: (0, 0)), pl.BlockSpec(memory_space=pl.ANY)] + dep_spec,
        out_specs=[row(D_MODEL), row(IN_WIDTH)],
        out_shape=[SDS((T, D_MODEL), BF16), SDS((T, IN_WIDTH), F32)],
        scratch_shapes=[pltpu.VMEM((IN_WIDTH, D_MODEL), BF16), pltpu.SemaphoreType.DMA],
        compiler_params=_params("arbitrary"), name="in_proj")(x, g1, w_in_t, *dep_arg)


def _mm_tn(a, b, out_dtype, name, tn, tt=1024):
    T, K = a.shape
    N = b.shape[1]
    nt = T // tt

    def body(a_ref, b_ref, o_ref, acc):
        t = pl.program_id(1)

        @pl.when(t == 0)
        def _():
            acc[...] = jnp.zeros_like(acc)

        acc[...] += _dot_tn(a_ref[...], b_ref[...])

        @pl.when(t == nt - 1)
        def _():
            o_ref[...] = acc[...].astype(o_ref.dtype)

    return pl.pallas_call(
        body, grid=(N // tn, nt),
        in_specs=[pl.BlockSpec((tt, K), lambda j, t: (t, 0)),
                  pl.BlockSpec((tt, tn), lambda j, t: (t, j))],
        out_specs=pl.BlockSpec((K, tn), lambda j, t: (0, j)),
        out_shape=SDS((K, N), out_dtype),
        scratch_shapes=[pltpu.VMEM((K, tn), F32)],
        compiler_params=_params("parallel", "arbitrary"), name=name)(a, b)


def _gather_classes(src_ref, dst, r, row0=0):
    L = SEQ // r
    for c in range(r):
        dst[row0 + c * L:row0 + (c + 1) * L, :] = src_ref[0, pl.ds(c, L, stride=r), :].astype(dst.dtype)


def _scatter_classes(src, dst, r, row0=0):
    L = SEQ // r
    for c in range(r):
        dst[pl.ds(c, L, stride=r), :] = src[row0 + c * L:row0 + (c + 1) * L, :].astype(dst.dtype)


def _attn_masks(slope_r):
    qi = lax.broadcasted_iota(jnp.int32, (Q_BLOCK, Q_BLOCK), 0)
    kj = lax.broadcasted_iota(jnp.int32, (Q_BLOCK, Q_BLOCK), 1)
    rel = (qi - kj).astype(F32)
    bias_cur = jnp.where(qi >= kj, -slope_r * rel, NEG)
    bias_prev = jnp.where(qi <= kj, -slope_r * (rel + float(Q_BLOCK)), NEG)
    return bias_cur, bias_prev


def _store_biases(bias, sl_ref, g, hp):
    for hh in range(2):
        cur, prev = _attn_masks(sl_ref[g * HEADS_PER_GROUP + 2 * hp + hh])
        rows = slice(hh * Q_BLOCK, (hh + 1) * Q_BLOCK)
        bias[rows, 0:Q_BLOCK] = prev
        bias[rows, Q_BLOCK:] = cur


def _transpose_pairs(src, dst):
    dst[0, :, 0:Q_BLOCK] = jnp.zeros((LANES, Q_BLOCK), dst.dtype)
    nblk = SEQ // Q_BLOCK
    for b in range(nblk):
        t = src[(b + 1) * Q_BLOCK:(b + 2) * Q_BLOCK, :].T
        dst[b, :, Q_BLOCK:] = t
        if b + 1 < nblk:
            dst[b + 1, :, 0:Q_BLOCK] = t


def _stack_heads(t, low):
    z = jnp.zeros_like(t)
    return jnp.concatenate([jnp.where(low, t, z), jnp.where(low, z, t)], axis=0)


def _unstack_heads(t2, low):
    return jnp.where(low, t2[:Q_BLOCK], t2[Q_BLOCK:])


def _key_span(u, nb):
    off = u * Q_BLOCK
    if u % nb == 0:
        return slice(off + Q_BLOCK, off + 2 * Q_BLOCK), slice(Q_BLOCK, 2 * Q_BLOCK)
    return slice(off, off + 2 * Q_BLOCK), slice(0, 2 * Q_BLOCK)


def _attn_fwd(qkv, slopes_r, batch, dep=None):
    nblk = SEQ // Q_BLOCK

    def body(sl_ref, *refs):
        qkv_refs = refs[:9]
        att_ref, lse_ref = refs[9:11]
        qd, kd, vd, kt, opos, lpos, bias = refs[11:]
        hp = pl.program_id(1)
        low = lax.broadcasted_iota(jnp.int32, (Q_BLOCK, LANES), 1) < HEAD_DIM

        for g in range(3):
            r = GROUPS[g][1]
            nb = SEQ // r // Q_BLOCK
            _gather_classes(qkv_refs[3 * g], qd, r)
            kd[0:Q_BLOCK, :] = jnp.zeros((Q_BLOCK, LANES), BF16)
            vd[0:Q_BLOCK, :] = jnp.zeros((Q_BLOCK, LANES), BF16)
            _gather_classes(qkv_refs[3 * g + 1], kd, r, Q_BLOCK)
            _gather_classes(qkv_refs[3 * g + 2], vd, r, Q_BLOCK)
            _transpose_pairs(kd, kt)
            _store_biases(bias, sl_ref, g, hp)

            for u in range(nblk):
                keys, cols = _key_span(u, nb)
                q2 = _stack_heads(qd[u * Q_BLOCK:(u + 1) * Q_BLOCK, :], low)
                s = _dot(q2, kt[u, :, cols]) * 0.125 + bias[:, cols]
                m = jnp.max(s, axis=-1, keepdims=True)
                p = jnp.exp(s - m)
                l = jnp.sum(p, axis=-1, keepdims=True)
                o2 = _dot(p.astype(BF16), vd[keys, :]) * (1.0 / l)
                lse2 = m + jnp.log(l)
                rows = pl.ds(u // nb + (u % nb) * (Q_BLOCK * r), Q_BLOCK, stride=r)
                opos[g, rows, :] = _unstack_heads(o2, low)
                lpos[g, rows, :] = jnp.where(low, lse2[:Q_BLOCK], lse2[Q_BLOCK:])

        def merge(i, carry):
            rows = pl.ds(pl.multiple_of(i * 256, 256), 256)
            l0, l1, l2 = lpos[0, rows, :], lpos[1, rows, :], lpos[2, rows, :]
            m = jnp.maximum(jnp.maximum(l0, l1), l2)
            e0, e1, e2 = jnp.exp(l0 - m), jnp.exp(l1 - m), jnp.exp(l2 - m)
            den = e0 + e1 + e2
            att = (e0 * opos[0, rows, :] + e1 * opos[1, rows, :] + e2 * opos[2, rows, :]) / den
            att_ref[0, rows, :] = att.astype(att_ref.dtype)
            lse_ref[0, rows, :] = m + jnp.log(den)
            return carry

        lax.fori_loop(0, SEQ // 256, merge, 0)

    def col(sec, g):
        return pl.BlockSpec((1, SEQ, LANES), lambda b, hp: (b, 0, sec * 12 + g * 4 + hp))

    out = pl.BlockSpec((1, SEQ, LANES), lambda b, hp: (b, 0, hp))
    body, dep_spec, dep_arg = _anchored(body, 10, dep)
    return pl.pallas_call(
        body, grid=(batch, 4),
        in_specs=[pl.BlockSpec(memory_space=pltpu.SMEM)] + [col(sec, g) for g in range(3) for sec in range(3)] + dep_spec,
        out_specs=[out, out],
        out_shape=[SDS((batch, SEQ, ATTN_OUT), BF16), SDS((batch, SEQ, ATTN_OUT), F32)],
        scratch_shapes=[pltpu.VMEM((SEQ, LANES), BF16), pltpu.VMEM((Q_BLOCK + SEQ, LANES), BF16),
                        pltpu.VMEM((Q_BLOCK + SEQ, LANES), BF16), pltpu.VMEM((nblk, LANES, 2 * Q_BLOCK), BF16),
                        pltpu.VMEM((3, SEQ, LANES), F32), pltpu.VMEM((3, SEQ, LANES), F32),
                        pltpu.VMEM((2 * Q_BLOCK, 2 * Q_BLOCK), F32)],
        compiler_params=_params("parallel", "parallel"), name="attn_fwd")(slopes_r, *([qkv] * 9), *dep_arg)


def _attn_bwd(qkv, datt, lse, dsum, slopes_r, batch):
    nblk = SEQ // Q_BLOCK

    def body(sl_ref, q_ref, k_ref, v_ref, do_ref, l_ref, d_ref, dq_ref, dk_ref, dv_ref,
             qd, kd, vd, dod, ld, dd, dq_acc, dk_acc, dv_acc, dk_part, dv_part, stage, bias):
        gid, hp = pl.program_id(1), pl.program_id(2)
        low = lax.broadcasted_iota(jnp.int32, (Q_BLOCK, LANES), 1) < HEAD_DIM

        def section(g):
            r = GROUPS[g][1]
            nb = SEQ // r // Q_BLOCK
            _gather_classes(q_ref, qd, r)
            kd[0:Q_BLOCK, :] = jnp.zeros((Q_BLOCK, LANES), BF16)
            vd[0:Q_BLOCK, :] = jnp.zeros((Q_BLOCK, LANES), BF16)
            _gather_classes(k_ref, kd, r, Q_BLOCK)
            _gather_classes(v_ref, vd, r, Q_BLOCK)
            _gather_classes(do_ref, dod, r)
            _gather_classes(l_ref, ld, r)
            _gather_classes(d_ref, dd, r)
            _store_biases(bias, sl_ref, g, hp)

            for u in range(nblk):
                keys, cols = _key_span(u, nb)
                rows = slice(u * Q_BLOCK, (u + 1) * Q_BLOCK)
                q2 = _stack_heads(qd[rows, :], low)
                do2 = _stack_heads(dod[rows, :], low)
                lse_t = ld[rows, :]
                dsum_t = dd[rows, :]
                lse2 = jnp.concatenate([lse_t[:, 0:1], lse_t[:, HEAD_DIM:HEAD_DIM + 1]], axis=0)
                dsum2 = jnp.concatenate([dsum_t[:, 0:1], dsum_t[:, HEAD_DIM:HEAD_DIM + 1]], axis=0)
                s = _dot_nt(q2, kd[keys, :]) * 0.125 + bias[:, cols]
                p = jnp.exp(s - lse2)
                ds = (p * (_dot_nt(do2, vd[keys, :]) - dsum2)).astype(BF16)
                dq_acc[rows, :] = _unstack_heads(_dot(ds, kd[keys, :]), low) * 0.125
                dk_part[u, cols, :] = _dot_tn(ds, q2) * 0.125
                dv_part[u, cols, :] = _dot_tn(p.astype(BF16), do2)

            for part, acc in ((dk_part, dk_acc), (dv_part, dv_acc)):
                for b in range(nblk):
                    t = part[b, Q_BLOCK:, :]
                    if b + 1 < nblk and (b + 1) % nb != 0:
                        t = t + part[b + 1, 0:Q_BLOCK, :]
                    acc[b * Q_BLOCK:(b + 1) * Q_BLOCK, :] = t
            for acc, out_ref in ((dq_acc, dq_ref), (dk_acc, dk_ref), (dv_acc, dv_ref)):
                _scatter_classes(acc, stage, r)
                out_ref[0] = stage[...].astype(out_ref.dtype)

        for g in range(3):
            pl.when(gid == g)(lambda g=g: section(g))

    def col(sec):
        return pl.BlockSpec((1, SEQ, LANES), lambda b, g, hp: (b, 0, sec * 12 + g * 4 + hp))

    pos = pl.BlockSpec((1, SEQ, LANES), lambda b, g, hp: (b, 0, hp))
    dout = pl.BlockSpec((1, SEQ, LANES), lambda b, g, hp: (b, 0, g * 4 + hp))
    out = SDS((batch, SEQ, ATTN_WIDTH), BF16)
    seq_bf = pltpu.VMEM((SEQ, LANES), BF16)
    seq_f = pltpu.VMEM((SEQ, LANES), F32)
    pad_bf = pltpu.VMEM((Q_BLOCK + SEQ, LANES), BF16)
    part = pltpu.VMEM((nblk, 2 * Q_BLOCK, LANES), F32)
    return pl.pallas_call(
        body, grid=(batch, 3, 4),
        in_specs=[pl.BlockSpec(memory_space=pltpu.SMEM), col(0), col(1), col(2), pos, pos, pos],
        out_specs=[dout, dout, dout],
        out_shape=[out, out, out],
        scratch_shapes=[seq_bf, pad_bf, pad_bf, seq_bf, seq_f, seq_f, seq_f, seq_f, seq_f, part, part, seq_f,
                        pltpu.VMEM((2 * Q_BLOCK, 2 * Q_BLOCK), F32)],
        compiler_params=_params("parallel", "parallel", "parallel"), name="attn_bwd")(
            slopes_r, qkv, qkv, qkv, datt, lse, dsum)


CONV_TC = 128
U_BLOCK0 = 3 * ATTN_WIDTH // CONV_TC
CONV_ROWS = 128
SUBLANES = 8


SHIFT_TAIL = CONV_PAD - SUBLANES
CONV_CHUNKS = SEQ // CONV_ROWS


def _fill_shifted_rows(sh, c):
    lo = c * CONV_ROWS + (SHIFT_TAIL if c else 0)
    hi = (c + 1) * CONV_ROWS + SHIFT_TAIL
    for s in range(1, SUBLANES):
        sh[s, lo:hi, :] = sh[0, lo + s:hi + s, :]


def _tap(sh, base, offset):
    s = offset % SUBLANES
    lo = base + offset - s
    return sh[s, lo:lo + CONV_ROWS, :]


def _conv_fwd(u, conv_w, conv_b, batch, dep=None):
    nct = D_MODEL // CONV_TC

    def body(ua_ref, ub_ref, w_ref, b_ref, o_ref, sh):
        sh[0, 0:CONV_PAD, :] = jnp.zeros((CONV_PAD, CONV_TC), F32)
        for c in range(CONV_CHUNKS):
            base = c * CONV_ROWS
            rows = slice(base, base + CONV_ROWS)
            sh[0, CONV_PAD + base:CONV_PAD + base + CONV_ROWS, :] = ua_ref[0, rows, :] * _sigmoid(ub_ref[0, rows, :])
            _fill_shifted_rows(sh, c)
            acc = jnp.broadcast_to(b_ref[...], (CONV_ROWS, CONV_TC))
            for t in range(CONV_K):
                acc = acc + _tap(sh, base, t + CONV_PAD - (CONV_K - 1)) * w_ref[t:t + 1, :]
            o_ref[0, rows, :] = acc

    body, dep_spec, dep_arg = _anchored(body, 4, dep)
    return pl.pallas_call(
        body, grid=(nct, batch),
        in_specs=[pl.BlockSpec((1, SEQ, CONV_TC), lambda j, b: (b, 0, U_BLOCK0 + j)),
                  pl.BlockSpec((1, SEQ, CONV_TC), lambda j, b: (b, 0, U_BLOCK0 + nct + j)),
                  pl.BlockSpec((CONV_PAD, CONV_TC), lambda j, b: (0, j)),
                  pl.BlockSpec((1, CONV_TC), lambda j, b: (0, j))] + dep_spec,
        out_specs=pl.BlockSpec((1, SEQ, CONV_TC), lambda j, b: (b, 0, j)),
        out_shape=SDS((batch, SEQ, D_MODEL), F32),
        scratch_shapes=[pltpu.VMEM((SUBLANES, SEQ + CONV_PAD, CONV_TC), F32)],
        compiler_params=_params("parallel", "parallel"), name="conv_fwd")(u, u, conv_w, conv_b, *dep_arg)


def _conv_bwd(u, dc1, conv_w, batch, dep=None):
    nct = D_MODEL // CONV_TC

    def body(ua_ref, ub_ref, d_ref, w_ref, dua_ref, dub_ref, gw_ref, gb_ref, shc, shd, gacc):
        b = pl.program_id(1)
        shc[0, 0:CONV_PAD, :] = jnp.zeros((CONV_PAD, CONV_TC), F32)
        shd[0, 0:SEQ, :] = d_ref[0]
        shd[0, SEQ:, :] = jnp.zeros((CONV_PAD, CONV_TC), F32)

        @pl.when(b == 0)
        def _():
            gacc[...] = jnp.zeros_like(gacc)
            gb_ref[...] = jnp.zeros_like(gb_ref)

        gb_ref[...] += _rowsum(d_ref[0])

        for c in range(CONV_CHUNKS):
            base = c * CONV_ROWS
            rows = slice(base, base + CONV_ROWS)
            ua = ua_ref[0, rows, :]
            sg = _sigmoid(ub_ref[0, rows, :])
            shc[0, CONV_PAD + base:CONV_PAD + base + CONV_ROWS, :] = ua * sg
            _fill_shifted_rows(shc, c)
            _fill_shifted_rows(shd, c)
            dcur = shd[0, rows, :]
            acc = jnp.zeros((CONV_ROWS, CONV_TC), F32)
            for t in range(CONV_K):
                acc = acc + _tap(shd, base, CONV_K - 1 - t) * w_ref[t:t + 1, :]
                prod = _tap(shc, base, t + CONV_PAD - (CONV_K - 1)) * dcur
                gacc[t] += jnp.sum(prod.reshape(CONV_ROWS // 8, 8, CONV_TC), axis=0)
            dua_ref[0, rows, :] = (acc * sg).astype(dua_ref.dtype)
            dub_ref[0, rows, :] = (acc * ua * sg * (1.0 - sg)).astype(dub_ref.dtype)

        @pl.when(b == batch - 1)
        def _():
            for t in range(CONV_K):
                gw_ref[t:t + 1, :] = jnp.sum(gacc[t], axis=0, keepdims=True)
            gw_ref[CONV_K:CONV_PAD, :] = jnp.zeros((CONV_PAD - CONV_K, CONV_TC), F32)

    du = SDS((batch, SEQ, D_MODEL), BF16)
    body, dep_spec, dep_arg = _anchored(body, 4, dep)
    return pl.pallas_call(
        body, grid=(nct, batch),
        in_specs=[pl.BlockSpec((1, SEQ, CONV_TC), lambda j, b: (b, 0, U_BLOCK0 + j)),
                  pl.BlockSpec((1, SEQ, CONV_TC), lambda j, b: (b, 0, U_BLOCK0 + nct + j)),
                  pl.BlockSpec((1, SEQ, CONV_TC), lambda j, b: (b, 0, j)),
                  pl.BlockSpec((CONV_PAD, CONV_TC), lambda j, b: (0, j))] + dep_spec,
        out_specs=[pl.BlockSpec((1, SEQ, CONV_TC), lambda j, b: (b, 0, j)),
                   pl.BlockSpec((1, SEQ, CONV_TC), lambda j, b: (b, 0, j)),
                   pl.BlockSpec((CONV_PAD, CONV_TC), lambda j, b: (0, j)),
                   pl.BlockSpec((1, CONV_TC), lambda j, b: (0, j))],
        out_shape=[du, du, SDS((CONV_PAD, D_MODEL), F32), SDS((1, D_MODEL), F32)],
        scratch_shapes=[pltpu.VMEM((SUBLANES, SEQ + CONV_PAD, CONV_TC), F32),
                        pltpu.VMEM((SUBLANES, SEQ + CONV_PAD, CONV_TC), F32),
                        pltpu.VMEM((CONV_K, 8, CONV_TC), F32)],
        compiler_params=_params("parallel", "arbitrary"), name="conv_bwd")(u, u, dc1, conv_w, *dep_arg)


MID_TM = 256


def _layernorm_stats(c1):
    mu = jnp.mean(c1, axis=-1, keepdims=True)
    cen = c1 - mu
    rs = lax.rsqrt(jnp.mean(cen * cen, axis=-1, keepdims=True) + LN_EPS)
    return cen * rs, rs


GATE_PARTS = 4
GATE_PART = 2 * D_MODEL // GATE_PARTS
GATE_PART0 = (IN_WIDTH - 2 * D_MODEL) // GATE_PART


def _gate_specs(tm):
    return [pl.BlockSpec((tm, GATE_PART), lambda i, k=k: (i, GATE_PART0 + k)) for k in range(GATE_PARTS)]


def _mid_fwd(att, c1, proj, x, w_a, w_c, w_o, gate_b, ln_g, ln_b, g2, dep=None):
    T = x.shape[0]
    tm = MID_TM

    def body(att_ref, c1_ref, lg0, lg1, lg2, lg3, x_ref, wa_ref, wc_ref, wo_ref, gb_ref, lng_ref, lnb_ref, g2_ref,
             c3_ref, ya_ref, yc_ref, mix_ref, x1_ref, h2_ref):
        logits = jnp.concatenate([lg0[...], lg1[...], lg2[...], lg3[...]], axis=1)
        ya = _dot(att_ref[...], wa_ref[...])
        xh, _ = _layernorm_stats(c1_ref[...])
        c2 = xh * lng_ref[...] + lnb_ref[...]
        c3 = (c2 * _sigmoid(c2)).astype(BF16)
        c3_ref[...] = c3
        yc = _dot(c3, wc_ref[...])
        gates = _sigmoid(logits + gb_ref[...])
        mix = (gates[:, :D_MODEL] * ya + gates[:, D_MODEL:] * yc).astype(BF16)
        ya_ref[...] = ya.astype(BF16)
        yc_ref[...] = yc.astype(BF16)
        mix_ref[...] = mix
        x1 = x_ref[...] + _dot(mix, wo_ref[...])
        x1_ref[...] = x1
        r = lax.rsqrt(jnp.mean(x1 * x1, axis=-1, keepdims=True) + RMS_EPS)
        h2_ref[...] = (x1 * r * g2_ref[...]).astype(BF16)

    row = lambda n: pl.BlockSpec((tm, n), lambda i: (i, 0))
    full = lambda a, b: pl.BlockSpec((a, b), lambda i: (0, 0))
    body, dep_spec, dep_arg = _anchored(body, 10 + GATE_PARTS, dep)
    return pl.pallas_call(
        body, grid=(T // tm,),
        in_specs=[row(ATTN_OUT), row(D_MODEL)] + _gate_specs(tm) + [row(D_MODEL),
                  full(ATTN_OUT, D_MODEL), full(D_MODEL, D_MODEL), full(D_MODEL, D_MODEL),
                  full(1, 2 * D_MODEL), full(1, D_MODEL), full(1, D_MODEL), full(1, D_MODEL)] + dep_spec,
        out_specs=[row(D_MODEL), row(D_MODEL), row(D_MODEL), row(D_MODEL), row(D_MODEL), row(D_MODEL)],
        out_shape=[SDS((T, D_MODEL), BF16), SDS((T, D_MODEL), BF16), SDS((T, D_MODEL), BF16), SDS((T, D_MODEL), BF16),
                   SDS((T, D_MODEL), F32), SDS((T, D_MODEL), BF16)],
        compiler_params=_params("parallel"), name="mid_fwd")(att, c1, *([proj] * GATE_PARTS), x, w_a, w_c, w_o,
                                                             gate_b, ln_g, ln_b, g2, *dep_arg)


def _mid_bwd(dx1b, ya, yc, proj, att, c1, w_a, w_c, w_o, gate_b, ln_g, ln_b, head_ones, dep=None):
    T = dx1b.shape[0]
    tm = MID_TM

    def body(dx_ref, ya_ref, yc_ref, lg0, lg1, lg2, lg3, att_ref, c1_ref, wa_ref, wc_ref, wo_ref, gb_ref, lng_ref,
             lnb_ref, e_ref, dlg_ref, dya_ref, dyc_ref, datt_ref, dsum_ref, dc1_ref, ggb_ref, glg_ref, glb_ref):
        logits = jnp.concatenate([lg0[...], lg1[...], lg2[...], lg3[...]], axis=1)
        @pl.when(pl.program_id(0) == 0)
        def _():
            ggb_ref[...] = jnp.zeros_like(ggb_ref)
            glg_ref[...] = jnp.zeros_like(glg_ref)
            glb_ref[...] = jnp.zeros_like(glb_ref)

        dmix = _dot_nt(dx_ref[...], wo_ref[...])
        gates = _sigmoid(logits + gb_ref[...])
        ga, gc = gates[:, :D_MODEL], gates[:, D_MODEL:]
        dla = dmix * ya_ref[...].astype(F32) * ga * (1.0 - ga)
        dlc = dmix * yc_ref[...].astype(F32) * gc * (1.0 - gc)
        dlg_ref[:, :D_MODEL] = dla.astype(BF16)
        dlg_ref[:, D_MODEL:] = dlc.astype(BF16)
        ggb_ref[:, :D_MODEL] += _rowsum(dla)
        ggb_ref[:, D_MODEL:] += _rowsum(dlc)
        dya = (dmix * ga).astype(BF16)
        dyc = (dmix * gc).astype(BF16)
        dya_ref[...] = dya
        dyc_ref[...] = dyc
        datt = _dot_nt(dya, wa_ref[...])
        datt_ref[...] = datt
        dsum_ref[...] = jnp.dot(datt * att_ref[...].astype(F32), e_ref[...], preferred_element_type=F32,
                                precision=lax.Precision.HIGHEST)
        dc3 = _dot_nt(dyc, wc_ref[...])
        xh, rs = _layernorm_stats(c1_ref[...])
        c2 = xh * lng_ref[...] + lnb_ref[...]
        sg = _sigmoid(c2)
        dc2 = dc3 * (sg * (1.0 + c2 * (1.0 - sg)))
        glg_ref[...] += _rowsum(dc2 * xh)
        glb_ref[...] += _rowsum(dc2)
        dxh = dc2 * lng_ref[...]
        dc1_ref[...] = rs * (dxh - jnp.mean(dxh, axis=-1, keepdims=True) - xh * jnp.mean(dxh * xh, axis=-1, keepdims=True))

    row = lambda n: pl.BlockSpec((tm, n), lambda i: (i, 0))
    full = lambda a, b: pl.BlockSpec((a, b), lambda i: (0, 0))
    body, dep_spec, dep_arg = _anchored(body, 12 + GATE_PARTS, dep)
    return pl.pallas_call(
        body, grid=(T // tm,),
        in_specs=[row(D_MODEL), row(D_MODEL), row(D_MODEL)] + _gate_specs(tm) + [row(ATTN_OUT), row(D_MODEL),
                  full(ATTN_OUT, D_MODEL), full(D_MODEL, D_MODEL), full(D_MODEL, D_MODEL),
                  full(1, 2 * D_MODEL), full(1, D_MODEL), full(1, D_MODEL), full(ATTN_OUT, ATTN_OUT)] + dep_spec,
        out_specs=[row(2 * D_MODEL), row(D_MODEL), row(D_MODEL), row(ATTN_OUT), row(ATTN_OUT), row(D_MODEL),
                   full(1, 2 * D_MODEL), full(1, D_MODEL), full(1, D_MODEL)],
        out_shape=[SDS((T, 2 * D_MODEL), BF16), SDS((T, D_MODEL), BF16), SDS((T, D_MODEL), BF16), SDS((T, ATTN_OUT), F32),
                   SDS((T, ATTN_OUT), F32), SDS((T, D_MODEL), F32),
                   SDS((1, 2 * D_MODEL), F32), SDS((1, D_MODEL), F32), SDS((1, D_MODEL), F32)],
        compiler_params=_params("arbitrary"), name="mid_bwd")(dx1b, ya, yc, *([proj] * GATE_PARTS), att, c1, w_a, w_c, w_o,
                                                               gate_b, ln_g, ln_b, head_ones, *dep_arg)


FFN_TM = 256
FFN_CHUNK = 512
FFN_SUB = tuple((lo, min(lo + FFN_CHUNK, D_FF)) for lo in range(0, D_FF, FFN_CHUNK))


def _rms_bwd(dy_times_g, xh, r):
    return r * (dy_times_g - xh * jnp.mean(dy_times_g * xh, axis=-1, keepdims=True))


def _load_resident(pairs, sems):
    @pl.when(pl.program_id(0) == 0)
    def _():
        copies = [pltpu.make_async_copy(src, dst, sems.at[k]) for k, (src, dst) in enumerate(pairs)]
        for cp in copies:
            cp.start()
        for cp in copies:
            cp.wait()


def _ffn_fwd(h2, x1, target, gf, w_g_t, w_u_t, w_d):
    T = h2.shape[0]
    tm = FFN_TM

    def body(h_ref, x1_ref, t_ref, gf_ref, wg_hbm, wu_hbm, wd_hbm,
             a_ref, b_ref, f_ref, dx2_ref, dx2b_ref, loss_ref, gnf_ref, wg, wu, wd, sems):
        _load_resident(((wg_hbm, wg), (wu_hbm, wu), (wd_hbm, wd)), sems)

        @pl.when(pl.program_id(0) == 0)
        def _():
            loss_ref[...] = jnp.zeros_like(loss_ref)
            gnf_ref[...] = jnp.zeros_like(gnf_ref)

        h = h_ref[...]
        x2 = x1_ref[...]
        for lo, hi in FFN_SUB:
            a = _dot_nt(h, wg[lo:hi, :])
            b = _dot_nt(h, wu[lo:hi, :])
            f = (a * _sigmoid(a) * b).astype(BF16)
            a_ref[:, lo:hi] = a.astype(BF16)
            b_ref[:, lo:hi] = b.astype(BF16)
            f_ref[:, lo:hi] = f
            x2 = x2 + _dot(f, wd[lo:hi, :])

        r = lax.rsqrt(jnp.mean(x2 * x2, axis=-1, keepdims=True) + RMS_EPS)
        xh = x2 * r
        err = xh * gf_ref[...] - t_ref[...]
        loss_ref[...] += (0.5 / D_MODEL) * jnp.sum(err * err)
        dy = err * (1.0 / D_MODEL)
        gnf_ref[...] += _rowsum(dy * xh)
        dx2 = _rms_bwd(dy * gf_ref[...], xh, r)
        dx2_ref[...] = dx2
        dx2b_ref[...] = dx2.astype(BF16)

    row = lambda n: pl.BlockSpec((tm, n), lambda i: (i, 0))
    const = lambda n: pl.BlockSpec((1, n), lambda i: (0, 0))
    wshape = pltpu.VMEM((D_FF, D_MODEL), BF16)
    return pl.pallas_call(
        body, grid=(T // tm,),
        in_specs=[row(D_MODEL), row(D_MODEL), row(D_MODEL), const(D_MODEL), ANY_SPEC, ANY_SPEC, ANY_SPEC],
        out_specs=[row(D_FF), row(D_FF), row(D_FF), row(D_MODEL), row(D_MODEL), const(128), const(D_MODEL)],
        out_shape=[SDS((T, D_FF), BF16), SDS((T, D_FF), BF16), SDS((T, D_FF), BF16), SDS((T, D_MODEL), F32),
                   SDS((T, D_MODEL), BF16), SDS((1, 128), F32), SDS((1, D_MODEL), F32)],
        scratch_shapes=[wshape, wshape, wshape, pltpu.SemaphoreType.DMA((3,))],
        compiler_params=_params("arbitrary"), name="ffn_fwd")(h2, x1, target, gf, w_g_t, w_u_t, w_d)


def _ffn_bwd(dx2b, dx2, a, b, x1, g2, w_g_t, w_u_t, w_d):
    T = dx2.shape[0]
    tm = FFN_TM

    def body(dxb_ref, dx2_ref, a_ref, b_ref, x1_ref, g2_ref, wg_hbm, wu_hbm, wd_hbm,
             da_ref, db_ref, dx1_ref, dx1b_ref, gn2_ref, wg, wu, wd, sems):
        _load_resident(((wg_hbm, wg), (wu_hbm, wu), (wd_hbm, wd)), sems)

        @pl.when(pl.program_id(0) == 0)
        def _():
            gn2_ref[...] = jnp.zeros_like(gn2_ref)

        dxb = dxb_ref[...]
        dh2 = jnp.zeros((tm, D_MODEL), F32)
        for lo, hi in FFN_SUB:
            df = _dot_nt(dxb, wd[lo:hi, :])
            av = a_ref[:, lo:hi].astype(F32)
            bv = b_ref[:, lo:hi].astype(F32)
            sg = _sigmoid(av)
            db = (df * av * sg).astype(BF16)
            da = (df * bv * (sg * (1.0 + av * (1.0 - sg)))).astype(BF16)
            da_ref[:, lo:hi] = da
            db_ref[:, lo:hi] = db
            dh2 = dh2 + _dot(da, wg[lo:hi, :]) + _dot(db, wu[lo:hi, :])

        x1 = x1_ref[...]
        r = lax.rsqrt(jnp.mean(x1 * x1, axis=-1, keepdims=True) + RMS_EPS)
        xh = x1 * r
        gn2_ref[...] += _rowsum(dh2 * xh)
        dx1 = dx2_ref[...] + _rms_bwd(dh2 * g2_ref[...], xh, r)
        dx1_ref[...] = dx1
        dx1b_ref[...] = dx1.astype(BF16)

    row = lambda n: pl.BlockSpec((tm, n), lambda i: (i, 0))
    const = lambda n: pl.BlockSpec((1, n), lambda i: (0, 0))
    wshape = pltpu.VMEM((D_FF, D_MODEL), BF16)
    return pl.pallas_call(
        body, grid=(T // tm,),
        in_specs=[row(D_MODEL), row(D_MODEL), row(D_FF), row(D_FF), row(D_MODEL), const(D_MODEL),
                  ANY_SPEC, ANY_SPEC, ANY_SPEC],
        out_specs=[row(D_FF), row(D_FF), row(D_MODEL), row(D_MODEL), const(D_MODEL)],
        out_shape=[SDS((T, D_FF), BF16), SDS((T, D_FF), BF16), SDS((T, D_MODEL), F32), SDS((T, D_MODEL), BF16),
                   SDS((1, D_MODEL), F32)],
        scratch_shapes=[wshape, wshape, wshape, pltpu.SemaphoreType.DMA((3,))],
        compiler_params=_params("arbitrary"), name="ffn_bwd")(dx2b, dx2, a, b, x1, g2, w_g_t, w_u_t, w_d)


def _in_bwd(pieces, w_in_t, x, dx1, g1, dep=None):
    T = x.shape[0]
    tm = IN_TM
    npc = len(pieces)
    assert sum(p.shape[1] for p in pieces) == IN_WIDTH

    def body(*refs):
        p_refs = refs[:npc]
        w_hbm, x_ref, dx1_ref, g_ref, dx_ref, gn1_ref, w_vmem, sem = refs[npc:]

        @pl.when(pl.program_id(0) == 0)
        def _():
            cp = pltpu.make_async_copy(w_hbm, w_vmem, sem)
            cp.start()
            cp.wait()
            gn1_ref[...] = jnp.zeros_like(gn1_ref)

        dh = jnp.zeros((tm, D_MODEL), F32)
        col = 0
        for p_ref in p_refs:
            for j in range(p_ref.shape[1] // IN_CHUNK):
                dh = dh + _dot(p_ref[:, j * IN_CHUNK:(j + 1) * IN_CHUNK], w_vmem[col:col + IN_CHUNK, :])
                col += IN_CHUNK
        xv = x_ref[...]
        r = lax.rsqrt(jnp.mean(xv * xv, axis=-1, keepdims=True) + RMS_EPS)
        xh = xv * r
        gn1_ref[...] += _rowsum(dh * xh)
        dx_ref[...] = dx1_ref[...] + _rms_bwd(dh * g_ref[...], xh, r)

    row = lambda n: pl.BlockSpec((tm, n), lambda i: (i, 0))
    body, dep_spec, dep_arg = _anchored(body, npc + 4, dep)
    return pl.pallas_call(
        body, grid=(T // tm,),
        in_specs=[row(p.shape[1]) for p in pieces]
        + [pl.BlockSpec(memory_space=pl.ANY), row(D_MODEL), row(D_MODEL), pl.BlockSpec((1, D_MODEL), lambda i: (0, 0))]
        + dep_spec,
        out_specs=[row(D_MODEL), pl.BlockSpec((1, D_MODEL), lambda i: (0, 0))],
        out_shape=[SDS((T, D_MODEL), F32), SDS((1, D_MODEL), F32)],
        scratch_shapes=[pltpu.VMEM((IN_WIDTH, D_MODEL), BF16), pltpu.SemaphoreType.DMA],
        compiler_params=_params("arbitrary"), name="in_bwd")(*pieces, w_in_t, x, dx1, g1, *dep_arg)


def _local_step(x, target, in_proj, small, late_weights=None, emit=None):
    T = x.shape[0]
    batch = T // SEQ
    slopes_r = jnp.asarray(_slopes_times_dilation())
    emit = emit or (lambda names, grads: None)

    h, proj, w = in_proj()
    proj3 = proj.reshape(batch, SEQ, IN_WIDTH)

    att, lse = _attn_fwd(proj3, slopes_r, batch, w.get("token"))
    att = att.reshape(T, ATTN_OUT)
    if late_weights is not None:
        w = {**w, **late_weights("after_attention", att)}

    c1 = _conv_fwd(proj3, w["conv_w"], small["conv_b"], batch, w.get("token")).reshape(T, D_MODEL)
    if late_weights is not None:
        w = {**w, **late_weights(LATE_MERGE, (att, c1))}

    c3, ya, yc, mix, x1, h2 = _mid_fwd(
        att, c1, proj, x, w["w_attn_out"], w["w_conv_out"], w["w_o"],
        small["gate_b"], small["conv_ln_g"], small["conv_ln_b"], small["norm2_g"], w.get("token"))
    if late_weights is not None:
        w = {**w, **late_weights(LATE_FFN, h2)}

    a, b, f, dx2, dx2b, loss, g_normf = _ffn_fwd(h2, x1, target, small["norm_f_g"],
                                                   w["w_ffn_gate"], w["w_ffn_up"], w["w_ffn_down"])

    da, db, dx1, dx1b, g_norm2 = _ffn_bwd(dx2b, dx2, a, b, x1, small["norm2_g"],
                                           w["w_ffn_gate"], w["w_ffn_up"], w["w_ffn_down"])
    gw = {}
    gw["w_ffn_down"] = _mm_tn(f, dx2b, BF16, "gw_ffn_down", tn=1024)
    gw["w_ffn_gate"] = _mm_tn(da, h2, BF16, "gw_ffn_gate", tn=1024)
    gw["w_ffn_up"] = _mm_tn(db, h2, BF16, "gw_ffn_up", tn=1024)
    token = emit(("w_ffn_gate", "w_ffn_up", "w_ffn_down"), gw)

    head_ones = jnp.asarray(np.kron(np.eye(HEADS_PER_GROUP, dtype=np.float32), np.ones((HEAD_DIM, HEAD_DIM), np.float32)))
    dlogits, dya, dyc, datt, dsum, dc1, g_gate_b, g_ln_g, g_ln_b = _mid_bwd(
        dx1b, ya, yc, proj, att, c1, w["w_attn_out"], w["w_conv_out"], w["w_o"],
        small["gate_b"], small["conv_ln_g"], small["conv_ln_b"], head_ones, token)
    gw["w_o"] = _mm_tn(mix, dx1b, BF16, "gw_o", tn=1024)
    gw["w_attn_out"] = _mm_tn(att, dya, BF16, "gw_attn_out", tn=1024)
    gw["w_conv_out"] = _mm_tn(c3, dyc, BF16, "gw_conv_out", tn=1024)
    token = emit(("w_conv_out", "w_attn_out", "w_o"), gw)

    dua, dub, g_conv_w, g_conv_b = _conv_bwd(proj3, dc1.reshape(batch, SEQ, D_MODEL), w["conv_w"], batch, token)

    dq, dk, dv = _attn_bwd(proj3, datt.reshape(batch, SEQ, ATTN_OUT), lse, dsum.reshape(batch, SEQ, ATTN_OUT),
                           slopes_r, batch)
    pieces = [dq.reshape(T, ATTN_WIDTH), dk.reshape(T, ATTN_WIDTH), dv.reshape(T, ATTN_WIDTH),
              dua.reshape(T, D_MODEL), dub.reshape(T, D_MODEL), dlogits]

    names = ("q", "k", "v", "ua", "ub", "gate")
    gw["w_in"] = jnp.concatenate([_mm_tn(p, h, BF16, "gw_in_" + nm, tn=1024) for nm, p in zip(names, pieces)], axis=0)
    gw["conv_w"] = g_conv_w
    token = emit(("w_in", "conv_w"), gw)
    grad_x, g_norm1 = _in_bwd(pieces, w["w_in"], x, dx1, small["norm1_g"], token)

    gsmall = {"norm1_g": g_norm1, "gate_b": g_gate_b, "conv_b": g_conv_b, "conv_ln_g": g_ln_g, "conv_ln_b": g_ln_b,
              "norm2_g": g_norm2, "norm_f_g": g_normf}
    return loss, grad_x, gw, gsmall


ANY = pl.BlockSpec(memory_space=pl.ANY)


def _all_gather(arrs):
    n = len(arrs)

    def body(*refs):
        ins, outs = refs[:n], refs[n:2 * n]
        send_sems, recv_sems, local_sems = refs[2 * n:]
        x, y, c = lax.axis_index("x"), lax.axis_index("y"), lax.axis_index("c")
        me, sibling = (x, y, c), (x, y, 1 - c)
        chips = [(1 - x, y), (x, 1 - y), (1 - x, 1 - y)]

        def copy(a, k, block, to, src=None):
            px, py, pc = block
            dst = outs[a].at[4 * px + 2 * py + pc]
            return pltpu.make_async_remote_copy(
                src_ref=dst if src is None else src, dst_ref=dst,
                send_sem=send_sems.at[a, k], recv_sem=recv_sems.at[a, k], device_id=to, device_id_type=MESH)

        mine = [pltpu.make_async_copy(ins[a], outs[a].at[4 * x + 2 * y + c], local_sems.at[a]) for a in range(n)]
        for cp in mine:
            cp.start()
        first = []
        for j, chip in enumerate(chips):
            first += [copy(a, 1 + j, me, (*chip, c), src=ins[a]) for a in range(n)]
        first += [copy(a, 0, me, sibling, src=ins[a]) for a in range(n)]
        for cp in first:
            cp.start()
        passed = []
        for j, chip in enumerate(chips):
            for a in range(n):
                copy(a, 1 + j, (*chip, c), me).wait_recv()
                cp = copy(a, 4 + j, (*chip, c), sibling)
                cp.start()
                passed.append(cp)
        for a in range(n):
            copy(a, 0, sibling, me).wait_recv()
        for j, chip in enumerate(chips):
            for a in range(n):
                copy(a, 4 + j, (*chip, 1 - c), me).wait_recv()
        for cp in first + passed:
            cp.wait_send()
        for cp in mine:
            cp.wait()

    return pl.pallas_call(
        body, in_specs=[ANY] * n, out_specs=[ANY] * n,
        out_shape=[SDS((N_DEV,) + a.shape, a.dtype) for a in arrs],
        scratch_shapes=[pltpu.SemaphoreType.DMA((n, 7)), pltpu.SemaphoreType.DMA((n, 7)), pltpu.SemaphoreType.DMA((n,))],
        name="all_gather_weights")(*arrs)


HBM =pl.BlockSpec(memory_space=pltpu.HBM)
SEM = pl.BlockSpec(memory_space=pltpu.SEMAPHORE)
ALL_PEERS = tuple(range(1, N_DEV))
OTHER_CHIPS = (2, 4, 6)
SPLIT_EFFECT = pltpu.CompilerParams(has_side_effects=pltpu.SideEffectType.DATAFLOW_SIDE_EFFECTING)


def _exchange_copies(mode, ks, srcs, lands, send_sems, recv_sems):
    x, y, c = lax.axis_index("x"), lax.axis_index("y"), lax.axis_index("c")
    me = 4 * x + 2 * y + c
    send, recv = [], []
    for a in range(len(lands)):
        for i, k in enumerate(ks):
            peer = (x ^ ((k >> 2) & 1), y ^ ((k >> 1) & 1), c ^ (k & 1))
            pidx = 4 * peer[0] + 2 * peer[1] + peer[2]
            if mode == "gather":
                src, to, out_slot, in_slot = srcs[a], peer, me, pidx
            elif mode == "scatter":
                src, to, out_slot, in_slot = srcs[a].at[pidx], peer, me, pidx
            elif mode == "chip_scatter":
                src, to, out_slot, in_slot = srcs[a].at[pidx >> 1], peer, me >> 1, pidx >> 1
            else:
                src, to, out_slot, in_slot = lands[a].at[pidx], (x, y, 1 - c), pidx, pidx ^ 1
            s = a * len(ks) + i
            send.append(pltpu.make_async_remote_copy(
                src_ref=src, dst_ref=lands[a].at[out_slot], send_sem=send_sems.at[s], recv_sem=recv_sems.at[s],
                device_id=to, device_id_type=MESH))
            recv.append(pltpu.make_async_remote_copy(
                src_ref=src, dst_ref=lands[a].at[in_slot], send_sem=send_sems.at[s], recv_sem=recv_sems.at[s],
                device_id=to, device_id_type=MESH))
    return send, recv


def _send_start(mode, ks, name, srcs=(), lands=None, dep=None):
    srcs = list(srcs)
    if lands is None:
        slots = 4 if mode == "chip_scatter" else N_DEV
        lands = [lax.empty((slots,) + (s.shape if mode == "gather" else s.shape[1:]), s.dtype) for s in srcs]
    ns, nl = len(srcs), len(lands)
    nsem = nl * len(ks)

    def body(*refs):
        send, _ = _exchange_copies(mode, ks, refs[:ns], refs[ns:ns + nl], refs[ns + nl], refs[ns + nl + 1])
        for cp in send:
            cp.start()
        token = refs[-1]
        token[...] = jnp.zeros_like(token)

    both = srcs + list(lands)
    body, dep_spec, dep_arg = _anchored(body, ns + nl, dep)
    res = pl.pallas_call(
        body, name=name,
        out_shape=(pltpu.SemaphoreType.DMA((nsem,)), pltpu.SemaphoreType.DMA((nsem,)),
                   *[pltpu.HBM(a.shape, a.dtype) for a in both], SDS((8, 128), F32)),
        in_specs=[HBM] * (ns + nl) + dep_spec,
        out_specs=(SEM, SEM, *([HBM] * (ns + nl)), pl.BlockSpec(memory_space=pltpu.VMEM)),
        input_output_aliases={i: 2 + i for i in range(ns + nl)}, compiler_params=SPLIT_EFFECT,
    )(*[pltpu.with_memory_space_constraint(a, pltpu.HBM) for a in both], *dep_arg)
    return dict(mode=mode, ks=ks, send_sems=res[0], recv_sems=res[1], srcs=res[2:2 + ns], lands=res[2 + ns:2 + ns + nl],
                token=res[-1])


def _send_wait(started, after, name):
    ns, nl = len(started["srcs"]), len(started["lands"])

    def body(*refs):
        send, recv = _exchange_copies(started["mode"], started["ks"], refs[:ns], refs[ns:ns + nl],
                                      refs[ns + nl], refs[ns + nl + 1])
        for cp in send:
            cp.wait_send()
        for cp in recv:
            cp.wait_recv()

    both = list(started["srcs"]) + list(started["lands"])
    after = after if isinstance(after, (tuple, list)) else (after,)
    res = pl.pallas_call(
        body, name=name,
        out_shape=tuple(pltpu.HBM(a.shape, a.dtype) for a in both),
        in_specs=[HBM] * (ns + nl) + [SEM, SEM] + [ANY] * len(after), out_specs=tuple([HBM] * (ns + nl)),
        input_output_aliases={i: i for i in range(ns + nl)}, compiler_params=SPLIT_EFFECT,
    )(*both, started["send_sems"], started["recv_sems"], *after)
    return res[:ns], res[ns:]


def _exchange_sibling(gs):
    n = len(gs)

    def body(*refs):
        ins, outs = refs[:n], refs[n:2 * n]
        send_sems, recv_sems = refs[2 * n:]
        x, y, c = lax.axis_index("x"), lax.axis_index("y"), lax.axis_index("c")
        copies = []
        for a in range(n):
            for j in range(4):
                copies.append(pltpu.make_async_remote_copy(
                    src_ref=ins[a].at[2 * j + (1 - c)], dst_ref=outs[a].at[j],
                    send_sem=send_sems.at[a, j], recv_sem=recv_sems.at[a, j],
                    device_id=(x, y, 1 - c), device_id_type=MESH))
        for cp in copies:
            cp.start()
        for cp in copies:
            cp.wait_recv()
        for cp in copies:
            cp.wait_send()

    return pl.pallas_call(
        body, in_specs=[ANY] * n, out_specs=[ANY] * n,
        out_shape=[SDS((4,) + g.shape[1:], g.dtype) for g in gs],
        scratch_shapes=[pltpu.SemaphoreType.DMA((n, 4)), pltpu.SemaphoreType.DMA((n, 4))],
        name="reduce_scatter_sibling")(*gs)


def _add_pair(g, r1, core, name):
    _, rows, cols = g.shape
    tr = _row_tile(rows, cols, 3 * g.dtype.itemsize)

    def body(c_ref, g_ref, r_ref, o_ref):
        o_ref[...] = (g_ref[...].astype(F32) + r_ref[...].astype(F32)).astype(o_ref.dtype)

    return pl.pallas_call(
        body,
        grid_spec=pltpu.PrefetchScalarGridSpec(
            num_scalar_prefetch=1, grid=(4, rows // tr),
            in_specs=[pl.BlockSpec((1, tr, cols), lambda j, i, c_ref: (2 * j + c_ref[0], i, 0)),
                      pl.BlockSpec((1, tr, cols), lambda j, i, c_ref: (j, i, 0))],
            out_specs=pl.BlockSpec((1, tr, cols), lambda j, i, c_ref: (j, i, 0))),
        out_shape=SDS((4, rows, cols), g.dtype),
        compiler_params=_params("parallel", "parallel"), name=name)(core, g, r1)


def _row_tile(rows, cols, itemsize_total):
    budget = (4 << 20) // max(1, cols * itemsize_total)
    if rows <= budget:
        return rows
    t = rows
    while t > budget and t % 2 == 0 and (t // 2) % 16 == 0:
        t //= 2
    return t


def _adam_math(g, w, m, v):
    m_new = ADAM_B1 * m + (1.0 - ADAM_B1) * g
    v_new = ADAM_B2 * v + (1.0 - ADAM_B2) * (g * g)
    m_hat = m_new / (1.0 - ADAM_B1 ** ADAM_STEP)
    v_hat = v_new / (1.0 - ADAM_B2 ** ADAM_STEP)
    delta = -ADAM_LR * (m_hat / (jnp.sqrt(v_hat) + ADAM_EPS) + ADAM_WD * w)
    return delta, m_new, v_new


def _sum_adam(parts, own, mine, w, m, v, name):
    rows, cols = w.shape
    nparts = parts.shape[0]
    tr = _row_tile(rows, cols, (nparts + 1) * parts.dtype.itemsize + 7 * 4)

    def body(mine_ref, p_ref, own_ref, w_ref, m_ref, v_ref, g_ref, d_ref, mo_ref, vo_ref):
        g = None
        for s in range(nparts):
            part = jnp.where(mine_ref[0] == s, own_ref[0], p_ref[s]).astype(F32)
            g = part if g is None else g + part
        delta, m_new, v_new = _adam_math(g, w_ref[...], m_ref[...], v_ref[...])
        g_ref[...] = g
        d_ref[...] = delta
        mo_ref[...] = m_new
        vo_ref[...] = v_new

    blk = pl.BlockSpec((tr, cols), lambda i, mine_ref: (i, 0))
    out = SDS((rows, cols), F32)
    return pl.pallas_call(
        body,
        grid_spec=pltpu.PrefetchScalarGridSpec(
            num_scalar_prefetch=1, grid=(rows // tr,),
            in_specs=[pl.BlockSpec((nparts, tr, cols), lambda i, mine_ref: (0, i, 0)),
                      pl.BlockSpec((1, tr, cols), lambda i, mine_ref: (mine_ref[0], i, 0)), blk, blk, blk],
            out_specs=[blk, blk, blk, blk]),
        out_shape=[out, out, out, out],
        compiler_params=_params("parallel"), name=name)(mine, parts, own, w, m, v)


SMALL_ROWS = 72


def _small_allreduce_adam(gpart, w, m, v, row_counts, dep=None):
    def reduce_body(g_ref, go_ref, gath, send_sems, recv_sems):
        x, y, c = lax.axis_index("x"), lax.axis_index("y"), lax.axis_index("c")
        me = 4 * x + 2 * y + c
        gath[me] = g_ref[...]
        copies = []
        for k in range(1, N_DEV):
            fx, fy, fc = (k >> 2) & 1, (k >> 1) & 1, k & 1
            peer = (x ^ fx, y ^ fy, c ^ fc)
            copies.append(pltpu.make_async_remote_copy(
                src_ref=gath.at[me], dst_ref=gath.at[me], send_sem=send_sems.at[k - 1], recv_sem=recv_sems.at[k - 1],
                device_id=peer, device_id_type=MESH))
        for cp in copies:
            cp.start()
        for cp in copies:
            cp.wait_recv()
        for cp in copies:
            cp.wait_send()
        g = gath[0]
        for d in range(1, N_DEV):
            g = g + gath[d]
        go_ref[...] = g

    def adam_body(g_ref, w_ref, m_ref, v_ref, *out_refs):
        g = g_ref[...]
        delta, m_new, v_new = _adam_math(g, w_ref[...], m_ref[...], v_ref[...])
        outs = iter(out_refs)
        for val in (g, delta, m_new, v_new):
            lo = 0
            for r in row_counts:
                next(outs)[...] = val[lo:lo + r]
                lo += r
        next(outs)[...] = g[SMALL_ROWS - SUBLANES:]

    vm = pl.BlockSpec(memory_space=pltpu.VMEM)
    reduce_body, dep_spec, dep_arg = _anchored(reduce_body, 1, dep)
    total = pl.pallas_call(
        reduce_body, in_specs=[vm] + dep_spec, out_specs=vm, out_shape=SDS((SMALL_ROWS, 128), F32),
        scratch_shapes=[pltpu.VMEM((N_DEV, SMALL_ROWS, 128), F32), pltpu.SemaphoreType.DMA((N_DEV - 1,)),
                        pltpu.SemaphoreType.DMA((N_DEV - 1,))],
        name="small_allreduce")(gpart, *dep_arg)
    out_shape = [SDS((r, 128), F32) for _ in range(4) for r in row_counts] + [SDS((SUBLANES, 128), F32)]
    res = pl.pallas_call(adam_body, in_specs=[vm] * 4, out_specs=[vm] * len(out_shape), out_shape=out_shape,
                         name="small_adam")(total, w, m, v)
    k = len(row_counts)
    return [res[i * k:(i + 1) * k] for i in range(4)], res[-1]


BIG = ("w_in", "conv_w", "w_conv_out", "w_attn_out", "w_o", "w_ffn_gate", "w_ffn_up", "w_ffn_down")
LATE_MERGE = ("w_conv_out", "w_attn_out", "w_o")
LATE_FFN = ("w_ffn_gate", "w_ffn_up", "w_ffn_down")
TRANSPOSED = ("w_in", "w_ffn_gate", "w_ffn_up")
COL_SHARDED = ("conv_w", "w_attn_out")
SMALL = ("norm1_g", "gate_b", "conv_b", "conv_ln_g", "conv_ln_b", "norm2_g", "norm_f_g")
WEIGHTS = ("norm1_g", "w_in", "gate_b", "conv_w", "conv_b", "conv_ln_g", "conv_ln_b", "w_conv_out", "w_attn_out", "w_o",
           "norm2_g", "w_ffn_gate", "w_ffn_up", "w_ffn_down", "norm_f_g")


def _shard2d(name, a):
    a = a.reshape(a.shape[-2], a.shape[-1])
    if name in TRANSPOSED:
        a = a.T
    if name == "conv_w":
        a = jnp.pad(a, ((0, CONV_PAD - CONV_K), (0, 0)))
    return a


def _from_shard2d(name, val, shape):
    if name in TRANSPOSED:
        val = val.T
    if name == "conv_w":
        val = val[:CONV_K]
    return val.reshape(shape)


def _gathered_to_full(name, g):
    if name in COL_SHARDED:
        return g.transpose(1, 0, 2).reshape(g.shape[1], N_DEV * g.shape[2])
    return g.reshape(N_DEV * g.shape[1], g.shape[2])


def _full_to_blocks(name, g):
    if name in COL_SHARDED:
        return g.reshape(g.shape[0], N_DEV, g.shape[1] // N_DEV).transpose(1, 0, 2)
    return g.reshape(N_DEV, g.shape[0] // N_DEV, g.shape[1])


def _pack_small(d, last_rows):
    vec = jnp.concatenate([d[n].reshape(-1) for n in SMALL]).reshape(SMALL_ROWS - SUBLANES, 128)
    return jnp.concatenate([vec, last_rows], axis=0)


def kernel(x, norm1_g, w_in, gate_b, conv_w, conv_b, conv_ln_g, conv_ln_b, w_conv_out, w_attn_out, w_o, norm2_g, w_ffn_gate, w_ffn_up, w_ffn_down, norm_f_g, loss_target, m_norm1_g, m_w_in, m_gate_b, m_conv_w, m_conv_b, m_conv_ln_g, m_conv_ln_b, m_w_conv_out, m_w_attn_out, m_w_o, m_norm2_g, m_w_ffn_gate, m_w_ffn_up, m_w_ffn_down, m_norm_f_g, v_norm1_g, v_w_in, v_gate_b, v_conv_w, v_conv_b, v_conv_ln_g, v_conv_ln_b, v_w_conv_out, v_w_attn_out, v_w_o, v_norm2_g, v_w_ffn_gate, v_w_ffn_up, v_w_ffn_down, v_norm_f_g):
    wts = dict(norm1_g=norm1_g, w_in=w_in, gate_b=gate_b, conv_w=conv_w, conv_b=conv_b, conv_ln_g=conv_ln_g,
               conv_ln_b=conv_ln_b, w_conv_out=w_conv_out, w_attn_out=w_attn_out, w_o=w_o, norm2_g=norm2_g,
               w_ffn_gate=w_ffn_gate, w_ffn_up=w_ffn_up, w_ffn_down=w_ffn_down, norm_f_g=norm_f_g)
    mom1 = dict(norm1_g=m_norm1_g, w_in=m_w_in, gate_b=m_gate_b, conv_w=m_conv_w, conv_b=m_conv_b, conv_ln_g=m_conv_ln_g,
                conv_ln_b=m_conv_ln_b, w_conv_out=m_w_conv_out, w_attn_out=m_w_attn_out, w_o=m_w_o, norm2_g=m_norm2_g,
                w_ffn_gate=m_w_ffn_gate, w_ffn_up=m_w_ffn_up, w_ffn_down=m_w_ffn_down, norm_f_g=m_norm_f_g)
    mom2 = dict(norm1_g=v_norm1_g, w_in=v_w_in, gate_b=v_gate_b, conv_w=v_conv_w, conv_b=v_conv_b, conv_ln_g=v_conv_ln_g,
                conv_ln_b=v_conv_ln_b, w_conv_out=v_w_conv_out, w_attn_out=v_w_attn_out, w_o=v_w_o, norm2_g=v_norm2_g,
                w_ffn_gate=v_w_ffn_gate, w_ffn_up=v_w_ffn_up, w_ffn_down=v_w_ffn_down, norm_f_g=v_norm_f_g)

    T = x.shape[0] * x.shape[1]
    x2 = x.reshape(T, D_MODEL)
    t2 = loss_target.reshape(T, D_MODEL)

    me = 4 * lax.axis_index("x") + 2 * lax.axis_index("y") + lax.axis_index("c")
    shards = {n: _shard2d(n, wts[n]) for n in BIG}
    sent = {n: shards[n] if n == "conv_w" else shards[n].astype(BF16) for n in BIG}
    small = {n: wts[n].reshape(1, -1) for n in SMALL}

    stage = {}

    def in_proj():
        w_in_blocks, conv_blocks = _all_gather([sent["w_in"], sent["conv_w"]])
        stage["merge"] = _send_start("gather", ALL_PEERS, "gather_start_merge", [sent[n] for n in LATE_MERGE],
                                     dep=w_in_blocks)
        stage["ffn"] = _send_start("gather", (1,) + OTHER_CHIPS, "gather_start_ffn", [sent[n] for n in LATE_FFN],
                                   dep=stage["merge"]["token"])
        w_in_t = _gathered_to_full("w_in", w_in_blocks)
        h, proj = _in_proj(x2, small["norm1_g"], w_in_t, stage["ffn"]["token"])
        return h, proj, {"w_in": w_in_t, "conv_w": _gathered_to_full("conv_w", conv_blocks)}

    def filled(names, srcs, lands):
        return {n: _gathered_to_full(n, lax.dynamic_update_slice(land, src[None], (me, 0, 0)))
                for n, src, land in zip(names, srcs, lands)}

    def pass_on(group, after):
        stage[group + "_srcs"], lands = _send_wait(stage[group], after, "gather_wait_" + group)
        stage[group + "_forward"] = _send_start("forward", OTHER_CHIPS, "forward_start_" + group, lands=lands)
        return stage[group + "_forward"]["token"]

    def arrived(group, names, after):
        _, lands = _send_wait(stage[group + "_forward"], after, "forward_wait_" + group)
        return filled(names, stage[group + "_srcs"], lands)

    def late_weights(which, after):
        if which == "after_attention":
            return {}
        if which is LATE_MERGE:
            srcs, lands = _send_wait(stage["merge"], after, "gather_wait_merge")
            return {**filled(LATE_MERGE, srcs, lands), "token": pass_on("ffn", after)}
        return arrived("ffn", LATE_FFN, after)

    scatters = []
    core = lax.axis_index("c").astype(jnp.int32).reshape(1)

    def emit(names, gw):
        blocks = [_full_to_blocks(n, gw[n]) for n in names]
        if "w_in" in names:
            sums = [_add_pair(g, r, core, "chip_sum_" + n) for n, g, r in zip(names, blocks, _exchange_sibling(blocks))]
            started = _send_start("chip_scatter", OTHER_CHIPS, "scatter_start_" + names[0], sums)
        else:
            started = _send_start("scatter", ALL_PEERS, "scatter_start_" + names[0], blocks)
        scatters.append((names, started))
        return started["token"]

    loss_part, grad_x, gw, gsmall = _local_step(x2, t2, in_proj, small, late_weights, emit)

    grads, deltas, new_m, new_v = {}, {}, {}, {}
    after = grad_x
    for names, started in scatters:
        srcs, lands = _send_wait(started, after, "scatter_wait_" + names[0])
        mine = (me >> 1 if started["mode"] == "chip_scatter" else me).astype(jnp.int32).reshape(1)
        for n, src, land in zip(names, srcs, lands):
            g, d, mo, vo = _sum_adam(land, src, mine, shards[n], _shard2d(n, mom1[n]), _shard2d(n, mom2[n]), "adam_" + n)
            for dst, val in ((grads, g), (deltas, d), (new_m, mo), (new_v, vo)):
                dst[n] = _from_shard2d(n, val, wts[n].shape)
            after = g

    zeros, ones = jnp.zeros((SUBLANES, 128), F32), jnp.ones((SUBLANES, 128), F32)
    row_counts = [wts[n].size // 128 for n in SMALL]
    kinds, loss_rows = _small_allreduce_adam(
        _pack_small(gsmall, jnp.broadcast_to(loss_part, (SUBLANES, 128))), _pack_small(wts, zeros),
        _pack_small(mom1, zeros), _pack_small(mom2, ones), row_counts, after)
    for dst, vals in zip((grads, deltas, new_m, new_v), kinds):
        dst.update({n: val.reshape(wts[n].shape) for n, val in zip(SMALL, vals)})
    loss = loss_rows[0, 0]
    return (loss, grad_x.reshape(x.shape), *[grads[n] for n in WEIGHTS], *[deltas[n] for n in WEIGHTS],
            *[new_m[n] for n in WEIGHTS], *[new_v[n] for n in WEIGHTS])
```

```python
import math

import numpy as np
import jax
import jax.numpy as jnp
from jax import lax
from jax.experimental import pallas as pl
from jax.experimental.pallas import tpu as pltpu

F32 = jnp.float32
BF16 = jnp.bfloat16
SDS = jax.ShapeDtypeStruct
MESH = pl.DeviceIdType.MESH

D_MODEL = 1024
SEQ = 2048
HEAD_DIM = 64
GROUPS = ((128, 1), (512, 4), (2048, 16))
HEADS_PER_GROUP = 8
N_HEADS = 24
ATTN_WIDTH = N_HEADS * HEAD_DIM
ATTN_OUT = HEADS_PER_GROUP * HEAD_DIM
CONV_K = 31
CONV_PAD = 32
D_FF = 2816
IN_WIDTH = 3 * ATTN_WIDTH + 2 * D_MODEL + 2 * D_MODEL
RMS_EPS = 1e-6
LN_EPS = 1e-5
Q_BLOCK = 128
LANES = 128
NEG = -1e30
N_DEV = 8

ADAM_LR = 0.001
ADAM_B1 = 0.9
ADAM_B2 = 0.999
ADAM_EPS = 1e-08
ADAM_WD = 0.01
ADAM_STEP = 10


def _alibi_slope_list(n):
    def pow2(m):
        start = 2.0 ** (-8.0 / m)
        return [start ** (i + 1) for i in range(m)]
    if math.log2(n).is_integer():
        return pow2(n)
    c = 2 ** math.floor(math.log2(n))
    return pow2(c) + _alibi_slope_list(2 * c)[0::2][: n - c]


def _slopes_times_dilation():
    s = np.asarray(sorted(_alibi_slope_list(N_HEADS), reverse=True), dtype=np.float32).reshape(3, HEADS_PER_GROUP)
    r = np.asarray([g[1] for g in GROUPS], dtype=np.float32)[:, None]
    return (s * r).reshape(N_HEADS)


def _sigmoid(x):
    return 0.5 * jnp.tanh(0.5 * x) + 0.5


def _dot(a, b):
    return jnp.dot(a, b, preferred_element_type=F32)


def _dot_nt(a, b):
    return lax.dot_general(a, b, (((1,), (1,)), ((), ())), preferred_element_type=F32)


def _dot_tn(a, b):
    return lax.dot_general(a, b, (((0,), (0,)), ((), ())), preferred_element_type=F32)


def _rowsum(x):
    return jnp.sum(x, axis=0, keepdims=True)


ANY_SPEC = pl.BlockSpec(memory_space=pl.ANY)


def _params(*sem):
    return pltpu.CompilerParams(dimension_semantics=sem)


def _anchored(body, n_in, dep):
    if dep is None:
        return body, [], []

    def wrapped(*refs):
        return body(*refs[:n_in], *refs[n_in + 1:])

    return wrapped, [pl.BlockSpec(memory_space=pl.ANY)], [dep]


IN_TM = 256
IN_CHUNK = 512


def _in_proj(x, g1, w_in_t, dep=None):
    T = x.shape[0]
    tm = IN_TM

    def body(x_ref, g_ref, w_hbm, h_ref, proj_ref, w_vmem, sem):
        @pl.when(pl.program_id(0) == 0)
        def _():
            cp = pltpu.make_async_copy(w_hbm, w_vmem, sem)
            cp.start()
            cp.wait()

        xv = x_ref[...]
        r = lax.rsqrt(jnp.mean(xv * xv, axis=-1, keepdims=True) + RMS_EPS)
        h = (xv * r * g_ref[...]).astype(BF16)
        h_ref[...] = h
        for lo in range(0, IN_WIDTH, IN_CHUNK):
            proj_ref[:, lo:lo + IN_CHUNK] = _dot_nt(h, w_vmem[lo:lo + IN_CHUNK, :])

    row = lambda n: pl.BlockSpec((tm, n), lambda i: (i, 0))
    body, dep_spec, dep_arg = _anchored(body, 3, dep)
    return pl.pallas_call(
        body, grid=(T // tm,),
        in_specs=[row(D_MODEL), pl.BlockSpec((1, D_MODEL), lambda i: (0, 0)), pl.BlockSpec(memory_space=pl.ANY)] + dep_spec,
        out_specs=[row(D_MODEL), row(IN_WIDTH)],
        out_shape=[SDS((T, D_MODEL), BF16), SDS((T, IN_WIDTH), F32)],
        scratch_shapes=[pltpu.VMEM((IN_WIDTH, D_MODEL), BF16), pltpu.SemaphoreType.DMA],
        compiler_params=_params("arbitrary"), name="in_proj")(x, g1, w_in_t, *dep_arg)


def _mm_tn(a, b, out_dtype, name, tn, tt=1024):
    T, K = a.shape
    N = b.shape[1]
    nt = T // tt

    def body(a_ref, b_ref, o_ref, acc):
        t = pl.program_id(1)

        @pl.when(t == 0)
        def _():
            acc[...] = jnp.zeros_like(acc)

        acc[...] += _dot_tn(a_ref[...], b_ref[...])

        @pl.when(t == nt - 1)
        def _():
            o_ref[...] = acc[...].astype(o_ref.dtype)

    return pl.pallas_call(
        body, grid=(N // tn, nt),
        in_specs=[pl.BlockSpec((tt, K), lambda j, t: (t, 0)),
                  pl.BlockSpec((tt, tn), lambda j, t: (t, j))],
        out_specs=pl.BlockSpec((K, tn), lambda j, t: (0, j)),
        out_shape=SDS((K, N), out_dtype),
        scratch_shapes=[pltpu.VMEM((K, tn), F32)],
        compiler_params=_params("parallel", "arbitrary"), name=name)(a, b)


def _gather_classes(src_ref, dst, r, row0=0):
    L = SEQ // r
    for c in range(r):
        dst[row0 + c * L:row0 + (c + 1) * L, :] = src_ref[0, pl.ds(c, L, stride=r), :].astype(dst.dtype)


def _scatter_classes(src, dst, r, row0=0):
    L = SEQ // r
    for c in range(r):
        dst[pl.ds(c, L, stride=r), :] = src[row0 + c * L:row0 + (c + 1) * L, :].astype(dst.dtype)


def _attn_masks(slope_r):
    qi = lax.broadcasted_iota(jnp.int32, (Q_BLOCK, Q_BLOCK), 0)
    kj = lax.broadcasted_iota(jnp.int32, (Q_BLOCK, Q_BLOCK), 1)
    rel = (qi - kj).astype(F32)
    bias_cur = jnp.where(qi >= kj, -slope_r * rel, NEG)
    bias_prev = jnp.where(qi <= kj, -slope_r * (rel + float(Q_BLOCK)), NEG)
    return bias_cur, bias_prev


def _store_biases(bias, sl_ref, g, hp):
    for hh in range(2):
        cur, prev = _attn_masks(sl_ref[g * HEADS_PER_GROUP + 2 * hp + hh])
        rows = slice(hh * Q_BLOCK, (hh + 1) * Q_BLOCK)
        bias[rows, 0:Q_BLOCK] = prev
        bias[rows, Q_BLOCK:] = cur


def _transpose_pairs(src, dst):
    dst[0, :, 0:Q_BLOCK] = jnp.zeros((LANES, Q_BLOCK), dst.dtype)
    nblk = SEQ // Q_BLOCK
    for b in range(nblk):
        t = src[(b + 1) * Q_BLOCK:(b + 2) * Q_BLOCK, :].T
        dst[b, :, Q_BLOCK:] = t
        if b + 1 < nblk:
            dst[b + 1, :, 0:Q_BLOCK] = t


def _stack_heads(t, low):
    z = jnp.zeros_like(t)
    return jnp.concatenate([jnp.where(low, t, z), jnp.where(low, z, t)], axis=0)


def _unstack_heads(t2, low):
    return jnp.where(low, t2[:Q_BLOCK], t2[Q_BLOCK:])


def _key_span(u, nb):
    off = u * Q_BLOCK
    if u % nb == 0:
        return slice(off + Q_BLOCK, off + 2 * Q_BLOCK), slice(Q_BLOCK, 2 * Q_BLOCK)
    return slice(off, off + 2 * Q_BLOCK), slice(0, 2 * Q_BLOCK)


def _attn_fwd(qkv, slopes_r, batch, dep=None):
    nblk = SEQ // Q_BLOCK

    def body(sl_ref, *refs):
        qkv_refs = refs[:9]
        att_ref, lse_ref = refs[9:11]
        qd, kd, vd, kt, opos, lpos, bias = refs[11:]
        hp = pl.program_id(1)
        low = lax.broadcasted_iota(jnp.int32, (Q_BLOCK, LANES), 1) < HEAD_DIM

        for g in range(3):
            r = GROUPS[g][1]
            nb = SEQ // r // Q_BLOCK
            _gather_classes(qkv_refs[3 * g], qd, r)
            kd[0:Q_BLOCK, :] = jnp.zeros((Q_BLOCK, LANES), BF16)
            vd[0:Q_BLOCK, :] = jnp.zeros((Q_BLOCK, LANES), BF16)
            _gather_classes(qkv_refs[3 * g + 1], kd, r, Q_BLOCK)
            _gather_classes(qkv_refs[3 * g + 2], vd, r, Q_BLOCK)
            _transpose_pairs(kd, kt)
            _store_biases(bias, sl_ref, g, hp)

            for u in range(nblk):
                keys, cols = _key_span(u, nb)
                q2 = _stack_heads(qd[u * Q_BLOCK:(u + 1) * Q_BLOCK, :], low)
                s = _dot(q2, kt[u, :, cols]) * 0.125 + bias[:, cols]
                m = jnp.max(s, axis=-1, keepdims=True)
                p = jnp.exp(s - m)
                l = jnp.sum(p, axis=-1, keepdims=True)
                o2 = _dot(p.astype(BF16), vd[keys, :]) * (1.0 / l)
                lse2 = m + jnp.log(l)
                rows = pl.ds(u // nb + (u % nb) * (Q_BLOCK * r), Q_BLOCK, stride=r)
                opos[g, rows, :] = _unstack_heads(o2, low)
                lpos[g, rows, :] = jnp.where(low, lse2[:Q_BLOCK], lse2[Q_BLOCK:])

        def merge(i, carry):
            rows = pl.ds(pl.multiple_of(i * 256, 256), 256)
            l0, l1, l2 = lpos[0, rows, :], lpos[1, rows, :], lpos[2, rows, :]
            m = jnp.maximum(jnp.maximum(l0, l1), l2)
            e0, e1, e2 = jnp.exp(l0 - m), jnp.exp(l1 - m), jnp.exp(l2 - m)
            den = e0 + e1 + e2
            att = (e0 * opos[0, rows, :] + e1 * opos[1, rows, :] + e2 * opos[2, rows, :]) / den
            att_ref[0, rows, :] = att.astype(att_ref.dtype)
            lse_ref[0, rows, :] = m + jnp.log(den)
            return carry

        lax.fori_loop(0, SEQ // 256, merge, 0)

    def col(sec, g):
        return pl.BlockSpec((1, SEQ, LANES), lambda b, hp: (b, 0, sec * 12 + g * 4 + hp))

    out = pl.BlockSpec((1, SEQ, LANES), lambda b, hp: (b, 0, hp))
    body, dep_spec, dep_arg = _anchored(body, 10, dep)
    return pl.pallas_call(
        body, grid=(batch, 4),
        in_specs=[pl.BlockSpec(memory_space=pltpu.SMEM)] + [col(sec, g) for g in range(3) for sec in range(3)] + dep_spec,
        out_specs=[out, out],
        out_shape=[SDS((batch, SEQ, ATTN_OUT), BF16), SDS((batch, SEQ, ATTN_OUT), F32)],
        scratch_shapes=[pltpu.VMEM((SEQ, LANES), BF16), pltpu.VMEM((Q_BLOCK + SEQ, LANES), BF16),
                        pltpu.VMEM((Q_BLOCK + SEQ, LANES), BF16), pltpu.VMEM((nblk, LANES, 2 * Q_BLOCK), BF16),
                        pltpu.VMEM((3, SEQ, LANES), F32), pltpu.VMEM((3, SEQ, LANES), F32),
                        pltpu.VMEM((2 * Q_BLOCK, 2 * Q_BLOCK), F32)],
        compiler_params=_params("parallel", "parallel"), name="attn_fwd")(slopes_r, *([qkv] * 9), *dep_arg)


def _attn_bwd(qkv, datt, lse, dsum, slopes_r, batch):
    nblk = SEQ // Q_BLOCK

    def body(sl_ref, q_ref, k_ref, v_ref, do_ref, l_ref, d_ref, dq_ref, dk_ref, dv_ref,
             qd, kd, vd, dod, ld, dd, dq_acc, dk_acc, dv_acc, dk_part, dv_part, stage, bias):
        gid, hp = pl.program_id(1), pl.program_id(2)
        low = lax.broadcasted_iota(jnp.int32, (Q_BLOCK, LANES), 1) < HEAD_DIM

        def section(g):
            r = GROUPS[g][1]
            nb = SEQ // r // Q_BLOCK
            _gather_classes(q_ref, qd, r)
            kd[0:Q_BLOCK, :] = jnp.zeros((Q_BLOCK, LANES), BF16)
            vd[0:Q_BLOCK, :] = jnp.zeros((Q_BLOCK, LANES), BF16)
            _gather_classes(k_ref, kd, r, Q_BLOCK)
            _gather_classes(v_ref, vd, r, Q_BLOCK)
            _gather_classes(do_ref, dod, r)
            _gather_classes(l_ref, ld, r)
            _gather_classes(d_ref, dd, r)
            _store_biases(bias, sl_ref, g, hp)

            for u in range(nblk):
                keys, cols = _key_span(u, nb)
                rows = slice(u * Q_BLOCK, (u + 1) * Q_BLOCK)
                q2 = _stack_heads(qd[rows, :], low)
                do2 = _stack_heads(dod[rows, :], low)
                lse_t = ld[rows, :]
                dsum_t = dd[rows, :]
                lse2 = jnp.concatenate([lse_t[:, 0:1], lse_t[:, HEAD_DIM:HEAD_DIM + 1]], axis=0)
                dsum2 = jnp.concatenate([dsum_t[:, 0:1], dsum_t[:, HEAD_DIM:HEAD_DIM + 1]], axis=0)
                s = _dot_nt(q2, kd[keys, :]) * 0.125 + bias[:, cols]
                p = jnp.exp(s - lse2)
                ds = (p * (_dot_nt(do2, vd[keys, :]) - dsum2)).astype(BF16)
                dq_acc[rows, :] = _unstack_heads(_dot(ds, kd[keys, :]), low) * 0.125
                dk_part[u, cols, :] = _dot_tn(ds, q2) * 0.125
                dv_part[u, cols, :] = _dot_tn(p.astype(BF16), do2)

            for part, acc in ((dk_part, dk_acc), (dv_part, dv_acc)):
                for b in range(nblk):
                    t = part[b, Q_BLOCK:, :]
                    if b + 1 < nblk and (b + 1) % nb != 0:
                        t = t + part[b + 1, 0:Q_BLOCK, :]
                    acc[b * Q_BLOCK:(b + 1) * Q_BLOCK, :] = t
            for acc, out_ref in ((dq_acc, dq_ref), (dk_acc, dk_ref), (dv_acc, dv_ref)):
                _scatter_classes(acc, stage, r)
                out_ref[0] = stage[...].astype(out_ref.dtype)

        for g in range(3):
            pl.when(gid == g)(lambda g=g: section(g))

    def col(sec):
        return pl.BlockSpec((1, SEQ, LANES), lambda b, g, hp: (b, 0, sec * 12 + g * 4 + hp))

    pos = pl.BlockSpec((1, SEQ, LANES), lambda b, g, hp: (b, 0, hp))
    dout = pl.BlockSpec((1, SEQ, LANES), lambda b, g, hp: (b, 0, g * 4 + hp))
    out = SDS((batch, SEQ, ATTN_WIDTH), BF16)
    seq_bf = pltpu.VMEM((SEQ, LANES), BF16)
    seq_f = pltpu.VMEM((SEQ, LANES), F32)
    pad_bf = pltpu.VMEM((Q_BLOCK + SEQ, LANES), BF16)
    part = pltpu.VMEM((nblk, 2 * Q_BLOCK, LANES), F32)
    return pl.pallas_call(
        body, grid=(batch, 3, 4),
        in_specs=[pl.BlockSpec(memory_space=pltpu.SMEM), col(0), col(1), col(2), pos, pos, pos],
        out_specs=[dout, dout, dout],
        out_shape=[out, out, out],
        scratch_shapes=[seq_bf, pad_bf, pad_bf, seq_bf, seq_f, seq_f, seq_f, seq_f, seq_f, part, part, seq_f,
                        pltpu.VMEM((2 * Q_BLOCK, 2 * Q_BLOCK), F32)],
        compiler_params=_params("parallel", "parallel", "parallel"), name="attn_bwd")(
            slopes_r, qkv, qkv, qkv, datt, lse, dsum)


CONV_TC = 128
U_BLOCK0 = 3 * ATTN_WIDTH // CONV_TC
CONV_ROWS = 128
SUBLANES = 8


SHIFT_TAIL = CONV_PAD - SUBLANES
CONV_CHUNKS = SEQ // CONV_ROWS


def _fill_shifted_rows(sh, c):
    lo = c * CONV_ROWS + (SHIFT_TAIL if c else 0)
    hi = (c + 1) * CONV_ROWS + SHIFT_TAIL
    for s in range(1, SUBLANES):
        sh[s, lo:hi, :] = sh[0, lo + s:hi + s, :]


def _tap(sh, base, offset):
    s = offset % SUBLANES
    lo = base + offset - s
    return sh[s, lo:lo + CONV_ROWS, :]


def _conv_fwd(u, conv_w, conv_b, batch, dep=None):
    nct = D_MODEL // CONV_TC

    def body(ua_ref, ub_ref, w_ref, b_ref, o_ref, sh):
        sh[0, 0:CONV_PAD, :] = jnp.zeros((CONV_PAD, CONV_TC), F32)
        for c in range(CONV_CHUNKS):
            base = c * CONV_ROWS
            rows = slice(base, base + CONV_ROWS)
            sh[0, CONV_PAD + base:CONV_PAD + base + CONV_ROWS, :] = ua_ref[0, rows, :] * _sigmoid(ub_ref[0, rows, :])
            _fill_shifted_rows(sh, c)
            acc = jnp.broadcast_to(b_ref[...], (CONV_ROWS, CONV_TC))
            for t in range(CONV_K):
                acc = acc + _tap(sh, base, t + CONV_PAD - (CONV_K - 1)) * w_ref[t:t + 1, :]
            o_ref[0, rows, :] = acc

    body, dep_spec, dep_arg = _anchored(body, 4, dep)
    return pl.pallas_call(
        body, grid=(nct, batch),
        in_specs=[pl.BlockSpec((1, SEQ, CONV_TC), lambda j, b: (b, 0, U_BLOCK0 + j)),
                  pl.BlockSpec((1, SEQ, CONV_TC), lambda j, b: (b, 0, U_BLOCK0 + nct + j)),
                  pl.BlockSpec((CONV_PAD, CONV_TC), lambda j, b: (0, j)),
                  pl.BlockSpec((1, CONV_TC), lambda j, b: (0, j))] + dep_spec,
        out_specs=pl.BlockSpec((1, SEQ, CONV_TC), lambda j, b: (b, 0, j)),
        out_shape=SDS((batch, SEQ, D_MODEL), F32),
        scratch_shapes=[pltpu.VMEM((SUBLANES, SEQ + CONV_PAD, CONV_TC), F32)],
        compiler_params=_params("parallel", "parallel"), name="conv_fwd")(u, u, conv_w, conv_b, *dep_arg)


def _conv_bwd(u, dc1, conv_w, batch, dep=None):
    nct = D_MODEL // CONV_TC

    def body(ua_ref, ub_ref, d_ref, w_ref, dua_ref, dub_ref, gw_ref, gb_ref, shc, shd, gacc):
        b = pl.program_id(1)
        shc[0, 0:CONV_PAD, :] = jnp.zeros((CONV_PAD, CONV_TC), F32)
        shd[0, 0:SEQ, :] = d_ref[0]
        shd[0, SEQ:, :] = jnp.zeros((CONV_PAD, CONV_TC), F32)

        @pl.when(b == 0)
        def _():
            gacc[...] = jnp.zeros_like(gacc)
            gb_ref[...] = jnp.zeros_like(gb_ref)

        gb_ref[...] += _rowsum(d_ref[0])

        for c in range(CONV_CHUNKS):
            base = c * CONV_ROWS
            rows = slice(base, base + CONV_ROWS)
            ua = ua_ref[0, rows, :]
            sg = _sigmoid(ub_ref[0, rows, :])
            shc[0, CONV_PAD + base:CONV_PAD + base + CONV_ROWS, :] = ua * sg
            _fill_shifted_rows(shc, c)
            _fill_shifted_rows(shd, c)
            dcur = shd[0, rows, :]
            acc = jnp.zeros((CONV_ROWS, CONV_TC), F32)
            for t in range(CONV_K):
                acc = acc + _tap(shd, base, CONV_K - 1 - t) * w_ref[t:t + 1, :]
                prod = _tap(shc, base, t + CONV_PAD - (CONV_K - 1)) * dcur
                gacc[t] += jnp.sum(prod.reshape(CONV_ROWS // 8, 8, CONV_TC), axis=0)
            dua_ref[0, rows, :] = (acc * sg).astype(dua_ref.dtype)
            dub_ref[0, rows, :] = (acc * ua * sg * (1.0 - sg)).astype(dub_ref.dtype)

        @pl.when(b == batch - 1)
        def _():
            for t in range(CONV_K):
                gw_ref[t:t + 1, :] = jnp.sum(gacc[t], axis=0, keepdims=True)
            gw_ref[CONV_K:CONV_PAD, :] = jnp.zeros((CONV_PAD - CONV_K, CONV_TC), F32)

    du = SDS((batch, SEQ, D_MODEL), BF16)
    body, dep_spec, dep_arg = _anchored(body, 4, dep)
    return pl.pallas_call(
        body, grid=(nct, batch),
        in_specs=[pl.BlockSpec((1, SEQ, CONV_TC), lambda j, b: (b, 0, U_BLOCK0 + j)),
                  pl.BlockSpec((1, SEQ, CONV_TC), lambda j, b: (b, 0, U_BLOCK0 + nct + j)),
                  pl.BlockSpec((1, SEQ, CONV_TC), lambda j, b: (b, 0, j)),
                  pl.BlockSpec((CONV_PAD, CONV_TC), lambda j, b: (0, j))] + dep_spec,
        out_specs=[pl.BlockSpec((1, SEQ, CONV_TC), lambda j, b: (b, 0, j)),
                   pl.BlockSpec((1, SEQ, CONV_TC), lambda j, b: (b, 0, j)),
                   pl.BlockSpec((CONV_PAD, CONV_TC), lambda j, b: (0, j)),
                   pl.BlockSpec((1, CONV_TC), lambda j, b: (0, j))],
        out_shape=[du, du, SDS((CONV_PAD, D_MODEL), F32), SDS((1, D_MODEL), F32)],
        scratch_shapes=[pltpu.VMEM((SUBLANES, SEQ + CONV_PAD, CONV_TC), F32),
                        pltpu.VMEM((SUBLANES, SEQ + CONV_PAD, CONV_TC), F32),
                        pltpu.VMEM((CONV_K, 8, CONV_TC), F32)],
        compiler_params=_params("parallel", "arbitrary"), name="conv_bwd")(u, u, dc1, conv_w, *dep_arg)


MID_TM = 256


def _layernorm_stats(c1):
    mu = jnp.mean(c1, axis=-1, keepdims=True)
    cen = c1 - mu
    rs = lax.rsqrt(jnp.mean(cen * cen, axis=-1, keepdims=True) + LN_EPS)
    return cen * rs, rs


GATE_PARTS = 4
GATE_PART = 2 * D_MODEL // GATE_PARTS
GATE_PART0 = (IN_WIDTH - 2 * D_MODEL) // GATE_PART


def _gate_specs(tm):
    return [pl.BlockSpec((tm, GATE_PART), lambda i, k=k: (i, GATE_PART0 + k)) for k in range(GATE_PARTS)]


def _mid_fwd(att, c1, proj, x, w_a, w_c, w_o, gate_b, ln_g, ln_b, g2, dep=None):
    T = x.shape[0]
    tm = MID_TM

    def body(att_ref, c1_ref, lg0, lg1, lg2, lg3, x_ref, wa_ref, wc_ref, wo_ref, gb_ref, lng_ref, lnb_ref, g2_ref,
             c3_ref, ya_ref, yc_ref, mix_ref, x1_ref, h2_ref):
        logits = jnp.concatenate([lg0[...], lg1[...], lg2[...], lg3[...]], axis=1)
        ya = _dot(att_ref[...], wa_ref[...])
        xh, _ = _layernorm_stats(c1_ref[...])
        c2 = xh * lng_ref[...] + lnb_ref[...]
        c3 = (c2 * _sigmoid(c2)).astype(BF16)
        c3_ref[...] = c3
        yc = _dot(c3, wc_ref[...])
        gates = _sigmoid(logits + gb_ref[...])
        mix = (gates[:, :D_MODEL] * ya + gates[:, D_MODEL:] * yc).astype(BF16)
        ya_ref[...] = ya.astype(BF16)
        yc_ref[...] = yc.astype(BF16)
        mix_ref[...] = mix
        x1 = x_ref[...] + _dot(mix, wo_ref[...])
        x1_ref[...] = x1
        r = lax.rsqrt(jnp.mean(x1 * x1, axis=-1, keepdims=True) + RMS_EPS)
        h2_ref[...] = (x1 * r * g2_ref[...]).astype(BF16)

    row = lambda n: pl.BlockSpec((tm, n), lambda i: (i, 0))
    full = lambda a, b: pl.BlockSpec((a, b), lambda i: (0, 0))
    body, dep_spec, dep_arg = _anchored(body, 10 + GATE_PARTS, dep)
    return pl.pallas_call(
        body, grid=(T // tm,),
        in_specs=[row(ATTN_OUT), row(D_MODEL)] + _gate_specs(tm) + [row(D_MODEL),
                  full(ATTN_OUT, D_MODEL), full(D_MODEL, D_MODEL), full(D_MODEL, D_MODEL),
                  full(1, 2 * D_MODEL), full(1, D_MODEL), full(1, D_MODEL), full(1, D_MODEL)] + dep_spec,
        out_specs=[row(D_MODEL), row(D_MODEL), row(D_MODEL), row(D_MODEL), row(D_MODEL), row(D_MODEL)],
        out_shape=[SDS((T, D_MODEL), BF16), SDS((T, D_MODEL), BF16), SDS((T, D_MODEL), BF16), SDS((T, D_MODEL), BF16),
                   SDS((T, D_MODEL), F32), SDS((T, D_MODEL), BF16)],
        compiler_params=_params("parallel"), name="mid_fwd")(att, c1, *([proj] * GATE_PARTS), x, w_a, w_c, w_o,
                                                             gate_b, ln_g, ln_b, g2, *dep_arg)


def _mid_bwd(dx1b, ya, yc, proj, att, c1, w_a, w_c, w_o, gate_b, ln_g, ln_b, head_ones, dep=None):
    T = dx1b.shape[0]
    tm = MID_TM

    def body(dx_ref, ya_ref, yc_ref, lg0, lg1, lg2, lg3, att_ref, c1_ref, wa_ref, wc_ref, wo_ref, gb_ref, lng_ref,
             lnb_ref, e_ref, dlg_ref, dya_ref, dyc_ref, datt_ref, dsum_ref, dc1_ref, ggb_ref, glg_ref, glb_ref):
        logits = jnp.concatenate([lg0[...], lg1[...], lg2[...], lg3[...]], axis=1)
        @pl.when(pl.program_id(0) == 0)
        def _():
            ggb_ref[...] = jnp.zeros_like(ggb_ref)
            glg_ref[...] = jnp.zeros_like(glg_ref)
            glb_ref[...] = jnp.zeros_like(glb_ref)

        dmix = _dot_nt(dx_ref[...], wo_ref[...])
        gates = _sigmoid(logits + gb_ref[...])
        ga, gc = gates[:, :D_MODEL], gates[:, D_MODEL:]
        dla = dmix * ya_ref[...].astype(F32) * ga * (1.0 - ga)
        dlc = dmix * yc_ref[...].astype(F32) * gc * (1.0 - gc)
        dlg_ref[:, :D_MODEL] = dla.astype(BF16)
        dlg_ref[:, D_MODEL:] = dlc.astype(BF16)
        ggb_ref[:, :D_MODEL] += _rowsum(dla)
        ggb_ref[:, D_MODEL:] += _rowsum(dlc)
        dya = (dmix * ga).astype(BF16)
        dyc = (dmix * gc).astype(BF16)
        dya_ref[...] = dya
        dyc_ref[...] = dyc
        datt = _dot_nt(dya, wa_ref[...])
        datt_ref[...] = datt
        dsum_ref[...] = jnp.dot(datt * att_ref[...].astype(F32), e_ref[...], preferred_element_type=F32,
                                precision=lax.Precision.HIGHEST)
        dc3 = _dot_nt(dyc, wc_ref[...])
        xh, rs = _layernorm_stats(c1_ref[...])
        c2 = xh * lng_ref[...] + lnb_ref[...]
        sg = _sigmoid(c2)
        dc2 = dc3 * (sg * (1.0 + c2 * (1.0 - sg)))
        glg_ref[...] += _rowsum(dc2 * xh)
        glb_ref[...] += _rowsum(dc2)
        dxh = dc2 * lng_ref[...]
        dc1_ref[...] = rs * (dxh - jnp.mean(dxh, axis=-1, keepdims=True) - xh * jnp.mean(dxh * xh, axis=-1, keepdims=True))

    row = lambda n: pl.BlockSpec((tm, n), lambda i: (i, 0))
    full = lambda a, b: pl.BlockSpec((a, b), lambda i: (0, 0))
    body, dep_spec, dep_arg = _anchored(body, 12 + GATE_PARTS, dep)
    return pl.pallas_call(
        body, grid=(T // tm,),
        in_specs=[row(D_MODEL), row(D_MODEL), row(D_MODEL)] + _gate_specs(tm) + [row(ATTN_OUT), row(D_MODEL),
                  full(ATTN_OUT, D_MODEL), full(D_MODEL, D_MODEL), full(D_MODEL, D_MODEL),
                  full(1, 2 * D_MODEL), full(1, D_MODEL), full(1, D_MODEL), full(ATTN_OUT, ATTN_OUT)] + dep_spec,
        out_specs=[row(2 * D_MODEL), row(D_MODEL), row(D_MODEL), row(ATTN_OUT), row(ATTN_OUT), row(D_MODEL),
                   full(1, 2 * D_MODEL), full(1, D_MODEL), full(1, D_MODEL)],
        out_shape=[SDS((T, 2 * D_MODEL), BF16), SDS((T, D_MODEL), BF16), SDS((T, D_MODEL), BF16), SDS((T, ATTN_OUT), F32),
                   SDS((T, ATTN_OUT), F32), SDS((T, D_MODEL), F32),
                   SDS((1, 2 * D_MODEL), F32), SDS((1, D_MODEL), F32), SDS((1, D_MODEL), F32)],
        compiler_params=_params("arbitrary"), name="mid_bwd")(dx1b, ya, yc, *([proj] * GATE_PARTS), att, c1, w_a, w_c, w_o,
                                                               gate_b, ln_g, ln_b, head_ones, *dep_arg)


FFN_TM = 256
FFN_CHUNK = 1024
FFN_SUB = tuple((lo, min(lo + FFN_CHUNK, D_FF)) for lo in range(0, D_FF, FFN_CHUNK))


def _rms_bwd(dy_times_g, xh, r):
    return r * (dy_times_g - xh * jnp.mean(dy_times_g * xh, axis=-1, keepdims=True))


def _load_resident(pairs, sems):
    @pl.when(pl.program_id(0) == 0)
    def _():
        copies = [pltpu.make_async_copy(src, dst, sems.at[k]) for k, (src, dst) in enumerate(pairs)]
        for cp in copies:
            cp.start()
        for cp in copies:
            cp.wait()


def _ffn_fwd(h2, x1, target, gf, w_g_t, w_u_t, w_d):
    T = h2.shape[0]
    tm = FFN_TM

    def body(h_ref, x1_ref, t_ref, gf_ref, wg_hbm, wu_hbm, wd_hbm,
             a_ref, b_ref, f_ref, dx2_ref, dx2b_ref, loss_ref, gnf_ref, wg, wu, wd, sems):
        _load_resident(((wg_hbm, wg), (wu_hbm, wu), (wd_hbm, wd)), sems)

        @pl.when(pl.program_id(0) == 0)
        def _():
            loss_ref[...] = jnp.zeros_like(loss_ref)
            gnf_ref[...] = jnp.zeros_like(gnf_ref)

        h = h_ref[...]
        x2 = x1_ref[...]
        for lo, hi in FFN_SUB:
            a = _dot_nt(h, wg[lo:hi, :])
            b = _dot_nt(h, wu[lo:hi, :])
            f = (a * _sigmoid(a) * b).astype(BF16)
            a_ref[:, lo:hi] = a.astype(BF16)
            b_ref[:, lo:hi] = b.astype(BF16)
            f_ref[:, lo:hi] = f
            x2 = x2 + _dot(f, wd[lo:hi, :])

        r = lax.rsqrt(jnp.mean(x2 * x2, axis=-1, keepdims=True) + RMS_EPS)
        xh = x2 * r
        err = xh * gf_ref[...] - t_ref[...]
        loss_ref[...] += (0.5 / D_MODEL) * jnp.sum(err * err)
        dy = err * (1.0 / D_MODEL)
        gnf_ref[...] += _rowsum(dy * xh)
        dx2 = _rms_bwd(dy * gf_ref[...], xh, r)
        dx2_ref[...] = dx2
        dx2b_ref[...] = dx2.astype(BF16)

    row = lambda n: pl.BlockSpec((tm, n), lambda i: (i, 0))
    const = lambda n: pl.BlockSpec((1, n), lambda i: (0, 0))
    wshape = pltpu.VMEM((D_FF, D_MODEL), BF16)
    return pl.pallas_call(
        body, grid=(T // tm,),
        in_specs=[row(D_MODEL), row(D_MODEL), row(D_MODEL), const(D_MODEL), ANY_SPEC, ANY_SPEC, ANY_SPEC],
        out_specs=[row(D_FF), row(D_FF), row(D_FF), row(D_MODEL), row(D_MODEL), const(128), const(D_MODEL)],
        out_shape=[SDS((T, D_FF), BF16), SDS((T, D_FF), BF16), SDS((T, D_FF), BF16), SDS((T, D_MODEL), F32),
                   SDS((T, D_MODEL), BF16), SDS((1, 128), F32), SDS((1, D_MODEL), F32)],
        scratch_shapes=[wshape, wshape, wshape, pltpu.SemaphoreType.DMA((3,))],
        compiler_params=_params("arbitrary"), name="ffn_fwd")(h2, x1, target, gf, w_g_t, w_u_t, w_d)


def _ffn_bwd(dx2b, dx2, a, b, x1, g2, w_g_t, w_u_t, w_d):
    T = dx2.shape[0]
    tm = FFN_TM

    def body(dxb_ref, dx2_ref, a_ref, b_ref, x1_ref, g2_ref, wg_hbm, wu_hbm, wd_hbm,
             da_ref, db_ref, dx1_ref, dx1b_ref, gn2_ref, wg, wu, wd, sems):
        _load_resident(((wg_hbm, wg), (wu_hbm, wu), (wd_hbm, wd)), sems)

        @pl.when(pl.program_id(0) == 0)
        def _():
            gn2_ref[...] = jnp.zeros_like(gn2_ref)

        dxb = dxb_ref[...]
        dh2 = jnp.zeros((tm, D_MODEL), F32)
        for lo, hi in FFN_SUB:
            df = _dot_nt(dxb, wd[lo:hi, :])
            av = a_ref[:, lo:hi].astype(F32)
            bv = b_ref[:, lo:hi].astype(F32)
            sg = _sigmoid(av)
            db = (df * av * sg).astype(BF16)
            da = (df * bv * (sg * (1.0 + av * (1.0 - sg)))).astype(BF16)
            da_ref[:, lo:hi] = da
            db_ref[:, lo:hi] = db
            dh2 = dh2 + _dot(da, wg[lo:hi, :]) + _dot(db, wu[lo:hi, :])

        x1 = x1_ref[...]
        r = lax.rsqrt(jnp.mean(x1 * x1, axis=-1, keepdims=True) + RMS_EPS)
        xh = x1 * r
        gn2_ref[...] += _rowsum(dh2 * xh)
        dx1 = dx2_ref[...] + _rms_bwd(dh2 * g2_ref[...], xh, r)
        dx1_ref[...] = dx1
        dx1b_ref[...] = dx1.astype(BF16)

    row = lambda n: pl.BlockSpec((tm, n), lambda i: (i, 0))
    const = lambda n: pl.BlockSpec((1, n), lambda i: (0, 0))
    wshape = pltpu.VMEM((D_FF, D_MODEL), BF16)
    return pl.pallas_call(
        body, grid=(T // tm,),
        in_specs=[row(D_MODEL), row(D_MODEL), row(D_FF), row(D_FF), row(D_MODEL), const(D_MODEL),
                  ANY_SPEC, ANY_SPEC, ANY_SPEC],
        out_specs=[row(D_FF), row(D_FF), row(D_MODEL), row(D_MODEL), const(D_MODEL)],
        out_shape=[SDS((T, D_FF), BF16), SDS((T, D_FF), BF16), SDS((T, D_MODEL), F32), SDS((T, D_MODEL), BF16),
                   SDS((1, D_MODEL), F32)],
        scratch_shapes=[wshape, wshape, wshape, pltpu.SemaphoreType.DMA((3,))],
        compiler_params=_params("arbitrary"), name="ffn_bwd")(dx2b, dx2, a, b, x1, g2, w_g_t, w_u_t, w_d)


def _in_bwd(pieces, w_in_t, x, dx1, g1, dep=None):
    T = x.shape[0]
    tm = IN_TM
    npc = len(pieces)
    assert sum(p.shape[1] for p in pieces) == IN_WIDTH

    def body(*refs):
        p_refs = refs[:npc]
        w_hbm, x_ref, dx1_ref, g_ref, dx_ref, gn1_ref, w_vmem, sem = refs[npc:]

        @pl.when(pl.program_id(0) == 0)
        def _():
            cp = pltpu.make_async_copy(w_hbm, w_vmem, sem)
            cp.start()
            cp.wait()
            gn1_ref[...] = jnp.zeros_like(gn1_ref)

        dh = jnp.zeros((tm, D_MODEL), F32)
        col = 0
        for p_ref in p_refs:
            for j in range(p_ref.shape[1] // IN_CHUNK):
                dh = dh + _dot(p_ref[:, j * IN_CHUNK:(j + 1) * IN_CHUNK], w_vmem[col:col + IN_CHUNK, :])
                col += IN_CHUNK
        xv = x_ref[...]
        r = lax.rsqrt(jnp.mean(xv * xv, axis=-1, keepdims=True) + RMS_EPS)
        xh = xv * r
        gn1_ref[...] += _rowsum(dh * xh)
        dx_ref[...] = dx1_ref[...] + _rms_bwd(dh * g_ref[...], xh, r)

    row = lambda n: pl.BlockSpec((tm, n), lambda i: (i, 0))
    body, dep_spec, dep_arg = _anchored(body, npc + 4, dep)
    return pl.pallas_call(
        body, grid=(T // tm,),
        in_specs=[row(p.shape[1]) for p in pieces]
        + [pl.BlockSpec(memory_space=pl.ANY), row(D_MODEL), row(D_MODEL), pl.BlockSpec((1, D_MODEL), lambda i: (0, 0))]
        + dep_spec,
        out_specs=[row(D_MODEL), pl.BlockSpec((1, D_MODEL), lambda i: (0, 0))],
        out_shape=[SDS((T, D_MODEL), F32), SDS((1, D_MODEL), F32)],
        scratch_shapes=[pltpu.VMEM((IN_WIDTH, D_MODEL), BF16), pltpu.SemaphoreType.DMA],
        compiler_params=_params("arbitrary"), name="in_bwd")(*pieces, w_in_t, x, dx1, g1, *dep_arg)


def _local_step(x, target, in_proj, small, late_weights=None, emit=None):
    T = x.shape[0]
    batch = T // SEQ
    slopes_r = jnp.asarray(_slopes_times_dilation())
    emit = emit or (lambda names, grads: None)

    h, proj, w = in_proj()
    proj3 = proj.reshape(batch, SEQ, IN_WIDTH)

    att, lse = _attn_fwd(proj3, slopes_r, batch, w.get("token"))
    att = att.reshape(T, ATTN_OUT)
    if late_weights is not None:
        w = {**w, **late_weights("after_attention", att)}

    c1 = _conv_fwd(proj3, w["conv_w"], small["conv_b"], batch, w.get("token")).reshape(T, D_MODEL)
    if late_weights is not None:
        w = {**w, **late_weights(LATE_MERGE, (att, c1))}

    c3, ya, yc, mix, x1, h2 = _mid_fwd(
        att, c1, proj, x, w["w_attn_out"], w["w_conv_out"], w["w_o"],
        small["gate_b"], small["conv_ln_g"], small["conv_ln_b"], small["norm2_g"], w.get("token"))
    if late_weights is not None:
        w = {**w, **late_weights(LATE_FFN, h2)}

    a, b, f, dx2, dx2b, loss, g_normf = _ffn_fwd(h2, x1, target, small["norm_f_g"],
                                                   w["w_ffn_gate"], w["w_ffn_up"], w["w_ffn_down"])

    da, db, dx1, dx1b, g_norm2 = _ffn_bwd(dx2b, dx2, a, b, x1, small["norm2_g"],
                                           w["w_ffn_gate"], w["w_ffn_up"], w["w_ffn_down"])
    gw = {}
    gw["w_ffn_down"] = _mm_tn(f, dx2b, BF16, "gw_ffn_down", tn=1024)
    gw["w_ffn_gate"] = _mm_tn(da, h2, BF16, "gw_ffn_gate", tn=1024)
    gw["w_ffn_up"] = _mm_tn(db, h2, BF16, "gw_ffn_up", tn=1024)
    token = emit(("w_ffn_gate", "w_ffn_up", "w_ffn_down"), gw)

    head_ones = jnp.asarray(np.kron(np.eye(HEADS_PER_GROUP, dtype=np.float32), np.ones((HEAD_DIM, HEAD_DIM), np.float32)))
    dlogits, dya, dyc, datt, dsum, dc1, g_gate_b, g_ln_g, g_ln_b = _mid_bwd(
        dx1b, ya, yc, proj, att, c1, w["w_attn_out"], w["w_conv_out"], w["w_o"],
        small["gate_b"], small["conv_ln_g"], small["conv_ln_b"], head_ones, token)
    gw["w_o"] = _mm_tn(mix, dx1b, BF16, "gw_o", tn=1024)
    gw["w_attn_out"] = _mm_tn(att, dya, BF16, "gw_attn_out", tn=1024)
    gw["w_conv_out"] = _mm_tn(c3, dyc, BF16, "gw_conv_out", tn=1024)
    token = emit(("w_conv_out", "w_attn_out", "w_o"), gw)

    dua, dub, g_conv_w, g_conv_b = _conv_bwd(proj3, dc1.reshape(batch, SEQ, D_MODEL), w["conv_w"], batch, token)

    dq, dk, dv = _attn_bwd(proj3, datt.reshape(batch, SEQ, ATTN_OUT), lse, dsum.reshape(batch, SEQ, ATTN_OUT),
                           slopes_r, batch)
    pieces = [dq.reshape(T, ATTN_WIDTH), dk.reshape(T, ATTN_WIDTH), dv.reshape(T, ATTN_WIDTH),
              dua.reshape(T, D_MODEL), dub.reshape(T, D_MODEL), dlogits]

    names = ("q", "k", "v", "ua", "ub", "gate")
    gw["w_in"] = jnp.concatenate([_mm_tn(p, h, BF16, "gw_in_" + nm, tn=1024) for nm, p in zip(names, pieces)], axis=0)
    gw["conv_w"] = g_conv_w
    token = emit(("w_in", "conv_w"), gw)
    grad_x, g_norm1 = _in_bwd(pieces, w["w_in"], x, dx1, small["norm1_g"], token)

    gsmall = {"norm1_g": g_norm1, "gate_b": g_gate_b, "conv_b": g_conv_b, "conv_ln_g": g_ln_g, "conv_ln_b": g_ln_b,
              "norm2_g": g_norm2, "norm_f_g": g_normf}
    return loss, grad_x, gw, gsmall


ANY = pl.BlockSpec(memory_space=pl.ANY)


def _all_gather(arrs):
    n = len(arrs)

    def body(*refs):
        ins, outs = refs[:n], refs[n:2 * n]
        send_sems, recv_sems, local_sems = refs[2 * n:]
        x, y, c = lax.axis_index("x"), lax.axis_index("y"), lax.axis_index("c")
        me, sibling = (x, y, c), (x, y, 1 - c)
        chips = [(1 - x, y), (x, 1 - y), (1 - x, 1 - y)]

        def copy(a, k, block, to, src=None):
            px, py, pc = block
            dst = outs[a].at[4 * px + 2 * py + pc]
            return pltpu.make_async_remote_copy(
                src_ref=dst if src is None else src, dst_ref=dst,
                send_sem=send_sems.at[a, k], recv_sem=recv_sems.at[a, k], device_id=to, device_id_type=MESH)

        mine = [pltpu.make_async_copy(ins[a], outs[a].at[4 * x + 2 * y + c], local_sems.at[a]) for a in range(n)]
        for cp in mine:
            cp.start()
        first = []
        for j, chip in enumerate(chips):
            first += [copy(a, 1 + j, me, (*chip, c), src=ins[a]) for a in range(n)]
        first += [copy(a, 0, me, sibling, src=ins[a]) for a in range(n)]
        for cp in first:
            cp.start()
        passed = []
        for j, chip in enumerate(chips):
            for a in range(n):
                copy(a, 1 + j, (*chip, c), me).wait_recv()
                cp = copy(a, 4 + j, (*chip, c), sibling)
                cp.start()
                passed.append(cp)
        for a in range(n):
            copy(a, 0, sibling, me).wait_recv()
        for j, chip in enumerate(chips):
            for a in range(n):
                copy(a, 4 + j, (*chip, 1 - c), me).wait_recv()
        for cp in first + passed:
            cp.wait_send()
        for cp in mine:
            cp.wait()

    return pl.pallas_call(
        body, in_specs=[ANY] * n, out_specs=[ANY] * n,
        out_shape=[SDS((N_DEV,) + a.shape, a.dtype) for a in arrs],
        scratch_shapes=[pltpu.SemaphoreType.DMA((n, 7)), pltpu.SemaphoreType.DMA((n, 7)), pltpu.SemaphoreType.DMA((n,))],
        name="all_gather_weights")(*arrs)


HBM =pl.BlockSpec(memory_space=pltpu.HBM)
SEM = pl.BlockSpec(memory_space=pltpu.SEMAPHORE)
ALL_PEERS = tuple(range(1, N_DEV))
OTHER_CHIPS = (2, 4, 6)
SPLIT_EFFECT = pltpu.CompilerParams(has_side_effects=pltpu.SideEffectType.DATAFLOW_SIDE_EFFECTING)


def _exchange_copies(mode, ks, srcs, lands, send_sems, recv_sems):
    x, y, c = lax.axis_index("x"), lax.axis_index("y"), lax.axis_index("c")
    me = 4 * x + 2 * y + c
    send, recv = [], []
    for a in range(len(lands)):
        for i, k in enumerate(ks):
            peer = (x ^ ((k >> 2) & 1), y ^ ((k >> 1) & 1), c ^ (k & 1))
            pidx = 4 * peer[0] + 2 * peer[1] + peer[2]
            if mode == "gather":
                src, to, out_slot, in_slot = srcs[a], peer, me, pidx
            elif mode == "scatter":
                src, to, out_slot, in_slot = srcs[a].at[pidx], peer, me, pidx
            elif mode == "chip_scatter":
                src, to, out_slot, in_slot = srcs[a].at[pidx >> 1], peer, me >> 1, pidx >> 1
            else:
                src, to, out_slot, in_slot = lands[a].at[pidx], (x, y, 1 - c), pidx, pidx ^ 1
            s = a * len(ks) + i
            send.append(pltpu.make_async_remote_copy(
                src_ref=src, dst_ref=lands[a].at[out_slot], send_sem=send_sems.at[s], recv_sem=recv_sems.at[s],
                device_id=to, device_id_type=MESH))
            recv.append(pltpu.make_async_remote_copy(
                src_ref=src, dst_ref=lands[a].at[in_slot], send_sem=send_sems.at[s], recv_sem=recv_sems.at[s],
                device_id=to, device_id_type=MESH))
    return send, recv


def _send_start(mode, ks, name, srcs=(), lands=None, dep=None):
    srcs = list(srcs)
    if lands is None:
        slots = 4 if mode == "chip_scatter" else N_DEV
        lands = [lax.empty((slots,) + (s.shape if mode == "gather" else s.shape[1:]), s.dtype) for s in srcs]
    ns, nl = len(srcs), len(lands)
    nsem = nl * len(ks)

    def body(*refs):
        send, _ = _exchange_copies(mode, ks, refs[:ns], refs[ns:ns + nl], refs[ns + nl], refs[ns + nl + 1])
        for cp in send:
            cp.start()
        token = refs[-1]
        token[...] = jnp.zeros_like(token)

    both = srcs + list(lands)
    body, dep_spec, dep_arg = _anchored(body, ns + nl, dep)
    res = pl.pallas_call(
        body, name=name,
        out_shape=(pltpu.SemaphoreType.DMA((nsem,)), pltpu.SemaphoreType.DMA((nsem,)),
                   *[pltpu.HBM(a.shape, a.dtype) for a in both], SDS((8, 128), F32)),
        in_specs=[HBM] * (ns + nl) + dep_spec,
        out_specs=(SEM, SEM, *([HBM] * (ns + nl)), pl.BlockSpec(memory_space=pltpu.VMEM)),
        input_output_aliases={i: 2 + i for i in range(ns + nl)}, compiler_params=SPLIT_EFFECT,
    )(*[pltpu.with_memory_space_constraint(a, pltpu.HBM) for a in both], *dep_arg)
    return dict(mode=mode, ks=ks, send_sems=res[0], recv_sems=res[1], srcs=res[2:2 + ns], lands=res[2 + ns:2 + ns + nl],
                token=res[-1])


def _send_wait(started, after, name):
    ns, nl = len(started["srcs"]), len(started["lands"])

    def body(*refs):
        send, recv = _exchange_copies(started["mode"], started["ks"], refs[:ns], refs[ns:ns + nl],
                                      refs[ns + nl], refs[ns + nl + 1])
        for cp in send:
            cp.wait_send()
        for cp in recv:
            cp.wait_recv()

    both = list(started["srcs"]) + list(started["lands"])
    after = after if isinstance(after, (tuple, list)) else (after,)
    res = pl.pallas_call(
        body, name=name,
        out_shape=tuple(pltpu.HBM(a.shape, a.dtype) for a in both),
        in_specs=[HBM] * (ns + nl) + [SEM, SEM] + [ANY] * len(after), out_specs=tuple([HBM] * (ns + nl)),
        input_output_aliases={i: i for i in range(ns + nl)}, compiler_params=SPLIT_EFFECT,
    )(*both, started["send_sems"], started["recv_sems"], *after)
    return res[:ns], res[ns:]


def _exchange_sibling(gs):
    n = len(gs)

    def body(*refs):
        ins, outs = refs[:n], refs[n:2 * n]
        send_sems, recv_sems = refs[2 * n:]
        x, y, c = lax.axis_index("x"), lax.axis_index("y"), lax.axis_index("c")
        copies = []
        for a in range(n):
            for j in range(4):
                copies.append(pltpu.make_async_remote_copy(
                    src_ref=ins[a].at[2 * j + (1 - c)], dst_ref=outs[a].at[j],
                    send_sem=send_sems.at[a, j], recv_sem=recv_sems.at[a, j],
                    device_id=(x, y, 1 - c), device_id_type=MESH))
        for cp in copies:
            cp.start()
        for cp in copies:
            cp.wait_recv()
        for cp in copies:
            cp.wait_send()

    return pl.pallas_call(
        body, in_specs=[ANY] * n, out_specs=[ANY] * n,
        out_shape=[SDS((4,) + g.shape[1:], g.dtype) for g in gs],
        scratch_shapes=[pltpu.SemaphoreType.DMA((n, 4)), pltpu.SemaphoreType.DMA((n, 4))],
        name="reduce_scatter_sibling")(*gs)


def _add_pair(g, r1, core, name):
    _, rows, cols = g.shape
    tr = _row_tile(rows, cols, 3 * g.dtype.itemsize)

    def body(c_ref, g_ref, r_ref, o_ref):
        o_ref[...] = (g_ref[...].astype(F32) + r_ref[...].astype(F32)).astype(o_ref.dtype)

    return pl.pallas_call(
        body,
        grid_spec=pltpu.PrefetchScalarGridSpec(
            num_scalar_prefetch=1, grid=(4, rows // tr),
            in_specs=[pl.BlockSpec((1, tr, cols), lambda j, i, c_ref: (2 * j + c_ref[0], i, 0)),
                      pl.BlockSpec((1, tr, cols), lambda j, i, c_ref: (j, i, 0))],
            out_specs=pl.BlockSpec((1, tr, cols), lambda j, i, c_ref: (j, i, 0))),
        out_shape=SDS((4, rows, cols), g.dtype),
        compiler_params=_params("parallel", "parallel"), name=name)(core, g, r1)


def _row_tile(rows, cols, itemsize_total):
    budget = (4 << 20) // max(1, cols * itemsize_total)
    if rows <= budget:
        return rows
    t = rows
    while t > budget and t % 2 == 0 and (t // 2) % 16 == 0:
        t //= 2
    return t


def _adam_math(g, w, m, v):
    m_new = ADAM_B1 * m + (1.0 - ADAM_B1) * g
    v_new = ADAM_B2 * v + (1.0 - ADAM_B2) * (g * g)
    m_hat = m_new / (1.0 - ADAM_B1 ** ADAM_STEP)
    v_hat = v_new / (1.0 - ADAM_B2 ** ADAM_STEP)
    delta = -ADAM_LR * (m_hat / (jnp.sqrt(v_hat) + ADAM_EPS) + ADAM_WD * w)
    return delta, m_new, v_new


def _sum_adam(parts, own, mine, w, m, v, name):
    rows, cols = w.shape
    nparts = parts.shape[0]
    tr = _row_tile(rows, cols, (nparts + 1) * parts.dtype.itemsize + 7 * 4)

    def body(mine_ref, p_ref, own_ref, w_ref, m_ref, v_ref, g_ref, d_ref, mo_ref, vo_ref):
        g = None
        for s in range(nparts):
            part = jnp.where(mine_ref[0] == s, own_ref[0], p_ref[s]).astype(F32)
            g = part if g is None else g + part
        delta, m_new, v_new = _adam_math(g, w_ref[...], m_ref[...], v_ref[...])
        g_ref[...] = g
        d_ref[...] = delta
        mo_ref[...] = m_new
        vo_ref[...] = v_new

    blk = pl.BlockSpec((tr, cols), lambda i, mine_ref: (i, 0))
    out = SDS((rows, cols), F32)
    return pl.pallas_call(
        body,
        grid_spec=pltpu.PrefetchScalarGridSpec(
            num_scalar_prefetch=1, grid=(rows // tr,),
            in_specs=[pl.BlockSpec((nparts, tr, cols), lambda i, mine_ref: (0, i, 0)),
                      pl.BlockSpec((1, tr, cols), lambda i, mine_ref: (mine_ref[0], i, 0)), blk, blk, blk],
            out_specs=[blk, blk, blk, blk]),
        out_shape=[out, out, out, out],
        compiler_params=_params("parallel"), name=name)(mine, parts, own, w, m, v)


SMALL_ROWS = 72


def _small_allreduce_adam(gpart, w, m, v, row_counts, dep=None):
    def reduce_body(g_ref, go_ref, gath, send_sems, recv_sems):
        x, y, c = lax.axis_index("x"), lax.axis_index("y"), lax.axis_index("c")
        me = 4 * x + 2 * y + c
        gath[me] = g_ref[...]
        copies = []
        for k in range(1, N_DEV):
            fx, fy, fc = (k >> 2) & 1, (k >> 1) & 1, k & 1
            peer = (x ^ fx, y ^ fy, c ^ fc)
            copies.append(pltpu.make_async_remote_copy(
                src_ref=gath.at[me], dst_ref=gath.at[me], send_sem=send_sems.at[k - 1], recv_sem=recv_sems.at[k - 1],
                device_id=peer, device_id_type=MESH))
        for cp in copies:
            cp.start()
        for cp in copies:
            cp.wait_recv()
        for cp in copies:
            cp.wait_send()
        g = gath[0]
        for d in range(1, N_DEV):
            g = g + gath[d]
        go_ref[...] = g

    def adam_body(g_ref, w_ref, m_ref, v_ref, *out_refs):
        g = g_ref[...]
        delta, m_new, v_new = _adam_math(g, w_ref[...], m_ref[...], v_ref[...])
        outs = iter(out_refs)
        for val in (g, delta, m_new, v_new):
            lo = 0
            for r in row_counts:
                next(outs)[...] = val[lo:lo + r]
                lo += r
        next(outs)[...] = g[SMALL_ROWS - SUBLANES:]

    vm = pl.BlockSpec(memory_space=pltpu.VMEM)
    reduce_body, dep_spec, dep_arg = _anchored(reduce_body, 1, dep)
    total = pl.pallas_call(
        reduce_body, in_specs=[vm] + dep_spec, out_specs=vm, out_shape=SDS((SMALL_ROWS, 128), F32),
        scratch_shapes=[pltpu.VMEM((N_DEV, SMALL_ROWS, 128), F32), pltpu.SemaphoreType.DMA((N_DEV - 1,)),
                        pltpu.SemaphoreType.DMA((N_DEV - 1,))],
        name="small_allreduce")(gpart, *dep_arg)
    out_shape = [SDS((r, 128), F32) for _ in range(4) for r in row_counts] + [SDS((SUBLANES, 128), F32)]
    res = pl.pallas_call(adam_body, in_specs=[vm] * 4, out_specs=[vm] * len(out_shape), out_shape=out_shape,
                         name="small_adam")(total, w, m, v)
    k = len(row_counts)
    return [res[i * k:(i + 1) * k] for i in range(4)], res[-1]


BIG = ("w_in", "conv_w", "w_conv_out", "w_attn_out", "w_o", "w_ffn_gate", "w_ffn_up", "w_ffn_down")
LATE_MERGE = ("w_conv_out", "w_attn_out", "w_o")
LATE_FFN = ("w_ffn_gate", "w_ffn_up", "w_ffn_down")
TRANSPOSED = ("w_in", "w_ffn_gate", "w_ffn_up")
COL_SHARDED = ("conv_w", "w_attn_out")
SMALL = ("norm1_g", "gate_b", "conv_b", "conv_ln_g", "conv_ln_b", "norm2_g", "norm_f_g")
WEIGHTS = ("norm1_g", "w_in", "gate_b", "conv_w", "conv_b", "conv_ln_g", "conv_ln_b", "w_conv_out", "w_attn_out", "w_o",
           "norm2_g", "w_ffn_gate", "w_ffn_up", "w_ffn_down", "norm_f_g")


def _shard2d(name, a):
    a = a.reshape(a.shape[-2], a.shape[-1])
    if name in TRANSPOSED:
        a = a.T
    if name == "conv_w":
        a = jnp.pad(a, ((0, CONV_PAD - CONV_K), (0, 0)))
    return a


def _from_shard2d(name, val, shape):
    if name in TRANSPOSED:
        val = val.T
    if name == "conv_w":
        val = val[:CONV_K]
    return val.reshape(shape)


def _gathered_to_full(name, g):
    if name in COL_SHARDED:
        return g.transpose(1, 0, 2).reshape(g.shape[1], N_DEV * g.shape[2])
    return g.reshape(N_DEV * g.shape[1], g.shape[2])


def _full_to_blocks(name, g):
    if name in COL_SHARDED:
        return g.reshape(g.shape[0], N_DEV, g.shape[1] // N_DEV).transpose(1, 0, 2)
    return g.reshape(N_DEV, g.shape[0] // N_DEV, g.shape[1])


def _pack_small(d, last_rows):
    vec = jnp.concatenate([d[n].reshape(-1) for n in SMALL]).reshape(SMALL_ROWS - SUBLANES, 128)
    return jnp.concatenate([vec, last_rows], axis=0)


def kernel(x, norm1_g, w_in, gate_b, conv_w, conv_b, conv_ln_g, conv_ln_b, w_conv_out, w_attn_out, w_o, norm2_g, w_ffn_gate, w_ffn_up, w_ffn_down, norm_f_g, loss_target, m_norm1_g, m_w_in, m_gate_b, m_conv_w, m_conv_b, m_conv_ln_g, m_conv_ln_b, m_w_conv_out, m_w_attn_out, m_w_o, m_norm2_g, m_w_ffn_gate, m_w_ffn_up, m_w_ffn_down, m_norm_f_g, v_norm1_g, v_w_in, v_gate_b, v_conv_w, v_conv_b, v_conv_ln_g, v_conv_ln_b, v_w_conv_out, v_w_attn_out, v_w_o, v_norm2_g, v_w_ffn_gate, v_w_ffn_up, v_w_ffn_down, v_norm_f_g):
    wts = dict(norm1_g=norm1_g, w_in=w_in, gate_b=gate_b, conv_w=conv_w, conv_b=conv_b, conv_ln_g=conv_ln_g,
               conv_ln_b=conv_ln_b, w_conv_out=w_conv_out, w_attn_out=w_attn_out, w_o=w_o, norm2_g=norm2_g,
               w_ffn_gate=w_ffn_gate, w_ffn_up=w_ffn_up, w_ffn_down=w_ffn_down, norm_f_g=norm_f_g)
    mom1 = dict(norm1_g=m_norm1_g, w_in=m_w_in, gate_b=m_gate_b, conv_w=m_conv_w, conv_b=m_conv_b, conv_ln_g=m_conv_ln_g,
                conv_ln_b=m_conv_ln_b, w_conv_out=m_w_conv_out, w_attn_out=m_w_attn_out, w_o=m_w_o, norm2_g=m_norm2_g,
                w_ffn_gate=m_w_ffn_gate, w_ffn_up=m_w_ffn_up, w_ffn_down=m_w_ffn_down, norm_f_g=m_norm_f_g)
    mom2 = dict(norm1_g=v_norm1_g, w_in=v_w_in, gate_b=v_gate_b, conv_w=v_conv_w, conv_b=v_conv_b, conv_ln_g=v_conv_ln_g,
                conv_ln_b=v_conv_ln_b, w_conv_out=v_w_conv_out, w_attn_out=v_w_attn_out, w_o=v_w_o, norm2_g=v_norm2_g,
                w_ffn_gate=v_w_ffn_gate, w_ffn_up=v_w_ffn_up, w_ffn_down=v_w_ffn_down, norm_f_g=v_norm_f_g)

    T = x.shape[0] * x.shape[1]
    x2 = x.reshape(T, D_MODEL)
    t2 = loss_target.reshape(T, D_MODEL)

    me = 4 * lax.axis_index("x") + 2 * lax.axis_index("y") + lax.axis_index("c")
    shards = {n: _shard2d(n, wts[n]) for n in BIG}
    sent = {n: shards[n] if n == "conv_w" else shards[n].astype(BF16) for n in BIG}
    small = {n: wts[n].reshape(1, -1) for n in SMALL}

    stage = {}

    def in_proj():
        w_in_blocks, conv_blocks = _all_gather([sent["w_in"], sent["conv_w"]])
        stage["merge"] = _send_start("gather", ALL_PEERS, "gather_start_merge", [sent[n] for n in LATE_MERGE],
                                     dep=w_in_blocks)
        stage["ffn"] = _send_start("gather", (1,) + OTHER_CHIPS, "gather_start_ffn", [sent[n] for n in LATE_FFN],
                                   dep=stage["merge"]["token"])
        w_in_t = _gathered_to_full("w_in", w_in_blocks)
        h, proj = _in_proj(x2, small["norm1_g"], w_in_t, stage["ffn"]["token"])
        return h, proj, {"w_in": w_in_t, "conv_w": _gathered_to_full("conv_w", conv_blocks)}

    def filled(names, srcs, lands):
        return {n: _gathered_to_full(n, lax.dynamic_update_slice(land, src[None], (me, 0, 0)))
                for n, src, land in zip(names, srcs, lands)}

    def pass_on(group, after):
        stage[group + "_srcs"], lands = _send_wait(stage[group], after, "gather_wait_" + group)
        stage[group + "_forward"] = _send_start("forward", OTHER_CHIPS, "forward_start_" + group, lands=lands)
        return stage[group + "_forward"]["token"]

    def arrived(group, names, after):
        _, lands = _send_wait(stage[group + "_forward"], after, "forward_wait_" + group)
        return filled(names, stage[group + "_srcs"], lands)

    def late_weights(which, after):
        if which == "after_attention":
            return {}
        if which is LATE_MERGE:
            srcs, lands = _send_wait(stage["merge"], after, "gather_wait_merge")
            return {**filled(LATE_MERGE, srcs, lands), "token": pass_on("ffn", after)}
        return arrived("ffn", LATE_FFN, after)

    scatters = []
    core = lax.axis_index("c").astype(jnp.int32).reshape(1)

    def emit(names, gw):
        blocks = [_full_to_blocks(n, gw[n]) for n in names]
        if "w_in" in names:
            sums = [_add_pair(g, r, core, "chip_sum_" + n) for n, g, r in zip(names, blocks, _exchange_sibling(blocks))]
            started = _send_start("chip_scatter", OTHER_CHIPS, "scatter_start_" + names[0], sums)
        else:
            started = _send_start("scatter", ALL_PEERS, "scatter_start_" + names[0], blocks)
        scatters.append((names, started))
        return started["token"]

    loss_part, grad_x, gw, gsmall = _local_step(x2, t2, in_proj, small, late_weights, emit)

    grads, deltas, new_m, new_v = {}, {}, {}, {}
    after = grad_x
    for names, started in scatters:
        srcs, lands = _send_wait(started, after, "scatter_wait_" + names[0])
        mine = (me >> 1 if started["mode"] == "chip_scatter" else me).astype(jnp.int32).reshape(1)
        for n, src, land in zip(names, srcs, lands):
            g, d, mo, vo = _sum_adam(land, src, mine, shards[n], _shard2d(n, mom1[n]), _shard2d(n, mom2[n]), "adam_" + n)
            for dst, val in ((grads, g), (deltas, d), (new_m, mo), (new_v, vo)):
                dst[n] = _from_shard2d(n, val, wts[n].shape)
            after = g

    zeros, ones = jnp.zeros((SUBLANES, 128), F32), jnp.ones((SUBLANES, 128), F32)
    row_counts = [wts[n].size // 128 for n in SMALL]
    kinds, loss_rows = _small_allreduce_adam(
        _pack_small(gsmall, jnp.broadcast_to(loss_part, (SUBLANES, 128))), _pack_small(wts, zeros),
        _pack_small(mom1, zeros), _pack_small(mom2, ones), row_counts, after)
    for dst, vals in zip((grads, deltas, new_m, new_v), kinds):
        dst.update({n: val.reshape(wts[n].shape) for n, val in zip(SMALL, vals)})
    loss = loss_rows[0, 0]
    return (loss, grad_x.reshape(x.shape), *[grads[n] for n in WEIGHTS], *[deltas[n] for n in WEIGHTS],
            *[new_m[n] for n in WEIGHTS], *[new_v[n] for n in WEIGHTS])
```
